```python
import math
import jax, jax.numpy as jnp
from jax import lax
import numpy as np

D_MODEL = 2048
BATCH = 2
SEQ = 4096
DEPTH = 1

N_META = 16
BLOCK = 128
PAD = (-N_META) % BLOCK
EPS = 1e-6
NEG_INF = -1e30

H_DA = 4
DK_DA = 128
DV_DA = 2 * DK_DA
H_M = 4
DK_M = 128
DV_M = 256
CONV_W = 5
N_BUCKETS = 32
MAX_DISTANCE = 128
N_EXPERTS = 32
TOP_K = 4
D_FF = 2048
SWIGLU_ALPHA = 1.702
SWIGLU_LIMIT = 7.0
MOE_BLOCK = 128

DA_Q = H_DA * 2 * DK_DA
DA_K = H_DA * 2 * DK_DA
DA_V = H_DA * DV_DA
M_Q = H_M * DK_M
M_K = H_M * DK_M
M_V = H_M * DV_M
M_O = H_M * DV_M
M_G = 4 * H_M
IN_SIZES = (DA_Q, DA_K, DA_V, M_Q, M_K, M_V, M_O, M_G)
IN_COLS = sum(IN_SIZES)
SPLIT_POINTS = tuple(int(s) for s in np.cumsum(IN_SIZES)[:-1])
W_DA = H_DA * DV_DA
W_M = H_M * DV_M

kernel_name = 'hybrid_diffattn_mlstm_moe_block'


def rmsnorm(x, gain):
    xf = x.astype(jnp.float32)
    y = xf * lax.rsqrt(jnp.mean(xf * xf, axis=-1, keepdims=True) + EPS)
    return (y * gain.astype(jnp.float32)).astype(x.dtype)


def front_pad(t, axis, value=0.0):
    widths = [(0, 0)] * t.ndim
    widths[axis] = (PAD, 0)
    return jnp.pad(t, widths, constant_values=value)


def t5_bucket(rel):
    nb = N_BUCKETS // 2
    max_exact = nb // 2
    base = jnp.where(rel > 0, nb, 0)
    n = jnp.abs(rel)
    nf = jnp.maximum(n, 1).astype(jnp.float32)
    large = max_exact + (jnp.log(nf / max_exact) / math.log(MAX_DISTANCE / max_exact)
                         * (nb - max_exact)).astype(jnp.int32)
    large = jnp.minimum(large, nb - 1)
    return base + jnp.where(n < max_exact, n, large)


def lambda_init(layer):
    return 0.8 - 0.6 * math.exp(-0.3 * layer)


def diff_attention(q, k, v, rel_bias, lam_params, subln_gain, lam_init):
    B, Lp = q.shape[:2]
    q = q.reshape(B, Lp, H_DA, 2, DK_DA).transpose(3, 0, 2, 1, 4)
    k = k.reshape(B, Lp, H_DA, 2, DK_DA).transpose(3, 0, 2, 1, 4)
    v = v.reshape(B, Lp, H_DA, DV_DA).transpose(0, 2, 1, 3)
    lp = lam_params.astype(jnp.float32)
    lam = jnp.exp(jnp.sum(lp[0] * lp[1])) - jnp.exp(jnp.sum(lp[2] * lp[3])) + lam_init
    pos = jnp.arange(Lp) - PAD
    key_valid = pos >= 0
    scale = DK_DA ** -0.5

    def block(j):
        q_blk = lax.dynamic_slice_in_dim(q, j * BLOCK, BLOCK, axis=3)
        q_pos = lax.dynamic_slice_in_dim(pos, j * BLOCK, BLOCK)
        bias = rel_bias[t5_bucket(pos[None, :] - q_pos[:, None])].astype(jnp.float32)
        bias = jnp.where(key_valid[None, :, None], bias, NEG_INF).transpose(2, 0, 1)
        s = jnp.einsum('mbhqd,mbhkd->mbhqk', q_blk, k,
                       preferred_element_type=jnp.float32) * scale + bias
        p = jax.nn.softmax(s, axis=-1)
        a = p[0] - lam * p[1]
        return jnp.einsum('bhqk,bhkd->bhqd', a.astype(v.dtype), v,
                          preferred_element_type=jnp.float32)

    o = lax.map(block, jnp.arange(Lp // BLOCK))
    o = o.transpose(1, 2, 0, 3, 4).reshape(B, H_DA, Lp, DV_DA)
    o = rmsnorm(o, subln_gain) * (1.0 - lam_init)
    o = o[:, :, PAD:].transpose(0, 2, 1, 3).reshape(B, Lp - PAD, W_DA)
    return o.astype(q.dtype)


def mlstm_chunkwise(q, k, v, log_i, log_f):
    B, H, Lp, DK = q.shape
    DV = v.shape[-1]
    nc = Lp // BLOCK
    chunks = lambda t: jnp.moveaxis(t.reshape(B, H, nc, BLOCK, *t.shape[3:]), 2, 0)
    qc, kc, vc = chunks(q * DK ** -0.5), chunks(k), chunks(v)
    lic, lfc = chunks(log_i), chunks(log_f)
    causal_in_chunk = jnp.tril(jnp.ones((BLOCK, BLOCK), dtype=bool))

    def step(carry, inp):
        C, n, m = carry
        qb, kb, vb, li, lf = inp
        b = jnp.cumsum(lf, axis=-1)
        dmat = jnp.where(causal_in_chunk, b[..., :, None] - b[..., None, :] + li[..., None, :], -jnp.inf)
        inter = b + m[..., None]
        m_t = jnp.maximum(inter, jnp.max(dmat, axis=-1))
        w_inter = jnp.exp(inter - m_t)
        s = jnp.einsum('bhtd,bhsd->bhts', qb, kb) * jnp.exp(dmat - m_t[..., None])
        num = w_inter[..., None] * jnp.einsum('bhtd,bhde->bhte', qb, C) + jnp.einsum('bhts,bhse->bhte', s, vb)
        den = w_inter * jnp.einsum('bhtd,bhd->bht', qb, n) + jnp.sum(s, axis=-1)
        h = num / jnp.maximum(jnp.abs(den), jnp.exp(-m_t))[..., None]
        b_end = b[..., -1]
        ldec = b_end[..., None] - b + li
        m_new = jnp.maximum(b_end + m, jnp.max(ldec, axis=-1))
        w_c = jnp.exp(b_end + m - m_new)
        w_s = jnp.exp(ldec - m_new[..., None])
        C = w_c[..., None, None] * C + jnp.einsum('bhs,bhsd,bhse->bhde', w_s, kb, vb)
        n = w_c[..., None] * n + jnp.einsum('bhs,bhsd->bhd', w_s, kb)
        return (C, n, m_new), h

    init = (jnp.zeros((B, H, DK, DV), jnp.float32), jnp.zeros((B, H, DK), jnp.float32),
            jnp.zeros((B, H), jnp.float32))
    _, hs = lax.scan(step, init, (qc, kc, vc, lic, lfc))
    return jnp.moveaxis(hs, 0, 2).reshape(B, H, Lp, DV)


def centred_conv(u, w):
    half = CONV_W // 2
    L = u.shape[1]
    up = jnp.pad(u, ((0, 0), (half, half), (0, 0)))
    return sum(w[j] * up[:, j:j + L] for j in range(CONV_W))


def mlstm_branch(q, k, v, o, gates, gate_bias):
    B, L = q.shape[:2]
    g = gates.reshape(B, L, 4, H_M).astype(jnp.float32) + gate_bias.astype(jnp.float32)
    g = g.transpose(2, 0, 3, 1)
    heads = lambda t, d: front_pad(t.reshape(B, L, H_M, d).transpose(0, 2, 1, 3).astype(jnp.float32), axis=2)
    qh, kh, vh = heads(q, DK_M), heads(k, DK_M), heads(v, DV_M)
    li_f = front_pad(g[0], axis=2, value=-jnp.inf)
    lf_f = front_pad(jax.nn.log_sigmoid(g[1]), axis=2, value=0.0)
    li_b = front_pad(g[2], axis=2, value=-jnp.inf)
    lf_b = front_pad(jax.nn.log_sigmoid(g[3]), axis=2, value=0.0)
    flip = lambda t: jnp.flip(t, axis=2)
    h_fwd = mlstm_chunkwise(qh, kh, vh, li_f, lf_f)
    h_bwd = flip(mlstm_chunkwise(flip(qh), flip(kh), flip(vh), flip(li_b), flip(lf_b)))
    h = (h_fwd + h_bwd)[:, :, PAD:].transpose(0, 2, 1, 3).reshape(B, L, W_M)
    return (jax.nn.sigmoid(o.astype(jnp.float32)) * h).astype(q.dtype)


def moe_ffn(t, w_router, b_router, w1, b1, w2, b2):
    N, D = t.shape
    A = N * TOP_K
    n_blocks = -(-A // MOE_BLOCK) + N_EXPERTS
    logits = jnp.dot(t, w_router, preferred_element_type=jnp.float32) + b_router.astype(jnp.float32)
    top_logit, top_e = lax.top_k(logits, TOP_K)
    weight = jax.nn.softmax(top_logit, axis=-1)
    flat_e = top_e.reshape(A)
    order = jnp.argsort(flat_e)
    sorted_e = flat_e[order]
    counts = jnp.bincount(flat_e, length=N_EXPERTS)
    padded = (counts + MOE_BLOCK - 1) // MOE_BLOCK * MOE_BLOCK
    start = jnp.cumsum(counts) - counts
    pstart = jnp.cumsum(padded) - padded
    dest_sorted = pstart[sorted_e] + jnp.arange(A) - start[sorted_e]
    dest = jnp.zeros((A,), jnp.int32).at[order].set(dest_sorted.astype(jnp.int32))
    slot_tok = jnp.full((n_blocks * MOE_BLOCK,), N, jnp.int32).at[dest].set(jnp.arange(A, dtype=jnp.int32) // TOP_K)
    block_e = jnp.minimum(jnp.searchsorted(jnp.cumsum(padded), jnp.arange(n_blocks) * MOE_BLOCK, side='right'),
                          N_EXPERTS - 1)
    t_pad = jnp.concatenate([t, jnp.zeros((1, D), t.dtype)], axis=0)
    xb = t_pad[slot_tok].reshape(n_blocks, MOE_BLOCK, D)

    def expert_block(args):
        xblk, e = args
        hu = xblk @ w1[e] + b1[e]
        glu, lin = jnp.split(hu, 2, axis=-1)
        glu = jnp.minimum(glu, SWIGLU_LIMIT)
        lin = jnp.clip(lin, -SWIGLU_LIMIT, SWIGLU_LIMIT)
        act = glu * jax.nn.sigmoid(SWIGLU_ALPHA * glu) * (lin + 1.0)
        return act @ w2[e] + b2[e]

    yb = lax.map(expert_block, (xb, block_e)).reshape(n_blocks * MOE_BLOCK, D)
    y = yb[dest].reshape(N, TOP_K, D)
    return jnp.einsum('nk,nkd->nd', weight.astype(y.dtype), y)


def setup_inputs(seed: int = 0) -> dict:
    key = jax.random.key(seed)
    ks = jax.random.split(key, 24)
    nrm = lambda k, shape, s: jax.random.normal(k, shape, jnp.float32) * s
    fb = jnp.array([0.0, 1.0, 0.0, 1.0], jnp.float32)[:, None] * jnp.linspace(3.0, 6.0, H_M)[None, :]
    return {
        'x': nrm(ks[0], (BATCH, SEQ, D_MODEL), 1.0),
        'meta_tokens': nrm(ks[1], (N_META, D_MODEL), 1.0),
        'rel_bias': nrm(ks[2], (N_BUCKETS, H_DA), 0.5),
        'norm_mix': 1.0 + nrm(ks[3], (DEPTH, D_MODEL), 0.05),
        'w_in': nrm(ks[4], (DEPTH, D_MODEL, IN_COLS), D_MODEL ** -0.5),
        'conv_w': nrm(ks[5], (DEPTH, CONV_W, M_Q + M_K), CONV_W ** -0.5),
        'gate_bias_m': fb[None] + nrm(ks[6], (DEPTH, 4, H_M), 0.1),
        'lambda_params': nrm(ks[7], (DEPTH, 4, DK_DA), 0.1),
        'subln_da': 1.0 + nrm(ks[8], (DEPTH, DV_DA), 0.05),
        'w_branch_da': nrm(ks[9], (DEPTH, W_DA, D_MODEL), W_DA ** -0.5),
        'w_branch_m': nrm(ks[10], (DEPTH, W_M, D_MODEL), W_M ** -0.5),
        'w_gate': nrm(ks[11], (DEPTH, D_MODEL, 2 * D_MODEL), D_MODEL ** -0.5),
        'b_gate': nrm(ks[12], (DEPTH, 2 * D_MODEL), 0.1),
        'w_out': nrm(ks[13], (DEPTH, D_MODEL, D_MODEL), D_MODEL ** -0.5),
        'norm_ffn': 1.0 + nrm(ks[14], (DEPTH, D_MODEL), 0.05),
        'w_router': nrm(ks[15], (DEPTH, D_MODEL, N_EXPERTS), D_MODEL ** -0.5),
        'b_router': nrm(ks[16], (DEPTH, N_EXPERTS), 0.01),
        'w1': nrm(ks[17], (DEPTH, N_EXPERTS, D_MODEL, 2 * D_FF), D_MODEL ** -0.5),
        'b1': nrm(ks[18], (DEPTH, N_EXPERTS, 2 * D_FF), 0.01),
        'w2': nrm(ks[19], (DEPTH, N_EXPERTS, D_FF, D_MODEL), D_FF ** -0.5),
        'b2': nrm(ks[20], (DEPTH, N_EXPERTS, D_MODEL), 0.01),
        'norm_final': 1.0 + nrm(ks[21], (D_MODEL,), 0.05),
    }


def reference(x, meta_tokens, rel_bias, norm_mix, w_in, conv_w, gate_bias_m, lambda_params, subln_da,
              w_branch_da, w_branch_m, w_gate, b_gate, w_out, norm_ffn, w_router, b_router,
              w1, b1, w2, b2, norm_final):
    B = x.shape[0]
    meta = jnp.broadcast_to(meta_tokens[None].astype(x.dtype), (B, N_META, D_MODEL))
    h = jnp.concatenate([meta, x], axis=1)
    L = h.shape[1]
    for layer in range(DEPTH):
        xn = rmsnorm(h, norm_mix[layer])
        proj = xn @ w_in[layer]
        da_q, da_k, da_v, m_q, m_k, m_v, m_o, m_g = jnp.split(proj, SPLIT_POINTS, axis=-1)
        y_da = diff_attention(front_pad(da_q, 1), front_pad(da_k, 1), front_pad(da_v, 1), rel_bias,
                              lambda_params[layer], subln_da[layer], lambda_init(layer))
        m_qk = jax.nn.silu(centred_conv(jnp.concatenate([m_q, m_k], axis=-1), conv_w[layer]))
        m_q, m_k = jnp.split(m_qk, 2, axis=-1)
        y_m = mlstm_branch(m_q, m_k, m_v, m_o, m_g, gate_bias_m[layer])
        g = jax.nn.sigmoid((xn @ w_gate[layer] + b_gate[layer]).astype(jnp.float32)).reshape(B, L, 2, D_MODEL)
        mixed = g[:, :, 0] * (y_da @ w_branch_da[layer]) + g[:, :, 1] * (y_m @ w_branch_m[layer])
        h = h + mixed.astype(h.dtype) @ w_out[layer]
        hn = rmsnorm(h, norm_ffn[layer]).reshape(B * L, D_MODEL)
        ff = moe_ffn(hn, w_router[layer], b_router[layer], w1[layer], b1[layer], w2[layer], b2[layer])
        h = h + ff.reshape(B, L, D_MODEL).astype(h.dtype)
    h = rmsnorm(h, norm_final)
    return h[:, N_META:]
```

```python
import functools
import math

import jax
import jax.numpy as jnp
from jax import lax
import numpy as np
from jax.experimental import pallas as pl
from jax.experimental.pallas import tpu as pltpu

F32 = jnp.float32
BF16 = jnp.bfloat16

D_MODEL = 2048
BATCH = 2
SEQ = 4096
N_META = 16
BLOCK = 128
PAD = (-N_META) % BLOCK
LP = PAD + N_META + SEQ
NBLK = LP // BLOCK
EPS = 1e-6
NEG_INF = -1e30

H_DA = 4
DK_DA = 128
DV_DA = 256
H_M = 4
DK_M = 128
DV_M = 256
CONV_W = 5
N_BUCKETS = 32
MAX_DISTANCE = 128
N_EXPERTS = 32
TOP_K = 4
D_FF = 2048
SWIGLU_ALPHA = 1.702
SWIGLU_LIMIT = 7.0
LAMBDA_INIT = 0.8 - 0.6 * math.exp(-0.3 * 0)

COL_DA_Q = 0
COL_DA_K = 1024
COL_DA_V = 2048
COL_M_Q = 3072
COL_M_K = 3584
COL_M_V = 4096
COL_M_O = 5120
COL_M_G = 6144
PROJ_COLS = 6144

N_TOK = BATCH * SEQ
N_ASSIGN = N_TOK * TOP_K
MOE_ROWS = 256
MOE_NBLK = N_ASSIGN // MOE_ROWS + N_EXPERTS
MOE_TOTAL = MOE_NBLK * MOE_ROWS

VMEM_LIMIT = 52 * 1024 * 1024


def _cparams(n_axes):
    return pltpu.CompilerParams(
        dimension_semantics=("arbitrary",) * n_axes, vmem_limit_bytes=VMEM_LIMIT)


def _rms(v, gain):
    ms = jnp.mean(v * v, axis=-1, keepdims=True)
    return v * lax.rsqrt(ms + EPS) * gain


def _norm_in_kernel(x_ref, meta_ref, g_ref, pad_ref, real_ref):
    j = pl.program_id(1)
    g = g_ref[...]

    @pl.when(j == 0)
    def _():
        pad_ref[0, :PAD, :] = jnp.zeros((PAD, D_MODEL), BF16)
        pad_ref[0, PAD:, :] = _rms(meta_ref[...], g).astype(BF16)

    @pl.when(j > 0)
    def _():
        y = _rms(x_ref[0], g).astype(BF16)
        pad_ref[0] = y
        real_ref[0] = y


def norm_in(x, meta, gain):
    return pl.pallas_call(
        _norm_in_kernel,
        grid=(BATCH, NBLK),
        in_specs=[
            pl.BlockSpec((1, BLOCK, D_MODEL), lambda b, j: (b, jnp.maximum(j - 1, 0), 0)),
            pl.BlockSpec((N_META, D_MODEL), lambda b, j: (0, 0)),
            pl.BlockSpec((1, D_MODEL), lambda b, j: (0, 0)),
        ],
        out_specs=[
            pl.BlockSpec((1, BLOCK, D_MODEL), lambda b, j: (b, j, 0)),
            pl.BlockSpec((1, BLOCK, D_MODEL), lambda b, j: (b, jnp.maximum(j - 1, 0), 0)),
        ],
        out_shape=[
            jax.ShapeDtypeStruct((BATCH, LP, D_MODEL), BF16),
            jax.ShapeDtypeStruct((BATCH, SEQ, D_MODEL), BF16),
        ],
        compiler_params=_cparams(2),
        name="norm_in",
    )(x, meta, gain.reshape(1, D_MODEL))


def _mm_kernel(*refs, has_bias, has_res, act):
    x_ref, w_ref = refs[0], refs[1]
    pos = 2
    b_ref = r_ref = None
    if has_bias:
        b_ref = refs[pos]
        pos += 1
    if has_res:
        r_ref = refs[pos]
        pos += 1
    o_ref, wbf_ref = refs[pos], refs[pos + 1]

    @pl.when(pl.program_id(1) == 0)
    def _():
        wbf_ref[...] = w_ref[...].astype(BF16)

    acc = jnp.dot(x_ref[...], wbf_ref[...], preferred_element_type=F32)
    if has_bias:
        acc = acc + b_ref[...]
    if act == "sigmoid":
        acc = jax.nn.sigmoid(acc)
    if has_res:
        acc = acc + r_ref[...]
    o_ref[...] = acc.astype(o_ref.dtype)


def matmul(x, w, *, n_cols, col_block0=0, tm, tn, bias=None, res=None, act=None,
           out_dtype=BF16, name):
    m, k = x.shape
    in_specs = [
        pl.BlockSpec((tm, k), lambda j, i: (i, 0)),
        pl.BlockSpec((k, tn), lambda j, i: (0, j + col_block0)),
    ]
    args = [x, w]
    if bias is not None:
        in_specs.append(pl.BlockSpec((1, tn), lambda j, i: (0, j)))
        args.append(bias.reshape(1, n_cols))
    if res is not None:
        in_specs.append(pl.BlockSpec((tm, tn), lambda j, i: (i, j)))
        args.append(res)
    return pl.pallas_call(
        functools.partial(_mm_kernel, has_bias=bias is not None, has_res=res is not None, act=act),
        grid=(n_cols // tn, m // tm),
        in_specs=in_specs,
        out_specs=pl.BlockSpec((tm, tn), lambda j, i: (i, j)),
        out_shape=jax.ShapeDtypeStruct((m, n_cols), out_dtype),
        scratch_shapes=[pltpu.VMEM((k, tn), BF16)],
        compiler_params=_cparams(2),
        name=name,
    )(*args)


def _conv_kernel(p_ref, w_ref, o_ref):
    c = pl.program_id(1)
    x = p_ref[0].astype(F32)
    w = w_ref[...]
    half = CONV_W // 2
    acc = w[half:half + 1, :] * x
    for j in range(CONV_W):
        if j != half:
            acc = acc + w[j:j + 1, :] * pltpu.roll(x, (half - j) % LP, axis=0)
    y = acc * jax.nn.sigmoid(acc)
    rows = lax.broadcasted_iota(jnp.int32, (LP, 1), 0)
    y = jnp.where(rows >= PAD, y, 0.0)
    scale = jnp.where(c < 2, DK_M ** -0.5, 1.0).astype(F32)
    o_ref[0] = (y * scale).astype(BF16)


def conv_qk(proj3, conv_w):
    cw = 256
    return pl.pallas_call(
        _conv_kernel,
        grid=(BATCH, (2 * H_M * DK_M) // cw),
        in_specs=[
            pl.BlockSpec((1, LP, cw), lambda b, c: (b, 0, COL_M_Q // cw + c)),
            pl.BlockSpec((CONV_W, cw), lambda b, c: (0, c)),
        ],
        out_specs=pl.BlockSpec((1, LP, cw), lambda b, c: (b, 0, c)),
        out_shape=jax.ShapeDtypeStruct((BATCH, LP, 2 * H_M * DK_M), BF16),
        compiler_params=_cparams(2),
        name="conv_qk",
    )(proj3, conv_w)


def _split_dot(tri, v):
    hi = v.astype(BF16)
    r1 = v - hi.astype(F32)
    mid = r1.astype(BF16)
    lo = (r1 - mid.astype(F32)).astype(BF16)
    return (jnp.dot(tri, hi, preferred_element_type=F32)
            + jnp.dot(tri, mid, preferred_element_type=F32)
            + jnp.dot(tri, lo, preferred_element_type=F32))


def _gate_kernel(g_ref, bias_ref, o_ref):
    ti = lax.broadcasted_iota(jnp.int32, (BLOCK, BLOCK), 0)
    ui = lax.broadcasted_iota(jnp.int32, (BLOCK, BLOCK), 1)
    tril = jnp.where(ui <= ti, 1.0, 0.0).astype(BF16)
    triu = jnp.where(ui >= ti, 1.0, 0.0).astype(BF16)
    ch = lax.broadcasted_iota(jnp.int32, (BLOCK, 4 * H_M), 1)
    typ = lax.shift_right_logical(ch, 2)
    rloc = lax.broadcasted_iota(jnp.int32, (BLOCK, 4 * H_M), 0)

    def body(c, carry):
        r0 = pl.multiple_of(c * BLOCK, BLOCK)
        g = g_ref[0, pl.ds(r0, BLOCK), :][:, :4 * H_M] + bias_ref[...]
        valid = (rloc + r0) >= PAD
        lsig = -(jnp.maximum(-g, 0.0) + jnp.log1p(jnp.exp(-jnp.abs(g))))
        lf = jnp.where(valid, lsig, 0.0)
        cum = _split_dot(tril, lf)
        rcum = _split_dot(triu, lf)
        li = jnp.where(valid, g, -jnp.inf)
        out = jnp.where(typ == 1, cum, jnp.where(typ == 3, rcum, li))
        o_ref[0, pl.ds(r0, BLOCK), :] = out
        return carry

    lax.fori_loop(0, NBLK, body, 0)


def gate_prep(mg3, gate_bias):
    return pl.pallas_call(
        _gate_kernel,
        grid=(BATCH,),
        in_specs=[
            pl.BlockSpec((1, LP, 128), lambda b: (b, 0, 0)),
            pl.BlockSpec((1, 4 * H_M), lambda b: (0, 0)),
        ],
        out_specs=pl.BlockSpec((1, LP, 4 * H_M), lambda b: (b, 0, 0)),
        out_shape=jax.ShapeDtypeStruct((BATCH, LP, 4 * H_M), F32),
        compiler_params=_cparams(1),
        name="gate_prep",
    )(mg3, gate_bias.reshape(1, 4 * H_M))


MLSTM_HP = 1


def _mlstm_kernel(q_ref, k_ref, v_ref, o_ref, gc_ref, gr_ref, y_ref,
                  hf_ref, hb_ref, c_ref, n_ref, m_ref):
    c_ref[...] = jnp.zeros_like(c_ref)
    n_ref[...] = jnp.zeros_like(n_ref)
    m_ref[...] = jnp.zeros_like(m_ref)
    ti = lax.broadcasted_iota(jnp.int32, (BLOCK, BLOCK), 0)
    si = lax.broadcasted_iota(jnp.int32, (BLOCK, BLOCK), 1)
    mask_f = si <= ti
    mask_b = si >= ti

    def chain(c, hl, bwd):
        idx = 2 * hl + bwd
        r0 = pl.multiple_of(c * BLOCK, BLOCK)
        q = q_ref[0, pl.ds(r0, BLOCK), hl * DK_M:(hl + 1) * DK_M]
        k = k_ref[0, pl.ds(r0, BLOCK), hl * DK_M:(hl + 1) * DK_M]
        v = v_ref[0, pl.ds(r0, BLOCK), hl * DV_M:(hl + 1) * DV_M]
        gc = gc_ref[0, hl, pl.ds(r0, BLOCK), :]
        gr = gr_ref[0, hl, :, pl.ds(r0, BLOCK)]
        a = 2 * bwd
        li_c, b_c = gc[:, a:a + 1], gc[:, a + 1:a + 2]
        li_r, b_r = gr[a:a + 1, :], gr[a + 1:a + 2, :]
        b_end = b_c[0:1, :] if bwd else b_c[BLOCK - 1:BLOCK, :]
        m_prev = m_ref[idx][:, 0:1]
        cst = c_ref[idx]
        nst = n_ref[idx]
        dmat = jnp.where(mask_b if bwd else mask_f, b_c - b_r + li_r, -jnp.inf)
        inter = b_c + m_prev
        m_t = jnp.maximum(inter, jnp.max(dmat, axis=1, keepdims=True))
        w_inter = jnp.exp(inter - m_t)
        qk = lax.dot_general(q, k, (((1,), (1,)), ((), ())), preferred_element_type=F32)
        s = qk * jnp.exp(dmat - m_t)
        num = (w_inter * jnp.dot(q, cst.astype(BF16), preferred_element_type=F32)
               + jnp.dot(s.astype(BF16), v, preferred_element_type=F32))
        den = jnp.sum(s + w_inter * (q.astype(F32) * nst), axis=1, keepdims=True)
        h = num / jnp.maximum(jnp.abs(den), jnp.exp(-m_t))
        dst = hb_ref if bwd else hf_ref
        dst[pl.ds(r0, BLOCK), hl * DV_M:(hl + 1) * DV_M] = h
        ldec_c = b_end - b_c + li_c
        ldec_r = b_end - b_r + li_r
        m_new = jnp.maximum(b_end + m_prev, jnp.max(ldec_r, axis=1, keepdims=True))
        w_c = jnp.exp(b_end + m_prev - m_new)
        w_s = jnp.exp(ldec_c - m_new)
        wv = (w_s * v.astype(F32)).astype(BF16)
        c_ref[idx] = w_c * cst + lax.dot_general(
            k, wv, (((0,), (0,)), ((), ())), preferred_element_type=F32)
        n_ref[idx] = w_c * nst + jnp.sum(w_s * k.astype(F32), axis=0, keepdims=True)
        m_ref[idx] = jnp.broadcast_to(m_new, (1, BLOCK))

    def step(c, carry):
        for hl in range(MLSTM_HP):
            chain(c, hl, 0)
            chain(NBLK - 1 - c, hl, 1)
        return carry

    lax.fori_loop(0, NBLK, step, 0)

    def epilogue(c, carry):
        r0 = pl.multiple_of(c * BLOCK, BLOCK)
        hsum = hf_ref[pl.ds(r0, BLOCK), :] + hb_ref[pl.ds(r0, BLOCK), :]
        og = jax.nn.sigmoid(o_ref[0, pl.ds(r0, BLOCK), :].astype(F32))
        y_ref[0, pl.ds(r0 - BLOCK, BLOCK), :] = (og * hsum).astype(BF16)
        return carry

    lax.fori_loop(1, NBLK, epilogue, 0)


def mlstm(qk_m, proj3, gcol, grow):
    hp = MLSTM_HP
    kw, vw = hp * DK_M, hp * DV_M
    return pl.pallas_call(
        _mlstm_kernel,
        grid=(BATCH, H_M // hp),
        in_specs=[
            pl.BlockSpec((1, LP, kw), lambda b, g: (b, 0, g)),
            pl.BlockSpec((1, LP, kw), lambda b, g: (b, 0, (H_M * DK_M) // kw + g)),
            pl.BlockSpec((1, LP, vw), lambda b, g: (b, 0, COL_M_V // vw + g)),
            pl.BlockSpec((1, LP, vw), lambda b, g: (b, 0, COL_M_O // vw + g)),
            pl.BlockSpec((1, hp, LP, 4), lambda b, g: (b, g, 0, 0)),
            pl.BlockSpec((1, hp, 4, LP), lambda b, g: (b, g, 0, 0)),
        ],
        out_specs=pl.BlockSpec((1, SEQ, vw), lambda b, g: (b, 0, g)),
        out_shape=jax.ShapeDtypeStruct((BATCH, SEQ, H_M * DV_M), BF16),
        scratch_shapes=[
            pltpu.VMEM((LP, vw), F32),
            pltpu.VMEM((LP, vw), F32),
            pltpu.VMEM((2 * hp, DK_M, DV_M), F32),
            pltpu.VMEM((2 * hp, 1, DK_M), F32),
            pltpu.VMEM((2 * hp, 1, BLOCK), F32),
        ],
        compiler_params=_cparams(2),
        name="mlstm",
    )(qk_m, qk_m, proj3, proj3, gcol, grow)


ATT_BAND = 3 * BLOCK
ATT_GROUP = 6 * BLOCK
ATT_NGROUP = (LP - ATT_BAND) // ATT_GROUP


def _attn_kernel(c_ref, q_ref, k_ref, v_ref, tab_ref, lam_ref, sg_ref, o_ref, s_ref):
    h = pl.program_id(1)
    qb = pl.program_id(2) + 1
    q = q_ref[0]
    scale = DK_DA ** -0.5
    c_neg = c_ref[h, 0]
    c_pos = c_ref[h, 1]
    lp = lam_ref[...]
    lam = (jnp.exp(jnp.sum(lp[0:1] * lp[1:2], axis=1, keepdims=True))
           - jnp.exp(jnp.sum(lp[2:3] * lp[3:4], axis=1, keepdims=True)) + LAMBDA_INIT)

    groups = [((qb - 1) * BLOCK, ATT_BAND, 0)]
    for g in range(ATT_NGROUP):
        groups.append(((qb + 2) * BLOCK + g * ATT_GROUP, ATT_GROUP, ATT_BAND + g * ATT_GROUP))

    def lane_fold(acc, t, op):
        for j in range(t.shape[1] // BLOCK):
            piece = t[:, j * BLOCK:(j + 1) * BLOCK]
            acc = piece if acc is None else op(acc, piece)
        return acc

    mx = [None, None]
    for gi, (koff, width, col0) in enumerate(groups):
        koff = pl.multiple_of(koff, BLOCK)
        if gi == 0:
            bias = tab_ref[0, 0]
        else:
            kpos = koff + lax.broadcasted_iota(jnp.int32, (1, width), 1)
            bias = jnp.where(kpos < LP, c_pos, jnp.where(kpos < LP + PAD, NEG_INF, c_neg))
        for m in range(2):
            kk = k_ref[0, pl.ds(koff, width), m * DK_DA:(m + 1) * DK_DA]
            s = lax.dot_general(q[:, m * DK_DA:(m + 1) * DK_DA], kk, (((1,), (1,)), ((), ())),
                                preferred_element_type=F32) * scale + bias
            s_ref[m, :, col0:col0 + width] = s
            mx[m] = lane_fold(mx[m], s, jnp.maximum)
    row_max = [jnp.max(mx[m], axis=1, keepdims=True) for m in range(2)]

    lsum = [None, None]
    acc = [None, None]
    for koff, width, col0 in groups:
        koff = pl.multiple_of(koff, BLOCK)
        vv = v_ref[0, pl.ds(koff, width), :]
        for m in range(2):
            p = jnp.exp(s_ref[m, :, col0:col0 + width] - row_max[m])
            lsum[m] = lane_fold(lsum[m], p, jnp.add)
            pv = jnp.dot(p.astype(BF16), vv, preferred_element_type=F32)
            acc[m] = pv if acc[m] is None else acc[m] + pv
    l1 = jnp.sum(lsum[0], axis=1, keepdims=True)
    l2 = jnp.sum(lsum[1], axis=1, keepdims=True)
    o = acc[0] / l1 - lam * (acc[1] / l2)
    o_ref[0] = (_rms(o, sg_ref[...]) * (1.0 - LAMBDA_INIT)).astype(BF16)


def diff_attention(consts, proj3, k2, v2, tab, lam_params, subln):
    nq = SEQ // BLOCK

    def tab_map(b, h, i):
        case = jnp.where(i == 0, 0, jnp.where(i == nq - 1, 2, 1))
        return (h, case, 0, 0)

    return pl.pallas_call(
        _attn_kernel,
        grid=(BATCH, H_DA, nq),
        in_specs=[
            pl.BlockSpec(memory_space=pltpu.SMEM),
            pl.BlockSpec((1, BLOCK, 2 * DK_DA), lambda b, h, i: (b, i + 1, h)),
            pl.BlockSpec((1, 2 * LP, 2 * DK_DA), lambda b, h, i: (b, 0, h)),
            pl.BlockSpec((1, 2 * LP, DV_DA), lambda b, h, i: (b, 0, h)),
            pl.BlockSpec((1, 1, BLOCK, ATT_BAND), tab_map),
            pl.BlockSpec((4, DK_DA), lambda b, h, i: (0, 0)),
            pl.BlockSpec((1, DV_DA), lambda b, h, i: (0, 0)),
        ],
        out_specs=pl.BlockSpec((1, BLOCK, DV_DA), lambda b, h, i: (b, i, h)),
        out_shape=jax.ShapeDtypeStruct((BATCH, SEQ, H_DA * DV_DA), BF16),
        scratch_shapes=[pltpu.VMEM((2, BLOCK, LP), F32)],
        compiler_params=_cparams(3),
        name="diff_attn",
    )(consts, proj3, k2, v2, tab, lam_params, subln.reshape(1, DV_DA))


def _bias_tables(rel_bias):
    rb = rel_bias.astype(F32)
    i = jnp.arange(BLOCK, dtype=jnp.int32)[:, None]
    j = jnp.arange(ATT_BAND, dtype=jnp.int32)[None, :]
    rel = j - BLOCK - i
    nb = N_BUCKETS // 2
    max_exact = nb // 2
    n = jnp.abs(rel)
    nf = jnp.maximum(n, 1).astype(F32)
    large = max_exact + (jnp.log(nf / max_exact) / math.log(MAX_DISTANCE / max_exact)
                         * (nb - max_exact)).astype(jnp.int32)
    large = jnp.minimum(large, nb - 1)
    bucket = jnp.where(rel > 0, nb, 0) + jnp.where(n < max_exact, n, large)
    gen = jnp.transpose(rb[bucket], (2, 0, 1))
    c_neg = rb[nb - 1]
    c_pos = rb[N_BUCKETS - 1]
    jj = j[None]
    first = jnp.where(jj < PAD, NEG_INF, gen)
    wrapped = jnp.where(jj - 2 * BLOCK < PAD, NEG_INF, c_neg[:, None, None])
    last = jnp.where(jj >= 2 * BLOCK, wrapped, gen)
    tab = jnp.stack([first, gen, last], axis=1)
    consts = jnp.stack([c_neg, c_pos], axis=1)
    return tab, consts


def _mix_kernel(ya_ref, ym_ref, wa_ref, wm_ref, ga_ref, gm_ref, o_ref, wa_bf, wm_bf):
    @pl.when(pl.program_id(1) == 0)
    def _():
        wa_bf[...] = wa_ref[...].astype(BF16)
        wm_bf[...] = wm_ref[...].astype(BF16)

    a = jnp.dot(ya_ref[...], wa_bf[...], preferred_element_type=F32)
    m = jnp.dot(ym_ref[...], wm_bf[...], preferred_element_type=F32)
    o_ref[...] = (ga_ref[...].astype(F32) * a + gm_ref[...].astype(F32) * m).astype(BF16)


def branch_mix(y_da, y_m, w_da, w_m, gate, *, tm=512, tn=1024):
    m, k = y_da.shape
    nj = D_MODEL // tn
    return pl.pallas_call(
        _mix_kernel,
        grid=(nj, m // tm),
        in_specs=[
            pl.BlockSpec((tm, k), lambda j, i: (i, 0)),
            pl.BlockSpec((tm, k), lambda j, i: (i, 0)),
            pl.BlockSpec((k, tn), lambda j, i: (0, j)),
            pl.BlockSpec((k, tn), lambda j, i: (0, j)),
            pl.BlockSpec((tm, tn), lambda j, i: (i, j)),
            pl.BlockSpec((tm, tn), lambda j, i: (i, nj + j)),
        ],
        out_specs=pl.BlockSpec((tm, tn), lambda j, i: (i, j)),
        out_shape=jax.ShapeDtypeStruct((m, D_MODEL), BF16),
        scratch_shapes=[pltpu.VMEM((k, tn), BF16), pltpu.VMEM((k, tn), BF16)],
        compiler_params=_cparams(2),
        name="branch_mix",
    )(y_da, y_m, w_da, w_m, gate, gate)


FFN_TM = 1024


def _ffn_prep_kernel(h_ref, g_ref, wr_ref, br_ref, hn_ref, e_ref, w_ref):
    i = pl.program_id(0)

    @pl.when(i < N_TOK // FFN_TM)
    def _():
        hn = _rms(h_ref[...], g_ref[...])
        hn_ref[...] = hn.astype(hn_ref.dtype)
        logits = jnp.dot(hn, wr_ref[...], preferred_element_type=F32,
                         precision=lax.Precision.HIGHEST) + br_ref[...]
        lane = lax.broadcasted_iota(jnp.int32, (FFN_TM, N_EXPERTS), 1)
        lane_o = lax.broadcasted_iota(jnp.int32, (FFN_TM, 128), 1)
        e_out = jnp.zeros((FFN_TM, 128), jnp.int32)
        l_out = jnp.full((FFN_TM, 128), -jnp.inf, F32)
        l = logits
        for kk in range(TOP_K):
            mk = jnp.max(l, axis=1, keepdims=True)
            ik = jnp.min(jnp.where(l == mk, lane, N_EXPERTS), axis=1, keepdims=True)
            e_out = jnp.where(lane_o == kk, ik, e_out)
            l_out = jnp.where(lane_o == kk, mk, l_out)
            l = jnp.where(lane == ik, -jnp.inf, l)
        ex = jnp.exp(l_out - jnp.max(l_out, axis=1, keepdims=True))
        e_ref[...] = e_out
        w_ref[...] = ex / jnp.sum(ex, axis=1, keepdims=True)

    @pl.when(i >= N_TOK // FFN_TM)
    def _():
        hn_ref[...] = jnp.zeros_like(hn_ref)


def ffn_prep(h2, gain, w_router, b_router):
    nt = N_TOK // FFN_TM
    clamp = lambda i: (jnp.minimum(i, nt - 1), 0)
    return pl.pallas_call(
        _ffn_prep_kernel,
        grid=(nt + 1,),
        in_specs=[
            pl.BlockSpec((FFN_TM, D_MODEL), clamp),
            pl.BlockSpec((1, D_MODEL), lambda i: (0, 0)),
            pl.BlockSpec((D_MODEL, N_EXPERTS), lambda i: (0, 0)),
            pl.BlockSpec((1, N_EXPERTS), lambda i: (0, 0)),
        ],
        out_specs=[
            pl.BlockSpec((FFN_TM, D_MODEL), lambda i: (i, 0)),
            pl.BlockSpec((FFN_TM, 128), clamp),
            pl.BlockSpec((FFN_TM, 128), clamp),
        ],
        out_shape=[
            jax.ShapeDtypeStruct((N_TOK + FFN_TM, D_MODEL), BF16),
            jax.ShapeDtypeStruct((N_TOK, 128), jnp.int32),
            jax.ShapeDtypeStruct((N_TOK, 128), F32),
        ],
        compiler_params=_cparams(1),
        name="ffn_prep",
    )(h2, gain.reshape(1, D_MODEL), w_router, b_router.reshape(1, N_EXPERTS))


MOE_TF = 512
MOE_TD = 1024


def _is_new_expert(be_ref, m):
    return jnp.logical_or(m == 0, be_ref[m] != be_ref[jnp.maximum(m - 1, 0)])


def _moe_up_kernel(be_ref, nu_ref, x_ref, wg_ref, wl_ref, bg_ref, bl_ref, o_ref, wg_bf, wl_bf):
    m = pl.program_id(1)

    @pl.when(m < nu_ref[0])
    def _():
        @pl.when(_is_new_expert(be_ref, m))
        def _():
            wg_bf[...] = wg_ref[0].astype(BF16)
            wl_bf[...] = wl_ref[0].astype(BF16)

        x = x_ref[...]
        glu = jnp.dot(x, wg_bf[...], preferred_element_type=F32) + bg_ref[0]
        lin = jnp.dot(x, wl_bf[...], preferred_element_type=F32) + bl_ref[0]
        glu = jnp.minimum(glu, SWIGLU_LIMIT)
        lin = jnp.clip(lin, -SWIGLU_LIMIT, SWIGLU_LIMIT)
        o_ref[...] = (glu * jax.nn.sigmoid(SWIGLU_ALPHA * glu) * (lin + 1.0)).astype(BF16)


def moe_up(block_e, n_used, xs, w1, b1):
    nf = D_FF // MOE_TF
    row = lambda f, m, be, nu: jnp.minimum(m, nu[0] - 1)
    grid_spec = pltpu.PrefetchScalarGridSpec(
        num_scalar_prefetch=2,
        grid=(nf, MOE_NBLK),
        in_specs=[
            pl.BlockSpec((MOE_ROWS, D_MODEL), lambda f, m, be, nu: (row(f, m, be, nu), 0)),
            pl.BlockSpec((1, D_MODEL, MOE_TF), lambda f, m, be, nu: (be[m], 0, f)),
            pl.BlockSpec((1, D_MODEL, MOE_TF), lambda f, m, be, nu: (be[m], 0, nf + f)),
            pl.BlockSpec((1, 1, MOE_TF), lambda f, m, be, nu: (be[m], 0, f)),
            pl.BlockSpec((1, 1, MOE_TF), lambda f, m, be, nu: (be[m], 0, nf + f)),
        ],
        out_specs=pl.BlockSpec((MOE_ROWS, MOE_TF), lambda f, m, be, nu: (row(f, m, be, nu), f)),
        scratch_shapes=[pltpu.VMEM((D_MODEL, MOE_TF), BF16), pltpu.VMEM((D_MODEL, MOE_TF), BF16)],
    )
    b13 = b1.reshape(N_EXPERTS, 1, 2 * D_FF)
    return pl.pallas_call(
        _moe_up_kernel,
        grid_spec=grid_spec,
        out_shape=jax.ShapeDtypeStruct((MOE_TOTAL, D_FF), BF16),
        compiler_params=_cparams(2),
        name="moe_up",
    )(block_e, n_used, xs, w1, w1, b13, b13)


def _moe_down_kernel(be_ref, nu_ref, a_ref, w_ref, b_ref, o_ref, w_bf):
    m = pl.program_id(1)

    @pl.when(m < nu_ref[0])
    def _():
        @pl.when(_is_new_expert(be_ref, m))
        def _():
            w_bf[...] = w_ref[0].astype(BF16)

        o_ref[...] = jnp.dot(a_ref[...], w_bf[...], preferred_element_type=F32) + b_ref[0]


def moe_down(block_e, n_used, act, w2, b2):
    nd = D_MODEL // MOE_TD
    row = lambda d, m, be, nu: jnp.minimum(m, nu[0] - 1)
    grid_spec = pltpu.PrefetchScalarGridSpec(
        num_scalar_prefetch=2,
        grid=(nd, MOE_NBLK),
        in_specs=[
            pl.BlockSpec((MOE_ROWS, D_FF), lambda d, m, be, nu: (row(d, m, be, nu), 0)),
            pl.BlockSpec((1, D_FF, MOE_TD), lambda d, m, be, nu: (be[m], 0, d)),
            pl.BlockSpec((1, 1, MOE_TD), lambda d, m, be, nu: (be[m], 0, d)),
        ],
        out_specs=pl.BlockSpec((MOE_ROWS, MOE_TD), lambda d, m, be, nu: (row(d, m, be, nu), d)),
        scratch_shapes=[pltpu.VMEM((D_FF, MOE_TD), BF16)],
    )
    return pl.pallas_call(
        _moe_down_kernel,
        grid_spec=grid_spec,
        out_shape=jax.ShapeDtypeStruct((MOE_TOTAL, D_MODEL), F32),
        compiler_params=_cparams(2),
        name="moe_down",
    )(block_e, n_used, act, w2, b2.reshape(N_EXPERTS, 1, D_MODEL))


COMB_TM = 256


def _combine_kernel(yg_ref, w_ref, h_ref, g_ref, o_ref):
    w = w_ref[...]
    ff = w[:, 0:1] * yg_ref[:, 0:D_MODEL]
    for kk in range(1, TOP_K):
        ff = ff + w[:, kk:kk + 1] * yg_ref[:, kk * D_MODEL:(kk + 1) * D_MODEL]
    o_ref[...] = _rms(h_ref[...] + ff, g_ref[...])


def combine(yg, weight, h2, gain):
    return pl.pallas_call(
        _combine_kernel,
        grid=(N_TOK // COMB_TM,),
        in_specs=[
            pl.BlockSpec((COMB_TM, TOP_K * D_MODEL), lambda i: (i, 0)),
            pl.BlockSpec((COMB_TM, 128), lambda i: (i, 0)),
            pl.BlockSpec((COMB_TM, D_MODEL), lambda i: (i, 0)),
            pl.BlockSpec((1, D_MODEL), lambda i: (0, 0)),
        ],
        out_specs=pl.BlockSpec((COMB_TM, D_MODEL), lambda i: (i, 0)),
        out_shape=jax.ShapeDtypeStruct((N_TOK, D_MODEL), F32),
        compiler_params=_cparams(1),
        name="combine",
    )(yg, weight, h2, gain.reshape(1, D_MODEL))


def _routing(top_e):
    flat_e = top_e.reshape(N_ASSIGN)
    onehot = (flat_e[:, None] == jnp.arange(N_EXPERTS, dtype=jnp.int32)[None, :]).astype(jnp.int32)
    csum = jnp.cumsum(onehot, axis=0)
    rank = jnp.sum(onehot * csum, axis=1) - 1
    counts = csum[-1]
    padded = (counts + MOE_ROWS - 1) // MOE_ROWS * MOE_ROWS
    pend = jnp.cumsum(padded)
    pstart = pend - padded
    dest = (pstart[flat_e] + rank).astype(jnp.int32)
    n_used = (pend[-1] // MOE_ROWS).astype(jnp.int32)
    blk = jnp.minimum(jnp.arange(MOE_NBLK, dtype=jnp.int32), n_used - 1)
    block_e = jnp.minimum(jnp.searchsorted(pend, blk * MOE_ROWS, side="right"),
                          N_EXPERTS - 1).astype(jnp.int32)
    slot_tok = jnp.full((MOE_TOTAL,), N_TOK, jnp.int32).at[dest].set(
        jnp.arange(N_ASSIGN, dtype=jnp.int32) // TOP_K)
    return dest, slot_tok, block_e, n_used.reshape(1)


def kernel(x, meta_tokens, rel_bias, norm_mix, w_in, conv_w, gate_bias_m, lambda_params, subln_da,
           w_branch_da, w_branch_m, w_gate, b_gate, w_out, norm_ffn, w_router, b_router,
           w1, b1, w2, b2, norm_final):
    layer = 0
    xn_pad, xn_real = norm_in(x, meta_tokens, norm_mix[layer])
    xn_pad2 = xn_pad.reshape(BATCH * LP, D_MODEL)
    proj = matmul(xn_pad2, w_in[layer], n_cols=PROJ_COLS, tm=768, tn=1024, name="proj_in")
    proj3 = proj.reshape(BATCH, LP, PROJ_COLS)
    w_g = jnp.pad(w_in[layer][:, COL_M_G:], ((0, 0), (0, 128 - 4 * H_M)))
    mg = matmul(xn_pad2, w_g, n_cols=128, tm=768, tn=128, out_dtype=F32, name="proj_gates")
    gate = matmul(xn_real.reshape(N_TOK, D_MODEL), w_gate[layer], n_cols=2 * D_MODEL,
                  tm=1024, tn=1024, bias=b_gate[layer], act="sigmoid", name="mix_gate")

    tab, consts = _bias_tables(rel_bias)
    k_da = proj3[:, :, COL_DA_K:COL_DA_K + H_DA * 2 * DK_DA]
    v_da = proj3[:, :, COL_DA_V:COL_DA_V + H_DA * DV_DA]
    k2 = jnp.concatenate([k_da, k_da], axis=1)
    v2 = jnp.concatenate([v_da, v_da], axis=1)
    y_da = diff_attention(consts, proj3, k2, v2, tab, lambda_params[layer], subln_da[layer])

    qk_m = conv_qk(proj3, conv_w[layer])
    gp = gate_prep(mg.reshape(BATCH, LP, 128), gate_bias_m[layer])
    gp4 = gp.reshape(BATCH, LP, 4, H_M)
    gcol = jnp.transpose(gp4, (0, 3, 1, 2))
    grow = jnp.transpose(gp4, (0, 3, 2, 1))
    y_m = mlstm(qk_m, proj3, gcol, grow)

    mixed = branch_mix(y_da.reshape(N_TOK, H_DA * DV_DA), y_m.reshape(N_TOK, H_M * DV_M),
                       w_branch_da[layer], w_branch_m[layer], gate)
    h2 = matmul(mixed, w_out[layer], n_cols=D_MODEL, tm=1024, tn=1024,
                res=x.reshape(N_TOK, D_MODEL), out_dtype=F32, name="out_proj")

    hn, top_e_pad, weight_pad = ffn_prep(h2, norm_ffn[layer], w_router[layer], b_router[layer])
    dest, slot_tok, block_e, n_used = _routing(top_e_pad[:, :TOP_K])
    xs = jnp.take(hn, slot_tok, axis=0)
    act = moe_up(block_e, n_used, xs, w1[layer], b1[layer])
    y = moe_down(block_e, n_used, act, w2[layer], b2[layer])
    yg = jnp.take(y, dest, axis=0).reshape(N_TOK, TOP_K * D_MODEL)
    out = combine(yg, weight_pad, h2, norm_final)
    return out.reshape(BATCH, SEQ, D_MODEL)
```

```python
import functools
import math

import jax
import jax.numpy as jnp
from jax import lax
from jax.experimental import pallas as pl
from jax.experimental.pallas import tpu as pltpu

F32 = jnp.float32
BF16 = jnp.bfloat16

D_MODEL = 2048
BATCH = 2
SEQ = 4096
N_META = 16
BLOCK = 128
PAD = (-N_META) % BLOCK
LP = PAD + N_META + SEQ
NBLK = LP // BLOCK
EPS = 1e-6
NEG_INF = -1e30

H_DA = 4
DK_DA = 128
DV_DA = 256
H_M = 4
DK_M = 128
DV_M = 256
CONV_W = 5
N_BUCKETS = 32
MAX_DISTANCE = 128
N_EXPERTS = 32
TOP_K = 4
D_FF = 2048
SWIGLU_ALPHA = 1.702
SWIGLU_LIMIT = 7.0
LAMBDA_INIT = 0.8 - 0.6 * math.exp(-0.3 * 0)

COL_DA_Q = 0
COL_DA_K = 1024
COL_DA_V = 2048
COL_M_Q = 3072
COL_M_K = 3584
COL_M_V = 4096
COL_M_O = 5120
COL_M_G = 6144
PROJ_COLS = 6144

N_TOK = BATCH * SEQ
N_ASSIGN = N_TOK * TOP_K

VMEM_LIMIT = 52 * 1024 * 1024
MOE_VMEM_LIMIT = 58 * 1024 * 1024


def _cparams(n_axes):
    return pltpu.CompilerParams(
        dimension_semantics=("arbitrary",) * n_axes, vmem_limit_bytes=VMEM_LIMIT)


def _rms(v, gain):
    ms = jnp.mean(v * v, axis=-1, keepdims=True)
    return v * lax.rsqrt(ms + EPS) * gain


def _norm_in_kernel(x_ref, meta_ref, g_ref, pad_ref, real_ref):
    j = pl.program_id(1)
    g = g_ref[...]

    @pl.when(j == 0)
    def _():
        pad_ref[0, :PAD, :] = jnp.zeros((PAD, D_MODEL), BF16)
        pad_ref[0, PAD:, :] = _rms(meta_ref[...], g).astype(BF16)

    @pl.when(j > 0)
    def _():
        y = _rms(x_ref[0], g).astype(BF16)
        pad_ref[0] = y
        real_ref[0] = y


def norm_in(x, meta, gain):
    return pl.pallas_call(
        _norm_in_kernel,
        grid=(BATCH, NBLK),
        in_specs=[
            pl.BlockSpec((1, BLOCK, D_MODEL), lambda b, j: (b, jnp.maximum(j - 1, 0), 0)),
            pl.BlockSpec((N_META, D_MODEL), lambda b, j: (0, 0)),
            pl.BlockSpec((1, D_MODEL), lambda b, j: (0, 0)),
        ],
        out_specs=[
            pl.BlockSpec((1, BLOCK, D_MODEL), lambda b, j: (b, j, 0)),
            pl.BlockSpec((1, BLOCK, D_MODEL), lambda b, j: (b, jnp.maximum(j - 1, 0), 0)),
        ],
        out_shape=[
            jax.ShapeDtypeStruct((BATCH, LP, D_MODEL), BF16),
            jax.ShapeDtypeStruct((BATCH, SEQ, D_MODEL), BF16),
        ],
        compiler_params=_cparams(2),
        name="norm_in",
    )(x, meta, gain.reshape(1, D_MODEL))


def _mm_kernel(*refs, has_bias, has_res, act):
    x_ref, w_ref = refs[0], refs[1]
    pos = 2
    b_ref = r_ref = None
    if has_bias:
        b_ref = refs[pos]
        pos += 1
    if has_res:
        r_ref = refs[pos]
        pos += 1
    o_ref, wbf_ref = refs[pos], refs[pos + 1]

    @pl.when(pl.program_id(1) == 0)
    def _():
        wbf_ref[...] = w_ref[...].astype(BF16)

    acc = jnp.dot(x_ref[...], wbf_ref[...], preferred_element_type=F32)
    if has_bias:
        acc = acc + b_ref[...]
    if act == "sigmoid":
        acc = jax.nn.sigmoid(acc)
    if has_res:
        acc = acc + r_ref[...]
    o_ref[...] = acc.astype(o_ref.dtype)


def matmul(x, w, *, n_cols, col_block0=0, tm, tn, bias=None, res=None, act=None,
           out_dtype=BF16, name):
    m, k = x.shape
    in_specs = [
        pl.BlockSpec((tm, k), lambda j, i: (i, 0)),
        pl.BlockSpec((k, tn), lambda j, i: (0, j + col_block0)),
    ]
    args = [x, w]
    if bias is not None:
        in_specs.append(pl.BlockSpec((1, tn), lambda j, i: (0, j)))
        args.append(bias.reshape(1, n_cols))
    if res is not None:
        in_specs.append(pl.BlockSpec((tm, tn), lambda j, i: (i, j)))
        args.append(res)
    return pl.pallas_call(
        functools.partial(_mm_kernel, has_bias=bias is not None, has_res=res is not None, act=act),
        grid=(n_cols // tn, m // tm),
        in_specs=in_specs,
        out_specs=pl.BlockSpec((tm, tn), lambda j, i: (i, j)),
        out_shape=jax.ShapeDtypeStruct((m, n_cols), out_dtype),
        scratch_shapes=[pltpu.VMEM((k, tn), BF16)],
        compiler_params=_cparams(2),
        name=name,
    )(*args)


def _conv_kernel(p_ref, w_ref, o_ref):
    c = pl.program_id(1)
    x = p_ref[0].astype(F32)
    w = w_ref[...]
    half = CONV_W // 2
    acc = w[half:half + 1, :] * x
    for j in range(CONV_W):
        if j != half:
            acc = acc + w[j:j + 1, :] * pltpu.roll(x, (half - j) % LP, axis=0)
    y = acc * jax.nn.sigmoid(acc)
    rows = lax.broadcasted_iota(jnp.int32, (LP, 1), 0)
    y = jnp.where(rows >= PAD, y, 0.0)
    scale = jnp.where(c < 2, DK_M ** -0.5, 1.0).astype(F32)
    o_ref[0] = (y * scale).astype(BF16)


def conv_qk(proj3, conv_w):
    cw = 256
    return pl.pallas_call(
        _conv_kernel,
        grid=(BATCH, (2 * H_M * DK_M) // cw),
        in_specs=[
            pl.BlockSpec((1, LP, cw), lambda b, c: (b, 0, COL_M_Q // cw + c)),
            pl.BlockSpec((CONV_W, cw), lambda b, c: (0, c)),
        ],
        out_specs=pl.BlockSpec((1, LP, cw), lambda b, c: (b, 0, c)),
        out_shape=jax.ShapeDtypeStruct((BATCH, LP, 2 * H_M * DK_M), BF16),
        compiler_params=_cparams(2),
        name="conv_qk",
    )(proj3, conv_w)


def _split_dot(tri, v):
    hi = v.astype(BF16)
    r1 = v - hi.astype(F32)
    mid = r1.astype(BF16)
    lo = (r1 - mid.astype(F32)).astype(BF16)
    return (jnp.dot(tri, hi, preferred_element_type=F32)
            + jnp.dot(tri, mid, preferred_element_type=F32)
            + jnp.dot(tri, lo, preferred_element_type=F32))


def _gate_kernel(g_ref, bias_ref, o_ref):
    ti = lax.broadcasted_iota(jnp.int32, (BLOCK, BLOCK), 0)
    ui = lax.broadcasted_iota(jnp.int32, (BLOCK, BLOCK), 1)
    tril = jnp.where(ui <= ti, 1.0, 0.0).astype(BF16)
    triu = jnp.where(ui >= ti, 1.0, 0.0).astype(BF16)
    ch = lax.broadcasted_iota(jnp.int32, (BLOCK, 4 * H_M), 1)
    typ = lax.shift_right_logical(ch, 2)
    rloc = lax.broadcasted_iota(jnp.int32, (BLOCK, 4 * H_M), 0)

    def body(c, carry):
        r0 = pl.multiple_of(c * BLOCK, BLOCK)
        g = g_ref[0, pl.ds(r0, BLOCK), :][:, :4 * H_M] + bias_ref[...]
        valid = (rloc + r0) >= PAD
        lsig = -(jnp.maximum(-g, 0.0) + jnp.log1p(jnp.exp(-jnp.abs(g))))
        lf = jnp.where(valid, lsig, 0.0)
        cum = _split_dot(tril, lf)
        rcum = _split_dot(triu, lf)
        li = jnp.where(valid, g, -jnp.inf)
        out = jnp.where(typ == 1, cum, jnp.where(typ == 3, rcum, li))
        o_ref[0, pl.ds(r0, BLOCK), :] = out
        return carry

    lax.fori_loop(0, NBLK, body, 0)


def gate_prep(mg3, gate_bias):
    return pl.pallas_call(
        _gate_kernel,
        grid=(BATCH,),
        in_specs=[
            pl.BlockSpec((1, LP, 128), lambda b: (b, 0, 0)),
            pl.BlockSpec((1, 4 * H_M), lambda b: (0, 0)),
        ],
        out_specs=pl.BlockSpec((1, LP, 4 * H_M), lambda b: (b, 0, 0)),
        out_shape=jax.ShapeDtypeStruct((BATCH, LP, 4 * H_M), F32),
        compiler_params=_cparams(1),
        name="gate_prep",
    )(mg3, gate_bias.reshape(1, 4 * H_M))


MLSTM_HP = 1


def _mlstm_kernel(q_ref, k_ref, v_ref, o_ref, gc_ref, gr_ref, y_ref,
                  hf_ref, hb_ref, c_ref, n_ref, m_ref):
    c_ref[...] = jnp.zeros_like(c_ref)
    n_ref[...] = jnp.zeros_like(n_ref)
    m_ref[...] = jnp.zeros_like(m_ref)
    ti = lax.broadcasted_iota(jnp.int32, (BLOCK, BLOCK), 0)
    si = lax.broadcasted_iota(jnp.int32, (BLOCK, BLOCK), 1)
    mask_f = si <= ti
    mask_b = si >= ti

    def chain(c, hl, bwd):
        idx = 2 * hl + bwd
        r0 = pl.multiple_of(c * BLOCK, BLOCK)
        q = q_ref[0, pl.ds(r0, BLOCK), hl * DK_M:(hl + 1) * DK_M]
        k = k_ref[0, pl.ds(r0, BLOCK), hl * DK_M:(hl + 1) * DK_M]
        v = v_ref[0, pl.ds(r0, BLOCK), hl * DV_M:(hl + 1) * DV_M]
        gc = gc_ref[0, hl, pl.ds(r0, BLOCK), :]
        gr = gr_ref[0, hl, :, pl.ds(r0, BLOCK)]
        a = 2 * bwd
        li_c, b_c = gc[:, a:a + 1], gc[:, a + 1:a + 2]
        li_r, b_r = gr[a:a + 1, :], gr[a + 1:a + 2, :]
        b_end = b_c[0:1, :] if bwd else b_c[BLOCK - 1:BLOCK, :]
        m_prev = m_ref[idx][:, 0:1]
        cst = c_ref[idx]
        nst = n_ref[idx]
        dmat = jnp.where(mask_b if bwd else mask_f, b_c - b_r + li_r, -jnp.inf)
        inter = b_c + m_prev
        m_t = jnp.maximum(inter, jnp.max(dmat, axis=1, keepdims=True))
        w_inter = jnp.exp(inter - m_t)
        qk = lax.dot_general(q, k, (((1,), (1,)), ((), ())), preferred_element_type=F32)
        s = qk * jnp.exp(dmat - m_t)
        num = (w_inter * jnp.dot(q, cst.astype(BF16), preferred_element_type=F32)
               + jnp.dot(s.astype(BF16), v, preferred_element_type=F32))
        den = jnp.sum(s + w_inter * (q.astype(F32) * nst), axis=1, keepdims=True)
        h = num / jnp.maximum(jnp.abs(den), jnp.exp(-m_t))
        dst = hb_ref if bwd else hf_ref
        dst[pl.ds(r0, BLOCK), hl * DV_M:(hl + 1) * DV_M] = h
        ldec_c = b_end - b_c + li_c
        ldec_r = b_end - b_r + li_r
        m_new = jnp.maximum(b_end + m_prev, jnp.max(ldec_r, axis=1, keepdims=True))
        w_c = jnp.exp(b_end + m_prev - m_new)
        w_s = jnp.exp(ldec_c - m_new)
        wv = (w_s * v.astype(F32)).astype(BF16)
        c_ref[idx] = w_c * cst + lax.dot_general(
            k, wv, (((0,), (0,)), ((), ())), preferred_element_type=F32)
        n_ref[idx] = w_c * nst + jnp.sum(w_s * k.astype(F32), axis=0, keepdims=True)
        m_ref[idx] = jnp.broadcast_to(m_new, (1, BLOCK))

    def step(c, carry):
        for hl in range(MLSTM_HP):
            chain(c, hl, 0)
            chain(NBLK - 1 - c, hl, 1)
        return carry

    lax.fori_loop(0, NBLK, step, 0)

    def epilogue(c, carry):
        r0 = pl.multiple_of(c * BLOCK, BLOCK)
        hsum = hf_ref[pl.ds(r0, BLOCK), :] + hb_ref[pl.ds(r0, BLOCK), :]
        og = jax.nn.sigmoid(o_ref[0, pl.ds(r0, BLOCK), :].astype(F32))
        y_ref[0, pl.ds(r0 - BLOCK, BLOCK), :] = (og * hsum).astype(BF16)
        return carry

    lax.fori_loop(1, NBLK, epilogue, 0)


def mlstm(qk_m, proj3, gcol, grow):
    hp = MLSTM_HP
    kw, vw = hp * DK_M, hp * DV_M
    return pl.pallas_call(
        _mlstm_kernel,
        grid=(BATCH, H_M // hp),
        in_specs=[
            pl.BlockSpec((1, LP, kw), lambda b, g: (b, 0, g)),
            pl.BlockSpec((1, LP, kw), lambda b, g: (b, 0, (H_M * DK_M) // kw + g)),
            pl.BlockSpec((1, LP, vw), lambda b, g: (b, 0, COL_M_V // vw + g)),
            pl.BlockSpec((1, LP, vw), lambda b, g: (b, 0, COL_M_O // vw + g)),
            pl.BlockSpec((1, hp, LP, 4), lambda b, g: (b, g, 0, 0)),
            pl.BlockSpec((1, hp, 4, LP), lambda b, g: (b, g, 0, 0)),
        ],
        out_specs=pl.BlockSpec((1, SEQ, vw), lambda b, g: (b, 0, g)),
        out_shape=jax.ShapeDtypeStruct((BATCH, SEQ, H_M * DV_M), BF16),
        scratch_shapes=[
            pltpu.VMEM((LP, vw), F32),
            pltpu.VMEM((LP, vw), F32),
            pltpu.VMEM((2 * hp, DK_M, DV_M), F32),
            pltpu.VMEM((2 * hp, 1, DK_M), F32),
            pltpu.VMEM((2 * hp, 1, BLOCK), F32),
        ],
        compiler_params=_cparams(2),
        name="mlstm",
    )(qk_m, qk_m, proj3, proj3, gcol, grow)


ATT_BAND = 3 * BLOCK
ATT_GROUP = 6 * BLOCK
ATT_NGROUP = (LP - ATT_BAND) // ATT_GROUP


def _attn_kernel(c_ref, q_ref, k_ref, v_ref, tab_ref, lam_ref, sg_ref, o_ref, s_ref):
    h = pl.program_id(1)
    qb = pl.program_id(2) + 1
    q = q_ref[0]
    scale = DK_DA ** -0.5
    c_neg = c_ref[h, 0]
    c_pos = c_ref[h, 1]
    lp = lam_ref[...]
    lam = (jnp.exp(jnp.sum(lp[0:1] * lp[1:2], axis=1, keepdims=True))
           - jnp.exp(jnp.sum(lp[2:3] * lp[3:4], axis=1, keepdims=True)) + LAMBDA_INIT)

    groups = [((qb - 1) * BLOCK, ATT_BAND, 0)]
    for g in range(ATT_NGROUP):
        groups.append(((qb + 2) * BLOCK + g * ATT_GROUP, ATT_GROUP, ATT_BAND + g * ATT_GROUP))

    def lane_fold(acc, t, op):
        for j in range(t.shape[1] // BLOCK):
            piece = t[:, j * BLOCK:(j + 1) * BLOCK]
            acc = piece if acc is None else op(acc, piece)
        return acc

    mx = [None, None]
    for gi, (koff, width, col0) in enumerate(groups):
        koff = pl.multiple_of(koff, BLOCK)
        if gi == 0:
            bias = tab_ref[0, 0]
        else:
            kpos = koff + lax.broadcasted_iota(jnp.int32, (1, width), 1)
            bias = jnp.where(kpos < LP, c_pos, jnp.where(kpos < LP + PAD, NEG_INF, c_neg))
        for m in range(2):
            kk = k_ref[0, pl.ds(koff, width), m * DK_DA:(m + 1) * DK_DA]
            s = lax.dot_general(q[:, m * DK_DA:(m + 1) * DK_DA], kk, (((1,), (1,)), ((), ())),
                                preferred_element_type=F32) * scale + bias
            s_ref[m, :, col0:col0 + width] = s
            mx[m] = lane_fold(mx[m], s, jnp.maximum)
    row_max = [jnp.max(mx[m], axis=1, keepdims=True) for m in range(2)]

    lsum = [None, None]
    acc = [None, None]
    for koff, width, col0 in groups:
        koff = pl.multiple_of(koff, BLOCK)
        vv = v_ref[0, pl.ds(koff, width), :]
        for m in range(2):
            p = jnp.exp(s_ref[m, :, col0:col0 + width] - row_max[m])
            lsum[m] = lane_fold(lsum[m], p, jnp.add)
            pv = jnp.dot(p.astype(BF16), vv, preferred_element_type=F32)
            acc[m] = pv if acc[m] is None else acc[m] + pv
    l1 = jnp.sum(lsum[0], axis=1, keepdims=True)
    l2 = jnp.sum(lsum[1], axis=1, keepdims=True)
    o = acc[0] / l1 - lam * (acc[1] / l2)
    o_ref[0] = (_rms(o, sg_ref[...]) * (1.0 - LAMBDA_INIT)).astype(BF16)


def diff_attention(consts, proj3, k2, v2, tab, lam_params, subln):
    nq = SEQ // BLOCK

    def tab_map(b, h, i):
        case = jnp.where(i == 0, 0, jnp.where(i == nq - 1, 2, 1))
        return (h, case, 0, 0)

    return pl.pallas_call(
        _attn_kernel,
        grid=(BATCH, H_DA, nq),
        in_specs=[
            pl.BlockSpec(memory_space=pltpu.SMEM),
            pl.BlockSpec((1, BLOCK, 2 * DK_DA), lambda b, h, i: (b, i + 1, h)),
            pl.BlockSpec((1, 2 * LP, 2 * DK_DA), lambda b, h, i: (b, 0, h)),
            pl.BlockSpec((1, 2 * LP, DV_DA), lambda b, h, i: (b, 0, h)),
            pl.BlockSpec((1, 1, BLOCK, ATT_BAND), tab_map),
            pl.BlockSpec((4, DK_DA), lambda b, h, i: (0, 0)),
            pl.BlockSpec((1, DV_DA), lambda b, h, i: (0, 0)),
        ],
        out_specs=pl.BlockSpec((1, BLOCK, DV_DA), lambda b, h, i: (b, i, h)),
        out_shape=jax.ShapeDtypeStruct((BATCH, SEQ, H_DA * DV_DA), BF16),
        scratch_shapes=[pltpu.VMEM((2, BLOCK, LP), F32)],
        compiler_params=_cparams(3),
        name="diff_attn",
    )(consts, proj3, k2, v2, tab, lam_params, subln.reshape(1, DV_DA))


def _bias_tables(rel_bias):
    rb = rel_bias.astype(F32)
    i = jnp.arange(BLOCK, dtype=jnp.int32)[:, None]
    j = jnp.arange(ATT_BAND, dtype=jnp.int32)[None, :]
    rel = j - BLOCK - i
    nb = N_BUCKETS // 2
    max_exact = nb // 2
    n = jnp.abs(rel)
    nf = jnp.maximum(n, 1).astype(F32)
    large = max_exact + (jnp.log(nf / max_exact) / math.log(MAX_DISTANCE / max_exact)
                         * (nb - max_exact)).astype(jnp.int32)
    large = jnp.minimum(large, nb - 1)
    bucket = jnp.where(rel > 0, nb, 0) + jnp.where(n < max_exact, n, large)
    gen = jnp.transpose(rb[bucket], (2, 0, 1))
    c_neg = rb[nb - 1]
    c_pos = rb[N_BUCKETS - 1]
    jj = j[None]
    first = jnp.where(jj < PAD, NEG_INF, gen)
    wrapped = jnp.where(jj - 2 * BLOCK < PAD, NEG_INF, c_neg[:, None, None])
    last = jnp.where(jj >= 2 * BLOCK, wrapped, gen)
    tab = jnp.stack([first, gen, last], axis=1)
    consts = jnp.stack([c_neg, c_pos], axis=1)
    return tab, consts


def _mix_kernel(ya_ref, ym_ref, wa_ref, wm_ref, ga_ref, gm_ref, o_ref, wa_bf, wm_bf):
    @pl.when(pl.program_id(1) == 0)
    def _():
        wa_bf[...] = wa_ref[...].astype(BF16)
        wm_bf[...] = wm_ref[...].astype(BF16)

    a = jnp.dot(ya_ref[...], wa_bf[...], preferred_element_type=F32)
    m = jnp.dot(ym_ref[...], wm_bf[...], preferred_element_type=F32)
    o_ref[...] = (ga_ref[...].astype(F32) * a + gm_ref[...].astype(F32) * m).astype(BF16)


def branch_mix(y_da, y_m, w_da, w_m, gate, *, tm=512, tn=1024):
    m, k = y_da.shape
    nj = D_MODEL // tn
    return pl.pallas_call(
        _mix_kernel,
        grid=(nj, m // tm),
        in_specs=[
            pl.BlockSpec((tm, k), lambda j, i: (i, 0)),
            pl.BlockSpec((tm, k), lambda j, i: (i, 0)),
            pl.BlockSpec((k, tn), lambda j, i: (0, j)),
            pl.BlockSpec((k, tn), lambda j, i: (0, j)),
            pl.BlockSpec((tm, tn), lambda j, i: (i, j)),
            pl.BlockSpec((tm, tn), lambda j, i: (i, nj + j)),
        ],
        out_specs=pl.BlockSpec((tm, tn), lambda j, i: (i, j)),
        out_shape=jax.ShapeDtypeStruct((m, D_MODEL), BF16),
        scratch_shapes=[pltpu.VMEM((k, tn), BF16), pltpu.VMEM((k, tn), BF16)],
        compiler_params=_cparams(2),
        name="branch_mix",
    )(y_da, y_m, w_da, w_m, gate, gate)


FFN_TM = 1024
HALF = D_MODEL // 2


def _pack_bf16_pairs(v):
    lo = lax.bitcast_convert_type(v[:, :HALF].astype(BF16).astype(F32), jnp.uint32)
    hi = lax.bitcast_convert_type(v[:, HALF:].astype(BF16).astype(F32), jnp.uint32)
    return (hi & jnp.uint32(0xFFFF0000)) | lax.shift_right_logical(lo, jnp.uint32(16))


def _unpack_bf16_pairs(w):
    lo = lax.bitcast_convert_type(lax.shift_left(w, jnp.uint32(16)), F32).astype(BF16)
    hi = lax.bitcast_convert_type(w & jnp.uint32(0xFFFF0000), F32).astype(BF16)
    return lo, hi


def _ffn_prep_kernel(h_ref, g_ref, wr_ref, br_ref, hn_ref, e_ref, w_ref, r_ref, cnt_ref, base_ref):
    @pl.when(pl.program_id(0) == 0)
    def _():
        base_ref[...] = jnp.zeros_like(base_ref)

    hn = _rms(h_ref[...], g_ref[...])
    hn_ref[...] = _pack_bf16_pairs(hn)
    logits = jnp.dot(hn, wr_ref[...], preferred_element_type=F32,
                     precision=lax.Precision.HIGHEST) + br_ref[...]
    lane = lax.broadcasted_iota(jnp.int32, (FFN_TM, N_EXPERTS), 1)
    lane_o = lax.broadcasted_iota(jnp.int32, (FFN_TM, 128), 1)
    ti = lax.broadcasted_iota(jnp.int32, (FFN_TM, FFN_TM), 0)
    ui = lax.broadcasted_iota(jnp.int32, (FFN_TM, FFN_TM), 1)
    tril = jnp.where(ui <= ti, 1.0, 0.0).astype(BF16)
    e_out = jnp.zeros((FFN_TM, 128), jnp.int32)
    r_out = jnp.zeros((FFN_TM, 128), jnp.int32)
    l_out = jnp.full((FFN_TM, 128), -jnp.inf, F32)
    base = base_ref[...]
    l = logits
    for kk in range(TOP_K):
        mk = jnp.max(l, axis=1, keepdims=True)
        ik = jnp.min(jnp.where(l == mk, lane, N_EXPERTS), axis=1, keepdims=True)
        hit = lane == ik
        oh = jnp.where(hit, 1.0, 0.0)
        cum = jnp.dot(tril, oh.astype(BF16), preferred_element_type=F32)
        rank = jnp.sum(oh * (cum + base), axis=1, keepdims=True) - 1.0
        base = base + jnp.sum(oh, axis=0, keepdims=True)
        e_out = jnp.where(lane_o == kk, ik, e_out)
        r_out = jnp.where(lane_o == kk, rank.astype(jnp.int32), r_out)
        l_out = jnp.where(lane_o == kk, mk, l_out)
        l = jnp.where(hit, -jnp.inf, l)
    base_ref[...] = base
    cnt_ref[...] = base
    ex = jnp.exp(l_out - jnp.max(l_out, axis=1, keepdims=True))
    e_ref[...] = e_out
    r_ref[...] = r_out
    w_ref[...] = ex / jnp.sum(ex, axis=1, keepdims=True)


def ffn_prep(h2, gain, w_router, b_router):
    row = lambda i: (i, 0)
    fixed = lambda i: (0, 0)
    return pl.pallas_call(
        _ffn_prep_kernel,
        grid=(N_TOK // FFN_TM,),
        in_specs=[
            pl.BlockSpec((FFN_TM, D_MODEL), row),
            pl.BlockSpec((1, D_MODEL), fixed),
            pl.BlockSpec((D_MODEL, N_EXPERTS), fixed),
            pl.BlockSpec((1, N_EXPERTS), fixed),
        ],
        out_specs=[
            pl.BlockSpec((FFN_TM, HALF), row),
            pl.BlockSpec((FFN_TM, 128), row),
            pl.BlockSpec((FFN_TM, 128), row),
            pl.BlockSpec((FFN_TM, 128), row),
            pl.BlockSpec((1, N_EXPERTS), fixed),
        ],
        out_shape=[
            jax.ShapeDtypeStruct((N_TOK, HALF), jnp.uint32),
            jax.ShapeDtypeStruct((N_TOK, 128), jnp.int32),
            jax.ShapeDtypeStruct((N_TOK, 128), F32),
            jax.ShapeDtypeStruct((N_TOK, 128), jnp.int32),
            jax.ShapeDtypeStruct((1, N_EXPERTS), F32),
        ],
        scratch_shapes=[pltpu.VMEM((1, N_EXPERTS), F32)],
        compiler_params=_cparams(1),
        name="ffn_prep",
    )(h2, gain.reshape(1, D_MODEL), w_router, b_router.reshape(1, N_EXPERTS))


SEG_ALIGN = 128
MOE_R = N_ASSIGN + N_EXPERTS * SEG_ALIGN


def _plan(counts_f, top_e, rank):
    counts = counts_f[0].astype(jnp.int32)
    seg_rows = (counts + SEG_ALIGN - 1) // SEG_ALIGN * SEG_ALIGN
    seg_start = jnp.cumsum(seg_rows) - seg_rows
    eq = top_e[:, :, None] == jnp.arange(N_EXPERTS, dtype=jnp.int32)[None, None, :]
    dest = jnp.sum(jnp.where(eq, seg_start[None, None, :], 0), axis=-1) + rank
    used = jnp.sum(seg_rows)
    slack = jnp.stack([used, (MOE_R - used) // SEG_ALIGN])
    return (dest.reshape(N_ASSIGN).astype(jnp.int32), seg_start.astype(jnp.int32),
            seg_rows.astype(jnp.int32), (seg_start + counts).astype(jnp.int32),
            (seg_rows - counts).astype(jnp.int32), slack.astype(jnp.int32))


def _zero_slack(slack_ref, zero_block, dst_rows, sem):
    zero_block[...] = jnp.zeros_like(zero_block)

    def copy(j):
        r0 = pl.multiple_of(slack_ref[0] + j * SEG_ALIGN, SEG_ALIGN)
        return pltpu.make_async_copy(zero_block, dst_rows(pl.ds(r0, SEG_ALIGN)), sem)

    def start(j, carry):
        copy(j).start()
        return carry

    def wait(j, carry):
        copy(j).wait()
        return carry

    lax.fori_loop(0, slack_ref[1], start, 0)
    lax.fori_loop(0, slack_ref[1], wait, 0)


DISP_TOK = 64


def _dispatch_kernel(dest_ref, pad0_ref, padn_ref, slack_ref, hn_ref, xs_ref,
                     zrow_ref, zblk_ref, sem, zsem):
    zrow_ref[...] = jnp.zeros_like(zrow_ref)
    _zero_slack(slack_ref, zblk_ref, lambda rows: xs_ref.at[rows], zsem.at[0])

    def wait_batch():
        n = DISP_TOK * TOP_K
        pltpu.make_async_copy(hn_ref.at[pl.ds(0, n)], xs_ref.at[pl.ds(0, n)], sem.at[0]).wait()

    def batch(b, carry):
        def tok(t, c2):
            n = b * DISP_TOK + t
            for kk in range(TOP_K):
                d = dest_ref[n * TOP_K + kk]
                pltpu.make_async_copy(hn_ref.at[pl.ds(n, 1)], xs_ref.at[pl.ds(d, 1)], sem.at[0]).start()
            return c2

        lax.fori_loop(0, DISP_TOK, tok, 0)

        @pl.when(b > 0)
        def _():
            wait_batch()

        return carry

    lax.fori_loop(0, N_TOK // DISP_TOK, batch, 0)
    wait_batch()

    def expert(e, carry):
        p0 = pad0_ref[e]
        pn = padn_ref[e]

        def zstart(r, c2):
            pltpu.make_async_copy(zrow_ref, xs_ref.at[pl.ds(p0 + r, 1)], zsem.at[0]).start()
            return c2

        def zwait(r, c2):
            pltpu.make_async_copy(zrow_ref, xs_ref.at[pl.ds(p0, 1)], zsem.at[0]).wait()
            return c2

        lax.fori_loop(0, pn, zstart, 0)
        lax.fori_loop(0, pn, zwait, 0)
        return carry

    lax.fori_loop(0, N_EXPERTS, expert, 0)


def dispatch(dest, pad0, padn, slack, hn_packed):
    grid_spec = pltpu.PrefetchScalarGridSpec(
        num_scalar_prefetch=4,
        grid=(1,),
        in_specs=[pl.BlockSpec(memory_space=pl.ANY)],
        out_specs=pl.BlockSpec(memory_space=pl.ANY),
        scratch_shapes=[pltpu.VMEM((1, HALF), jnp.uint32),
                        pltpu.VMEM((SEG_ALIGN, HALF), jnp.uint32),
                        pltpu.SemaphoreType.DMA((1,)), pltpu.SemaphoreType.DMA((1,))],
    )
    return pl.pallas_call(
        _dispatch_kernel,
        grid_spec=grid_spec,
        out_shape=jax.ShapeDtypeStruct((MOE_R, HALF), jnp.uint32),
        compiler_params=_cparams(1),
        name="moe_dispatch",
    )(dest, pad0, padn, slack, hn_packed)


MOE_CH = 256
MOE_TF = 1024
MOE_NF = D_FF // MOE_TF


class _CopyGroup:
    def __init__(self, copies):
        self.copies = copies

    def start(self):
        for cp in self.copies:
            cp.start()

    def wait(self):
        for cp in self.copies:
            cp.wait()


def _stream_rows(start, rows, make_in, make_out, make_tail_in, make_tail_out,
                 compute_chunk, compute_tail, before_first_wait):
    n_ch = lax.shift_right_logical(rows, 8)
    tail = rows - n_ch * MOE_CH
    tail_row = start + n_ch * MOE_CH

    @pl.when(tail > 0)
    def _():
        make_tail_in(tail_row).start()

    @pl.when(n_ch > 0)
    def _():
        make_in(0, 0).start()

    before_first_wait()

    @pl.when(tail > 0)
    def _():
        make_tail_in(tail_row).wait()
        compute_tail()
        make_tail_out(tail_row).start()

    def body(c, carry):
        slot = lax.rem(c, 2)
        make_in(c, slot).wait()

        @pl.when(c + 1 < n_ch)
        def _():
            make_in(c + 1, 1 - slot).start()

        @pl.when(c >= 2)
        def _():
            make_out(c - 2, slot).wait()

        compute_chunk(slot)
        make_out(c, slot).start()
        return carry

    lax.fori_loop(0, n_ch, body, 0)

    @pl.when(n_ch >= 2)
    def _():
        make_out(n_ch - 2, lax.rem(n_ch, 2)).wait()

    @pl.when(n_ch >= 1)
    def _():
        make_out(n_ch - 1, lax.rem(n_ch + 1, 2)).wait()

    @pl.when(tail > 0)
    def _():
        make_tail_out(tail_row).wait()


def _moe_up_kernel(seg_ref, rows_ref, slack_ref, xs_ref, wg_ref, wl_ref, bg_ref, bl_ref, act_ref,
                   xbuf, obuf, xtail, otail, wg_bf, wl_bf, isem, osem, tsem):
    f = pl.program_id(0)
    e = pl.program_id(1)
    start = seg_ref[e]
    rows = rows_ref[e]

    def chunk_rows(c):
        return pl.ds(pl.multiple_of(start + c * MOE_CH, SEG_ALIGN), MOE_CH)

    def tail_rows(r0):
        return pl.ds(pl.multiple_of(r0, SEG_ALIGN), SEG_ALIGN)

    def make_in(c, slot):
        return pltpu.make_async_copy(xs_ref.at[chunk_rows(c)], xbuf.at[slot], isem.at[slot])

    def make_out(c, slot):
        return pltpu.make_async_copy(obuf.at[slot], act_ref.at[f, chunk_rows(c)], osem.at[slot])

    def make_tail_in(r0):
        return pltpu.make_async_copy(xs_ref.at[tail_rows(r0)], xtail, tsem.at[0])

    def make_tail_out(r0):
        return pltpu.make_async_copy(otail, act_ref.at[f, tail_rows(r0)], tsem.at[1])

    def cast_weights():
        @pl.when(rows > 0)
        def _():
            wg_bf[...] = wg_ref[0].astype(BF16)
            wl_bf[...] = wl_ref[0].astype(BF16)

    def expert_mlp(words):
        lo, hi = _unpack_bf16_pairs(words)
        glu = (jnp.dot(lo, wg_bf[:HALF, :], preferred_element_type=F32)
               + jnp.dot(hi, wg_bf[HALF:, :], preferred_element_type=F32) + bg_ref[0])
        lin = (jnp.dot(lo, wl_bf[:HALF, :], preferred_element_type=F32)
               + jnp.dot(hi, wl_bf[HALF:, :], preferred_element_type=F32) + bl_ref[0])
        glu = jnp.minimum(glu, SWIGLU_LIMIT)
        lin = jnp.clip(lin, -SWIGLU_LIMIT, SWIGLU_LIMIT)
        return (glu * jax.nn.sigmoid(SWIGLU_ALPHA * glu) * (lin + 1.0)).astype(BF16)

    def compute_chunk(slot):
        obuf[slot] = expert_mlp(xbuf[slot])

    def compute_tail():
        otail[...] = expert_mlp(xtail[...])

    _stream_rows(start, rows, make_in, make_out, make_tail_in, make_tail_out,
                 compute_chunk, compute_tail, cast_weights)

    @pl.when(e == N_EXPERTS - 1)
    def _():
        _zero_slack(slack_ref, otail, lambda rr: act_ref.at[f, rr], tsem.at[1])


def moe_up(seg_start, seg_rows, slack, xs, w1, b1):
    grid_spec = pltpu.PrefetchScalarGridSpec(
        num_scalar_prefetch=3,
        grid=(MOE_NF, N_EXPERTS),
        in_specs=[
            pl.BlockSpec(memory_space=pl.ANY),
            pl.BlockSpec((1, D_MODEL, MOE_TF), lambda f, e, s, r, z: (e, 0, f)),
            pl.BlockSpec((1, D_MODEL, MOE_TF), lambda f, e, s, r, z: (e, 0, MOE_NF + f)),
            pl.BlockSpec((1, 1, MOE_TF), lambda f, e, s, r, z: (e, 0, f)),
            pl.BlockSpec((1, 1, MOE_TF), lambda f, e, s, r, z: (e, 0, MOE_NF + f)),
        ],
        out_specs=pl.BlockSpec(memory_space=pl.ANY),
        scratch_shapes=[
            pltpu.VMEM((2, MOE_CH, HALF), jnp.uint32),
            pltpu.VMEM((2, MOE_CH, MOE_TF), BF16),
            pltpu.VMEM((SEG_ALIGN, HALF), jnp.uint32),
            pltpu.VMEM((SEG_ALIGN, MOE_TF), BF16),
            pltpu.VMEM((D_MODEL, MOE_TF), BF16),
            pltpu.VMEM((D_MODEL, MOE_TF), BF16),
            pltpu.SemaphoreType.DMA((2,)),
            pltpu.SemaphoreType.DMA((2,)),
            pltpu.SemaphoreType.DMA((2,)),
        ],
    )
    b13 = b1.reshape(N_EXPERTS, 1, 2 * D_FF)
    return pl.pallas_call(
        _moe_up_kernel,
        grid_spec=grid_spec,
        out_shape=jax.ShapeDtypeStruct((MOE_NF, MOE_R, MOE_TF), BF16),
        compiler_params=pltpu.CompilerParams(
            dimension_semantics=("arbitrary", "arbitrary"), vmem_limit_bytes=MOE_VMEM_LIMIT),
        name="moe_up",
    )(seg_start, seg_rows, slack, xs, w1, w1, b13, b13)


def _moe_down_kernel(seg_ref, rows_ref, slack_ref, act_ref, w_ref, b_ref, y_ref,
                     xbuf, obuf, xtail, otail, w_bf, isem, osem, tsem):
    e = pl.program_id(0)
    start = seg_ref[e]
    rows = rows_ref[e]

    def chunk_rows(c):
        return pl.ds(pl.multiple_of(start + c * MOE_CH, SEG_ALIGN), MOE_CH)

    def tail_rows(r0):
        return pl.ds(pl.multiple_of(r0, SEG_ALIGN), SEG_ALIGN)

    def make_in(c, slot):
        return _CopyGroup([
            pltpu.make_async_copy(act_ref.at[j, chunk_rows(c)],
                                  xbuf.at[slot, :, pl.ds(j * MOE_TF, MOE_TF)], isem.at[slot])
            for j in range(MOE_NF)])

    def make_out(c, slot):
        return pltpu.make_async_copy(obuf.at[slot], y_ref.at[chunk_rows(c)], osem.at[slot])

    def make_tail_in(r0):
        return _CopyGroup([
            pltpu.make_async_copy(act_ref.at[j, tail_rows(r0)],
                                  xtail.at[:, pl.ds(j * MOE_TF, MOE_TF)], tsem.at[0])
            for j in range(MOE_NF)])

    def make_tail_out(r0):
        return pltpu.make_async_copy(otail, y_ref.at[tail_rows(r0)], tsem.at[1])

    def cast_weights():
        @pl.when(rows > 0)
        def _():
            w_bf[...] = w_ref[0].astype(BF16)

    def compute_chunk(slot):
        obuf[slot] = jnp.dot(xbuf[slot], w_bf[...], preferred_element_type=F32) + b_ref[0]

    def compute_tail():
        otail[...] = jnp.dot(xtail[...], w_bf[...], preferred_element_type=F32) + b_ref[0]

    _stream_rows(start, rows, make_in, make_out, make_tail_in, make_tail_out,
                 compute_chunk, compute_tail, cast_weights)

    @pl.when(e == N_EXPERTS - 1)
    def _():
        _zero_slack(slack_ref, otail, lambda rr: y_ref.at[rr], tsem.at[1])


def moe_down(seg_start, seg_rows, slack, act, w2, b2):
    grid_spec = pltpu.PrefetchScalarGridSpec(
        num_scalar_prefetch=3,
        grid=(N_EXPERTS,),
        in_specs=[
            pl.BlockSpec(memory_space=pl.ANY),
            pl.BlockSpec((1, D_FF, D_MODEL), lambda e, s, r, z: (e, 0, 0)),
            pl.BlockSpec((1, 1, D_MODEL), lambda e, s, r, z: (e, 0, 0)),
        ],
        out_specs=pl.BlockSpec(memory_space=pl.ANY),
        scratch_shapes=[
            pltpu.VMEM((2, MOE_CH, D_FF), BF16),
            pltpu.VMEM((2, MOE_CH, D_MODEL), F32),
            pltpu.VMEM((SEG_ALIGN, D_FF), BF16),
            pltpu.VMEM((SEG_ALIGN, D_MODEL), F32),
            pltpu.VMEM((D_FF, D_MODEL), BF16),
            pltpu.SemaphoreType.DMA((2,)),
            pltpu.SemaphoreType.DMA((2,)),
            pltpu.SemaphoreType.DMA((2,)),
        ],
    )
    return pl.pallas_call(
        _moe_down_kernel,
        grid_spec=grid_spec,
        out_shape=jax.ShapeDtypeStruct((MOE_R, D_MODEL), F32),
        compiler_params=pltpu.CompilerParams(
            dimension_semantics=("arbitrary",), vmem_limit_bytes=MOE_VMEM_LIMIT),
        name="moe_down",
    )(seg_start, seg_rows, slack, act, w2, b2.reshape(N_EXPERTS, 1, D_MODEL))


COMB_TM = 128
COMB_NT = N_TOK // COMB_TM


def _combine_kernel(dest_ref, y_ref, w_ref, h_ref, g_ref, o_ref, buf, sem):
    i = pl.program_id(0)

    def fetch(tile, slot):
        def tok(t, carry):
            a = (tile * COMB_TM + t) * TOP_K
            for kk in range(TOP_K):
                pltpu.make_async_copy(y_ref.at[pl.ds(dest_ref[a + kk], 1)],
                                      buf.at[slot, kk, pl.ds(t, 1)], sem.at[slot]).start()
            return carry

        lax.fori_loop(0, COMB_TM, tok, 0)

    @pl.when(i == 0)
    def _():
        fetch(0, 0)

    slot = lax.rem(i, 2)

    @pl.when(i + 1 < COMB_NT)
    def _():
        fetch(i + 1, 1 - slot)

    for kk in range(TOP_K):
        pltpu.make_async_copy(y_ref.at[pl.ds(0, COMB_TM)], buf.at[slot, kk], sem.at[slot]).wait()
    w = w_ref[...]
    ff = w[:, 0:1] * buf[slot, 0]
    for kk in range(1, TOP_K):
        ff = ff + w[:, kk:kk + 1] * buf[slot, kk]
    o_ref[...] = _rms(h_ref[...] + ff, g_ref[...])


def combine(dest, y, weight, h2, gain):
    grid_spec = pltpu.PrefetchScalarGridSpec(
        num_scalar_prefetch=1,
        grid=(COMB_NT,),
        in_specs=[
            pl.BlockSpec(memory_space=pl.ANY),
            pl.BlockSpec((COMB_TM, 128), lambda i, d: (i, 0)),
            pl.BlockSpec((COMB_TM, D_MODEL), lambda i, d: (i, 0)),
            pl.BlockSpec((1, D_MODEL), lambda i, d: (0, 0)),
        ],
        out_specs=pl.BlockSpec((COMB_TM, D_MODEL), lambda i, d: (i, 0)),
        scratch_shapes=[pltpu.VMEM((2, TOP_K, COMB_TM, D_MODEL), F32),
                        pltpu.SemaphoreType.DMA((2,))],
    )
    return pl.pallas_call(
        _combine_kernel,
        grid_spec=grid_spec,
        out_shape=jax.ShapeDtypeStruct((N_TOK, D_MODEL), F32),
        compiler_params=_cparams(1),
        name="moe_combine",
    )(dest, y, weight, h2, gain.reshape(1, D_MODEL))


def kernel(x, meta_tokens, rel_bias, norm_mix, w_in, conv_w, gate_bias_m, lambda_params, subln_da,
           w_branch_da, w_branch_m, w_gate, b_gate, w_out, norm_ffn, w_router, b_router,
           w1, b1, w2, b2, norm_final):
    layer = 0
    xn_pad, xn_real = norm_in(x, meta_tokens, norm_mix[layer])
    xn_pad2 = xn_pad.reshape(BATCH * LP, D_MODEL)
    proj = matmul(xn_pad2, w_in[layer], n_cols=PROJ_COLS, tm=768, tn=1024, name="proj_in")
    proj3 = proj.reshape(BATCH, LP, PROJ_COLS)
    w_g = jnp.pad(w_in[layer][:, COL_M_G:], ((0, 0), (0, 128 - 4 * H_M)))
    mg = matmul(xn_pad2, w_g, n_cols=128, tm=768, tn=128, out_dtype=F32, name="proj_gates")
    gate = matmul(xn_real.reshape(N_TOK, D_MODEL), w_gate[layer], n_cols=2 * D_MODEL,
                  tm=1024, tn=1024, bias=b_gate[layer], act="sigmoid", name="mix_gate")

    tab, consts = _bias_tables(rel_bias)
    k_da = proj3[:, :, COL_DA_K:COL_DA_K + H_DA * 2 * DK_DA]
    v_da = proj3[:, :, COL_DA_V:COL_DA_V + H_DA * DV_DA]
    k2 = jnp.concatenate([k_da, k_da], axis=1)
    v2 = jnp.concatenate([v_da, v_da], axis=1)
    y_da = diff_attention(consts, proj3, k2, v2, tab, lambda_params[layer], subln_da[layer])

    qk_m = conv_qk(proj3, conv_w[layer])
    gp = gate_prep(mg.reshape(BATCH, LP, 128), gate_bias_m[layer])
    gp4 = gp.reshape(BATCH, LP, 4, H_M)
    gcol = jnp.transpose(gp4, (0, 3, 1, 2))
    grow = jnp.transpose(gp4, (0, 3, 2, 1))
    y_m = mlstm(qk_m, proj3, gcol, grow)

    mixed = branch_mix(y_da.reshape(N_TOK, H_DA * DV_DA), y_m.reshape(N_TOK, H_M * DV_M),
                       w_branch_da[layer], w_branch_m[layer], gate)
    h2 = matmul(mixed, w_out[layer], n_cols=D_MODEL, tm=1024, tn=1024,
                res=x.reshape(N_TOK, D_MODEL), out_dtype=F32, name="out_proj")

    hn_packed, top_e, weight, rank, counts = ffn_prep(
        h2, norm_ffn[layer], w_router[layer], b_router[layer])
    dest, seg_start, seg_rows, pad0, padn, slack = _plan(
        counts, top_e[:, :TOP_K], rank[:, :TOP_K])
    xs = dispatch(dest, pad0, padn, slack, hn_packed)
    act = moe_up(seg_start, seg_rows, slack, xs, w1[layer], b1[layer])
    y = moe_down(seg_start, seg_rows, slack, act, w2[layer], b2[layer])
    out = combine(dest, y, weight, h2, norm_final)
    return out.reshape(BATCH, SEQ, D_MODEL)
```

```python
import functools
import math

import jax
import jax.numpy as jnp
from jax import lax
from jax.experimental import pallas as pl
from jax.experimental.pallas import tpu as pltpu

F32 = jnp.float32
BF16 = jnp.bfloat16

D_MODEL = 2048
BATCH = 2
SEQ = 4096
N_META = 16
BLOCK = 128
PAD = (-N_META) % BLOCK
LP = PAD + N_META + SEQ
NBLK = LP // BLOCK
EPS = 1e-6
NEG_INF = -1e30

H_DA = 4
DK_DA = 128
DV_DA = 256
H_M = 4
DK_M = 128
DV_M = 256
CONV_W = 5
N_BUCKETS = 32
MAX_DISTANCE = 128
N_EXPERTS = 32
TOP_K = 4
D_FF = 2048
SWIGLU_ALPHA = 1.702
SWIGLU_LIMIT = 7.0
LAMBDA_INIT = 0.8 - 0.6 * math.exp(-0.3 * 0)

COL_DA_Q = 0
COL_DA_K = 1024
COL_DA_V = 2048
COL_M_Q = 3072
COL_M_K = 3584
COL_M_V = 4096
COL_M_O = 5120
COL_M_G = 6144
PROJ_COLS = 6144

N_TOK = BATCH * SEQ
N_ASSIGN = N_TOK * TOP_K

VMEM_LIMIT = 52 * 1024 * 1024
MOE_VMEM_LIMIT = 58 * 1024 * 1024


def _cparams(n_axes):
    return pltpu.CompilerParams(
        dimension_semantics=("arbitrary",) * n_axes, vmem_limit_bytes=VMEM_LIMIT)


def _rms(v, gain):
    ms = jnp.mean(v * v, axis=-1, keepdims=True)
    return v * lax.rsqrt(ms + EPS) * gain


def _norm_in_kernel(x_ref, meta_ref, g_ref, pad_ref, real_ref):
    j = pl.program_id(1)
    g = g_ref[...]

    @pl.when(j == 0)
    def _():
        pad_ref[0, :PAD, :] = jnp.zeros((PAD, D_MODEL), BF16)
        pad_ref[0, PAD:, :] = _rms(meta_ref[...], g).astype(BF16)

    @pl.when(j > 0)
    def _():
        y = _rms(x_ref[0], g).astype(BF16)
        pad_ref[0] = y
        real_ref[0] = y


def norm_in(x, meta, gain):
    return pl.pallas_call(
        _norm_in_kernel,
        grid=(BATCH, NBLK),
        in_specs=[
            pl.BlockSpec((1, BLOCK, D_MODEL), lambda b, j: (b, jnp.maximum(j - 1, 0), 0)),
            pl.BlockSpec((N_META, D_MODEL), lambda b, j: (0, 0)),
            pl.BlockSpec((1, D_MODEL), lambda b, j: (0, 0)),
        ],
        out_specs=[
            pl.BlockSpec((1, BLOCK, D_MODEL), lambda b, j: (b, j, 0)),
            pl.BlockSpec((1, BLOCK, D_MODEL), lambda b, j: (b, jnp.maximum(j - 1, 0), 0)),
        ],
        out_shape=[
            jax.ShapeDtypeStruct((BATCH, LP, D_MODEL), BF16),
            jax.ShapeDtypeStruct((BATCH, SEQ, D_MODEL), BF16),
        ],
        compiler_params=_cparams(2),
        name="norm_in",
    )(x, meta, gain.reshape(1, D_MODEL))


def _mm_kernel(*refs, has_bias, has_res, act):
    x_ref, w_ref = refs[0], refs[1]
    pos = 2
    b_ref = r_ref = None
    if has_bias:
        b_ref = refs[pos]
        pos += 1
    if has_res:
        r_ref = refs[pos]
        pos += 1
    o_ref, wbf_ref = refs[pos], refs[pos + 1]

    @pl.when(pl.program_id(1) == 0)
    def _():
        wbf_ref[...] = w_ref[...].astype(BF16)

    acc = jnp.dot(x_ref[...], wbf_ref[...], preferred_element_type=F32)
    if has_bias:
        acc = acc + b_ref[...]
    if act == "sigmoid":
        acc = jax.nn.sigmoid(acc)
    if has_res:
        acc = acc + r_ref[...]
    o_ref[...] = acc.astype(o_ref.dtype)


def matmul(x, w, *, n_cols, col_block0=0, tm, tn, bias=None, res=None, act=None,
           out_dtype=BF16, name):
    m, k = x.shape
    in_specs = [
        pl.BlockSpec((tm, k), lambda j, i: (i, 0)),
        pl.BlockSpec((k, tn), lambda j, i: (0, j + col_block0)),
    ]
    args = [x, w]
    if bias is not None:
        in_specs.append(pl.BlockSpec((1, tn), lambda j, i: (0, j)))
        args.append(bias.reshape(1, n_cols))
    if res is not None:
        in_specs.append(pl.BlockSpec((tm, tn), lambda j, i: (i, j)))
        args.append(res)
    return pl.pallas_call(
        functools.partial(_mm_kernel, has_bias=bias is not None, has_res=res is not None, act=act),
        grid=(n_cols // tn, m // tm),
        in_specs=in_specs,
        out_specs=pl.BlockSpec((tm, tn), lambda j, i: (i, j)),
        out_shape=jax.ShapeDtypeStruct((m, n_cols), out_dtype),
        scratch_shapes=[pltpu.VMEM((k, tn), BF16)],
        compiler_params=_cparams(2),
        name=name,
    )(*args)


def _conv_kernel(p_ref, w_ref, o_ref):
    c = pl.program_id(1)
    x = p_ref[0].astype(F32)
    w = w_ref[...]
    half = CONV_W // 2
    acc = w[half:half + 1, :] * x
    for j in range(CONV_W):
        if j != half:
            acc = acc + w[j:j + 1, :] * pltpu.roll(x, (half - j) % LP, axis=0)
    y = acc * jax.nn.sigmoid(acc)
    rows = lax.broadcasted_iota(jnp.int32, (LP, 1), 0)
    y = jnp.where(rows >= PAD, y, 0.0)
    scale = jnp.where(c < 2, DK_M ** -0.5, 1.0).astype(F32)
    o_ref[0] = (y * scale).astype(BF16)


def conv_qk(proj3, conv_w):
    cw = 256
    return pl.pallas_call(
        _conv_kernel,
        grid=(BATCH, (2 * H_M * DK_M) // cw),
        in_specs=[
            pl.BlockSpec((1, LP, cw), lambda b, c: (b, 0, COL_M_Q // cw + c)),
            pl.BlockSpec((CONV_W, cw), lambda b, c: (0, c)),
        ],
        out_specs=pl.BlockSpec((1, LP, cw), lambda b, c: (b, 0, c)),
        out_shape=jax.ShapeDtypeStruct((BATCH, LP, 2 * H_M * DK_M), BF16),
        compiler_params=_cparams(2),
        name="conv_qk",
    )(proj3, conv_w)


def _split_dot(tri, v):
    hi = v.astype(BF16)
    r1 = v - hi.astype(F32)
    mid = r1.astype(BF16)
    lo = (r1 - mid.astype(F32)).astype(BF16)
    return (jnp.dot(tri, hi, preferred_element_type=F32)
            + jnp.dot(tri, mid, preferred_element_type=F32)
            + jnp.dot(tri, lo, preferred_element_type=F32))


def _gate_kernel(g_ref, bias_ref, o_ref):
    ti = lax.broadcasted_iota(jnp.int32, (BLOCK, BLOCK), 0)
    ui = lax.broadcasted_iota(jnp.int32, (BLOCK, BLOCK), 1)
    tril = jnp.where(ui <= ti, 1.0, 0.0).astype(BF16)
    triu = jnp.where(ui >= ti, 1.0, 0.0).astype(BF16)
    ch = lax.broadcasted_iota(jnp.int32, (BLOCK, 4 * H_M), 1)
    typ = lax.shift_right_logical(ch, 2)
    rloc = lax.broadcasted_iota(jnp.int32, (BLOCK, 4 * H_M), 0)

    def body(c, carry):
        r0 = pl.multiple_of(c * BLOCK, BLOCK)
        g = g_ref[0, pl.ds(r0, BLOCK), :][:, :4 * H_M] + bias_ref[...]
        valid = (rloc + r0) >= PAD
        lsig = -(jnp.maximum(-g, 0.0) + jnp.log1p(jnp.exp(-jnp.abs(g))))
        lf = jnp.where(valid, lsig, 0.0)
        cum = _split_dot(tril, lf)
        rcum = _split_dot(triu, lf)
        li = jnp.where(valid, g, -jnp.inf)
        out = jnp.where(typ == 1, cum, jnp.where(typ == 3, rcum, li))
        o_ref[0, pl.ds(r0, BLOCK), :] = out
        return carry

    lax.fori_loop(0, NBLK, body, 0)


def gate_prep(mg3, gate_bias):
    return pl.pallas_call(
        _gate_kernel,
        grid=(BATCH,),
        in_specs=[
            pl.BlockSpec((1, LP, 128), lambda b: (b, 0, 0)),
            pl.BlockSpec((1, 4 * H_M), lambda b: (0, 0)),
        ],
        out_specs=pl.BlockSpec((1, LP, 4 * H_M), lambda b: (b, 0, 0)),
        out_shape=jax.ShapeDtypeStruct((BATCH, LP, 4 * H_M), F32),
        compiler_params=_cparams(1),
        name="gate_prep",
    )(mg3, gate_bias.reshape(1, 4 * H_M))


MLSTM_HP = 2
MLSTM_MID = NBLK // 2


def _mlstm_kernel(q_ref, k_ref, v_ref, o_ref, gc_ref, gr_ref, y_ref,
                  hs_ref, c_ref, n_ref, m_ref):
    c_ref[...] = jnp.zeros_like(c_ref)
    n_ref[...] = jnp.zeros_like(n_ref)
    m_ref[...] = jnp.zeros_like(m_ref)
    ti = lax.broadcasted_iota(jnp.int32, (BLOCK, BLOCK), 0)
    si = lax.broadcasted_iota(jnp.int32, (BLOCK, BLOCK), 1)
    mask_f = si <= ti
    mask_b = si >= ti

    def chain(c, hl, bwd, final):
        idx = 2 * hl + bwd
        r0 = c * BLOCK if isinstance(c, int) else pl.multiple_of(c * BLOCK, BLOCK)
        q = q_ref[0, pl.ds(r0, BLOCK), hl * DK_M:(hl + 1) * DK_M]
        k = k_ref[0, pl.ds(r0, BLOCK), hl * DK_M:(hl + 1) * DK_M]
        v = v_ref[0, pl.ds(r0, BLOCK), hl * DV_M:(hl + 1) * DV_M]
        gc = gc_ref[0, hl, pl.ds(r0, BLOCK), :]
        gr = gr_ref[0, hl, :, pl.ds(r0, BLOCK)]
        a = 2 * bwd
        li_c, b_c = gc[:, a:a + 1], gc[:, a + 1:a + 2]
        li_r, b_r = gr[a:a + 1, :], gr[a + 1:a + 2, :]
        b_end = b_c[0:1, :] if bwd else b_c[BLOCK - 1:BLOCK, :]
        m_prev = m_ref[idx][:, 0:1]
        cst = c_ref[idx]
        nst = n_ref[idx]
        dmat = jnp.where(mask_b if bwd else mask_f, b_c - b_r + li_r, -jnp.inf)
        inter = b_c + m_prev
        m_t = jnp.maximum(inter, jnp.max(dmat, axis=1, keepdims=True))
        w_inter = jnp.exp(inter - m_t)
        qk = lax.dot_general(q, k, (((1,), (1,)), ((), ())), preferred_element_type=F32)
        s = qk * jnp.exp(dmat - m_t)
        num = (w_inter * jnp.dot(q, cst.astype(BF16), preferred_element_type=F32)
               + jnp.dot(s.astype(BF16), v, preferred_element_type=F32))
        den = jnp.sum(s + w_inter * (q.astype(F32) * nst), axis=1, keepdims=True)
        h = num / jnp.maximum(jnp.abs(den), jnp.exp(-m_t))
        cols = slice(hl * DV_M, (hl + 1) * DV_M)
        if final:
            og = jax.nn.sigmoid(o_ref[0, pl.ds(r0, BLOCK), cols].astype(F32))
            y_ref[0, pl.ds(r0 - BLOCK, BLOCK), cols] = (
                og * (hs_ref[pl.ds(r0, BLOCK), cols] + h)).astype(BF16)
        else:
            hs_ref[pl.ds(r0, BLOCK), cols] = h
        ldec_c = b_end - b_c + li_c
        ldec_r = b_end - b_r + li_r
        m_new = jnp.maximum(b_end + m_prev, jnp.max(ldec_r, axis=1, keepdims=True))
        w_c = jnp.exp(b_end + m_prev - m_new)
        w_s = jnp.exp(ldec_c - m_new)
        wv = (w_s * v.astype(F32)).astype(BF16)
        c_ref[idx] = w_c * cst + lax.dot_general(
            k, wv, (((0,), (0,)), ((), ())), preferred_element_type=F32)
        n_ref[idx] = w_c * nst + jnp.sum(w_s * k.astype(F32), axis=0, keepdims=True)
        m_ref[idx] = jnp.broadcast_to(m_new, (1, BLOCK))

    def first_half(i, carry):
        for hl in range(MLSTM_HP):
            chain(i, hl, 0, False)
            chain(NBLK - 1 - i, hl, 1, False)
        return carry

    def second_half(i, carry):
        for hl in range(MLSTM_HP):
            chain(i, hl, 0, True)
            chain(NBLK - 1 - i, hl, 1, True)
        return carry

    lax.fori_loop(0, MLSTM_MID, first_half, 0)
    for hl in range(MLSTM_HP):
        chain(MLSTM_MID, hl, 0, False)
        chain(MLSTM_MID, hl, 1, True)
    lax.fori_loop(MLSTM_MID + 1, NBLK - 1, second_half, 0)
    for hl in range(MLSTM_HP):
        chain(NBLK - 1, hl, 0, True)


def mlstm(qk_m, proj3, gcol, grow):
    hp = MLSTM_HP
    kw, vw = hp * DK_M, hp * DV_M
    return pl.pallas_call(
        _mlstm_kernel,
        grid=(BATCH, H_M // hp),
        in_specs=[
            pl.BlockSpec((1, LP, kw), lambda b, g: (b, 0, g)),
            pl.BlockSpec((1, LP, kw), lambda b, g: (b, 0, (H_M * DK_M) // kw + g)),
            pl.BlockSpec((1, LP, vw), lambda b, g: (b, 0, COL_M_V // vw + g)),
            pl.BlockSpec((1, LP, vw), lambda b, g: (b, 0, COL_M_O // vw + g)),
            pl.BlockSpec((1, hp, LP, 4), lambda b, g: (b, g, 0, 0)),
            pl.BlockSpec((1, hp, 4, LP), lambda b, g: (b, g, 0, 0)),
        ],
        out_specs=pl.BlockSpec((1, SEQ, vw), lambda b, g: (b, 0, g)),
        out_shape=jax.ShapeDtypeStruct((BATCH, SEQ, H_M * DV_M), BF16),
        scratch_shapes=[
            pltpu.VMEM((LP, vw), F32),
            pltpu.VMEM((2 * hp, DK_M, DV_M), F32),
            pltpu.VMEM((2 * hp, 1, DK_M), F32),
            pltpu.VMEM((2 * hp, 1, BLOCK), F32),
        ],
        compiler_params=_cparams(2),
        name="mlstm",
    )(qk_m, qk_m, proj3, proj3, gcol, grow)


ATT_BAND = 3 * BLOCK
ATT_GROUP = 6 * BLOCK
ATT_NGROUP = (LP - ATT_BAND) // ATT_GROUP


def _attn_kernel(c_ref, q_ref, k1_ref, v1_ref, tab_ref, lam_ref, sg_ref, o_ref,
                 s_ref, k_ref, v_ref):
    h = pl.program_id(1)
    qb = pl.program_id(2) + 1

    @pl.when(pl.program_id(2) == 0)
    def _():
        for rep in range(2):
            k_ref[0, rep * LP:(rep + 1) * LP, :] = k1_ref[0]
            v_ref[0, rep * LP:(rep + 1) * LP, :] = v1_ref[0]

    q = q_ref[0]
    scale = DK_DA ** -0.5
    c_neg = c_ref[h, 0]
    c_pos = c_ref[h, 1]
    lp = lam_ref[...]
    lam = (jnp.exp(jnp.sum(lp[0:1] * lp[1:2], axis=1, keepdims=True))
           - jnp.exp(jnp.sum(lp[2:3] * lp[3:4], axis=1, keepdims=True)) + LAMBDA_INIT)

    groups = [((qb - 1) * BLOCK, ATT_BAND, 0)]
    for g in range(ATT_NGROUP):
        groups.append(((qb + 2) * BLOCK + g * ATT_GROUP, ATT_GROUP, ATT_BAND + g * ATT_GROUP))

    def lane_fold(acc, t, op):
        for j in range(t.shape[1] // BLOCK):
            piece = t[:, j * BLOCK:(j + 1) * BLOCK]
            acc = piece if acc is None else op(acc, piece)
        return acc

    mx = [None, None]
    for gi, (koff, width, col0) in enumerate(groups):
        koff = pl.multiple_of(koff, BLOCK)
        if gi == 0:
            bias = tab_ref[0, 0]
        else:
            kpos = koff + lax.broadcasted_iota(jnp.int32, (1, width), 1)
            bias = jnp.where(kpos < LP, c_pos, jnp.where(kpos < LP + PAD, NEG_INF, c_neg))
        for m in range(2):
            kk = k_ref[0, pl.ds(koff, width), m * DK_DA:(m + 1) * DK_DA]
            s = lax.dot_general(q[:, m * DK_DA:(m + 1) * DK_DA], kk, (((1,), (1,)), ((), ())),
                                preferred_element_type=F32) * scale + bias
            s_ref[m, :, col0:col0 + width] = s
            mx[m] = lane_fold(mx[m], s, jnp.maximum)
    row_max = [jnp.max(mx[m], axis=1, keepdims=True) for m in range(2)]

    lsum = [None, None]
    acc = [None, None]
    for koff, width, col0 in groups:
        koff = pl.multiple_of(koff, BLOCK)
        vv = v_ref[0, pl.ds(koff, width), :]
        for m in range(2):
            p = jnp.exp(s_ref[m, :, col0:col0 + width] - row_max[m])
            lsum[m] = lane_fold(lsum[m], p, jnp.add)
            pv = jnp.dot(p.astype(BF16), vv, preferred_element_type=F32)
            acc[m] = pv if acc[m] is None else acc[m] + pv
    l1 = jnp.sum(lsum[0], axis=1, keepdims=True)
    l2 = jnp.sum(lsum[1], axis=1, keepdims=True)
    o = acc[0] / l1 - lam * (acc[1] / l2)
    o_ref[0] = (_rms(o, sg_ref[...]) * (1.0 - LAMBDA_INIT)).astype(BF16)


def diff_attention(consts, proj3, tab, lam_params, subln):
    nq = SEQ // BLOCK
    kblk0 = COL_DA_K // (2 * DK_DA)
    vblk0 = COL_DA_V // DV_DA

    def tab_map(b, h, i):
        case = jnp.where(i == 0, 0, jnp.where(i == nq - 1, 2, 1))
        return (h, case, 0, 0)

    return pl.pallas_call(
        _attn_kernel,
        grid=(BATCH, H_DA, nq),
        in_specs=[
            pl.BlockSpec(memory_space=pltpu.SMEM),
            pl.BlockSpec((1, BLOCK, 2 * DK_DA), lambda b, h, i: (b, i + 1, h)),
            pl.BlockSpec((1, LP, 2 * DK_DA), lambda b, h, i: (b, 0, kblk0 + h)),
            pl.BlockSpec((1, LP, DV_DA), lambda b, h, i: (b, 0, vblk0 + h)),
            pl.BlockSpec((1, 1, BLOCK, ATT_BAND), tab_map),
            pl.BlockSpec((4, DK_DA), lambda b, h, i: (0, 0)),
            pl.BlockSpec((1, DV_DA), lambda b, h, i: (0, 0)),
        ],
        out_specs=pl.BlockSpec((1, BLOCK, DV_DA), lambda b, h, i: (b, i, h)),
        out_shape=jax.ShapeDtypeStruct((BATCH, SEQ, H_DA * DV_DA), BF16),
        scratch_shapes=[pltpu.VMEM((2, BLOCK, LP), F32),
                        pltpu.VMEM((1, 2 * LP, 2 * DK_DA), BF16),
                        pltpu.VMEM((1, 2 * LP, DV_DA), BF16)],
        compiler_params=_cparams(3),
        name="diff_attn",
    )(consts, proj3, proj3, proj3, tab, lam_params, subln.reshape(1, DV_DA))


def _bias_tables(rel_bias):
    rb = rel_bias.astype(F32)
    i = jnp.arange(BLOCK, dtype=jnp.int32)[:, None]
    j = jnp.arange(ATT_BAND, dtype=jnp.int32)[None, :]
    rel = j - BLOCK - i
    nb = N_BUCKETS // 2
    max_exact = nb // 2
    n = jnp.abs(rel)
    nf = jnp.maximum(n, 1).astype(F32)
    large = max_exact + (jnp.log(nf / max_exact) / math.log(MAX_DISTANCE / max_exact)
                         * (nb - max_exact)).astype(jnp.int32)
    large = jnp.minimum(large, nb - 1)
    bucket = jnp.where(rel > 0, nb, 0) + jnp.where(n < max_exact, n, large)
    hit = bucket[None, :, :, None] == jnp.arange(N_BUCKETS, dtype=jnp.int32)
    gen = jnp.sum(jnp.where(hit, rb.T[:, None, None, :], 0.0), axis=-1)
    c_neg = rb[nb - 1]
    c_pos = rb[N_BUCKETS - 1]
    jj = j[None]
    first = jnp.where(jj < PAD, NEG_INF, gen)
    wrapped = jnp.where(jj - 2 * BLOCK < PAD, NEG_INF, c_neg[:, None, None])
    last = jnp.where(jj >= 2 * BLOCK, wrapped, gen)
    tab = jnp.stack([first, gen, last], axis=1)
    consts = jnp.stack([c_neg, c_pos], axis=1)
    return tab, consts


def _mix_kernel(ya_ref, ym_ref, wa_ref, wm_ref, ga_ref, gm_ref, o_ref, wa_bf, wm_bf):
    @pl.when(pl.program_id(1) == 0)
    def _():
        wa_bf[...] = wa_ref[...].astype(BF16)
        wm_bf[...] = wm_ref[...].astype(BF16)

    a = jnp.dot(ya_ref[...], wa_bf[...], preferred_element_type=F32)
    m = jnp.dot(ym_ref[...], wm_bf[...], preferred_element_type=F32)
    o_ref[...] = (ga_ref[...].astype(F32) * a + gm_ref[...].astype(F32) * m).astype(BF16)


def branch_mix(y_da, y_m, w_da, w_m, gate, *, tm=512, tn=1024):
    m, k = y_da.shape
    nj = D_MODEL // tn
    return pl.pallas_call(
        _mix_kernel,
        grid=(nj, m // tm),
        in_specs=[
            pl.BlockSpec((tm, k), lambda j, i: (i, 0)),
            pl.BlockSpec((tm, k), lambda j, i: (i, 0)),
            pl.BlockSpec((k, tn), lambda j, i: (0, j)),
            pl.BlockSpec((k, tn), lambda j, i: (0, j)),
            pl.BlockSpec((tm, tn), lambda j, i: (i, j)),
            pl.BlockSpec((tm, tn), lambda j, i: (i, nj + j)),
        ],
        out_specs=pl.BlockSpec((tm, tn), lambda j, i: (i, j)),
        out_shape=jax.ShapeDtypeStruct((m, D_MODEL), BF16),
        scratch_shapes=[pltpu.VMEM((k, tn), BF16), pltpu.VMEM((k, tn), BF16)],
        compiler_params=_cparams(2),
        name="branch_mix",
    )(y_da, y_m, w_da, w_m, gate, gate)


FFN_TM = 1024
HALF = D_MODEL // 2


def _pack_bf16_pairs(v):
    lo = lax.bitcast_convert_type(v[:, :HALF].astype(BF16).astype(F32), jnp.uint32)
    hi = lax.bitcast_convert_type(v[:, HALF:].astype(BF16).astype(F32), jnp.uint32)
    return (hi & jnp.uint32(0xFFFF0000)) | lax.shift_right_logical(lo, jnp.uint32(16))


def _unpack_bf16_pairs(w):
    lo = lax.bitcast_convert_type(lax.shift_left(w, jnp.uint32(16)), F32).astype(BF16)
    hi = lax.bitcast_convert_type(w & jnp.uint32(0xFFFF0000), F32).astype(BF16)
    return lo, hi


def _ffn_prep_kernel(h_ref, g_ref, wr_ref, br_ref, hn_ref, e_ref, w_ref, r_ref, cnt_ref, base_ref):
    @pl.when(pl.program_id(0) == 0)
    def _():
        base_ref[...] = jnp.zeros_like(base_ref)

    hn = _rms(h_ref[...], g_ref[...])
    hn_ref[...] = _pack_bf16_pairs(hn)
    logits = jnp.dot(hn, wr_ref[...], preferred_element_type=F32,
                     precision=lax.Precision.HIGHEST) + br_ref[...]
    lane = lax.broadcasted_iota(jnp.int32, (FFN_TM, N_EXPERTS), 1)
    lane_o = lax.broadcasted_iota(jnp.int32, (FFN_TM, 128), 1)
    ti = lax.broadcasted_iota(jnp.int32, (FFN_TM, FFN_TM), 0)
    ui = lax.broadcasted_iota(jnp.int32, (FFN_TM, FFN_TM), 1)
    tril = jnp.where(ui <= ti, 1.0, 0.0).astype(BF16)
    e_out = jnp.zeros((FFN_TM, 128), jnp.int32)
    r_out = jnp.zeros((FFN_TM, 128), jnp.int32)
    l_out = jnp.full((FFN_TM, 128), -jnp.inf, F32)
    base = base_ref[...]
    l = logits
    for kk in range(TOP_K):
        mk = jnp.max(l, axis=1, keepdims=True)
        ik = jnp.min(jnp.where(l == mk, lane, N_EXPERTS), axis=1, keepdims=True)
        hit = lane == ik
        oh = jnp.where(hit, 1.0, 0.0)
        cum = jnp.dot(tril, oh.astype(BF16), preferred_element_type=F32)
        rank = jnp.sum(oh * (cum + base), axis=1, keepdims=True) - 1.0
        base = base + jnp.sum(oh, axis=0, keepdims=True)
        e_out = jnp.where(lane_o == kk, ik, e_out)
        r_out = jnp.where(lane_o == kk, rank.astype(jnp.int32), r_out)
        l_out = jnp.where(lane_o == kk, mk, l_out)
        l = jnp.where(hit, -jnp.inf, l)
    base_ref[...] = base
    cnt_ref[...] = base
    ex = jnp.exp(l_out - jnp.max(l_out, axis=1, keepdims=True))
    e_ref[...] = e_out
    r_ref[...] = r_out
    w_ref[...] = ex / jnp.sum(ex, axis=1, keepdims=True)


def ffn_prep(h2, gain, w_router, b_router):
    row = lambda i: (i, 0)
    fixed = lambda i: (0, 0)
    return pl.pallas_call(
        _ffn_prep_kernel,
        grid=(N_TOK // FFN_TM,),
        in_specs=[
            pl.BlockSpec((FFN_TM, D_MODEL), row),
            pl.BlockSpec((1, D_MODEL), fixed),
            pl.BlockSpec((D_MODEL, N_EXPERTS), fixed),
            pl.BlockSpec((1, N_EXPERTS), fixed),
        ],
        out_specs=[
            pl.BlockSpec((FFN_TM, HALF), row),
            pl.BlockSpec((FFN_TM, 128), row),
            pl.BlockSpec((FFN_TM, 128), row),
            pl.BlockSpec((FFN_TM, 128), row),
            pl.BlockSpec((1, N_EXPERTS), fixed),
        ],
        out_shape=[
            jax.ShapeDtypeStruct((N_TOK, HALF), jnp.uint32),
            jax.ShapeDtypeStruct((N_TOK, 128), jnp.int32),
            jax.ShapeDtypeStruct((N_TOK, 128), F32),
            jax.ShapeDtypeStruct((N_TOK, 128), jnp.int32),
            jax.ShapeDtypeStruct((1, N_EXPERTS), F32),
        ],
        scratch_shapes=[pltpu.VMEM((1, N_EXPERTS), F32)],
        compiler_params=_cparams(1),
        name="ffn_prep",
    )(h2, gain.reshape(1, D_MODEL), w_router, b_router.reshape(1, N_EXPERTS))


SEG_ALIGN = 128
MOE_R = N_ASSIGN + N_EXPERTS * SEG_ALIGN


def _plan(counts_f, top_e, rank):
    counts = counts_f[0].astype(jnp.int32)
    seg_rows = (counts + SEG_ALIGN - 1) // SEG_ALIGN * SEG_ALIGN
    seg_start = jnp.cumsum(seg_rows) - seg_rows
    eq = top_e[:, :, None] == jnp.arange(N_EXPERTS, dtype=jnp.int32)[None, None, :]
    dest = jnp.sum(jnp.where(eq, seg_start[None, None, :], 0), axis=-1) + rank
    used = jnp.sum(seg_rows)
    slack = jnp.stack([used, (MOE_R - used) // SEG_ALIGN])
    return (dest.reshape(N_ASSIGN).astype(jnp.int32), seg_start.astype(jnp.int32),
            seg_rows.astype(jnp.int32), (seg_start + counts).astype(jnp.int32),
            (seg_rows - counts).astype(jnp.int32), slack.astype(jnp.int32))


def _zero_slack(slack_ref, zero_block, dst_rows, sem):
    zero_block[...] = jnp.zeros_like(zero_block)

    def copy(j):
        r0 = pl.multiple_of(slack_ref[0] + j * SEG_ALIGN, SEG_ALIGN)
        return pltpu.make_async_copy(zero_block, dst_rows(pl.ds(r0, SEG_ALIGN)), sem)

    def start(j, carry):
        copy(j).start()
        return carry

    def wait(j, carry):
        copy(j).wait()
        return carry

    lax.fori_loop(0, slack_ref[1], start, 0)
    lax.fori_loop(0, slack_ref[1], wait, 0)


DISP_TOK = 256


def _dispatch_kernel(dest_ref, pad0_ref, padn_ref, slack_ref, hn_ref, xs_ref,
                     zrow_ref, zblk_ref, sem, zsem):
    i = pl.program_id(0)

    @pl.when(i == 0)
    def _():
        zrow_ref[...] = jnp.zeros_like(zrow_ref)
        _zero_slack(slack_ref, zblk_ref, lambda rows: xs_ref.at[rows], zsem.at[0])

        def expert(e, carry):
            p0 = pad0_ref[e]
            pn = padn_ref[e]

            def zstart(r, c2):
                pltpu.make_async_copy(zrow_ref, xs_ref.at[pl.ds(p0 + r, 1)], zsem.at[0]).start()
                return c2

            def zwait(r, c2):
                pltpu.make_async_copy(zrow_ref, xs_ref.at[pl.ds(p0, 1)], zsem.at[0]).wait()
                return c2

            lax.fori_loop(0, pn, zstart, 0)
            lax.fori_loop(0, pn, zwait, 0)
            return carry

        lax.fori_loop(0, N_EXPERTS, expert, 0)

    def tok(t, carry):
        a = (i * DISP_TOK + t) * TOP_K
        for kk in range(TOP_K):
            pltpu.make_async_copy(hn_ref.at[pl.ds(t, 1)], xs_ref.at[pl.ds(dest_ref[a + kk], 1)],
                                  sem.at[0]).start(priority=kk % 2)
        return carry

    lax.fori_loop(0, DISP_TOK, tok, 0, unroll=4)
    for kk in range(TOP_K):
        pltpu.make_async_copy(hn_ref, xs_ref.at[pl.ds(0, DISP_TOK)], sem.at[0]).wait()


def dispatch(dest, pad0, padn, slack, hn_packed):
    grid_spec = pltpu.PrefetchScalarGridSpec(
        num_scalar_prefetch=4,
        grid=(N_TOK // DISP_TOK,),
        in_specs=[pl.BlockSpec((DISP_TOK, HALF), lambda i, d, p0, pn, z: (i, 0))],
        out_specs=pl.BlockSpec(memory_space=pl.ANY),
        scratch_shapes=[pltpu.VMEM((1, HALF), jnp.uint32),
                        pltpu.VMEM((SEG_ALIGN, HALF), jnp.uint32),
                        pltpu.SemaphoreType.DMA((1,)), pltpu.SemaphoreType.DMA((1,))],
    )
    return pl.pallas_call(
        _dispatch_kernel,
        grid_spec=grid_spec,
        out_shape=jax.ShapeDtypeStruct((MOE_R, HALF), jnp.uint32),
        compiler_params=_cparams(1),
        name="moe_dispatch",
    )(dest, pad0, padn, slack, hn_packed)


MOE_CH = 256
MOE_TF = 1024
MOE_NF = D_FF // MOE_TF


class _CopyGroup:
    def __init__(self, copies):
        self.copies = copies

    def start(self, priority=0):
        for cp in self.copies:
            cp.start(priority=priority)

    def wait(self):
        for cp in self.copies:
            cp.wait()


def _stream_rows(start, rows, make_in, make_out, make_tail_in, make_tail_out,
                 compute_chunk, compute_tail, before_first_wait):
    n_ch = lax.shift_right_logical(rows, 8)
    tail = rows - n_ch * MOE_CH
    tail_row = start + n_ch * MOE_CH

    @pl.when(tail > 0)
    def _():
        make_tail_in(tail_row).start(priority=1)

    @pl.when(n_ch > 0)
    def _():
        make_in(0, 0).start(priority=1)

    before_first_wait()

    @pl.when(tail > 0)
    def _():
        make_tail_in(tail_row).wait()
        compute_tail()
        make_tail_out(tail_row).start(priority=1)

    def body(c, carry):
        slot = lax.rem(c, 2)
        make_in(c, slot).wait()

        @pl.when(c + 1 < n_ch)
        def _():
            make_in(c + 1, 1 - slot).start(priority=1)

        @pl.when(c >= 2)
        def _():
            make_out(c - 2, slot).wait()

        compute_chunk(slot)
        make_out(c, slot).start(priority=1)
        return carry

    lax.fori_loop(0, n_ch, body, 0)

    @pl.when(n_ch >= 2)
    def _():
        make_out(n_ch - 2, lax.rem(n_ch, 2)).wait()

    @pl.when(n_ch >= 1)
    def _():
        make_out(n_ch - 1, lax.rem(n_ch + 1, 2)).wait()

    @pl.when(tail > 0)
    def _():
        make_tail_out(tail_row).wait()


def _moe_up_kernel(seg_ref, rows_ref, slack_ref, xs_ref, wg_ref, wl_ref, bg_ref, bl_ref, act_ref,
                   xbuf, obuf, xtail, otail, wg_bf, wl_bf, isem, osem, tsem):
    f = pl.program_id(0)
    e = pl.program_id(1)
    start = seg_ref[e]
    rows = rows_ref[e]

    def chunk_rows(c):
        return pl.ds(pl.multiple_of(start + c * MOE_CH, SEG_ALIGN), MOE_CH)

    def tail_rows(r0):
        return pl.ds(pl.multiple_of(r0, SEG_ALIGN), SEG_ALIGN)

    def make_in(c, slot):
        return pltpu.make_async_copy(xs_ref.at[chunk_rows(c)], xbuf.at[slot], isem.at[slot])

    def make_out(c, slot):
        return pltpu.make_async_copy(obuf.at[slot], act_ref.at[f, chunk_rows(c)], osem.at[slot])

    def make_tail_in(r0):
        return pltpu.make_async_copy(xs_ref.at[tail_rows(r0)], xtail, tsem.at[0])

    def make_tail_out(r0):
        return pltpu.make_async_copy(otail, act_ref.at[f, tail_rows(r0)], tsem.at[1])

    def cast_weights():
        @pl.when(rows > 0)
        def _():
            wg_bf[...] = wg_ref[0].astype(BF16)
            wl_bf[...] = wl_ref[0].astype(BF16)

    def expert_mlp(words):
        lo, hi = _unpack_bf16_pairs(words)
        glu = (jnp.dot(lo, wg_bf[:HALF, :], preferred_element_type=F32)
               + jnp.dot(hi, wg_bf[HALF:, :], preferred_element_type=F32) + bg_ref[0])
        lin = (jnp.dot(lo, wl_bf[:HALF, :], preferred_element_type=F32)
               + jnp.dot(hi, wl_bf[HALF:, :], preferred_element_type=F32) + bl_ref[0])
        glu = jnp.minimum(glu, SWIGLU_LIMIT)
        lin = jnp.clip(lin, -SWIGLU_LIMIT, SWIGLU_LIMIT)
        return (glu * jax.nn.sigmoid(SWIGLU_ALPHA * glu) * (lin + 1.0)).astype(BF16)

    def compute_chunk(slot):
        obuf[slot] = expert_mlp(xbuf[slot])

    def compute_tail():
        otail[...] = expert_mlp(xtail[...])

    _stream_rows(start, rows, make_in, make_out, make_tail_in, make_tail_out,
                 compute_chunk, compute_tail, cast_weights)

    @pl.when(e == N_EXPERTS - 1)
    def _():
        _zero_slack(slack_ref, otail, lambda rr: act_ref.at[f, rr], tsem.at[1])


def moe_up(seg_start, seg_rows, slack, xs, w1, b1):
    grid_spec = pltpu.PrefetchScalarGridSpec(
        num_scalar_prefetch=3,
        grid=(MOE_NF, N_EXPERTS),
        in_specs=[
            pl.BlockSpec(memory_space=pl.ANY),
            pl.BlockSpec((1, D_MODEL, MOE_TF), lambda f, e, s, r, z: (e, 0, f)),
            pl.BlockSpec((1, D_MODEL, MOE_TF), lambda f, e, s, r, z: (e, 0, MOE_NF + f)),
            pl.BlockSpec((1, 1, MOE_TF), lambda f, e, s, r, z: (e, 0, f)),
            pl.BlockSpec((1, 1, MOE_TF), lambda f, e, s, r, z: (e, 0, MOE_NF + f)),
        ],
        out_specs=pl.BlockSpec(memory_space=pl.ANY),
        scratch_shapes=[
            pltpu.VMEM((2, MOE_CH, HALF), jnp.uint32),
            pltpu.VMEM((2, MOE_CH, MOE_TF), BF16),
            pltpu.VMEM((SEG_ALIGN, HALF), jnp.uint32),
            pltpu.VMEM((SEG_ALIGN, MOE_TF), BF16),
            pltpu.VMEM((D_MODEL, MOE_TF), BF16),
            pltpu.VMEM((D_MODEL, MOE_TF), BF16),
            pltpu.SemaphoreType.DMA((2,)),
            pltpu.SemaphoreType.DMA((2,)),
            pltpu.SemaphoreType.DMA((2,)),
        ],
    )
    b13 = b1.reshape(N_EXPERTS, 1, 2 * D_FF)
    return pl.pallas_call(
        _moe_up_kernel,
        grid_spec=grid_spec,
        out_shape=jax.ShapeDtypeStruct((MOE_NF, MOE_R, MOE_TF), BF16),
        compiler_params=pltpu.CompilerParams(
            dimension_semantics=("arbitrary", "arbitrary"), vmem_limit_bytes=MOE_VMEM_LIMIT),
        name="moe_up",
    )(seg_start, seg_rows, slack, xs, w1, w1, b13, b13)


def _moe_down_kernel(seg_ref, rows_ref, slack_ref, act_ref, w_ref, b_ref, y_ref,
                     xbuf, obuf, xtail, otail, w_bf, isem, osem, tsem):
    e = pl.program_id(0)
    start = seg_ref[e]
    rows = rows_ref[e]

    def chunk_rows(c):
        return pl.ds(pl.multiple_of(start + c * MOE_CH, SEG_ALIGN), MOE_CH)

    def tail_rows(r0):
        return pl.ds(pl.multiple_of(r0, SEG_ALIGN), SEG_ALIGN)

    def make_in(c, slot):
        return _CopyGroup([
            pltpu.make_async_copy(act_ref.at[j, chunk_rows(c)],
                                  xbuf.at[slot, :, pl.ds(j * MOE_TF, MOE_TF)], isem.at[slot])
            for j in range(MOE_NF)])

    def make_out(c, slot):
        return pltpu.make_async_copy(obuf.at[slot], y_ref.at[chunk_rows(c)], osem.at[slot])

    def make_tail_in(r0):
        return _CopyGroup([
            pltpu.make_async_copy(act_ref.at[j, tail_rows(r0)],
                                  xtail.at[:, pl.ds(j * MOE_TF, MOE_TF)], tsem.at[0])
            for j in range(MOE_NF)])

    def make_tail_out(r0):
        return pltpu.make_async_copy(otail, y_ref.at[tail_rows(r0)], tsem.at[1])

    def cast_weights():
        @pl.when(rows > 0)
        def _():
            w_bf[...] = w_ref[0].astype(BF16)

    def compute_chunk(slot):
        obuf[slot] = jnp.dot(xbuf[slot], w_bf[...], preferred_element_type=F32) + b_ref[0]

    def compute_tail():
        otail[...] = jnp.dot(xtail[...], w_bf[...], preferred_element_type=F32) + b_ref[0]

    _stream_rows(start, rows, make_in, make_out, make_tail_in, make_tail_out,
                 compute_chunk, compute_tail, cast_weights)

    @pl.when(e == N_EXPERTS - 1)
    def _():
        _zero_slack(slack_ref, otail, lambda rr: y_ref.at[rr], tsem.at[1])


def moe_down(seg_start, seg_rows, slack, act, w2, b2):
    grid_spec = pltpu.PrefetchScalarGridSpec(
        num_scalar_prefetch=3,
        grid=(N_EXPERTS,),
        in_specs=[
            pl.BlockSpec(memory_space=pl.ANY),
            pl.BlockSpec((1, D_FF, D_MODEL), lambda e, s, r, z: (e, 0, 0)),
            pl.BlockSpec((1, 1, D_MODEL), lambda e, s, r, z: (e, 0, 0)),
        ],
        out_specs=pl.BlockSpec(memory_space=pl.ANY),
        scratch_shapes=[
            pltpu.VMEM((2, MOE_CH, D_FF), BF16),
            pltpu.VMEM((2, MOE_CH, D_MODEL), F32),
            pltpu.VMEM((SEG_ALIGN, D_FF), BF16),
            pltpu.VMEM((SEG_ALIGN, D_MODEL), F32),
            pltpu.VMEM((D_FF, D_MODEL), BF16),
            pltpu.SemaphoreType.DMA((2,)),
            pltpu.SemaphoreType.DMA((2,)),
            pltpu.SemaphoreType.DMA((2,)),
        ],
    )
    return pl.pallas_call(
        _moe_down_kernel,
        grid_spec=grid_spec,
        out_shape=jax.ShapeDtypeStruct((MOE_R, D_MODEL), F32),
        compiler_params=pltpu.CompilerParams(
            dimension_semantics=("arbitrary",), vmem_limit_bytes=MOE_VMEM_LIMIT),
        name="moe_down",
    )(seg_start, seg_rows, slack, act, w2, b2.reshape(N_EXPERTS, 1, D_MODEL))


COMB_TM = 128
COMB_NT = N_TOK // COMB_TM


def _combine_kernel(dest_ref, y_ref, w_ref, h_ref, g_ref, o_ref, buf, sem):
    i = pl.program_id(0)

    def fetch(tile, slot):
        def tok(t, carry):
            a = (tile * COMB_TM + t) * TOP_K
            for kk in range(TOP_K):
                pltpu.make_async_copy(y_ref.at[pl.ds(dest_ref[a + kk], 1)],
                                      buf.at[slot, kk, pl.ds(t, 1)], sem.at[slot]).start(
                                          priority=kk % 2)
            return carry

        lax.fori_loop(0, COMB_TM, tok, 0, unroll=4)

    @pl.when(i == 0)
    def _():
        fetch(0, 0)

    slot = lax.rem(i, 2)

    @pl.when(i + 1 < COMB_NT)
    def _():
        fetch(i + 1, 1 - slot)

    for kk in range(TOP_K):
        pltpu.make_async_copy(y_ref.at[pl.ds(0, COMB_TM)], buf.at[slot, kk], sem.at[slot]).wait()
    w = w_ref[...]
    ff = w[:, 0:1] * buf[slot, 0]
    for kk in range(1, TOP_K):
        ff = ff + w[:, kk:kk + 1] * buf[slot, kk]
    o_ref[...] = _rms(h_ref[...] + ff, g_ref[...])


def combine(dest, y, weight, h2, gain):
    grid_spec = pltpu.PrefetchScalarGridSpec(
        num_scalar_prefetch=1,
        grid=(COMB_NT,),
        in_specs=[
            pl.BlockSpec(memory_space=pl.ANY),
            pl.BlockSpec((COMB_TM, 128), lambda i, d: (i, 0)),
            pl.BlockSpec((COMB_TM, D_MODEL), lambda i, d: (i, 0)),
            pl.BlockSpec((1, D_MODEL), lambda i, d: (0, 0)),
        ],
        out_specs=pl.BlockSpec((COMB_TM, D_MODEL), lambda i, d: (i, 0)),
        scratch_shapes=[pltpu.VMEM((2, TOP_K, COMB_TM, D_MODEL), F32),
                        pltpu.SemaphoreType.DMA((2,))],
    )
    return pl.pallas_call(
        _combine_kernel,
        grid_spec=grid_spec,
        out_shape=jax.ShapeDtypeStruct((N_TOK, D_MODEL), F32),
        compiler_params=_cparams(1),
        name="moe_combine",
    )(dest, y, weight, h2, gain.reshape(1, D_MODEL))


def kernel(x, meta_tokens, rel_bias, norm_mix, w_in, conv_w, gate_bias_m, lambda_params, subln_da,
           w_branch_da, w_branch_m, w_gate, b_gate, w_out, norm_ffn, w_router, b_router,
           w1, b1, w2, b2, norm_final):
    layer = 0
    xn_pad, xn_real = norm_in(x, meta_tokens, norm_mix[layer])
    xn_pad2 = xn_pad.reshape(BATCH * LP, D_MODEL)
    proj = matmul(xn_pad2, w_in[layer], n_cols=PROJ_COLS, tm=768, tn=1024, name="proj_in")
    proj3 = proj.reshape(BATCH, LP, PROJ_COLS)
    w_g = jnp.pad(w_in[layer][:, COL_M_G:], ((0, 0), (0, 128 - 4 * H_M)))
    mg = matmul(xn_pad2, w_g, n_cols=128, tm=768, tn=128, out_dtype=F32, name="proj_gates")
    gate = matmul(xn_real.reshape(N_TOK, D_MODEL), w_gate[layer], n_cols=2 * D_MODEL,
                  tm=1024, tn=1024, bias=b_gate[layer], act="sigmoid", name="mix_gate")

    tab, consts = _bias_tables(rel_bias)
    y_da = diff_attention(consts, proj3, tab, lambda_params[layer], subln_da[layer])

    qk_m = conv_qk(proj3, conv_w[layer])
    gp = gate_prep(mg.reshape(BATCH, LP, 128), gate_bias_m[layer])
    gp4 = gp.reshape(BATCH, LP, 4, H_M)
    gcol = jnp.transpose(gp4, (0, 3, 1, 2))
    grow = jnp.transpose(gp4, (0, 3, 2, 1))
    y_m = mlstm(qk_m, proj3, gcol, grow)

    mixed = branch_mix(y_da.reshape(N_TOK, H_DA * DV_DA), y_m.reshape(N_TOK, H_M * DV_M),
                       w_branch_da[layer], w_branch_m[layer], gate)
    h2 = matmul(mixed, w_out[layer], n_cols=D_MODEL, tm=1024, tn=1024,
                res=x.reshape(N_TOK, D_MODEL), out_dtype=F32, name="out_proj")

    hn_packed, top_e, weight, rank, counts = ffn_prep(
        h2, norm_ffn[layer], w_router[layer], b_router[layer])
    dest, seg_start, seg_rows, pad0, padn, slack = _plan(
        counts, top_e[:, :TOP_K], rank[:, :TOP_K])
    xs = dispatch(dest, pad0, padn, slack, hn_packed)
    act = moe_up(seg_start, seg_rows, slack, xs, w1[layer], b1[layer])
    y = moe_down(seg_start, seg_rows, slack, act, w2[layer], b2[layer])
    out = combine(dest, y, weight, h2, norm_final)
    return out.reshape(BATCH, SEQ, D_MODEL)
```

```python
import functools
import math

import jax
import jax.numpy as jnp
from jax import lax
from jax.experimental import pallas as pl
from jax.experimental.pallas import tpu as pltpu

F32 = jnp.float32
BF16 = jnp.bfloat16

D_MODEL = 2048
BATCH = 2
SEQ = 4096
N_META = 16
BLOCK = 128
PAD = (-N_META) % BLOCK
LP = PAD + N_META + SEQ
NBLK = LP // BLOCK
EPS = 1e-6
NEG_INF = -1e30

H_DA = 4
DK_DA = 128
DV_DA = 256
H_M = 4
DK_M = 128
DV_M = 256
CONV_W = 5
N_BUCKETS = 32
MAX_DISTANCE = 128
N_EXPERTS = 32
TOP_K = 4
D_FF = 2048
SWIGLU_ALPHA = 1.702
SWIGLU_LIMIT = 7.0
LAMBDA_INIT = 0.8 - 0.6 * math.exp(-0.3 * 0)

COL_DA_Q = 0
COL_DA_K = 1024
COL_DA_V = 2048
COL_M_Q = 3072
COL_M_K = 3584
COL_M_V = 4096
COL_M_O = 5120
COL_M_G = 6144
PROJ_COLS = 6144

N_TOK = BATCH * SEQ
N_ASSIGN = N_TOK * TOP_K

VMEM_LIMIT = 52 * 1024 * 1024
MOE_VMEM_LIMIT = 58 * 1024 * 1024


def _cparams(n_axes):
    return pltpu.CompilerParams(
        dimension_semantics=("arbitrary",) * n_axes, vmem_limit_bytes=VMEM_LIMIT)


def _rms(v, gain):
    ms = jnp.mean(v * v, axis=-1, keepdims=True)
    return v * lax.rsqrt(ms + EPS) * gain


def _norm_in_kernel(x_ref, meta_ref, g_ref, pad_ref, real_ref):
    j = pl.program_id(1)
    g = g_ref[...]

    @pl.when(j == 0)
    def _():
        pad_ref[0, :PAD, :] = jnp.zeros((PAD, D_MODEL), BF16)
        pad_ref[0, PAD:, :] = _rms(meta_ref[...], g).astype(BF16)

    @pl.when(j > 0)
    def _():
        y = _rms(x_ref[0], g).astype(BF16)
        pad_ref[0] = y
        real_ref[0] = y


def norm_in(x, meta, gain):
    return pl.pallas_call(
        _norm_in_kernel,
        grid=(BATCH, NBLK),
        in_specs=[
            pl.BlockSpec((1, BLOCK, D_MODEL), lambda b, j: (b, jnp.maximum(j - 1, 0), 0)),
            pl.BlockSpec((N_META, D_MODEL), lambda b, j: (0, 0)),
            pl.BlockSpec((1, D_MODEL), lambda b, j: (0, 0)),
        ],
        out_specs=[
            pl.BlockSpec((1, BLOCK, D_MODEL), lambda b, j: (b, j, 0)),
            pl.BlockSpec((1, BLOCK, D_MODEL), lambda b, j: (b, jnp.maximum(j - 1, 0), 0)),
        ],
        out_shape=[
            jax.ShapeDtypeStruct((BATCH, LP, D_MODEL), BF16),
            jax.ShapeDtypeStruct((BATCH, SEQ, D_MODEL), BF16),
        ],
        compiler_params=_cparams(2),
        name="norm_in",
    )(x, meta, gain.reshape(1, D_MODEL))


def _mm_kernel(*refs, has_bias, has_res, act):
    x_ref, w_ref = refs[0], refs[1]
    pos = 2
    b_ref = r_ref = None
    if has_bias:
        b_ref = refs[pos]
        pos += 1
    if has_res:
        r_ref = refs[pos]
        pos += 1
    o_ref, wbf_ref = refs[pos], refs[pos + 1]

    @pl.when(pl.program_id(1) == 0)
    def _():
        wbf_ref[...] = w_ref[...].astype(BF16)

    acc = jnp.dot(x_ref[...], wbf_ref[...], preferred_element_type=F32)
    if has_bias:
        acc = acc + b_ref[...]
    if act == "sigmoid":
        acc = jax.nn.sigmoid(acc)
    if has_res:
        acc = acc + r_ref[...]
    o_ref[...] = acc.astype(o_ref.dtype)


def matmul(x, w, *, n_cols, col_block0=0, tm, tn, bias=None, res=None, act=None,
           out_dtype=BF16, name):
    m, k = x.shape
    in_specs = [
        pl.BlockSpec((tm, k), lambda j, i: (i, 0)),
        pl.BlockSpec((k, tn), lambda j, i: (0, j + col_block0)),
    ]
    args = [x, w]
    if bias is not None:
        in_specs.append(pl.BlockSpec((1, tn), lambda j, i: (0, j)))
        args.append(bias.reshape(1, n_cols))
    if res is not None:
        in_specs.append(pl.BlockSpec((tm, tn), lambda j, i: (i, j)))
        args.append(res)
    return pl.pallas_call(
        functools.partial(_mm_kernel, has_bias=bias is not None, has_res=res is not None, act=act),
        grid=(n_cols // tn, m // tm),
        in_specs=in_specs,
        out_specs=pl.BlockSpec((tm, tn), lambda j, i: (i, j)),
        out_shape=jax.ShapeDtypeStruct((m, n_cols), out_dtype),
        scratch_shapes=[pltpu.VMEM((k, tn), BF16)],
        compiler_params=_cparams(2),
        name=name,
    )(*args)


def _conv_kernel(p_ref, w_ref, o_ref):
    c = pl.program_id(1)
    x = p_ref[0].astype(F32)
    w = w_ref[...]
    half = CONV_W // 2
    acc = w[half:half + 1, :] * x
    for j in range(CONV_W):
        if j != half:
            acc = acc + w[j:j + 1, :] * pltpu.roll(x, (half - j) % LP, axis=0)
    y = acc * jax.nn.sigmoid(acc)
    rows = lax.broadcasted_iota(jnp.int32, (LP, 1), 0)
    y = jnp.where(rows >= PAD, y, 0.0)
    scale = jnp.where(c < 2, DK_M ** -0.5, 1.0).astype(F32)
    o_ref[0] = (y * scale).astype(BF16)


def conv_qk(proj3, conv_w):
    cw = 256
    return pl.pallas_call(
        _conv_kernel,
        grid=(BATCH, (2 * H_M * DK_M) // cw),
        in_specs=[
            pl.BlockSpec((1, LP, cw), lambda b, c: (b, 0, COL_M_Q // cw + c)),
            pl.BlockSpec((CONV_W, cw), lambda b, c: (0, c)),
        ],
        out_specs=pl.BlockSpec((1, LP, cw), lambda b, c: (b, 0, c)),
        out_shape=jax.ShapeDtypeStruct((BATCH, LP, 2 * H_M * DK_M), BF16),
        compiler_params=_cparams(2),
        name="conv_qk",
    )(proj3, conv_w)


def _split_dot(tri, v):
    hi = v.astype(BF16)
    r1 = v - hi.astype(F32)
    mid = r1.astype(BF16)
    lo = (r1 - mid.astype(F32)).astype(BF16)
    return (jnp.dot(tri, hi, preferred_element_type=F32)
            + jnp.dot(tri, mid, preferred_element_type=F32)
            + jnp.dot(tri, lo, preferred_element_type=F32))


def _gate_kernel(g_ref, bias_ref, o_ref):
    ti = lax.broadcasted_iota(jnp.int32, (BLOCK, BLOCK), 0)
    ui = lax.broadcasted_iota(jnp.int32, (BLOCK, BLOCK), 1)
    tril = jnp.where(ui <= ti, 1.0, 0.0).astype(BF16)
    triu = jnp.where(ui >= ti, 1.0, 0.0).astype(BF16)
    ch = lax.broadcasted_iota(jnp.int32, (BLOCK, 4 * H_M), 1)
    typ = lax.shift_right_logical(ch, 2)
    rloc = lax.broadcasted_iota(jnp.int32, (BLOCK, 4 * H_M), 0)

    def body(c, carry):
        r0 = pl.multiple_of(c * BLOCK, BLOCK)
        g = g_ref[0, pl.ds(r0, BLOCK), :][:, :4 * H_M] + bias_ref[...]
        valid = (rloc + r0) >= PAD
        lsig = -(jnp.maximum(-g, 0.0) + jnp.log1p(jnp.exp(-jnp.abs(g))))
        lf = jnp.where(valid, lsig, 0.0)
        cum = _split_dot(tril, lf)
        rcum = _split_dot(triu, lf)
        li = jnp.where(valid, g, -jnp.inf)
        out = jnp.where(typ == 1, cum, jnp.where(typ == 3, rcum, li))
        o_ref[0, pl.ds(r0, BLOCK), :] = out
        return carry

    lax.fori_loop(0, NBLK, body, 0)


def gate_prep(mg3, gate_bias):
    return pl.pallas_call(
        _gate_kernel,
        grid=(BATCH,),
        in_specs=[
            pl.BlockSpec((1, LP, 128), lambda b: (b, 0, 0)),
            pl.BlockSpec((1, 4 * H_M), lambda b: (0, 0)),
        ],
        out_specs=pl.BlockSpec((1, LP, 4 * H_M), lambda b: (b, 0, 0)),
        out_shape=jax.ShapeDtypeStruct((BATCH, LP, 4 * H_M), F32),
        compiler_params=_cparams(1),
        name="gate_prep",
    )(mg3, gate_bias.reshape(1, 4 * H_M))


MLSTM_HP = 2
MLSTM_MID = NBLK // 2


def _mlstm_kernel(q_ref, k_ref, v_ref, o_ref, gc_ref, gr_ref, y_ref,
                  hs_ref, c_ref, n_ref, m_ref):
    c_ref[...] = jnp.zeros_like(c_ref)
    n_ref[...] = jnp.zeros_like(n_ref)
    m_ref[...] = jnp.zeros_like(m_ref)
    ti = lax.broadcasted_iota(jnp.int32, (BLOCK, BLOCK), 0)
    si = lax.broadcasted_iota(jnp.int32, (BLOCK, BLOCK), 1)
    mask_f = si <= ti
    mask_b = si >= ti

    def chain(c, hl, bwd, final):
        idx = 2 * hl + bwd
        r0 = c * BLOCK if isinstance(c, int) else pl.multiple_of(c * BLOCK, BLOCK)
        q = q_ref[0, pl.ds(r0, BLOCK), hl * DK_M:(hl + 1) * DK_M]
        k = k_ref[0, pl.ds(r0, BLOCK), hl * DK_M:(hl + 1) * DK_M]
        v = v_ref[0, pl.ds(r0, BLOCK), hl * DV_M:(hl + 1) * DV_M]
        gc = gc_ref[0, hl, pl.ds(r0, BLOCK), :]
        gr = gr_ref[0, hl, :, pl.ds(r0, BLOCK)]
        a = 2 * bwd
        li_c, b_c = gc[:, a:a + 1], gc[:, a + 1:a + 2]
        li_r, b_r = gr[a:a + 1, :], gr[a + 1:a + 2, :]
        b_end = b_c[0:1, :] if bwd else b_c[BLOCK - 1:BLOCK, :]
        m_prev = m_ref[idx][:, 0:1]
        cst = c_ref[idx]
        nst = n_ref[idx]
        dmat = jnp.where(mask_b if bwd else mask_f, b_c - b_r + li_r, -jnp.inf)
        inter = b_c + m_prev
        m_t = jnp.maximum(inter, jnp.max(dmat, axis=1, keepdims=True))
        w_inter = jnp.exp(inter - m_t)
        qk = lax.dot_general(q, k, (((1,), (1,)), ((), ())), preferred_element_type=F32)
        s = qk * jnp.exp(dmat - m_t)
        num = (w_inter * jnp.dot(q, cst.astype(BF16), preferred_element_type=F32)
               + jnp.dot(s.astype(BF16), v, preferred_element_type=F32))
        den = jnp.sum(s + w_inter * (q.astype(F32) * nst), axis=1, keepdims=True)
        h = num / jnp.maximum(jnp.abs(den), jnp.exp(-m_t))
        cols = slice(hl * DV_M, (hl + 1) * DV_M)
        if final:
            og = jax.nn.sigmoid(o_ref[0, pl.ds(r0, BLOCK), cols].astype(F32))
            y_ref[0, pl.ds(r0 - BLOCK, BLOCK), cols] = (
                og * (hs_ref[pl.ds(r0, BLOCK), cols] + h)).astype(BF16)
        else:
            hs_ref[pl.ds(r0, BLOCK), cols] = h
        ldec_c = b_end - b_c + li_c
        ldec_r = b_end - b_r + li_r
        m_new = jnp.maximum(b_end + m_prev, jnp.max(ldec_r, axis=1, keepdims=True))
        w_c = jnp.exp(b_end + m_prev - m_new)
        w_s = jnp.exp(ldec_c - m_new)
        wv = (w_s * v.astype(F32)).astype(BF16)
        c_ref[idx] = w_c * cst + lax.dot_general(
            k, wv, (((0,), (0,)), ((), ())), preferred_element_type=F32)
        n_ref[idx] = w_c * nst + jnp.sum(w_s * k.astype(F32), axis=0, keepdims=True)
        m_ref[idx] = jnp.broadcast_to(m_new, (1, BLOCK))

    def first_half(i, carry):
        for hl in range(MLSTM_HP):
            chain(i, hl, 0, False)
            chain(NBLK - 1 - i, hl, 1, False)
        return carry

    def second_half(i, carry):
        for hl in range(MLSTM_HP):
            chain(i, hl, 0, True)
            chain(NBLK - 1 - i, hl, 1, True)
        return carry

    lax.fori_loop(0, MLSTM_MID, first_half, 0)
    for hl in range(MLSTM_HP):
        chain(MLSTM_MID, hl, 0, False)
        chain(MLSTM_MID, hl, 1, True)
    lax.fori_loop(MLSTM_MID + 1, NBLK - 1, second_half, 0)
    for hl in range(MLSTM_HP):
        chain(NBLK - 1, hl, 0, True)


def mlstm(qk_m, proj3, gcol, grow):
    hp = MLSTM_HP
    kw, vw = hp * DK_M, hp * DV_M
    return pl.pallas_call(
        _mlstm_kernel,
        grid=(BATCH, H_M // hp),
        in_specs=[
            pl.BlockSpec((1, LP, kw), lambda b, g: (b, 0, g)),
            pl.BlockSpec((1, LP, kw), lambda b, g: (b, 0, (H_M * DK_M) // kw + g)),
            pl.BlockSpec((1, LP, vw), lambda b, g: (b, 0, COL_M_V // vw + g)),
            pl.BlockSpec((1, LP, vw), lambda b, g: (b, 0, COL_M_O // vw + g)),
            pl.BlockSpec((1, hp, LP, 4), lambda b, g: (b, g, 0, 0)),
            pl.BlockSpec((1, hp, 4, LP), lambda b, g: (b, g, 0, 0)),
        ],
        out_specs=pl.BlockSpec((1, SEQ, vw), lambda b, g: (b, 0, g)),
        out_shape=jax.ShapeDtypeStruct((BATCH, SEQ, H_M * DV_M), BF16),
        scratch_shapes=[
            pltpu.VMEM((LP, vw), F32),
            pltpu.VMEM((2 * hp, DK_M, DV_M), F32),
            pltpu.VMEM((2 * hp, 1, DK_M), F32),
            pltpu.VMEM((2 * hp, 1, BLOCK), F32),
        ],
        compiler_params=_cparams(2),
        name="mlstm",
    )(qk_m, qk_m, proj3, proj3, gcol, grow)


ATT_BAND = 3 * BLOCK
ATT_GROUP = 6 * BLOCK
ATT_NGROUP = (LP - ATT_BAND) // ATT_GROUP


def _attn_kernel(c_ref, q_ref, k1_ref, v1_ref, tab_ref, lam_ref, sg_ref, o_ref,
                 s_ref, k_ref, v_ref):
    h = pl.program_id(1)
    qb = pl.program_id(2) + 1

    @pl.when(pl.program_id(2) == 0)
    def _():
        for rep in range(2):
            k_ref[0, rep * LP:(rep + 1) * LP, :] = k1_ref[0]
            v_ref[0, rep * LP:(rep + 1) * LP, :] = v1_ref[0]

    q = q_ref[0]
    scale = DK_DA ** -0.5
    c_neg = c_ref[h, 0]
    c_pos = c_ref[h, 1]
    lp = lam_ref[...]
    lam = (jnp.exp(jnp.sum(lp[0:1] * lp[1:2], axis=1, keepdims=True))
           - jnp.exp(jnp.sum(lp[2:3] * lp[3:4], axis=1, keepdims=True)) + LAMBDA_INIT)

    groups = [((qb - 1) * BLOCK, ATT_BAND, 0)]
    for g in range(ATT_NGROUP):
        groups.append(((qb + 2) * BLOCK + g * ATT_GROUP, ATT_GROUP, ATT_BAND + g * ATT_GROUP))

    def lane_fold(acc, t, op):
        for j in range(t.shape[1] // BLOCK):
            piece = t[:, j * BLOCK:(j + 1) * BLOCK]
            acc = piece if acc is None else op(acc, piece)
        return acc

    mx = [None, None]
    for gi, (koff, width, col0) in enumerate(groups):
        koff = pl.multiple_of(koff, BLOCK)
        if gi == 0:
            bias = tab_ref[0, 0]
        else:
            kpos = koff + lax.broadcasted_iota(jnp.int32, (1, width), 1)
            bias = jnp.where(kpos < LP, c_pos, jnp.where(kpos < LP + PAD, NEG_INF, c_neg))
        for m in range(2):
            kk = k_ref[0, pl.ds(koff, width), m * DK_DA:(m + 1) * DK_DA]
            s = lax.dot_general(q[:, m * DK_DA:(m + 1) * DK_DA], kk, (((1,), (1,)), ((), ())),
                                preferred_element_type=F32) * scale + bias
            s_ref[m, :, col0:col0 + width] = s
            mx[m] = lane_fold(mx[m], s, jnp.maximum)
    row_max = [jnp.max(mx[m], axis=1, keepdims=True) for m in range(2)]

    lsum = [None, None]
    acc = [None, None]
    for koff, width, col0 in groups:
        koff = pl.multiple_of(koff, BLOCK)
        vv = v_ref[0, pl.ds(koff, width), :]
        for m in range(2):
            p = jnp.exp(s_ref[m, :, col0:col0 + width] - row_max[m])
            lsum[m] = lane_fold(lsum[m], p, jnp.add)
            pv = jnp.dot(p.astype(BF16), vv, preferred_element_type=F32)
            acc[m] = pv if acc[m] is None else acc[m] + pv
    l1 = jnp.sum(lsum[0], axis=1, keepdims=True)
    l2 = jnp.sum(lsum[1], axis=1, keepdims=True)
    o = acc[0] / l1 - lam * (acc[1] / l2)
    o_ref[0] = (_rms(o, sg_ref[...]) * (1.0 - LAMBDA_INIT)).astype(BF16)


def diff_attention(consts, proj3, tab, lam_params, subln):
    nq = SEQ // BLOCK
    kblk0 = COL_DA_K // (2 * DK_DA)
    vblk0 = COL_DA_V // DV_DA

    def tab_map(b, h, i):
        case = jnp.where(i == 0, 0, jnp.where(i == nq - 1, 2, 1))
        return (h, case, 0, 0)

    return pl.pallas_call(
        _attn_kernel,
        grid=(BATCH, H_DA, nq),
        in_specs=[
            pl.BlockSpec(memory_space=pltpu.SMEM),
            pl.BlockSpec((1, BLOCK, 2 * DK_DA), lambda b, h, i: (b, i + 1, h)),
            pl.BlockSpec((1, LP, 2 * DK_DA), lambda b, h, i: (b, 0, kblk0 + h)),
            pl.BlockSpec((1, LP, DV_DA), lambda b, h, i: (b, 0, vblk0 + h)),
            pl.BlockSpec((1, 1, BLOCK, ATT_BAND), tab_map),
            pl.BlockSpec((4, DK_DA), lambda b, h, i: (0, 0)),
            pl.BlockSpec((1, DV_DA), lambda b, h, i: (0, 0)),
        ],
        out_specs=pl.BlockSpec((1, BLOCK, DV_DA), lambda b, h, i: (b, i, h)),
        out_shape=jax.ShapeDtypeStruct((BATCH, SEQ, H_DA * DV_DA), BF16),
        scratch_shapes=[pltpu.VMEM((2, BLOCK, LP), F32),
                        pltpu.VMEM((1, 2 * LP, 2 * DK_DA), BF16),
                        pltpu.VMEM((1, 2 * LP, DV_DA), BF16)],
        compiler_params=_cparams(3),
        name="diff_attn",
    )(consts, proj3, proj3, proj3, tab, lam_params, subln.reshape(1, DV_DA))


def _bias_tables(rel_bias):
    rb = rel_bias.astype(F32)
    i = jnp.arange(BLOCK, dtype=jnp.int32)[:, None]
    j = jnp.arange(ATT_BAND, dtype=jnp.int32)[None, :]
    rel = j - BLOCK - i
    nb = N_BUCKETS // 2
    max_exact = nb // 2
    n = jnp.abs(rel)
    nf = jnp.maximum(n, 1).astype(F32)
    large = max_exact + (jnp.log(nf / max_exact) / math.log(MAX_DISTANCE / max_exact)
                         * (nb - max_exact)).astype(jnp.int32)
    large = jnp.minimum(large, nb - 1)
    bucket = jnp.where(rel > 0, nb, 0) + jnp.where(n < max_exact, n, large)
    hit = bucket[None, :, :, None] == jnp.arange(N_BUCKETS, dtype=jnp.int32)
    gen = jnp.sum(jnp.where(hit, rb.T[:, None, None, :], 0.0), axis=-1)
    c_neg = rb[nb - 1]
    c_pos = rb[N_BUCKETS - 1]
    jj = j[None]
    first = jnp.where(jj < PAD, NEG_INF, gen)
    wrapped = jnp.where(jj - 2 * BLOCK < PAD, NEG_INF, c_neg[:, None, None])
    last = jnp.where(jj >= 2 * BLOCK, wrapped, gen)
    tab = jnp.stack([first, gen, last], axis=1)
    consts = jnp.stack([c_neg, c_pos], axis=1)
    return tab, consts


def _mix_kernel(ya_ref, ym_ref, wa_ref, wm_ref, ga_ref, gm_ref, o_ref, wa_bf, wm_bf):
    @pl.when(pl.program_id(1) == 0)
    def _():
        wa_bf[...] = wa_ref[...].astype(BF16)
        wm_bf[...] = wm_ref[...].astype(BF16)

    a = jnp.dot(ya_ref[...], wa_bf[...], preferred_element_type=F32)
    m = jnp.dot(ym_ref[...], wm_bf[...], preferred_element_type=F32)
    o_ref[...] = (ga_ref[...].astype(F32) * a + gm_ref[...].astype(F32) * m).astype(BF16)


def branch_mix(y_da, y_m, w_da, w_m, gate, *, tm=512, tn=1024):
    m, k = y_da.shape
    nj = D_MODEL // tn
    return pl.pallas_call(
        _mix_kernel,
        grid=(nj, m // tm),
        in_specs=[
            pl.BlockSpec((tm, k), lambda j, i: (i, 0)),
            pl.BlockSpec((tm, k), lambda j, i: (i, 0)),
            pl.BlockSpec((k, tn), lambda j, i: (0, j)),
            pl.BlockSpec((k, tn), lambda j, i: (0, j)),
            pl.BlockSpec((tm, tn), lambda j, i: (i, j)),
            pl.BlockSpec((tm, tn), lambda j, i: (i, nj + j)),
        ],
        out_specs=pl.BlockSpec((tm, tn), lambda j, i: (i, j)),
        out_shape=jax.ShapeDtypeStruct((m, D_MODEL), BF16),
        scratch_shapes=[pltpu.VMEM((k, tn), BF16), pltpu.VMEM((k, tn), BF16)],
        compiler_params=_cparams(2),
        name="branch_mix",
    )(y_da, y_m, w_da, w_m, gate, gate)


FFN_TM = 1024
HALF = D_MODEL // 2


def _pack_bf16_pairs(v):
    lo = lax.bitcast_convert_type(v[:, :HALF].astype(BF16).astype(F32), jnp.uint32)
    hi = lax.bitcast_convert_type(v[:, HALF:].astype(BF16).astype(F32), jnp.uint32)
    return (hi & jnp.uint32(0xFFFF0000)) | lax.shift_right_logical(lo, jnp.uint32(16))


def _unpack_bf16_pairs(w):
    lo = lax.bitcast_convert_type(lax.shift_left(w, jnp.uint32(16)), F32).astype(BF16)
    hi = lax.bitcast_convert_type(w & jnp.uint32(0xFFFF0000), F32).astype(BF16)
    return lo, hi


def _ffn_prep_kernel(h_ref, g_ref, wr_ref, br_ref, hn_ref, e_ref, w_ref, r_ref, cnt_ref, base_ref):
    @pl.when(pl.program_id(0) == 0)
    def _():
        base_ref[...] = jnp.zeros_like(base_ref)

    hn = _rms(h_ref[...], g_ref[...])
    hn_ref[...] = _pack_bf16_pairs(hn)
    logits = jnp.dot(hn, wr_ref[...], preferred_element_type=F32,
                     precision=lax.Precision.HIGHEST) + br_ref[...]
    lane = lax.broadcasted_iota(jnp.int32, (FFN_TM, N_EXPERTS), 1)
    lane_o = lax.broadcasted_iota(jnp.int32, (FFN_TM, 128), 1)
    ti = lax.broadcasted_iota(jnp.int32, (FFN_TM, FFN_TM), 0)
    ui = lax.broadcasted_iota(jnp.int32, (FFN_TM, FFN_TM), 1)
    tril = jnp.where(ui <= ti, 1.0, 0.0).astype(BF16)
    e_out = jnp.zeros((FFN_TM, 128), jnp.int32)
    r_out = jnp.zeros((FFN_TM, 128), jnp.int32)
    l_out = jnp.full((FFN_TM, 128), -jnp.inf, F32)
    base = base_ref[...]
    l = logits
    for kk in range(TOP_K):
        mk = jnp.max(l, axis=1, keepdims=True)
        ik = jnp.min(jnp.where(l == mk, lane, N_EXPERTS), axis=1, keepdims=True)
        hit = lane == ik
        oh = jnp.where(hit, 1.0, 0.0)
        cum = jnp.dot(tril, oh.astype(BF16), preferred_element_type=F32)
        rank = jnp.sum(oh * (cum + base), axis=1, keepdims=True) - 1.0
        base = base + jnp.sum(oh, axis=0, keepdims=True)
        e_out = jnp.where(lane_o == kk, ik, e_out)
        r_out = jnp.where(lane_o == kk, rank.astype(jnp.int32), r_out)
        l_out = jnp.where(lane_o == kk, mk, l_out)
        l = jnp.where(hit, -jnp.inf, l)
    base_ref[...] = base
    cnt_ref[...] = base
    ex = jnp.exp(l_out - jnp.max(l_out, axis=1, keepdims=True))
    e_ref[...] = e_out
    r_ref[...] = r_out
    w_ref[...] = ex / jnp.sum(ex, axis=1, keepdims=True)


def ffn_prep(h2, gain, w_router, b_router):
    row = lambda i: (i, 0)
    fixed = lambda i: (0, 0)
    return pl.pallas_call(
        _ffn_prep_kernel,
        grid=(N_TOK // FFN_TM,),
        in_specs=[
            pl.BlockSpec((FFN_TM, D_MODEL), row),
            pl.BlockSpec((1, D_MODEL), fixed),
            pl.BlockSpec((D_MODEL, N_EXPERTS), fixed),
            pl.BlockSpec((1, N_EXPERTS), fixed),
        ],
        out_specs=[
            pl.BlockSpec((FFN_TM, HALF), row),
            pl.BlockSpec((FFN_TM, 128), row),
            pl.BlockSpec((FFN_TM, 128), row),
            pl.BlockSpec((FFN_TM, 128), row),
            pl.BlockSpec((1, N_EXPERTS), fixed),
        ],
        out_shape=[
            jax.ShapeDtypeStruct((N_TOK, HALF), jnp.uint32),
            jax.ShapeDtypeStruct((N_TOK, 128), jnp.int32),
            jax.ShapeDtypeStruct((N_TOK, 128), F32),
            jax.ShapeDtypeStruct((N_TOK, 128), jnp.int32),
            jax.ShapeDtypeStruct((1, N_EXPERTS), F32),
        ],
        scratch_shapes=[pltpu.VMEM((1, N_EXPERTS), F32)],
        compiler_params=_cparams(1),
        name="ffn_prep",
    )(h2, gain.reshape(1, D_MODEL), w_router, b_router.reshape(1, N_EXPERTS))


SEG_ALIGN = 128
MOE_R = N_ASSIGN + N_EXPERTS * SEG_ALIGN


def _plan(counts_f, top_e, rank):
    counts = counts_f[0].astype(jnp.int32)
    seg_rows = (counts + SEG_ALIGN - 1) // SEG_ALIGN * SEG_ALIGN
    seg_start = jnp.cumsum(seg_rows) - seg_rows
    eq = top_e[:, :, None] == jnp.arange(N_EXPERTS, dtype=jnp.int32)[None, None, :]
    dest = jnp.sum(jnp.where(eq, seg_start[None, None, :], 0), axis=-1) + rank
    used = jnp.sum(seg_rows)
    slack = jnp.stack([used, (MOE_R - used) // SEG_ALIGN])
    return (dest.reshape(N_ASSIGN).astype(jnp.int32), seg_start.astype(jnp.int32),
            seg_rows.astype(jnp.int32), (seg_start + counts).astype(jnp.int32),
            (seg_rows - counts).astype(jnp.int32), slack.astype(jnp.int32))


def _zero_slack(slack_ref, zero_block, dst_rows, sem):
    zero_block[...] = jnp.zeros_like(zero_block)

    def copy(j):
        r0 = pl.multiple_of(slack_ref[0] + j * SEG_ALIGN, SEG_ALIGN)
        return pltpu.make_async_copy(zero_block, dst_rows(pl.ds(r0, SEG_ALIGN)), sem)

    def start(j, carry):
        copy(j).start()
        return carry

    def wait(j, carry):
        copy(j).wait()
        return carry

    lax.fori_loop(0, slack_ref[1], start, 0)
    lax.fori_loop(0, slack_ref[1], wait, 0)


DISP_TOK = 256


def _dispatch_kernel(dest_ref, pad0_ref, padn_ref, slack_ref, hn_ref, xs_ref,
                     zrow_ref, zblk_ref, sem, zsem):
    i = pl.program_id(0)

    @pl.when(i == 0)
    def _():
        zrow_ref[...] = jnp.zeros_like(zrow_ref)
        _zero_slack(slack_ref, zblk_ref, lambda rows: xs_ref.at[rows], zsem.at[0])

        def expert(e, carry):
            p0 = pad0_ref[e]
            pn = padn_ref[e]

            def zstart(r, c2):
                pltpu.make_async_copy(zrow_ref, xs_ref.at[pl.ds(p0 + r, 1)], zsem.at[0]).start()
                return c2

            def zwait(r, c2):
                pltpu.make_async_copy(zrow_ref, xs_ref.at[pl.ds(p0, 1)], zsem.at[0]).wait()
                return c2

            lax.fori_loop(0, pn, zstart, 0)
            lax.fori_loop(0, pn, zwait, 0)
            return carry

        lax.fori_loop(0, N_EXPERTS, expert, 0)

    def tok(t, carry):
        a = (i * DISP_TOK + t) * TOP_K
        for kk in range(TOP_K):
            pltpu.make_async_copy(hn_ref.at[pl.ds(t, 1)], xs_ref.at[pl.ds(dest_ref[a + kk], 1)],
                                  sem.at[0]).start(priority=kk % 2)
        return carry

    lax.fori_loop(0, DISP_TOK, tok, 0, unroll=4)
    for kk in range(TOP_K):
        pltpu.make_async_copy(hn_ref, xs_ref.at[pl.ds(0, DISP_TOK)], sem.at[0]).wait()


def dispatch(dest, pad0, padn, slack, hn_packed):
    grid_spec = pltpu.PrefetchScalarGridSpec(
        num_scalar_prefetch=4,
        grid=(N_TOK // DISP_TOK,),
        in_specs=[pl.BlockSpec((DISP_TOK, HALF), lambda i, d, p0, pn, z: (i, 0))],
        out_specs=pl.BlockSpec(memory_space=pl.ANY),
        scratch_shapes=[pltpu.VMEM((1, HALF), jnp.uint32),
                        pltpu.VMEM((SEG_ALIGN, HALF), jnp.uint32),
                        pltpu.SemaphoreType.DMA((1,)), pltpu.SemaphoreType.DMA((1,))],
    )
    return pl.pallas_call(
        _dispatch_kernel,
        grid_spec=grid_spec,
        out_shape=jax.ShapeDtypeStruct((MOE_R, HALF), jnp.uint32),
        compiler_params=_cparams(1),
        name="moe_dispatch",
    )(dest, pad0, padn, slack, hn_packed)


MOE_CH = 256
MOE_TF = 1024
MOE_NF = D_FF // MOE_TF


class _CopyGroup:
    def __init__(self, copies):
        self.copies = copies

    def start(self, priority=0):
        for cp in self.copies:
            cp.start(priority=priority)

    def wait(self):
        for cp in self.copies:
            cp.wait()


def _stream_rows(step, n_steps, start, rows, next_start, next_rows, state,
                 make_in, make_out, make_tail_in, make_tail_out,
                 compute_chunk, compute_tail, before_first_wait):
    @pl.when(step == 0)
    def _():
        for j in range(4):
            state[j] = 0

    n_ch = lax.shift_right_logical(rows, 8)
    tail = rows - n_ch * MOE_CH
    tail_row = start + n_ch * MOE_CH
    g0 = state[0]
    feeds_next = jnp.logical_and(step + 1 < n_steps, next_rows >= MOE_CH)

    def chunk_row(c):
        return start + c * MOE_CH

    @pl.when(tail > 0)
    def _():
        make_tail_in(tail_row).start(priority=1)

    @pl.when(jnp.logical_and(n_ch > 0, state[3] == 0))
    def _():
        make_in(start, lax.rem(g0, 2)).start(priority=1)

    before_first_wait()

    @pl.when(tail > 0)
    def _():
        make_tail_in(tail_row).wait()
        compute_tail()
        make_tail_out(tail_row).start(priority=1)

    def body(c, carry):
        slot = lax.rem(g0 + c, 2)
        make_in(chunk_row(c), slot).wait()

        @pl.when(c + 1 < n_ch)
        def _():
            make_in(chunk_row(c + 1), 1 - slot).start(priority=1)

        @pl.when(jnp.logical_and(c + 1 == n_ch, feeds_next))
        def _():
            make_in(next_start, 1 - slot).start(priority=1)

        @pl.when(state[1 + slot] == 1)
        def _():
            make_out(chunk_row(c), slot).wait()

        compute_chunk(slot)
        make_out(chunk_row(c), slot).start(priority=1)
        state[1 + slot] = 1
        return carry

    lax.fori_loop(0, n_ch, body, 0)
    state[0] = g0 + n_ch
    state[3] = jnp.where(jnp.logical_and(n_ch > 0, feeds_next), 1, 0)

    @pl.when(tail > 0)
    def _():
        make_tail_out(tail_row).wait()

    @pl.when(step == n_steps - 1)
    def _():
        for slot in range(2):
            @pl.when(state[1 + slot] == 1)
            def _():
                make_out(0, slot).wait()
                state[1 + slot] = 0


def _rows_at(row0, n):
    return pl.ds(row0 if isinstance(row0, int) else pl.multiple_of(row0, SEG_ALIGN), n)


def _moe_up_kernel(seg_ref, rows_ref, slack_ref, xs_ref, wg_ref, wl_ref, bg_ref, bl_ref, act_ref,
                   xbuf, obuf, xtail, otail, wg_bf, wl_bf, state, isem, osem, tsem):
    f = pl.program_id(0)
    e = pl.program_id(1)
    start = seg_ref[e]
    rows = rows_ref[e]
    e_next = lax.rem(e + 1, N_EXPERTS)

    def make_in(r0, slot):
        return pltpu.make_async_copy(xs_ref.at[_rows_at(r0, MOE_CH)], xbuf.at[slot], isem.at[slot])

    def make_out(r0, slot):
        return pltpu.make_async_copy(obuf.at[slot], act_ref.at[f, _rows_at(r0, MOE_CH)],
                                     osem.at[slot])

    def make_tail_in(r0):
        return pltpu.make_async_copy(xs_ref.at[_rows_at(r0, SEG_ALIGN)], xtail, tsem.at[0])

    def make_tail_out(r0):
        return pltpu.make_async_copy(otail, act_ref.at[f, _rows_at(r0, SEG_ALIGN)], tsem.at[1])

    def cast_weights():
        @pl.when(rows > 0)
        def _():
            wg_bf[...] = wg_ref[0].astype(BF16)
            wl_bf[...] = wl_ref[0].astype(BF16)

    def expert_mlp(words):
        lo, hi = _unpack_bf16_pairs(words)
        glu = (jnp.dot(lo, wg_bf[:HALF, :], preferred_element_type=F32)
               + jnp.dot(hi, wg_bf[HALF:, :], preferred_element_type=F32) + bg_ref[0])
        lin = (jnp.dot(lo, wl_bf[:HALF, :], preferred_element_type=F32)
               + jnp.dot(hi, wl_bf[HALF:, :], preferred_element_type=F32) + bl_ref[0])
        glu = jnp.minimum(glu, SWIGLU_LIMIT)
        lin = jnp.clip(lin, -SWIGLU_LIMIT, SWIGLU_LIMIT)
        return (glu * jax.nn.sigmoid(SWIGLU_ALPHA * glu) * (lin + 1.0)).astype(BF16)

    def compute_chunk(slot):
        obuf[slot] = expert_mlp(xbuf[slot])

    def compute_tail():
        otail[...] = expert_mlp(xtail[...])

    _stream_rows(f * N_EXPERTS + e, MOE_NF * N_EXPERTS, start, rows,
                 seg_ref[e_next], rows_ref[e_next], state,
                 make_in, make_out, make_tail_in, make_tail_out,
                 compute_chunk, compute_tail, cast_weights)

    @pl.when(e == N_EXPERTS - 1)
    def _():
        _zero_slack(slack_ref, otail, lambda rr: act_ref.at[f, rr], tsem.at[1])


def moe_up(seg_start, seg_rows, slack, xs, w1, b1):
    grid_spec = pltpu.PrefetchScalarGridSpec(
        num_scalar_prefetch=3,
        grid=(MOE_NF, N_EXPERTS),
        in_specs=[
            pl.BlockSpec(memory_space=pl.ANY),
            pl.BlockSpec((1, D_MODEL, MOE_TF), lambda f, e, s, r, z: (e, 0, f)),
            pl.BlockSpec((1, D_MODEL, MOE_TF), lambda f, e, s, r, z: (e, 0, MOE_NF + f)),
            pl.BlockSpec((1, 1, MOE_TF), lambda f, e, s, r, z: (e, 0, f)),
            pl.BlockSpec((1, 1, MOE_TF), lambda f, e, s, r, z: (e, 0, MOE_NF + f)),
        ],
        out_specs=pl.BlockSpec(memory_space=pl.ANY),
        scratch_shapes=[
            pltpu.VMEM((2, MOE_CH, HALF), jnp.uint32),
            pltpu.VMEM((2, MOE_CH, MOE_TF), BF16),
            pltpu.VMEM((SEG_ALIGN, HALF), jnp.uint32),
            pltpu.VMEM((SEG_ALIGN, MOE_TF), BF16),
            pltpu.VMEM((D_MODEL, MOE_TF), BF16),
            pltpu.VMEM((D_MODEL, MOE_TF), BF16),
            pltpu.SMEM((4,), jnp.int32),
            pltpu.SemaphoreType.DMA((2,)),
            pltpu.SemaphoreType.DMA((2,)),
            pltpu.SemaphoreType.DMA((2,)),
        ],
    )
    b13 = b1.reshape(N_EXPERTS, 1, 2 * D_FF)
    return pl.pallas_call(
        _moe_up_kernel,
        grid_spec=grid_spec,
        out_shape=jax.ShapeDtypeStruct((MOE_NF, MOE_R, MOE_TF), BF16),
        compiler_params=pltpu.CompilerParams(
            dimension_semantics=("arbitrary", "arbitrary"), vmem_limit_bytes=MOE_VMEM_LIMIT),
        name="moe_up",
    )(seg_start, seg_rows, slack, xs, w1, w1, b13, b13)


def _moe_down_kernel(seg_ref, rows_ref, slack_ref, act_ref, w_ref, b_ref, y_ref,
                     xbuf, obuf, xtail, otail, w_bf, state, isem, osem, tsem):
    e = pl.program_id(0)
    start = seg_ref[e]
    rows = rows_ref[e]
    e_next = lax.rem(e + 1, N_EXPERTS)

    def make_in(r0, slot):
        return _CopyGroup([
            pltpu.make_async_copy(act_ref.at[j, _rows_at(r0, MOE_CH)],
                                  xbuf.at[slot, :, pl.ds(j * MOE_TF, MOE_TF)], isem.at[slot])
            for j in range(MOE_NF)])

    def make_out(r0, slot):
        return pltpu.make_async_copy(obuf.at[slot], y_ref.at[_rows_at(r0, MOE_CH)], osem.at[slot])

    def make_tail_in(r0):
        return _CopyGroup([
            pltpu.make_async_copy(act_ref.at[j, _rows_at(r0, SEG_ALIGN)],
                                  xtail.at[:, pl.ds(j * MOE_TF, MOE_TF)], tsem.at[0])
            for j in range(MOE_NF)])

    def make_tail_out(r0):
        return pltpu.make_async_copy(otail, y_ref.at[_rows_at(r0, SEG_ALIGN)], tsem.at[1])

    def cast_weights():
        @pl.when(rows > 0)
        def _():
            w_bf[...] = w_ref[0].astype(BF16)

    def expert_out(a):
        return _pack_bf16_pairs(jnp.dot(a, w_bf[...], preferred_element_type=F32) + b_ref[0])

    def compute_chunk(slot):
        obuf[slot] = expert_out(xbuf[slot])

    def compute_tail():
        otail[...] = expert_out(xtail[...])

    _stream_rows(e, N_EXPERTS, start, rows, seg_ref[e_next], rows_ref[e_next], state,
                 make_in, make_out, make_tail_in, make_tail_out,
                 compute_chunk, compute_tail, cast_weights)

    @pl.when(e == N_EXPERTS - 1)
    def _():
        _zero_slack(slack_ref, otail, lambda rr: y_ref.at[rr], tsem.at[1])


def moe_down(seg_start, seg_rows, slack, act, w2, b2):
    grid_spec = pltpu.PrefetchScalarGridSpec(
        num_scalar_prefetch=3,
        grid=(N_EXPERTS,),
        in_specs=[
            pl.BlockSpec(memory_space=pl.ANY),
            pl.BlockSpec((1, D_FF, D_MODEL), lambda e, s, r, z: (e, 0, 0)),
            pl.BlockSpec((1, 1, D_MODEL), lambda e, s, r, z: (e, 0, 0)),
        ],
        out_specs=pl.BlockSpec(memory_space=pl.ANY),
        scratch_shapes=[
            pltpu.VMEM((2, MOE_CH, D_FF), BF16),
            pltpu.VMEM((2, MOE_CH, HALF), jnp.uint32),
            pltpu.VMEM((SEG_ALIGN, D_FF), BF16),
            pltpu.VMEM((SEG_ALIGN, HALF), jnp.uint32),
            pltpu.VMEM((D_FF, D_MODEL), BF16),
            pltpu.SMEM((4,), jnp.int32),
            pltpu.SemaphoreType.DMA((2,)),
            pltpu.SemaphoreType.DMA((2,)),
            pltpu.SemaphoreType.DMA((2,)),
        ],
    )
    return pl.pallas_call(
        _moe_down_kernel,
        grid_spec=grid_spec,
        out_shape=jax.ShapeDtypeStruct((MOE_R, HALF), jnp.uint32),
        compiler_params=pltpu.CompilerParams(
            dimension_semantics=("arbitrary",), vmem_limit_bytes=MOE_VMEM_LIMIT),
        name="moe_down",
    )(seg_start, seg_rows, slack, act, w2, b2.reshape(N_EXPERTS, 1, D_MODEL))


COMB_TM = 128
COMB_NT = N_TOK // COMB_TM


def _combine_kernel(dest_ref, y_ref, w_ref, h_ref, g_ref, o_ref, buf, sem):
    i = pl.program_id(0)

    def fetch(tile, slot):
        def tok(t, carry):
            a = (tile * COMB_TM + t) * TOP_K
            for kk in range(TOP_K):
                pltpu.make_async_copy(y_ref.at[pl.ds(dest_ref[a + kk], 1)],
                                      buf.at[slot, kk, pl.ds(t, 1)], sem.at[slot]).start(
                                          priority=kk % 2)
            return carry

        lax.fori_loop(0, COMB_TM, tok, 0, unroll=4)

    @pl.when(i == 0)
    def _():
        fetch(0, 0)

    slot = lax.rem(i, 2)

    @pl.when(i + 1 < COMB_NT)
    def _():
        fetch(i + 1, 1 - slot)

    for kk in range(TOP_K):
        pltpu.make_async_copy(y_ref.at[pl.ds(0, COMB_TM)], buf.at[slot, kk], sem.at[slot]).wait()
    w = w_ref[...]
    lo = h_ref[:, :HALF]
    hi = h_ref[:, HALF:]
    for kk in range(TOP_K):
        words = buf[slot, kk]
        wk = w[:, kk:kk + 1]
        lo = lo + wk * lax.bitcast_convert_type(lax.shift_left(words, jnp.uint32(16)), F32)
        hi = hi + wk * lax.bitcast_convert_type(words & jnp.uint32(0xFFFF0000), F32)
    ms = (jnp.sum(lo * lo, axis=-1, keepdims=True)
          + jnp.sum(hi * hi, axis=-1, keepdims=True)) * (1.0 / D_MODEL)
    inv = lax.rsqrt(ms + EPS)
    o_ref[:, :HALF] = lo * inv * g_ref[:, :HALF]
    o_ref[:, HALF:] = hi * inv * g_ref[:, HALF:]


def combine(dest, y, weight, h2, gain):
    grid_spec = pltpu.PrefetchScalarGridSpec(
        num_scalar_prefetch=1,
        grid=(COMB_NT,),
        in_specs=[
            pl.BlockSpec(memory_space=pl.ANY),
            pl.BlockSpec((COMB_TM, 128), lambda i, d: (i, 0)),
            pl.BlockSpec((COMB_TM, D_MODEL), lambda i, d: (i, 0)),
            pl.BlockSpec((1, D_MODEL), lambda i, d: (0, 0)),
        ],
        out_specs=pl.BlockSpec((COMB_TM, D_MODEL), lambda i, d: (i, 0)),
        scratch_shapes=[pltpu.VMEM((2, TOP_K, COMB_TM, HALF), jnp.uint32),
                        pltpu.SemaphoreType.DMA((2,))],
    )
    return pl.pallas_call(
        _combine_kernel,
        grid_spec=grid_spec,
        out_shape=jax.ShapeDtypeStruct((N_TOK, D_MODEL), F32),
        compiler_params=_cparams(1),
        name="moe_combine",
    )(dest, y, weight, h2, gain.reshape(1, D_MODEL))


def kernel(x, meta_tokens, rel_bias, norm_mix, w_in, conv_w, gate_bias_m, lambda_params, subln_da,
           w_branch_da, w_branch_m, w_gate, b_gate, w_out, norm_ffn, w_router, b_router,
           w1, b1, w2, b2, norm_final):
    layer = 0
    xn_pad, xn_real = norm_in(x, meta_tokens, norm_mix[layer])
    xn_pad2 = xn_pad.reshape(BATCH * LP, D_MODEL)
    proj = matmul(xn_pad2, w_in[layer], n_cols=PROJ_COLS, tm=768, tn=1024, name="proj_in")
    proj3 = proj.reshape(BATCH, LP, PROJ_COLS)
    w_g = jnp.pad(w_in[layer][:, COL_M_G:], ((0, 0), (0, 128 - 4 * H_M)))
    mg = matmul(xn_pad2, w_g, n_cols=128, tm=768, tn=128, out_dtype=F32, name="proj_gates")
    gate = matmul(xn_real.reshape(N_TOK, D_MODEL), w_gate[layer], n_cols=2 * D_MODEL,
                  tm=1024, tn=1024, bias=b_gate[layer], act="sigmoid", name="mix_gate")

    tab, consts = _bias_tables(rel_bias)
    y_da = diff_attention(consts, proj3, tab, lambda_params[layer], subln_da[layer])

    qk_m = conv_qk(proj3, conv_w[layer])
    gp = gate_prep(mg.reshape(BATCH, LP, 128), gate_bias_m[layer])
    gp4 = gp.reshape(BATCH, LP, 4, H_M)
    gcol = jnp.transpose(gp4, (0, 3, 1, 2))
    grow = jnp.transpose(gp4, (0, 3, 2, 1))
    y_m = mlstm(qk_m, proj3, gcol, grow)

    mixed = branch_mix(y_da.reshape(N_TOK, H_DA * DV_DA), y_m.reshape(N_TOK, H_M * DV_M),
                       w_branch_da[layer], w_branch_m[layer], gate)
    h2 = matmul(mixed, w_out[layer], n_cols=D_MODEL, tm=1024, tn=1024,
                res=x.reshape(N_TOK, D_MODEL), out_dtype=F32, name="out_proj")

    hn_packed, top_e, weight, rank, counts = ffn_prep(
        h2, norm_ffn[layer], w_router[layer], b_router[layer])
    dest, seg_start, seg_rows, pad0, padn, slack = _plan(
        counts, top_e[:, :TOP_K], rank[:, :TOP_K])
    xs = dispatch(dest, pad0, padn, slack, hn_packed)
    act = moe_up(seg_start, seg_rows, slack, xs, w1[layer], b1[layer])
    y = moe_down(seg_start, seg_rows, slack, act, w2[layer], b2[layer])
    out = combine(dest, y, weight, h2, norm_final)
    return out.reshape(BATCH, SEQ, D_MODEL)
```

```python
import functools
import math

import jax
import jax.numpy as jnp
from jax import lax
from jax.experimental import pallas as pl
from jax.experimental.pallas import tpu as pltpu

F32 = jnp.float32
BF16 = jnp.bfloat16

D_MODEL = 2048
BATCH = 2
SEQ = 4096
N_META = 16
BLOCK = 128
PAD = (-N_META) % BLOCK
LP = PAD + N_META + SEQ
NBLK = LP // BLOCK
EPS = 1e-6
NEG_INF = -1e30

H_DA = 4
DK_DA = 128
DV_DA = 256
H_M = 4
DK_M = 128
DV_M = 256
CONV_W = 5
N_BUCKETS = 32
MAX_DISTANCE = 128
N_EXPERTS = 32
TOP_K = 4
D_FF = 2048
SWIGLU_ALPHA = 1.702
SWIGLU_LIMIT = 7.0
LAMBDA_INIT = 0.8 - 0.6 * math.exp(-0.3 * 0)

COL_DA_Q = 0
COL_DA_K = 1024
COL_DA_V = 2048
COL_M_Q = 3072
COL_M_K = 3584
COL_M_V = 4096
COL_M_O = 5120
COL_M_G = 6144
PROJ_COLS = 6144

N_TOK = BATCH * SEQ
N_ASSIGN = N_TOK * TOP_K

VMEM_LIMIT = 52 * 1024 * 1024
MOE_VMEM_LIMIT = 58 * 1024 * 1024


def _cparams(n_axes):
    return pltpu.CompilerParams(
        dimension_semantics=("arbitrary",) * n_axes, vmem_limit_bytes=VMEM_LIMIT)


def _rms(v, gain):
    ms = jnp.mean(v * v, axis=-1, keepdims=True)
    return v * lax.rsqrt(ms + EPS) * gain


def _norm_in_kernel(x_ref, meta_ref, g_ref, pad_ref, real_ref):
    j = pl.program_id(1)
    g = g_ref[...]

    @pl.when(j == 0)
    def _():
        pad_ref[0, :PAD, :] = jnp.zeros((PAD, D_MODEL), BF16)
        pad_ref[0, PAD:, :] = _rms(meta_ref[...], g).astype(BF16)

    @pl.when(j > 0)
    def _():
        y = _rms(x_ref[0], g).astype(BF16)
        pad_ref[0] = y
        real_ref[0] = y


def norm_in(x, meta, gain):
    return pl.pallas_call(
        _norm_in_kernel,
        grid=(BATCH, NBLK),
        in_specs=[
            pl.BlockSpec((1, BLOCK, D_MODEL), lambda b, j: (b, jnp.maximum(j - 1, 0), 0)),
            pl.BlockSpec((N_META, D_MODEL), lambda b, j: (0, 0)),
            pl.BlockSpec((1, D_MODEL), lambda b, j: (0, 0)),
        ],
        out_specs=[
            pl.BlockSpec((1, BLOCK, D_MODEL), lambda b, j: (b, j, 0)),
            pl.BlockSpec((1, BLOCK, D_MODEL), lambda b, j: (b, jnp.maximum(j - 1, 0), 0)),
        ],
        out_shape=[
            jax.ShapeDtypeStruct((BATCH, LP, D_MODEL), BF16),
            jax.ShapeDtypeStruct((BATCH, SEQ, D_MODEL), BF16),
        ],
        compiler_params=_cparams(2),
        name="norm_in",
    )(x, meta, gain.reshape(1, D_MODEL))


def _mm_kernel(*refs, has_bias, has_res, act):
    x_ref, w_ref = refs[0], refs[1]
    pos = 2
    b_ref = r_ref = None
    if has_bias:
        b_ref = refs[pos]
        pos += 1
    if has_res:
        r_ref = refs[pos]
        pos += 1
    o_ref, wbf_ref = refs[pos], refs[pos + 1]

    @pl.when(pl.program_id(1) == 0)
    def _():
        wbf_ref[...] = w_ref[...].astype(BF16)

    acc = jnp.dot(x_ref[...], wbf_ref[...], preferred_element_type=F32)
    if has_bias:
        acc = acc + b_ref[...]
    if act == "sigmoid":
        acc = jax.nn.sigmoid(acc)
    if has_res:
        acc = acc + r_ref[...]
    o_ref[...] = acc.astype(o_ref.dtype)


def matmul(x, w, *, n_cols, col_block0=0, tm, tn, bias=None, res=None, act=None,
           out_dtype=BF16, name):
    m, k = x.shape
    in_specs = [
        pl.BlockSpec((tm, k), lambda j, i: (i, 0)),
        pl.BlockSpec((k, tn), lambda j, i: (0, j + col_block0)),
    ]
    args = [x, w]
    if bias is not None:
        in_specs.append(pl.BlockSpec((1, tn), lambda j, i: (0, j)))
        args.append(bias.reshape(1, n_cols))
    if res is not None:
        in_specs.append(pl.BlockSpec((tm, tn), lambda j, i: (i, j)))
        args.append(res)
    return pl.pallas_call(
        functools.partial(_mm_kernel, has_bias=bias is not None, has_res=res is not None, act=act),
        grid=(n_cols // tn, m // tm),
        in_specs=in_specs,
        out_specs=pl.BlockSpec((tm, tn), lambda j, i: (i, j)),
        out_shape=jax.ShapeDtypeStruct((m, n_cols), out_dtype),
        scratch_shapes=[pltpu.VMEM((k, tn), BF16)],
        compiler_params=_cparams(2),
        name=name,
    )(*args)


def _conv_kernel(p_ref, w_ref, o_ref):
    c = pl.program_id(1)
    x = p_ref[0].astype(F32)
    w = w_ref[...]
    half = CONV_W // 2
    acc = w[half:half + 1, :] * x
    for j in range(CONV_W):
        if j != half:
            acc = acc + w[j:j + 1, :] * pltpu.roll(x, (half - j) % LP, axis=0)
    y = acc * jax.nn.sigmoid(acc)
    rows = lax.broadcasted_iota(jnp.int32, (LP, 1), 0)
    y = jnp.where(rows >= PAD, y, 0.0)
    scale = jnp.where(c < 2, DK_M ** -0.5, 1.0).astype(F32)
    o_ref[0] = (y * scale).astype(BF16)


def conv_qk(proj3, conv_w):
    cw = 256
    return pl.pallas_call(
        _conv_kernel,
        grid=(BATCH, (2 * H_M * DK_M) // cw),
        in_specs=[
            pl.BlockSpec((1, LP, cw), lambda b, c: (b, 0, COL_M_Q // cw + c)),
            pl.BlockSpec((CONV_W, cw), lambda b, c: (0, c)),
        ],
        out_specs=pl.BlockSpec((1, LP, cw), lambda b, c: (b, 0, c)),
        out_shape=jax.ShapeDtypeStruct((BATCH, LP, 2 * H_M * DK_M), BF16),
        compiler_params=_cparams(2),
        name="conv_qk",
    )(proj3, conv_w)


def _split_dot(tri, v):
    hi = v.astype(BF16)
    r1 = v - hi.astype(F32)
    mid = r1.astype(BF16)
    lo = (r1 - mid.astype(F32)).astype(BF16)
    return (jnp.dot(tri, hi, preferred_element_type=F32)
            + jnp.dot(tri, mid, preferred_element_type=F32)
            + jnp.dot(tri, lo, preferred_element_type=F32))


def _gate_kernel(g_ref, bias_ref, o_ref):
    ti = lax.broadcasted_iota(jnp.int32, (BLOCK, BLOCK), 0)
    ui = lax.broadcasted_iota(jnp.int32, (BLOCK, BLOCK), 1)
    tril = jnp.where(ui <= ti, 1.0, 0.0).astype(BF16)
    triu = jnp.where(ui >= ti, 1.0, 0.0).astype(BF16)
    ch = lax.broadcasted_iota(jnp.int32, (BLOCK, 4 * H_M), 1)
    typ = lax.shift_right_logical(ch, 2)
    rloc = lax.broadcasted_iota(jnp.int32, (BLOCK, 4 * H_M), 0)

    def body(c, carry):
        r0 = pl.multiple_of(c * BLOCK, BLOCK)
        g = g_ref[0, pl.ds(r0, BLOCK), :][:, :4 * H_M] + bias_ref[...]
        valid = (rloc + r0) >= PAD
        lsig = -(jnp.maximum(-g, 0.0) + jnp.log1p(jnp.exp(-jnp.abs(g))))
        lf = jnp.where(valid, lsig, 0.0)
        cum = _split_dot(tril, lf)
        rcum = _split_dot(triu, lf)
        li = jnp.where(valid, g, -jnp.inf)
        out = jnp.where(typ == 1, cum, jnp.where(typ == 3, rcum, li))
        o_ref[0, pl.ds(r0, BLOCK), :] = out
        return carry

    lax.fori_loop(0, NBLK, body, 0)


def gate_prep(mg3, gate_bias):
    return pl.pallas_call(
        _gate_kernel,
        grid=(BATCH,),
        in_specs=[
            pl.BlockSpec((1, LP, 128), lambda b: (b, 0, 0)),
            pl.BlockSpec((1, 4 * H_M), lambda b: (0, 0)),
        ],
        out_specs=pl.BlockSpec((1, LP, 4 * H_M), lambda b: (b, 0, 0)),
        out_shape=jax.ShapeDtypeStruct((BATCH, LP, 4 * H_M), F32),
        compiler_params=_cparams(1),
        name="gate_prep",
    )(mg3, gate_bias.reshape(1, 4 * H_M))


MLSTM_HP = 2
MLSTM_MID = NBLK // 2


def _mlstm_kernel(q_ref, k_ref, v_ref, o_ref, gc_ref, gr_ref, y_ref,
                  hs_ref, c_ref, n_ref, m_ref):
    c_ref[...] = jnp.zeros_like(c_ref)
    n_ref[...] = jnp.zeros_like(n_ref)
    m_ref[...] = jnp.zeros_like(m_ref)
    ti = lax.broadcasted_iota(jnp.int32, (BLOCK, BLOCK), 0)
    si = lax.broadcasted_iota(jnp.int32, (BLOCK, BLOCK), 1)
    mask_f = si <= ti
    mask_b = si >= ti

    def chain(c, hl, bwd, final):
        idx = 2 * hl + bwd
        r0 = c * BLOCK if isinstance(c, int) else pl.multiple_of(c * BLOCK, BLOCK)
        q = q_ref[0, pl.ds(r0, BLOCK), hl * DK_M:(hl + 1) * DK_M]
        k = k_ref[0, pl.ds(r0, BLOCK), hl * DK_M:(hl + 1) * DK_M]
        v = v_ref[0, pl.ds(r0, BLOCK), hl * DV_M:(hl + 1) * DV_M]
        gc = gc_ref[0, hl, pl.ds(r0, BLOCK), :]
        gr = gr_ref[0, hl, :, pl.ds(r0, BLOCK)]
        a = 2 * bwd
        li_c, b_c = gc[:, a:a + 1], gc[:, a + 1:a + 2]
        li_r, b_r = gr[a:a + 1, :], gr[a + 1:a + 2, :]
        b_end = b_c[0:1, :] if bwd else b_c[BLOCK - 1:BLOCK, :]
        m_prev = m_ref[idx][:, 0:1]
        cst = c_ref[idx]
        nst = n_ref[idx]
        dmat = jnp.where(mask_b if bwd else mask_f, b_c - b_r + li_r, -jnp.inf)
        inter = b_c + m_prev
        m_t = jnp.maximum(inter, jnp.max(dmat, axis=1, keepdims=True))
        w_inter = jnp.exp(inter - m_t)
        qk = lax.dot_general(q, k, (((1,), (1,)), ((), ())), preferred_element_type=F32)
        s = qk * jnp.exp(dmat - m_t)
        num = (w_inter * jnp.dot(q, cst.astype(BF16), preferred_element_type=F32)
               + jnp.dot(s.astype(BF16), v, preferred_element_type=F32))
        den = jnp.sum(s + w_inter * (q.astype(F32) * nst), axis=1, keepdims=True)
        h = num / jnp.maximum(jnp.abs(den), jnp.exp(-m_t))
        cols = slice(hl * DV_M, (hl + 1) * DV_M)
        if final:
            og = jax.nn.sigmoid(o_ref[0, pl.ds(r0, BLOCK), cols].astype(F32))
            y_ref[0, pl.ds(r0 - BLOCK, BLOCK), cols] = (
                og * (hs_ref[pl.ds(r0, BLOCK), cols] + h)).astype(BF16)
        else:
            hs_ref[pl.ds(r0, BLOCK), cols] = h
        ldec_c = b_end - b_c + li_c
        ldec_r = b_end - b_r + li_r
        m_new = jnp.maximum(b_end + m_prev, jnp.max(ldec_r, axis=1, keepdims=True))
        w_c = jnp.exp(b_end + m_prev - m_new)
        w_s = jnp.exp(ldec_c - m_new)
        wv = (w_s * v.astype(F32)).astype(BF16)
        c_ref[idx] = w_c * cst + lax.dot_general(
            k, wv, (((0,), (0,)), ((), ())), preferred_element_type=F32)
        n_ref[idx] = w_c * nst + jnp.sum(w_s * k.astype(F32), axis=0, keepdims=True)
        m_ref[idx] = jnp.broadcast_to(m_new, (1, BLOCK))

    def first_half(i, carry):
        for hl in range(MLSTM_HP):
            chain(i, hl, 0, False)
            chain(NBLK - 1 - i, hl, 1, False)
        return carry

    def second_half(i, carry):
        for hl in range(MLSTM_HP):
            chain(i, hl, 0, True)
            chain(NBLK - 1 - i, hl, 1, True)
        return carry

    lax.fori_loop(0, MLSTM_MID, first_half, 0)
    for hl in range(MLSTM_HP):
        chain(MLSTM_MID, hl, 0, False)
        chain(MLSTM_MID, hl, 1, True)
    lax.fori_loop(MLSTM_MID + 1, NBLK - 1, second_half, 0)
    for hl in range(MLSTM_HP):
        chain(NBLK - 1, hl, 0, True)


def mlstm(qk_m, proj3, gcol, grow):
    hp = MLSTM_HP
    kw, vw = hp * DK_M, hp * DV_M
    return pl.pallas_call(
        _mlstm_kernel,
        grid=(BATCH, H_M // hp),
        in_specs=[
            pl.BlockSpec((1, LP, kw), lambda b, g: (b, 0, g)),
            pl.BlockSpec((1, LP, kw), lambda b, g: (b, 0, (H_M * DK_M) // kw + g)),
            pl.BlockSpec((1, LP, vw), lambda b, g: (b, 0, COL_M_V // vw + g)),
            pl.BlockSpec((1, LP, vw), lambda b, g: (b, 0, COL_M_O // vw + g)),
            pl.BlockSpec((1, hp, LP, 4), lambda b, g: (b, g, 0, 0)),
            pl.BlockSpec((1, hp, 4, LP), lambda b, g: (b, g, 0, 0)),
        ],
        out_specs=pl.BlockSpec((1, SEQ, vw), lambda b, g: (b, 0, g)),
        out_shape=jax.ShapeDtypeStruct((BATCH, SEQ, H_M * DV_M), BF16),
        scratch_shapes=[
            pltpu.VMEM((LP, vw), F32),
            pltpu.VMEM((2 * hp, DK_M, DV_M), F32),
            pltpu.VMEM((2 * hp, 1, DK_M), F32),
            pltpu.VMEM((2 * hp, 1, BLOCK), F32),
        ],
        compiler_params=_cparams(2),
        name="mlstm",
    )(qk_m, qk_m, proj3, proj3, gcol, grow)


ATT_BAND = 3 * BLOCK
ATT_GROUP = 6 * BLOCK
ATT_NGROUP = (LP - ATT_BAND) // ATT_GROUP


def _attn_kernel(c_ref, q_ref, k1_ref, v1_ref, tab_ref, lam_ref, sg_ref, o_ref,
                 s_ref, k_ref, v_ref):
    h = pl.program_id(1)
    qb = pl.program_id(2) + 1

    @pl.when(pl.program_id(2) == 0)
    def _():
        for rep in range(2):
            k_ref[0, rep * LP:(rep + 1) * LP, :] = k1_ref[0]
            v_ref[0, rep * LP:(rep + 1) * LP, :] = v1_ref[0]

    q = q_ref[0]
    scale = DK_DA ** -0.5
    c_neg = c_ref[h, 0]
    c_pos = c_ref[h, 1]
    lp = lam_ref[...]
    lam = (jnp.exp(jnp.sum(lp[0:1] * lp[1:2], axis=1, keepdims=True))
           - jnp.exp(jnp.sum(lp[2:3] * lp[3:4], axis=1, keepdims=True)) + LAMBDA_INIT)

    groups = [((qb - 1) * BLOCK, ATT_BAND, 0)]
    for g in range(ATT_NGROUP):
        groups.append(((qb + 2) * BLOCK + g * ATT_GROUP, ATT_GROUP, ATT_BAND + g * ATT_GROUP))

    def lane_fold(acc, t, op):
        for j in range(t.shape[1] // BLOCK):
            piece = t[:, j * BLOCK:(j + 1) * BLOCK]
            acc = piece if acc is None else op(acc, piece)
        return acc

    mx = [None, None]
    for gi, (koff, width, col0) in enumerate(groups):
        koff = pl.multiple_of(koff, BLOCK)
        if gi == 0:
            bias = tab_ref[0, 0]
        else:
            kpos = koff + lax.broadcasted_iota(jnp.int32, (1, width), 1)
            bias = jnp.where(kpos < LP, c_pos, jnp.where(kpos < LP + PAD, NEG_INF, c_neg))
        for m in range(2):
            kk = k_ref[0, pl.ds(koff, width), m * DK_DA:(m + 1) * DK_DA]
            s = lax.dot_general(q[:, m * DK_DA:(m + 1) * DK_DA], kk, (((1,), (1,)), ((), ())),
                                preferred_element_type=F32) * scale + bias
            s_ref[m, :, col0:col0 + width] = s
            mx[m] = lane_fold(mx[m], s, jnp.maximum)
    row_max = [jnp.max(mx[m], axis=1, keepdims=True) for m in range(2)]

    lsum = [None, None]
    acc = [None, None]
    for koff, width, col0 in groups:
        koff = pl.multiple_of(koff, BLOCK)
        vv = v_ref[0, pl.ds(koff, width), :]
        for m in range(2):
            p = jnp.exp(s_ref[m, :, col0:col0 + width] - row_max[m])
            lsum[m] = lane_fold(lsum[m], p, jnp.add)
            pv = jnp.dot(p.astype(BF16), vv, preferred_element_type=F32)
            acc[m] = pv if acc[m] is None else acc[m] + pv
    l1 = jnp.sum(lsum[0], axis=1, keepdims=True)
    l2 = jnp.sum(lsum[1], axis=1, keepdims=True)
    o = acc[0] / l1 - lam * (acc[1] / l2)
    o_ref[0] = (_rms(o, sg_ref[...]) * (1.0 - LAMBDA_INIT)).astype(BF16)


def diff_attention(consts, proj3, tab, lam_params, subln):
    nq = SEQ // BLOCK
    kblk0 = COL_DA_K // (2 * DK_DA)
    vblk0 = COL_DA_V // DV_DA

    def tab_map(b, h, i):
        case = jnp.where(i == 0, 0, jnp.where(i == nq - 1, 2, 1))
        return (h, case, 0, 0)

    return pl.pallas_call(
        _attn_kernel,
        grid=(BATCH, H_DA, nq),
        in_specs=[
            pl.BlockSpec(memory_space=pltpu.SMEM),
            pl.BlockSpec((1, BLOCK, 2 * DK_DA), lambda b, h, i: (b, i + 1, h)),
            pl.BlockSpec((1, LP, 2 * DK_DA), lambda b, h, i: (b, 0, kblk0 + h)),
            pl.BlockSpec((1, LP, DV_DA), lambda b, h, i: (b, 0, vblk0 + h)),
            pl.BlockSpec((1, 1, BLOCK, ATT_BAND), tab_map),
            pl.BlockSpec((4, DK_DA), lambda b, h, i: (0, 0)),
            pl.BlockSpec((1, DV_DA), lambda b, h, i: (0, 0)),
        ],
        out_specs=pl.BlockSpec((1, BLOCK, DV_DA), lambda b, h, i: (b, i, h)),
        out_shape=jax.ShapeDtypeStruct((BATCH, SEQ, H_DA * DV_DA), BF16),
        scratch_shapes=[pltpu.VMEM((2, BLOCK, LP), F32),
                        pltpu.VMEM((1, 2 * LP, 2 * DK_DA), BF16),
                        pltpu.VMEM((1, 2 * LP, DV_DA), BF16)],
        compiler_params=_cparams(3),
        name="diff_attn",
    )(consts, proj3, proj3, proj3, tab, lam_params, subln.reshape(1, DV_DA))


def _bias_tables(rel_bias):
    rb = rel_bias.astype(F32)
    i = jnp.arange(BLOCK, dtype=jnp.int32)[:, None]
    j = jnp.arange(ATT_BAND, dtype=jnp.int32)[None, :]
    rel = j - BLOCK - i
    nb = N_BUCKETS // 2
    max_exact = nb // 2
    n = jnp.abs(rel)
    nf = jnp.maximum(n, 1).astype(F32)
    large = max_exact + (jnp.log(nf / max_exact) / math.log(MAX_DISTANCE / max_exact)
                         * (nb - max_exact)).astype(jnp.int32)
    large = jnp.minimum(large, nb - 1)
    bucket = jnp.where(rel > 0, nb, 0) + jnp.where(n < max_exact, n, large)
    hit = bucket[None, :, :, None] == jnp.arange(N_BUCKETS, dtype=jnp.int32)
    gen = jnp.sum(jnp.where(hit, rb.T[:, None, None, :], 0.0), axis=-1)
    c_neg = rb[nb - 1]
    c_pos = rb[N_BUCKETS - 1]
    jj = j[None]
    first = jnp.where(jj < PAD, NEG_INF, gen)
    wrapped = jnp.where(jj - 2 * BLOCK < PAD, NEG_INF, c_neg[:, None, None])
    last = jnp.where(jj >= 2 * BLOCK, wrapped, gen)
    tab = jnp.stack([first, gen, last], axis=1)
    consts = jnp.stack([c_neg, c_pos], axis=1)
    return tab, consts


def _mix_kernel(ya_ref, ym_ref, wa_ref, wm_ref, ga_ref, gm_ref, o_ref, wa_bf, wm_bf):
    @pl.when(pl.program_id(1) == 0)
    def _():
        wa_bf[...] = wa_ref[...].astype(BF16)
        wm_bf[...] = wm_ref[...].astype(BF16)

    a = jnp.dot(ya_ref[...], wa_bf[...], preferred_element_type=F32)
    m = jnp.dot(ym_ref[...], wm_bf[...], preferred_element_type=F32)
    o_ref[...] = (ga_ref[...].astype(F32) * a + gm_ref[...].astype(F32) * m).astype(BF16)


def branch_mix(y_da, y_m, w_da, w_m, gate, *, tm=512, tn=1024):
    m, k = y_da.shape
    nj = D_MODEL // tn
    return pl.pallas_call(
        _mix_kernel,
        grid=(nj, m // tm),
        in_specs=[
            pl.BlockSpec((tm, k), lambda j, i: (i, 0)),
            pl.BlockSpec((tm, k), lambda j, i: (i, 0)),
            pl.BlockSpec((k, tn), lambda j, i: (0, j)),
            pl.BlockSpec((k, tn), lambda j, i: (0, j)),
            pl.BlockSpec((tm, tn), lambda j, i: (i, j)),
            pl.BlockSpec((tm, tn), lambda j, i: (i, nj + j)),
        ],
        out_specs=pl.BlockSpec((tm, tn), lambda j, i: (i, j)),
        out_shape=jax.ShapeDtypeStruct((m, D_MODEL), BF16),
        scratch_shapes=[pltpu.VMEM((k, tn), BF16), pltpu.VMEM((k, tn), BF16)],
        compiler_params=_cparams(2),
        name="branch_mix",
    )(y_da, y_m, w_da, w_m, gate, gate)


FFN_TM = 1024
HALF = D_MODEL // 2


def _pack_bf16_pairs(v):
    lo = lax.bitcast_convert_type(v[:, :HALF].astype(BF16).astype(F32), jnp.uint32)
    hi = lax.bitcast_convert_type(v[:, HALF:].astype(BF16).astype(F32), jnp.uint32)
    return (hi & jnp.uint32(0xFFFF0000)) | lax.shift_right_logical(lo, jnp.uint32(16))


def _unpack_bf16_pairs(w):
    lo = lax.bitcast_convert_type(lax.shift_left(w, jnp.uint32(16)), F32).astype(BF16)
    hi = lax.bitcast_convert_type(w & jnp.uint32(0xFFFF0000), F32).astype(BF16)
    return lo, hi


def _ffn_prep_kernel(h_ref, g_ref, wr_ref, br_ref, hn_ref, e_ref, w_ref, r_ref, cnt_ref, base_ref):
    @pl.when(pl.program_id(0) == 0)
    def _():
        base_ref[...] = jnp.zeros_like(base_ref)

    hn = _rms(h_ref[...], g_ref[...])
    hn_ref[...] = _pack_bf16_pairs(hn)
    logits = jnp.dot(hn, wr_ref[...], preferred_element_type=F32,
                     precision=lax.Precision.HIGHEST) + br_ref[...]
    lane = lax.broadcasted_iota(jnp.int32, (FFN_TM, N_EXPERTS), 1)
    lane_o = lax.broadcasted_iota(jnp.int32, (FFN_TM, 128), 1)
    ti = lax.broadcasted_iota(jnp.int32, (FFN_TM, FFN_TM), 0)
    ui = lax.broadcasted_iota(jnp.int32, (FFN_TM, FFN_TM), 1)
    tril = jnp.where(ui <= ti, 1.0, 0.0).astype(BF16)
    e_out = jnp.zeros((FFN_TM, 128), jnp.int32)
    r_out = jnp.zeros((FFN_TM, 128), jnp.int32)
    l_out = jnp.full((FFN_TM, 128), -jnp.inf, F32)
    base = base_ref[...]
    l = logits
    for kk in range(TOP_K):
        mk = jnp.max(l, axis=1, keepdims=True)
        ik = jnp.min(jnp.where(l == mk, lane, N_EXPERTS), axis=1, keepdims=True)
        hit = lane == ik
        oh = jnp.where(hit, 1.0, 0.0)
        cum = jnp.dot(tril, oh.astype(BF16), preferred_element_type=F32)
        rank = jnp.sum(oh * (cum + base), axis=1, keepdims=True) - 1.0
        base = base + jnp.sum(oh, axis=0, keepdims=True)
        e_out = jnp.where(lane_o == kk, ik, e_out)
        r_out = jnp.where(lane_o == kk, rank.astype(jnp.int32), r_out)
        l_out = jnp.where(lane_o == kk, mk, l_out)
        l = jnp.where(hit, -jnp.inf, l)
    base_ref[...] = base
    cnt_ref[...] = base
    ex = jnp.exp(l_out - jnp.max(l_out, axis=1, keepdims=True))
    e_ref[...] = e_out
    r_ref[...] = r_out
    w_ref[...] = ex / jnp.sum(ex, axis=1, keepdims=True)


def ffn_prep(h2, gain, w_router, b_router):
    row = lambda i: (i, 0)
    fixed = lambda i: (0, 0)
    return pl.pallas_call(
        _ffn_prep_kernel,
        grid=(N_TOK // FFN_TM,),
        in_specs=[
            pl.BlockSpec((FFN_TM, D_MODEL), row),
            pl.BlockSpec((1, D_MODEL), fixed),
            pl.BlockSpec((D_MODEL, N_EXPERTS), fixed),
            pl.BlockSpec((1, N_EXPERTS), fixed),
        ],
        out_specs=[
            pl.BlockSpec((FFN_TM, HALF), row),
            pl.BlockSpec((FFN_TM, 128), row),
            pl.BlockSpec((FFN_TM, 128), row),
            pl.BlockSpec((FFN_TM, 128), row),
            pl.BlockSpec((1, N_EXPERTS), fixed),
        ],
        out_shape=[
            jax.ShapeDtypeStruct((N_TOK, HALF), jnp.uint32),
            jax.ShapeDtypeStruct((N_TOK, 128), jnp.int32),
            jax.ShapeDtypeStruct((N_TOK, 128), F32),
            jax.ShapeDtypeStruct((N_TOK, 128), jnp.int32),
            jax.ShapeDtypeStruct((1, N_EXPERTS), F32),
        ],
        scratch_shapes=[pltpu.VMEM((1, N_EXPERTS), F32)],
        compiler_params=_cparams(1),
        name="ffn_prep",
    )(h2, gain.reshape(1, D_MODEL), w_router, b_router.reshape(1, N_EXPERTS))


SEG_ALIGN = 128
MOE_R = N_ASSIGN + N_EXPERTS * SEG_ALIGN


def _plan(counts_f, top_e, rank):
    counts = counts_f[0].astype(jnp.int32)
    seg_rows = (counts + SEG_ALIGN - 1) // SEG_ALIGN * SEG_ALIGN
    seg_start = jnp.cumsum(seg_rows) - seg_rows
    eq = top_e[:, :, None] == jnp.arange(N_EXPERTS, dtype=jnp.int32)[None, None, :]
    dest = jnp.sum(jnp.where(eq, seg_start[None, None, :], 0), axis=-1) + rank
    used = jnp.sum(seg_rows)
    slack = jnp.stack([used, (MOE_R - used) // SEG_ALIGN])
    return (dest.reshape(N_ASSIGN).astype(jnp.int32), seg_start.astype(jnp.int32),
            seg_rows.astype(jnp.int32), (seg_start + counts).astype(jnp.int32),
            (seg_rows - counts).astype(jnp.int32), slack.astype(jnp.int32))


def _zero_slack(slack_ref, zero_block, dst_rows, sem):
    zero_block[...] = jnp.zeros_like(zero_block)

    def copy(j):
        r0 = pl.multiple_of(slack_ref[0] + j * SEG_ALIGN, SEG_ALIGN)
        return pltpu.make_async_copy(zero_block, dst_rows(pl.ds(r0, SEG_ALIGN)), sem)

    def start(j, carry):
        copy(j).start()
        return carry

    def wait(j, carry):
        copy(j).wait()
        return carry

    lax.fori_loop(0, slack_ref[1], start, 0)
    lax.fori_loop(0, slack_ref[1], wait, 0)


DISP_TOK = 256


def _dispatch_kernel(dest_ref, pad0_ref, padn_ref, slack_ref, hn_ref, xs_ref,
                     zrow_ref, zblk_ref, sem, zsem):
    i = pl.program_id(0)

    @pl.when(i == 0)
    def _():
        zrow_ref[...] = jnp.zeros_like(zrow_ref)
        _zero_slack(slack_ref, zblk_ref, lambda rows: xs_ref.at[rows], zsem.at[0])

        def expert(e, carry):
            p0 = pad0_ref[e]
            pn = padn_ref[e]

            def zstart(r, c2):
                pltpu.make_async_copy(zrow_ref, xs_ref.at[pl.ds(p0 + r, 1)], zsem.at[0]).start()
                return c2

            def zwait(r, c2):
                pltpu.make_async_copy(zrow_ref, xs_ref.at[pl.ds(p0, 1)], zsem.at[0]).wait()
                return c2

            lax.fori_loop(0, pn, zstart, 0)
            lax.fori_loop(0, pn, zwait, 0)
            return carry

        lax.fori_loop(0, N_EXPERTS, expert, 0)

    def tok(t, carry):
        a = (i * DISP_TOK + t) * TOP_K
        for kk in range(TOP_K):
            pltpu.make_async_copy(hn_ref.at[pl.ds(t, 1)], xs_ref.at[pl.ds(dest_ref[a + kk], 1)],
                                  sem.at[0]).start(priority=kk % 2)
        return carry

    lax.fori_loop(0, DISP_TOK, tok, 0, unroll=4)
    for kk in range(TOP_K):
        pltpu.make_async_copy(hn_ref, xs_ref.at[pl.ds(0, DISP_TOK)], sem.at[0]).wait()


def dispatch(dest, pad0, padn, slack, hn_packed):
    grid_spec = pltpu.PrefetchScalarGridSpec(
        num_scalar_prefetch=4,
        grid=(N_TOK // DISP_TOK,),
        in_specs=[pl.BlockSpec((DISP_TOK, HALF), lambda i, d, p0, pn, z: (i, 0))],
        out_specs=pl.BlockSpec(memory_space=pl.ANY),
        scratch_shapes=[pltpu.VMEM((1, HALF), jnp.uint32),
                        pltpu.VMEM((SEG_ALIGN, HALF), jnp.uint32),
                        pltpu.SemaphoreType.DMA((1,)), pltpu.SemaphoreType.DMA((1,))],
    )
    return pl.pallas_call(
        _dispatch_kernel,
        grid_spec=grid_spec,
        out_shape=jax.ShapeDtypeStruct((MOE_R, HALF), jnp.uint32),
        compiler_params=_cparams(1),
        name="moe_dispatch",
    )(dest, pad0, padn, slack, hn_packed)


MOE_CH = 256
MOE_TF = 1024
MOE_NF = D_FF // MOE_TF


class _CopyGroup:
    def __init__(self, copies):
        self.copies = copies

    def start(self, priority=0):
        for cp in self.copies:
            cp.start(priority=priority)

    def wait(self):
        for cp in self.copies:
            cp.wait()


def _stream_rows(step, n_steps, start, rows, next_start, next_rows, state,
                 make_in, make_out, make_tail_in, make_tail_out,
                 compute_chunk, compute_tail, before_first_wait):
    @pl.when(step == 0)
    def _():
        for j in range(4):
            state[j] = 0

    n_ch = lax.shift_right_logical(rows, 8)
    tail = rows - n_ch * MOE_CH
    tail_row = start + n_ch * MOE_CH
    g0 = state[0]
    feeds_next = jnp.logical_and(step + 1 < n_steps, next_rows >= MOE_CH)

    def chunk_row(c):
        return start + c * MOE_CH

    @pl.when(tail > 0)
    def _():
        make_tail_in(tail_row).start(priority=1)

    @pl.when(jnp.logical_and(n_ch > 0, state[3] == 0))
    def _():
        make_in(start, lax.rem(g0, 2)).start(priority=1)

    before_first_wait()

    @pl.when(tail > 0)
    def _():
        make_tail_in(tail_row).wait()
        compute_tail()
        make_tail_out(tail_row).start(priority=1)

    def body(c, carry):
        slot = lax.rem(g0 + c, 2)
        make_in(chunk_row(c), slot).wait()

        @pl.when(c + 1 < n_ch)
        def _():
            make_in(chunk_row(c + 1), 1 - slot).start(priority=1)

        @pl.when(jnp.logical_and(c + 1 == n_ch, feeds_next))
        def _():
            make_in(next_start, 1 - slot).start(priority=1)

        @pl.when(state[1 + slot] == 1)
        def _():
            make_out(chunk_row(c), slot).wait()

        compute_chunk(slot)
        make_out(chunk_row(c), slot).start(priority=1)
        state[1 + slot] = 1
        return carry

    lax.fori_loop(0, n_ch, body, 0)
    state[0] = g0 + n_ch
    state[3] = jnp.where(jnp.logical_and(n_ch > 0, feeds_next), 1, 0)

    @pl.when(tail > 0)
    def _():
        make_tail_out(tail_row).wait()

    @pl.when(step == n_steps - 1)
    def _():
        for slot in range(2):
            @pl.when(state[1 + slot] == 1)
            def _():
                make_out(0, slot).wait()
                state[1 + slot] = 0


def _rows_at(row0, n):
    return pl.ds(row0 if isinstance(row0, int) else pl.multiple_of(row0, SEG_ALIGN), n)


W_PARTS = 2


def _start_all(copies):
    for p, cp in enumerate(copies):
        cp.start(priority=p % 2)


def _moe_up_kernel(seg_ref, rows_ref, slack_ref, xs_ref, w_hbm, bg_ref, bl_ref, act_ref,
                   xbuf, obuf, xtail, otail, wbuf, wg_bf, wl_bf, state, isem, osem, tsem, wsem):
    f = pl.program_id(0)
    e = pl.program_id(1)
    start = seg_ref[e]
    rows = rows_ref[e]
    e_next = lax.rem(e + 1, N_EXPERTS)
    f_next = jnp.where(e == N_EXPERTS - 1, f + 1, f)
    step = f * N_EXPERTS + e
    n_steps = MOE_NF * N_EXPERTS
    wslot = lax.rem(step, 2)

    def weight_copies(expert, ftile, slot):
        band = D_MODEL // W_PARTS
        copies = []
        for t in range(2):
            col0 = pl.multiple_of((t * MOE_NF + ftile) * MOE_TF, MOE_TF)
            for p in range(W_PARTS):
                copies.append(pltpu.make_async_copy(
                    w_hbm.at[expert, pl.ds(p * band, band), pl.ds(col0, MOE_TF)],
                    wbuf.at[slot, t, pl.ds(p * band, band)], wsem.at[slot]))
        return copies

    @pl.when(step == 0)
    def _():
        _start_all(weight_copies(e, f, wslot))

    @pl.when(step + 1 < n_steps)
    def _():
        _start_all(weight_copies(e_next, f_next, 1 - wslot))

    def make_in(r0, slot):
        return pltpu.make_async_copy(xs_ref.at[_rows_at(r0, MOE_CH)], xbuf.at[slot], isem.at[slot])

    def make_out(r0, slot):
        return pltpu.make_async_copy(obuf.at[slot], act_ref.at[f, _rows_at(r0, MOE_CH)],
                                     osem.at[slot])

    def make_tail_in(r0):
        return pltpu.make_async_copy(xs_ref.at[_rows_at(r0, SEG_ALIGN)], xtail, tsem.at[0])

    def make_tail_out(r0):
        return pltpu.make_async_copy(otail, act_ref.at[f, _rows_at(r0, SEG_ALIGN)], tsem.at[1])

    def cast_weights():
        for cp in weight_copies(e, f, wslot):
            cp.wait()

        @pl.when(rows > 0)
        def _():
            wg_bf[...] = wbuf[wslot, 0].astype(BF16)
            wl_bf[...] = wbuf[wslot, 1].astype(BF16)

    def expert_mlp(words):
        lo, hi = _unpack_bf16_pairs(words)
        glu = (jnp.dot(lo, wg_bf[:HALF, :], preferred_element_type=F32)
               + jnp.dot(hi, wg_bf[HALF:, :], preferred_element_type=F32) + bg_ref[0])
        lin = (jnp.dot(lo, wl_bf[:HALF, :], preferred_element_type=F32)
               + jnp.dot(hi, wl_bf[HALF:, :], preferred_element_type=F32) + bl_ref[0])
        glu = jnp.minimum(glu, SWIGLU_LIMIT)
        lin = jnp.clip(lin, -SWIGLU_LIMIT, SWIGLU_LIMIT)
        return (glu * jax.nn.sigmoid(SWIGLU_ALPHA * glu) * (lin + 1.0)).astype(BF16)

    def compute_chunk(slot):
        obuf[slot] = expert_mlp(xbuf[slot])

    def compute_tail():
        otail[...] = expert_mlp(xtail[...])

    _stream_rows(step, n_steps, start, rows,
                 seg_ref[e_next], rows_ref[e_next], state,
                 make_in, make_out, make_tail_in, make_tail_out,
                 compute_chunk, compute_tail, cast_weights)

    @pl.when(e == N_EXPERTS - 1)
    def _():
        _zero_slack(slack_ref, otail, lambda rr: act_ref.at[f, rr], tsem.at[1])


def moe_up(seg_start, seg_rows, slack, xs, w1, b1):
    grid_spec = pltpu.PrefetchScalarGridSpec(
        num_scalar_prefetch=3,
        grid=(MOE_NF, N_EXPERTS),
        in_specs=[
            pl.BlockSpec(memory_space=pl.ANY),
            pl.BlockSpec(memory_space=pl.ANY),
            pl.BlockSpec((1, 1, MOE_TF), lambda f, e, s, r, z: (e, 0, f)),
            pl.BlockSpec((1, 1, MOE_TF), lambda f, e, s, r, z: (e, 0, MOE_NF + f)),
        ],
        out_specs=pl.BlockSpec(memory_space=pl.ANY),
        scratch_shapes=[
            pltpu.VMEM((2, MOE_CH, HALF), jnp.uint32),
            pltpu.VMEM((2, MOE_CH, MOE_TF), BF16),
            pltpu.VMEM((SEG_ALIGN, HALF), jnp.uint32),
            pltpu.VMEM((SEG_ALIGN, MOE_TF), BF16),
            pltpu.VMEM((2, 2, D_MODEL, MOE_TF), F32),
            pltpu.VMEM((D_MODEL, MOE_TF), BF16),
            pltpu.VMEM((D_MODEL, MOE_TF), BF16),
            pltpu.SMEM((4,), jnp.int32),
            pltpu.SemaphoreType.DMA((2,)),
            pltpu.SemaphoreType.DMA((2,)),
            pltpu.SemaphoreType.DMA((2,)),
            pltpu.SemaphoreType.DMA((2,)),
        ],
    )
    b13 = b1.reshape(N_EXPERTS, 1, 2 * D_FF)
    return pl.pallas_call(
        _moe_up_kernel,
        grid_spec=grid_spec,
        out_shape=jax.ShapeDtypeStruct((MOE_NF, MOE_R, MOE_TF), BF16),
        compiler_params=pltpu.CompilerParams(
            dimension_semantics=("arbitrary", "arbitrary"), vmem_limit_bytes=MOE_VMEM_LIMIT),
        name="moe_up",
    )(seg_start, seg_rows, slack, xs, w1, b13, b13)


def _moe_down_kernel(seg_ref, rows_ref, slack_ref, act_ref, w_hbm, b_ref, y_ref,
                     xbuf, obuf, xtail, otail, wbuf, w_bf, state, isem, osem, tsem, wsem):
    e = pl.program_id(0)
    start = seg_ref[e]
    rows = rows_ref[e]
    e_next = lax.rem(e + 1, N_EXPERTS)
    wslot = lax.rem(e, 2)

    def weight_copies(expert, slot):
        band = D_FF // (2 * W_PARTS)
        return [pltpu.make_async_copy(w_hbm.at[expert, pl.ds(p * band, band)],
                                      wbuf.at[slot, pl.ds(p * band, band)], wsem.at[slot])
                for p in range(2 * W_PARTS)]

    @pl.when(e == 0)
    def _():
        _start_all(weight_copies(e, wslot))

    @pl.when(e + 1 < N_EXPERTS)
    def _():
        _start_all(weight_copies(e_next, 1 - wslot))

    def make_in(r0, slot):
        return _CopyGroup([
            pltpu.make_async_copy(act_ref.at[j, _rows_at(r0, MOE_CH)],
                                  xbuf.at[slot, :, pl.ds(j * MOE_TF, MOE_TF)], isem.at[slot])
            for j in range(MOE_NF)])

    def make_out(r0, slot):
        return pltpu.make_async_copy(obuf.at[slot], y_ref.at[_rows_at(r0, MOE_CH)], osem.at[slot])

    def make_tail_in(r0):
        return _CopyGroup([
            pltpu.make_async_copy(act_ref.at[j, _rows_at(r0, SEG_ALIGN)],
                                  xtail.at[:, pl.ds(j * MOE_TF, MOE_TF)], tsem.at[0])
            for j in range(MOE_NF)])

    def make_tail_out(r0):
        return pltpu.make_async_copy(otail, y_ref.at[_rows_at(r0, SEG_ALIGN)], tsem.at[1])

    def cast_weights():
        for cp in weight_copies(e, wslot):
            cp.wait()

        @pl.when(rows > 0)
        def _():
            w_bf[...] = wbuf[wslot].astype(BF16)

    def expert_out(a):
        return _pack_bf16_pairs(jnp.dot(a, w_bf[...], preferred_element_type=F32) + b_ref[0])

    def compute_chunk(slot):
        obuf[slot] = expert_out(xbuf[slot])

    def compute_tail():
        otail[...] = expert_out(xtail[...])

    _stream_rows(e, N_EXPERTS, start, rows, seg_ref[e_next], rows_ref[e_next], state,
                 make_in, make_out, make_tail_in, make_tail_out,
                 compute_chunk, compute_tail, cast_weights)

    @pl.when(e == N_EXPERTS - 1)
    def _():
        _zero_slack(slack_ref, otail, lambda rr: y_ref.at[rr], tsem.at[1])


def moe_down(seg_start, seg_rows, slack, act, w2, b2):
    grid_spec = pltpu.PrefetchScalarGridSpec(
        num_scalar_prefetch=3,
        grid=(N_EXPERTS,),
        in_specs=[
            pl.BlockSpec(memory_space=pl.ANY),
            pl.BlockSpec(memory_space=pl.ANY),
            pl.BlockSpec((1, 1, D_MODEL), lambda e, s, r, z: (e, 0, 0)),
        ],
        out_specs=pl.BlockSpec(memory_space=pl.ANY),
        scratch_shapes=[
            pltpu.VMEM((2, MOE_CH, D_FF), BF16),
            pltpu.VMEM((2, MOE_CH, HALF), jnp.uint32),
            pltpu.VMEM((SEG_ALIGN, D_FF), BF16),
            pltpu.VMEM((SEG_ALIGN, HALF), jnp.uint32),
            pltpu.VMEM((2, D_FF, D_MODEL), F32),
            pltpu.VMEM((D_FF, D_MODEL), BF16),
            pltpu.SMEM((4,), jnp.int32),
            pltpu.SemaphoreType.DMA((2,)),
            pltpu.SemaphoreType.DMA((2,)),
            pltpu.SemaphoreType.DMA((2,)),
            pltpu.SemaphoreType.DMA((2,)),
        ],
    )
    return pl.pallas_call(
        _moe_down_kernel,
        grid_spec=grid_spec,
        out_shape=jax.ShapeDtypeStruct((MOE_R, HALF), jnp.uint32),
        compiler_params=pltpu.CompilerParams(
            dimension_semantics=("arbitrary",), vmem_limit_bytes=MOE_VMEM_LIMIT),
        name="moe_down",
    )(seg_start, seg_rows, slack, act, w2, b2.reshape(N_EXPERTS, 1, D_MODEL))


COMB_TM = 128
COMB_NT = N_TOK // COMB_TM


def _combine_kernel(dest_ref, y_ref, w_ref, h_ref, g_ref, o_ref, buf, sem):
    i = pl.program_id(0)

    def fetch(tile, slot):
        def tok(t, carry):
            a = (tile * COMB_TM + t) * TOP_K
            for kk in range(TOP_K):
                pltpu.make_async_copy(y_ref.at[pl.ds(dest_ref[a + kk], 1)],
                                      buf.at[slot, kk, pl.ds(t, 1)], sem.at[slot]).start(
                                          priority=kk % 2)
            return carry

        lax.fori_loop(0, COMB_TM, tok, 0, unroll=4)

    @pl.when(i == 0)
    def _():
        fetch(0, 0)

    slot = lax.rem(i, 2)

    @pl.when(i + 1 < COMB_NT)
    def _():
        fetch(i + 1, 1 - slot)

    for kk in range(TOP_K):
        pltpu.make_async_copy(y_ref.at[pl.ds(0, COMB_TM)], buf.at[slot, kk], sem.at[slot]).wait()
    w = w_ref[...]
    lo = h_ref[:, :HALF]
    hi = h_ref[:, HALF:]
    for kk in range(TOP_K):
        words = buf[slot, kk]
        wk = w[:, kk:kk + 1]
        lo = lo + wk * lax.bitcast_convert_type(lax.shift_left(words, jnp.uint32(16)), F32)
        hi = hi + wk * lax.bitcast_convert_type(words & jnp.uint32(0xFFFF0000), F32)
    ms = (jnp.sum(lo * lo, axis=-1, keepdims=True)
          + jnp.sum(hi * hi, axis=-1, keepdims=True)) * (1.0 / D_MODEL)
    inv = lax.rsqrt(ms + EPS)
    o_ref[:, :HALF] = lo * inv * g_ref[:, :HALF]
    o_ref[:, HALF:] = hi * inv * g_ref[:, HALF:]


def combine(dest, y, weight, h2, gain):
    grid_spec = pltpu.PrefetchScalarGridSpec(
        num_scalar_prefetch=1,
        grid=(COMB_NT,),
        in_specs=[
            pl.BlockSpec(memory_space=pl.ANY),
            pl.BlockSpec((COMB_TM, 128), lambda i, d: (i, 0)),
            pl.BlockSpec((COMB_TM, D_MODEL), lambda i, d: (i, 0)),
            pl.BlockSpec((1, D_MODEL), lambda i, d: (0, 0)),
        ],
        out_specs=pl.BlockSpec((COMB_TM, D_MODEL), lambda i, d: (i, 0)),
        scratch_shapes=[pltpu.VMEM((2, TOP_K, COMB_TM, HALF), jnp.uint32),
                        pltpu.SemaphoreType.DMA((2,))],
    )
    return pl.pallas_call(
        _combine_kernel,
        grid_spec=grid_spec,
        out_shape=jax.ShapeDtypeStruct((N_TOK, D_MODEL), F32),
        compiler_params=_cparams(1),
        name="moe_combine",
    )(dest, y, weight, h2, gain.reshape(1, D_MODEL))


def kernel(x, meta_tokens, rel_bias, norm_mix, w_in, conv_w, gate_bias_m, lambda_params, subln_da,
           w_branch_da, w_branch_m, w_gate, b_gate, w_out, norm_ffn, w_router, b_router,
           w1, b1, w2, b2, norm_final):
    layer = 0
    xn_pad, xn_real = norm_in(x, meta_tokens, norm_mix[layer])
    xn_pad2 = xn_pad.reshape(BATCH * LP, D_MODEL)
    proj = matmul(xn_pad2, w_in[layer], n_cols=PROJ_COLS, tm=768, tn=1024, name="proj_in")
    proj3 = proj.reshape(BATCH, LP, PROJ_COLS)
    w_g = jnp.pad(w_in[layer][:, COL_M_G:], ((0, 0), (0, 128 - 4 * H_M)))
    mg = matmul(xn_pad2, w_g, n_cols=128, tm=768, tn=128, out_dtype=F32, name="proj_gates")
    gate = matmul(xn_real.reshape(N_TOK, D_MODEL), w_gate[layer], n_cols=2 * D_MODEL,
                  tm=1024, tn=1024, bias=b_gate[layer], act="sigmoid", name="mix_gate")

    tab, consts = _bias_tables(rel_bias)
    y_da = diff_attention(consts, proj3, tab, lambda_params[layer], subln_da[layer])

    qk_m = conv_qk(proj3, conv_w[layer])
    gp = gate_prep(mg.reshape(BATCH, LP, 128), gate_bias_m[layer])
    gp4 = gp.reshape(BATCH, LP, 4, H_M)
    gcol = jnp.transpose(gp4, (0, 3, 1, 2))
    grow = jnp.transpose(gp4, (0, 3, 2, 1))
    y_m = mlstm(qk_m, proj3, gcol, grow)

    mixed = branch_mix(y_da.reshape(N_TOK, H_DA * DV_DA), y_m.reshape(N_TOK, H_M * DV_M),
                       w_branch_da[layer], w_branch_m[layer], gate)
    h2 = matmul(mixed, w_out[layer], n_cols=D_MODEL, tm=1024, tn=1024,
                res=x.reshape(N_TOK, D_MODEL), out_dtype=F32, name="out_proj")

    hn_packed, top_e, weight, rank, counts = ffn_prep(
        h2, norm_ffn[layer], w_router[layer], b_router[layer])
    dest, seg_start, seg_rows, pad0, padn, slack = _plan(
        counts, top_e[:, :TOP_K], rank[:, :TOP_K])
    xs = dispatch(dest, pad0, padn, slack, hn_packed)
    act = moe_up(seg_start, seg_rows, slack, xs, w1[layer], b1[layer])
    y = moe_down(seg_start, seg_rows, slack, act, w2[layer], b2[layer])
    out = combine(dest, y, weight, h2, norm_final)
    return out.reshape(BATCH, SEQ, D_MODEL)
```

```python
import functools
import math

import jax
import jax.numpy as jnp
from jax import lax
from jax.experimental import pallas as pl
from jax.experimental.pallas import tpu as pltpu

F32 = jnp.float32
BF16 = jnp.bfloat16

D_MODEL = 2048
BATCH = 2
SEQ = 4096
N_META = 16
BLOCK = 128
PAD = (-N_META) % BLOCK
LP = PAD + N_META + SEQ
NBLK = LP // BLOCK
EPS = 1e-6
NEG_INF = -1e30

H_DA = 4
DK_DA = 128
DV_DA = 256
H_M = 4
DK_M = 128
DV_M = 256
CONV_W = 5
N_BUCKETS = 32
MAX_DISTANCE = 128
N_EXPERTS = 32
TOP_K = 4
D_FF = 2048
SWIGLU_ALPHA = 1.702
SWIGLU_LIMIT = 7.0
LAMBDA_INIT = 0.8 - 0.6 * math.exp(-0.3 * 0)

COL_DA_Q = 0
COL_DA_K = 1024
COL_DA_V = 2048
COL_M_Q = 3072
COL_M_K = 3584
COL_M_V = 4096
COL_M_O = 5120
COL_M_G = 6144
PROJ_COLS = 6144

N_TOK = BATCH * SEQ
N_ASSIGN = N_TOK * TOP_K

VMEM_LIMIT = 52 * 1024 * 1024
MOE_VMEM_LIMIT = 58 * 1024 * 1024


def _cparams(n_axes):
    return pltpu.CompilerParams(
        dimension_semantics=("arbitrary",) * n_axes, vmem_limit_bytes=VMEM_LIMIT)


def _rms(v, gain):
    ms = jnp.mean(v * v, axis=-1, keepdims=True)
    return v * lax.rsqrt(ms + EPS) * gain


def _norm_in_kernel(x_ref, meta_ref, g_ref, pad_ref, real_ref):
    j = pl.program_id(1)
    g = g_ref[...]

    @pl.when(j == 0)
    def _():
        pad_ref[0, :PAD, :] = jnp.zeros((PAD, D_MODEL), BF16)
        pad_ref[0, PAD:, :] = _rms(meta_ref[...], g).astype(BF16)

    @pl.when(j > 0)
    def _():
        y = _rms(x_ref[0], g).astype(BF16)
        pad_ref[0] = y
        real_ref[0] = y


def norm_in(x, meta, gain):
    return pl.pallas_call(
        _norm_in_kernel,
        grid=(BATCH, NBLK),
        in_specs=[
            pl.BlockSpec((1, BLOCK, D_MODEL), lambda b, j: (b, jnp.maximum(j - 1, 0), 0)),
            pl.BlockSpec((N_META, D_MODEL), lambda b, j: (0, 0)),
            pl.BlockSpec((1, D_MODEL), lambda b, j: (0, 0)),
        ],
        out_specs=[
            pl.BlockSpec((1, BLOCK, D_MODEL), lambda b, j: (b, j, 0)),
            pl.BlockSpec((1, BLOCK, D_MODEL), lambda b, j: (b, jnp.maximum(j - 1, 0), 0)),
        ],
        out_shape=[
            jax.ShapeDtypeStruct((BATCH, LP, D_MODEL), BF16),
            jax.ShapeDtypeStruct((BATCH, SEQ, D_MODEL), BF16),
        ],
        compiler_params=_cparams(2),
        name="norm_in",
    )(x, meta, gain.reshape(1, D_MODEL))


def _mm_kernel(*refs, has_bias, has_res, act):
    x_ref, w_ref = refs[0], refs[1]
    pos = 2
    b_ref = r_ref = None
    if has_bias:
        b_ref = refs[pos]
        pos += 1
    if has_res:
        r_ref = refs[pos]
        pos += 1
    o_ref, wbf_ref = refs[pos], refs[pos + 1]

    @pl.when(pl.program_id(1) == 0)
    def _():
        wbf_ref[...] = w_ref[...].astype(BF16)

    acc = jnp.dot(x_ref[...], wbf_ref[...], preferred_element_type=F32)
    if has_bias:
        acc = acc + b_ref[...]
    if act == "sigmoid":
        acc = jax.nn.sigmoid(acc)
    if has_res:
        acc = acc + r_ref[...]
    o_ref[...] = acc.astype(o_ref.dtype)


def matmul(x, w, *, n_cols, col_block0=0, tm, tn, bias=None, res=None, act=None,
           out_dtype=BF16, name):
    m, k = x.shape
    in_specs = [
        pl.BlockSpec((tm, k), lambda j, i: (i, 0)),
        pl.BlockSpec((k, tn), lambda j, i: (0, j + col_block0)),
    ]
    args = [x, w]
    if bias is not None:
        in_specs.append(pl.BlockSpec((1, tn), lambda j, i: (0, j)))
        args.append(bias.reshape(1, n_cols))
    if res is not None:
        in_specs.append(pl.BlockSpec((tm, tn), lambda j, i: (i, j)))
        args.append(res)
    return pl.pallas_call(
        functools.partial(_mm_kernel, has_bias=bias is not None, has_res=res is not None, act=act),
        grid=(n_cols // tn, m // tm),
        in_specs=in_specs,
        out_specs=pl.BlockSpec((tm, tn), lambda j, i: (i, j)),
        out_shape=jax.ShapeDtypeStruct((m, n_cols), out_dtype),
        scratch_shapes=[pltpu.VMEM((k, tn), BF16)],
        compiler_params=_cparams(2),
        name=name,
    )(*args)


def _conv_kernel(p_ref, w_ref, o_ref):
    c = pl.program_id(1)
    x = p_ref[0].astype(F32)
    w = w_ref[...]
    half = CONV_W // 2
    acc = w[half:half + 1, :] * x
    for j in range(CONV_W):
        if j != half:
            acc = acc + w[j:j + 1, :] * pltpu.roll(x, (half - j) % LP, axis=0)
    y = acc * jax.nn.sigmoid(acc)
    rows = lax.broadcasted_iota(jnp.int32, (LP, 1), 0)
    y = jnp.where(rows >= PAD, y, 0.0)
    scale = jnp.where(c < 2, DK_M ** -0.5, 1.0).astype(F32)
    o_ref[0] = (y * scale).astype(BF16)


def conv_qk(proj3, conv_w):
    cw = 256
    return pl.pallas_call(
        _conv_kernel,
        grid=(BATCH, (2 * H_M * DK_M) // cw),
        in_specs=[
            pl.BlockSpec((1, LP, cw), lambda b, c: (b, 0, COL_M_Q // cw + c)),
            pl.BlockSpec((CONV_W, cw), lambda b, c: (0, c)),
        ],
        out_specs=pl.BlockSpec((1, LP, cw), lambda b, c: (b, 0, c)),
        out_shape=jax.ShapeDtypeStruct((BATCH, LP, 2 * H_M * DK_M), BF16),
        compiler_params=_cparams(2),
        name="conv_qk",
    )(proj3, conv_w)


def _split_dot(tri, v):
    hi = v.astype(BF16)
    r1 = v - hi.astype(F32)
    mid = r1.astype(BF16)
    lo = (r1 - mid.astype(F32)).astype(BF16)
    return (jnp.dot(tri, hi, preferred_element_type=F32)
            + jnp.dot(tri, mid, preferred_element_type=F32)
            + jnp.dot(tri, lo, preferred_element_type=F32))


def _gate_kernel(g_ref, bias_ref, o_ref):
    ti = lax.broadcasted_iota(jnp.int32, (BLOCK, BLOCK), 0)
    ui = lax.broadcasted_iota(jnp.int32, (BLOCK, BLOCK), 1)
    tril = jnp.where(ui <= ti, 1.0, 0.0).astype(BF16)
    triu = jnp.where(ui >= ti, 1.0, 0.0).astype(BF16)
    ch = lax.broadcasted_iota(jnp.int32, (BLOCK, 4 * H_M), 1)
    typ = lax.shift_right_logical(ch, 2)
    rloc = lax.broadcasted_iota(jnp.int32, (BLOCK, 4 * H_M), 0)

    def body(c, carry):
        r0 = pl.multiple_of(c * BLOCK, BLOCK)
        g = g_ref[0, pl.ds(r0, BLOCK), :][:, :4 * H_M] + bias_ref[...]
        valid = (rloc + r0) >= PAD
        lsig = -(jnp.maximum(-g, 0.0) + jnp.log1p(jnp.exp(-jnp.abs(g))))
        lf = jnp.where(valid, lsig, 0.0)
        cum = _split_dot(tril, lf)
        rcum = _split_dot(triu, lf)
        li = jnp.where(valid, g, -jnp.inf)
        out = jnp.where(typ == 1, cum, jnp.where(typ == 3, rcum, li))
        o_ref[0, pl.ds(r0, BLOCK), :] = out
        return carry

    lax.fori_loop(0, NBLK, body, 0)


def gate_prep(mg3, gate_bias):
    return pl.pallas_call(
        _gate_kernel,
        grid=(BATCH,),
        in_specs=[
            pl.BlockSpec((1, LP, 128), lambda b: (b, 0, 0)),
            pl.BlockSpec((1, 4 * H_M), lambda b: (0, 0)),
        ],
        out_specs=pl.BlockSpec((1, LP, 4 * H_M), lambda b: (b, 0, 0)),
        out_shape=jax.ShapeDtypeStruct((BATCH, LP, 4 * H_M), F32),
        compiler_params=_cparams(1),
        name="gate_prep",
    )(mg3, gate_bias.reshape(1, 4 * H_M))


MLSTM_HP = 2
MLSTM_MID = NBLK // 2


def _mlstm_kernel(q_ref, k_ref, v_ref, o_ref, gc_ref, gr_ref, y_ref,
                  hs_ref, c_ref, n_ref, m_ref):
    c_ref[...] = jnp.zeros_like(c_ref)
    n_ref[...] = jnp.zeros_like(n_ref)
    m_ref[...] = jnp.zeros_like(m_ref)
    ti = lax.broadcasted_iota(jnp.int32, (BLOCK, BLOCK), 0)
    si = lax.broadcasted_iota(jnp.int32, (BLOCK, BLOCK), 1)
    mask_f = si <= ti
    mask_b = si >= ti

    def chain(c, hl, bwd, final):
        idx = 2 * hl + bwd
        r0 = c * BLOCK if isinstance(c, int) else pl.multiple_of(c * BLOCK, BLOCK)
        q = q_ref[0, pl.ds(r0, BLOCK), hl * DK_M:(hl + 1) * DK_M]
        k = k_ref[0, pl.ds(r0, BLOCK), hl * DK_M:(hl + 1) * DK_M]
        v = v_ref[0, pl.ds(r0, BLOCK), hl * DV_M:(hl + 1) * DV_M]
        gc = gc_ref[0, hl, pl.ds(r0, BLOCK), :]
        gr = gr_ref[0, hl, :, pl.ds(r0, BLOCK)]
        a = 2 * bwd
        li_c, b_c = gc[:, a:a + 1], gc[:, a + 1:a + 2]
        li_r, b_r = gr[a:a + 1, :], gr[a + 1:a + 2, :]
        b_end = b_c[0:1, :] if bwd else b_c[BLOCK - 1:BLOCK, :]
        m_prev = m_ref[idx][:, 0:1]
        cst = c_ref[idx]
        nst = n_ref[idx]
        dmat = jnp.where(mask_b if bwd else mask_f, b_c - b_r + li_r, -jnp.inf)
        inter = b_c + m_prev
        m_t = jnp.maximum(inter, jnp.max(dmat, axis=1, keepdims=True))
        w_inter = jnp.exp(inter - m_t)
        qk = lax.dot_general(q, k, (((1,), (1,)), ((), ())), preferred_element_type=F32)
        s = qk * jnp.exp(dmat - m_t)
        num = (w_inter * jnp.dot(q, cst.astype(BF16), preferred_element_type=F32)
               + jnp.dot(s.astype(BF16), v, preferred_element_type=F32))
        den = jnp.sum(s + w_inter * (q.astype(F32) * nst), axis=1, keepdims=True)
        h = num / jnp.maximum(jnp.abs(den), jnp.exp(-m_t))
        cols = slice(hl * DV_M, (hl + 1) * DV_M)
        if final:
            og = jax.nn.sigmoid(o_ref[0, pl.ds(r0, BLOCK), cols].astype(F32))
            y_ref[0, pl.ds(r0 - BLOCK, BLOCK), cols] = (
                og * (hs_ref[pl.ds(r0, BLOCK), cols] + h)).astype(BF16)
        else:
            hs_ref[pl.ds(r0, BLOCK), cols] = h
        ldec_c = b_end - b_c + li_c
        ldec_r = b_end - b_r + li_r
        m_new = jnp.maximum(b_end + m_prev, jnp.max(ldec_r, axis=1, keepdims=True))
        w_c = jnp.exp(b_end + m_prev - m_new)
        w_s = jnp.exp(ldec_c - m_new)
        wv = (w_s * v.astype(F32)).astype(BF16)
        c_ref[idx] = w_c * cst + lax.dot_general(
            k, wv, (((0,), (0,)), ((), ())), preferred_element_type=F32)
        n_ref[idx] = w_c * nst + jnp.sum(w_s * k.astype(F32), axis=0, keepdims=True)
        m_ref[idx] = jnp.broadcast_to(m_new, (1, BLOCK))

    def first_half(i, carry):
        for hl in range(MLSTM_HP):
            chain(i, hl, 0, False)
            chain(NBLK - 1 - i, hl, 1, False)
        return carry

    def second_half(i, carry):
        for hl in range(MLSTM_HP):
            chain(i, hl, 0, True)
            chain(NBLK - 1 - i, hl, 1, True)
        return carry

    lax.fori_loop(0, MLSTM_MID, first_half, 0)
    for hl in range(MLSTM_HP):
        chain(MLSTM_MID, hl, 0, False)
        chain(MLSTM_MID, hl, 1, True)
    lax.fori_loop(MLSTM_MID + 1, NBLK - 1, second_half, 0)
    for hl in range(MLSTM_HP):
        chain(NBLK - 1, hl, 0, True)


def mlstm(qk_m, proj3, gcol, grow):
    hp = MLSTM_HP
    kw, vw = hp * DK_M, hp * DV_M
    return pl.pallas_call(
        _mlstm_kernel,
        grid=(BATCH, H_M // hp),
        in_specs=[
            pl.BlockSpec((1, LP, kw), lambda b, g: (b, 0, g)),
            pl.BlockSpec((1, LP, kw), lambda b, g: (b, 0, (H_M * DK_M) // kw + g)),
            pl.BlockSpec((1, LP, vw), lambda b, g: (b, 0, COL_M_V // vw + g)),
            pl.BlockSpec((1, LP, vw), lambda b, g: (b, 0, COL_M_O // vw + g)),
            pl.BlockSpec((1, hp, LP, 4), lambda b, g: (b, g, 0, 0)),
            pl.BlockSpec((1, hp, 4, LP), lambda b, g: (b, g, 0, 0)),
        ],
        out_specs=pl.BlockSpec((1, SEQ, vw), lambda b, g: (b, 0, g)),
        out_shape=jax.ShapeDtypeStruct((BATCH, SEQ, H_M * DV_M), BF16),
        scratch_shapes=[
            pltpu.VMEM((LP, vw), F32),
            pltpu.VMEM((2 * hp, DK_M, DV_M), F32),
            pltpu.VMEM((2 * hp, 1, DK_M), F32),
            pltpu.VMEM((2 * hp, 1, BLOCK), F32),
        ],
        compiler_params=_cparams(2),
        name="mlstm",
    )(qk_m, qk_m, proj3, proj3, gcol, grow)


ATT_QB = 2
ATT_TQ = ATT_QB * BLOCK
ATT_BAND = (ATT_QB + 2) * BLOCK
ATT_GROUPS = (6, 6, 6, 6, 5)
assert BLOCK >= MAX_DISTANCE and ATT_BAND + sum(ATT_GROUPS) * BLOCK == LP


def _attn_kernel(c_ref, qa_ref, qb_ref, k1_ref, v1_ref, tab_ref, lam_ref, sg_ref, o_ref,
                 s_ref, k_ref, v_ref):
    h = pl.program_id(1)
    qb = ATT_QB * pl.program_id(2) + 1

    @pl.when(pl.program_id(2) == 0)
    def _():
        for rep in range(2):
            k_ref[0, rep * LP:(rep + 1) * LP, :] = k1_ref[0]
            v_ref[0, rep * LP:(rep + 1) * LP, :] = v1_ref[0]

    q = jnp.concatenate([qa_ref[0], qb_ref[0]], axis=0)
    scale = DK_DA ** -0.5
    c_neg = c_ref[h, 0]
    c_pos = c_ref[h, 1]
    lp = lam_ref[...]
    lam = (jnp.exp(jnp.sum(lp[0:1] * lp[1:2], axis=1, keepdims=True))
           - jnp.exp(jnp.sum(lp[2:3] * lp[3:4], axis=1, keepdims=True)) + LAMBDA_INIT)

    groups = [((qb - 1) * BLOCK, ATT_BAND, 0)]
    col = ATT_BAND
    for nblk in ATT_GROUPS:
        groups.append(((qb - 1) * BLOCK + col, nblk * BLOCK, col))
        col += nblk * BLOCK

    def lane_fold(acc, t, op):
        for j in range(t.shape[1] // BLOCK):
            piece = t[:, j * BLOCK:(j + 1) * BLOCK]
            acc = piece if acc is None else op(acc, piece)
        return acc

    mx = [None, None]
    for gi, (koff, width, col0) in enumerate(groups):
        koff = pl.multiple_of(koff, BLOCK)
        if gi == 0:
            bias = tab_ref[0, 0]
        else:
            kpos = koff + lax.broadcasted_iota(jnp.int32, (1, width), 1)
            bias = jnp.where(kpos < LP, c_pos, jnp.where(kpos < LP + PAD, NEG_INF, c_neg))
        for m in range(2):
            kk = k_ref[0, pl.ds(koff, width), m * DK_DA:(m + 1) * DK_DA]
            s = lax.dot_general(q[:, m * DK_DA:(m + 1) * DK_DA], kk, (((1,), (1,)), ((), ())),
                                preferred_element_type=F32) * scale + bias
            s_ref[m, :, col0:col0 + width] = s
            mx[m] = lane_fold(mx[m], s, jnp.maximum)
    row_max = [jnp.max(mx[m], axis=1, keepdims=True) for m in range(2)]

    lsum = [None, None]
    acc = [None, None]
    for koff, width, col0 in groups:
        koff = pl.multiple_of(koff, BLOCK)
        vv = v_ref[0, pl.ds(koff, width), :]
        for m in range(2):
            p = jnp.exp(s_ref[m, :, col0:col0 + width] - row_max[m])
            lsum[m] = lane_fold(lsum[m], p, jnp.add)
            pv = jnp.dot(p.astype(BF16), vv, preferred_element_type=F32)
            acc[m] = pv if acc[m] is None else acc[m] + pv
    l1 = jnp.sum(lsum[0], axis=1, keepdims=True)
    l2 = jnp.sum(lsum[1], axis=1, keepdims=True)
    o = acc[0] / l1 - lam * (acc[1] / l2)
    o_ref[0] = (_rms(o, sg_ref[...]) * (1.0 - LAMBDA_INIT)).astype(BF16)


def diff_attention(consts, proj3, tab, lam_params, subln):
    nq = SEQ // ATT_TQ
    kblk0 = COL_DA_K // (2 * DK_DA)
    vblk0 = COL_DA_V // DV_DA

    def tab_map(b, h, i):
        case = jnp.where(i == 0, 0, jnp.where(i == nq - 1, 2, 1))
        return (h, case, 0, 0)

    return pl.pallas_call(
        _attn_kernel,
        grid=(BATCH, H_DA, nq),
        in_specs=[
            pl.BlockSpec(memory_space=pltpu.SMEM),
            pl.BlockSpec((1, BLOCK, 2 * DK_DA), lambda b, h, i: (b, ATT_QB * i + 1, h)),
            pl.BlockSpec((1, BLOCK, 2 * DK_DA), lambda b, h, i: (b, ATT_QB * i + 2, h)),
            pl.BlockSpec((1, LP, 2 * DK_DA), lambda b, h, i: (b, 0, kblk0 + h)),
            pl.BlockSpec((1, LP, DV_DA), lambda b, h, i: (b, 0, vblk0 + h)),
            pl.BlockSpec((1, 1, ATT_TQ, ATT_BAND), tab_map),
            pl.BlockSpec((4, DK_DA), lambda b, h, i: (0, 0)),
            pl.BlockSpec((1, DV_DA), lambda b, h, i: (0, 0)),
        ],
        out_specs=pl.BlockSpec((1, ATT_TQ, DV_DA), lambda b, h, i: (b, i, h)),
        out_shape=jax.ShapeDtypeStruct((BATCH, SEQ, H_DA * DV_DA), BF16),
        scratch_shapes=[pltpu.VMEM((2, ATT_TQ, LP), F32),
                        pltpu.VMEM((1, 2 * LP, 2 * DK_DA), BF16),
                        pltpu.VMEM((1, 2 * LP, DV_DA), BF16)],
        compiler_params=_cparams(3),
        name="diff_attn",
    )(consts, proj3, proj3, proj3, proj3, tab, lam_params, subln.reshape(1, DV_DA))


def _bias_tables(rel_bias):
    rb = rel_bias.astype(F32)
    i = jnp.arange(ATT_TQ, dtype=jnp.int32)[:, None]
    j = jnp.arange(ATT_BAND, dtype=jnp.int32)[None, :]
    rel = j - BLOCK - i
    nb = N_BUCKETS // 2
    max_exact = nb // 2
    n = jnp.abs(rel)
    nf = jnp.maximum(n, 1).astype(F32)
    large = max_exact + (jnp.log(nf / max_exact) / math.log(MAX_DISTANCE / max_exact)
                         * (nb - max_exact)).astype(jnp.int32)
    large = jnp.minimum(large, nb - 1)
    bucket = jnp.where(rel > 0, nb, 0) + jnp.where(n < max_exact, n, large)
    hit = bucket[None, :, :, None] == jnp.arange(N_BUCKETS, dtype=jnp.int32)
    gen = jnp.sum(jnp.where(hit, rb.T[:, None, None, :], 0.0), axis=-1)
    c_neg = rb[nb - 1]
    c_pos = rb[N_BUCKETS - 1]
    jj = j[None]
    first = jnp.where(jj < PAD, NEG_INF, gen)
    wrap0 = ATT_BAND - BLOCK
    wrapped = jnp.where(jj - wrap0 < PAD, NEG_INF, c_neg[:, None, None])
    last = jnp.where(jj >= wrap0, wrapped, gen)
    tab = jnp.stack([first, gen, last], axis=1)
    consts = jnp.stack([c_neg, c_pos], axis=1)
    return tab, consts


def _mix_kernel(ya_ref, ym_ref, wa_ref, wm_ref, ga_ref, gm_ref, o_ref, wa_bf, wm_bf):
    @pl.when(pl.program_id(1) == 0)
    def _():
        wa_bf[...] = wa_ref[...].astype(BF16)
        wm_bf[...] = wm_ref[...].astype(BF16)

    a = jnp.dot(ya_ref[...], wa_bf[...], preferred_element_type=F32)
    m = jnp.dot(ym_ref[...], wm_bf[...], preferred_element_type=F32)
    o_ref[...] = (ga_ref[...].astype(F32) * a + gm_ref[...].astype(F32) * m).astype(BF16)


def branch_mix(y_da, y_m, w_da, w_m, gate, *, tm=512, tn=1024):
    m, k = y_da.shape
    nj = D_MODEL // tn
    return pl.pallas_call(
        _mix_kernel,
        grid=(nj, m // tm),
        in_specs=[
            pl.BlockSpec((tm, k), lambda j, i: (i, 0)),
            pl.BlockSpec((tm, k), lambda j, i: (i, 0)),
            pl.BlockSpec((k, tn), lambda j, i: (0, j)),
            pl.BlockSpec((k, tn), lambda j, i: (0, j)),
            pl.BlockSpec((tm, tn), lambda j, i: (i, j)),
            pl.BlockSpec((tm, tn), lambda j, i: (i, nj + j)),
        ],
        out_specs=pl.BlockSpec((tm, tn), lambda j, i: (i, j)),
        out_shape=jax.ShapeDtypeStruct((m, D_MODEL), BF16),
        scratch_shapes=[pltpu.VMEM((k, tn), BF16), pltpu.VMEM((k, tn), BF16)],
        compiler_params=_cparams(2),
        name="branch_mix",
    )(y_da, y_m, w_da, w_m, gate, gate)


FFN_TM = 1024
HALF = D_MODEL // 2


def _pack_bf16_pairs(v):
    lo = lax.bitcast_convert_type(v[:, :HALF].astype(BF16).astype(F32), jnp.uint32)
    hi = lax.bitcast_convert_type(v[:, HALF:].astype(BF16).astype(F32), jnp.uint32)
    return (hi & jnp.uint32(0xFFFF0000)) | lax.shift_right_logical(lo, jnp.uint32(16))


def _unpack_bf16_pairs(w):
    lo = lax.bitcast_convert_type(lax.shift_left(w, jnp.uint32(16)), F32).astype(BF16)
    hi = lax.bitcast_convert_type(w & jnp.uint32(0xFFFF0000), F32).astype(BF16)
    return lo, hi


def _ffn_prep_kernel(h_ref, g_ref, wr_ref, br_ref, hn_ref, e_ref, w_ref, r_ref, cnt_ref, base_ref):
    @pl.when(pl.program_id(0) == 0)
    def _():
        base_ref[...] = jnp.zeros_like(base_ref)

    hn = _rms(h_ref[...], g_ref[...])
    hn_ref[...] = _pack_bf16_pairs(hn)
    logits = jnp.dot(hn, wr_ref[...], preferred_element_type=F32,
                     precision=lax.Precision.HIGHEST) + br_ref[...]
    lane = lax.broadcasted_iota(jnp.int32, (FFN_TM, N_EXPERTS), 1)
    lane_o = lax.broadcasted_iota(jnp.int32, (FFN_TM, 128), 1)
    ti = lax.broadcasted_iota(jnp.int32, (FFN_TM, FFN_TM), 0)
    ui = lax.broadcasted_iota(jnp.int32, (FFN_TM, FFN_TM), 1)
    tril = jnp.where(ui <= ti, 1.0, 0.0).astype(BF16)
    e_out = jnp.zeros((FFN_TM, 128), jnp.int32)
    r_out = jnp.zeros((FFN_TM, 128), jnp.int32)
    l_out = jnp.full((FFN_TM, 128), -jnp.inf, F32)
    base = base_ref[...]
    l = logits
    for kk in range(TOP_K):
        mk = jnp.max(l, axis=1, keepdims=True)
        ik = jnp.min(jnp.where(l == mk, lane, N_EXPERTS), axis=1, keepdims=True)
        hit = lane == ik
        oh = jnp.where(hit, 1.0, 0.0)
        cum = jnp.dot(tril, oh.astype(BF16), preferred_element_type=F32)
        rank = jnp.sum(oh * (cum + base), axis=1, keepdims=True) - 1.0
        base = base + jnp.sum(oh, axis=0, keepdims=True)
        e_out = jnp.where(lane_o == kk, ik, e_out)
        r_out = jnp.where(lane_o == kk, rank.astype(jnp.int32), r_out)
        l_out = jnp.where(lane_o == kk, mk, l_out)
        l = jnp.where(hit, -jnp.inf, l)
    base_ref[...] = base
    cnt_ref[...] = base
    ex = jnp.exp(l_out - jnp.max(l_out, axis=1, keepdims=True))
    e_ref[...] = e_out
    r_ref[...] = r_out
    w_ref[...] = ex / jnp.sum(ex, axis=1, keepdims=True)


def ffn_prep(h2, gain, w_router, b_router):
    row = lambda i: (i, 0)
    fixed = lambda i: (0, 0)
    return pl.pallas_call(
        _ffn_prep_kernel,
        grid=(N_TOK // FFN_TM,),
        in_specs=[
            pl.BlockSpec((FFN_TM, D_MODEL), row),
            pl.BlockSpec((1, D_MODEL), fixed),
            pl.BlockSpec((D_MODEL, N_EXPERTS), fixed),
            pl.BlockSpec((1, N_EXPERTS), fixed),
        ],
        out_specs=[
            pl.BlockSpec((FFN_TM, HALF), row),
            pl.BlockSpec((FFN_TM, 128), row),
            pl.BlockSpec((FFN_TM, 128), row),
            pl.BlockSpec((FFN_TM, 128), row),
            pl.BlockSpec((1, N_EXPERTS), fixed),
        ],
        out_shape=[
            jax.ShapeDtypeStruct((N_TOK, HALF), jnp.uint32),
            jax.ShapeDtypeStruct((N_TOK, 128), jnp.int32),
            jax.ShapeDtypeStruct((N_TOK, 128), F32),
            jax.ShapeDtypeStruct((N_TOK, 128), jnp.int32),
            jax.ShapeDtypeStruct((1, N_EXPERTS), F32),
        ],
        scratch_shapes=[pltpu.VMEM((1, N_EXPERTS), F32)],
        compiler_params=_cparams(1),
        name="ffn_prep",
    )(h2, gain.reshape(1, D_MODEL), w_router, b_router.reshape(1, N_EXPERTS))


SEG_ALIGN = 256
MOE_R = N_ASSIGN + N_EXPERTS * SEG_ALIGN


def _plan(counts_f, top_e, rank):
    counts = counts_f[0].astype(jnp.int32)
    seg_rows = (counts + SEG_ALIGN - 1) // SEG_ALIGN * SEG_ALIGN
    seg_start = jnp.cumsum(seg_rows) - seg_rows
    eq = top_e[:, :, None] == jnp.arange(N_EXPERTS, dtype=jnp.int32)[None, None, :]
    dest = jnp.sum(jnp.where(eq, seg_start[None, None, :], 0), axis=-1) + rank
    used = jnp.sum(seg_rows)
    slack = jnp.stack([used, (MOE_R - used) // SEG_ALIGN])
    return (dest.reshape(N_ASSIGN).astype(jnp.int32), seg_start.astype(jnp.int32),
            seg_rows.astype(jnp.int32), (seg_start + counts).astype(jnp.int32),
            (seg_rows - counts).astype(jnp.int32), slack.astype(jnp.int32))


def _zero_slack(slack_ref, zero_block, dst_rows, sem):
    zero_block[...] = jnp.zeros_like(zero_block)

    def copy(j):
        r0 = pl.multiple_of(slack_ref[0] + j * SEG_ALIGN, SEG_ALIGN)
        return pltpu.make_async_copy(zero_block, dst_rows(pl.ds(r0, SEG_ALIGN)), sem)

    def start(j, carry):
        copy(j).start()
        return carry

    def wait(j, carry):
        copy(j).wait()
        return carry

    lax.fori_loop(0, slack_ref[1], start, 0)
    lax.fori_loop(0, slack_ref[1], wait, 0)


DISP_TOK = 256


def _dispatch_kernel(dest_ref, pad0_ref, padn_ref, slack_ref, hn_ref, xs_ref,
                     zrow_ref, zblk_ref, sem, zsem):
    i = pl.program_id(0)

    @pl.when(i == 0)
    def _():
        zrow_ref[...] = jnp.zeros_like(zrow_ref)
        _zero_slack(slack_ref, zblk_ref, lambda rows: xs_ref.at[rows], zsem.at[0])

        def expert(e, carry):
            p0 = pad0_ref[e]
            pn = padn_ref[e]

            def zstart(r, c2):
                pltpu.make_async_copy(zrow_ref, xs_ref.at[pl.ds(p0 + r, 1)], zsem.at[0]).start()
                return c2

            def zwait(r, c2):
                pltpu.make_async_copy(zrow_ref, xs_ref.at[pl.ds(p0, 1)], zsem.at[0]).wait()
                return c2

            lax.fori_loop(0, pn, zstart, 0)
            lax.fori_loop(0, pn, zwait, 0)
            return carry

        lax.fori_loop(0, N_EXPERTS, expert, 0)

    def tok(t, carry):
        a = (i * DISP_TOK + t) * TOP_K
        for kk in range(TOP_K):
            pltpu.make_async_copy(hn_ref.at[pl.ds(t, 1)], xs_ref.at[pl.ds(dest_ref[a + kk], 1)],
                                  sem.at[0]).start(priority=kk % 2)
        return carry

    lax.fori_loop(0, DISP_TOK, tok, 0, unroll=4)
    for kk in range(TOP_K):
        pltpu.make_async_copy(hn_ref, xs_ref.at[pl.ds(0, DISP_TOK)], sem.at[0]).wait()


def dispatch(dest, pad0, padn, slack, hn_packed):
    grid_spec = pltpu.PrefetchScalarGridSpec(
        num_scalar_prefetch=4,
        grid=(N_TOK // DISP_TOK,),
        in_specs=[pl.BlockSpec((DISP_TOK, HALF), lambda i, d, p0, pn, z: (i, 0))],
        out_specs=pl.BlockSpec(memory_space=pl.ANY),
        scratch_shapes=[pltpu.VMEM((1, HALF), jnp.uint32),
                        pltpu.VMEM((SEG_ALIGN, HALF), jnp.uint32),
                        pltpu.SemaphoreType.DMA((1,)), pltpu.SemaphoreType.DMA((1,))],
    )
    return pl.pallas_call(
        _dispatch_kernel,
        grid_spec=grid_spec,
        out_shape=jax.ShapeDtypeStruct((MOE_R, HALF), jnp.uint32),
        compiler_params=_cparams(1),
        name="moe_dispatch",
    )(dest, pad0, padn, slack, hn_packed)


MOE_CH = 2 * SEG_ALIGN
MOE_TF = 1024
MOE_NF = D_FF // MOE_TF


class _CopyGroup:
    def __init__(self, copies):
        self.copies = copies

    def start(self, priority=0):
        for cp in self.copies:
            cp.start(priority=priority)

    def wait(self):
        for cp in self.copies:
            cp.wait()


def _stream_rows(step, n_steps, start, rows, next_start, next_rows, state,
                 make_in, make_out, make_tail_in, make_tail_out,
                 compute_chunk, compute_tail, before_first_wait):
    @pl.when(step == 0)
    def _():
        for j in range(4):
            state[j] = 0

    n_ch = lax.shift_right_logical(rows, MOE_CH.bit_length() - 1)
    tail = rows - n_ch * MOE_CH
    tail_row = start + n_ch * MOE_CH
    g0 = state[0]
    feeds_next = jnp.logical_and(step + 1 < n_steps, next_rows >= MOE_CH)

    def chunk_row(c):
        return start + c * MOE_CH

    @pl.when(tail > 0)
    def _():
        make_tail_in(tail_row).start(priority=1)

    @pl.when(jnp.logical_and(n_ch > 0, state[3] == 0))
    def _():
        make_in(start, lax.rem(g0, 2)).start(priority=1)

    before_first_wait()

    @pl.when(tail > 0)
    def _():
        make_tail_in(tail_row).wait()
        compute_tail()
        make_tail_out(tail_row).start(priority=1)

    def body(c, carry):
        slot = lax.rem(g0 + c, 2)
        make_in(chunk_row(c), slot).wait()

        @pl.when(c + 1 < n_ch)
        def _():
            make_in(chunk_row(c + 1), 1 - slot).start(priority=1)

        @pl.when(jnp.logical_and(c + 1 == n_ch, feeds_next))
        def _():
            make_in(next_start, 1 - slot).start(priority=1)

        @pl.when(state[1 + slot] == 1)
        def _():
            make_out(chunk_row(c), slot).wait()

        compute_chunk(slot)
        make_out(chunk_row(c), slot).start(priority=1)
        state[1 + slot] = 1
        return carry

    lax.fori_loop(0, n_ch, body, 0)
    state[0] = g0 + n_ch
    state[3] = jnp.where(jnp.logical_and(n_ch > 0, feeds_next), 1, 0)

    @pl.when(tail > 0)
    def _():
        make_tail_out(tail_row).wait()

    @pl.when(step == n_steps - 1)
    def _():
        for slot in range(2):
            @pl.when(state[1 + slot] == 1)
            def _():
                make_out(0, slot).wait()
                state[1 + slot] = 0


def _rows_at(row0, n):
    return pl.ds(row0 if isinstance(row0, int) else pl.multiple_of(row0, SEG_ALIGN), n)


W_PARTS = 2


def _start_all(copies):
    for p, cp in enumerate(copies):
        cp.start(priority=p % 2)


def _moe_up_kernel(seg_ref, rows_ref, slack_ref, xs_ref, w_hbm, bg_ref, bl_ref, act_ref,
                   xbuf, obuf, xtail, otail, wbuf, wg_bf, wl_bf, state, isem, osem, tsem, wsem):
    f = pl.program_id(0)
    e = pl.program_id(1)
    start = seg_ref[e]
    rows = rows_ref[e]
    e_next = lax.rem(e + 1, N_EXPERTS)
    f_next = jnp.where(e == N_EXPERTS - 1, f + 1, f)
    step = f * N_EXPERTS + e
    n_steps = MOE_NF * N_EXPERTS
    wslot = lax.rem(step, 2)

    def weight_copies(expert, ftile, slot):
        band = D_MODEL // W_PARTS
        copies = []
        for t in range(2):
            col0 = pl.multiple_of((t * MOE_NF + ftile) * MOE_TF, MOE_TF)
            for p in range(W_PARTS):
                copies.append(pltpu.make_async_copy(
                    w_hbm.at[expert, pl.ds(p * band, band), pl.ds(col0, MOE_TF)],
                    wbuf.at[slot, t, pl.ds(p * band, band)], wsem.at[slot]))
        return copies

    @pl.when(step == 0)
    def _():
        _start_all(weight_copies(e, f, wslot))

    @pl.when(step + 1 < n_steps)
    def _():
        _start_all(weight_copies(e_next, f_next, 1 - wslot))

    def make_in(r0, slot):
        return pltpu.make_async_copy(xs_ref.at[_rows_at(r0, MOE_CH)], xbuf.at[slot], isem.at[slot])

    def make_out(r0, slot):
        return pltpu.make_async_copy(obuf.at[slot], act_ref.at[f, _rows_at(r0, MOE_CH)],
                                     osem.at[slot])

    def make_tail_in(r0):
        return pltpu.make_async_copy(xs_ref.at[_rows_at(r0, SEG_ALIGN)], xtail, tsem.at[0])

    def make_tail_out(r0):
        return pltpu.make_async_copy(otail, act_ref.at[f, _rows_at(r0, SEG_ALIGN)], tsem.at[1])

    def cast_weights():
        for cp in weight_copies(e, f, wslot):
            cp.wait()

        @pl.when(rows > 0)
        def _():
            wg_bf[...] = wbuf[wslot, 0].astype(BF16)
            wl_bf[...] = wbuf[wslot, 1].astype(BF16)

    def expert_mlp(words):
        lo, hi = _unpack_bf16_pairs(words)
        glu = (jnp.dot(lo, wg_bf[:HALF, :], preferred_element_type=F32)
               + jnp.dot(hi, wg_bf[HALF:, :], preferred_element_type=F32) + bg_ref[0])
        lin = (jnp.dot(lo, wl_bf[:HALF, :], preferred_element_type=F32)
               + jnp.dot(hi, wl_bf[HALF:, :], preferred_element_type=F32) + bl_ref[0])
        glu = jnp.minimum(glu, SWIGLU_LIMIT)
        lin = jnp.clip(lin, -SWIGLU_LIMIT, SWIGLU_LIMIT)
        return (glu * jax.nn.sigmoid(SWIGLU_ALPHA * glu) * (lin + 1.0)).astype(BF16)

    def compute_chunk(slot):
        obuf[slot] = expert_mlp(xbuf[slot])

    def compute_tail():
        otail[...] = expert_mlp(xtail[...])

    _stream_rows(step, n_steps, start, rows,
                 seg_ref[e_next], rows_ref[e_next], state,
                 make_in, make_out, make_tail_in, make_tail_out,
                 compute_chunk, compute_tail, cast_weights)

    @pl.when(e == N_EXPERTS - 1)
    def _():
        _zero_slack(slack_ref, otail, lambda rr: act_ref.at[f, rr], tsem.at[1])


def moe_up(seg_start, seg_rows, slack, xs, w1, b1):
    grid_spec = pltpu.PrefetchScalarGridSpec(
        num_scalar_prefetch=3,
        grid=(MOE_NF, N_EXPERTS),
        in_specs=[
            pl.BlockSpec(memory_space=pl.ANY),
            pl.BlockSpec(memory_space=pl.ANY),
            pl.BlockSpec((1, 1, MOE_TF), lambda f, e, s, r, z: (e, 0, f)),
            pl.BlockSpec((1, 1, MOE_TF), lambda f, e, s, r, z: (e, 0, MOE_NF + f)),
        ],
        out_specs=pl.BlockSpec(memory_space=pl.ANY),
        scratch_shapes=[
            pltpu.VMEM((2, MOE_CH, HALF), jnp.uint32),
            pltpu.VMEM((2, MOE_CH, MOE_TF), BF16),
            pltpu.VMEM((SEG_ALIGN, HALF), jnp.uint32),
            pltpu.VMEM((SEG_ALIGN, MOE_TF), BF16),
            pltpu.VMEM((2, 2, D_MODEL, MOE_TF), F32),
            pltpu.VMEM((D_MODEL, MOE_TF), BF16),
            pltpu.VMEM((D_MODEL, MOE_TF), BF16),
            pltpu.SMEM((4,), jnp.int32),
            pltpu.SemaphoreType.DMA((2,)),
            pltpu.SemaphoreType.DMA((2,)),
            pltpu.SemaphoreType.DMA((2,)),
            pltpu.SemaphoreType.DMA((2,)),
        ],
    )
    b13 = b1.reshape(N_EXPERTS, 1, 2 * D_FF)
    return pl.pallas_call(
        _moe_up_kernel,
        grid_spec=grid_spec,
        out_shape=jax.ShapeDtypeStruct((MOE_NF, MOE_R, MOE_TF), BF16),
        compiler_params=pltpu.CompilerParams(
            dimension_semantics=("arbitrary", "arbitrary"), vmem_limit_bytes=MOE_VMEM_LIMIT),
        name="moe_up",
    )(seg_start, seg_rows, slack, xs, w1, b13, b13)


def _moe_down_kernel(seg_ref, rows_ref, slack_ref, act_ref, w_hbm, b_ref, y_ref,
                     xbuf, obuf, xtail, otail, wbuf, w_bf, state, isem, osem, tsem, wsem):
    e = pl.program_id(0)
    start = seg_ref[e]
    rows = rows_ref[e]
    e_next = lax.rem(e + 1, N_EXPERTS)
    wslot = lax.rem(e, 2)

    def weight_copies(expert, slot):
        band = D_FF // (2 * W_PARTS)
        return [pltpu.make_async_copy(w_hbm.at[expert, pl.ds(p * band, band)],
                                      wbuf.at[slot, pl.ds(p * band, band)], wsem.at[slot])
                for p in range(2 * W_PARTS)]

    @pl.when(e == 0)
    def _():
        _start_all(weight_copies(e, wslot))

    @pl.when(e + 1 < N_EXPERTS)
    def _():
        _start_all(weight_copies(e_next, 1 - wslot))

    def make_in(r0, slot):
        return _CopyGroup([
            pltpu.make_async_copy(act_ref.at[j, _rows_at(r0, MOE_CH)],
                                  xbuf.at[slot, :, pl.ds(j * MOE_TF, MOE_TF)], isem.at[slot])
            for j in range(MOE_NF)])

    def make_out(r0, slot):
        return pltpu.make_async_copy(obuf.at[slot], y_ref.at[_rows_at(r0, MOE_CH)], osem.at[slot])

    def make_tail_in(r0):
        return _CopyGroup([
            pltpu.make_async_copy(act_ref.at[j, _rows_at(r0, SEG_ALIGN)],
                                  xtail.at[:, pl.ds(j * MOE_TF, MOE_TF)], tsem.at[0])
            for j in range(MOE_NF)])

    def make_tail_out(r0):
        return pltpu.make_async_copy(otail, y_ref.at[_rows_at(r0, SEG_ALIGN)], tsem.at[1])

    def cast_weights():
        for cp in weight_copies(e, wslot):
            cp.wait()

        @pl.when(rows > 0)
        def _():
            w_bf[...] = wbuf[wslot].astype(BF16)

    def expert_out(a):
        return _pack_bf16_pairs(jnp.dot(a, w_bf[...], preferred_element_type=F32) + b_ref[0])

    def compute_chunk(slot):
        obuf[slot] = expert_out(xbuf[slot])

    def compute_tail():
        otail[...] = expert_out(xtail[...])

    _stream_rows(e, N_EXPERTS, start, rows, seg_ref[e_next], rows_ref[e_next], state,
                 make_in, make_out, make_tail_in, make_tail_out,
                 compute_chunk, compute_tail, cast_weights)

    @pl.when(e == N_EXPERTS - 1)
    def _():
        _zero_slack(slack_ref, otail, lambda rr: y_ref.at[rr], tsem.at[1])


def moe_down(seg_start, seg_rows, slack, act, w2, b2):
    grid_spec = pltpu.PrefetchScalarGridSpec(
        num_scalar_prefetch=3,
        grid=(N_EXPERTS,),
        in_specs=[
            pl.BlockSpec(memory_space=pl.ANY),
            pl.BlockSpec(memory_space=pl.ANY),
            pl.BlockSpec((1, 1, D_MODEL), lambda e, s, r, z: (e, 0, 0)),
        ],
        out_specs=pl.BlockSpec(memory_space=pl.ANY),
        scratch_shapes=[
            pltpu.VMEM((2, MOE_CH, D_FF), BF16),
            pltpu.VMEM((2, MOE_CH, HALF), jnp.uint32),
            pltpu.VMEM((SEG_ALIGN, D_FF), BF16),
            pltpu.VMEM((SEG_ALIGN, HALF), jnp.uint32),
            pltpu.VMEM((2, D_FF, D_MODEL), F32),
            pltpu.VMEM((D_FF, D_MODEL), BF16),
            pltpu.SMEM((4,), jnp.int32),
            pltpu.SemaphoreType.DMA((2,)),
            pltpu.SemaphoreType.DMA((2,)),
            pltpu.SemaphoreType.DMA((2,)),
            pltpu.SemaphoreType.DMA((2,)),
        ],
    )
    return pl.pallas_call(
        _moe_down_kernel,
        grid_spec=grid_spec,
        out_shape=jax.ShapeDtypeStruct((MOE_R, HALF), jnp.uint32),
        compiler_params=pltpu.CompilerParams(
            dimension_semantics=("arbitrary",), vmem_limit_bytes=MOE_VMEM_LIMIT),
        name="moe_down",
    )(seg_start, seg_rows, slack, act, w2, b2.reshape(N_EXPERTS, 1, D_MODEL))


COMB_TM = 128
COMB_NT = N_TOK // COMB_TM


def _combine_kernel(dest_ref, y_ref, w_ref, h_ref, g_ref, o_ref, buf, sem):
    i = pl.program_id(0)

    def fetch(tile, slot):
        def tok(t, carry):
            a = (tile * COMB_TM + t) * TOP_K
            for kk in range(TOP_K):
                pltpu.make_async_copy(y_ref.at[pl.ds(dest_ref[a + kk], 1)],
                                      buf.at[slot, kk, pl.ds(t, 1)], sem.at[slot]).start(
                                          priority=kk % 2)
            return carry

        lax.fori_loop(0, COMB_TM, tok, 0, unroll=4)

    @pl.when(i == 0)
    def _():
        fetch(0, 0)

    slot = lax.rem(i, 2)

    @pl.when(i + 1 < COMB_NT)
    def _():
        fetch(i + 1, 1 - slot)

    for kk in range(TOP_K):
        pltpu.make_async_copy(y_ref.at[pl.ds(0, COMB_TM)], buf.at[slot, kk], sem.at[slot]).wait()
    w = w_ref[...]
    lo = h_ref[:, :HALF]
    hi = h_ref[:, HALF:]
    for kk in range(TOP_K):
        words = buf[slot, kk]
        wk = w[:, kk:kk + 1]
        lo = lo + wk * lax.bitcast_convert_type(lax.shift_left(words, jnp.uint32(16)), F32)
        hi = hi + wk * lax.bitcast_convert_type(words & jnp.uint32(0xFFFF0000), F32)
    ms = (jnp.sum(lo * lo, axis=-1, keepdims=True)
          + jnp.sum(hi * hi, axis=-1, keepdims=True)) * (1.0 / D_MODEL)
    inv = lax.rsqrt(ms + EPS)
    o_ref[:, :HALF] = lo * inv * g_ref[:, :HALF]
    o_ref[:, HALF:] = hi * inv * g_ref[:, HALF:]


def combine(dest, y, weight, h2, gain):
    grid_spec = pltpu.PrefetchScalarGridSpec(
        num_scalar_prefetch=1,
        grid=(COMB_NT,),
        in_specs=[
            pl.BlockSpec(memory_space=pl.ANY),
            pl.BlockSpec((COMB_TM, 128), lambda i, d: (i, 0)),
            pl.BlockSpec((COMB_TM, D_MODEL), lambda i, d: (i, 0)),
            pl.BlockSpec((1, D_MODEL), lambda i, d: (0, 0)),
        ],
        out_specs=pl.BlockSpec((COMB_TM, D_MODEL), lambda i, d: (i, 0)),
        scratch_shapes=[pltpu.VMEM((2, TOP_K, COMB_TM, HALF), jnp.uint32),
                        pltpu.SemaphoreType.DMA((2,))],
    )
    return pl.pallas_call(
        _combine_kernel,
        grid_spec=grid_spec,
        out_shape=jax.ShapeDtypeStruct((N_TOK, D_MODEL), F32),
        compiler_params=_cparams(1),
        name="moe_combine",
    )(dest, y, weight, h2, gain.reshape(1, D_MODEL))


def kernel(x, meta_tokens, rel_bias, norm_mix, w_in, conv_w, gate_bias_m, lambda_params, subln_da,
           w_branch_da, w_branch_m, w_gate, b_gate, w_out, norm_ffn, w_router, b_router,
           w1, b1, w2, b2, norm_final):
    layer = 0
    xn_pad, xn_real = norm_in(x, meta_tokens, norm_mix[layer])
    xn_pad2 = xn_pad.reshape(BATCH * LP, D_MODEL)
    proj = matmul(xn_pad2, w_in[layer], n_cols=PROJ_COLS, tm=768, tn=1024, name="proj_in")
    proj3 = proj.reshape(BATCH, LP, PROJ_COLS)
    w_g = jnp.pad(w_in[layer][:, COL_M_G:], ((0, 0), (0, 128 - 4 * H_M)))
    mg = matmul(xn_pad2, w_g, n_cols=128, tm=768, tn=128, out_dtype=F32, name="proj_gates")
    gate = matmul(xn_real.reshape(N_TOK, D_MODEL), w_gate[layer], n_cols=2 * D_MODEL,
                  tm=1024, tn=1024, bias=b_gate[layer], act="sigmoid", name="mix_gate")

    tab, consts = _bias_tables(rel_bias)
    y_da = diff_attention(consts, proj3, tab, lambda_params[layer], subln_da[layer])

    qk_m = conv_qk(proj3, conv_w[layer])
    gp = gate_prep(mg.reshape(BATCH, LP, 128), gate_bias_m[layer])
    gp4 = gp.reshape(BATCH, LP, 4, H_M)
    gcol = jnp.transpose(gp4, (0, 3, 1, 2))
    grow = jnp.transpose(gp4, (0, 3, 2, 1))
    y_m = mlstm(qk_m, proj3, gcol, grow)

    mixed = branch_mix(y_da.reshape(N_TOK, H_DA * DV_DA), y_m.reshape(N_TOK, H_M * DV_M),
                       w_branch_da[layer], w_branch_m[layer], gate)
    h2 = matmul(mixed, w_out[layer], n_cols=D_MODEL, tm=1024, tn=1024,
                res=x.reshape(N_TOK, D_MODEL), out_dtype=F32, name="out_proj")

    hn_packed, top_e, weight, rank, counts = ffn_prep(
        h2, norm_ffn[layer], w_router[layer], b_router[layer])
    dest, seg_start, seg_rows, pad0, padn, slack = _plan(
        counts, top_e[:, :TOP_K], rank[:, :TOP_K])
    xs = dispatch(dest, pad0, padn, slack, hn_packed)
    act = moe_up(seg_start, seg_rows, slack, xs, w1[layer], b1[layer])
    y = moe_down(seg_start, seg_rows, slack, act, w2[layer], b2[layer])
    out = combine(dest, y, weight, h2, norm_final)
    return out.reshape(BATCH, SEQ, D_MODEL)
```

```python
import functools
import math

import jax
import jax.numpy as jnp
from jax import lax
from jax.experimental import pallas as pl
from jax.experimental.pallas import tpu as pltpu

F32 = jnp.float32
BF16 = jnp.bfloat16

D_MODEL = 2048
BATCH = 2
SEQ = 4096
N_META = 16
BLOCK = 128
PAD = (-N_META) % BLOCK
LP = PAD + N_META + SEQ
NBLK = LP // BLOCK
EPS = 1e-6
NEG_INF = -1e30

H_DA = 4
DK_DA = 128
DV_DA = 256
H_M = 4
DK_M = 128
DV_M = 256
CONV_W = 5
N_BUCKETS = 32
MAX_DISTANCE = 128
N_EXPERTS = 32
TOP_K = 4
D_FF = 2048
SWIGLU_ALPHA = 1.702
SWIGLU_LIMIT = 7.0
LAMBDA_INIT = 0.8 - 0.6 * math.exp(-0.3 * 0)

COL_DA_Q = 0
COL_DA_K = 1024
COL_DA_V = 2048
COL_M_Q = 3072
COL_M_K = 3584
COL_M_V = 4096
COL_M_O = 5120
COL_M_G = 6144
PROJ_COLS = 6144

N_TOK = BATCH * SEQ
N_ASSIGN = N_TOK * TOP_K

VMEM_LIMIT = 52 * 1024 * 1024
MOE_VMEM_LIMIT = 58 * 1024 * 1024


def _cparams(n_axes):
    return pltpu.CompilerParams(
        dimension_semantics=("arbitrary",) * n_axes, vmem_limit_bytes=VMEM_LIMIT)


def _rms(v, gain):
    ms = jnp.mean(v * v, axis=-1, keepdims=True)
    return v * lax.rsqrt(ms + EPS) * gain


def _norm_in_kernel(x_ref, meta_ref, g_ref, pad_ref, real_ref):
    j = pl.program_id(1)
    g = g_ref[...]

    @pl.when(j == 0)
    def _():
        pad_ref[0, :PAD, :] = jnp.zeros((PAD, D_MODEL), BF16)
        pad_ref[0, PAD:, :] = _rms(meta_ref[...], g).astype(BF16)

    @pl.when(j > 0)
    def _():
        y = _rms(x_ref[0], g).astype(BF16)
        pad_ref[0] = y
        real_ref[0] = y


def norm_in(x, meta, gain):
    return pl.pallas_call(
        _norm_in_kernel,
        grid=(BATCH, NBLK),
        in_specs=[
            pl.BlockSpec((1, BLOCK, D_MODEL), lambda b, j: (b, jnp.maximum(j - 1, 0), 0)),
            pl.BlockSpec((N_META, D_MODEL), lambda b, j: (0, 0)),
            pl.BlockSpec((1, D_MODEL), lambda b, j: (0, 0)),
        ],
        out_specs=[
            pl.BlockSpec((1, BLOCK, D_MODEL), lambda b, j: (b, j, 0)),
            pl.BlockSpec((1, BLOCK, D_MODEL), lambda b, j: (b, jnp.maximum(j - 1, 0), 0)),
        ],
        out_shape=[
            jax.ShapeDtypeStruct((BATCH, LP, D_MODEL), BF16),
            jax.ShapeDtypeStruct((BATCH, SEQ, D_MODEL), BF16),
        ],
        compiler_params=_cparams(2),
        name="norm_in",
    )(x, meta, gain.reshape(1, D_MODEL))


def _mm_kernel(*refs, has_bias, has_res, act):
    x_ref, w_ref = refs[0], refs[1]
    pos = 2
    b_ref = r_ref = None
    if has_bias:
        b_ref = refs[pos]
        pos += 1
    if has_res:
        r_ref = refs[pos]
        pos += 1
    o_ref, wbf_ref = refs[pos], refs[pos + 1]

    @pl.when(pl.program_id(1) == 0)
    def _():
        wbf_ref[...] = w_ref[...].astype(BF16)

    acc = jnp.dot(x_ref[...], wbf_ref[...], preferred_element_type=F32)
    if has_bias:
        acc = acc + b_ref[...]
    if act == "sigmoid":
        acc = jax.nn.sigmoid(acc)
    if has_res:
        acc = acc + r_ref[...]
    o_ref[...] = acc.astype(o_ref.dtype)


def matmul(x, w, *, n_cols, col_block0=0, tm, tn, bias=None, res=None, act=None,
           out_dtype=BF16, name):
    m, k = x.shape
    in_specs = [
        pl.BlockSpec((tm, k), lambda j, i: (i, 0)),
        pl.BlockSpec((k, tn), lambda j, i: (0, j + col_block0)),
    ]
    args = [x, w]
    if bias is not None:
        in_specs.append(pl.BlockSpec((1, tn), lambda j, i: (0, j)))
        args.append(bias.reshape(1, n_cols))
    if res is not None:
        in_specs.append(pl.BlockSpec((tm, tn), lambda j, i: (i, j)))
        args.append(res)
    return pl.pallas_call(
        functools.partial(_mm_kernel, has_bias=bias is not None, has_res=res is not None, act=act),
        grid=(n_cols // tn, m // tm),
        in_specs=in_specs,
        out_specs=pl.BlockSpec((tm, tn), lambda j, i: (i, j)),
        out_shape=jax.ShapeDtypeStruct((m, n_cols), out_dtype),
        scratch_shapes=[pltpu.VMEM((k, tn), BF16)],
        compiler_params=_cparams(2),
        name=name,
    )(*args)


def _conv_kernel(p_ref, w_ref, o_ref):
    c = pl.program_id(1)
    x = p_ref[0].astype(F32)
    w = w_ref[...]
    half = CONV_W // 2
    acc = w[half:half + 1, :] * x
    for j in range(CONV_W):
        if j != half:
            acc = acc + w[j:j + 1, :] * pltpu.roll(x, (half - j) % LP, axis=0)
    y = acc * jax.nn.sigmoid(acc)
    rows = lax.broadcasted_iota(jnp.int32, (LP, 1), 0)
    y = jnp.where(rows >= PAD, y, 0.0)
    scale = jnp.where(c < 2, DK_M ** -0.5, 1.0).astype(F32)
    o_ref[0] = (y * scale).astype(BF16)


def conv_qk(proj3, conv_w):
    cw = 256
    return pl.pallas_call(
        _conv_kernel,
        grid=(BATCH, (2 * H_M * DK_M) // cw),
        in_specs=[
            pl.BlockSpec((1, LP, cw), lambda b, c: (b, 0, COL_M_Q // cw + c)),
            pl.BlockSpec((CONV_W, cw), lambda b, c: (0, c)),
        ],
        out_specs=pl.BlockSpec((1, LP, cw), lambda b, c: (b, 0, c)),
        out_shape=jax.ShapeDtypeStruct((BATCH, LP, 2 * H_M * DK_M), BF16),
        compiler_params=_cparams(2),
        name="conv_qk",
    )(proj3, conv_w)


def _split_dot(tri, v):
    hi = v.astype(BF16)
    r1 = v - hi.astype(F32)
    mid = r1.astype(BF16)
    lo = (r1 - mid.astype(F32)).astype(BF16)
    return (jnp.dot(tri, hi, preferred_element_type=F32)
            + jnp.dot(tri, mid, preferred_element_type=F32)
            + jnp.dot(tri, lo, preferred_element_type=F32))


def _gate_kernel(g_ref, bias_ref, o_ref):
    ti = lax.broadcasted_iota(jnp.int32, (BLOCK, BLOCK), 0)
    ui = lax.broadcasted_iota(jnp.int32, (BLOCK, BLOCK), 1)
    tril = jnp.where(ui <= ti, 1.0, 0.0).astype(BF16)
    triu = jnp.where(ui >= ti, 1.0, 0.0).astype(BF16)
    ch = lax.broadcasted_iota(jnp.int32, (BLOCK, 4 * H_M), 1)
    typ = lax.shift_right_logical(ch, 2)
    rloc = lax.broadcasted_iota(jnp.int32, (BLOCK, 4 * H_M), 0)

    def body(c, carry):
        r0 = pl.multiple_of(c * BLOCK, BLOCK)
        g = g_ref[0, pl.ds(r0, BLOCK), :][:, :4 * H_M] + bias_ref[...]
        valid = (rloc + r0) >= PAD
        lsig = -(jnp.maximum(-g, 0.0) + jnp.log1p(jnp.exp(-jnp.abs(g))))
        lf = jnp.where(valid, lsig, 0.0)
        cum = _split_dot(tril, lf)
        rcum = _split_dot(triu, lf)
        li = jnp.where(valid, g, -jnp.inf)
        out = jnp.where(typ == 1, cum, jnp.where(typ == 3, rcum, li))
        o_ref[0, pl.ds(r0, BLOCK), :] = out
        return carry

    lax.fori_loop(0, NBLK, body, 0)


def gate_prep(mg3, gate_bias):
    return pl.pallas_call(
        _gate_kernel,
        grid=(BATCH,),
        in_specs=[
            pl.BlockSpec((1, LP, 128), lambda b: (b, 0, 0)),
            pl.BlockSpec((1, 4 * H_M), lambda b: (0, 0)),
        ],
        out_specs=pl.BlockSpec((1, LP, 4 * H_M), lambda b: (b, 0, 0)),
        out_shape=jax.ShapeDtypeStruct((BATCH, LP, 4 * H_M), F32),
        compiler_params=_cparams(1),
        name="gate_prep",
    )(mg3, gate_bias.reshape(1, 4 * H_M))


MLSTM_HP = 2
MLSTM_MID = NBLK // 2


def _mlstm_kernel(q_ref, k_ref, v_ref, o_ref, gc_ref, gr_ref, y_ref,
                  hs_ref, c_ref, n_ref, m_ref):
    c_ref[...] = jnp.zeros_like(c_ref)
    n_ref[...] = jnp.zeros_like(n_ref)
    m_ref[...] = jnp.zeros_like(m_ref)
    ti = lax.broadcasted_iota(jnp.int32, (BLOCK, BLOCK), 0)
    si = lax.broadcasted_iota(jnp.int32, (BLOCK, BLOCK), 1)
    mask_f = si <= ti
    mask_b = si >= ti

    def chain(c, hl, bwd, final):
        idx = 2 * hl + bwd
        r0 = c * BLOCK if isinstance(c, int) else pl.multiple_of(c * BLOCK, BLOCK)
        q = q_ref[0, pl.ds(r0, BLOCK), hl * DK_M:(hl + 1) * DK_M]
        k = k_ref[0, pl.ds(r0, BLOCK), hl * DK_M:(hl + 1) * DK_M]
        v = v_ref[0, pl.ds(r0, BLOCK), hl * DV_M:(hl + 1) * DV_M]
        gc = gc_ref[0, hl, pl.ds(r0, BLOCK), :]
        gr = gr_ref[0, hl, :, pl.ds(r0, BLOCK)]
        a = 2 * bwd
        li_c, b_c = gc[:, a:a + 1], gc[:, a + 1:a + 2]
        li_r, b_r = gr[a:a + 1, :], gr[a + 1:a + 2, :]
        b_end = b_c[0:1, :] if bwd else b_c[BLOCK - 1:BLOCK, :]
        m_prev = m_ref[idx][:, 0:1]
        cst = c_ref[idx]
        nst = n_ref[idx]
        dmat = jnp.where(mask_b if bwd else mask_f, b_c - b_r + li_r, -jnp.inf)
        inter = b_c + m_prev
        m_t = jnp.maximum(inter, jnp.max(dmat, axis=1, keepdims=True))
        w_inter = jnp.exp(inter - m_t)
        qk = lax.dot_general(q, k, (((1,), (1,)), ((), ())), preferred_element_type=F32)
        s = qk * jnp.exp(dmat - m_t)
        num = (w_inter * jnp.dot(q, cst.astype(BF16), preferred_element_type=F32)
               + jnp.dot(s.astype(BF16), v, preferred_element_type=F32))
        den = jnp.sum(s + w_inter * (q.astype(F32) * nst), axis=1, keepdims=True)
        h = num / jnp.maximum(jnp.abs(den), jnp.exp(-m_t))
        cols = slice(hl * DV_M, (hl + 1) * DV_M)
        if final:
            og = jax.nn.sigmoid(o_ref[0, pl.ds(r0, BLOCK), cols].astype(F32))
            y_ref[0, pl.ds(r0 - BLOCK, BLOCK), cols] = (
                og * (hs_ref[pl.ds(r0, BLOCK), cols] + h)).astype(BF16)
        else:
            hs_ref[pl.ds(r0, BLOCK), cols] = h
        ldec_c = b_end - b_c + li_c
        ldec_r = b_end - b_r + li_r
        m_new = jnp.maximum(b_end + m_prev, jnp.max(ldec_r, axis=1, keepdims=True))
        w_c = jnp.exp(b_end + m_prev - m_new)
        w_s = jnp.exp(ldec_c - m_new)
        wv = (w_s * v.astype(F32)).astype(BF16)
        c_ref[idx] = w_c * cst + lax.dot_general(
            k, wv, (((0,), (0,)), ((), ())), preferred_element_type=F32)
        n_ref[idx] = w_c * nst + jnp.sum(w_s * k.astype(F32), axis=0, keepdims=True)
        m_ref[idx] = jnp.broadcast_to(m_new, (1, BLOCK))

    def first_half(i, carry):
        for hl in range(MLSTM_HP):
            chain(i, hl, 0, False)
            chain(NBLK - 1 - i, hl, 1, False)
        return carry

    def second_half(i, carry):
        for hl in range(MLSTM_HP):
            chain(i, hl, 0, True)
            chain(NBLK - 1 - i, hl, 1, True)
        return carry

    lax.fori_loop(0, MLSTM_MID, first_half, 0)
    for hl in range(MLSTM_HP):
        chain(MLSTM_MID, hl, 0, False)
        chain(MLSTM_MID, hl, 1, True)
    lax.fori_loop(MLSTM_MID + 1, NBLK - 1, second_half, 0)
    for hl in range(MLSTM_HP):
        chain(NBLK - 1, hl, 0, True)


def mlstm(qk_m, proj3, gcol, grow):
    hp = MLSTM_HP
    kw, vw = hp * DK_M, hp * DV_M
    return pl.pallas_call(
        _mlstm_kernel,
        grid=(BATCH, H_M // hp),
        in_specs=[
            pl.BlockSpec((1, LP, kw), lambda b, g: (b, 0, g)),
            pl.BlockSpec((1, LP, kw), lambda b, g: (b, 0, (H_M * DK_M) // kw + g)),
            pl.BlockSpec((1, LP, vw), lambda b, g: (b, 0, COL_M_V // vw + g)),
            pl.BlockSpec((1, LP, vw), lambda b, g: (b, 0, COL_M_O // vw + g)),
            pl.BlockSpec((1, hp, LP, 4), lambda b, g: (b, g, 0, 0)),
            pl.BlockSpec((1, hp, 4, LP), lambda b, g: (b, g, 0, 0)),
        ],
        out_specs=pl.BlockSpec((1, SEQ, vw), lambda b, g: (b, 0, g)),
        out_shape=jax.ShapeDtypeStruct((BATCH, SEQ, H_M * DV_M), BF16),
        scratch_shapes=[
            pltpu.VMEM((LP, vw), F32),
            pltpu.VMEM((2 * hp, DK_M, DV_M), F32),
            pltpu.VMEM((2 * hp, 1, DK_M), F32),
            pltpu.VMEM((2 * hp, 1, BLOCK), F32),
        ],
        compiler_params=_cparams(2),
        name="mlstm",
    )(qk_m, qk_m, proj3, proj3, gcol, grow)


ATT_QB = 2
ATT_TQ = ATT_QB * BLOCK
ATT_BAND = (ATT_QB + 2) * BLOCK
ATT_GROUPS = (6, 6, 6, 6, 5)
assert BLOCK >= MAX_DISTANCE and ATT_BAND + sum(ATT_GROUPS) * BLOCK == LP


def _attn_kernel(c_ref, qa_ref, qb_ref, k1_ref, v1_ref, tab_ref, lam_ref, sg_ref, o_ref,
                 s_ref, k_ref, v_ref):
    h = pl.program_id(1)
    qb = ATT_QB * pl.program_id(2) + 1

    @pl.when(pl.program_id(2) == 0)
    def _():
        for rep in range(2):
            k_ref[0, rep * LP:(rep + 1) * LP, :] = k1_ref[0]
            v_ref[0, rep * LP:(rep + 1) * LP, :] = v1_ref[0]

    q = jnp.concatenate([qa_ref[0], qb_ref[0]], axis=0)
    scale = DK_DA ** -0.5
    c_neg = c_ref[h, 0]
    c_pos = c_ref[h, 1]
    lp = lam_ref[...]
    lam = (jnp.exp(jnp.sum(lp[0:1] * lp[1:2], axis=1, keepdims=True))
           - jnp.exp(jnp.sum(lp[2:3] * lp[3:4], axis=1, keepdims=True)) + LAMBDA_INIT)

    groups = [((qb - 1) * BLOCK, ATT_BAND, 0)]
    col = ATT_BAND
    for nblk in ATT_GROUPS:
        groups.append(((qb - 1) * BLOCK + col, nblk * BLOCK, col))
        col += nblk * BLOCK

    def lane_fold(acc, t, op):
        for j in range(t.shape[1] // BLOCK):
            piece = t[:, j * BLOCK:(j + 1) * BLOCK]
            acc = piece if acc is None else op(acc, piece)
        return acc

    mx = [None, None]
    for gi, (koff, width, col0) in enumerate(groups):
        koff = pl.multiple_of(koff, BLOCK)
        if gi == 0:
            bias = tab_ref[0, 0]
        else:
            kpos = koff + lax.broadcasted_iota(jnp.int32, (1, width), 1)
            bias = jnp.where(kpos < LP, c_pos, jnp.where(kpos < LP + PAD, NEG_INF, c_neg))
        for m in range(2):
            kk = k_ref[0, pl.ds(koff, width), m * DK_DA:(m + 1) * DK_DA]
            s = lax.dot_general(q[:, m * DK_DA:(m + 1) * DK_DA], kk, (((1,), (1,)), ((), ())),
                                preferred_element_type=F32) * scale + bias
            s_ref[m, :, col0:col0 + width] = s
            mx[m] = lane_fold(mx[m], s, jnp.maximum)
    row_max = [jnp.max(mx[m], axis=1, keepdims=True) for m in range(2)]

    lsum = [None, None]
    acc = [None, None]
    for koff, width, col0 in groups:
        koff = pl.multiple_of(koff, BLOCK)
        vv = v_ref[0, pl.ds(koff, width), :]
        for m in range(2):
            p = jnp.exp(s_ref[m, :, col0:col0 + width] - row_max[m])
            lsum[m] = lane_fold(lsum[m], p, jnp.add)
            pv = jnp.dot(p.astype(BF16), vv, preferred_element_type=F32)
            acc[m] = pv if acc[m] is None else acc[m] + pv
    l1 = jnp.sum(lsum[0], axis=1, keepdims=True)
    l2 = jnp.sum(lsum[1], axis=1, keepdims=True)
    o = acc[0] / l1 - lam * (acc[1] / l2)
    o_ref[0] = (_rms(o, sg_ref[...]) * (1.0 - LAMBDA_INIT)).astype(BF16)


def diff_attention(consts, proj3, tab, lam_params, subln):
    nq = SEQ // ATT_TQ
    kblk0 = COL_DA_K // (2 * DK_DA)
    vblk0 = COL_DA_V // DV_DA

    def tab_map(b, h, i):
        case = jnp.where(i == 0, 0, jnp.where(i == nq - 1, 2, 1))
        return (h, case, 0, 0)

    return pl.pallas_call(
        _attn_kernel,
        grid=(BATCH, H_DA, nq),
        in_specs=[
            pl.BlockSpec(memory_space=pltpu.SMEM),
            pl.BlockSpec((1, BLOCK, 2 * DK_DA), lambda b, h, i: (b, ATT_QB * i + 1, h)),
            pl.BlockSpec((1, BLOCK, 2 * DK_DA), lambda b, h, i: (b, ATT_QB * i + 2, h)),
            pl.BlockSpec((1, LP, 2 * DK_DA), lambda b, h, i: (b, 0, kblk0 + h)),
            pl.BlockSpec((1, LP, DV_DA), lambda b, h, i: (b, 0, vblk0 + h)),
            pl.BlockSpec((1, 1, ATT_TQ, ATT_BAND), tab_map),
            pl.BlockSpec((4, DK_DA), lambda b, h, i: (0, 0)),
            pl.BlockSpec((1, DV_DA), lambda b, h, i: (0, 0)),
        ],
        out_specs=pl.BlockSpec((1, ATT_TQ, DV_DA), lambda b, h, i: (b, i, h)),
        out_shape=jax.ShapeDtypeStruct((BATCH, SEQ, H_DA * DV_DA), BF16),
        scratch_shapes=[pltpu.VMEM((2, ATT_TQ, LP), F32),
                        pltpu.VMEM((1, 2 * LP, 2 * DK_DA), BF16),
                        pltpu.VMEM((1, 2 * LP, DV_DA), BF16)],
        compiler_params=_cparams(3),
        name="diff_attn",
    )(consts, proj3, proj3, proj3, proj3, tab, lam_params, subln.reshape(1, DV_DA))


def _bias_tables(rel_bias):
    rb = rel_bias.astype(F32)
    i = jnp.arange(ATT_TQ, dtype=jnp.int32)[:, None]
    j = jnp.arange(ATT_BAND, dtype=jnp.int32)[None, :]
    rel = j - BLOCK - i
    nb = N_BUCKETS // 2
    max_exact = nb // 2
    n = jnp.abs(rel)
    nf = jnp.maximum(n, 1).astype(F32)
    large = max_exact + (jnp.log(nf / max_exact) / math.log(MAX_DISTANCE / max_exact)
                         * (nb - max_exact)).astype(jnp.int32)
    large = jnp.minimum(large, nb - 1)
    bucket = jnp.where(rel > 0, nb, 0) + jnp.where(n < max_exact, n, large)
    hit = bucket[None, :, :, None] == jnp.arange(N_BUCKETS, dtype=jnp.int32)
    gen = jnp.sum(jnp.where(hit, rb.T[:, None, None, :], 0.0), axis=-1)
    c_neg = rb[nb - 1]
    c_pos = rb[N_BUCKETS - 1]
    jj = j[None]
    first = jnp.where(jj < PAD, NEG_INF, gen)
    wrap0 = ATT_BAND - BLOCK
    wrapped = jnp.where(jj - wrap0 < PAD, NEG_INF, c_neg[:, None, None])
    last = jnp.where(jj >= wrap0, wrapped, gen)
    tab = jnp.stack([first, gen, last], axis=1)
    consts = jnp.stack([c_neg, c_pos], axis=1)
    return tab, consts


def _mix_kernel(ya_ref, ym_ref, wa_ref, wm_ref, ga_ref, gm_ref, o_ref, wa_bf, wm_bf):
    @pl.when(pl.program_id(1) == 0)
    def _():
        wa_bf[...] = wa_ref[...].astype(BF16)
        wm_bf[...] = wm_ref[...].astype(BF16)

    a = jnp.dot(ya_ref[...], wa_bf[...], preferred_element_type=F32)
    m = jnp.dot(ym_ref[...], wm_bf[...], preferred_element_type=F32)
    o_ref[...] = (ga_ref[...].astype(F32) * a + gm_ref[...].astype(F32) * m).astype(BF16)


def branch_mix(y_da, y_m, w_da, w_m, gate, *, tm=512, tn=1024):
    m, k = y_da.shape
    nj = D_MODEL // tn
    return pl.pallas_call(
        _mix_kernel,
        grid=(nj, m // tm),
        in_specs=[
            pl.BlockSpec((tm, k), lambda j, i: (i, 0)),
            pl.BlockSpec((tm, k), lambda j, i: (i, 0)),
            pl.BlockSpec((k, tn), lambda j, i: (0, j)),
            pl.BlockSpec((k, tn), lambda j, i: (0, j)),
            pl.BlockSpec((tm, tn), lambda j, i: (i, j)),
            pl.BlockSpec((tm, tn), lambda j, i: (i, nj + j)),
        ],
        out_specs=pl.BlockSpec((tm, tn), lambda j, i: (i, j)),
        out_shape=jax.ShapeDtypeStruct((m, D_MODEL), BF16),
        scratch_shapes=[pltpu.VMEM((k, tn), BF16), pltpu.VMEM((k, tn), BF16)],
        compiler_params=_cparams(2),
        name="branch_mix",
    )(y_da, y_m, w_da, w_m, gate, gate)


FFN_TM = 1024
HALF = D_MODEL // 2


def _pack_bf16_pairs(v):
    lo = lax.bitcast_convert_type(v[:, :HALF].astype(BF16).astype(F32), jnp.uint32)
    hi = lax.bitcast_convert_type(v[:, HALF:].astype(BF16).astype(F32), jnp.uint32)
    return (hi & jnp.uint32(0xFFFF0000)) | lax.shift_right_logical(lo, jnp.uint32(16))


def _unpack_bf16_pairs(w):
    lo = lax.bitcast_convert_type(lax.shift_left(w, jnp.uint32(16)), F32).astype(BF16)
    hi = lax.bitcast_convert_type(w & jnp.uint32(0xFFFF0000), F32).astype(BF16)
    return lo, hi


def _ffn_prep_kernel(h_ref, g_ref, wr_ref, br_ref, hn_ref, e_ref, w_ref, r_ref, cnt_ref, base_ref):
    @pl.when(pl.program_id(0) == 0)
    def _():
        base_ref[...] = jnp.zeros_like(base_ref)

    hn = _rms(h_ref[...], g_ref[...])
    hn_ref[...] = _pack_bf16_pairs(hn)
    logits = jnp.dot(hn, wr_ref[...], preferred_element_type=F32,
                     precision=lax.Precision.HIGHEST) + br_ref[...]
    lane = lax.broadcasted_iota(jnp.int32, (FFN_TM, N_EXPERTS), 1)
    lane_o = lax.broadcasted_iota(jnp.int32, (FFN_TM, 128), 1)
    ti = lax.broadcasted_iota(jnp.int32, (FFN_TM, FFN_TM), 0)
    ui = lax.broadcasted_iota(jnp.int32, (FFN_TM, FFN_TM), 1)
    tril = jnp.where(ui <= ti, 1.0, 0.0).astype(BF16)
    e_out = jnp.zeros((FFN_TM, 128), jnp.int32)
    r_out = jnp.zeros((FFN_TM, 128), jnp.int32)
    l_out = jnp.full((FFN_TM, 128), -jnp.inf, F32)
    base = base_ref[...]
    l = logits
    for kk in range(TOP_K):
        mk = jnp.max(l, axis=1, keepdims=True)
        ik = jnp.min(jnp.where(l == mk, lane, N_EXPERTS), axis=1, keepdims=True)
        hit = lane == ik
        oh = jnp.where(hit, 1.0, 0.0)
        cum = jnp.dot(tril, oh.astype(BF16), preferred_element_type=F32)
        rank = jnp.sum(oh * (cum + base), axis=1, keepdims=True) - 1.0
        base = base + jnp.sum(oh, axis=0, keepdims=True)
        e_out = jnp.where(lane_o == kk, ik, e_out)
        r_out = jnp.where(lane_o == kk, rank.astype(jnp.int32), r_out)
        l_out = jnp.where(lane_o == kk, mk, l_out)
        l = jnp.where(hit, -jnp.inf, l)
    base_ref[...] = base
    cnt_ref[...] = base
    ex = jnp.exp(l_out - jnp.max(l_out, axis=1, keepdims=True))
    e_ref[...] = e_out
    r_ref[...] = r_out
    w_ref[...] = ex / jnp.sum(ex, axis=1, keepdims=True)


def ffn_prep(h2, gain, w_router, b_router):
    row = lambda i: (i, 0)
    fixed = lambda i: (0, 0)
    return pl.pallas_call(
        _ffn_prep_kernel,
        grid=(N_TOK // FFN_TM,),
        in_specs=[
            pl.BlockSpec((FFN_TM, D_MODEL), row),
            pl.BlockSpec((1, D_MODEL), fixed),
            pl.BlockSpec((D_MODEL, N_EXPERTS), fixed),
            pl.BlockSpec((1, N_EXPERTS), fixed),
        ],
        out_specs=[
            pl.BlockSpec((FFN_TM, HALF), row),
            pl.BlockSpec((FFN_TM, 128), row),
            pl.BlockSpec((FFN_TM, 128), row),
            pl.BlockSpec((FFN_TM, 128), row),
            pl.BlockSpec((1, N_EXPERTS), fixed),
        ],
        out_shape=[
            jax.ShapeDtypeStruct((N_TOK, HALF), jnp.uint32),
            jax.ShapeDtypeStruct((N_TOK, 128), jnp.int32),
            jax.ShapeDtypeStruct((N_TOK, 128), F32),
            jax.ShapeDtypeStruct((N_TOK, 128), jnp.int32),
            jax.ShapeDtypeStruct((1, N_EXPERTS), F32),
        ],
        scratch_shapes=[pltpu.VMEM((1, N_EXPERTS), F32)],
        compiler_params=_cparams(1),
        name="ffn_prep",
    )(h2, gain.reshape(1, D_MODEL), w_router, b_router.reshape(1, N_EXPERTS))


SEG_ALIGN = 128
MOE_R = N_ASSIGN + N_EXPERTS * SEG_ALIGN


def _plan(counts_f, top_e, rank):
    counts = counts_f[0].astype(jnp.int32)
    seg_rows = (counts + SEG_ALIGN - 1) // SEG_ALIGN * SEG_ALIGN
    seg_start = jnp.cumsum(seg_rows) - seg_rows
    eq = top_e[:, :, None] == jnp.arange(N_EXPERTS, dtype=jnp.int32)[None, None, :]
    dest = jnp.sum(jnp.where(eq, seg_start[None, None, :], 0), axis=-1) + rank
    used = jnp.sum(seg_rows)
    slack = jnp.stack([used, (MOE_R - used) // SEG_ALIGN])
    return (dest.reshape(N_ASSIGN).astype(jnp.int32), seg_start.astype(jnp.int32),
            seg_rows.astype(jnp.int32), (seg_start + counts).astype(jnp.int32),
            (seg_rows - counts).astype(jnp.int32), slack.astype(jnp.int32))


def _zero_slack(slack_ref, zero_block, dst_rows, sem):
    zero_block[...] = jnp.zeros_like(zero_block)

    def copy(j):
        r0 = pl.multiple_of(slack_ref[0] + j * SEG_ALIGN, SEG_ALIGN)
        return pltpu.make_async_copy(zero_block, dst_rows(pl.ds(r0, SEG_ALIGN)), sem)

    def start(j, carry):
        copy(j).start()
        return carry

    def wait(j, carry):
        copy(j).wait()
        return carry

    lax.fori_loop(0, slack_ref[1], start, 0)
    lax.fori_loop(0, slack_ref[1], wait, 0)


DISP_TOK = 256


def _dispatch_kernel(dest_ref, pad0_ref, padn_ref, slack_ref, hn_ref, xs_ref,
                     zrow_ref, zblk_ref, sem, zsem):
    i = pl.program_id(0)

    @pl.when(i == 0)
    def _():
        zrow_ref[...] = jnp.zeros_like(zrow_ref)
        _zero_slack(slack_ref, zblk_ref, lambda rows: xs_ref.at[rows], zsem.at[0])

        def expert(e, carry):
            p0 = pad0_ref[e]
            pn = padn_ref[e]

            def zstart(r, c2):
                pltpu.make_async_copy(zrow_ref, xs_ref.at[pl.ds(p0 + r, 1)], zsem.at[0]).start()
                return c2

            def zwait(r, c2):
                pltpu.make_async_copy(zrow_ref, xs_ref.at[pl.ds(p0, 1)], zsem.at[0]).wait()
                return c2

            lax.fori_loop(0, pn, zstart, 0)
            lax.fori_loop(0, pn, zwait, 0)
            return carry

        lax.fori_loop(0, N_EXPERTS, expert, 0)

    def tok(t, carry):
        a = (i * DISP_TOK + t) * TOP_K
        for kk in range(TOP_K):
            pltpu.make_async_copy(hn_ref.at[pl.ds(t, 1)], xs_ref.at[pl.ds(dest_ref[a + kk], 1)],
                                  sem.at[0]).start(priority=kk % 2)
        return carry

    lax.fori_loop(0, DISP_TOK, tok, 0, unroll=4)
    for kk in range(TOP_K):
        pltpu.make_async_copy(hn_ref, xs_ref.at[pl.ds(0, DISP_TOK)], sem.at[0]).wait()


def dispatch(dest, pad0, padn, slack, hn_packed):
    grid_spec = pltpu.PrefetchScalarGridSpec(
        num_scalar_prefetch=4,
        grid=(N_TOK // DISP_TOK,),
        in_specs=[pl.BlockSpec((DISP_TOK, HALF), lambda i, d, p0, pn, z: (i, 0))],
        out_specs=pl.BlockSpec(memory_space=pl.ANY),
        scratch_shapes=[pltpu.VMEM((1, HALF), jnp.uint32),
                        pltpu.VMEM((SEG_ALIGN, HALF), jnp.uint32),
                        pltpu.SemaphoreType.DMA((1,)), pltpu.SemaphoreType.DMA((1,))],
    )
    return pl.pallas_call(
        _dispatch_kernel,
        grid_spec=grid_spec,
        out_shape=jax.ShapeDtypeStruct((MOE_R, HALF), jnp.uint32),
        compiler_params=_cparams(1),
        name="moe_dispatch",
    )(dest, pad0, padn, slack, hn_packed)


MOE_CH = 2 * SEG_ALIGN
MOE_TF = 1024
MOE_NF = D_FF // MOE_TF


class _CopyGroup:
    def __init__(self, copies):
        self.copies = copies

    def start(self, priority=0):
        for cp in self.copies:
            cp.start(priority=priority)

    def wait(self):
        for cp in self.copies:
            cp.wait()


W_PIECES = 8
W_PER_CHUNK = 2


def _stream_rows(step, n_steps, start, rows, next_start, next_rows, state,
                 make_in, make_out, make_tail_in, make_tail_out,
                 compute_chunk, compute_tail, before_first_wait, next_weight_piece):
    has_next = step + 1 < n_steps

    def request_weights(first, count):
        def one(p, carry):
            @pl.when(jnp.logical_and(has_next, p < W_PIECES))
            def _():
                next_weight_piece(p).start()
            return carry

        lax.fori_loop(first, first + count, one, 0)

    @pl.when(step == 0)
    def _():
        for j in range(4):
            state[j] = 0

    n_ch = lax.shift_right_logical(rows, MOE_CH.bit_length() - 1)
    tail = rows - n_ch * MOE_CH
    tail_row = start + n_ch * MOE_CH
    g0 = state[0]
    feeds_next = jnp.logical_and(step + 1 < n_steps, next_rows >= MOE_CH)

    def chunk_row(c):
        return start + c * MOE_CH

    @pl.when(tail > 0)
    def _():
        make_tail_in(tail_row).start(priority=1)

    @pl.when(jnp.logical_and(n_ch > 0, state[3] == 0))
    def _():
        make_in(start, lax.rem(g0, 2)).start(priority=1)

    before_first_wait()

    @pl.when(tail > 0)
    def _():
        make_tail_in(tail_row).wait()
        compute_tail()
        make_tail_out(tail_row).start(priority=1)

    def body(c, carry):
        slot = lax.rem(g0 + c, 2)
        make_in(chunk_row(c), slot).wait()

        @pl.when(c + 1 < n_ch)
        def _():
            make_in(chunk_row(c + 1), 1 - slot).start(priority=1)

        @pl.when(jnp.logical_and(c + 1 == n_ch, feeds_next))
        def _():
            make_in(next_start, 1 - slot).start(priority=1)

        request_weights(c * W_PER_CHUNK, W_PER_CHUNK)

        @pl.when(state[1 + slot] == 1)
        def _():
            make_out(chunk_row(c), slot).wait()

        compute_chunk(slot)
        make_out(chunk_row(c), slot).start(priority=1)
        state[1 + slot] = 1
        return carry

    lax.fori_loop(0, n_ch, body, 0)
    request_weights(n_ch * W_PER_CHUNK, W_PIECES)
    state[0] = g0 + n_ch
    state[3] = jnp.where(jnp.logical_and(n_ch > 0, feeds_next), 1, 0)

    @pl.when(tail > 0)
    def _():
        make_tail_out(tail_row).wait()

    @pl.when(step == n_steps - 1)
    def _():
        for slot in range(2):
            @pl.when(state[1 + slot] == 1)
            def _():
                make_out(0, slot).wait()
                state[1 + slot] = 0


def _rows_at(row0, n):
    return pl.ds(row0 if isinstance(row0, int) else pl.multiple_of(row0, SEG_ALIGN), n)


def _moe_up_kernel(seg_ref, rows_ref, slack_ref, xs_ref, w_hbm, bg_ref, bl_ref, act_ref,
                   xbuf, obuf, xtail, otail, wbuf, wg_bf, wl_bf, state, isem, osem, tsem, wsem):
    f = pl.program_id(0)
    e = pl.program_id(1)
    start = seg_ref[e]
    rows = rows_ref[e]
    e_next = lax.rem(e + 1, N_EXPERTS)
    f_next = jnp.where(e == N_EXPERTS - 1, f + 1, f)
    step = f * N_EXPERTS + e
    n_steps = MOE_NF * N_EXPERTS
    wslot = lax.rem(step, 2)

    def weight_piece(expert, ftile, slot, p):
        per_half = W_PIECES // 2
        band = D_MODEL // per_half
        t = p // per_half if isinstance(p, int) else lax.shift_right_logical(p, 2)
        r0 = (p - t * per_half) * band
        r0 = r0 if isinstance(r0, int) else pl.multiple_of(r0, band)
        col0 = pl.multiple_of((t * MOE_NF + ftile) * MOE_TF, MOE_TF)
        return pltpu.make_async_copy(
            w_hbm.at[expert, pl.ds(r0, band), pl.ds(col0, MOE_TF)],
            wbuf.at[slot, t, pl.ds(r0, band)], wsem.at[slot])

    @pl.when(step == 0)
    def _():
        for p in range(W_PIECES):
            weight_piece(e, f, wslot, p).start()

    def make_in(r0, slot):
        return pltpu.make_async_copy(xs_ref.at[_rows_at(r0, MOE_CH)], xbuf.at[slot], isem.at[slot])

    def make_out(r0, slot):
        return pltpu.make_async_copy(obuf.at[slot], act_ref.at[f, _rows_at(r0, MOE_CH)],
                                     osem.at[slot])

    def make_tail_in(r0):
        return pltpu.make_async_copy(xs_ref.at[_rows_at(r0, SEG_ALIGN)], xtail, tsem.at[0])

    def make_tail_out(r0):
        return pltpu.make_async_copy(otail, act_ref.at[f, _rows_at(r0, SEG_ALIGN)], tsem.at[1])

    def cast_weights():
        for p in range(W_PIECES):
            weight_piece(e, f, wslot, p).wait()

        @pl.when(rows > 0)
        def _():
            wg_bf[...] = wbuf[wslot, 0].astype(BF16)
            wl_bf[...] = wbuf[wslot, 1].astype(BF16)

    def expert_mlp(words):
        lo, hi = _unpack_bf16_pairs(words)
        glu = (jnp.dot(lo, wg_bf[:HALF, :], preferred_element_type=F32)
               + jnp.dot(hi, wg_bf[HALF:, :], preferred_element_type=F32) + bg_ref[0])
        lin = (jnp.dot(lo, wl_bf[:HALF, :], preferred_element_type=F32)
               + jnp.dot(hi, wl_bf[HALF:, :], preferred_element_type=F32) + bl_ref[0])
        glu = jnp.minimum(glu, SWIGLU_LIMIT)
        lin = jnp.clip(lin, -SWIGLU_LIMIT, SWIGLU_LIMIT)
        return (glu * jax.nn.sigmoid(SWIGLU_ALPHA * glu) * (lin + 1.0)).astype(BF16)

    def compute_chunk(slot):
        obuf[slot] = expert_mlp(xbuf[slot])

    def compute_tail():
        otail[...] = expert_mlp(xtail[...])

    _stream_rows(step, n_steps, start, rows,
                 seg_ref[e_next], rows_ref[e_next], state,
                 make_in, make_out, make_tail_in, make_tail_out,
                 compute_chunk, compute_tail, cast_weights,
                 lambda p: weight_piece(e_next, f_next, 1 - wslot, p))

    @pl.when(e == N_EXPERTS - 1)
    def _():
        _zero_slack(slack_ref, otail, lambda rr: act_ref.at[f, rr], tsem.at[1])


def moe_up(seg_start, seg_rows, slack, xs, w1, b1):
    grid_spec = pltpu.PrefetchScalarGridSpec(
        num_scalar_prefetch=3,
        grid=(MOE_NF, N_EXPERTS),
        in_specs=[
            pl.BlockSpec(memory_space=pl.ANY),
            pl.BlockSpec(memory_space=pl.ANY),
            pl.BlockSpec((1, 1, MOE_TF), lambda f, e, s, r, z: (e, 0, f)),
            pl.BlockSpec((1, 1, MOE_TF), lambda f, e, s, r, z: (e, 0, MOE_NF + f)),
        ],
        out_specs=pl.BlockSpec(memory_space=pl.ANY),
        scratch_shapes=[
            pltpu.VMEM((2, MOE_CH, HALF), jnp.uint32),
            pltpu.VMEM((2, MOE_CH, MOE_TF), BF16),
            pltpu.VMEM((SEG_ALIGN, HALF), jnp.uint32),
            pltpu.VMEM((SEG_ALIGN, MOE_TF), BF16),
            pltpu.VMEM((2, 2, D_MODEL, MOE_TF), F32),
            pltpu.VMEM((D_MODEL, MOE_TF), BF16),
            pltpu.VMEM((D_MODEL, MOE_TF), BF16),
            pltpu.SMEM((4,), jnp.int32),
            pltpu.SemaphoreType.DMA((2,)),
            pltpu.SemaphoreType.DMA((2,)),
            pltpu.SemaphoreType.DMA((2,)),
            pltpu.SemaphoreType.DMA((2,)),
        ],
    )
    b13 = b1.reshape(N_EXPERTS, 1, 2 * D_FF)
    return pl.pallas_call(
        _moe_up_kernel,
        grid_spec=grid_spec,
        out_shape=jax.ShapeDtypeStruct((MOE_NF, MOE_R, MOE_TF), BF16),
        compiler_params=pltpu.CompilerParams(
            dimension_semantics=("arbitrary", "arbitrary"), vmem_limit_bytes=MOE_VMEM_LIMIT),
        name="moe_up",
    )(seg_start, seg_rows, slack, xs, w1, b13, b13)


def _moe_down_kernel(seg_ref, rows_ref, slack_ref, act_ref, w_hbm, b_ref, y_ref,
                     xbuf, obuf, xtail, otail, wbuf, w_bf, state, isem, osem, tsem, wsem):
    e = pl.program_id(0)
    start = seg_ref[e]
    rows = rows_ref[e]
    e_next = lax.rem(e + 1, N_EXPERTS)
    wslot = lax.rem(e, 2)

    def weight_piece(expert, slot, p):
        band = D_FF // W_PIECES
        r0 = p * band if isinstance(p, int) else pl.multiple_of(p * band, band)
        return pltpu.make_async_copy(w_hbm.at[expert, pl.ds(r0, band)],
                                     wbuf.at[slot, pl.ds(r0, band)], wsem.at[slot])

    @pl.when(e == 0)
    def _():
        for p in range(W_PIECES):
            weight_piece(e, wslot, p).start()

    def make_in(r0, slot):
        return _CopyGroup([
            pltpu.make_async_copy(act_ref.at[j, _rows_at(r0, MOE_CH)],
                                  xbuf.at[slot, :, pl.ds(j * MOE_TF, MOE_TF)], isem.at[slot])
            for j in range(MOE_NF)])

    def make_out(r0, slot):
        return pltpu.make_async_copy(obuf.at[slot], y_ref.at[_rows_at(r0, MOE_CH)], osem.at[slot])

    def make_tail_in(r0):
        return _CopyGroup([
            pltpu.make_async_copy(act_ref.at[j, _rows_at(r0, SEG_ALIGN)],
                                  xtail.at[:, pl.ds(j * MOE_TF, MOE_TF)], tsem.at[0])
            for j in range(MOE_NF)])

    def make_tail_out(r0):
        return pltpu.make_async_copy(otail, y_ref.at[_rows_at(r0, SEG_ALIGN)], tsem.at[1])

    def cast_weights():
        for p in range(W_PIECES):
            weight_piece(e, wslot, p).wait()

        @pl.when(rows > 0)
        def _():
            w_bf[...] = wbuf[wslot].astype(BF16)

    def expert_out(a):
        return _pack_bf16_pairs(jnp.dot(a, w_bf[...], preferred_element_type=F32) + b_ref[0])

    def compute_chunk(slot):
        obuf[slot] = expert_out(xbuf[slot])

    def compute_tail():
        otail[...] = expert_out(xtail[...])

    _stream_rows(e, N_EXPERTS, start, rows, seg_ref[e_next], rows_ref[e_next], state,
                 make_in, make_out, make_tail_in, make_tail_out,
                 compute_chunk, compute_tail, cast_weights,
                 lambda p: weight_piece(e_next, 1 - wslot, p))

    @pl.when(e == N_EXPERTS - 1)
    def _():
        _zero_slack(slack_ref, otail, lambda rr: y_ref.at[rr], tsem.at[1])


def moe_down(seg_start, seg_rows, slack, act, w2, b2):
    grid_spec = pltpu.PrefetchScalarGridSpec(
        num_scalar_prefetch=3,
        grid=(N_EXPERTS,),
        in_specs=[
            pl.BlockSpec(memory_space=pl.ANY),
            pl.BlockSpec(memory_space=pl.ANY),
            pl.BlockSpec((1, 1, D_MODEL), lambda e, s, r, z: (e, 0, 0)),
        ],
        out_specs=pl.BlockSpec(memory_space=pl.ANY),
        scratch_shapes=[
            pltpu.VMEM((2, MOE_CH, D_FF), BF16),
            pltpu.VMEM((2, MOE_CH, HALF), jnp.uint32),
            pltpu.VMEM((SEG_ALIGN, D_FF), BF16),
            pltpu.VMEM((SEG_ALIGN, HALF), jnp.uint32),
            pltpu.VMEM((2, D_FF, D_MODEL), F32),
            pltpu.VMEM((D_FF, D_MODEL), BF16),
            pltpu.SMEM((4,), jnp.int32),
            pltpu.SemaphoreType.DMA((2,)),
            pltpu.SemaphoreType.DMA((2,)),
            pltpu.SemaphoreType.DMA((2,)),
            pltpu.SemaphoreType.DMA((2,)),
        ],
    )
    return pl.pallas_call(
        _moe_down_kernel,
        grid_spec=grid_spec,
        out_shape=jax.ShapeDtypeStruct((MOE_R, HALF), jnp.uint32),
        compiler_params=pltpu.CompilerParams(
            dimension_semantics=("arbitrary",), vmem_limit_bytes=MOE_VMEM_LIMIT),
        name="moe_down",
    )(seg_start, seg_rows, slack, act, w2, b2.reshape(N_EXPERTS, 1, D_MODEL))


COMB_TM = 128
COMB_NT = N_TOK // COMB_TM


def _combine_kernel(dest_ref, y_ref, w_ref, h_ref, g_ref, o_ref, buf, sem):
    i = pl.program_id(0)

    def fetch(tile, slot):
        def tok(t, carry):
            a = (tile * COMB_TM + t) * TOP_K
            for kk in range(TOP_K):
                pltpu.make_async_copy(y_ref.at[pl.ds(dest_ref[a + kk], 1)],
                                      buf.at[slot, kk, pl.ds(t, 1)], sem.at[slot]).start(
                                          priority=kk % 2)
            return carry

        lax.fori_loop(0, COMB_TM, tok, 0, unroll=4)

    @pl.when(i == 0)
    def _():
        fetch(0, 0)

    slot = lax.rem(i, 2)

    @pl.when(i + 1 < COMB_NT)
    def _():
        fetch(i + 1, 1 - slot)

    for kk in range(TOP_K):
        pltpu.make_async_copy(y_ref.at[pl.ds(0, COMB_TM)], buf.at[slot, kk], sem.at[slot]).wait()
    w = w_ref[...]
    lo = h_ref[:, :HALF]
    hi = h_ref[:, HALF:]
    for kk in range(TOP_K):
        words = buf[slot, kk]
        wk = w[:, kk:kk + 1]
        lo = lo + wk * lax.bitcast_convert_type(lax.shift_left(words, jnp.uint32(16)), F32)
        hi = hi + wk * lax.bitcast_convert_type(words & jnp.uint32(0xFFFF0000), F32)
    ms = (jnp.sum(lo * lo, axis=-1, keepdims=True)
          + jnp.sum(hi * hi, axis=-1, keepdims=True)) * (1.0 / D_MODEL)
    inv = lax.rsqrt(ms + EPS)
    o_ref[:, :HALF] = lo * inv * g_ref[:, :HALF]
    o_ref[:, HALF:] = hi * inv * g_ref[:, HALF:]


def combine(dest, y, weight, h2, gain):
    grid_spec = pltpu.PrefetchScalarGridSpec(
        num_scalar_prefetch=1,
        grid=(COMB_NT,),
        in_specs=[
            pl.BlockSpec(memory_space=pl.ANY),
            pl.BlockSpec((COMB_TM, 128), lambda i, d: (i, 0)),
            pl.BlockSpec((COMB_TM, D_MODEL), lambda i, d: (i, 0)),
            pl.BlockSpec((1, D_MODEL), lambda i, d: (0, 0)),
        ],
        out_specs=pl.BlockSpec((COMB_TM, D_MODEL), lambda i, d: (i, 0)),
        scratch_shapes=[pltpu.VMEM((2, TOP_K, COMB_TM, HALF), jnp.uint32),
                        pltpu.SemaphoreType.DMA((2,))],
    )
    return pl.pallas_call(
        _combine_kernel,
        grid_spec=grid_spec,
        out_shape=jax.ShapeDtypeStruct((N_TOK, D_MODEL), F32),
        compiler_params=_cparams(1),
        name="moe_combine",
    )(dest, y, weight, h2, gain.reshape(1, D_MODEL))


def kernel(x, meta_tokens, rel_bias, norm_mix, w_in, conv_w, gate_bias_m, lambda_params, subln_da,
           w_branch_da, w_branch_m, w_gate, b_gate, w_out, norm_ffn, w_router, b_router,
           w1, b1, w2, b2, norm_final):
    layer = 0
    xn_pad, xn_real = norm_in(x, meta_tokens, norm_mix[layer])
    xn_pad2 = xn_pad.reshape(BATCH * LP, D_MODEL)
    proj = matmul(xn_pad2, w_in[layer], n_cols=PROJ_COLS, tm=768, tn=1024, name="proj_in")
    proj3 = proj.reshape(BATCH, LP, PROJ_COLS)
    w_g = jnp.pad(w_in[layer][:, COL_M_G:], ((0, 0), (0, 128 - 4 * H_M)))
    mg = matmul(xn_pad2, w_g, n_cols=128, tm=768, tn=128, out_dtype=F32, name="proj_gates")
    gate = matmul(xn_real.reshape(N_TOK, D_MODEL), w_gate[layer], n_cols=2 * D_MODEL,
                  tm=1024, tn=1024, bias=b_gate[layer], act="sigmoid", name="mix_gate")

    tab, consts = _bias_tables(rel_bias)
    y_da = diff_attention(consts, proj3, tab, lambda_params[layer], subln_da[layer])

    qk_m = conv_qk(proj3, conv_w[layer])
    gp = gate_prep(mg.reshape(BATCH, LP, 128), gate_bias_m[layer])
    gp4 = gp.reshape(BATCH, LP, 4, H_M)
    gcol = jnp.transpose(gp4, (0, 3, 1, 2))
    grow = jnp.transpose(gp4, (0, 3, 2, 1))
    y_m = mlstm(qk_m, proj3, gcol, grow)

    mixed = branch_mix(y_da.reshape(N_TOK, H_DA * DV_DA), y_m.reshape(N_TOK, H_M * DV_M),
                       w_branch_da[layer], w_branch_m[layer], gate)
    h2 = matmul(mixed, w_out[layer], n_cols=D_MODEL, tm=1024, tn=1024,
                res=x.reshape(N_TOK, D_MODEL), out_dtype=F32, name="out_proj")

    hn_packed, top_e, weight, rank, counts = ffn_prep(
        h2, norm_ffn[layer], w_router[layer], b_router[layer])
    dest, seg_start, seg_rows, pad0, padn, slack = _plan(
        counts, top_e[:, :TOP_K], rank[:, :TOP_K])
    xs = dispatch(dest, pad0, padn, slack, hn_packed)
    act = moe_up(seg_start, seg_rows, slack, xs, w1[layer], b1[layer])
    y = moe_down(seg_start, seg_rows, slack, act, w2[layer], b2[layer])
    out = combine(dest, y, weight, h2, norm_final)
    return out.reshape(BATCH, SEQ, D_MODEL)
```

```python
import functools
import math

import jax
import jax.numpy as jnp
from jax import lax
from jax.experimental import pallas as pl
from jax.experimental.pallas import tpu as pltpu

F32 = jnp.float32
BF16 = jnp.bfloat16

D_MODEL = 2048
BATCH = 2
SEQ = 4096
N_META = 16
BLOCK = 128
PAD = (-N_META) % BLOCK
LP = PAD + N_META + SEQ
NBLK = LP // BLOCK
EPS = 1e-6
NEG_INF = -1e30

H_DA = 4
DK_DA = 128
DV_DA = 256
H_M = 4
DK_M = 128
DV_M = 256
CONV_W = 5
N_BUCKETS = 32
MAX_DISTANCE = 128
N_EXPERTS = 32
TOP_K = 4
D_FF = 2048
SWIGLU_ALPHA = 1.702
SWIGLU_LIMIT = 7.0
LAMBDA_INIT = 0.8 - 0.6 * math.exp(-0.3 * 0)

COL_DA_Q = 0
COL_DA_K = 1024
COL_DA_V = 2048
COL_M_Q = 3072
COL_M_K = 3584
COL_M_V = 4096
COL_M_O = 5120
COL_M_G = 6144
PROJ_COLS = 6144

N_TOK = BATCH * SEQ
N_ASSIGN = N_TOK * TOP_K

VMEM_LIMIT = 52 * 1024 * 1024
MOE_VMEM_LIMIT = 58 * 1024 * 1024


def _cparams(n_axes):
    return pltpu.CompilerParams(
        dimension_semantics=("arbitrary",) * n_axes, vmem_limit_bytes=VMEM_LIMIT)


def _rms(v, gain):
    ms = jnp.mean(v * v, axis=-1, keepdims=True)
    return v * lax.rsqrt(ms + EPS) * gain


def _norm_in_kernel(x_ref, meta_ref, g_ref, pad_ref, real_ref):
    j = pl.program_id(1)
    g = g_ref[...]

    @pl.when(j == 0)
    def _():
        pad_ref[0, :PAD, :] = jnp.zeros((PAD, D_MODEL), BF16)
        pad_ref[0, PAD:, :] = _rms(meta_ref[...], g).astype(BF16)

    @pl.when(j > 0)
    def _():
        y = _rms(x_ref[0], g).astype(BF16)
        pad_ref[0] = y
        real_ref[0] = y


def norm_in(x, meta, gain):
    return pl.pallas_call(
        _norm_in_kernel,
        grid=(BATCH, NBLK),
        in_specs=[
            pl.BlockSpec((1, BLOCK, D_MODEL), lambda b, j: (b, jnp.maximum(j - 1, 0), 0)),
            pl.BlockSpec((N_META, D_MODEL), lambda b, j: (0, 0)),
            pl.BlockSpec((1, D_MODEL), lambda b, j: (0, 0)),
        ],
        out_specs=[
            pl.BlockSpec((1, BLOCK, D_MODEL), lambda b, j: (b, j, 0)),
            pl.BlockSpec((1, BLOCK, D_MODEL), lambda b, j: (b, jnp.maximum(j - 1, 0), 0)),
        ],
        out_shape=[
            jax.ShapeDtypeStruct((BATCH, LP, D_MODEL), BF16),
            jax.ShapeDtypeStruct((BATCH, SEQ, D_MODEL), BF16),
        ],
        compiler_params=_cparams(2),
        name="norm_in",
    )(x, meta, gain.reshape(1, D_MODEL))


def _mm_kernel(*refs, has_bias, has_res, act):
    x_ref, w_ref = refs[0], refs[1]
    pos = 2
    b_ref = r_ref = None
    if has_bias:
        b_ref = refs[pos]
        pos += 1
    if has_res:
        r_ref = refs[pos]
        pos += 1
    o_ref, wbf_ref = refs[pos], refs[pos + 1]

    @pl.when(pl.program_id(1) == 0)
    def _():
        wbf_ref[...] = w_ref[...].astype(BF16)

    acc = jnp.dot(x_ref[...], wbf_ref[...], preferred_element_type=F32)
    if has_bias:
        acc = acc + b_ref[...]
    if act == "sigmoid":
        acc = jax.nn.sigmoid(acc)
    if has_res:
        acc = acc + r_ref[...]
    o_ref[...] = acc.astype(o_ref.dtype)


def matmul(x, w, *, n_cols, col_block0=0, tm, tn, bias=None, res=None, act=None,
           out_dtype=BF16, name):
    m, k = x.shape
    in_specs = [
        pl.BlockSpec((tm, k), lambda j, i: (i, 0)),
        pl.BlockSpec((k, tn), lambda j, i: (0, j + col_block0)),
    ]
    args = [x, w]
    if bias is not None:
        in_specs.append(pl.BlockSpec((1, tn), lambda j, i: (0, j)))
        args.append(bias.reshape(1, n_cols))
    if res is not None:
        in_specs.append(pl.BlockSpec((tm, tn), lambda j, i: (i, j)))
        args.append(res)
    return pl.pallas_call(
        functools.partial(_mm_kernel, has_bias=bias is not None, has_res=res is not None, act=act),
        grid=(n_cols // tn, m // tm),
        in_specs=in_specs,
        out_specs=pl.BlockSpec((tm, tn), lambda j, i: (i, j)),
        out_shape=jax.ShapeDtypeStruct((m, n_cols), out_dtype),
        scratch_shapes=[pltpu.VMEM((k, tn), BF16)],
        compiler_params=_cparams(2),
        name=name,
    )(*args)


def _conv_kernel(p_ref, w_ref, o_ref):
    c = pl.program_id(1)
    x = p_ref[0].astype(F32)
    w = w_ref[...]
    half = CONV_W // 2
    acc = w[half:half + 1, :] * x
    for j in range(CONV_W):
        if j != half:
            acc = acc + w[j:j + 1, :] * pltpu.roll(x, (half - j) % LP, axis=0)
    y = acc * jax.nn.sigmoid(acc)
    rows = lax.broadcasted_iota(jnp.int32, (LP, 1), 0)
    y = jnp.where(rows >= PAD, y, 0.0)
    scale = jnp.where(c < 2, DK_M ** -0.5, 1.0).astype(F32)
    o_ref[0] = (y * scale).astype(BF16)


def conv_qk(proj3, conv_w):
    cw = 256
    return pl.pallas_call(
        _conv_kernel,
        grid=(BATCH, (2 * H_M * DK_M) // cw),
        in_specs=[
            pl.BlockSpec((1, LP, cw), lambda b, c: (b, 0, COL_M_Q // cw + c)),
            pl.BlockSpec((CONV_W, cw), lambda b, c: (0, c)),
        ],
        out_specs=pl.BlockSpec((1, LP, cw), lambda b, c: (b, 0, c)),
        out_shape=jax.ShapeDtypeStruct((BATCH, LP, 2 * H_M * DK_M), BF16),
        compiler_params=_cparams(2),
        name="conv_qk",
    )(proj3, conv_w)


def _split_dot(tri, v):
    hi = v.astype(BF16)
    r1 = v - hi.astype(F32)
    mid = r1.astype(BF16)
    lo = (r1 - mid.astype(F32)).astype(BF16)
    return (jnp.dot(tri, hi, preferred_element_type=F32)
            + jnp.dot(tri, mid, preferred_element_type=F32)
            + jnp.dot(tri, lo, preferred_element_type=F32))


def _gate_kernel(g_ref, bias_ref, o_ref):
    ti = lax.broadcasted_iota(jnp.int32, (BLOCK, BLOCK), 0)
    ui = lax.broadcasted_iota(jnp.int32, (BLOCK, BLOCK), 1)
    tril = jnp.where(ui <= ti, 1.0, 0.0).astype(BF16)
    triu = jnp.where(ui >= ti, 1.0, 0.0).astype(BF16)
    ch = lax.broadcasted_iota(jnp.int32, (BLOCK, 4 * H_M), 1)
    typ = lax.shift_right_logical(ch, 2)
    rloc = lax.broadcasted_iota(jnp.int32, (BLOCK, 4 * H_M), 0)

    def body(c, carry):
        r0 = pl.multiple_of(c * BLOCK, BLOCK)
        g = g_ref[0, pl.ds(r0, BLOCK), :][:, :4 * H_M] + bias_ref[...]
        valid = (rloc + r0) >= PAD
        lsig = -(jnp.maximum(-g, 0.0) + jnp.log1p(jnp.exp(-jnp.abs(g))))
        lf = jnp.where(valid, lsig, 0.0)
        cum = _split_dot(tril, lf)
        rcum = _split_dot(triu, lf)
        li = jnp.where(valid, g, -jnp.inf)
        out = jnp.where(typ == 1, cum, jnp.where(typ == 3, rcum, li))
        o_ref[0, pl.ds(r0, BLOCK), :] = out
        return carry

    lax.fori_loop(0, NBLK, body, 0)


def gate_prep(mg3, gate_bias):
    return pl.pallas_call(
        _gate_kernel,
        grid=(BATCH,),
        in_specs=[
            pl.BlockSpec((1, LP, 128), lambda b: (b, 0, 0)),
            pl.BlockSpec((1, 4 * H_M), lambda b: (0, 0)),
        ],
        out_specs=pl.BlockSpec((1, LP, 4 * H_M), lambda b: (b, 0, 0)),
        out_shape=jax.ShapeDtypeStruct((BATCH, LP, 4 * H_M), F32),
        compiler_params=_cparams(1),
        name="gate_prep",
    )(mg3, gate_bias.reshape(1, 4 * H_M))


MLSTM_HP = 2
MLSTM_MID = NBLK // 2


def _mlstm_kernel(q_ref, k_ref, v_ref, o_ref, gc_ref, gr_ref, y_ref,
                  hs_ref, c_ref, n_ref, m_ref):
    c_ref[...] = jnp.zeros_like(c_ref)
    n_ref[...] = jnp.zeros_like(n_ref)
    m_ref[...] = jnp.zeros_like(m_ref)
    ti = lax.broadcasted_iota(jnp.int32, (BLOCK, BLOCK), 0)
    si = lax.broadcasted_iota(jnp.int32, (BLOCK, BLOCK), 1)
    mask_f = si <= ti
    mask_b = si >= ti

    def chain(c, hl, bwd, final):
        idx = 2 * hl + bwd
        r0 = c * BLOCK if isinstance(c, int) else pl.multiple_of(c * BLOCK, BLOCK)
        q = q_ref[0, pl.ds(r0, BLOCK), hl * DK_M:(hl + 1) * DK_M]
        k = k_ref[0, pl.ds(r0, BLOCK), hl * DK_M:(hl + 1) * DK_M]
        v = v_ref[0, pl.ds(r0, BLOCK), hl * DV_M:(hl + 1) * DV_M]
        gc = gc_ref[0, hl, pl.ds(r0, BLOCK), :]
        gr = gr_ref[0, hl, :, pl.ds(r0, BLOCK)]
        a = 2 * bwd
        li_c, b_c = gc[:, a:a + 1], gc[:, a + 1:a + 2]
        li_r, b_r = gr[a:a + 1, :], gr[a + 1:a + 2, :]
        b_end = b_c[0:1, :] if bwd else b_c[BLOCK - 1:BLOCK, :]
        m_prev = m_ref[idx][:, 0:1]
        cst = c_ref[idx]
        nst = n_ref[idx]
        dmat = jnp.where(mask_b if bwd else mask_f, b_c - b_r + li_r, -jnp.inf)
        inter = b_c + m_prev
        m_t = jnp.maximum(inter, jnp.max(dmat, axis=1, keepdims=True))
        w_inter = jnp.exp(inter - m_t)
        qk = lax.dot_general(q, k, (((1,), (1,)), ((), ())), preferred_element_type=F32)
        s = qk * jnp.exp(dmat - m_t)
        num = (w_inter * jnp.dot(q, cst.astype(BF16), preferred_element_type=F32)
               + jnp.dot(s.astype(BF16), v, preferred_element_type=F32))
        den = jnp.sum(s + w_inter * (q.astype(F32) * nst), axis=1, keepdims=True)
        h = num / jnp.maximum(jnp.abs(den), jnp.exp(-m_t))
        cols = slice(hl * DV_M, (hl + 1) * DV_M)
        if final:
            og = jax.nn.sigmoid(o_ref[0, pl.ds(r0, BLOCK), cols].astype(F32))
            y_ref[0, pl.ds(r0 - BLOCK, BLOCK), cols] = (
                og * (hs_ref[pl.ds(r0, BLOCK), cols] + h)).astype(BF16)
        else:
            hs_ref[pl.ds(r0, BLOCK), cols] = h
        ldec_c = b_end - b_c + li_c
        ldec_r = b_end - b_r + li_r
        m_new = jnp.maximum(b_end + m_prev, jnp.max(ldec_r, axis=1, keepdims=True))
        w_c = jnp.exp(b_end + m_prev - m_new)
        w_s = jnp.exp(ldec_c - m_new)
        wv = (w_s * v.astype(F32)).astype(BF16)
        c_ref[idx] = w_c * cst + lax.dot_general(
            k, wv, (((0,), (0,)), ((), ())), preferred_element_type=F32)
        n_ref[idx] = w_c * nst + jnp.sum(w_s * k.astype(F32), axis=0, keepdims=True)
        m_ref[idx] = jnp.broadcast_to(m_new, (1, BLOCK))

    def first_half(i, carry):
        for hl in range(MLSTM_HP):
            chain(i, hl, 0, False)
            chain(NBLK - 1 - i, hl, 1, False)
        return carry

    def second_half(i, carry):
        for hl in range(MLSTM_HP):
            chain(i, hl, 0, True)
            chain(NBLK - 1 - i, hl, 1, True)
        return carry

    lax.fori_loop(0, MLSTM_MID, first_half, 0)
    for hl in range(MLSTM_HP):
        chain(MLSTM_MID, hl, 0, False)
        chain(MLSTM_MID, hl, 1, True)
    lax.fori_loop(MLSTM_MID + 1, NBLK - 1, second_half, 0)
    for hl in range(MLSTM_HP):
        chain(NBLK - 1, hl, 0, True)


def mlstm(qk_m, proj3, gcol, grow):
    hp = MLSTM_HP
    kw, vw = hp * DK_M, hp * DV_M
    return pl.pallas_call(
        _mlstm_kernel,
        grid=(BATCH, H_M // hp),
        in_specs=[
            pl.BlockSpec((1, LP, kw), lambda b, g: (b, 0, g)),
            pl.BlockSpec((1, LP, kw), lambda b, g: (b, 0, (H_M * DK_M) // kw + g)),
            pl.BlockSpec((1, LP, vw), lambda b, g: (b, 0, COL_M_V // vw + g)),
            pl.BlockSpec((1, LP, vw), lambda b, g: (b, 0, COL_M_O // vw + g)),
            pl.BlockSpec((1, hp, LP, 4), lambda b, g: (b, g, 0, 0)),
            pl.BlockSpec((1, hp, 4, LP), lambda b, g: (b, g, 0, 0)),
        ],
        out_specs=pl.BlockSpec((1, SEQ, vw), lambda b, g: (b, 0, g)),
        out_shape=jax.ShapeDtypeStruct((BATCH, SEQ, H_M * DV_M), BF16),
        scratch_shapes=[
            pltpu.VMEM((LP, vw), F32),
            pltpu.VMEM((2 * hp, DK_M, DV_M), F32),
            pltpu.VMEM((2 * hp, 1, DK_M), F32),
            pltpu.VMEM((2 * hp, 1, BLOCK), F32),
        ],
        compiler_params=_cparams(2),
        name="mlstm",
    )(qk_m, qk_m, proj3, proj3, gcol, grow)


LOG2E = 1.4426950408889634
ATT_QB = 2
ATT_TQ = ATT_QB * BLOCK
ATT_BAND = (ATT_QB + 2) * BLOCK
ATT_GROUPS = (6, 6, 6, 6, 5)
assert BLOCK >= MAX_DISTANCE and ATT_BAND + sum(ATT_GROUPS) * BLOCK == LP


def _attn_kernel(c_ref, qa_ref, qb_ref, k1_ref, v1_ref, tab_ref, lam_ref, sg_ref, o_ref,
                 s_ref, k_ref, v_ref):
    h = pl.program_id(1)
    qb = ATT_QB * pl.program_id(2) + 1

    @pl.when(pl.program_id(2) == 0)
    def _():
        for rep in range(2):
            k_ref[0, rep * LP:(rep + 1) * LP, :] = k1_ref[0]
            v_ref[0, rep * LP:(rep + 1) * LP, :] = v1_ref[0]

    q = jnp.concatenate([qa_ref[0], qb_ref[0]], axis=0)
    scale = DK_DA ** -0.5 * LOG2E
    c_neg = c_ref[h, 0]
    c_pos = c_ref[h, 1]
    lp = lam_ref[...]
    lam = (jnp.exp(jnp.sum(lp[0:1] * lp[1:2], axis=1, keepdims=True))
           - jnp.exp(jnp.sum(lp[2:3] * lp[3:4], axis=1, keepdims=True)) + LAMBDA_INIT)

    groups = [((qb - 1) * BLOCK, ATT_BAND, 0)]
    col = ATT_BAND
    for nblk in ATT_GROUPS:
        groups.append(((qb - 1) * BLOCK + col, nblk * BLOCK, col))
        col += nblk * BLOCK

    def lane_fold(acc, t, op):
        for j in range(t.shape[1] // BLOCK):
            piece = t[:, j * BLOCK:(j + 1) * BLOCK]
            acc = piece if acc is None else op(acc, piece)
        return acc

    mx = [None, None]
    for gi, (koff, width, col0) in enumerate(groups):
        koff = pl.multiple_of(koff, BLOCK)
        if gi == 0:
            bias = tab_ref[0, 0]
        else:
            kpos = koff + lax.broadcasted_iota(jnp.int32, (1, width), 1)
            bias = jnp.where(kpos < LP, c_pos, jnp.where(kpos < LP + PAD, NEG_INF, c_neg))
        for m in range(2):
            kk = k_ref[0, pl.ds(koff, width), m * DK_DA:(m + 1) * DK_DA]
            s = lax.dot_general(q[:, m * DK_DA:(m + 1) * DK_DA], kk, (((1,), (1,)), ((), ())),
                                preferred_element_type=F32) * scale + bias
            s_ref[m, :, col0:col0 + width] = s
            mx[m] = lane_fold(mx[m], s, jnp.maximum)
    row_max = [jnp.max(mx[m], axis=1, keepdims=True) for m in range(2)]

    lsum = [None, None]
    acc = [None, None]
    for koff, width, col0 in groups:
        koff = pl.multiple_of(koff, BLOCK)
        vv = v_ref[0, pl.ds(koff, width), :]
        for m in range(2):
            p = jnp.exp2(s_ref[m, :, col0:col0 + width] - row_max[m])
            lsum[m] = lane_fold(lsum[m], p, jnp.add)
            pv = jnp.dot(p.astype(BF16), vv, preferred_element_type=F32)
            acc[m] = pv if acc[m] is None else acc[m] + pv
    l1 = jnp.sum(lsum[0], axis=1, keepdims=True)
    l2 = jnp.sum(lsum[1], axis=1, keepdims=True)
    o = acc[0] / l1 - lam * (acc[1] / l2)
    o_ref[0] = (_rms(o, sg_ref[...]) * (1.0 - LAMBDA_INIT)).astype(BF16)


def diff_attention(consts, proj3, tab, lam_params, subln):
    nq = SEQ // ATT_TQ
    kblk0 = COL_DA_K // (2 * DK_DA)
    vblk0 = COL_DA_V // DV_DA

    def tab_map(b, h, i):
        case = jnp.where(i == 0, 0, jnp.where(i == nq - 1, 2, 1))
        return (h, case, 0, 0)

    return pl.pallas_call(
        _attn_kernel,
        grid=(BATCH, H_DA, nq),
        in_specs=[
            pl.BlockSpec(memory_space=pltpu.SMEM),
            pl.BlockSpec((1, BLOCK, 2 * DK_DA), lambda b, h, i: (b, ATT_QB * i + 1, h)),
            pl.BlockSpec((1, BLOCK, 2 * DK_DA), lambda b, h, i: (b, ATT_QB * i + 2, h)),
            pl.BlockSpec((1, LP, 2 * DK_DA), lambda b, h, i: (b, 0, kblk0 + h)),
            pl.BlockSpec((1, LP, DV_DA), lambda b, h, i: (b, 0, vblk0 + h)),
            pl.BlockSpec((1, 1, ATT_TQ, ATT_BAND), tab_map),
            pl.BlockSpec((4, DK_DA), lambda b, h, i: (0, 0)),
            pl.BlockSpec((1, DV_DA), lambda b, h, i: (0, 0)),
        ],
        out_specs=pl.BlockSpec((1, ATT_TQ, DV_DA), lambda b, h, i: (b, i, h)),
        out_shape=jax.ShapeDtypeStruct((BATCH, SEQ, H_DA * DV_DA), BF16),
        scratch_shapes=[pltpu.VMEM((2, ATT_TQ, LP), F32),
                        pltpu.VMEM((1, 2 * LP, 2 * DK_DA), BF16),
                        pltpu.VMEM((1, 2 * LP, DV_DA), BF16)],
        compiler_params=_cparams(3),
        name="diff_attn",
    )(consts, proj3, proj3, proj3, proj3, tab, lam_params, subln.reshape(1, DV_DA))


def _bias_tables(rel_bias):
    rb = rel_bias.astype(F32)
    i = jnp.arange(ATT_TQ, dtype=jnp.int32)[:, None]
    j = jnp.arange(ATT_BAND, dtype=jnp.int32)[None, :]
    rel = j - BLOCK - i
    nb = N_BUCKETS // 2
    max_exact = nb // 2
    n = jnp.abs(rel)
    nf = jnp.maximum(n, 1).astype(F32)
    large = max_exact + (jnp.log(nf / max_exact) / math.log(MAX_DISTANCE / max_exact)
                         * (nb - max_exact)).astype(jnp.int32)
    large = jnp.minimum(large, nb - 1)
    bucket = jnp.where(rel > 0, nb, 0) + jnp.where(n < max_exact, n, large)
    hit = bucket[None, :, :, None] == jnp.arange(N_BUCKETS, dtype=jnp.int32)
    gen = jnp.sum(jnp.where(hit, rb.T[:, None, None, :], 0.0), axis=-1)
    c_neg = rb[nb - 1]
    c_pos = rb[N_BUCKETS - 1]
    jj = j[None]
    first = jnp.where(jj < PAD, NEG_INF, gen)
    wrap0 = ATT_BAND - BLOCK
    wrapped = jnp.where(jj - wrap0 < PAD, NEG_INF, c_neg[:, None, None])
    last = jnp.where(jj >= wrap0, wrapped, gen)
    tab = jnp.stack([first, gen, last], axis=1)
    consts = jnp.stack([c_neg, c_pos], axis=1)
    return tab * LOG2E, consts * LOG2E


def _mix_kernel(ya_ref, ym_ref, wa_ref, wm_ref, ga_ref, gm_ref, o_ref, wa_bf, wm_bf):
    @pl.when(pl.program_id(1) == 0)
    def _():
        wa_bf[...] = wa_ref[...].astype(BF16)
        wm_bf[...] = wm_ref[...].astype(BF16)

    a = jnp.dot(ya_ref[...], wa_bf[...], preferred_element_type=F32)
    m = jnp.dot(ym_ref[...], wm_bf[...], preferred_element_type=F32)
    o_ref[...] = (ga_ref[...].astype(F32) * a + gm_ref[...].astype(F32) * m).astype(BF16)


def branch_mix(y_da, y_m, w_da, w_m, gate, *, tm=512, tn=1024):
    m, k = y_da.shape
    nj = D_MODEL // tn
    return pl.pallas_call(
        _mix_kernel,
        grid=(nj, m // tm),
        in_specs=[
            pl.BlockSpec((tm, k), lambda j, i: (i, 0)),
            pl.BlockSpec((tm, k), lambda j, i: (i, 0)),
            pl.BlockSpec((k, tn), lambda j, i: (0, j)),
            pl.BlockSpec((k, tn), lambda j, i: (0, j)),
            pl.BlockSpec((tm, tn), lambda j, i: (i, j)),
            pl.BlockSpec((tm, tn), lambda j, i: (i, nj + j)),
        ],
        out_specs=pl.BlockSpec((tm, tn), lambda j, i: (i, j)),
        out_shape=jax.ShapeDtypeStruct((m, D_MODEL), BF16),
        scratch_shapes=[pltpu.VMEM((k, tn), BF16), pltpu.VMEM((k, tn), BF16)],
        compiler_params=_cparams(2),
        name="branch_mix",
    )(y_da, y_m, w_da, w_m, gate, gate)


FFN_TM = 1024
HALF = D_MODEL // 2


def _pack_bf16_pairs(v):
    lo = lax.bitcast_convert_type(v[:, :HALF].astype(BF16).astype(F32), jnp.uint32)
    hi = lax.bitcast_convert_type(v[:, HALF:].astype(BF16).astype(F32), jnp.uint32)
    return (hi & jnp.uint32(0xFFFF0000)) | lax.shift_right_logical(lo, jnp.uint32(16))


def _unpack_bf16_pairs(w):
    lo = lax.bitcast_convert_type(lax.shift_left(w, jnp.uint32(16)), F32).astype(BF16)
    hi = lax.bitcast_convert_type(w & jnp.uint32(0xFFFF0000), F32).astype(BF16)
    return lo, hi


def _ffn_prep_kernel(h_ref, g_ref, wr_ref, br_ref, hn_ref, e_ref, w_ref, r_ref, cnt_ref, base_ref):
    @pl.when(pl.program_id(0) == 0)
    def _():
        base_ref[...] = jnp.zeros_like(base_ref)

    hn = _rms(h_ref[...], g_ref[...])
    hn_ref[...] = _pack_bf16_pairs(hn)
    logits = jnp.dot(hn.astype(BF16), wr_ref[...].astype(BF16),
                     preferred_element_type=F32) + br_ref[...]
    lane = lax.broadcasted_iota(jnp.int32, (FFN_TM, N_EXPERTS), 1)
    lane_o = lax.broadcasted_iota(jnp.int32, (FFN_TM, 128), 1)
    ti = lax.broadcasted_iota(jnp.int32, (FFN_TM, FFN_TM), 0)
    ui = lax.broadcasted_iota(jnp.int32, (FFN_TM, FFN_TM), 1)
    tril = jnp.where(ui <= ti, 1.0, 0.0).astype(BF16)
    e_out = jnp.zeros((FFN_TM, 128), jnp.int32)
    r_out = jnp.zeros((FFN_TM, 128), jnp.int32)
    l_out = jnp.full((FFN_TM, 128), -jnp.inf, F32)
    base = base_ref[...]
    l = logits
    for kk in range(TOP_K):
        mk = jnp.max(l, axis=1, keepdims=True)
        ik = jnp.min(jnp.where(l == mk, lane, N_EXPERTS), axis=1, keepdims=True)
        hit = lane == ik
        oh = jnp.where(hit, 1.0, 0.0)
        cum = jnp.dot(tril, oh.astype(BF16), preferred_element_type=F32)
        rank = jnp.sum(oh * (cum + base), axis=1, keepdims=True) - 1.0
        base = base + jnp.sum(oh, axis=0, keepdims=True)
        e_out = jnp.where(lane_o == kk, ik, e_out)
        r_out = jnp.where(lane_o == kk, rank.astype(jnp.int32), r_out)
        l_out = jnp.where(lane_o == kk, mk, l_out)
        l = jnp.where(hit, -jnp.inf, l)
    base_ref[...] = base
    cnt_ref[...] = base
    ex = jnp.exp(l_out - jnp.max(l_out, axis=1, keepdims=True))
    e_ref[...] = e_out
    r_ref[...] = r_out
    w_ref[...] = ex / jnp.sum(ex, axis=1, keepdims=True)


def ffn_prep(h2, gain, w_router, b_router):
    row = lambda i: (i, 0)
    fixed = lambda i: (0, 0)
    return pl.pallas_call(
        _ffn_prep_kernel,
        grid=(N_TOK // FFN_TM,),
        in_specs=[
            pl.BlockSpec((FFN_TM, D_MODEL), row),
            pl.BlockSpec((1, D_MODEL), fixed),
            pl.BlockSpec((D_MODEL, N_EXPERTS), fixed),
            pl.BlockSpec((1, N_EXPERTS), fixed),
        ],
        out_specs=[
            pl.BlockSpec((FFN_TM, HALF), row),
            pl.BlockSpec((FFN_TM, 128), row),
            pl.BlockSpec((FFN_TM, 128), row),
            pl.BlockSpec((FFN_TM, 128), row),
            pl.BlockSpec((1, N_EXPERTS), fixed),
        ],
        out_shape=[
            jax.ShapeDtypeStruct((N_TOK, HALF), jnp.uint32),
            jax.ShapeDtypeStruct((N_TOK, 128), jnp.int32),
            jax.ShapeDtypeStruct((N_TOK, 128), F32),
            jax.ShapeDtypeStruct((N_TOK, 128), jnp.int32),
            jax.ShapeDtypeStruct((1, N_EXPERTS), F32),
        ],
        scratch_shapes=[pltpu.VMEM((1, N_EXPERTS), F32)],
        compiler_params=_cparams(1),
        name="ffn_prep",
    )(h2, gain.reshape(1, D_MODEL), w_router, b_router.reshape(1, N_EXPERTS))


SEG_ALIGN = 128
MOE_R = N_ASSIGN + N_EXPERTS * SEG_ALIGN


def _plan(counts_f, top_e, rank):
    counts = counts_f[0].astype(jnp.int32)
    seg_rows = (counts + SEG_ALIGN - 1) // SEG_ALIGN * SEG_ALIGN
    seg_start = jnp.cumsum(seg_rows) - seg_rows
    eq = top_e[:, :, None] == jnp.arange(N_EXPERTS, dtype=jnp.int32)[None, None, :]
    dest = jnp.sum(jnp.where(eq, seg_start[None, None, :], 0), axis=-1) + rank
    used = jnp.sum(seg_rows)
    slack = jnp.stack([used, (MOE_R - used) // SEG_ALIGN])
    return (dest.reshape(N_ASSIGN).astype(jnp.int32), seg_start.astype(jnp.int32),
            seg_rows.astype(jnp.int32), (seg_start + counts).astype(jnp.int32),
            (seg_rows - counts).astype(jnp.int32), slack.astype(jnp.int32))


def _zero_slack(slack_ref, zero_block, dst_rows, sem):
    zero_block[...] = jnp.zeros_like(zero_block)

    def copy(j):
        r0 = pl.multiple_of(slack_ref[0] + j * SEG_ALIGN, SEG_ALIGN)
        return pltpu.make_async_copy(zero_block, dst_rows(pl.ds(r0, SEG_ALIGN)), sem)

    def start(j, carry):
        copy(j).start()
        return carry

    def wait(j, carry):
        copy(j).wait()
        return carry

    lax.fori_loop(0, slack_ref[1], start, 0)
    lax.fori_loop(0, slack_ref[1], wait, 0)


DISP_TOK = 256


def _dispatch_kernel(dest_ref, pad0_ref, padn_ref, slack_ref, hn_ref, xs_ref,
                     zrow_ref, zblk_ref, sem, zsem):
    i = pl.program_id(0)

    @pl.when(i == 0)
    def _():
        zrow_ref[...] = jnp.zeros_like(zrow_ref)
        _zero_slack(slack_ref, zblk_ref, lambda rows: xs_ref.at[rows], zsem.at[0])

        def expert(e, carry):
            p0 = pad0_ref[e]
            pn = padn_ref[e]

            def zstart(r, c2):
                pltpu.make_async_copy(zrow_ref, xs_ref.at[pl.ds(p0 + r, 1)], zsem.at[0]).start()
                return c2

            def zwait(r, c2):
                pltpu.make_async_copy(zrow_ref, xs_ref.at[pl.ds(p0, 1)], zsem.at[0]).wait()
                return c2

            lax.fori_loop(0, pn, zstart, 0)
            lax.fori_loop(0, pn, zwait, 0)
            return carry

        lax.fori_loop(0, N_EXPERTS, expert, 0)

    def tok(t, carry):
        a = (i * DISP_TOK + t) * TOP_K
        for kk in range(TOP_K):
            pltpu.make_async_copy(hn_ref.at[pl.ds(t, 1)], xs_ref.at[pl.ds(dest_ref[a + kk], 1)],
                                  sem.at[0]).start(priority=kk % 2)
        return carry

    lax.fori_loop(0, DISP_TOK, tok, 0, unroll=4)
    for kk in range(TOP_K):
        pltpu.make_async_copy(hn_ref, xs_ref.at[pl.ds(0, DISP_TOK)], sem.at[0]).wait()


def dispatch(dest, pad0, padn, slack, hn_packed):
    grid_spec = pltpu.PrefetchScalarGridSpec(
        num_scalar_prefetch=4,
        grid=(N_TOK // DISP_TOK,),
        in_specs=[pl.BlockSpec((DISP_TOK, HALF), lambda i, d, p0, pn, z: (i, 0))],
        out_specs=pl.BlockSpec(memory_space=pl.ANY),
        scratch_shapes=[pltpu.VMEM((1, HALF), jnp.uint32),
                        pltpu.VMEM((SEG_ALIGN, HALF), jnp.uint32),
                        pltpu.SemaphoreType.DMA((1,)), pltpu.SemaphoreType.DMA((1,))],
    )
    return pl.pallas_call(
        _dispatch_kernel,
        grid_spec=grid_spec,
        out_shape=jax.ShapeDtypeStruct((MOE_R, HALF), jnp.uint32),
        compiler_params=_cparams(1),
        name="moe_dispatch",
    )(dest, pad0, padn, slack, hn_packed)


MOE_CH = 2 * SEG_ALIGN
MOE_TF = 1024
MOE_NF = D_FF // MOE_TF


class _CopyGroup:
    def __init__(self, copies):
        self.copies = copies

    def start(self, priority=0):
        for cp in self.copies:
            cp.start(priority=priority)

    def wait(self):
        for cp in self.copies:
            cp.wait()


W_PIECES = 16
W_PER_CHUNK = 4


def _stream_rows(step, n_steps, start, rows, next_start, next_rows, state,
                 make_in, make_out, make_tail_in, make_tail_out,
                 compute_chunk, compute_tail, before_first_wait, next_weight_piece):
    has_next = step + 1 < n_steps

    def request_weights(first, count):
        def one(p, carry):
            @pl.when(jnp.logical_and(has_next, p < W_PIECES))
            def _():
                next_weight_piece(p).start()
            return carry

        lax.fori_loop(first, first + count, one, 0)

    @pl.when(step == 0)
    def _():
        for j in range(4):
            state[j] = 0

    n_ch = lax.shift_right_logical(rows, MOE_CH.bit_length() - 1)
    tail = rows - n_ch * MOE_CH
    tail_row = start + n_ch * MOE_CH
    g0 = state[0]
    feeds_next = jnp.logical_and(step + 1 < n_steps, next_rows >= MOE_CH)

    def chunk_row(c):
        return start + c * MOE_CH

    @pl.when(tail > 0)
    def _():
        make_tail_in(tail_row).start(priority=1)

    @pl.when(jnp.logical_and(n_ch > 0, state[3] == 0))
    def _():
        make_in(start, lax.rem(g0, 2)).start(priority=1)

    before_first_wait()

    @pl.when(tail > 0)
    def _():
        make_tail_in(tail_row).wait()
        compute_tail()
        make_tail_out(tail_row).start(priority=1)

    def body(c, carry):
        slot = lax.rem(g0 + c, 2)
        make_in(chunk_row(c), slot).wait()

        @pl.when(c + 1 < n_ch)
        def _():
            make_in(chunk_row(c + 1), 1 - slot).start(priority=1)

        @pl.when(jnp.logical_and(c + 1 == n_ch, feeds_next))
        def _():
            make_in(next_start, 1 - slot).start(priority=1)

        request_weights(c * W_PER_CHUNK, W_PER_CHUNK)

        @pl.when(state[1 + slot] == 1)
        def _():
            make_out(chunk_row(c), slot).wait()

        compute_chunk(slot)
        make_out(chunk_row(c), slot).start(priority=1)
        state[1 + slot] = 1
        return carry

    lax.fori_loop(0, n_ch, body, 0)
    request_weights(n_ch * W_PER_CHUNK, W_PIECES)
    state[0] = g0 + n_ch
    state[3] = jnp.where(jnp.logical_and(n_ch > 0, feeds_next), 1, 0)

    @pl.when(tail > 0)
    def _():
        make_tail_out(tail_row).wait()

    @pl.when(step == n_steps - 1)
    def _():
        for slot in range(2):
            @pl.when(state[1 + slot] == 1)
            def _():
                make_out(0, slot).wait()
                state[1 + slot] = 0


def _rows_at(row0, n):
    return pl.ds(row0 if isinstance(row0, int) else pl.multiple_of(row0, SEG_ALIGN), n)


def _moe_up_kernel(seg_ref, rows_ref, slack_ref, xs_ref, w_hbm, bg_ref, bl_ref, act_ref,
                   xbuf, obuf, xtail, otail, wbuf, wg_bf, wl_bf, state, isem, osem, tsem, wsem):
    f = pl.program_id(0)
    e = pl.program_id(1)
    start = seg_ref[e]
    rows = rows_ref[e]
    e_next = lax.rem(e + 1, N_EXPERTS)
    f_next = jnp.where(e == N_EXPERTS - 1, f + 1, f)
    step = f * N_EXPERTS + e
    n_steps = MOE_NF * N_EXPERTS
    wslot = lax.rem(step, 2)

    def weight_piece(expert, ftile, slot, p):
        per_half = W_PIECES // 2
        band = D_MODEL // per_half
        t = p // per_half if isinstance(p, int) else lax.shift_right_logical(
            p, per_half.bit_length() - 1)
        r0 = (p - t * per_half) * band
        r0 = r0 if isinstance(r0, int) else pl.multiple_of(r0, band)
        col0 = pl.multiple_of((t * MOE_NF + ftile) * MOE_TF, MOE_TF)
        return pltpu.make_async_copy(
            w_hbm.at[expert, pl.ds(r0, band), pl.ds(col0, MOE_TF)],
            wbuf.at[slot, t, pl.ds(r0, band)], wsem.at[slot])

    @pl.when(step == 0)
    def _():
        for p in range(W_PIECES):
            weight_piece(e, f, wslot, p).start()

    def make_in(r0, slot):
        return pltpu.make_async_copy(xs_ref.at[_rows_at(r0, MOE_CH)], xbuf.at[slot], isem.at[slot])

    def make_out(r0, slot):
        return pltpu.make_async_copy(obuf.at[slot], act_ref.at[f, _rows_at(r0, MOE_CH)],
                                     osem.at[slot])

    def make_tail_in(r0):
        return pltpu.make_async_copy(xs_ref.at[_rows_at(r0, SEG_ALIGN)], xtail, tsem.at[0])

    def make_tail_out(r0):
        return pltpu.make_async_copy(otail, act_ref.at[f, _rows_at(r0, SEG_ALIGN)], tsem.at[1])

    def cast_weights():
        for p in range(W_PIECES):
            weight_piece(e, f, wslot, p).wait()

        @pl.when(rows > 0)
        def _():
            wg_bf[...] = wbuf[wslot, 0].astype(BF16)
            wl_bf[...] = wbuf[wslot, 1].astype(BF16)

    def expert_mlp(words):
        lo, hi = _unpack_bf16_pairs(words)
        glu = (jnp.dot(lo, wg_bf[:HALF, :], preferred_element_type=F32)
               + jnp.dot(hi, wg_bf[HALF:, :], preferred_element_type=F32) + bg_ref[0])
        lin = (jnp.dot(lo, wl_bf[:HALF, :], preferred_element_type=F32)
               + jnp.dot(hi, wl_bf[HALF:, :], preferred_element_type=F32) + bl_ref[0])
        glu = jnp.minimum(glu, SWIGLU_LIMIT)
        lin = jnp.clip(lin, -SWIGLU_LIMIT, SWIGLU_LIMIT)
        return (glu * jax.nn.sigmoid(SWIGLU_ALPHA * glu) * (lin + 1.0)).astype(BF16)

    def compute_chunk(slot):
        obuf[slot] = expert_mlp(xbuf[slot])

    def compute_tail():
        otail[...] = expert_mlp(xtail[...])

    _stream_rows(step, n_steps, start, rows,
                 seg_ref[e_next], rows_ref[e_next], state,
                 make_in, make_out, make_tail_in, make_tail_out,
                 compute_chunk, compute_tail, cast_weights,
                 lambda p: weight_piece(e_next, f_next, 1 - wslot, p))

    @pl.when(e == N_EXPERTS - 1)
    def _():
        _zero_slack(slack_ref, otail, lambda rr: act_ref.at[f, rr], tsem.at[1])


def moe_up(seg_start, seg_rows, slack, xs, w1, b1):
    grid_spec = pltpu.PrefetchScalarGridSpec(
        num_scalar_prefetch=3,
        grid=(MOE_NF, N_EXPERTS),
        in_specs=[
            pl.BlockSpec(memory_space=pl.ANY),
            pl.BlockSpec(memory_space=pl.ANY),
            pl.BlockSpec((1, 1, MOE_TF), lambda f, e, s, r, z: (e, 0, f)),
            pl.BlockSpec((1, 1, MOE_TF), lambda f, e, s, r, z: (e, 0, MOE_NF + f)),
        ],
        out_specs=pl.BlockSpec(memory_space=pl.ANY),
        scratch_shapes=[
            pltpu.VMEM((2, MOE_CH, HALF), jnp.uint32),
            pltpu.VMEM((2, MOE_CH, MOE_TF), BF16),
            pltpu.VMEM((SEG_ALIGN, HALF), jnp.uint32),
            pltpu.VMEM((SEG_ALIGN, MOE_TF), BF16),
            pltpu.VMEM((2, 2, D_MODEL, MOE_TF), F32),
            pltpu.VMEM((D_MODEL, MOE_TF), BF16),
            pltpu.VMEM((D_MODEL, MOE_TF), BF16),
            pltpu.SMEM((4,), jnp.int32),
            pltpu.SemaphoreType.DMA((2,)),
            pltpu.SemaphoreType.DMA((2,)),
            pltpu.SemaphoreType.DMA((2,)),
            pltpu.SemaphoreType.DMA((2,)),
        ],
    )
    b13 = b1.reshape(N_EXPERTS, 1, 2 * D_FF)
    return pl.pallas_call(
        _moe_up_kernel,
        grid_spec=grid_spec,
        out_shape=jax.ShapeDtypeStruct((MOE_NF, MOE_R, MOE_TF), BF16),
        compiler_params=pltpu.CompilerParams(
            dimension_semantics=("arbitrary", "arbitrary"), vmem_limit_bytes=MOE_VMEM_LIMIT),
        name="moe_up",
    )(seg_start, seg_rows, slack, xs, w1, b13, b13)


def _moe_down_kernel(seg_ref, rows_ref, slack_ref, act_ref, w_hbm, b_ref, y_ref,
                     xbuf, obuf, xtail, otail, wbuf, w_bf, state, isem, osem, tsem, wsem):
    e = pl.program_id(0)
    start = seg_ref[e]
    rows = rows_ref[e]
    e_next = lax.rem(e + 1, N_EXPERTS)
    wslot = lax.rem(e, 2)

    def weight_piece(expert, slot, p):
        band = D_FF // W_PIECES
        r0 = p * band if isinstance(p, int) else pl.multiple_of(p * band, band)
        return pltpu.make_async_copy(w_hbm.at[expert, pl.ds(r0, band)],
                                     wbuf.at[slot, pl.ds(r0, band)], wsem.at[slot])

    @pl.when(e == 0)
    def _():
        for p in range(W_PIECES):
            weight_piece(e, wslot, p).start()

    def make_in(r0, slot):
        return _CopyGroup([
            pltpu.make_async_copy(act_ref.at[j, _rows_at(r0, MOE_CH)],
                                  xbuf.at[slot, :, pl.ds(j * MOE_TF, MOE_TF)], isem.at[slot])
            for j in range(MOE_NF)])

    def make_out(r0, slot):
        return pltpu.make_async_copy(obuf.at[slot], y_ref.at[_rows_at(r0, MOE_CH)], osem.at[slot])

    def make_tail_in(r0):
        return _CopyGroup([
            pltpu.make_async_copy(act_ref.at[j, _rows_at(r0, SEG_ALIGN)],
                                  xtail.at[:, pl.ds(j * MOE_TF, MOE_TF)], tsem.at[0])
            for j in range(MOE_NF)])

    def make_tail_out(r0):
        return pltpu.make_async_copy(otail, y_ref.at[_rows_at(r0, SEG_ALIGN)], tsem.at[1])

    def cast_weights():
        for p in range(W_PIECES):
            weight_piece(e, wslot, p).wait()

        @pl.when(rows > 0)
        def _():
            w_bf[...] = wbuf[wslot].astype(BF16)

    def expert_out(a):
        return _pack_bf16_pairs(jnp.dot(a, w_bf[...], preferred_element_type=F32) + b_ref[0])

    def compute_chunk(slot):
        obuf[slot] = expert_out(xbuf[slot])

    def compute_tail():
        otail[...] = expert_out(xtail[...])

    _stream_rows(e, N_EXPERTS, start, rows, seg_ref[e_next], rows_ref[e_next], state,
                 make_in, make_out, make_tail_in, make_tail_out,
                 compute_chunk, compute_tail, cast_weights,
                 lambda p: weight_piece(e_next, 1 - wslot, p))

    @pl.when(e == N_EXPERTS - 1)
    def _():
        _zero_slack(slack_ref, otail, lambda rr: y_ref.at[rr], tsem.at[1])


def moe_down(seg_start, seg_rows, slack, act, w2, b2):
    grid_spec = pltpu.PrefetchScalarGridSpec(
        num_scalar_prefetch=3,
        grid=(N_EXPERTS,),
        in_specs=[
            pl.BlockSpec(memory_space=pl.ANY),
            pl.BlockSpec(memory_space=pl.ANY),
            pl.BlockSpec((1, 1, D_MODEL), lambda e, s, r, z: (e, 0, 0)),
        ],
        out_specs=pl.BlockSpec(memory_space=pl.ANY),
        scratch_shapes=[
            pltpu.VMEM((2, MOE_CH, D_FF), BF16),
            pltpu.VMEM((2, MOE_CH, HALF), jnp.uint32),
            pltpu.VMEM((SEG_ALIGN, D_FF), BF16),
            pltpu.VMEM((SEG_ALIGN, HALF), jnp.uint32),
            pltpu.VMEM((2, D_FF, D_MODEL), F32),
            pltpu.VMEM((D_FF, D_MODEL), BF16),
            pltpu.SMEM((4,), jnp.int32),
            pltpu.SemaphoreType.DMA((2,)),
            pltpu.SemaphoreType.DMA((2,)),
            pltpu.SemaphoreType.DMA((2,)),
            pltpu.SemaphoreType.DMA((2,)),
        ],
    )
    return pl.pallas_call(
        _moe_down_kernel,
        grid_spec=grid_spec,
        out_shape=jax.ShapeDtypeStruct((MOE_R, HALF), jnp.uint32),
        compiler_params=pltpu.CompilerParams(
            dimension_semantics=("arbitrary",), vmem_limit_bytes=MOE_VMEM_LIMIT),
        name="moe_down",
    )(seg_start, seg_rows, slack, act, w2, b2.reshape(N_EXPERTS, 1, D_MODEL))


COMB_TM = 128
COMB_NT = N_TOK // COMB_TM


def _combine_kernel(dest_ref, y_ref, w_ref, h_ref, g_ref, o_ref, buf, sem):
    i = pl.program_id(0)

    def fetch(tile, slot):
        def tok(t, carry):
            a = (tile * COMB_TM + t) * TOP_K
            for kk in range(TOP_K):
                pltpu.make_async_copy(y_ref.at[pl.ds(dest_ref[a + kk], 1)],
                                      buf.at[slot, kk, pl.ds(t, 1)], sem.at[slot]).start(
                                          priority=kk % 2)
            return carry

        lax.fori_loop(0, COMB_TM, tok, 0, unroll=4)

    @pl.when(i == 0)
    def _():
        fetch(0, 0)

    slot = lax.rem(i, 2)

    @pl.when(i + 1 < COMB_NT)
    def _():
        fetch(i + 1, 1 - slot)

    for kk in range(TOP_K):
        pltpu.make_async_copy(y_ref.at[pl.ds(0, COMB_TM)], buf.at[slot, kk], sem.at[slot]).wait()
    w = w_ref[...]
    lo = h_ref[:, :HALF]
    hi = h_ref[:, HALF:]
    for kk in range(TOP_K):
        words = buf[slot, kk]
        wk = w[:, kk:kk + 1]
        lo = lo + wk * lax.bitcast_convert_type(lax.shift_left(words, jnp.uint32(16)), F32)
        hi = hi + wk * lax.bitcast_convert_type(words & jnp.uint32(0xFFFF0000), F32)
    ms = (jnp.sum(lo * lo, axis=-1, keepdims=True)
          + jnp.sum(hi * hi, axis=-1, keepdims=True)) * (1.0 / D_MODEL)
    inv = lax.rsqrt(ms + EPS)
    o_ref[:, :HALF] = lo * inv * g_ref[:, :HALF]
    o_ref[:, HALF:] = hi * inv * g_ref[:, HALF:]


def combine(dest, y, weight, h2, gain):
    grid_spec = pltpu.PrefetchScalarGridSpec(
        num_scalar_prefetch=1,
        grid=(COMB_NT,),
        in_specs=[
            pl.BlockSpec(memory_space=pl.ANY),
            pl.BlockSpec((COMB_TM, 128), lambda i, d: (i, 0)),
            pl.BlockSpec((COMB_TM, D_MODEL), lambda i, d: (i, 0)),
            pl.BlockSpec((1, D_MODEL), lambda i, d: (0, 0)),
        ],
        out_specs=pl.BlockSpec((COMB_TM, D_MODEL), lambda i, d: (i, 0)),
        scratch_shapes=[pltpu.VMEM((2, TOP_K, COMB_TM, HALF), jnp.uint32),
                        pltpu.SemaphoreType.DMA((2,))],
    )
    return pl.pallas_call(
        _combine_kernel,
        grid_spec=grid_spec,
        out_shape=jax.ShapeDtypeStruct((N_TOK, D_MODEL), F32),
        compiler_params=_cparams(1),
        name="moe_combine",
    )(dest, y, weight, h2, gain.reshape(1, D_MODEL))


def kernel(x, meta_tokens, rel_bias, norm_mix, w_in, conv_w, gate_bias_m, lambda_params, subln_da,
           w_branch_da, w_branch_m, w_gate, b_gate, w_out, norm_ffn, w_router, b_router,
           w1, b1, w2, b2, norm_final):
    layer = 0
    xn_pad, xn_real = norm_in(x, meta_tokens, norm_mix[layer])
    xn_pad2 = xn_pad.reshape(BATCH * LP, D_MODEL)
    proj = matmul(xn_pad2, w_in[layer], n_cols=PROJ_COLS, tm=768, tn=1024, name="proj_in")
    proj3 = proj.reshape(BATCH, LP, PROJ_COLS)
    w_g = jnp.pad(w_in[layer][:, COL_M_G:], ((0, 0), (0, 128 - 4 * H_M)))
    mg = matmul(xn_pad2, w_g, n_cols=128, tm=768, tn=128, out_dtype=F32, name="proj_gates")
    gate = matmul(xn_real.reshape(N_TOK, D_MODEL), w_gate[layer], n_cols=2 * D_MODEL,
                  tm=1024, tn=1024, bias=b_gate[layer], act="sigmoid", name="mix_gate")

    tab, consts = _bias_tables(rel_bias)
    y_da = diff_attention(consts, proj3, tab, lambda_params[layer], subln_da[layer])

    qk_m = conv_qk(proj3, conv_w[layer])
    gp = gate_prep(mg.reshape(BATCH, LP, 128), gate_bias_m[layer])
    gp4 = gp.reshape(BATCH, LP, 4, H_M)
    gcol = jnp.transpose(gp4, (0, 3, 1, 2))
    grow = jnp.transpose(gp4, (0, 3, 2, 1))
    y_m = mlstm(qk_m, proj3, gcol, grow)

    mixed = branch_mix(y_da.reshape(N_TOK, H_DA * DV_DA), y_m.reshape(N_TOK, H_M * DV_M),
                       w_branch_da[layer], w_branch_m[layer], gate)
    h2 = matmul(mixed, w_out[layer], n_cols=D_MODEL, tm=1024, tn=1024,
                res=x.reshape(N_TOK, D_MODEL), out_dtype=F32, name="out_proj")

    hn_packed, top_e, weight, rank, counts = ffn_prep(
        h2, norm_ffn[layer], w_router[layer], b_router[layer])
    dest, seg_start, seg_rows, pad0, padn, slack = _plan(
        counts, top_e[:, :TOP_K], rank[:, :TOP_K])
    xs = dispatch(dest, pad0, padn, slack, hn_packed)
    act = moe_up(seg_start, seg_rows, slack, xs, w1[layer], b1[layer])
    y = moe_down(seg_start, seg_rows, slack, act, w2[layer], b2[layer])
    out = combine(dest, y, weight, h2, norm_final)
    return out.reshape(BATCH, SEQ, D_MODEL)
```

```python
import functools
import math

import jax
import jax.numpy as jnp
from jax import lax
from jax.experimental import pallas as pl
from jax.experimental.pallas import tpu as pltpu

F32 = jnp.float32
BF16 = jnp.bfloat16

D_MODEL = 2048
BATCH = 2
SEQ = 4096
N_META = 16
BLOCK = 128
PAD = (-N_META) % BLOCK
LP = PAD + N_META + SEQ
NBLK = LP // BLOCK
EPS = 1e-6
NEG_INF = -1e30

H_DA = 4
DK_DA = 128
DV_DA = 256
H_M = 4
DK_M = 128
DV_M = 256
CONV_W = 5
N_BUCKETS = 32
MAX_DISTANCE = 128
N_EXPERTS = 32
TOP_K = 4
D_FF = 2048
SWIGLU_ALPHA = 1.702
SWIGLU_LIMIT = 7.0
LAMBDA_INIT = 0.8 - 0.6 * math.exp(-0.3 * 0)

COL_DA_Q = 0
COL_DA_K = 1024
COL_DA_V = 2048
COL_M_Q = 3072
COL_M_K = 3584
COL_M_V = 4096
COL_M_O = 5120
COL_M_G = 6144
PROJ_COLS = 6144

N_TOK = BATCH * SEQ
N_ASSIGN = N_TOK * TOP_K

VMEM_LIMIT = 52 * 1024 * 1024
MOE_VMEM_LIMIT = 58 * 1024 * 1024


def _cparams(n_axes):
    return pltpu.CompilerParams(
        dimension_semantics=("arbitrary",) * n_axes, vmem_limit_bytes=VMEM_LIMIT)


def _rms(v, gain):
    ms = jnp.mean(v * v, axis=-1, keepdims=True)
    return v * lax.rsqrt(ms + EPS) * gain


def _norm_in_kernel(x_ref, meta_ref, g_ref, pad_ref, real_ref):
    j = pl.program_id(1)
    g = g_ref[...]

    @pl.when(j == 0)
    def _():
        pad_ref[0, :PAD, :] = jnp.zeros((PAD, D_MODEL), BF16)
        pad_ref[0, PAD:, :] = _rms(meta_ref[...], g).astype(BF16)

    @pl.when(j > 0)
    def _():
        y = _rms(x_ref[0], g).astype(BF16)
        pad_ref[0] = y
        real_ref[0] = y


def norm_in(x, meta, gain):
    return pl.pallas_call(
        _norm_in_kernel,
        grid=(BATCH, NBLK),
        in_specs=[
            pl.BlockSpec((1, BLOCK, D_MODEL), lambda b, j: (b, jnp.maximum(j - 1, 0), 0)),
            pl.BlockSpec((N_META, D_MODEL), lambda b, j: (0, 0)),
            pl.BlockSpec((1, D_MODEL), lambda b, j: (0, 0)),
        ],
        out_specs=[
            pl.BlockSpec((1, BLOCK, D_MODEL), lambda b, j: (b, j, 0)),
            pl.BlockSpec((1, BLOCK, D_MODEL), lambda b, j: (b, jnp.maximum(j - 1, 0), 0)),
        ],
        out_shape=[
            jax.ShapeDtypeStruct((BATCH, LP, D_MODEL), BF16),
            jax.ShapeDtypeStruct((BATCH, SEQ, D_MODEL), BF16),
        ],
        compiler_params=_cparams(2),
        name="norm_in",
    )(x, meta, gain.reshape(1, D_MODEL))


def _mm_kernel(*refs, has_bias, has_res, act):
    x_ref, w_ref = refs[0], refs[1]
    pos = 2
    b_ref = r_ref = None
    if has_bias:
        b_ref = refs[pos]
        pos += 1
    if has_res:
        r_ref = refs[pos]
        pos += 1
    o_ref, wbf_ref = refs[pos], refs[pos + 1]

    @pl.when(pl.program_id(1) == 0)
    def _():
        wbf_ref[...] = w_ref[...].astype(BF16)

    acc = jnp.dot(x_ref[...], wbf_ref[...], preferred_element_type=F32)
    if has_bias:
        acc = acc + b_ref[...]
    if act == "sigmoid":
        acc = jax.nn.sigmoid(acc)
    if has_res:
        acc = acc + r_ref[...]
    o_ref[...] = acc.astype(o_ref.dtype)


def matmul(x, w, *, n_cols, col_block0=0, tm, tn, bias=None, res=None, act=None,
           out_dtype=BF16, name):
    m, k = x.shape
    in_specs = [
        pl.BlockSpec((tm, k), lambda j, i: (i, 0)),
        pl.BlockSpec((k, tn), lambda j, i: (0, j + col_block0)),
    ]
    args = [x, w]
    if bias is not None:
        in_specs.append(pl.BlockSpec((1, tn), lambda j, i: (0, j)))
        args.append(bias.reshape(1, n_cols))
    if res is not None:
        in_specs.append(pl.BlockSpec((tm, tn), lambda j, i: (i, j)))
        args.append(res)
    return pl.pallas_call(
        functools.partial(_mm_kernel, has_bias=bias is not None, has_res=res is not None, act=act),
        grid=(n_cols // tn, m // tm),
        in_specs=in_specs,
        out_specs=pl.BlockSpec((tm, tn), lambda j, i: (i, j)),
        out_shape=jax.ShapeDtypeStruct((m, n_cols), out_dtype),
        scratch_shapes=[pltpu.VMEM((k, tn), BF16)],
        compiler_params=_cparams(2),
        name=name,
    )(*args)


def _conv_kernel(p_ref, w_ref, o_ref):
    c = pl.program_id(1)
    x = p_ref[0].astype(F32)
    w = w_ref[...]
    half = CONV_W // 2
    acc = w[half:half + 1, :] * x
    for j in range(CONV_W):
        if j != half:
            acc = acc + w[j:j + 1, :] * pltpu.roll(x, (half - j) % LP, axis=0)
    y = acc * jax.nn.sigmoid(acc)
    rows = lax.broadcasted_iota(jnp.int32, (LP, 1), 0)
    y = jnp.where(rows >= PAD, y, 0.0)
    scale = jnp.where(c < 2, DK_M ** -0.5, 1.0).astype(F32)
    o_ref[0] = (y * scale).astype(BF16)


def conv_qk(proj3, conv_w):
    cw = 256
    return pl.pallas_call(
        _conv_kernel,
        grid=(BATCH, (2 * H_M * DK_M) // cw),
        in_specs=[
            pl.BlockSpec((1, LP, cw), lambda b, c: (b, 0, COL_M_Q // cw + c)),
            pl.BlockSpec((CONV_W, cw), lambda b, c: (0, c)),
        ],
        out_specs=pl.BlockSpec((1, LP, cw), lambda b, c: (b, 0, c)),
        out_shape=jax.ShapeDtypeStruct((BATCH, LP, 2 * H_M * DK_M), BF16),
        compiler_params=_cparams(2),
        name="conv_qk",
    )(proj3, conv_w)


def _split_dot(tri, v):
    hi = v.astype(BF16)
    r1 = v - hi.astype(F32)
    mid = r1.astype(BF16)
    lo = (r1 - mid.astype(F32)).astype(BF16)
    return (jnp.dot(tri, hi, preferred_element_type=F32)
            + jnp.dot(tri, mid, preferred_element_type=F32)
            + jnp.dot(tri, lo, preferred_element_type=F32))


def _gate_kernel(g_ref, bias_ref, o_ref):
    ti = lax.broadcasted_iota(jnp.int32, (BLOCK, BLOCK), 0)
    ui = lax.broadcasted_iota(jnp.int32, (BLOCK, BLOCK), 1)
    tril = jnp.where(ui <= ti, 1.0, 0.0).astype(BF16)
    triu = jnp.where(ui >= ti, 1.0, 0.0).astype(BF16)
    ch = lax.broadcasted_iota(jnp.int32, (BLOCK, 4 * H_M), 1)
    typ = lax.shift_right_logical(ch, 2)
    rloc = lax.broadcasted_iota(jnp.int32, (BLOCK, 4 * H_M), 0)

    def body(c, carry):
        r0 = pl.multiple_of(c * BLOCK, BLOCK)
        g = g_ref[0, pl.ds(r0, BLOCK), :][:, :4 * H_M] + bias_ref[...]
        valid = (rloc + r0) >= PAD
        lsig = -(jnp.maximum(-g, 0.0) + jnp.log1p(jnp.exp(-jnp.abs(g))))
        lf = jnp.where(valid, lsig, 0.0)
        cum = _split_dot(tril, lf)
        rcum = _split_dot(triu, lf)
        li = jnp.where(valid, g, -jnp.inf)
        out = jnp.where(typ == 1, cum, jnp.where(typ == 3, rcum, li))
        o_ref[0, pl.ds(r0, BLOCK), :] = out
        return carry

    lax.fori_loop(0, NBLK, body, 0)


def gate_prep(mg3, gate_bias):
    return pl.pallas_call(
        _gate_kernel,
        grid=(BATCH,),
        in_specs=[
            pl.BlockSpec((1, LP, 128), lambda b: (b, 0, 0)),
            pl.BlockSpec((1, 4 * H_M), lambda b: (0, 0)),
        ],
        out_specs=pl.BlockSpec((1, LP, 4 * H_M), lambda b: (b, 0, 0)),
        out_shape=jax.ShapeDtypeStruct((BATCH, LP, 4 * H_M), F32),
        compiler_params=_cparams(1),
        name="gate_prep",
    )(mg3, gate_bias.reshape(1, 4 * H_M))


MLSTM_HP = 2
MLSTM_MID = NBLK // 2


def _mlstm_kernel(qt_ref, k_ref, vt_ref, ot_ref, ac_ref, gr_ref, y_ref,
                  hs_ref, c_ref, n_ref, m_ref):
    c_ref[...] = jnp.zeros_like(c_ref)
    n_ref[...] = jnp.zeros_like(n_ref)
    m_ref[...] = jnp.zeros_like(m_ref)
    si = lax.broadcasted_iota(jnp.int32, (BLOCK, BLOCK), 0)
    ti = lax.broadcasted_iota(jnp.int32, (BLOCK, BLOCK), 1)
    mask_f = si <= ti
    mask_b = si >= ti

    def chain(c, hl, bwd, final):
        idx = 2 * hl + bwd
        r0 = c * BLOCK if isinstance(c, int) else pl.multiple_of(c * BLOCK, BLOCK)
        t_sl = pl.ds(r0, BLOCK)
        qt = qt_ref[0, hl * DK_M:(hl + 1) * DK_M, t_sl]
        k = k_ref[0, t_sl, hl * DK_M:(hl + 1) * DK_M]
        feat = slice(hl * DV_M, (hl + 1) * DV_M)
        vt = vt_ref[0, feat, t_sl]
        gr = gr_ref[0, hl, :, t_sl]
        li_r, b_r = gr[2 * bwd:2 * bwd + 1, :], gr[2 * bwd + 1:2 * bwd + 2, :]
        a_c = ac_ref[0, hl, t_sl, bwd:bwd + 1]
        b_end = b_r[:, 0:1] if bwd else b_r[:, BLOCK - 1:BLOCK]
        m_prev = m_ref[idx][:, 0:1]
        ct = c_ref[idx]
        nst = n_ref[idx]
        dmat = jnp.where(mask_b if bwd else mask_f, a_c + b_r, -jnp.inf)
        inter = b_r + m_prev
        m_t = jnp.maximum(inter, jnp.max(dmat, axis=0, keepdims=True))
        w_inter = jnp.exp(inter - m_t)
        st = jnp.dot(k, qt, preferred_element_type=F32) * jnp.exp(dmat - m_t)
        num = (w_inter * jnp.dot(ct.astype(BF16), qt, preferred_element_type=F32)
               + jnp.dot(vt, st.astype(BF16), preferred_element_type=F32))
        nq = jnp.dot(nst.astype(BF16), qt, preferred_element_type=F32)
        den = w_inter * nq + jnp.sum(st, axis=0, keepdims=True)
        h = num * (1.0 / jnp.maximum(jnp.abs(den), jnp.exp(-m_t)))
        if final:
            og = jax.nn.sigmoid(ot_ref[0, feat, t_sl].astype(F32))
            y_ref[0, feat, pl.ds(r0 - BLOCK, BLOCK)] = (og * (hs_ref[feat, t_sl] + h)).astype(BF16)
        else:
            hs_ref[feat, t_sl] = h
        ldec = b_end - b_r + li_r
        m_new = jnp.maximum(b_end + m_prev, jnp.max(ldec, axis=1, keepdims=True))
        w_c = jnp.exp(b_end + m_prev - m_new)
        w_s = jnp.exp(ldec - m_new)
        wvt = (vt.astype(F32) * w_s).astype(BF16)
        c_ref[idx] = w_c * ct + jnp.dot(wvt, k, preferred_element_type=F32)
        n_ref[idx] = w_c * nst + jnp.dot(w_s.astype(BF16), k, preferred_element_type=F32)
        m_ref[idx] = jnp.broadcast_to(m_new, (1, BLOCK))

    def first_half(i, carry):
        for hl in range(MLSTM_HP):
            chain(i, hl, 0, False)
            chain(NBLK - 1 - i, hl, 1, False)
        return carry

    def second_half(i, carry):
        for hl in range(MLSTM_HP):
            chain(i, hl, 0, True)
            chain(NBLK - 1 - i, hl, 1, True)
        return carry

    lax.fori_loop(0, MLSTM_MID, first_half, 0)
    for hl in range(MLSTM_HP):
        chain(MLSTM_MID, hl, 0, False)
        chain(MLSTM_MID, hl, 1, True)
    lax.fori_loop(MLSTM_MID + 1, NBLK - 1, second_half, 0)
    for hl in range(MLSTM_HP):
        chain(NBLK - 1, hl, 0, True)


def mlstm(q_t, qk_m, v_t, o_t, acol, grow):
    hp = MLSTM_HP
    kw, vw = hp * DK_M, hp * DV_M
    return pl.pallas_call(
        _mlstm_kernel,
        grid=(BATCH, H_M // hp),
        in_specs=[
            pl.BlockSpec((1, kw, LP), lambda b, g: (b, g, 0)),
            pl.BlockSpec((1, LP, kw), lambda b, g: (b, 0, (H_M * DK_M) // kw + g)),
            pl.BlockSpec((1, vw, LP), lambda b, g: (b, g, 0)),
            pl.BlockSpec((1, vw, LP), lambda b, g: (b, g, 0)),
            pl.BlockSpec((1, hp, LP, 2), lambda b, g: (b, g, 0, 0)),
            pl.BlockSpec((1, hp, 4, LP), lambda b, g: (b, g, 0, 0)),
        ],
        out_specs=pl.BlockSpec((1, vw, SEQ), lambda b, g: (b, g, 0)),
        out_shape=jax.ShapeDtypeStruct((BATCH, H_M * DV_M, SEQ), BF16),
        scratch_shapes=[
            pltpu.VMEM((vw, LP), F32),
            pltpu.VMEM((2 * hp, DV_M, DK_M), F32),
            pltpu.VMEM((2 * hp, 1, DK_M), F32),
            pltpu.VMEM((2 * hp, 1, BLOCK), F32),
        ],
        compiler_params=_cparams(2),
        name="mlstm",
    )(q_t, qk_m, v_t, o_t, acol, grow)


LOG2E = 1.4426950408889634
ATT_QB = 2
ATT_TQ = ATT_QB * BLOCK
ATT_BAND = (ATT_QB + 2) * BLOCK
ATT_GROUPS = (6, 6, 6, 6, 5)
assert BLOCK >= MAX_DISTANCE and ATT_BAND + sum(ATT_GROUPS) * BLOCK == LP


def _attn_kernel(c_ref, qa_ref, qb_ref, k1_ref, v1_ref, tab_ref, lam_ref, sg_ref, o_ref,
                 s_ref, k_ref, v_ref):
    h = pl.program_id(1)
    qb = ATT_QB * pl.program_id(2) + 1

    @pl.when(pl.program_id(2) == 0)
    def _():
        for rep in range(2):
            k_ref[0, rep * LP:(rep + 1) * LP, :] = k1_ref[0]
            v_ref[0, rep * LP:(rep + 1) * LP, :] = v1_ref[0]

    q = jnp.concatenate([qa_ref[0], qb_ref[0]], axis=0)
    scale = DK_DA ** -0.5 * LOG2E
    c_neg = c_ref[h, 0]
    c_pos = c_ref[h, 1]
    lp = lam_ref[...]
    lam = (jnp.exp(jnp.sum(lp[0:1] * lp[1:2], axis=1, keepdims=True))
           - jnp.exp(jnp.sum(lp[2:3] * lp[3:4], axis=1, keepdims=True)) + LAMBDA_INIT)

    groups = [((qb - 1) * BLOCK, ATT_BAND, 0)]
    col = ATT_BAND
    for nblk in ATT_GROUPS:
        groups.append(((qb - 1) * BLOCK + col, nblk * BLOCK, col))
        col += nblk * BLOCK

    def lane_fold(acc, t, op):
        for j in range(t.shape[1] // BLOCK):
            piece = t[:, j * BLOCK:(j + 1) * BLOCK]
            acc = piece if acc is None else op(acc, piece)
        return acc

    mx = [None, None]
    for gi, (koff, width, col0) in enumerate(groups):
        koff = pl.multiple_of(koff, BLOCK)
        if gi == 0:
            bias = tab_ref[0, 0]
        else:
            kpos = koff + lax.broadcasted_iota(jnp.int32, (1, width), 1)
            bias = jnp.where(kpos < LP, c_pos, jnp.where(kpos < LP + PAD, NEG_INF, c_neg))
        for m in range(2):
            kk = k_ref[0, pl.ds(koff, width), m * DK_DA:(m + 1) * DK_DA]
            s = lax.dot_general(q[:, m * DK_DA:(m + 1) * DK_DA], kk, (((1,), (1,)), ((), ())),
                                preferred_element_type=F32) * scale + bias
            s_ref[m, :, col0:col0 + width] = s
            mx[m] = lane_fold(mx[m], s, jnp.maximum)
    row_max = [jnp.max(mx[m], axis=1, keepdims=True) for m in range(2)]

    lsum = [None, None]
    acc = [None, None]
    for koff, width, col0 in groups:
        koff = pl.multiple_of(koff, BLOCK)
        vv = v_ref[0, pl.ds(koff, width), :]
        for m in range(2):
            p = jnp.exp2(s_ref[m, :, col0:col0 + width] - row_max[m])
            lsum[m] = lane_fold(lsum[m], p, jnp.add)
            pv = jnp.dot(p.astype(BF16), vv, preferred_element_type=F32)
            acc[m] = pv if acc[m] is None else acc[m] + pv
    l1 = jnp.sum(lsum[0], axis=1, keepdims=True)
    l2 = jnp.sum(lsum[1], axis=1, keepdims=True)
    o = acc[0] / l1 - lam * (acc[1] / l2)
    o_ref[0] = (_rms(o, sg_ref[...]) * (1.0 - LAMBDA_INIT)).astype(BF16)


def diff_attention(consts, proj3, tab, lam_params, subln):
    nq = SEQ // ATT_TQ
    kblk0 = COL_DA_K // (2 * DK_DA)
    vblk0 = COL_DA_V // DV_DA

    def tab_map(b, h, i):
        case = jnp.where(i == 0, 0, jnp.where(i == nq - 1, 2, 1))
        return (h, case, 0, 0)

    return pl.pallas_call(
        _attn_kernel,
        grid=(BATCH, H_DA, nq),
        in_specs=[
            pl.BlockSpec(memory_space=pltpu.SMEM),
            pl.BlockSpec((1, BLOCK, 2 * DK_DA), lambda b, h, i: (b, ATT_QB * i + 1, h)),
            pl.BlockSpec((1, BLOCK, 2 * DK_DA), lambda b, h, i: (b, ATT_QB * i + 2, h)),
            pl.BlockSpec((1, LP, 2 * DK_DA), lambda b, h, i: (b, 0, kblk0 + h)),
            pl.BlockSpec((1, LP, DV_DA), lambda b, h, i: (b, 0, vblk0 + h)),
            pl.BlockSpec((1, 1, ATT_TQ, ATT_BAND), tab_map),
            pl.BlockSpec((4, DK_DA), lambda b, h, i: (0, 0)),
            pl.BlockSpec((1, DV_DA), lambda b, h, i: (0, 0)),
        ],
        out_specs=pl.BlockSpec((1, ATT_TQ, DV_DA), lambda b, h, i: (b, i, h)),
        out_shape=jax.ShapeDtypeStruct((BATCH, SEQ, H_DA * DV_DA), BF16),
        scratch_shapes=[pltpu.VMEM((2, ATT_TQ, LP), F32),
                        pltpu.VMEM((1, 2 * LP, 2 * DK_DA), BF16),
                        pltpu.VMEM((1, 2 * LP, DV_DA), BF16)],
        compiler_params=_cparams(3),
        name="diff_attn",
    )(consts, proj3, proj3, proj3, proj3, tab, lam_params, subln.reshape(1, DV_DA))


def _bias_tables(rel_bias):
    rb = rel_bias.astype(F32)
    i = jnp.arange(ATT_TQ, dtype=jnp.int32)[:, None]
    j = jnp.arange(ATT_BAND, dtype=jnp.int32)[None, :]
    rel = j - BLOCK - i
    nb = N_BUCKETS // 2
    max_exact = nb // 2
    n = jnp.abs(rel)
    nf = jnp.maximum(n, 1).astype(F32)
    large = max_exact + (jnp.log(nf / max_exact) / math.log(MAX_DISTANCE / max_exact)
                         * (nb - max_exact)).astype(jnp.int32)
    large = jnp.minimum(large, nb - 1)
    bucket = jnp.where(rel > 0, nb, 0) + jnp.where(n < max_exact, n, large)
    hit = bucket[None, :, :, None] == jnp.arange(N_BUCKETS, dtype=jnp.int32)
    gen = jnp.sum(jnp.where(hit, rb.T[:, None, None, :], 0.0), axis=-1)
    c_neg = rb[nb - 1]
    c_pos = rb[N_BUCKETS - 1]
    jj = j[None]
    first = jnp.where(jj < PAD, NEG_INF, gen)
    wrap0 = ATT_BAND - BLOCK
    wrapped = jnp.where(jj - wrap0 < PAD, NEG_INF, c_neg[:, None, None])
    last = jnp.where(jj >= wrap0, wrapped, gen)
    tab = jnp.stack([first, gen, last], axis=1)
    consts = jnp.stack([c_neg, c_pos], axis=1)
    return tab * LOG2E, consts * LOG2E


def _mix_kernel(ya_ref, ym_ref, wa_ref, wm_ref, ga_ref, gm_ref, o_ref, wa_bf, wm_bf):
    @pl.when(pl.program_id(1) == 0)
    def _():
        wa_bf[...] = wa_ref[...].astype(BF16)
        wm_bf[...] = wm_ref[...].astype(BF16)

    a = jnp.dot(ya_ref[...], wa_bf[...], preferred_element_type=F32)
    m = jnp.dot(ym_ref[...], wm_bf[...], preferred_element_type=F32)
    o_ref[...] = (ga_ref[...].astype(F32) * a + gm_ref[...].astype(F32) * m).astype(BF16)


def branch_mix(y_da, y_m, w_da, w_m, gate, *, tm=512, tn=1024):
    m, k = y_da.shape
    nj = D_MODEL // tn
    return pl.pallas_call(
        _mix_kernel,
        grid=(nj, m // tm),
        in_specs=[
            pl.BlockSpec((tm, k), lambda j, i: (i, 0)),
            pl.BlockSpec((tm, k), lambda j, i: (i, 0)),
            pl.BlockSpec((k, tn), lambda j, i: (0, j)),
            pl.BlockSpec((k, tn), lambda j, i: (0, j)),
            pl.BlockSpec((tm, tn), lambda j, i: (i, j)),
            pl.BlockSpec((tm, tn), lambda j, i: (i, nj + j)),
        ],
        out_specs=pl.BlockSpec((tm, tn), lambda j, i: (i, j)),
        out_shape=jax.ShapeDtypeStruct((m, D_MODEL), BF16),
        scratch_shapes=[pltpu.VMEM((k, tn), BF16), pltpu.VMEM((k, tn), BF16)],
        compiler_params=_cparams(2),
        name="branch_mix",
    )(y_da, y_m, w_da, w_m, gate, gate)


FFN_TM = 1024
HALF = D_MODEL // 2


def _pack_bf16_pairs(v):
    lo = lax.bitcast_convert_type(v[:, :HALF].astype(BF16).astype(F32), jnp.uint32)
    hi = lax.bitcast_convert_type(v[:, HALF:].astype(BF16).astype(F32), jnp.uint32)
    return (hi & jnp.uint32(0xFFFF0000)) | lax.shift_right_logical(lo, jnp.uint32(16))


def _unpack_bf16_pairs(w):
    lo = lax.bitcast_convert_type(lax.shift_left(w, jnp.uint32(16)), F32).astype(BF16)
    hi = lax.bitcast_convert_type(w & jnp.uint32(0xFFFF0000), F32).astype(BF16)
    return lo, hi


def _ffn_prep_kernel(h_ref, g_ref, wr_ref, br_ref, hn_ref, e_ref, w_ref, r_ref, cnt_ref, base_ref):
    @pl.when(pl.program_id(0) == 0)
    def _():
        base_ref[...] = jnp.zeros_like(base_ref)

    hn = _rms(h_ref[...], g_ref[...])
    hn_ref[...] = _pack_bf16_pairs(hn)
    logits = jnp.dot(hn.astype(BF16), wr_ref[...].astype(BF16),
                     preferred_element_type=F32) + br_ref[...]
    lane = lax.broadcasted_iota(jnp.int32, (FFN_TM, N_EXPERTS), 1)
    lane_o = lax.broadcasted_iota(jnp.int32, (FFN_TM, 128), 1)
    ti = lax.broadcasted_iota(jnp.int32, (FFN_TM, FFN_TM), 0)
    ui = lax.broadcasted_iota(jnp.int32, (FFN_TM, FFN_TM), 1)
    tril = jnp.where(ui <= ti, 1.0, 0.0).astype(BF16)
    e_out = jnp.zeros((FFN_TM, 128), jnp.int32)
    r_out = jnp.zeros((FFN_TM, 128), jnp.int32)
    l_out = jnp.full((FFN_TM, 128), -jnp.inf, F32)
    base = base_ref[...]
    l = logits
    for kk in range(TOP_K):
        mk = jnp.max(l, axis=1, keepdims=True)
        ik = jnp.min(jnp.where(l == mk, lane, N_EXPERTS), axis=1, keepdims=True)
        hit = lane == ik
        oh = jnp.where(hit, 1.0, 0.0)
        cum = jnp.dot(tril, oh.astype(BF16), preferred_element_type=F32)
        rank = jnp.sum(oh * (cum + base), axis=1, keepdims=True) - 1.0
        base = base + jnp.sum(oh, axis=0, keepdims=True)
        e_out = jnp.where(lane_o == kk, ik, e_out)
        r_out = jnp.where(lane_o == kk, rank.astype(jnp.int32), r_out)
        l_out = jnp.where(lane_o == kk, mk, l_out)
        l = jnp.where(hit, -jnp.inf, l)
    base_ref[...] = base
    cnt_ref[...] = base
    ex = jnp.exp(l_out - jnp.max(l_out, axis=1, keepdims=True))
    e_ref[...] = e_out
    r_ref[...] = r_out
    w_ref[...] = ex / jnp.sum(ex, axis=1, keepdims=True)


def ffn_prep(h2, gain, w_router, b_router):
    row = lambda i: (i, 0)
    fixed = lambda i: (0, 0)
    return pl.pallas_call(
        _ffn_prep_kernel,
        grid=(N_TOK // FFN_TM,),
        in_specs=[
            pl.BlockSpec((FFN_TM, D_MODEL), row),
            pl.BlockSpec((1, D_MODEL), fixed),
            pl.BlockSpec((D_MODEL, N_EXPERTS), fixed),
            pl.BlockSpec((1, N_EXPERTS), fixed),
        ],
        out_specs=[
            pl.BlockSpec((FFN_TM, HALF), row),
            pl.BlockSpec((FFN_TM, 128), row),
            pl.BlockSpec((FFN_TM, 128), row),
            pl.BlockSpec((FFN_TM, 128), row),
            pl.BlockSpec((1, N_EXPERTS), fixed),
        ],
        out_shape=[
            jax.ShapeDtypeStruct((N_TOK, HALF), jnp.uint32),
            jax.ShapeDtypeStruct((N_TOK, 128), jnp.int32),
            jax.ShapeDtypeStruct((N_TOK, 128), F32),
            jax.ShapeDtypeStruct((N_TOK, 128), jnp.int32),
            jax.ShapeDtypeStruct((1, N_EXPERTS), F32),
        ],
        scratch_shapes=[pltpu.VMEM((1, N_EXPERTS), F32)],
        compiler_params=_cparams(1),
        name="ffn_prep",
    )(h2, gain.reshape(1, D_MODEL), w_router, b_router.reshape(1, N_EXPERTS))


SEG_ALIGN = 128
MOE_R = N_ASSIGN + N_EXPERTS * SEG_ALIGN


def _plan(counts_f, top_e, rank):
    counts = counts_f[0].astype(jnp.int32)
    seg_rows = (counts + SEG_ALIGN - 1) // SEG_ALIGN * SEG_ALIGN
    seg_start = jnp.cumsum(seg_rows) - seg_rows
    eq = top_e[:, :, None] == jnp.arange(N_EXPERTS, dtype=jnp.int32)[None, None, :]
    dest = jnp.sum(jnp.where(eq, seg_start[None, None, :], 0), axis=-1) + rank
    used = jnp.sum(seg_rows)
    slack = jnp.stack([used, (MOE_R - used) // SEG_ALIGN])
    return (dest.reshape(N_ASSIGN).astype(jnp.int32), seg_start.astype(jnp.int32),
            seg_rows.astype(jnp.int32), (seg_start + counts).astype(jnp.int32),
            (seg_rows - counts).astype(jnp.int32), slack.astype(jnp.int32))


def _zero_slack(slack_ref, zero_block, dst_rows, sem):
    zero_block[...] = jnp.zeros_like(zero_block)

    def copy(j):
        r0 = pl.multiple_of(slack_ref[0] + j * SEG_ALIGN, SEG_ALIGN)
        return pltpu.make_async_copy(zero_block, dst_rows(pl.ds(r0, SEG_ALIGN)), sem)

    def start(j, carry):
        copy(j).start()
        return carry

    def wait(j, carry):
        copy(j).wait()
        return carry

    lax.fori_loop(0, slack_ref[1], start, 0)
    lax.fori_loop(0, slack_ref[1], wait, 0)


DISP_TOK = 256


def _dispatch_kernel(dest_ref, pad0_ref, padn_ref, slack_ref, hn_ref, xs_ref,
                     zrow_ref, zblk_ref, sem, zsem):
    i = pl.program_id(0)

    @pl.when(i == 0)
    def _():
        zrow_ref[...] = jnp.zeros_like(zrow_ref)
        _zero_slack(slack_ref, zblk_ref, lambda rows: xs_ref.at[rows], zsem.at[0])

        def expert(e, carry):
            p0 = pad0_ref[e]
            pn = padn_ref[e]

            def zstart(r, c2):
                pltpu.make_async_copy(zrow_ref, xs_ref.at[pl.ds(p0 + r, 1)], zsem.at[0]).start()
                return c2

            def zwait(r, c2):
                pltpu.make_async_copy(zrow_ref, xs_ref.at[pl.ds(p0, 1)], zsem.at[0]).wait()
                return c2

            lax.fori_loop(0, pn, zstart, 0)
            lax.fori_loop(0, pn, zwait, 0)
            return carry

        lax.fori_loop(0, N_EXPERTS, expert, 0)

    def tok(t, carry):
        a = (i * DISP_TOK + t) * TOP_K
        for kk in range(TOP_K):
            pltpu.make_async_copy(hn_ref.at[pl.ds(t, 1)], xs_ref.at[pl.ds(dest_ref[a + kk], 1)],
                                  sem.at[0]).start(priority=kk % 2)
        return carry

    lax.fori_loop(0, DISP_TOK, tok, 0, unroll=4)
    for kk in range(TOP_K):
        pltpu.make_async_copy(hn_ref, xs_ref.at[pl.ds(0, DISP_TOK)], sem.at[0]).wait()


def dispatch(dest, pad0, padn, slack, hn_packed):
    grid_spec = pltpu.PrefetchScalarGridSpec(
        num_scalar_prefetch=4,
        grid=(N_TOK // DISP_TOK,),
        in_specs=[pl.BlockSpec((DISP_TOK, HALF), lambda i, d, p0, pn, z: (i, 0))],
        out_specs=pl.BlockSpec(memory_space=pl.ANY),
        scratch_shapes=[pltpu.VMEM((1, HALF), jnp.uint32),
                        pltpu.VMEM((SEG_ALIGN, HALF), jnp.uint32),
                        pltpu.SemaphoreType.DMA((1,)), pltpu.SemaphoreType.DMA((1,))],
    )
    return pl.pallas_call(
        _dispatch_kernel,
        grid_spec=grid_spec,
        out_shape=jax.ShapeDtypeStruct((MOE_R, HALF), jnp.uint32),
        compiler_params=_cparams(1),
        name="moe_dispatch",
    )(dest, pad0, padn, slack, hn_packed)


MOE_CH = 2 * SEG_ALIGN
MOE_TF = 1024
MOE_NF = D_FF // MOE_TF


class _CopyGroup:
    def __init__(self, copies):
        self.copies = copies

    def start(self, priority=0):
        for cp in self.copies:
            cp.start(priority=priority)

    def wait(self):
        for cp in self.copies:
            cp.wait()


W_PIECES = 8
W_PER_CHUNK = 2


def _stream_rows(step, n_steps, start, rows, next_start, next_rows, state,
                 make_in, make_out, make_tail_in, make_tail_out,
                 compute_chunk, compute_tail, before_first_wait, next_weight_piece):
    has_next = step + 1 < n_steps

    def request_weights(first, count):
        def one(p, carry):
            @pl.when(jnp.logical_and(has_next, p < W_PIECES))
            def _():
                next_weight_piece(p).start()
            return carry

        lax.fori_loop(first, first + count, one, 0)

    @pl.when(step == 0)
    def _():
        for j in range(4):
            state[j] = 0

    n_ch = lax.shift_right_logical(rows, MOE_CH.bit_length() - 1)
    tail = rows - n_ch * MOE_CH
    tail_row = start + n_ch * MOE_CH
    g0 = state[0]
    feeds_next = jnp.logical_and(step + 1 < n_steps, next_rows >= MOE_CH)

    def chunk_row(c):
        return start + c * MOE_CH

    @pl.when(tail > 0)
    def _():
        make_tail_in(tail_row).start(priority=1)

    @pl.when(jnp.logical_and(n_ch > 0, state[3] == 0))
    def _():
        make_in(start, lax.rem(g0, 2)).start(priority=1)

    before_first_wait()

    @pl.when(tail > 0)
    def _():
        make_tail_in(tail_row).wait()
        compute_tail()
        make_tail_out(tail_row).start(priority=1)

    def body(c, carry):
        slot = lax.rem(g0 + c, 2)
        make_in(chunk_row(c), slot).wait()

        @pl.when(c + 1 < n_ch)
        def _():
            make_in(chunk_row(c + 1), 1 - slot).start(priority=1)

        @pl.when(jnp.logical_and(c + 1 == n_ch, feeds_next))
        def _():
            make_in(next_start, 1 - slot).start(priority=1)

        request_weights(c * W_PER_CHUNK, W_PER_CHUNK)

        @pl.when(state[1 + slot] == 1)
        def _():
            make_out(chunk_row(c), slot).wait()

        compute_chunk(slot)
        make_out(chunk_row(c), slot).start(priority=1)
        state[1 + slot] = 1
        return carry

    lax.fori_loop(0, n_ch, body, 0)
    request_weights(n_ch * W_PER_CHUNK, W_PIECES)
    state[0] = g0 + n_ch
    state[3] = jnp.where(jnp.logical_and(n_ch > 0, feeds_next), 1, 0)

    @pl.when(tail > 0)
    def _():
        make_tail_out(tail_row).wait()

    @pl.when(step == n_steps - 1)
    def _():
        for slot in range(2):
            @pl.when(state[1 + slot] == 1)
            def _():
                make_out(0, slot).wait()
                state[1 + slot] = 0


def _rows_at(row0, n):
    return pl.ds(row0 if isinstance(row0, int) else pl.multiple_of(row0, SEG_ALIGN), n)


def _moe_up_kernel(seg_ref, rows_ref, slack_ref, xs_ref, w_hbm, bg_ref, bl_ref, act_ref,
                   xbuf, obuf, xtail, otail, wbuf, wg_bf, wl_bf, state, isem, osem, tsem, wsem):
    f = pl.program_id(0)
    e = pl.program_id(1)
    start = seg_ref[e]
    rows = rows_ref[e]
    e_next = lax.rem(e + 1, N_EXPERTS)
    f_next = jnp.where(e == N_EXPERTS - 1, f + 1, f)
    step = f * N_EXPERTS + e
    n_steps = MOE_NF * N_EXPERTS
    wslot = lax.rem(step, 2)

    def weight_piece(expert, ftile, slot, p):
        per_half = W_PIECES // 2
        band = D_MODEL // per_half
        t = p // per_half if isinstance(p, int) else lax.shift_right_logical(
            p, per_half.bit_length() - 1)
        r0 = (p - t * per_half) * band
        r0 = r0 if isinstance(r0, int) else pl.multiple_of(r0, band)
        col0 = pl.multiple_of((t * MOE_NF + ftile) * MOE_TF, MOE_TF)
        return pltpu.make_async_copy(
            w_hbm.at[expert, pl.ds(r0, band), pl.ds(col0, MOE_TF)],
            wbuf.at[slot, t, pl.ds(r0, band)], wsem.at[slot])

    @pl.when(step == 0)
    def _():
        for p in range(W_PIECES):
            weight_piece(e, f, wslot, p).start()

    def make_in(r0, slot):
        return pltpu.make_async_copy(xs_ref.at[_rows_at(r0, MOE_CH)], xbuf.at[slot], isem.at[slot])

    def make_out(r0, slot):
        return pltpu.make_async_copy(obuf.at[slot], act_ref.at[f, _rows_at(r0, MOE_CH)],
                                     osem.at[slot])

    def make_tail_in(r0):
        return pltpu.make_async_copy(xs_ref.at[_rows_at(r0, SEG_ALIGN)], xtail, tsem.at[0])

    def make_tail_out(r0):
        return pltpu.make_async_copy(otail, act_ref.at[f, _rows_at(r0, SEG_ALIGN)], tsem.at[1])

    def cast_weights():
        for p in range(W_PIECES):
            weight_piece(e, f, wslot, p).wait()

        @pl.when(rows > 0)
        def _():
            wg_bf[...] = wbuf[wslot, 0].astype(BF16)
            wl_bf[...] = wbuf[wslot, 1].astype(BF16)

    def expert_mlp(words):
        lo, hi = _unpack_bf16_pairs(words)
        glu = (jnp.dot(lo, wg_bf[:HALF, :], preferred_element_type=F32)
               + jnp.dot(hi, wg_bf[HALF:, :], preferred_element_type=F32) + bg_ref[0])
        lin = (jnp.dot(lo, wl_bf[:HALF, :], preferred_element_type=F32)
               + jnp.dot(hi, wl_bf[HALF:, :], preferred_element_type=F32) + bl_ref[0])
        glu = jnp.minimum(glu, SWIGLU_LIMIT)
        lin = jnp.clip(lin, -SWIGLU_LIMIT, SWIGLU_LIMIT)
        return (glu * jax.nn.sigmoid(SWIGLU_ALPHA * glu) * (lin + 1.0)).astype(BF16)

    def compute_chunk(slot):
        obuf[slot] = expert_mlp(xbuf[slot])

    def compute_tail():
        otail[...] = expert_mlp(xtail[...])

    _stream_rows(step, n_steps, start, rows,
                 seg_ref[e_next], rows_ref[e_next], state,
                 make_in, make_out, make_tail_in, make_tail_out,
                 compute_chunk, compute_tail, cast_weights,
                 lambda p: weight_piece(e_next, f_next, 1 - wslot, p))

    @pl.when(e == N_EXPERTS - 1)
    def _():
        _zero_slack(slack_ref, otail, lambda rr: act_ref.at[f, rr], tsem.at[1])


def moe_up(seg_start, seg_rows, slack, xs, w1, b1):
    grid_spec = pltpu.PrefetchScalarGridSpec(
        num_scalar_prefetch=3,
        grid=(MOE_NF, N_EXPERTS),
        in_specs=[
            pl.BlockSpec(memory_space=pl.ANY),
            pl.BlockSpec(memory_space=pl.ANY),
            pl.BlockSpec((1, 1, MOE_TF), lambda f, e, s, r, z: (e, 0, f)),
            pl.BlockSpec((1, 1, MOE_TF), lambda f, e, s, r, z: (e, 0, MOE_NF + f)),
        ],
        out_specs=pl.BlockSpec(memory_space=pl.ANY),
        scratch_shapes=[
            pltpu.VMEM((2, MOE_CH, HALF), jnp.uint32),
            pltpu.VMEM((2, MOE_CH, MOE_TF), BF16),
            pltpu.VMEM((SEG_ALIGN, HALF), jnp.uint32),
            pltpu.VMEM((SEG_ALIGN, MOE_TF), BF16),
            pltpu.VMEM((2, 2, D_MODEL, MOE_TF), F32),
            pltpu.VMEM((D_MODEL, MOE_TF), BF16),
            pltpu.VMEM((D_MODEL, MOE_TF), BF16),
            pltpu.SMEM((4,), jnp.int32),
            pltpu.SemaphoreType.DMA((2,)),
            pltpu.SemaphoreType.DMA((2,)),
            pltpu.SemaphoreType.DMA((2,)),
            pltpu.SemaphoreType.DMA((2,)),
        ],
    )
    b13 = b1.reshape(N_EXPERTS, 1, 2 * D_FF)
    return pl.pallas_call(
        _moe_up_kernel,
        grid_spec=grid_spec,
        out_shape=jax.ShapeDtypeStruct((MOE_NF, MOE_R, MOE_TF), BF16),
        compiler_params=pltpu.CompilerParams(
            dimension_semantics=("arbitrary", "arbitrary"), vmem_limit_bytes=MOE_VMEM_LIMIT),
        name="moe_up",
    )(seg_start, seg_rows, slack, xs, w1, b13, b13)


def _moe_down_kernel(seg_ref, rows_ref, slack_ref, act_ref, w_hbm, b_ref, y_ref,
                     xbuf, obuf, xtail, otail, wbuf, w_bf, state, isem, osem, tsem, wsem):
    e = pl.program_id(0)
    start = seg_ref[e]
    rows = rows_ref[e]
    e_next = lax.rem(e + 1, N_EXPERTS)
    wslot = lax.rem(e, 2)

    def weight_piece(expert, slot, p):
        band = D_FF // W_PIECES
        r0 = p * band if isinstance(p, int) else pl.multiple_of(p * band, band)
        return pltpu.make_async_copy(w_hbm.at[expert, pl.ds(r0, band)],
                                     wbuf.at[slot, pl.ds(r0, band)], wsem.at[slot])

    @pl.when(e == 0)
    def _():
        for p in range(W_PIECES):
            weight_piece(e, wslot, p).start()

    def make_in(r0, slot):
        return _CopyGroup([
            pltpu.make_async_copy(act_ref.at[j, _rows_at(r0, MOE_CH)],
                                  xbuf.at[slot, :, pl.ds(j * MOE_TF, MOE_TF)], isem.at[slot])
            for j in range(MOE_NF)])

    def make_out(r0, slot):
        return pltpu.make_async_copy(obuf.at[slot], y_ref.at[_rows_at(r0, MOE_CH)], osem.at[slot])

    def make_tail_in(r0):
        return _CopyGroup([
            pltpu.make_async_copy(act_ref.at[j, _rows_at(r0, SEG_ALIGN)],
                                  xtail.at[:, pl.ds(j * MOE_TF, MOE_TF)], tsem.at[0])
            for j in range(MOE_NF)])

    def make_tail_out(r0):
        return pltpu.make_async_copy(otail, y_ref.at[_rows_at(r0, SEG_ALIGN)], tsem.at[1])

    def cast_weights():
        for p in range(W_PIECES):
            weight_piece(e, wslot, p).wait()

        @pl.when(rows > 0)
        def _():
            w_bf[...] = wbuf[wslot].astype(BF16)

    def expert_out(a):
        return _pack_bf16_pairs(jnp.dot(a, w_bf[...], preferred_element_type=F32) + b_ref[0])

    def compute_chunk(slot):
        obuf[slot] = expert_out(xbuf[slot])

    def compute_tail():
        otail[...] = expert_out(xtail[...])

    _stream_rows(e, N_EXPERTS, start, rows, seg_ref[e_next], rows_ref[e_next], state,
                 make_in, make_out, make_tail_in, make_tail_out,
                 compute_chunk, compute_tail, cast_weights,
                 lambda p: weight_piece(e_next, 1 - wslot, p))

    @pl.when(e == N_EXPERTS - 1)
    def _():
        _zero_slack(slack_ref, otail, lambda rr: y_ref.at[rr], tsem.at[1])


def moe_down(seg_start, seg_rows, slack, act, w2, b2):
    grid_spec = pltpu.PrefetchScalarGridSpec(
        num_scalar_prefetch=3,
        grid=(N_EXPERTS,),
        in_specs=[
            pl.BlockSpec(memory_space=pl.ANY),
            pl.BlockSpec(memory_space=pl.ANY),
            pl.BlockSpec((1, 1, D_MODEL), lambda e, s, r, z: (e, 0, 0)),
        ],
        out_specs=pl.BlockSpec(memory_space=pl.ANY),
        scratch_shapes=[
            pltpu.VMEM((2, MOE_CH, D_FF), BF16),
            pltpu.VMEM((2, MOE_CH, HALF), jnp.uint32),
            pltpu.VMEM((SEG_ALIGN, D_FF), BF16),
            pltpu.VMEM((SEG_ALIGN, HALF), jnp.uint32),
            pltpu.VMEM((2, D_FF, D_MODEL), F32),
            pltpu.VMEM((D_FF, D_MODEL), BF16),
            pltpu.SMEM((4,), jnp.int32),
            pltpu.SemaphoreType.DMA((2,)),
            pltpu.SemaphoreType.DMA((2,)),
            pltpu.SemaphoreType.DMA((2,)),
            pltpu.SemaphoreType.DMA((2,)),
        ],
    )
    return pl.pallas_call(
        _moe_down_kernel,
        grid_spec=grid_spec,
        out_shape=jax.ShapeDtypeStruct((MOE_R, HALF), jnp.uint32),
        compiler_params=pltpu.CompilerParams(
            dimension_semantics=("arbitrary",), vmem_limit_bytes=MOE_VMEM_LIMIT),
        name="moe_down",
    )(seg_start, seg_rows, slack, act, w2, b2.reshape(N_EXPERTS, 1, D_MODEL))


COMB_TM = 128
COMB_NT = N_TOK // COMB_TM


def _combine_kernel(dest_ref, y_ref, w_ref, h_ref, g_ref, o_ref, buf, sem):
    i = pl.program_id(0)

    def fetch(tile, slot):
        def tok(t, carry):
            a = (tile * COMB_TM + t) * TOP_K
            for kk in range(TOP_K):
                pltpu.make_async_copy(y_ref.at[pl.ds(dest_ref[a + kk], 1)],
                                      buf.at[slot, kk, pl.ds(t, 1)], sem.at[slot]).start(
                                          priority=kk % 2)
            return carry

        lax.fori_loop(0, COMB_TM, tok, 0, unroll=4)

    @pl.when(i == 0)
    def _():
        fetch(0, 0)

    slot = lax.rem(i, 2)

    @pl.when(i + 1 < COMB_NT)
    def _():
        fetch(i + 1, 1 - slot)

    for kk in range(TOP_K):
        pltpu.make_async_copy(y_ref.at[pl.ds(0, COMB_TM)], buf.at[slot, kk], sem.at[slot]).wait()
    w = w_ref[...]
    lo = h_ref[:, :HALF]
    hi = h_ref[:, HALF:]
    for kk in range(TOP_K):
        words = buf[slot, kk]
        wk = w[:, kk:kk + 1]
        lo = lo + wk * lax.bitcast_convert_type(lax.shift_left(words, jnp.uint32(16)), F32)
        hi = hi + wk * lax.bitcast_convert_type(words & jnp.uint32(0xFFFF0000), F32)
    ms = (jnp.sum(lo * lo, axis=-1, keepdims=True)
          + jnp.sum(hi * hi, axis=-1, keepdims=True)) * (1.0 / D_MODEL)
    inv = lax.rsqrt(ms + EPS)
    o_ref[:, :HALF] = lo * inv * g_ref[:, :HALF]
    o_ref[:, HALF:] = hi * inv * g_ref[:, HALF:]


def combine(dest, y, weight, h2, gain):
    grid_spec = pltpu.PrefetchScalarGridSpec(
        num_scalar_prefetch=1,
        grid=(COMB_NT,),
        in_specs=[
            pl.BlockSpec(memory_space=pl.ANY),
            pl.BlockSpec((COMB_TM, 128), lambda i, d: (i, 0)),
            pl.BlockSpec((COMB_TM, D_MODEL), lambda i, d: (i, 0)),
            pl.BlockSpec((1, D_MODEL), lambda i, d: (0, 0)),
        ],
        out_specs=pl.BlockSpec((COMB_TM, D_MODEL), lambda i, d: (i, 0)),
        scratch_shapes=[pltpu.VMEM((2, TOP_K, COMB_TM, HALF), jnp.uint32),
                        pltpu.SemaphoreType.DMA((2,))],
    )
    return pl.pallas_call(
        _combine_kernel,
        grid_spec=grid_spec,
        out_shape=jax.ShapeDtypeStruct((N_TOK, D_MODEL), F32),
        compiler_params=_cparams(1),
        name="moe_combine",
    )(dest, y, weight, h2, gain.reshape(1, D_MODEL))


def kernel(x, meta_tokens, rel_bias, norm_mix, w_in, conv_w, gate_bias_m, lambda_params, subln_da,
           w_branch_da, w_branch_m, w_gate, b_gate, w_out, norm_ffn, w_router, b_router,
           w1, b1, w2, b2, norm_final):
    layer = 0
    xn_pad, xn_real = norm_in(x, meta_tokens, norm_mix[layer])
    xn_pad2 = xn_pad.reshape(BATCH * LP, D_MODEL)
    proj = matmul(xn_pad2, w_in[layer], n_cols=PROJ_COLS, tm=768, tn=1024, name="proj_in")
    proj3 = proj.reshape(BATCH, LP, PROJ_COLS)
    w_g = jnp.pad(w_in[layer][:, COL_M_G:], ((0, 0), (0, 128 - 4 * H_M)))
    mg = matmul(xn_pad2, w_g, n_cols=128, tm=768, tn=128, out_dtype=F32, name="proj_gates")
    gate = matmul(xn_real.reshape(N_TOK, D_MODEL), w_gate[layer], n_cols=2 * D_MODEL,
                  tm=1024, tn=1024, bias=b_gate[layer], act="sigmoid", name="mix_gate")

    tab, consts = _bias_tables(rel_bias)
    y_da = diff_attention(consts, proj3, tab, lambda_params[layer], subln_da[layer])

    qk_m = conv_qk(proj3, conv_w[layer])
    gp = gate_prep(mg.reshape(BATCH, LP, 128), gate_bias_m[layer])
    gp4 = gp.reshape(BATCH, LP, 4, H_M)
    grow = jnp.transpose(gp4, (0, 3, 2, 1))
    acol = jnp.transpose(gp4[:, :, 0::2] - gp4[:, :, 1::2], (0, 3, 1, 2))
    swap = lambda t: jnp.swapaxes(t, 1, 2)
    y_m = swap(mlstm(swap(qk_m[:, :, :H_M * DK_M]), qk_m,
                     swap(proj3[:, :, COL_M_V:COL_M_V + H_M * DV_M]),
                     swap(proj3[:, :, COL_M_O:COL_M_O + H_M * DV_M]), acol, grow))

    mixed = branch_mix(y_da.reshape(N_TOK, H_DA * DV_DA), y_m.reshape(N_TOK, H_M * DV_M),
                       w_branch_da[layer], w_branch_m[layer], gate)
    h2 = matmul(mixed, w_out[layer], n_cols=D_MODEL, tm=1024, tn=1024,
                res=x.reshape(N_TOK, D_MODEL), out_dtype=F32, name="out_proj")

    hn_packed, top_e, weight, rank, counts = ffn_prep(
        h2, norm_ffn[layer], w_router[layer], b_router[layer])
    dest, seg_start, seg_rows, pad0, padn, slack = _plan(
        counts, top_e[:, :TOP_K], rank[:, :TOP_K])
    xs = dispatch(dest, pad0, padn, slack, hn_packed)
    act = moe_up(seg_start, seg_rows, slack, xs, w1[layer], b1[layer])
    y = moe_down(seg_start, seg_rows, slack, act, w2[layer], b2[layer])
    out = combine(dest, y, weight, h2, norm_final)
    return out.reshape(BATCH, SEQ, D_MODEL)
```

```python
import functools
import math

import jax
import jax.numpy as jnp
from jax import lax
from jax.experimental import pallas as pl
from jax.experimental.pallas import tpu as pltpu

F32 = jnp.float32
BF16 = jnp.bfloat16

D_MODEL = 2048
BATCH = 2
SEQ = 4096
N_META = 16
BLOCK = 128
PAD = (-N_META) % BLOCK
LP = PAD + N_META + SEQ
NBLK = LP // BLOCK
EPS = 1e-6
NEG_INF = -1e30

H_DA = 4
DK_DA = 128
DV_DA = 256
H_M = 4
DK_M = 128
DV_M = 256
CONV_W = 5
N_BUCKETS = 32
MAX_DISTANCE = 128
N_EXPERTS = 32
TOP_K = 4
D_FF = 2048
SWIGLU_ALPHA = 1.702
SWIGLU_LIMIT = 7.0
LAMBDA_INIT = 0.8 - 0.6 * math.exp(-0.3 * 0)

COL_DA_Q = 0
COL_DA_K = 1024
COL_DA_V = 2048
COL_M_Q = 3072
COL_M_K = 3584
COL_M_V = 4096
COL_M_O = 5120
COL_M_G = 6144
PROJ_COLS = 6144

N_TOK = BATCH * SEQ
N_ASSIGN = N_TOK * TOP_K

VMEM_LIMIT = 52 * 1024 * 1024
MOE_VMEM_LIMIT = 58 * 1024 * 1024


def _cparams(n_axes):
    return pltpu.CompilerParams(
        dimension_semantics=("arbitrary",) * n_axes, vmem_limit_bytes=VMEM_LIMIT)


def _rms(v, gain):
    ms = jnp.mean(v * v, axis=-1, keepdims=True)
    return v * lax.rsqrt(ms + EPS) * gain


def _norm_in_kernel(x_ref, meta_ref, g_ref, pad_ref, real_ref):
    j = pl.program_id(1)
    g = g_ref[...]

    @pl.when(j == 0)
    def _():
        pad_ref[0, :PAD, :] = jnp.zeros((PAD, D_MODEL), BF16)
        pad_ref[0, PAD:, :] = _rms(meta_ref[...], g).astype(BF16)

    @pl.when(j > 0)
    def _():
        y = _rms(x_ref[0], g).astype(BF16)
        pad_ref[0] = y
        real_ref[0] = y


def norm_in(x, meta, gain):
    return pl.pallas_call(
        _norm_in_kernel,
        grid=(BATCH, NBLK),
        in_specs=[
            pl.BlockSpec((1, BLOCK, D_MODEL), lambda b, j: (b, jnp.maximum(j - 1, 0), 0)),
            pl.BlockSpec((N_META, D_MODEL), lambda b, j: (0, 0)),
            pl.BlockSpec((1, D_MODEL), lambda b, j: (0, 0)),
        ],
        out_specs=[
            pl.BlockSpec((1, BLOCK, D_MODEL), lambda b, j: (b, j, 0)),
            pl.BlockSpec((1, BLOCK, D_MODEL), lambda b, j: (b, jnp.maximum(j - 1, 0), 0)),
        ],
        out_shape=[
            jax.ShapeDtypeStruct((BATCH, LP, D_MODEL), BF16),
            jax.ShapeDtypeStruct((BATCH, SEQ, D_MODEL), BF16),
        ],
        compiler_params=_cparams(2),
        name="norm_in",
    )(x, meta, gain.reshape(1, D_MODEL))


def _mm_kernel(*refs, has_bias, has_res, act):
    x_ref, w_ref = refs[0], refs[1]
    pos = 2
    b_ref = r_ref = None
    if has_bias:
        b_ref = refs[pos]
        pos += 1
    if has_res:
        r_ref = refs[pos]
        pos += 1
    o_ref, wbf_ref = refs[pos], refs[pos + 1]

    @pl.when(pl.program_id(1) == 0)
    def _():
        wbf_ref[...] = w_ref[...].astype(BF16)

    acc = jnp.dot(x_ref[...], wbf_ref[...], preferred_element_type=F32)
    if has_bias:
        acc = acc + b_ref[...]
    if act == "sigmoid":
        acc = jax.nn.sigmoid(acc)
    if has_res:
        acc = acc + r_ref[...]
    o_ref[...] = acc.astype(o_ref.dtype)


def matmul(x, w, *, n_cols, col_block0=0, tm, tn, bias=None, res=None, act=None,
           out_dtype=BF16, name):
    m, k = x.shape
    in_specs = [
        pl.BlockSpec((tm, k), lambda j, i: (i, 0)),
        pl.BlockSpec((k, tn), lambda j, i: (0, j + col_block0)),
    ]
    args = [x, w]
    if bias is not None:
        in_specs.append(pl.BlockSpec((1, tn), lambda j, i: (0, j)))
        args.append(bias.reshape(1, n_cols))
    if res is not None:
        in_specs.append(pl.BlockSpec((tm, tn), lambda j, i: (i, j)))
        args.append(res)
    return pl.pallas_call(
        functools.partial(_mm_kernel, has_bias=bias is not None, has_res=res is not None, act=act),
        grid=(n_cols // tn, m // tm),
        in_specs=in_specs,
        out_specs=pl.BlockSpec((tm, tn), lambda j, i: (i, j)),
        out_shape=jax.ShapeDtypeStruct((m, n_cols), out_dtype),
        scratch_shapes=[pltpu.VMEM((k, tn), BF16)],
        compiler_params=_cparams(2),
        name=name,
    )(*args)


def _conv_kernel(p_ref, w_ref, o_ref):
    c = pl.program_id(1)
    x = p_ref[0].astype(F32)
    w = w_ref[...]
    half = CONV_W // 2
    acc = w[half:half + 1, :] * x
    for j in range(CONV_W):
        if j != half:
            acc = acc + w[j:j + 1, :] * pltpu.roll(x, (half - j) % LP, axis=0)
    y = acc * jax.nn.sigmoid(acc)
    rows = lax.broadcasted_iota(jnp.int32, (LP, 1), 0)
    y = jnp.where(rows >= PAD, y, 0.0)
    scale = jnp.where(c < 2, DK_M ** -0.5, 1.0).astype(F32)
    o_ref[0] = (y * scale).astype(BF16)


def conv_qk(proj3, conv_w):
    cw = 256
    return pl.pallas_call(
        _conv_kernel,
        grid=(BATCH, (2 * H_M * DK_M) // cw),
        in_specs=[
            pl.BlockSpec((1, LP, cw), lambda b, c: (b, 0, COL_M_Q // cw + c)),
            pl.BlockSpec((CONV_W, cw), lambda b, c: (0, c)),
        ],
        out_specs=pl.BlockSpec((1, LP, cw), lambda b, c: (b, 0, c)),
        out_shape=jax.ShapeDtypeStruct((BATCH, LP, 2 * H_M * DK_M), BF16),
        compiler_params=_cparams(2),
        name="conv_qk",
    )(proj3, conv_w)


def _split_dot(tri, v):
    hi = v.astype(BF16)
    r1 = v - hi.astype(F32)
    mid = r1.astype(BF16)
    lo = (r1 - mid.astype(F32)).astype(BF16)
    return (jnp.dot(tri, hi, preferred_element_type=F32)
            + jnp.dot(tri, mid, preferred_element_type=F32)
            + jnp.dot(tri, lo, preferred_element_type=F32))


def _gate_kernel(g_ref, bias_ref, o_ref):
    ti = lax.broadcasted_iota(jnp.int32, (BLOCK, BLOCK), 0)
    ui = lax.broadcasted_iota(jnp.int32, (BLOCK, BLOCK), 1)
    tril = jnp.where(ui <= ti, 1.0, 0.0).astype(BF16)
    triu = jnp.where(ui >= ti, 1.0, 0.0).astype(BF16)
    ch = lax.broadcasted_iota(jnp.int32, (BLOCK, 4 * H_M), 1)
    typ = lax.shift_right_logical(ch, 2)
    rloc = lax.broadcasted_iota(jnp.int32, (BLOCK, 4 * H_M), 0)

    def body(c, carry):
        r0 = pl.multiple_of(c * BLOCK, BLOCK)
        g = g_ref[0, pl.ds(r0, BLOCK), :][:, :4 * H_M] + bias_ref[...]
        valid = (rloc + r0) >= PAD
        lsig = -(jnp.maximum(-g, 0.0) + jnp.log1p(jnp.exp(-jnp.abs(g))))
        lf = jnp.where(valid, lsig, 0.0)
        cum = _split_dot(tril, lf)
        rcum = _split_dot(triu, lf)
        li = jnp.where(valid, g, -jnp.inf)
        out = jnp.where(typ == 1, cum, jnp.where(typ == 3, rcum, li))
        o_ref[0, pl.ds(r0, BLOCK), :] = out
        return carry

    lax.fori_loop(0, NBLK, body, 0)


def gate_prep(mg3, gate_bias):
    return pl.pallas_call(
        _gate_kernel,
        grid=(BATCH,),
        in_specs=[
            pl.BlockSpec((1, LP, 128), lambda b: (b, 0, 0)),
            pl.BlockSpec((1, 4 * H_M), lambda b: (0, 0)),
        ],
        out_specs=pl.BlockSpec((1, LP, 4 * H_M), lambda b: (b, 0, 0)),
        out_shape=jax.ShapeDtypeStruct((BATCH, LP, 4 * H_M), F32),
        compiler_params=_cparams(1),
        name="gate_prep",
    )(mg3, gate_bias.reshape(1, 4 * H_M))


MLSTM_HP = 2
MLSTM_MID = NBLK // 2


def _mlstm_kernel(qt_ref, k_ref, vt_ref, ot_ref, gr_ref, y_ref,
                  hs_ref, c_ref, n_ref, m_ref):
    c_ref[...] = jnp.zeros_like(c_ref)
    n_ref[...] = jnp.zeros_like(n_ref)
    m_ref[...] = jnp.zeros_like(m_ref)
    si = lax.broadcasted_iota(jnp.int32, (BLOCK, BLOCK), 0)
    ti = lax.broadcasted_iota(jnp.int32, (BLOCK, BLOCK), 1)
    mask_f = si <= ti
    mask_b = si >= ti

    def chain(c, hl, bwd, final):
        idx = 2 * hl + bwd
        r0 = c * BLOCK if isinstance(c, int) else pl.multiple_of(c * BLOCK, BLOCK)
        t_sl = pl.ds(r0, BLOCK)
        qt = qt_ref[0, hl * DK_M:(hl + 1) * DK_M, t_sl]
        k = k_ref[0, t_sl, hl * DK_M:(hl + 1) * DK_M]
        feat = slice(hl * DV_M, (hl + 1) * DV_M)
        vt = vt_ref[0, feat, t_sl]
        gr = gr_ref[0, hl, :, t_sl]
        li_r, b_r = gr[2 * bwd:2 * bwd + 1, :], gr[2 * bwd + 1:2 * bwd + 2, :]
        a_c = jnp.transpose(jnp.broadcast_to(li_r - b_r, (BLOCK, BLOCK)))
        b_end = b_r[:, 0:1] if bwd else b_r[:, BLOCK - 1:BLOCK]
        m_prev = m_ref[idx][:, 0:1]
        ct = c_ref[idx]
        nst = n_ref[idx]
        dmat = jnp.where(mask_b if bwd else mask_f, a_c + b_r, -jnp.inf)
        inter = b_r + m_prev
        m_t = jnp.maximum(inter, jnp.max(dmat, axis=0, keepdims=True))
        w_inter = jnp.exp(inter - m_t)
        st = jnp.dot(k, qt, preferred_element_type=F32) * jnp.exp(dmat - m_t)
        num = (w_inter * jnp.dot(ct.astype(BF16), qt, preferred_element_type=F32)
               + jnp.dot(vt, st.astype(BF16), preferred_element_type=F32))
        nq = jnp.dot(nst.astype(BF16), qt, preferred_element_type=F32)
        den = w_inter * nq + jnp.sum(st, axis=0, keepdims=True)
        h = num * (1.0 / jnp.maximum(jnp.abs(den), jnp.exp(-m_t)))
        if final:
            og = jax.nn.sigmoid(ot_ref[0, feat, t_sl].astype(F32))
            y_ref[0, feat, pl.ds(r0 - BLOCK, BLOCK)] = (og * (hs_ref[feat, t_sl] + h)).astype(BF16)
        else:
            hs_ref[feat, t_sl] = h
        ldec = b_end - b_r + li_r
        m_new = jnp.maximum(b_end + m_prev, jnp.max(ldec, axis=1, keepdims=True))
        w_c = jnp.exp(b_end + m_prev - m_new)
        w_s = jnp.exp(ldec - m_new)
        wvt = (vt.astype(F32) * w_s).astype(BF16)
        c_ref[idx] = w_c * ct + jnp.dot(wvt, k, preferred_element_type=F32)
        n_ref[idx] = w_c * nst + jnp.dot(w_s.astype(BF16), k, preferred_element_type=F32)
        m_ref[idx] = jnp.broadcast_to(m_new, (1, BLOCK))

    def first_half(i, carry):
        for hl in range(MLSTM_HP):
            chain(i, hl, 0, False)
            chain(NBLK - 1 - i, hl, 1, False)
        return carry

    def second_half(i, carry):
        for hl in range(MLSTM_HP):
            chain(i, hl, 0, True)
            chain(NBLK - 1 - i, hl, 1, True)
        return carry

    lax.fori_loop(0, MLSTM_MID, first_half, 0)
    for hl in range(MLSTM_HP):
        chain(MLSTM_MID, hl, 0, False)
        chain(MLSTM_MID, hl, 1, True)
    lax.fori_loop(MLSTM_MID + 1, NBLK - 1, second_half, 0)
    for hl in range(MLSTM_HP):
        chain(NBLK - 1, hl, 0, True)


def mlstm(q_t, qk_m, v_t, o_t, grow):
    hp = MLSTM_HP
    kw, vw = hp * DK_M, hp * DV_M
    return pl.pallas_call(
        _mlstm_kernel,
        grid=(BATCH, H_M // hp),
        in_specs=[
            pl.BlockSpec((1, kw, LP), lambda b, g: (b, g, 0)),
            pl.BlockSpec((1, LP, kw), lambda b, g: (b, 0, (H_M * DK_M) // kw + g)),
            pl.BlockSpec((1, vw, LP), lambda b, g: (b, g, 0)),
            pl.BlockSpec((1, vw, LP), lambda b, g: (b, g, 0)),
            pl.BlockSpec((1, hp, 4, LP), lambda b, g: (b, g, 0, 0)),
        ],
        out_specs=pl.BlockSpec((1, vw, SEQ), lambda b, g: (b, g, 0)),
        out_shape=jax.ShapeDtypeStruct((BATCH, H_M * DV_M, SEQ), BF16),
        scratch_shapes=[
            pltpu.VMEM((vw, LP), F32),
            pltpu.VMEM((2 * hp, DV_M, DK_M), F32),
            pltpu.VMEM((2 * hp, 1, DK_M), F32),
            pltpu.VMEM((2 * hp, 1, BLOCK), F32),
        ],
        compiler_params=_cparams(2),
        name="mlstm",
    )(q_t, qk_m, v_t, o_t, grow)


LOG2E = 1.4426950408889634
ATT_QB = 2
ATT_TQ = ATT_QB * BLOCK
ATT_BAND = (ATT_QB + 2) * BLOCK
ATT_GROUPS = (6, 6, 6, 6, 5)
assert BLOCK >= MAX_DISTANCE and ATT_BAND + sum(ATT_GROUPS) * BLOCK == LP


def _attn_kernel(c_ref, qa_ref, qb_ref, k1_ref, v1_ref, tab_ref, lam_ref, sg_ref, o_ref,
                 s_ref, k_ref, v_ref):
    h = pl.program_id(1)
    qb = ATT_QB * pl.program_id(2) + 1

    @pl.when(pl.program_id(2) == 0)
    def _():
        for rep in range(2):
            k_ref[0, rep * LP:(rep + 1) * LP, :] = k1_ref[0]
            v_ref[0, rep * LP:(rep + 1) * LP, :] = v1_ref[0]

    q = jnp.concatenate([qa_ref[0], qb_ref[0]], axis=0)
    scale = DK_DA ** -0.5 * LOG2E
    c_neg = c_ref[h, 0]
    c_pos = c_ref[h, 1]
    lp = lam_ref[...]
    lam = (jnp.exp(jnp.sum(lp[0:1] * lp[1:2], axis=1, keepdims=True))
           - jnp.exp(jnp.sum(lp[2:3] * lp[3:4], axis=1, keepdims=True)) + LAMBDA_INIT)

    groups = [((qb - 1) * BLOCK, ATT_BAND, 0)]
    col = ATT_BAND
    for nblk in ATT_GROUPS:
        groups.append(((qb - 1) * BLOCK + col, nblk * BLOCK, col))
        col += nblk * BLOCK

    def lane_fold(acc, t, op):
        for j in range(t.shape[1] // BLOCK):
            piece = t[:, j * BLOCK:(j + 1) * BLOCK]
            acc = piece if acc is None else op(acc, piece)
        return acc

    mx = [None, None]
    for gi, (koff, width, col0) in enumerate(groups):
        koff = pl.multiple_of(koff, BLOCK)
        if gi == 0:
            bias = tab_ref[0, 0]
        else:
            kpos = koff + lax.broadcasted_iota(jnp.int32, (1, width), 1)
            bias = jnp.where(kpos < LP, c_pos, jnp.where(kpos < LP + PAD, NEG_INF, c_neg))
        for m in range(2):
            kk = k_ref[0, pl.ds(koff, width), m * DK_DA:(m + 1) * DK_DA]
            s = lax.dot_general(q[:, m * DK_DA:(m + 1) * DK_DA], kk, (((1,), (1,)), ((), ())),
                                preferred_element_type=F32) * scale + bias
            s_ref[m, :, col0:col0 + width] = s
            mx[m] = lane_fold(mx[m], s, jnp.maximum)
    row_max = [jnp.max(mx[m], axis=1, keepdims=True) for m in range(2)]

    lsum = [None, None]
    acc = [None, None]
    for koff, width, col0 in groups:
        koff = pl.multiple_of(koff, BLOCK)
        vv = v_ref[0, pl.ds(koff, width), :]
        for m in range(2):
            p = jnp.exp2(s_ref[m, :, col0:col0 + width] - row_max[m])
            lsum[m] = lane_fold(lsum[m], p, jnp.add)
            pv = jnp.dot(p.astype(BF16), vv, preferred_element_type=F32)
            acc[m] = pv if acc[m] is None else acc[m] + pv
    l1 = jnp.sum(lsum[0], axis=1, keepdims=True)
    l2 = jnp.sum(lsum[1], axis=1, keepdims=True)
    o = acc[0] / l1 - lam * (acc[1] / l2)
    o_ref[0] = (_rms(o, sg_ref[...]) * (1.0 - LAMBDA_INIT)).astype(BF16)


def diff_attention(consts, proj3, tab, lam_params, subln):
    nq = SEQ // ATT_TQ
    kblk0 = COL_DA_K // (2 * DK_DA)
    vblk0 = COL_DA_V // DV_DA

    def tab_map(b, h, i):
        case = jnp.where(i == 0, 0, jnp.where(i == nq - 1, 2, 1))
        return (h, case, 0, 0)

    return pl.pallas_call(
        _attn_kernel,
        grid=(BATCH, H_DA, nq),
        in_specs=[
            pl.BlockSpec(memory_space=pltpu.SMEM),
            pl.BlockSpec((1, BLOCK, 2 * DK_DA), lambda b, h, i: (b, ATT_QB * i + 1, h)),
            pl.BlockSpec((1, BLOCK, 2 * DK_DA), lambda b, h, i: (b, ATT_QB * i + 2, h)),
            pl.BlockSpec((1, LP, 2 * DK_DA), lambda b, h, i: (b, 0, kblk0 + h)),
            pl.BlockSpec((1, LP, DV_DA), lambda b, h, i: (b, 0, vblk0 + h)),
            pl.BlockSpec((1, 1, ATT_TQ, ATT_BAND), tab_map),
            pl.BlockSpec((4, DK_DA), lambda b, h, i: (0, 0)),
            pl.BlockSpec((1, DV_DA), lambda b, h, i: (0, 0)),
        ],
        out_specs=pl.BlockSpec((1, ATT_TQ, DV_DA), lambda b, h, i: (b, i, h)),
        out_shape=jax.ShapeDtypeStruct((BATCH, SEQ, H_DA * DV_DA), BF16),
        scratch_shapes=[pltpu.VMEM((2, ATT_TQ, LP), F32),
                        pltpu.VMEM((1, 2 * LP, 2 * DK_DA), BF16),
                        pltpu.VMEM((1, 2 * LP, DV_DA), BF16)],
        compiler_params=_cparams(3),
        name="diff_attn",
    )(consts, proj3, proj3, proj3, proj3, tab, lam_params, subln.reshape(1, DV_DA))


def _bias_tables(rel_bias):
    rb = rel_bias.astype(F32)
    span = 1024
    assert span >= ATT_TQ + ATT_BAND
    rel = jnp.arange(span, dtype=jnp.int32) - span // 2
    nb = N_BUCKETS // 2
    max_exact = nb // 2
    n = jnp.abs(rel)
    nf = jnp.maximum(n, 1).astype(F32)
    large = max_exact + (jnp.log(nf / max_exact) / math.log(MAX_DISTANCE / max_exact)
                         * (nb - max_exact)).astype(jnp.int32)
    large = jnp.minimum(large, nb - 1)
    bucket = jnp.where(rel > 0, nb, 0) + jnp.where(n < max_exact, n, large)
    hit = bucket[None, :, None] == jnp.arange(N_BUCKETS, dtype=jnp.int32)
    by_rel = jnp.sum(jnp.where(hit, rb.T[:, None, :], 0.0), axis=-1)
    shifted = jnp.tile(by_rel, (1, ATT_TQ))[:, :ATT_TQ * (span - 1)].reshape(
        H_DA, ATT_TQ, span - 1)
    c0 = span // 2 - BLOCK
    gen = shifted[:, :, c0:c0 + ATT_BAND]
    c_neg = rb[nb - 1]
    c_pos = rb[N_BUCKETS - 1]
    jj = jnp.arange(ATT_BAND, dtype=jnp.int32)[None, None, :]
    first = jnp.where(jj < PAD, NEG_INF, gen)
    wrap0 = ATT_BAND - BLOCK
    wrapped = jnp.where(jj - wrap0 < PAD, NEG_INF, c_neg[:, None, None])
    last = jnp.where(jj >= wrap0, wrapped, gen)
    tab = jnp.stack([first, gen, last], axis=1)
    consts = jnp.stack([c_neg, c_pos], axis=1)
    return tab * LOG2E, consts * LOG2E


def _mix_kernel(ya_ref, ym_ref, wa_ref, wm_ref, ga_ref, gm_ref, o_ref, wa_bf, wm_bf):
    @pl.when(pl.program_id(1) == 0)
    def _():
        wa_bf[...] = wa_ref[...].astype(BF16)
        wm_bf[...] = wm_ref[...].astype(BF16)

    a = jnp.dot(ya_ref[...], wa_bf[...], preferred_element_type=F32)
    m = jnp.dot(ym_ref[...], wm_bf[...], preferred_element_type=F32)
    o_ref[...] = (ga_ref[...].astype(F32) * a + gm_ref[...].astype(F32) * m).astype(BF16)


def branch_mix(y_da, y_m, w_da, w_m, gate, *, tm=512, tn=1024):
    m, k = y_da.shape
    nj = D_MODEL // tn
    return pl.pallas_call(
        _mix_kernel,
        grid=(nj, m // tm),
        in_specs=[
            pl.BlockSpec((tm, k), lambda j, i: (i, 0)),
            pl.BlockSpec((tm, k), lambda j, i: (i, 0)),
            pl.BlockSpec((k, tn), lambda j, i: (0, j)),
            pl.BlockSpec((k, tn), lambda j, i: (0, j)),
            pl.BlockSpec((tm, tn), lambda j, i: (i, j)),
            pl.BlockSpec((tm, tn), lambda j, i: (i, nj + j)),
        ],
        out_specs=pl.BlockSpec((tm, tn), lambda j, i: (i, j)),
        out_shape=jax.ShapeDtypeStruct((m, D_MODEL), BF16),
        scratch_shapes=[pltpu.VMEM((k, tn), BF16), pltpu.VMEM((k, tn), BF16)],
        compiler_params=_cparams(2),
        name="branch_mix",
    )(y_da, y_m, w_da, w_m, gate, gate)


FFN_TM = 1024
HALF = D_MODEL // 2


def _pack_bf16_pairs(v):
    lo = lax.bitcast_convert_type(v[:, :HALF].astype(BF16).astype(F32), jnp.uint32)
    hi = lax.bitcast_convert_type(v[:, HALF:].astype(BF16).astype(F32), jnp.uint32)
    return (hi & jnp.uint32(0xFFFF0000)) | lax.shift_right_logical(lo, jnp.uint32(16))


def _unpack_bf16_pairs(w):
    lo = lax.bitcast_convert_type(lax.shift_left(w, jnp.uint32(16)), F32).astype(BF16)
    hi = lax.bitcast_convert_type(w & jnp.uint32(0xFFFF0000), F32).astype(BF16)
    return lo, hi


def _ffn_prep_kernel(h_ref, g_ref, wr_ref, br_ref, hn_ref, e_ref, w_ref, r_ref, cnt_ref, base_ref):
    @pl.when(pl.program_id(0) == 0)
    def _():
        base_ref[...] = jnp.zeros_like(base_ref)

    hn = _rms(h_ref[...], g_ref[...])
    hn_ref[...] = _pack_bf16_pairs(hn)
    logits = jnp.dot(hn.astype(BF16), wr_ref[...].astype(BF16),
                     preferred_element_type=F32) + br_ref[...]
    lane = lax.broadcasted_iota(jnp.int32, (FFN_TM, N_EXPERTS), 1)
    lane_o = lax.broadcasted_iota(jnp.int32, (FFN_TM, 128), 1)
    ti = lax.broadcasted_iota(jnp.int32, (FFN_TM, FFN_TM), 0)
    ui = lax.broadcasted_iota(jnp.int32, (FFN_TM, FFN_TM), 1)
    tril = jnp.where(ui <= ti, 1.0, 0.0).astype(BF16)
    e_out = jnp.zeros((FFN_TM, 128), jnp.int32)
    r_out = jnp.zeros((FFN_TM, 128), jnp.int32)
    l_out = jnp.full((FFN_TM, 128), -jnp.inf, F32)
    base = base_ref[...]
    l = logits
    for kk in range(TOP_K):
        mk = jnp.max(l, axis=1, keepdims=True)
        ik = jnp.min(jnp.where(l == mk, lane, N_EXPERTS), axis=1, keepdims=True)
        hit = lane == ik
        oh = jnp.where(hit, 1.0, 0.0)
        cum = jnp.dot(tril, oh.astype(BF16), preferred_element_type=F32)
        rank = jnp.sum(oh * (cum + base), axis=1, keepdims=True) - 1.0
        base = base + jnp.sum(oh, axis=0, keepdims=True)
        e_out = jnp.where(lane_o == kk, ik, e_out)
        r_out = jnp.where(lane_o == kk, rank.astype(jnp.int32), r_out)
        l_out = jnp.where(lane_o == kk, mk, l_out)
        l = jnp.where(hit, -jnp.inf, l)
    base_ref[...] = base
    cnt_ref[...] = base
    ex = jnp.exp(l_out - jnp.max(l_out, axis=1, keepdims=True))
    e_ref[...] = e_out
    r_ref[...] = r_out
    w_ref[...] = ex / jnp.sum(ex, axis=1, keepdims=True)


def ffn_prep(h2, gain, w_router, b_router):
    row = lambda i: (i, 0)
    fixed = lambda i: (0, 0)
    return pl.pallas_call(
        _ffn_prep_kernel,
        grid=(N_TOK // FFN_TM,),
        in_specs=[
            pl.BlockSpec((FFN_TM, D_MODEL), row),
            pl.BlockSpec((1, D_MODEL), fixed),
            pl.BlockSpec((D_MODEL, N_EXPERTS), fixed),
            pl.BlockSpec((1, N_EXPERTS), fixed),
        ],
        out_specs=[
            pl.BlockSpec((FFN_TM, HALF), row),
            pl.BlockSpec((FFN_TM, 128), row),
            pl.BlockSpec((FFN_TM, 128), row),
            pl.BlockSpec((FFN_TM, 128), row),
            pl.BlockSpec((1, N_EXPERTS), fixed),
        ],
        out_shape=[
            jax.ShapeDtypeStruct((N_TOK, HALF), jnp.uint32),
            jax.ShapeDtypeStruct((N_TOK, 128), jnp.int32),
            jax.ShapeDtypeStruct((N_TOK, 128), F32),
            jax.ShapeDtypeStruct((N_TOK, 128), jnp.int32),
            jax.ShapeDtypeStruct((1, N_EXPERTS), F32),
        ],
        scratch_shapes=[pltpu.VMEM((1, N_EXPERTS), F32)],
        compiler_params=_cparams(1),
        name="ffn_prep",
    )(h2, gain.reshape(1, D_MODEL), w_router, b_router.reshape(1, N_EXPERTS))


SEG_ALIGN = 128
MOE_R = N_ASSIGN + N_EXPERTS * SEG_ALIGN


def _plan(counts_f, top_e, rank):
    counts = counts_f[0].astype(jnp.int32)
    seg_rows = (counts + SEG_ALIGN - 1) // SEG_ALIGN * SEG_ALIGN
    seg_start = jnp.cumsum(seg_rows) - seg_rows
    eq = top_e[:, :, None] == jnp.arange(N_EXPERTS, dtype=jnp.int32)[None, None, :]
    dest = jnp.sum(jnp.where(eq, seg_start[None, None, :], 0), axis=-1) + rank
    used = jnp.sum(seg_rows)
    slack = jnp.stack([used, (MOE_R - used) // SEG_ALIGN])
    return (dest.reshape(N_ASSIGN).astype(jnp.int32), seg_start.astype(jnp.int32),
            seg_rows.astype(jnp.int32), (seg_start + counts).astype(jnp.int32),
            (seg_rows - counts).astype(jnp.int32), slack.astype(jnp.int32))


def _zero_slack(slack_ref, zero_block, dst_rows, sem):
    zero_block[...] = jnp.zeros_like(zero_block)

    def copy(j):
        r0 = pl.multiple_of(slack_ref[0] + j * SEG_ALIGN, SEG_ALIGN)
        return pltpu.make_async_copy(zero_block, dst_rows(pl.ds(r0, SEG_ALIGN)), sem)

    def start(j, carry):
        copy(j).start()
        return carry

    def wait(j, carry):
        copy(j).wait()
        return carry

    lax.fori_loop(0, slack_ref[1], start, 0)
    lax.fori_loop(0, slack_ref[1], wait, 0)


DISP_TOK = 256


def _dispatch_kernel(dest_ref, pad0_ref, padn_ref, slack_ref, hn_ref, xs_ref,
                     zrow_ref, zblk_ref, sem, zsem):
    i = pl.program_id(0)

    @pl.when(i == 0)
    def _():
        zrow_ref[...] = jnp.zeros_like(zrow_ref)
        _zero_slack(slack_ref, zblk_ref, lambda rows: xs_ref.at[rows], zsem.at[0])

        def expert(e, carry):
            p0 = pad0_ref[e]
            pn = padn_ref[e]

            def zstart(r, c2):
                pltpu.make_async_copy(zrow_ref, xs_ref.at[pl.ds(p0 + r, 1)], zsem.at[0]).start()
                return c2

            def zwait(r, c2):
                pltpu.make_async_copy(zrow_ref, xs_ref.at[pl.ds(p0, 1)], zsem.at[0]).wait()
                return c2

            lax.fori_loop(0, pn, zstart, 0)
            lax.fori_loop(0, pn, zwait, 0)
            return carry

        lax.fori_loop(0, N_EXPERTS, expert, 0)

    def tok(t, carry):
        a = (i * DISP_TOK + t) * TOP_K
        for kk in range(TOP_K):
            pltpu.make_async_copy(hn_ref.at[pl.ds(t, 1)], xs_ref.at[pl.ds(dest_ref[a + kk], 1)],
                                  sem.at[0]).start(priority=kk % 2)
        return carry

    lax.fori_loop(0, DISP_TOK, tok, 0, unroll=4)
    for kk in range(TOP_K):
        pltpu.make_async_copy(hn_ref, xs_ref.at[pl.ds(0, DISP_TOK)], sem.at[0]).wait()


def dispatch(dest, pad0, padn, slack, hn_packed):
    grid_spec = pltpu.PrefetchScalarGridSpec(
        num_scalar_prefetch=4,
        grid=(N_TOK // DISP_TOK,),
        in_specs=[pl.BlockSpec((DISP_TOK, HALF), lambda i, d, p0, pn, z: (i, 0))],
        out_specs=pl.BlockSpec(memory_space=pl.ANY),
        scratch_shapes=[pltpu.VMEM((1, HALF), jnp.uint32),
                        pltpu.VMEM((SEG_ALIGN, HALF), jnp.uint32),
                        pltpu.SemaphoreType.DMA((1,)), pltpu.SemaphoreType.DMA((1,))],
    )
    return pl.pallas_call(
        _dispatch_kernel,
        grid_spec=grid_spec,
        out_shape=jax.ShapeDtypeStruct((MOE_R, HALF), jnp.uint32),
        compiler_params=_cparams(1),
        name="moe_dispatch",
    )(dest, pad0, padn, slack, hn_packed)


MOE_CH = 2 * SEG_ALIGN
MOE_TF = 1024
MOE_NF = D_FF // MOE_TF


class _CopyGroup:
    def __init__(self, copies):
        self.copies = copies

    def start(self, priority=0):
        for cp in self.copies:
            cp.start(priority=priority)

    def wait(self):
        for cp in self.copies:
            cp.wait()


W_PIECES = 8
W_PER_CHUNK = 2


N_STATE = 4


def _stream_rows(step, n_steps, start, rows, next_start, next_rows, state,
                 make_in, make_out, make_tail_in, make_tail_out,
                 compute_chunk, compute_tail, before_first_wait, next_weight_piece):
    has_next = step + 1 < n_steps

    def request_weights(first, count):
        def one(p, carry):
            @pl.when(jnp.logical_and(has_next, p < W_PIECES))
            def _():
                next_weight_piece(p).start()
            return carry

        lax.fori_loop(first, first + count, one, 0)

    @pl.when(step == 0)
    def _():
        for j in range(N_STATE):
            state[j] = 0

    n_ch = lax.shift_right_logical(rows, MOE_CH.bit_length() - 1)
    tail = rows - n_ch * MOE_CH
    tail_row = start + n_ch * MOE_CH
    g0 = state[0]
    feeds_next = jnp.logical_and(step + 1 < n_steps, next_rows >= MOE_CH)

    def chunk_row(c):
        return start + c * MOE_CH

    @pl.when(tail > 0)
    def _():
        make_tail_in(tail_row).start(priority=1)

    @pl.when(jnp.logical_and(n_ch > 0, state[3] == 0))
    def _():
        make_in(start, lax.rem(g0, 2)).start(priority=1)

    before_first_wait()

    @pl.when(tail > 0)
    def _():
        make_tail_in(tail_row).wait()
        compute_tail()
        make_tail_out(tail_row).start(priority=1)

    def body(c, carry):
        slot = lax.rem(g0 + c, 2)
        make_in(chunk_row(c), slot).wait()

        @pl.when(c + 1 < n_ch)
        def _():
            make_in(chunk_row(c + 1), 1 - slot).start(priority=1)

        @pl.when(jnp.logical_and(c + 1 == n_ch, feeds_next))
        def _():
            make_in(next_start, 1 - slot).start(priority=1)

        request_weights(c * W_PER_CHUNK, W_PER_CHUNK)

        @pl.when(state[1 + slot] == 1)
        def _():
            make_out(chunk_row(c), slot).wait()

        compute_chunk(slot)
        make_out(chunk_row(c), slot).start(priority=1)
        state[1 + slot] = 1
        return carry

    lax.fori_loop(0, n_ch, body, 0)
    request_weights(n_ch * W_PER_CHUNK, W_PIECES)
    state[0] = g0 + n_ch
    state[3] = jnp.where(jnp.logical_and(n_ch > 0, feeds_next), 1, 0)

    @pl.when(tail > 0)
    def _():
        make_tail_out(tail_row).wait()

    @pl.when(step == n_steps - 1)
    def _():
        for slot in range(2):
            @pl.when(state[1 + slot] == 1)
            def _():
                make_out(0, slot).wait()
                state[1 + slot] = 0


def _rows_at(row0, n):
    return pl.ds(row0 if isinstance(row0, int) else pl.multiple_of(row0, SEG_ALIGN), n)


def _moe_up_kernel(seg_ref, rows_ref, slack_ref, xs_ref, w_hbm, bg_ref, bl_ref, act_ref,
                   xbuf, obuf, xtail, otail, wbuf, wg_bf, wl_bf, state, isem, osem, tsem, wsem):
    f = pl.program_id(0)
    e = pl.program_id(1)
    start = seg_ref[e]
    rows = rows_ref[e]
    e_next = lax.rem(e + 1, N_EXPERTS)
    f_next = jnp.where(e == N_EXPERTS - 1, f + 1, f)
    step = f * N_EXPERTS + e
    n_steps = MOE_NF * N_EXPERTS
    wslot = lax.rem(step, 2)

    def weight_piece(expert, ftile, slot, p):
        per_half = W_PIECES // 2
        band = D_MODEL // per_half
        t = p // per_half if isinstance(p, int) else lax.shift_right_logical(
            p, per_half.bit_length() - 1)
        r0 = (p - t * per_half) * band
        r0 = r0 if isinstance(r0, int) else pl.multiple_of(r0, band)
        col0 = pl.multiple_of((t * MOE_NF + ftile) * MOE_TF, MOE_TF)
        return pltpu.make_async_copy(
            w_hbm.at[expert, pl.ds(r0, band), pl.ds(col0, MOE_TF)],
            wbuf.at[slot, t, pl.ds(r0, band)], wsem.at[slot])

    @pl.when(step == 0)
    def _():
        for p in range(W_PIECES):
            weight_piece(e, f, wslot, p).start()

    def make_in(r0, slot):
        return pltpu.make_async_copy(xs_ref.at[_rows_at(r0, MOE_CH)], xbuf.at[slot], isem.at[slot])

    def make_out(r0, slot):
        return pltpu.make_async_copy(obuf.at[slot], act_ref.at[f, _rows_at(r0, MOE_CH)],
                                     osem.at[slot])

    def make_tail_in(r0):
        return pltpu.make_async_copy(xs_ref.at[_rows_at(r0, SEG_ALIGN)], xtail, tsem.at[0])

    def make_tail_out(r0):
        return pltpu.make_async_copy(otail, act_ref.at[f, _rows_at(r0, SEG_ALIGN)], tsem.at[1])

    def cast_weights():
        for p in range(W_PIECES):
            weight_piece(e, f, wslot, p).wait()

        @pl.when(rows > 0)
        def _():
            wg_bf[...] = wbuf[wslot, 0].astype(BF16)
            wl_bf[...] = wbuf[wslot, 1].astype(BF16)

    def expert_mlp(words):
        lo, hi = _unpack_bf16_pairs(words)
        glu = (jnp.dot(lo, wg_bf[:HALF, :], preferred_element_type=F32)
               + jnp.dot(hi, wg_bf[HALF:, :], preferred_element_type=F32) + bg_ref[0])
        lin = (jnp.dot(lo, wl_bf[:HALF, :], preferred_element_type=F32)
               + jnp.dot(hi, wl_bf[HALF:, :], preferred_element_type=F32) + bl_ref[0])
        glu = jnp.minimum(glu, SWIGLU_LIMIT)
        lin = jnp.clip(lin, -SWIGLU_LIMIT, SWIGLU_LIMIT)
        return (glu * jax.nn.sigmoid(SWIGLU_ALPHA * glu) * (lin + 1.0)).astype(BF16)

    def compute_chunk(slot):
        obuf[slot] = expert_mlp(xbuf[slot])

    def compute_tail():
        otail[...] = expert_mlp(xtail[...])

    _stream_rows(step, n_steps, start, rows,
                 seg_ref[e_next], rows_ref[e_next], state,
                 make_in, make_out, make_tail_in, make_tail_out,
                 compute_chunk, compute_tail, cast_weights,
                 lambda p: weight_piece(e_next, f_next, 1 - wslot, p))

    @pl.when(e == N_EXPERTS - 1)
    def _():
        _zero_slack(slack_ref, otail, lambda rr: act_ref.at[f, rr], tsem.at[1])


def moe_up(seg_start, seg_rows, slack, xs, w1, b1):
    grid_spec = pltpu.PrefetchScalarGridSpec(
        num_scalar_prefetch=3,
        grid=(MOE_NF, N_EXPERTS),
        in_specs=[
            pl.BlockSpec(memory_space=pl.ANY),
            pl.BlockSpec(memory_space=pl.ANY),
            pl.BlockSpec((1, 1, MOE_TF), lambda f, e, s, r, z: (e, 0, f)),
            pl.BlockSpec((1, 1, MOE_TF), lambda f, e, s, r, z: (e, 0, MOE_NF + f)),
        ],
        out_specs=pl.BlockSpec(memory_space=pl.ANY),
        scratch_shapes=[
            pltpu.VMEM((2, MOE_CH, HALF), jnp.uint32),
            pltpu.VMEM((2, MOE_CH, MOE_TF), BF16),
            pltpu.VMEM((SEG_ALIGN, HALF), jnp.uint32),
            pltpu.VMEM((SEG_ALIGN, MOE_TF), BF16),
            pltpu.VMEM((2, 2, D_MODEL, MOE_TF), F32),
            pltpu.VMEM((D_MODEL, MOE_TF), BF16),
            pltpu.VMEM((D_MODEL, MOE_TF), BF16),
            pltpu.SMEM((N_STATE,), jnp.int32),
            pltpu.SemaphoreType.DMA((2,)),
            pltpu.SemaphoreType.DMA((2,)),
            pltpu.SemaphoreType.DMA((2,)),
            pltpu.SemaphoreType.DMA((2,)),
        ],
    )
    b13 = b1.reshape(N_EXPERTS, 1, 2 * D_FF)
    return pl.pallas_call(
        _moe_up_kernel,
        grid_spec=grid_spec,
        out_shape=jax.ShapeDtypeStruct((MOE_NF, MOE_R, MOE_TF), BF16),
        compiler_params=pltpu.CompilerParams(
            dimension_semantics=("arbitrary", "arbitrary"), vmem_limit_bytes=MOE_VMEM_LIMIT),
        name="moe_up",
    )(seg_start, seg_rows, slack, xs, w1, b13, b13)


def _moe_down_kernel(seg_ref, rows_ref, slack_ref, act_ref, w_hbm, b_ref, y_ref,
                     xbuf, obuf, xtail, otail, wbuf, w_bf, state, isem, osem, tsem, wsem):
    e = pl.program_id(0)
    start = seg_ref[e]
    rows = rows_ref[e]
    e_next = lax.rem(e + 1, N_EXPERTS)
    wslot = lax.rem(e, 2)

    def weight_piece(expert, slot, p):
        band = D_FF // W_PIECES
        r0 = p * band if isinstance(p, int) else pl.multiple_of(p * band, band)
        return pltpu.make_async_copy(w_hbm.at[expert, pl.ds(r0, band)],
                                     wbuf.at[slot, pl.ds(r0, band)], wsem.at[slot])

    @pl.when(e == 0)
    def _():
        for p in range(W_PIECES):
            weight_piece(e, wslot, p).start()

    def make_in(r0, slot):
        return _CopyGroup([
            pltpu.make_async_copy(act_ref.at[j, _rows_at(r0, MOE_CH)],
                                  xbuf.at[slot, :, pl.ds(j * MOE_TF, MOE_TF)], isem.at[slot])
            for j in range(MOE_NF)])

    def make_out(r0, slot):
        return pltpu.make_async_copy(obuf.at[slot], y_ref.at[_rows_at(r0, MOE_CH)], osem.at[slot])

    def make_tail_in(r0):
        return _CopyGroup([
            pltpu.make_async_copy(act_ref.at[j, _rows_at(r0, SEG_ALIGN)],
                                  xtail.at[:, pl.ds(j * MOE_TF, MOE_TF)], tsem.at[0])
            for j in range(MOE_NF)])

    def make_tail_out(r0):
        return pltpu.make_async_copy(otail, y_ref.at[_rows_at(r0, SEG_ALIGN)], tsem.at[1])

    def cast_weights():
        for p in range(W_PIECES):
            weight_piece(e, wslot, p).wait()

        @pl.when(rows > 0)
        def _():
            w_bf[...] = wbuf[wslot].astype(BF16)

    def expert_out(a):
        return _pack_bf16_pairs(jnp.dot(a, w_bf[...], preferred_element_type=F32) + b_ref[0])

    def compute_chunk(slot):
        obuf[slot] = expert_out(xbuf[slot])

    def compute_tail():
        otail[...] = expert_out(xtail[...])

    _stream_rows(e, N_EXPERTS, start, rows, seg_ref[e_next], rows_ref[e_next], state,
                 make_in, make_out, make_tail_in, make_tail_out,
                 compute_chunk, compute_tail, cast_weights,
                 lambda p: weight_piece(e_next, 1 - wslot, p))

    @pl.when(e == N_EXPERTS - 1)
    def _():
        _zero_slack(slack_ref, otail, lambda rr: y_ref.at[rr], tsem.at[1])


def moe_down(seg_start, seg_rows, slack, act, w2, b2):
    grid_spec = pltpu.PrefetchScalarGridSpec(
        num_scalar_prefetch=3,
        grid=(N_EXPERTS,),
        in_specs=[
            pl.BlockSpec(memory_space=pl.ANY),
            pl.BlockSpec(memory_space=pl.ANY),
            pl.BlockSpec((1, 1, D_MODEL), lambda e, s, r, z: (e, 0, 0)),
        ],
        out_specs=pl.BlockSpec(memory_space=pl.ANY),
        scratch_shapes=[
            pltpu.VMEM((2, MOE_CH, D_FF), BF16),
            pltpu.VMEM((2, MOE_CH, HALF), jnp.uint32),
            pltpu.VMEM((SEG_ALIGN, D_FF), BF16),
            pltpu.VMEM((SEG_ALIGN, HALF), jnp.uint32),
            pltpu.VMEM((2, D_FF, D_MODEL), F32),
            pltpu.VMEM((D_FF, D_MODEL), BF16),
            pltpu.SMEM((N_STATE,), jnp.int32),
            pltpu.SemaphoreType.DMA((2,)),
            pltpu.SemaphoreType.DMA((2,)),
            pltpu.SemaphoreType.DMA((2,)),
            pltpu.SemaphoreType.DMA((2,)),
        ],
    )
    return pl.pallas_call(
        _moe_down_kernel,
        grid_spec=grid_spec,
        out_shape=jax.ShapeDtypeStruct((MOE_R, HALF), jnp.uint32),
        compiler_params=pltpu.CompilerParams(
            dimension_semantics=("arbitrary",), vmem_limit_bytes=MOE_VMEM_LIMIT),
        name="moe_down",
    )(seg_start, seg_rows, slack, act, w2, b2.reshape(N_EXPERTS, 1, D_MODEL))


COMB_TM = 128
COMB_NT = N_TOK // COMB_TM


def _combine_kernel(dest_ref, y_ref, w_ref, h_ref, g_ref, o_ref, buf, sem):
    i = pl.program_id(0)

    def fetch(tile, slot):
        def tok(t, carry):
            a = (tile * COMB_TM + t) * TOP_K
            for kk in range(TOP_K):
                pltpu.make_async_copy(y_ref.at[pl.ds(dest_ref[a + kk], 1)],
                                      buf.at[slot, kk, pl.ds(t, 1)], sem.at[slot]).start(
                                          priority=kk % 2)
            return carry

        lax.fori_loop(0, COMB_TM, tok, 0, unroll=4)

    @pl.when(i == 0)
    def _():
        fetch(0, 0)

    slot = lax.rem(i, 2)

    @pl.when(i + 1 < COMB_NT)
    def _():
        fetch(i + 1, 1 - slot)

    for kk in range(TOP_K):
        pltpu.make_async_copy(y_ref.at[pl.ds(0, COMB_TM)], buf.at[slot, kk], sem.at[slot]).wait()
    w = w_ref[...]
    lo = h_ref[:, :HALF]
    hi = h_ref[:, HALF:]
    for kk in range(TOP_K):
        words = buf[slot, kk]
        wk = w[:, kk:kk + 1]
        lo = lo + wk * lax.bitcast_convert_type(lax.shift_left(words, jnp.uint32(16)), F32)
        hi = hi + wk * lax.bitcast_convert_type(words & jnp.uint32(0xFFFF0000), F32)
    ms = (jnp.sum(lo * lo, axis=-1, keepdims=True)
          + jnp.sum(hi * hi, axis=-1, keepdims=True)) * (1.0 / D_MODEL)
    inv = lax.rsqrt(ms + EPS)
    o_ref[:, :HALF] = lo * inv * g_ref[:, :HALF]
    o_ref[:, HALF:] = hi * inv * g_ref[:, HALF:]


def combine(dest, y, weight, h2, gain):
    grid_spec = pltpu.PrefetchScalarGridSpec(
        num_scalar_prefetch=1,
        grid=(COMB_NT,),
        in_specs=[
            pl.BlockSpec(memory_space=pl.ANY),
            pl.BlockSpec((COMB_TM, 128), lambda i, d: (i, 0)),
            pl.BlockSpec((COMB_TM, D_MODEL), lambda i, d: (i, 0)),
            pl.BlockSpec((1, D_MODEL), lambda i, d: (0, 0)),
        ],
        out_specs=pl.BlockSpec((COMB_TM, D_MODEL), lambda i, d: (i, 0)),
        scratch_shapes=[pltpu.VMEM((2, TOP_K, COMB_TM, HALF), jnp.uint32),
                        pltpu.SemaphoreType.DMA((2,))],
    )
    return pl.pallas_call(
        _combine_kernel,
        grid_spec=grid_spec,
        out_shape=jax.ShapeDtypeStruct((N_TOK, D_MODEL), F32),
        compiler_params=_cparams(1),
        name="moe_combine",
    )(dest, y, weight, h2, gain.reshape(1, D_MODEL))


def kernel(x, meta_tokens, rel_bias, norm_mix, w_in, conv_w, gate_bias_m, lambda_params, subln_da,
           w_branch_da, w_branch_m, w_gate, b_gate, w_out, norm_ffn, w_router, b_router,
           w1, b1, w2, b2, norm_final):
    layer = 0
    xn_pad, xn_real = norm_in(x, meta_tokens, norm_mix[layer])
    xn_pad2 = xn_pad.reshape(BATCH * LP, D_MODEL)
    proj = matmul(xn_pad2, w_in[layer], n_cols=PROJ_COLS, tm=768, tn=1024, name="proj_in")
    proj3 = proj.reshape(BATCH, LP, PROJ_COLS)
    w_g = jnp.pad(w_in[layer][:, COL_M_G:], ((0, 0), (0, 128 - 4 * H_M)))
    mg = matmul(xn_pad2, w_g, n_cols=128, tm=768, tn=128, out_dtype=F32, name="proj_gates")
    gate = matmul(xn_real.reshape(N_TOK, D_MODEL), w_gate[layer], n_cols=2 * D_MODEL,
                  tm=1024, tn=1024, bias=b_gate[layer], act="sigmoid", name="mix_gate")

    tab, consts = _bias_tables(rel_bias)
    y_da = diff_attention(consts, proj3, tab, lambda_params[layer], subln_da[layer])

    qk_m = conv_qk(proj3, conv_w[layer])
    gp = gate_prep(mg.reshape(BATCH, LP, 128), gate_bias_m[layer])
    gp4 = gp.reshape(BATCH, LP, 4, H_M)
    grow = jnp.transpose(gp4, (0, 3, 2, 1))
    swap = lambda t: jnp.swapaxes(t, 1, 2)
    y_m = swap(mlstm(swap(qk_m[:, :, :H_M * DK_M]), qk_m,
                     swap(proj3[:, :, COL_M_V:COL_M_V + H_M * DV_M]),
                     swap(proj3[:, :, COL_M_O:COL_M_O + H_M * DV_M]), grow))

    mixed = branch_mix(y_da.reshape(N_TOK, H_DA * DV_DA), y_m.reshape(N_TOK, H_M * DV_M),
                       w_branch_da[layer], w_branch_m[layer], gate)
    h2 = matmul(mixed, w_out[layer], n_cols=D_MODEL, tm=1024, tn=1024,
                res=x.reshape(N_TOK, D_MODEL), out_dtype=F32, name="out_proj")

    hn_packed, top_e, weight, rank, counts = ffn_prep(
        h2, norm_ffn[layer], w_router[layer], b_router[layer])
    dest, seg_start, seg_rows, pad0, padn, slack = _plan(
        counts, top_e[:, :TOP_K], rank[:, :TOP_K])
    xs = dispatch(dest, pad0, padn, slack, hn_packed)
    act = moe_up(seg_start, seg_rows, slack, xs, w1[layer], b1[layer])
    y = moe_down(seg_start, seg_rows, slack, act, w2[layer], b2[layer])
    out = combine(dest, y, weight, h2, norm_final)
    return out.reshape(BATCH, SEQ, D_MODEL)
```

```python
import functools
import math

import jax
import jax.numpy as jnp
from jax import lax
from jax.experimental import pallas as pl
from jax.experimental.pallas import tpu as pltpu

F32 = jnp.float32
BF16 = jnp.bfloat16

D_MODEL = 2048
BATCH = 2
SEQ = 4096
N_META = 16
BLOCK = 128
PAD = (-N_META) % BLOCK
LP = PAD + N_META + SEQ
NBLK = LP // BLOCK
EPS = 1e-6
NEG_INF = -1e30

H_DA = 4
DK_DA = 128
DV_DA = 256
H_M = 4
DK_M = 128
DV_M = 256
CONV_W = 5
N_BUCKETS = 32
MAX_DISTANCE = 128
N_EXPERTS = 32
TOP_K = 4
D_FF = 2048
SWIGLU_ALPHA = 1.702
SWIGLU_LIMIT = 7.0
LAMBDA_INIT = 0.8 - 0.6 * math.exp(-0.3 * 0)

COL_DA_Q = 0
COL_DA_K = 1024
COL_DA_V = 2048
COL_M_Q = 3072
COL_M_K = 3584
COL_M_V = 4096
COL_M_O = 5120
COL_M_G = 6144
PROJ_COLS = 6144

N_TOK = BATCH * SEQ
N_ASSIGN = N_TOK * TOP_K

VMEM_LIMIT = 52 * 1024 * 1024
MOE_VMEM_LIMIT = 58 * 1024 * 1024


def _cparams(n_axes):
    return pltpu.CompilerParams(
        dimension_semantics=("arbitrary",) * n_axes, vmem_limit_bytes=VMEM_LIMIT)


def _rms(v, gain):
    ms = jnp.mean(v * v, axis=-1, keepdims=True)
    return v * lax.rsqrt(ms + EPS) * gain


def _norm_in_kernel(x_ref, meta_ref, g_ref, pad_ref, real_ref):
    j = pl.program_id(1)
    g = g_ref[...]

    @pl.when(j == 0)
    def _():
        pad_ref[0, :PAD, :] = jnp.zeros((PAD, D_MODEL), BF16)
        pad_ref[0, PAD:, :] = _rms(meta_ref[...], g).astype(BF16)

    @pl.when(j > 0)
    def _():
        y = _rms(x_ref[0], g).astype(BF16)
        pad_ref[0] = y
        real_ref[0] = y


def norm_in(x, meta, gain):
    return pl.pallas_call(
        _norm_in_kernel,
        grid=(BATCH, NBLK),
        in_specs=[
            pl.BlockSpec((1, BLOCK, D_MODEL), lambda b, j: (b, jnp.maximum(j - 1, 0), 0)),
            pl.BlockSpec((N_META, D_MODEL), lambda b, j: (0, 0)),
            pl.BlockSpec((1, D_MODEL), lambda b, j: (0, 0)),
        ],
        out_specs=[
            pl.BlockSpec((1, BLOCK, D_MODEL), lambda b, j: (b, j, 0)),
            pl.BlockSpec((1, BLOCK, D_MODEL), lambda b, j: (b, jnp.maximum(j - 1, 0), 0)),
        ],
        out_shape=[
            jax.ShapeDtypeStruct((BATCH, LP, D_MODEL), BF16),
            jax.ShapeDtypeStruct((BATCH, SEQ, D_MODEL), BF16),
        ],
        compiler_params=_cparams(2),
        name="norm_in",
    )(x, meta, gain.reshape(1, D_MODEL))


def _mm_kernel(*refs, has_bias, has_res, act, w_transposed):
    x_ref, w_ref = refs[0], refs[1]
    pos = 2
    b_ref = r_ref = None
    if has_bias:
        b_ref = refs[pos]
        pos += 1
    if has_res:
        r_ref = refs[pos]
        pos += 1
    o_ref, wbf_ref = refs[pos], refs[pos + 1]

    @pl.when(pl.program_id(1) == 0)
    def _():
        w = w_ref[...]
        wbf_ref[...] = (w.T if w_transposed else w).astype(BF16)

    acc = jnp.dot(x_ref[...], wbf_ref[...], preferred_element_type=F32)
    if has_bias:
        acc = acc + b_ref[...]
    if act == "sigmoid":
        acc = jax.nn.sigmoid(acc)
    if has_res:
        acc = acc + r_ref[...]
    o_ref[...] = acc.astype(o_ref.dtype)


def matmul(x, w, *, n_cols, col_block0=0, tm, tn, bias=None, res=None, act=None,
           out_dtype=BF16, w_transposed=False, name):
    m, k = x.shape
    in_specs = [
        pl.BlockSpec((tm, k), lambda j, i: (i, 0)),
        pl.BlockSpec((tn, k), lambda j, i: (j + col_block0, 0)) if w_transposed
        else pl.BlockSpec((k, tn), lambda j, i: (0, j + col_block0)),
    ]
    args = [x, w]
    if bias is not None:
        in_specs.append(pl.BlockSpec((1, tn), lambda j, i: (0, j)))
        args.append(bias.reshape(1, n_cols))
    if res is not None:
        in_specs.append(pl.BlockSpec((tm, tn), lambda j, i: (i, j)))
        args.append(res)
    return pl.pallas_call(
        functools.partial(_mm_kernel, has_bias=bias is not None, has_res=res is not None, act=act,
                          w_transposed=w_transposed),
        grid=(n_cols // tn, m // tm),
        in_specs=in_specs,
        out_specs=pl.BlockSpec((tm, tn), lambda j, i: (i, j)),
        out_shape=jax.ShapeDtypeStruct((m, n_cols), out_dtype),
        scratch_shapes=[pltpu.VMEM((k, tn), BF16)],
        compiler_params=_cparams(2),
        name=name,
    )(*args)


def _conv_kernel(p_ref, w_ref, o_ref):
    c = pl.program_id(1)
    x = p_ref[0].astype(F32)
    w = w_ref[...]
    half = CONV_W // 2
    acc = w[half:half + 1, :] * x
    for j in range(CONV_W):
        if j != half:
            acc = acc + w[j:j + 1, :] * pltpu.roll(x, (half - j) % LP, axis=0)
    y = acc * jax.nn.sigmoid(acc)
    rows = lax.broadcasted_iota(jnp.int32, (LP, 1), 0)
    y = jnp.where(rows >= PAD, y, 0.0)
    scale = jnp.where(c < 2, DK_M ** -0.5, 1.0).astype(F32)
    o_ref[0] = (y * scale).astype(BF16)


def conv_qk(proj3, conv_w):
    cw = 256
    return pl.pallas_call(
        _conv_kernel,
        grid=(BATCH, (2 * H_M * DK_M) // cw),
        in_specs=[
            pl.BlockSpec((1, LP, cw), lambda b, c: (b, 0, COL_M_Q // cw + c)),
            pl.BlockSpec((CONV_W, cw), lambda b, c: (0, c)),
        ],
        out_specs=pl.BlockSpec((1, LP, cw), lambda b, c: (b, 0, c)),
        out_shape=jax.ShapeDtypeStruct((BATCH, LP, 2 * H_M * DK_M), BF16),
        compiler_params=_cparams(2),
        name="conv_qk",
    )(proj3, conv_w)


def _split_dot(tri, v):
    hi = v.astype(BF16)
    r1 = v - hi.astype(F32)
    mid = r1.astype(BF16)
    lo = (r1 - mid.astype(F32)).astype(BF16)
    return (jnp.dot(tri, hi, preferred_element_type=F32)
            + jnp.dot(tri, mid, preferred_element_type=F32)
            + jnp.dot(tri, lo, preferred_element_type=F32))


def _gate_kernel(g_ref, bias_ref, o_ref):
    ti = lax.broadcasted_iota(jnp.int32, (BLOCK, BLOCK), 0)
    ui = lax.broadcasted_iota(jnp.int32, (BLOCK, BLOCK), 1)
    tril = jnp.where(ui <= ti, 1.0, 0.0).astype(BF16)
    triu = jnp.where(ui >= ti, 1.0, 0.0).astype(BF16)
    ch = lax.broadcasted_iota(jnp.int32, (BLOCK, 4 * H_M), 1)
    typ = lax.shift_right_logical(ch, 2)
    rloc = lax.broadcasted_iota(jnp.int32, (BLOCK, 4 * H_M), 0)

    def body(c, carry):
        r0 = pl.multiple_of(c * BLOCK, BLOCK)
        g = g_ref[0, pl.ds(r0, BLOCK), :][:, :4 * H_M] + bias_ref[...]
        valid = (rloc + r0) >= PAD
        lsig = -(jnp.maximum(-g, 0.0) + jnp.log1p(jnp.exp(-jnp.abs(g))))
        lf = jnp.where(valid, lsig, 0.0)
        cum = _split_dot(tril, lf)
        rcum = _split_dot(triu, lf)
        li = jnp.where(valid, g, -jnp.inf)
        out = jnp.where(typ == 1, cum, jnp.where(typ == 3, rcum, li))
        o_ref[0, pl.ds(r0, BLOCK), :] = out
        return carry

    lax.fori_loop(0, NBLK, body, 0)


def gate_prep(mg3, gate_bias):
    return pl.pallas_call(
        _gate_kernel,
        grid=(BATCH,),
        in_specs=[
            pl.BlockSpec((1, LP, 128), lambda b: (b, 0, 0)),
            pl.BlockSpec((1, 4 * H_M), lambda b: (0, 0)),
        ],
        out_specs=pl.BlockSpec((1, LP, 4 * H_M), lambda b: (b, 0, 0)),
        out_shape=jax.ShapeDtypeStruct((BATCH, LP, 4 * H_M), F32),
        compiler_params=_cparams(1),
        name="gate_prep",
    )(mg3, gate_bias.reshape(1, 4 * H_M))


MLSTM_HP = 2
MLSTM_MID = NBLK // 2


def _mlstm_kernel(qt_ref, k_ref, vt_ref, ot_ref, gr_ref, y_ref,
                  hs_ref, c_ref, n_ref, m_ref):
    c_ref[...] = jnp.zeros_like(c_ref)
    n_ref[...] = jnp.zeros_like(n_ref)
    m_ref[...] = jnp.zeros_like(m_ref)
    si = lax.broadcasted_iota(jnp.int32, (BLOCK, BLOCK), 0)
    ti = lax.broadcasted_iota(jnp.int32, (BLOCK, BLOCK), 1)
    mask_f = si <= ti
    mask_b = si >= ti

    def chain(c, hl, bwd, final):
        idx = 2 * hl + bwd
        r0 = c * BLOCK if isinstance(c, int) else pl.multiple_of(c * BLOCK, BLOCK)
        t_sl = pl.ds(r0, BLOCK)
        qt = qt_ref[0, hl * DK_M:(hl + 1) * DK_M, t_sl]
        k = k_ref[0, t_sl, hl * DK_M:(hl + 1) * DK_M]
        feat = slice(hl * DV_M, (hl + 1) * DV_M)
        vt = vt_ref[0, feat, t_sl]
        gr = gr_ref[0, hl, :, t_sl]
        li_r, b_r = gr[2 * bwd:2 * bwd + 1, :], gr[2 * bwd + 1:2 * bwd + 2, :]
        a_c = jnp.transpose(jnp.broadcast_to(li_r - b_r, (BLOCK, BLOCK)))
        b_end = b_r[:, 0:1] if bwd else b_r[:, BLOCK - 1:BLOCK]
        m_prev = m_ref[idx][:, 0:1]
        ct = c_ref[idx]
        nst = n_ref[idx]
        dmat = jnp.where(mask_b if bwd else mask_f, a_c + b_r, -jnp.inf)
        inter = b_r + m_prev
        m_t = jnp.maximum(inter, jnp.max(dmat, axis=0, keepdims=True))
        w_inter = jnp.exp(inter - m_t)
        st = jnp.dot(k, qt, preferred_element_type=F32) * jnp.exp(dmat - m_t)
        num = (w_inter * jnp.dot(ct.astype(BF16), qt, preferred_element_type=F32)
               + jnp.dot(vt, st.astype(BF16), preferred_element_type=F32))
        nq = jnp.dot(nst.astype(BF16), qt, preferred_element_type=F32)
        den = w_inter * nq + jnp.sum(st, axis=0, keepdims=True)
        h = num * (1.0 / jnp.maximum(jnp.abs(den), jnp.exp(-m_t)))
        if final:
            og = jax.nn.sigmoid(ot_ref[0, feat, t_sl].astype(F32))
            y_ref[0, feat, pl.ds(r0 - BLOCK, BLOCK)] = (og * (hs_ref[feat, t_sl] + h)).astype(BF16)
        else:
            hs_ref[feat, t_sl] = h
        ldec = b_end - b_r + li_r
        m_new = jnp.maximum(b_end + m_prev, jnp.max(ldec, axis=1, keepdims=True))
        w_c = jnp.exp(b_end + m_prev - m_new)
        w_s = jnp.exp(ldec - m_new)
        wvt = (vt.astype(F32) * w_s).astype(BF16)
        c_ref[idx] = w_c * ct + jnp.dot(wvt, k, preferred_element_type=F32)
        n_ref[idx] = w_c * nst + jnp.dot(w_s.astype(BF16), k, preferred_element_type=F32)
        m_ref[idx] = jnp.broadcast_to(m_new, (1, BLOCK))

    def first_half(i, carry):
        for hl in range(MLSTM_HP):
            chain(i, hl, 0, False)
            chain(NBLK - 1 - i, hl, 1, False)
        return carry

    def second_half(i, carry):
        for hl in range(MLSTM_HP):
            chain(i, hl, 0, True)
            chain(NBLK - 1 - i, hl, 1, True)
        return carry

    lax.fori_loop(0, MLSTM_MID, first_half, 0)
    for hl in range(MLSTM_HP):
        chain(MLSTM_MID, hl, 0, False)
        chain(MLSTM_MID, hl, 1, True)
    lax.fori_loop(MLSTM_MID + 1, NBLK - 1, second_half, 0)
    for hl in range(MLSTM_HP):
        chain(NBLK - 1, hl, 0, True)


def mlstm(q_t, qk_m, v_t, o_t, grow):
    hp = MLSTM_HP
    kw, vw = hp * DK_M, hp * DV_M
    return pl.pallas_call(
        _mlstm_kernel,
        grid=(BATCH, H_M // hp),
        in_specs=[
            pl.BlockSpec((1, kw, LP), lambda b, g: (b, g, 0)),
            pl.BlockSpec((1, LP, kw), lambda b, g: (b, 0, (H_M * DK_M) // kw + g)),
            pl.BlockSpec((1, vw, LP), lambda b, g: (b, g, 0)),
            pl.BlockSpec((1, vw, LP), lambda b, g: (b, g, 0)),
            pl.BlockSpec((1, hp, 4, LP), lambda b, g: (b, g, 0, 0)),
        ],
        out_specs=pl.BlockSpec((1, vw, SEQ), lambda b, g: (b, g, 0)),
        out_shape=jax.ShapeDtypeStruct((BATCH, H_M * DV_M, SEQ), BF16),
        scratch_shapes=[
            pltpu.VMEM((vw, LP), F32),
            pltpu.VMEM((2 * hp, DV_M, DK_M), F32),
            pltpu.VMEM((2 * hp, 1, DK_M), F32),
            pltpu.VMEM((2 * hp, 1, BLOCK), F32),
        ],
        compiler_params=_cparams(2),
        name="mlstm",
    )(q_t, qk_m, v_t, o_t, grow)


LOG2E = 1.4426950408889634
ATT_QB = 2
ATT_TQ = ATT_QB * BLOCK
ATT_BAND = (ATT_QB + 2) * BLOCK
ATT_GROUPS = (6, 6, 6, 6, 5)
assert BLOCK >= MAX_DISTANCE and ATT_BAND + sum(ATT_GROUPS) * BLOCK == LP


def _attn_kernel(c_ref, *refs):
    q_refs = refs[:ATT_QB]
    k1_ref, v1_ref, tab_ref, lam_ref, sg_ref, o_ref, s_ref, k_ref, v_ref = refs[ATT_QB:]
    h = pl.program_id(1)
    qb = ATT_QB * pl.program_id(2) + 1

    @pl.when(pl.program_id(2) == 0)
    def _():
        for rep in range(2):
            k_ref[0, rep * LP:(rep + 1) * LP, :] = k1_ref[0]
            v_ref[0, rep * LP:(rep + 1) * LP, :] = v1_ref[0]

    q = jnp.concatenate([r[0] for r in q_refs], axis=0)
    scale = DK_DA ** -0.5 * LOG2E
    c_neg = c_ref[h, 0]
    c_pos = c_ref[h, 1]
    lp = lam_ref[...]
    lam = (jnp.exp(jnp.sum(lp[0:1] * lp[1:2], axis=1, keepdims=True))
           - jnp.exp(jnp.sum(lp[2:3] * lp[3:4], axis=1, keepdims=True)) + LAMBDA_INIT)

    groups = [((qb - 1) * BLOCK, ATT_BAND, 0)]
    col = ATT_BAND
    for nblk in ATT_GROUPS:
        groups.append(((qb - 1) * BLOCK + col, nblk * BLOCK, col))
        col += nblk * BLOCK

    def lane_fold(acc, t, op):
        for j in range(t.shape[1] // BLOCK):
            piece = t[:, j * BLOCK:(j + 1) * BLOCK]
            acc = piece if acc is None else op(acc, piece)
        return acc

    mx = [None, None]
    for gi, (koff, width, col0) in enumerate(groups):
        koff = pl.multiple_of(koff, BLOCK)
        if gi == 0:
            bias = tab_ref[0, 0]
        else:
            kpos = koff + lax.broadcasted_iota(jnp.int32, (1, width), 1)
            bias = jnp.where(kpos < LP, c_pos, jnp.where(kpos < LP + PAD, NEG_INF, c_neg))
        for m in range(2):
            kk = k_ref[0, pl.ds(koff, width), m * DK_DA:(m + 1) * DK_DA]
            s = lax.dot_general(q[:, m * DK_DA:(m + 1) * DK_DA], kk, (((1,), (1,)), ((), ())),
                                preferred_element_type=F32) * scale + bias
            s_ref[m, :, col0:col0 + width] = s
            mx[m] = lane_fold(mx[m], s, jnp.maximum)
    row_max = [jnp.max(mx[m], axis=1, keepdims=True) for m in range(2)]

    lsum = [None, None]
    acc = [None, None]
    for koff, width, col0 in groups:
        koff = pl.multiple_of(koff, BLOCK)
        vv = v_ref[0, pl.ds(koff, width), :]
        for m in range(2):
            p = jnp.exp2(s_ref[m, :, col0:col0 + width] - row_max[m])
            lsum[m] = lane_fold(lsum[m], p, jnp.add)
            pv = jnp.dot(p.astype(BF16), vv, preferred_element_type=F32)
            acc[m] = pv if acc[m] is None else acc[m] + pv
    l1 = jnp.sum(lsum[0], axis=1, keepdims=True)
    l2 = jnp.sum(lsum[1], axis=1, keepdims=True)
    o = acc[0] / l1 - lam * (acc[1] / l2)
    o_ref[0] = (_rms(o, sg_ref[...]) * (1.0 - LAMBDA_INIT)).astype(BF16)


def diff_attention(consts, proj3, tab, lam_params, subln):
    nq = SEQ // ATT_TQ
    kblk0 = COL_DA_K // (2 * DK_DA)
    vblk0 = COL_DA_V // DV_DA

    def tab_map(b, h, i):
        case = jnp.where(i == 0, 0, jnp.where(i == nq - 1, 2, 1))
        return (h, case, 0, 0)

    return pl.pallas_call(
        _attn_kernel,
        grid=(BATCH, H_DA, nq),
        in_specs=[
            pl.BlockSpec(memory_space=pltpu.SMEM),
            *[pl.BlockSpec((1, BLOCK, 2 * DK_DA),
                           functools.partial(lambda b, h, i, r: (b, ATT_QB * i + 1 + r, h), r=r))
              for r in range(ATT_QB)],
            pl.BlockSpec((1, LP, 2 * DK_DA), lambda b, h, i: (b, 0, kblk0 + h)),
            pl.BlockSpec((1, LP, DV_DA), lambda b, h, i: (b, 0, vblk0 + h)),
            pl.BlockSpec((1, 1, ATT_TQ, ATT_BAND), tab_map),
            pl.BlockSpec((4, DK_DA), lambda b, h, i: (0, 0)),
            pl.BlockSpec((1, DV_DA), lambda b, h, i: (0, 0)),
        ],
        out_specs=pl.BlockSpec((1, ATT_TQ, DV_DA), lambda b, h, i: (b, i, h)),
        out_shape=jax.ShapeDtypeStruct((BATCH, SEQ, H_DA * DV_DA), BF16),
        scratch_shapes=[pltpu.VMEM((2, ATT_TQ, LP), F32),
                        pltpu.VMEM((1, 2 * LP, 2 * DK_DA), BF16),
                        pltpu.VMEM((1, 2 * LP, DV_DA), BF16)],
        compiler_params=_cparams(3),
        name="diff_attn",
    )(consts, *([proj3] * (ATT_QB + 2)), tab, lam_params, subln.reshape(1, DV_DA))


def _bias_tables(rel_bias):
    rb = rel_bias.astype(F32)
    span = 1024
    assert span >= ATT_TQ + ATT_BAND
    rel = jnp.arange(span, dtype=jnp.int32) - span // 2
    nb = N_BUCKETS // 2
    max_exact = nb // 2
    n = jnp.abs(rel)
    nf = jnp.maximum(n, 1).astype(F32)
    large = max_exact + (jnp.log(nf / max_exact) / math.log(MAX_DISTANCE / max_exact)
                         * (nb - max_exact)).astype(jnp.int32)
    large = jnp.minimum(large, nb - 1)
    bucket = jnp.where(rel > 0, nb, 0) + jnp.where(n < max_exact, n, large)
    hit = bucket[None, :, None] == jnp.arange(N_BUCKETS, dtype=jnp.int32)
    by_rel = jnp.sum(jnp.where(hit, rb.T[:, None, :], 0.0), axis=-1)
    shifted = jnp.tile(by_rel, (1, ATT_TQ))[:, :ATT_TQ * (span - 1)].reshape(
        H_DA, ATT_TQ, span - 1)
    c0 = span // 2 - BLOCK
    gen = shifted[:, :, c0:c0 + ATT_BAND]
    c_neg = rb[nb - 1]
    c_pos = rb[N_BUCKETS - 1]
    jj = jnp.arange(ATT_BAND, dtype=jnp.int32)[None, None, :]
    first = jnp.where(jj < PAD, NEG_INF, gen)
    wrap0 = ATT_BAND - BLOCK
    wrapped = jnp.where(jj - wrap0 < PAD, NEG_INF, c_neg[:, None, None])
    last = jnp.where(jj >= wrap0, wrapped, gen)
    tab = jnp.stack([first, gen, last], axis=1)
    consts = jnp.stack([c_neg, c_pos], axis=1)
    return tab * LOG2E, consts * LOG2E


def _mix_kernel(ya_ref, ym_ref, wa_ref, wm_ref, ga_ref, gm_ref, o_ref, wa_bf, wm_bf):
    @pl.when(pl.program_id(1) == 0)
    def _():
        wa_bf[...] = wa_ref[...].astype(BF16)
        wm_bf[...] = wm_ref[...].astype(BF16)

    a = jnp.dot(ya_ref[...], wa_bf[...], preferred_element_type=F32)
    m = jnp.dot(ym_ref[...], wm_bf[...], preferred_element_type=F32)
    o_ref[...] = (ga_ref[...].astype(F32) * a + gm_ref[...].astype(F32) * m).astype(BF16)


def branch_mix(y_da, y_m, w_da, w_m, gate, *, tm=512, tn=1024):
    m, k = y_da.shape
    nj = D_MODEL // tn
    return pl.pallas_call(
        _mix_kernel,
        grid=(nj, m // tm),
        in_specs=[
            pl.BlockSpec((tm, k), lambda j, i: (i, 0)),
            pl.BlockSpec((tm, k), lambda j, i: (i, 0)),
            pl.BlockSpec((k, tn), lambda j, i: (0, j)),
            pl.BlockSpec((k, tn), lambda j, i: (0, j)),
            pl.BlockSpec((tm, tn), lambda j, i: (i, j)),
            pl.BlockSpec((tm, tn), lambda j, i: (i, nj + j)),
        ],
        out_specs=pl.BlockSpec((tm, tn), lambda j, i: (i, j)),
        out_shape=jax.ShapeDtypeStruct((m, D_MODEL), BF16),
        scratch_shapes=[pltpu.VMEM((k, tn), BF16), pltpu.VMEM((k, tn), BF16)],
        compiler_params=_cparams(2),
        name="branch_mix",
    )(y_da, y_m, w_da, w_m, gate, gate)


FFN_TM = 1024
HALF = D_MODEL // 2


def _pack_bf16_pairs(v):
    lo = lax.bitcast_convert_type(v[:, :HALF].astype(BF16).astype(F32), jnp.uint32)
    hi = lax.bitcast_convert_type(v[:, HALF:].astype(BF16).astype(F32), jnp.uint32)
    return (hi & jnp.uint32(0xFFFF0000)) | lax.shift_right_logical(lo, jnp.uint32(16))


def _unpack_bf16_pairs(w):
    lo = lax.bitcast_convert_type(lax.shift_left(w, jnp.uint32(16)), F32).astype(BF16)
    hi = lax.bitcast_convert_type(w & jnp.uint32(0xFFFF0000), F32).astype(BF16)
    return lo, hi


def _ffn_prep_kernel(h_ref, g_ref, wr_ref, br_ref, hn_ref, e_ref, w_ref, r_ref, cnt_ref, base_ref):
    @pl.when(pl.program_id(0) == 0)
    def _():
        base_ref[...] = jnp.zeros_like(base_ref)

    hn = _rms(h_ref[...], g_ref[...])
    hn_ref[...] = _pack_bf16_pairs(hn)
    logits = jnp.dot(hn.astype(BF16), wr_ref[...].astype(BF16),
                     preferred_element_type=F32) + br_ref[...]
    lane = lax.broadcasted_iota(jnp.int32, (FFN_TM, N_EXPERTS), 1)
    lane_o = lax.broadcasted_iota(jnp.int32, (FFN_TM, 128), 1)
    ti = lax.broadcasted_iota(jnp.int32, (FFN_TM, FFN_TM), 0)
    ui = lax.broadcasted_iota(jnp.int32, (FFN_TM, FFN_TM), 1)
    tril = jnp.where(ui <= ti, 1.0, 0.0).astype(BF16)
    e_out = jnp.zeros((FFN_TM, 128), jnp.int32)
    r_out = jnp.zeros((FFN_TM, 128), jnp.int32)
    l_out = jnp.full((FFN_TM, 128), -jnp.inf, F32)
    base = base_ref[...]
    l = logits
    for kk in range(TOP_K):
        mk = jnp.max(l, axis=1, keepdims=True)
        ik = jnp.min(jnp.where(l == mk, lane, N_EXPERTS), axis=1, keepdims=True)
        hit = lane == ik
        oh = jnp.where(hit, 1.0, 0.0)
        cum = jnp.dot(tril, oh.astype(BF16), preferred_element_type=F32)
        rank = jnp.sum(oh * (cum + base), axis=1, keepdims=True) - 1.0
        base = base + jnp.sum(oh, axis=0, keepdims=True)
        e_out = jnp.where(lane_o == kk, ik, e_out)
        r_out = jnp.where(lane_o == kk, rank.astype(jnp.int32), r_out)
        l_out = jnp.where(lane_o == kk, mk, l_out)
        l = jnp.where(hit, -jnp.inf, l)
    base_ref[...] = base
    cnt_ref[...] = base
    ex = jnp.exp(l_out - jnp.max(l_out, axis=1, keepdims=True))
    e_ref[...] = e_out
    r_ref[...] = r_out
    w_ref[...] = ex / jnp.sum(ex, axis=1, keepdims=True)


def ffn_prep(h2, gain, w_router, b_router):
    row = lambda i: (i, 0)
    fixed = lambda i: (0, 0)
    return pl.pallas_call(
        _ffn_prep_kernel,
        grid=(N_TOK // FFN_TM,),
        in_specs=[
            pl.BlockSpec((FFN_TM, D_MODEL), row),
            pl.BlockSpec((1, D_MODEL), fixed),
            pl.BlockSpec((D_MODEL, N_EXPERTS), fixed),
            pl.BlockSpec((1, N_EXPERTS), fixed),
        ],
        out_specs=[
            pl.BlockSpec((FFN_TM, HALF), row),
            pl.BlockSpec((FFN_TM, 128), row),
            pl.BlockSpec((FFN_TM, 128), row),
            pl.BlockSpec((FFN_TM, 128), row),
            pl.BlockSpec((1, N_EXPERTS), fixed),
        ],
        out_shape=[
            jax.ShapeDtypeStruct((N_TOK, HALF), jnp.uint32),
            jax.ShapeDtypeStruct((N_TOK, 128), jnp.int32),
            jax.ShapeDtypeStruct((N_TOK, 128), F32),
            jax.ShapeDtypeStruct((N_TOK, 128), jnp.int32),
            jax.ShapeDtypeStruct((1, N_EXPERTS), F32),
        ],
        scratch_shapes=[pltpu.VMEM((1, N_EXPERTS), F32)],
        compiler_params=_cparams(1),
        name="ffn_prep",
    )(h2, gain.reshape(1, D_MODEL), w_router, b_router.reshape(1, N_EXPERTS))


SEG_ALIGN = 128
MOE_R = N_ASSIGN + N_EXPERTS * SEG_ALIGN


def _plan(counts_f, top_e, rank):
    counts = counts_f[0].astype(jnp.int32)
    seg_rows = (counts + SEG_ALIGN - 1) // SEG_ALIGN * SEG_ALIGN
    seg_start = jnp.cumsum(seg_rows) - seg_rows
    eq = top_e[:, :, None] == jnp.arange(N_EXPERTS, dtype=jnp.int32)[None, None, :]
    dest = jnp.sum(jnp.where(eq, seg_start[None, None, :], 0), axis=-1) + rank
    used = jnp.sum(seg_rows)
    slack = jnp.stack([used, (MOE_R - used) // SEG_ALIGN])
    return (dest.reshape(N_ASSIGN).astype(jnp.int32), seg_start.astype(jnp.int32),
            seg_rows.astype(jnp.int32), (seg_start + counts).astype(jnp.int32),
            (seg_rows - counts).astype(jnp.int32), slack.astype(jnp.int32))


def _zero_slack(slack_ref, zero_block, dst_rows, sem):
    zero_block[...] = jnp.zeros_like(zero_block)

    def copy(j):
        r0 = pl.multiple_of(slack_ref[0] + j * SEG_ALIGN, SEG_ALIGN)
        return pltpu.make_async_copy(zero_block, dst_rows(pl.ds(r0, SEG_ALIGN)), sem)

    def start(j, carry):
        copy(j).start()
        return carry

    def wait(j, carry):
        copy(j).wait()
        return carry

    lax.fori_loop(0, slack_ref[1], start, 0)
    lax.fori_loop(0, slack_ref[1], wait, 0)


DISP_TOK = 256


def _dispatch_kernel(dest_ref, pad0_ref, padn_ref, slack_ref, hn_ref, xs_ref,
                     zrow_ref, zblk_ref, sem, zsem):
    i = pl.program_id(0)

    @pl.when(i == 0)
    def _():
        zrow_ref[...] = jnp.zeros_like(zrow_ref)
        _zero_slack(slack_ref, zblk_ref, lambda rows: xs_ref.at[rows], zsem.at[0])

        def expert(e, carry):
            p0 = pad0_ref[e]
            pn = padn_ref[e]

            def zstart(r, c2):
                pltpu.make_async_copy(zrow_ref, xs_ref.at[pl.ds(p0 + r, 1)], zsem.at[0]).start()
                return c2

            def zwait(r, c2):
                pltpu.make_async_copy(zrow_ref, xs_ref.at[pl.ds(p0, 1)], zsem.at[0]).wait()
                return c2

            lax.fori_loop(0, pn, zstart, 0)
            lax.fori_loop(0, pn, zwait, 0)
            return carry

        lax.fori_loop(0, N_EXPERTS, expert, 0)

    def tok(t, carry):
        a = (i * DISP_TOK + t) * TOP_K
        for kk in range(TOP_K):
            pltpu.make_async_copy(hn_ref.at[pl.ds(t, 1)], xs_ref.at[pl.ds(dest_ref[a + kk], 1)],
                                  sem.at[0]).start(priority=kk % 2)
        return carry

    lax.fori_loop(0, DISP_TOK, tok, 0, unroll=4)
    for kk in range(TOP_K):
        pltpu.make_async_copy(hn_ref, xs_ref.at[pl.ds(0, DISP_TOK)], sem.at[0]).wait()


def dispatch(dest, pad0, padn, slack, hn_packed):
    grid_spec = pltpu.PrefetchScalarGridSpec(
        num_scalar_prefetch=4,
        grid=(N_TOK // DISP_TOK,),
        in_specs=[pl.BlockSpec((DISP_TOK, HALF), lambda i, d, p0, pn, z: (i, 0))],
        out_specs=pl.BlockSpec(memory_space=pl.ANY),
        scratch_shapes=[pltpu.VMEM((1, HALF), jnp.uint32),
                        pltpu.VMEM((SEG_ALIGN, HALF), jnp.uint32),
                        pltpu.SemaphoreType.DMA((1,)), pltpu.SemaphoreType.DMA((1,))],
    )
    return pl.pallas_call(
        _dispatch_kernel,
        grid_spec=grid_spec,
        out_shape=jax.ShapeDtypeStruct((MOE_R, HALF), jnp.uint32),
        compiler_params=_cparams(1),
        name="moe_dispatch",
    )(dest, pad0, padn, slack, hn_packed)


MOE_CH = 2 * SEG_ALIGN
MOE_TF = 1024
MOE_NF = D_FF // MOE_TF


class _CopyGroup:
    def __init__(self, copies):
        self.copies = copies

    def start(self, priority=0):
        for cp in self.copies:
            cp.start(priority=priority)

    def wait(self):
        for cp in self.copies:
            cp.wait()


W_PIECES = 8
W_PER_CHUNK = 2


N_STATE = 4


def _stream_rows(step, n_steps, start, rows, next_start, next_rows, state,
                 make_in, make_out, make_tail_in, make_tail_out,
                 compute_chunk, compute_tail, before_first_wait, next_weight_piece):
    has_next = step + 1 < n_steps

    def request_weights(first, count):
        def one(p, carry):
            @pl.when(jnp.logical_and(has_next, p < W_PIECES))
            def _():
                next_weight_piece(p).start()
            return carry

        lax.fori_loop(first, first + count, one, 0)

    @pl.when(step == 0)
    def _():
        for j in range(N_STATE):
            state[j] = 0

    n_ch = lax.shift_right_logical(rows, MOE_CH.bit_length() - 1)
    tail = rows - n_ch * MOE_CH
    tail_row = start + n_ch * MOE_CH
    g0 = state[0]
    feeds_next = jnp.logical_and(step + 1 < n_steps, next_rows >= MOE_CH)

    def chunk_row(c):
        return start + c * MOE_CH

    @pl.when(tail > 0)
    def _():
        make_tail_in(tail_row).start(priority=1)

    @pl.when(jnp.logical_and(n_ch > 0, state[3] == 0))
    def _():
        make_in(start, lax.rem(g0, 2)).start(priority=1)

    before_first_wait()

    @pl.when(tail > 0)
    def _():
        make_tail_in(tail_row).wait()
        compute_tail()
        make_tail_out(tail_row).start(priority=1)

    def body(c, carry):
        slot = lax.rem(g0 + c, 2)
        make_in(chunk_row(c), slot).wait()

        @pl.when(c + 1 < n_ch)
        def _():
            make_in(chunk_row(c + 1), 1 - slot).start(priority=1)

        @pl.when(jnp.logical_and(c + 1 == n_ch, feeds_next))
        def _():
            make_in(next_start, 1 - slot).start(priority=1)

        request_weights(c * W_PER_CHUNK, W_PER_CHUNK)

        @pl.when(state[1 + slot] == 1)
        def _():
            make_out(chunk_row(c), slot).wait()

        compute_chunk(slot)
        make_out(chunk_row(c), slot).start(priority=1)
        state[1 + slot] = 1
        return carry

    lax.fori_loop(0, n_ch, body, 0)
    request_weights(n_ch * W_PER_CHUNK, W_PIECES)
    state[0] = g0 + n_ch
    state[3] = jnp.where(jnp.logical_and(n_ch > 0, feeds_next), 1, 0)

    @pl.when(tail > 0)
    def _():
        make_tail_out(tail_row).wait()

    @pl.when(step == n_steps - 1)
    def _():
        for slot in range(2):
            @pl.when(state[1 + slot] == 1)
            def _():
                make_out(0, slot).wait()
                state[1 + slot] = 0


def _rows_at(row0, n):
    return pl.ds(row0 if isinstance(row0, int) else pl.multiple_of(row0, SEG_ALIGN), n)


def _moe_up_kernel(seg_ref, rows_ref, slack_ref, xs_ref, w_hbm, bg_ref, bl_ref, act_ref,
                   xbuf, obuf, xtail, otail, wbuf, wg_bf, wl_bf, state, isem, osem, tsem, wsem):
    f = pl.program_id(0)
    e = pl.program_id(1)
    start = seg_ref[e]
    rows = rows_ref[e]
    e_next = lax.rem(e + 1, N_EXPERTS)
    f_next = jnp.where(e == N_EXPERTS - 1, f + 1, f)
    step = f * N_EXPERTS + e
    n_steps = MOE_NF * N_EXPERTS
    wslot = lax.rem(step, 2)

    def weight_piece(expert, ftile, slot, p):
        per_half = W_PIECES // 2
        band = D_MODEL // per_half
        t = p // per_half if isinstance(p, int) else lax.shift_right_logical(
            p, per_half.bit_length() - 1)
        r0 = (p - t * per_half) * band
        r0 = r0 if isinstance(r0, int) else pl.multiple_of(r0, band)
        col0 = pl.multiple_of((t * MOE_NF + ftile) * MOE_TF, MOE_TF)
        return pltpu.make_async_copy(
            w_hbm.at[expert, pl.ds(r0, band), pl.ds(col0, MOE_TF)],
            wbuf.at[slot, t, pl.ds(r0, band)], wsem.at[slot])

    @pl.when(step == 0)
    def _():
        for p in range(W_PIECES):
            weight_piece(e, f, wslot, p).start()

    def make_in(r0, slot):
        return pltpu.make_async_copy(xs_ref.at[_rows_at(r0, MOE_CH)], xbuf.at[slot], isem.at[slot])

    def make_out(r0, slot):
        return pltpu.make_async_copy(obuf.at[slot], act_ref.at[f, _rows_at(r0, MOE_CH)],
                                     osem.at[slot])

    def make_tail_in(r0):
        return pltpu.make_async_copy(xs_ref.at[_rows_at(r0, SEG_ALIGN)], xtail, tsem.at[0])

    def make_tail_out(r0):
        return pltpu.make_async_copy(otail, act_ref.at[f, _rows_at(r0, SEG_ALIGN)], tsem.at[1])

    def cast_weights():
        for p in range(W_PIECES):
            weight_piece(e, f, wslot, p).wait()

        @pl.when(rows > 0)
        def _():
            wg_bf[...] = wbuf[wslot, 0].astype(BF16)
            wl_bf[...] = wbuf[wslot, 1].astype(BF16)

    def expert_mlp(words):
        lo, hi = _unpack_bf16_pairs(words)
        glu = (jnp.dot(lo, wg_bf[:HALF, :], preferred_element_type=F32)
               + jnp.dot(hi, wg_bf[HALF:, :], preferred_element_type=F32) + bg_ref[0])
        lin = (jnp.dot(lo, wl_bf[:HALF, :], preferred_element_type=F32)
               + jnp.dot(hi, wl_bf[HALF:, :], preferred_element_type=F32) + bl_ref[0])
        glu = jnp.minimum(glu, SWIGLU_LIMIT)
        lin = jnp.clip(lin, -SWIGLU_LIMIT, SWIGLU_LIMIT)
        return (glu * jax.nn.sigmoid(SWIGLU_ALPHA * glu) * (lin + 1.0)).astype(BF16)

    def compute_chunk(slot):
        obuf[slot] = expert_mlp(xbuf[slot])

    def compute_tail():
        otail[...] = expert_mlp(xtail[...])

    _stream_rows(step, n_steps, start, rows,
                 seg_ref[e_next], rows_ref[e_next], state,
                 make_in, make_out, make_tail_in, make_tail_out,
                 compute_chunk, compute_tail, cast_weights,
                 lambda p: weight_piece(e_next, f_next, 1 - wslot, p))

    @pl.when(e == N_EXPERTS - 1)
    def _():
        _zero_slack(slack_ref, otail, lambda rr: act_ref.at[f, rr], tsem.at[1])


def moe_up(seg_start, seg_rows, slack, xs, w1, b1):
    grid_spec = pltpu.PrefetchScalarGridSpec(
        num_scalar_prefetch=3,
        grid=(MOE_NF, N_EXPERTS),
        in_specs=[
            pl.BlockSpec(memory_space=pl.ANY),
            pl.BlockSpec(memory_space=pl.ANY),
            pl.BlockSpec((1, 1, MOE_TF), lambda f, e, s, r, z: (e, 0, f)),
            pl.BlockSpec((1, 1, MOE_TF), lambda f, e, s, r, z: (e, 0, MOE_NF + f)),
        ],
        out_specs=pl.BlockSpec(memory_space=pl.ANY),
        scratch_shapes=[
            pltpu.VMEM((2, MOE_CH, HALF), jnp.uint32),
            pltpu.VMEM((2, MOE_CH, MOE_TF), BF16),
            pltpu.VMEM((SEG_ALIGN, HALF), jnp.uint32),
            pltpu.VMEM((SEG_ALIGN, MOE_TF), BF16),
            pltpu.VMEM((2, 2, D_MODEL, MOE_TF), F32),
            pltpu.VMEM((D_MODEL, MOE_TF), BF16),
            pltpu.VMEM((D_MODEL, MOE_TF), BF16),
            pltpu.SMEM((N_STATE,), jnp.int32),
            pltpu.SemaphoreType.DMA((2,)),
            pltpu.SemaphoreType.DMA((2,)),
            pltpu.SemaphoreType.DMA((2,)),
            pltpu.SemaphoreType.DMA((2,)),
        ],
    )
    b13 = b1.reshape(N_EXPERTS, 1, 2 * D_FF)
    return pl.pallas_call(
        _moe_up_kernel,
        grid_spec=grid_spec,
        out_shape=jax.ShapeDtypeStruct((MOE_NF, MOE_R, MOE_TF), BF16),
        compiler_params=pltpu.CompilerParams(
            dimension_semantics=("arbitrary", "arbitrary"), vmem_limit_bytes=MOE_VMEM_LIMIT),
        name="moe_up",
    )(seg_start, seg_rows, slack, xs, w1, b13, b13)


def _moe_down_kernel(seg_ref, rows_ref, slack_ref, act_ref, w_hbm, b_ref, y_ref,
                     xbuf, obuf, xtail, otail, wbuf, w_bf, state, isem, osem, tsem, wsem):
    e = pl.program_id(0)
    start = seg_ref[e]
    rows = rows_ref[e]
    e_next = lax.rem(e + 1, N_EXPERTS)
    wslot = lax.rem(e, 2)

    def weight_piece(expert, slot, p):
        band = D_FF // W_PIECES
        r0 = p * band if isinstance(p, int) else pl.multiple_of(p * band, band)
        return pltpu.make_async_copy(w_hbm.at[expert, pl.ds(r0, band)],
                                     wbuf.at[slot, pl.ds(r0, band)], wsem.at[slot])

    @pl.when(e == 0)
    def _():
        for p in range(W_PIECES):
            weight_piece(e, wslot, p).start()

    def make_in(r0, slot):
        return _CopyGroup([
            pltpu.make_async_copy(act_ref.at[j, _rows_at(r0, MOE_CH)],
                                  xbuf.at[slot, :, pl.ds(j * MOE_TF, MOE_TF)], isem.at[slot])
            for j in range(MOE_NF)])

    def make_out(r0, slot):
        return pltpu.make_async_copy(obuf.at[slot], y_ref.at[_rows_at(r0, MOE_CH)], osem.at[slot])

    def make_tail_in(r0):
        return _CopyGroup([
            pltpu.make_async_copy(act_ref.at[j, _rows_at(r0, SEG_ALIGN)],
                                  xtail.at[:, pl.ds(j * MOE_TF, MOE_TF)], tsem.at[0])
            for j in range(MOE_NF)])

    def make_tail_out(r0):
        return pltpu.make_async_copy(otail, y_ref.at[_rows_at(r0, SEG_ALIGN)], tsem.at[1])

    def cast_weights():
        for p in range(W_PIECES):
            weight_piece(e, wslot, p).wait()

        @pl.when(rows > 0)
        def _():
            w_bf[...] = wbuf[wslot].astype(BF16)

    def expert_out(a):
        return _pack_bf16_pairs(jnp.dot(a, w_bf[...], preferred_element_type=F32) + b_ref[0])

    def compute_chunk(slot):
        obuf[slot] = expert_out(xbuf[slot])

    def compute_tail():
        otail[...] = expert_out(xtail[...])

    _stream_rows(e, N_EXPERTS, start, rows, seg_ref[e_next], rows_ref[e_next], state,
                 make_in, make_out, make_tail_in, make_tail_out,
                 compute_chunk, compute_tail, cast_weights,
                 lambda p: weight_piece(e_next, 1 - wslot, p))

    @pl.when(e == N_EXPERTS - 1)
    def _():
        _zero_slack(slack_ref, otail, lambda rr: y_ref.at[rr], tsem.at[1])


def moe_down(seg_start, seg_rows, slack, act, w2, b2):
    grid_spec = pltpu.PrefetchScalarGridSpec(
        num_scalar_prefetch=3,
        grid=(N_EXPERTS,),
        in_specs=[
            pl.BlockSpec(memory_space=pl.ANY),
            pl.BlockSpec(memory_space=pl.ANY),
            pl.BlockSpec((1, 1, D_MODEL), lambda e, s, r, z: (e, 0, 0)),
        ],
        out_specs=pl.BlockSpec(memory_space=pl.ANY),
        scratch_shapes=[
            pltpu.VMEM((2, MOE_CH, D_FF), BF16),
            pltpu.VMEM((2, MOE_CH, HALF), jnp.uint32),
            pltpu.VMEM((SEG_ALIGN, D_FF), BF16),
            pltpu.VMEM((SEG_ALIGN, HALF), jnp.uint32),
            pltpu.VMEM((2, D_FF, D_MODEL), F32),
            pltpu.VMEM((D_FF, D_MODEL), BF16),
            pltpu.SMEM((N_STATE,), jnp.int32),
            pltpu.SemaphoreType.DMA((2,)),
            pltpu.SemaphoreType.DMA((2,)),
            pltpu.SemaphoreType.DMA((2,)),
            pltpu.SemaphoreType.DMA((2,)),
        ],
    )
    return pl.pallas_call(
        _moe_down_kernel,
        grid_spec=grid_spec,
        out_shape=jax.ShapeDtypeStruct((MOE_R, HALF), jnp.uint32),
        compiler_params=pltpu.CompilerParams(
            dimension_semantics=("arbitrary",), vmem_limit_bytes=MOE_VMEM_LIMIT),
        name="moe_down",
    )(seg_start, seg_rows, slack, act, w2, b2.reshape(N_EXPERTS, 1, D_MODEL))


COMB_TM = 128
COMB_NT = N_TOK // COMB_TM


def _combine_kernel(dest_ref, y_ref, w_ref, h_ref, g_ref, o_ref, buf, sem):
    i = pl.program_id(0)

    def fetch(tile, slot):
        def tok(t, carry):
            a = (tile * COMB_TM + t) * TOP_K
            for kk in range(TOP_K):
                pltpu.make_async_copy(y_ref.at[pl.ds(dest_ref[a + kk], 1)],
                                      buf.at[slot, kk, pl.ds(t, 1)], sem.at[slot]).start(
                                          priority=kk % 2)
            return carry

        lax.fori_loop(0, COMB_TM, tok, 0, unroll=4)

    @pl.when(i == 0)
    def _():
        fetch(0, 0)

    slot = lax.rem(i, 2)

    @pl.when(i + 1 < COMB_NT)
    def _():
        fetch(i + 1, 1 - slot)

    for kk in range(TOP_K):
        pltpu.make_async_copy(y_ref.at[pl.ds(0, COMB_TM)], buf.at[slot, kk], sem.at[slot]).wait()
    w = w_ref[...]
    lo = h_ref[:, :HALF]
    hi = h_ref[:, HALF:]
    for kk in range(TOP_K):
        words = buf[slot, kk]
        wk = w[:, kk:kk + 1]
        lo = lo + wk * lax.bitcast_convert_type(lax.shift_left(words, jnp.uint32(16)), F32)
        hi = hi + wk * lax.bitcast_convert_type(words & jnp.uint32(0xFFFF0000), F32)
    ms = (jnp.sum(lo * lo, axis=-1, keepdims=True)
          + jnp.sum(hi * hi, axis=-1, keepdims=True)) * (1.0 / D_MODEL)
    inv = lax.rsqrt(ms + EPS)
    o_ref[:, :HALF] = lo * inv * g_ref[:, :HALF]
    o_ref[:, HALF:] = hi * inv * g_ref[:, HALF:]


def combine(dest, y, weight, h2, gain):
    grid_spec = pltpu.PrefetchScalarGridSpec(
        num_scalar_prefetch=1,
        grid=(COMB_NT,),
        in_specs=[
            pl.BlockSpec(memory_space=pl.ANY),
            pl.BlockSpec((COMB_TM, 128), lambda i, d: (i, 0)),
            pl.BlockSpec((COMB_TM, D_MODEL), lambda i, d: (i, 0)),
            pl.BlockSpec((1, D_MODEL), lambda i, d: (0, 0)),
        ],
        out_specs=pl.BlockSpec((COMB_TM, D_MODEL), lambda i, d: (i, 0)),
        scratch_shapes=[pltpu.VMEM((2, TOP_K, COMB_TM, HALF), jnp.uint32),
                        pltpu.SemaphoreType.DMA((2,))],
    )
    return pl.pallas_call(
        _combine_kernel,
        grid_spec=grid_spec,
        out_shape=jax.ShapeDtypeStruct((N_TOK, D_MODEL), F32),
        compiler_params=_cparams(1),
        name="moe_combine",
    )(dest, y, weight, h2, gain.reshape(1, D_MODEL))


def kernel(x, meta_tokens, rel_bias, norm_mix, w_in, conv_w, gate_bias_m, lambda_params, subln_da,
           w_branch_da, w_branch_m, w_gate, b_gate, w_out, norm_ffn, w_router, b_router,
           w1, b1, w2, b2, norm_final):
    layer = 0
    xn_pad, xn_real = norm_in(x, meta_tokens, norm_mix[layer])
    xn_pad2 = xn_pad.reshape(BATCH * LP, D_MODEL)
    w_in_t = w_in[layer].T
    proj = matmul(xn_pad2, w_in_t, n_cols=PROJ_COLS, tm=768, tn=1024, w_transposed=True,
                  name="proj_in")
    proj3 = proj.reshape(BATCH, LP, PROJ_COLS)
    w_g_t = jnp.pad(w_in_t[COL_M_G:], ((0, 128 - 4 * H_M), (0, 0)))
    mg = matmul(xn_pad2, w_g_t, n_cols=128, tm=768, tn=128, out_dtype=F32, w_transposed=True,
                name="proj_gates")
    gate = matmul(xn_real.reshape(N_TOK, D_MODEL), w_gate[layer], n_cols=2 * D_MODEL,
                  tm=1024, tn=1024, bias=b_gate[layer], act="sigmoid", name="mix_gate")

    tab, consts = _bias_tables(rel_bias)
    y_da = diff_attention(consts, proj3, tab, lambda_params[layer], subln_da[layer])

    qk_m = conv_qk(proj3, conv_w[layer])
    gp = gate_prep(mg.reshape(BATCH, LP, 128), gate_bias_m[layer])
    gp4 = gp.reshape(BATCH, LP, 4, H_M)
    grow = jnp.transpose(gp4, (0, 3, 2, 1))
    swap = lambda t: jnp.swapaxes(t, 1, 2)
    y_m = swap(mlstm(swap(qk_m[:, :, :H_M * DK_M]), qk_m,
                     swap(proj3[:, :, COL_M_V:COL_M_V + H_M * DV_M]),
                     swap(proj3[:, :, COL_M_O:COL_M_O + H_M * DV_M]), grow))

    mixed = branch_mix(y_da.reshape(N_TOK, H_DA * DV_DA), y_m.reshape(N_TOK, H_M * DV_M),
                       w_branch_da[layer], w_branch_m[layer], gate)
    h2 = matmul(mixed, w_out[layer], n_cols=D_MODEL, tm=1024, tn=1024,
                res=x.reshape(N_TOK, D_MODEL), out_dtype=F32, name="out_proj")

    hn_packed, top_e, weight, rank, counts = ffn_prep(
        h2, norm_ffn[layer], w_router[layer], b_router[layer])
    dest, seg_start, seg_rows, pad0, padn, slack = _plan(
        counts, top_e[:, :TOP_K], rank[:, :TOP_K])
    xs = dispatch(dest, pad0, padn, slack, hn_packed)
    act = moe_up(seg_start, seg_rows, slack, xs, w1[layer], b1[layer])
    y = moe_down(seg_start, seg_rows, slack, act, w2[layer], b2[layer])
    out = combine(dest, y, weight, h2, norm_final)
    return out.reshape(BATCH, SEQ, D_MODEL)
```

```python
import functools
import math

import jax
import jax.numpy as jnp
from jax import lax
from jax.experimental import pallas as pl
from jax.experimental.pallas import tpu as pltpu

F32 = jnp.float32
BF16 = jnp.bfloat16

D_MODEL = 2048
BATCH = 2
SEQ = 4096
N_META = 16
BLOCK = 128
PAD = (-N_META) % BLOCK
LP = PAD + N_META + SEQ
NBLK = LP // BLOCK
EPS = 1e-6
NEG_INF = -1e30

H_DA = 4
DK_DA = 128
DV_DA = 256
H_M = 4
DK_M = 128
DV_M = 256
CONV_W = 5
N_BUCKETS = 32
MAX_DISTANCE = 128
N_EXPERTS = 32
TOP_K = 4
D_FF = 2048
SWIGLU_ALPHA = 1.702
SWIGLU_LIMIT = 7.0
LAMBDA_INIT = 0.8 - 0.6 * math.exp(-0.3 * 0)

COL_DA_Q = 0
COL_DA_K = 1024
COL_DA_V = 2048
COL_M_Q = 3072
COL_M_K = 3584
COL_M_V = 4096
COL_M_O = 5120
COL_M_G = 6144
PROJ_COLS = 6144

N_TOK = BATCH * SEQ
N_ASSIGN = N_TOK * TOP_K

VMEM_LIMIT = 52 * 1024 * 1024
MOE_VMEM_LIMIT = 58 * 1024 * 1024


def _cparams(n_axes):
    return pltpu.CompilerParams(
        dimension_semantics=("arbitrary",) * n_axes, vmem_limit_bytes=VMEM_LIMIT)


def _rms(v, gain):
    ms = jnp.mean(v * v, axis=-1, keepdims=True)
    return v * lax.rsqrt(ms + EPS) * gain


def _norm_in_kernel(x_ref, meta_ref, g_ref, pad_ref, real_ref):
    j = pl.program_id(1)
    g = g_ref[...]

    @pl.when(j == 0)
    def _():
        pad_ref[0, :PAD, :] = jnp.zeros((PAD, D_MODEL), BF16)
        pad_ref[0, PAD:, :] = _rms(meta_ref[...], g).astype(BF16)

    @pl.when(j > 0)
    def _():
        y = _rms(x_ref[0], g).astype(BF16)
        pad_ref[0] = y
        real_ref[0] = y


def norm_in(x, meta, gain):
    return pl.pallas_call(
        _norm_in_kernel,
        grid=(BATCH, NBLK),
        in_specs=[
            pl.BlockSpec((1, BLOCK, D_MODEL), lambda b, j: (b, jnp.maximum(j - 1, 0), 0)),
            pl.BlockSpec((N_META, D_MODEL), lambda b, j: (0, 0)),
            pl.BlockSpec((1, D_MODEL), lambda b, j: (0, 0)),
        ],
        out_specs=[
            pl.BlockSpec((1, BLOCK, D_MODEL), lambda b, j: (b, j, 0)),
            pl.BlockSpec((1, BLOCK, D_MODEL), lambda b, j: (b, jnp.maximum(j - 1, 0), 0)),
        ],
        out_shape=[
            jax.ShapeDtypeStruct((BATCH, LP, D_MODEL), BF16),
            jax.ShapeDtypeStruct((BATCH, SEQ, D_MODEL), BF16),
        ],
        compiler_params=_cparams(2),
        name="norm_in",
    )(x, meta, gain.reshape(1, D_MODEL))


def _mm_kernel(*refs, has_bias, has_res, act, w_transposed):
    x_ref, w_ref = refs[0], refs[1]
    pos = 2
    b_ref = r_ref = None
    if has_bias:
        b_ref = refs[pos]
        pos += 1
    if has_res:
        r_ref = refs[pos]
        pos += 1
    o_ref, wbf_ref = refs[pos], refs[pos + 1]

    @pl.when(pl.program_id(1) == 0)
    def _():
        w = w_ref[...]
        wbf_ref[...] = (w.T if w_transposed else w).astype(BF16)

    acc = jnp.dot(x_ref[...], wbf_ref[...], preferred_element_type=F32)
    if has_bias:
        acc = acc + b_ref[...]
    if act == "sigmoid":
        acc = jax.nn.sigmoid(acc)
    if has_res:
        acc = acc + r_ref[...]
    o_ref[...] = acc.astype(o_ref.dtype)


def matmul(x, w, *, n_cols, col_block0=0, tm, tn, bias=None, res=None, act=None,
           out_dtype=BF16, w_transposed=False, name):
    m, k = x.shape
    in_specs = [
        pl.BlockSpec((tm, k), lambda j, i: (i, 0)),
        pl.BlockSpec((tn, k), lambda j, i: (j + col_block0, 0)) if w_transposed
        else pl.BlockSpec((k, tn), lambda j, i: (0, j + col_block0)),
    ]
    args = [x, w]
    if bias is not None:
        in_specs.append(pl.BlockSpec((1, tn), lambda j, i: (0, j)))
        args.append(bias.reshape(1, n_cols))
    if res is not None:
        in_specs.append(pl.BlockSpec((tm, tn), lambda j, i: (i, j)))
        args.append(res)
    return pl.pallas_call(
        functools.partial(_mm_kernel, has_bias=bias is not None, has_res=res is not None, act=act,
                          w_transposed=w_transposed),
        grid=(n_cols // tn, m // tm),
        in_specs=in_specs,
        out_specs=pl.BlockSpec((tm, tn), lambda j, i: (i, j)),
        out_shape=jax.ShapeDtypeStruct((m, n_cols), out_dtype),
        scratch_shapes=[pltpu.VMEM((k, tn), BF16)],
        compiler_params=_cparams(2),
        name=name,
    )(*args)


def _conv_kernel(p_ref, w_ref, o_ref):
    c = pl.program_id(1)
    x = p_ref[0].astype(F32)
    w = w_ref[...]
    half = CONV_W // 2
    acc = w[half:half + 1, :] * x
    for j in range(CONV_W):
        if j != half:
            acc = acc + w[j:j + 1, :] * pltpu.roll(x, (half - j) % LP, axis=0)
    y = acc * jax.nn.sigmoid(acc)
    rows = lax.broadcasted_iota(jnp.int32, (LP, 1), 0)
    y = jnp.where(rows >= PAD, y, 0.0)
    scale = jnp.where(c < 2, DK_M ** -0.5, 1.0).astype(F32)
    o_ref[0] = (y * scale).astype(BF16)


def conv_qk(proj3, conv_w):
    cw = 256
    return pl.pallas_call(
        _conv_kernel,
        grid=(BATCH, (2 * H_M * DK_M) // cw),
        in_specs=[
            pl.BlockSpec((1, LP, cw), lambda b, c: (b, 0, COL_M_Q // cw + c)),
            pl.BlockSpec((CONV_W, cw), lambda b, c: (0, c)),
        ],
        out_specs=pl.BlockSpec((1, LP, cw), lambda b, c: (b, 0, c)),
        out_shape=jax.ShapeDtypeStruct((BATCH, LP, 2 * H_M * DK_M), BF16),
        compiler_params=_cparams(2),
        name="conv_qk",
    )(proj3, conv_w)


def _split_dot(tri, v):
    hi = v.astype(BF16)
    r1 = v - hi.astype(F32)
    mid = r1.astype(BF16)
    lo = (r1 - mid.astype(F32)).astype(BF16)
    return (jnp.dot(tri, hi, preferred_element_type=F32)
            + jnp.dot(tri, mid, preferred_element_type=F32)
            + jnp.dot(tri, lo, preferred_element_type=F32))


def _gate_kernel(g_ref, bias_ref, o_ref):
    ti = lax.broadcasted_iota(jnp.int32, (BLOCK, BLOCK), 0)
    ui = lax.broadcasted_iota(jnp.int32, (BLOCK, BLOCK), 1)
    tril = jnp.where(ui <= ti, 1.0, 0.0).astype(BF16)
    triu = jnp.where(ui >= ti, 1.0, 0.0).astype(BF16)
    ch = lax.broadcasted_iota(jnp.int32, (BLOCK, 4 * H_M), 1)
    typ = lax.shift_right_logical(ch, 2)
    rloc = lax.broadcasted_iota(jnp.int32, (BLOCK, 4 * H_M), 0)

    def body(c, carry):
        r0 = pl.multiple_of(c * BLOCK, BLOCK)
        g = g_ref[0, pl.ds(r0, BLOCK), :][:, :4 * H_M] + bias_ref[...]
        valid = (rloc + r0) >= PAD
        lsig = -(jnp.maximum(-g, 0.0) + jnp.log1p(jnp.exp(-jnp.abs(g))))
        lf = jnp.where(valid, lsig, 0.0)
        cum = _split_dot(tril, lf)
        rcum = _split_dot(triu, lf)
        li = jnp.where(valid, g, -jnp.inf)
        out = jnp.where(typ == 1, cum, jnp.where(typ == 3, rcum, li))
        o_ref[0, pl.ds(r0, BLOCK), :] = out
        return carry

    lax.fori_loop(0, NBLK, body, 0)


def gate_prep(mg3, gate_bias):
    return pl.pallas_call(
        _gate_kernel,
        grid=(BATCH,),
        in_specs=[
            pl.BlockSpec((1, LP, 128), lambda b: (b, 0, 0)),
            pl.BlockSpec((1, 4 * H_M), lambda b: (0, 0)),
        ],
        out_specs=pl.BlockSpec((1, LP, 4 * H_M), lambda b: (b, 0, 0)),
        out_shape=jax.ShapeDtypeStruct((BATCH, LP, 4 * H_M), F32),
        compiler_params=_cparams(1),
        name="gate_prep",
    )(mg3, gate_bias.reshape(1, 4 * H_M))


MLSTM_HP = 2
MLSTM_MID = NBLK // 2


def _mlstm_kernel(qt_ref, k_ref, vt_ref, ot_ref, gr_ref, y_ref,
                  hs_ref, c_ref, n_ref, m_ref):
    c_ref[...] = jnp.zeros_like(c_ref)
    n_ref[...] = jnp.zeros_like(n_ref)
    m_ref[...] = jnp.zeros_like(m_ref)
    si = lax.broadcasted_iota(jnp.int32, (BLOCK, BLOCK), 0)
    ti = lax.broadcasted_iota(jnp.int32, (BLOCK, BLOCK), 1)
    mask_f = si <= ti
    mask_b = si >= ti

    def chain(c, hl, bwd, final):
        idx = 2 * hl + bwd
        r0 = c * BLOCK if isinstance(c, int) else pl.multiple_of(c * BLOCK, BLOCK)
        t_sl = pl.ds(r0, BLOCK)
        qt = qt_ref[0, hl * DK_M:(hl + 1) * DK_M, t_sl]
        k = k_ref[0, t_sl, hl * DK_M:(hl + 1) * DK_M]
        feat = slice(hl * DV_M, (hl + 1) * DV_M)
        vt = vt_ref[0, feat, t_sl]
        gr = gr_ref[0, hl, :, t_sl]
        li_r, b_r = gr[2 * bwd:2 * bwd + 1, :], gr[2 * bwd + 1:2 * bwd + 2, :]
        a_c = jnp.transpose(jnp.broadcast_to(li_r - b_r, (BLOCK, BLOCK)))
        b_end = b_r[:, 0:1] if bwd else b_r[:, BLOCK - 1:BLOCK]
        m_prev = m_ref[idx][:, 0:1]
        ct = c_ref[idx]
        nst = n_ref[idx]
        dmat = jnp.where(mask_b if bwd else mask_f, a_c + b_r, -jnp.inf)
        inter = b_r + m_prev
        m_t = jnp.maximum(inter, jnp.max(dmat, axis=0, keepdims=True))
        w_inter = jnp.exp(inter - m_t)
        st = jnp.dot(k, qt, preferred_element_type=F32) * jnp.exp(dmat - m_t)
        num = (w_inter * jnp.dot(ct.astype(BF16), qt, preferred_element_type=F32)
               + jnp.dot(vt, st.astype(BF16), preferred_element_type=F32))
        nq = jnp.dot(nst.astype(BF16), qt, preferred_element_type=F32)
        den = w_inter * nq + jnp.sum(st, axis=0, keepdims=True)
        h = num * (1.0 / jnp.maximum(jnp.abs(den), jnp.exp(-m_t)))
        if final:
            og = jax.nn.sigmoid(ot_ref[0, feat, t_sl].astype(F32))
            y_ref[0, feat, pl.ds(r0 - BLOCK, BLOCK)] = (og * (hs_ref[feat, t_sl] + h)).astype(BF16)
        else:
            hs_ref[feat, t_sl] = h
        ldec = b_end - b_r + li_r
        m_new = jnp.maximum(b_end + m_prev, jnp.max(ldec, axis=1, keepdims=True))
        w_c = jnp.exp(b_end + m_prev - m_new)
        w_s = jnp.exp(ldec - m_new)
        wvt = (vt.astype(F32) * w_s).astype(BF16)
        c_ref[idx] = w_c * ct + jnp.dot(wvt, k, preferred_element_type=F32)
        n_ref[idx] = w_c * nst + jnp.dot(w_s.astype(BF16), k, preferred_element_type=F32)
        m_ref[idx] = jnp.broadcast_to(m_new, (1, BLOCK))

    def first_half(i, carry):
        for hl in range(MLSTM_HP):
            chain(i, hl, 0, False)
            chain(NBLK - 1 - i, hl, 1, False)
        return carry

    def second_half(i, carry):
        for hl in range(MLSTM_HP):
            chain(i, hl, 0, True)
            chain(NBLK - 1 - i, hl, 1, True)
        return carry

    lax.fori_loop(0, MLSTM_MID, first_half, 0)
    for hl in range(MLSTM_HP):
        chain(MLSTM_MID, hl, 0, False)
        chain(MLSTM_MID, hl, 1, True)
    lax.fori_loop(MLSTM_MID + 1, NBLK - 1, second_half, 0)
    for hl in range(MLSTM_HP):
        chain(NBLK - 1, hl, 0, True)


def mlstm(q_t, qk_m, v_t, o_t, grow):
    hp = MLSTM_HP
    kw, vw = hp * DK_M, hp * DV_M
    return pl.pallas_call(
        _mlstm_kernel,
        grid=(BATCH, H_M // hp),
        in_specs=[
            pl.BlockSpec((1, kw, LP), lambda b, g: (b, g, 0)),
            pl.BlockSpec((1, LP, kw), lambda b, g: (b, 0, (H_M * DK_M) // kw + g)),
            pl.BlockSpec((1, vw, LP), lambda b, g: (b, g, 0)),
            pl.BlockSpec((1, vw, LP), lambda b, g: (b, g, 0)),
            pl.BlockSpec((1, hp, 4, LP), lambda b, g: (b, g, 0, 0)),
        ],
        out_specs=pl.BlockSpec((1, vw, SEQ), lambda b, g: (b, g, 0)),
        out_shape=jax.ShapeDtypeStruct((BATCH, H_M * DV_M, SEQ), BF16),
        scratch_shapes=[
            pltpu.VMEM((vw, LP), F32),
            pltpu.VMEM((2 * hp, DV_M, DK_M), F32),
            pltpu.VMEM((2 * hp, 1, DK_M), F32),
            pltpu.VMEM((2 * hp, 1, BLOCK), F32),
        ],
        compiler_params=_cparams(2),
        name="mlstm",
    )(q_t, qk_m, v_t, o_t, grow)


LOG2E = 1.4426950408889634
ATT_QB = 2
ATT_TQ = ATT_QB * BLOCK
ATT_BAND = (ATT_QB + 2) * BLOCK
ATT_GROUPS = (6, 6, 6, 6, 5)
assert BLOCK >= MAX_DISTANCE and ATT_BAND + sum(ATT_GROUPS) * BLOCK == LP


ATT_NQ = SEQ // ATT_TQ


def _attn_kernel(c_ref, *refs):
    q_refs = refs[:ATT_QB]
    (k1_ref, v1_ref, tab_ref, lam_ref, sg_ref, o_ref,
     s_a, s_b, mx_a, mx_b, k_ref, v_ref) = refs[ATT_QB:]
    h = pl.program_id(1)
    t = pl.program_id(2)

    @pl.when(t == 0)
    def _():
        for rep in range(2):
            k_ref[0, rep * LP:(rep + 1) * LP, :] = k1_ref[0]
            v_ref[0, rep * LP:(rep + 1) * LP, :] = v1_ref[0]

    scale = DK_DA ** -0.5 * LOG2E
    c_neg = c_ref[h, 0]
    c_pos = c_ref[h, 1]

    def groups_of(tile):
        koff = (ATT_QB * tile) * BLOCK
        out = [(koff, ATT_BAND, 0)]
        col = ATT_BAND
        for nblk in ATT_GROUPS:
            out.append((koff + col, nblk * BLOCK, col))
            col += nblk * BLOCK
        return out

    def lane_fold(acc, x, op):
        for j in range(x.shape[1] // BLOCK):
            piece = x[:, j * BLOCK:(j + 1) * BLOCK]
            acc = piece if acc is None else op(acc, piece)
        return acc

    def score(tile, s_ref, mx_ref):
        q = jnp.concatenate([r[0] for r in q_refs], axis=0)
        mx = [None, None]
        for gi, (koff, width, col0) in enumerate(groups_of(tile)):
            koff = pl.multiple_of(koff, BLOCK)
            if gi == 0:
                bias = tab_ref[0, 0]
            else:
                kpos = koff + lax.broadcasted_iota(jnp.int32, (1, width), 1)
                bias = jnp.where(kpos < LP, c_pos, jnp.where(kpos < LP + PAD, NEG_INF, c_neg))
            for m in range(2):
                kk = k_ref[0, pl.ds(koff, width), m * DK_DA:(m + 1) * DK_DA]
                s = lax.dot_general(q[:, m * DK_DA:(m + 1) * DK_DA], kk,
                                    (((1,), (1,)), ((), ())),
                                    preferred_element_type=F32) * scale + bias
                s_ref[m, :, col0:col0 + width] = s
                mx[m] = lane_fold(mx[m], s, jnp.maximum)
        for m in range(2):
            mx_ref[m] = mx[m]

    def finish(tile, s_ref, mx_ref):
        lp = lam_ref[...]
        lam = (jnp.exp(jnp.sum(lp[0:1] * lp[1:2], axis=1, keepdims=True))
               - jnp.exp(jnp.sum(lp[2:3] * lp[3:4], axis=1, keepdims=True)) + LAMBDA_INIT)
        row_max = [jnp.max(mx_ref[m], axis=1, keepdims=True) for m in range(2)]
        lsum = [None, None]
        acc = [None, None]
        for koff, width, col0 in groups_of(tile):
            koff = pl.multiple_of(koff, BLOCK)
            vv = v_ref[0, pl.ds(koff, width), :]
            for m in range(2):
                p = jnp.exp2(s_ref[m, :, col0:col0 + width] - row_max[m])
                lsum[m] = lane_fold(lsum[m], p, jnp.add)
                pv = jnp.dot(p.astype(BF16), vv, preferred_element_type=F32)
                acc[m] = pv if acc[m] is None else acc[m] + pv
        l1 = jnp.sum(lsum[0], axis=1, keepdims=True)
        l2 = jnp.sum(lsum[1], axis=1, keepdims=True)
        o = acc[0] / l1 - lam * (acc[1] / l2)
        o_ref[0] = (_rms(o, sg_ref[...]) * (1.0 - LAMBDA_INIT)).astype(BF16)

    even = lax.rem(t, 2) == 0
    inner = jnp.logical_and(t > 0, t < ATT_NQ)

    @pl.when(t == 0)
    def _():
        score(t, s_a, mx_a)

    @pl.when(jnp.logical_and(inner, even))
    def _():
        score(t, s_a, mx_a)
        finish(t - 1, s_b, mx_b)

    @pl.when(jnp.logical_and(inner, jnp.logical_not(even)))
    def _():
        score(t, s_b, mx_b)
        finish(t - 1, s_a, mx_a)

    @pl.when(t == ATT_NQ)
    def _():
        if (ATT_NQ - 1) % 2 == 0:
            finish(t - 1, s_a, mx_a)
        else:
            finish(t - 1, s_b, mx_b)


def diff_attention(consts, proj3, tab, lam_params, subln):
    nq = ATT_NQ
    kblk0 = COL_DA_K // (2 * DK_DA)
    vblk0 = COL_DA_V // DV_DA
    scored = lambda i: jnp.minimum(i, nq - 1)
    finished = lambda i: jnp.maximum(i - 1, 0)

    def tab_map(b, h, i):
        tile = scored(i)
        case = jnp.where(tile == 0, 0, jnp.where(tile == nq - 1, 2, 1))
        return (h, case, 0, 0)

    return pl.pallas_call(
        _attn_kernel,
        grid=(BATCH, H_DA, nq + 1),
        in_specs=[
            pl.BlockSpec(memory_space=pltpu.SMEM),
            *[pl.BlockSpec((1, BLOCK, 2 * DK_DA),
                           functools.partial(
                               lambda b, h, i, r: (b, ATT_QB * scored(i) + 1 + r, h), r=r))
              for r in range(ATT_QB)],
            pl.BlockSpec((1, LP, 2 * DK_DA), lambda b, h, i: (b, 0, kblk0 + h)),
            pl.BlockSpec((1, LP, DV_DA), lambda b, h, i: (b, 0, vblk0 + h)),
            pl.BlockSpec((1, 1, ATT_TQ, ATT_BAND), tab_map),
            pl.BlockSpec((4, DK_DA), lambda b, h, i: (0, 0)),
            pl.BlockSpec((1, DV_DA), lambda b, h, i: (0, 0)),
        ],
        out_specs=pl.BlockSpec((1, ATT_TQ, DV_DA), lambda b, h, i: (b, finished(i), h)),
        out_shape=jax.ShapeDtypeStruct((BATCH, SEQ, H_DA * DV_DA), BF16),
        scratch_shapes=[pltpu.VMEM((2, ATT_TQ, LP), F32),
                        pltpu.VMEM((2, ATT_TQ, LP), F32),
                        pltpu.VMEM((2, ATT_TQ, BLOCK), F32),
                        pltpu.VMEM((2, ATT_TQ, BLOCK), F32),
                        pltpu.VMEM((1, 2 * LP, 2 * DK_DA), BF16),
                        pltpu.VMEM((1, 2 * LP, DV_DA), BF16)],
        compiler_params=_cparams(3),
        name="diff_attn",
    )(consts, *([proj3] * (ATT_QB + 2)), tab, lam_params, subln.reshape(1, DV_DA))


def _bias_tables(rel_bias):
    rb = rel_bias.astype(F32)
    span = 1024
    assert span >= ATT_TQ + ATT_BAND
    rel = jnp.arange(span, dtype=jnp.int32) - span // 2
    nb = N_BUCKETS // 2
    max_exact = nb // 2
    n = jnp.abs(rel)
    nf = jnp.maximum(n, 1).astype(F32)
    large = max_exact + (jnp.log(nf / max_exact) / math.log(MAX_DISTANCE / max_exact)
                         * (nb - max_exact)).astype(jnp.int32)
    large = jnp.minimum(large, nb - 1)
    bucket = jnp.where(rel > 0, nb, 0) + jnp.where(n < max_exact, n, large)
    hit = bucket[None, :, None] == jnp.arange(N_BUCKETS, dtype=jnp.int32)
    by_rel = jnp.sum(jnp.where(hit, rb.T[:, None, :], 0.0), axis=-1)
    shifted = jnp.tile(by_rel, (1, ATT_TQ))[:, :ATT_TQ * (span - 1)].reshape(
        H_DA, ATT_TQ, span - 1)
    c0 = span // 2 - BLOCK
    gen = shifted[:, :, c0:c0 + ATT_BAND]
    c_neg = rb[nb - 1]
    c_pos = rb[N_BUCKETS - 1]
    jj = jnp.arange(ATT_BAND, dtype=jnp.int32)[None, None, :]
    first = jnp.where(jj < PAD, NEG_INF, gen)
    wrap0 = ATT_BAND - BLOCK
    wrapped = jnp.where(jj - wrap0 < PAD, NEG_INF, c_neg[:, None, None])
    last = jnp.where(jj >= wrap0, wrapped, gen)
    tab = jnp.stack([first, gen, last], axis=1)
    consts = jnp.stack([c_neg, c_pos], axis=1)
    return tab * LOG2E, consts * LOG2E


def _mix_kernel(ya_ref, ym_ref, wa_ref, wm_ref, ga_ref, gm_ref, o_ref, wa_bf, wm_bf):
    @pl.when(pl.program_id(1) == 0)
    def _():
        wa_bf[...] = wa_ref[...].astype(BF16)
        wm_bf[...] = wm_ref[...].astype(BF16)

    a = jnp.dot(ya_ref[...], wa_bf[...], preferred_element_type=F32)
    m = jnp.dot(ym_ref[...], wm_bf[...], preferred_element_type=F32)
    o_ref[...] = (ga_ref[...].astype(F32) * a + gm_ref[...].astype(F32) * m).astype(BF16)


def branch_mix(y_da, y_m, w_da, w_m, gate, *, tm=512, tn=1024):
    m, k = y_da.shape
    nj = D_MODEL // tn
    return pl.pallas_call(
        _mix_kernel,
        grid=(nj, m // tm),
        in_specs=[
            pl.BlockSpec((tm, k), lambda j, i: (i, 0)),
            pl.BlockSpec((tm, k), lambda j, i: (i, 0)),
            pl.BlockSpec((k, tn), lambda j, i: (0, j)),
            pl.BlockSpec((k, tn), lambda j, i: (0, j)),
            pl.BlockSpec((tm, tn), lambda j, i: (i, j)),
            pl.BlockSpec((tm, tn), lambda j, i: (i, nj + j)),
        ],
        out_specs=pl.BlockSpec((tm, tn), lambda j, i: (i, j)),
        out_shape=jax.ShapeDtypeStruct((m, D_MODEL), BF16),
        scratch_shapes=[pltpu.VMEM((k, tn), BF16), pltpu.VMEM((k, tn), BF16)],
        compiler_params=_cparams(2),
        name="branch_mix",
    )(y_da, y_m, w_da, w_m, gate, gate)


FFN_TM = 1024
HALF = D_MODEL // 2


def _pack_bf16_pairs(v):
    lo = lax.bitcast_convert_type(v[:, :HALF].astype(BF16).astype(F32), jnp.uint32)
    hi = lax.bitcast_convert_type(v[:, HALF:].astype(BF16).astype(F32), jnp.uint32)
    return (hi & jnp.uint32(0xFFFF0000)) | lax.shift_right_logical(lo, jnp.uint32(16))


def _unpack_bf16_pairs(w):
    lo = lax.bitcast_convert_type(lax.shift_left(w, jnp.uint32(16)), F32).astype(BF16)
    hi = lax.bitcast_convert_type(w & jnp.uint32(0xFFFF0000), F32).astype(BF16)
    return lo, hi


def _ffn_prep_kernel(h_ref, g_ref, wr_ref, br_ref, hn_ref, e_ref, w_ref, r_ref, cnt_ref, base_ref):
    @pl.when(pl.program_id(0) == 0)
    def _():
        base_ref[...] = jnp.zeros_like(base_ref)

    hn = _rms(h_ref[...], g_ref[...])
    hn_ref[...] = _pack_bf16_pairs(hn)
    logits = jnp.dot(hn.astype(BF16), wr_ref[...].astype(BF16),
                     preferred_element_type=F32) + br_ref[...]
    lane = lax.broadcasted_iota(jnp.int32, (FFN_TM, N_EXPERTS), 1)
    lane_o = lax.broadcasted_iota(jnp.int32, (FFN_TM, 128), 1)
    ti = lax.broadcasted_iota(jnp.int32, (FFN_TM, FFN_TM), 0)
    ui = lax.broadcasted_iota(jnp.int32, (FFN_TM, FFN_TM), 1)
    tril = jnp.where(ui <= ti, 1.0, 0.0).astype(BF16)
    e_out = jnp.zeros((FFN_TM, 128), jnp.int32)
    r_out = jnp.zeros((FFN_TM, 128), jnp.int32)
    l_out = jnp.full((FFN_TM, 128), -jnp.inf, F32)
    base = base_ref[...]
    l = logits
    for kk in range(TOP_K):
        mk = jnp.max(l, axis=1, keepdims=True)
        ik = jnp.min(jnp.where(l == mk, lane, N_EXPERTS), axis=1, keepdims=True)
        hit = lane == ik
        oh = jnp.where(hit, 1.0, 0.0)
        cum = jnp.dot(tril, oh.astype(BF16), preferred_element_type=F32)
        rank = jnp.sum(oh * (cum + base), axis=1, keepdims=True) - 1.0
        base = base + jnp.sum(oh, axis=0, keepdims=True)
        e_out = jnp.where(lane_o == kk, ik, e_out)
        r_out = jnp.where(lane_o == kk, rank.astype(jnp.int32), r_out)
        l_out = jnp.where(lane_o == kk, mk, l_out)
        l = jnp.where(hit, -jnp.inf, l)
    base_ref[...] = base
    cnt_ref[...] = base
    ex = jnp.exp(l_out - jnp.max(l_out, axis=1, keepdims=True))
    e_ref[...] = e_out
    r_ref[...] = r_out
    w_ref[...] = ex / jnp.sum(ex, axis=1, keepdims=True)


def ffn_prep(h2, gain, w_router, b_router):
    row = lambda i: (i, 0)
    fixed = lambda i: (0, 0)
    return pl.pallas_call(
        _ffn_prep_kernel,
        grid=(N_TOK // FFN_TM,),
        in_specs=[
            pl.BlockSpec((FFN_TM, D_MODEL), row),
            pl.BlockSpec((1, D_MODEL), fixed),
            pl.BlockSpec((D_MODEL, N_EXPERTS), fixed),
            pl.BlockSpec((1, N_EXPERTS), fixed),
        ],
        out_specs=[
            pl.BlockSpec((FFN_TM, HALF), row),
            pl.BlockSpec((FFN_TM, 128), row),
            pl.BlockSpec((FFN_TM, 128), row),
            pl.BlockSpec((FFN_TM, 128), row),
            pl.BlockSpec((1, N_EXPERTS), fixed),
        ],
        out_shape=[
            jax.ShapeDtypeStruct((N_TOK, HALF), jnp.uint32),
            jax.ShapeDtypeStruct((N_TOK, 128), jnp.int32),
            jax.ShapeDtypeStruct((N_TOK, 128), F32),
            jax.ShapeDtypeStruct((N_TOK, 128), jnp.int32),
            jax.ShapeDtypeStruct((1, N_EXPERTS), F32),
        ],
        scratch_shapes=[pltpu.VMEM((1, N_EXPERTS), F32)],
        compiler_params=_cparams(1),
        name="ffn_prep",
    )(h2, gain.reshape(1, D_MODEL), w_router, b_router.reshape(1, N_EXPERTS))


SEG_ALIGN = 128
MOE_R = N_ASSIGN + N_EXPERTS * SEG_ALIGN


def _plan(counts_f, top_e, rank):
    counts = counts_f[0].astype(jnp.int32)
    seg_rows = (counts + SEG_ALIGN - 1) // SEG_ALIGN * SEG_ALIGN
    seg_start = jnp.cumsum(seg_rows) - seg_rows
    eq = top_e[:, :, None] == jnp.arange(N_EXPERTS, dtype=jnp.int32)[None, None, :]
    dest = jnp.sum(jnp.where(eq, seg_start[None, None, :], 0), axis=-1) + rank
    used = jnp.sum(seg_rows)
    slack = jnp.stack([used, (MOE_R - used) // SEG_ALIGN])
    return (dest.reshape(N_ASSIGN).astype(jnp.int32), seg_start.astype(jnp.int32),
            seg_rows.astype(jnp.int32), (seg_start + counts).astype(jnp.int32),
            (seg_rows - counts).astype(jnp.int32), slack.astype(jnp.int32))


def _zero_slack(slack_ref, zero_block, dst_rows, sem):
    zero_block[...] = jnp.zeros_like(zero_block)

    def copy(j):
        r0 = pl.multiple_of(slack_ref[0] + j * SEG_ALIGN, SEG_ALIGN)
        return pltpu.make_async_copy(zero_block, dst_rows(pl.ds(r0, SEG_ALIGN)), sem)

    def start(j, carry):
        copy(j).start()
        return carry

    def wait(j, carry):
        copy(j).wait()
        return carry

    lax.fori_loop(0, slack_ref[1], start, 0)
    lax.fori_loop(0, slack_ref[1], wait, 0)


DISP_TOK = 256


def _dispatch_kernel(dest_ref, pad0_ref, padn_ref, slack_ref, hn_ref, xs_ref,
                     zrow_ref, zblk_ref, sem, zsem):
    i = pl.program_id(0)

    @pl.when(i == 0)
    def _():
        zrow_ref[...] = jnp.zeros_like(zrow_ref)
        _zero_slack(slack_ref, zblk_ref, lambda rows: xs_ref.at[rows], zsem.at[0])

        def expert(e, carry):
            p0 = pad0_ref[e]
            pn = padn_ref[e]

            def zstart(r, c2):
                pltpu.make_async_copy(zrow_ref, xs_ref.at[pl.ds(p0 + r, 1)], zsem.at[0]).start()
                return c2

            def zwait(r, c2):
                pltpu.make_async_copy(zrow_ref, xs_ref.at[pl.ds(p0, 1)], zsem.at[0]).wait()
                return c2

            lax.fori_loop(0, pn, zstart, 0)
            lax.fori_loop(0, pn, zwait, 0)
            return carry

        lax.fori_loop(0, N_EXPERTS, expert, 0)

    def tok(t, carry):
        a = (i * DISP_TOK + t) * TOP_K
        for kk in range(TOP_K):
            pltpu.make_async_copy(hn_ref.at[pl.ds(t, 1)], xs_ref.at[pl.ds(dest_ref[a + kk], 1)],
                                  sem.at[0]).start(priority=kk % 2)
        return carry

    lax.fori_loop(0, DISP_TOK, tok, 0, unroll=4)
    for kk in range(TOP_K):
        pltpu.make_async_copy(hn_ref, xs_ref.at[pl.ds(0, DISP_TOK)], sem.at[0]).wait()


def dispatch(dest, pad0, padn, slack, hn_packed):
    grid_spec = pltpu.PrefetchScalarGridSpec(
        num_scalar_prefetch=4,
        grid=(N_TOK // DISP_TOK,),
        in_specs=[pl.BlockSpec((DISP_TOK, HALF), lambda i, d, p0, pn, z: (i, 0))],
        out_specs=pl.BlockSpec(memory_space=pl.ANY),
        scratch_shapes=[pltpu.VMEM((1, HALF), jnp.uint32),
                        pltpu.VMEM((SEG_ALIGN, HALF), jnp.uint32),
                        pltpu.SemaphoreType.DMA((1,)), pltpu.SemaphoreType.DMA((1,))],
    )
    return pl.pallas_call(
        _dispatch_kernel,
        grid_spec=grid_spec,
        out_shape=jax.ShapeDtypeStruct((MOE_R, HALF), jnp.uint32),
        compiler_params=_cparams(1),
        name="moe_dispatch",
    )(dest, pad0, padn, slack, hn_packed)


MOE_CH = 2 * SEG_ALIGN
MOE_TF = 1024
MOE_NF = D_FF // MOE_TF


class _CopyGroup:
    def __init__(self, copies):
        self.copies = copies

    def start(self, priority=0):
        for cp in self.copies:
            cp.start(priority=priority)

    def wait(self):
        for cp in self.copies:
            cp.wait()


W_PIECES = 8
W_PER_CHUNK = 2


N_STATE = 4


def _stream_rows(step, n_steps, start, rows, next_start, next_rows, state,
                 make_in, make_out, make_tail_in, make_tail_out,
                 compute_chunk, compute_tail, before_first_wait, next_weight_piece):
    has_next = step + 1 < n_steps

    def request_weights(first, count):
        def one(p, carry):
            @pl.when(jnp.logical_and(has_next, p < W_PIECES))
            def _():
                next_weight_piece(p).start()
            return carry

        lax.fori_loop(first, first + count, one, 0)

    @pl.when(step == 0)
    def _():
        for j in range(N_STATE):
            state[j] = 0

    n_ch = lax.shift_right_logical(rows, MOE_CH.bit_length() - 1)
    tail = rows - n_ch * MOE_CH
    tail_row = start + n_ch * MOE_CH
    g0 = state[0]
    feeds_next = jnp.logical_and(step + 1 < n_steps, next_rows >= MOE_CH)

    def chunk_row(c):
        return start + c * MOE_CH

    @pl.when(tail > 0)
    def _():
        make_tail_in(tail_row).start(priority=1)

    @pl.when(jnp.logical_and(n_ch > 0, state[3] == 0))
    def _():
        make_in(start, lax.rem(g0, 2)).start(priority=1)

    before_first_wait()

    @pl.when(tail > 0)
    def _():
        make_tail_in(tail_row).wait()
        compute_tail()
        make_tail_out(tail_row).start(priority=1)

    def body(c, carry):
        slot = lax.rem(g0 + c, 2)
        make_in(chunk_row(c), slot).wait()

        @pl.when(c + 1 < n_ch)
        def _():
            make_in(chunk_row(c + 1), 1 - slot).start(priority=1)

        @pl.when(jnp.logical_and(c + 1 == n_ch, feeds_next))
        def _():
            make_in(next_start, 1 - slot).start(priority=1)

        request_weights(c * W_PER_CHUNK, W_PER_CHUNK)

        @pl.when(state[1 + slot] == 1)
        def _():
            make_out(chunk_row(c), slot).wait()

        compute_chunk(slot)
        make_out(chunk_row(c), slot).start(priority=1)
        state[1 + slot] = 1
        return carry

    lax.fori_loop(0, n_ch, body, 0)
    request_weights(n_ch * W_PER_CHUNK, W_PIECES)
    state[0] = g0 + n_ch
    state[3] = jnp.where(jnp.logical_and(n_ch > 0, feeds_next), 1, 0)

    @pl.when(tail > 0)
    def _():
        make_tail_out(tail_row).wait()

    @pl.when(step == n_steps - 1)
    def _():
        for slot in range(2):
            @pl.when(state[1 + slot] == 1)
            def _():
                make_out(0, slot).wait()
                state[1 + slot] = 0


def _rows_at(row0, n):
    return pl.ds(row0 if isinstance(row0, int) else pl.multiple_of(row0, SEG_ALIGN), n)


def _moe_up_kernel(seg_ref, rows_ref, slack_ref, xs_ref, w_hbm, bg_ref, bl_ref, act_ref,
                   xbuf, obuf, xtail, otail, wbuf, wg_bf, wl_bf, state, isem, osem, tsem, wsem):
    f = pl.program_id(0)
    e = pl.program_id(1)
    start = seg_ref[e]
    rows = rows_ref[e]
    e_next = lax.rem(e + 1, N_EXPERTS)
    f_next = jnp.where(e == N_EXPERTS - 1, f + 1, f)
    step = f * N_EXPERTS + e
    n_steps = MOE_NF * N_EXPERTS
    wslot = lax.rem(step, 2)

    def weight_piece(expert, ftile, slot, p):
        per_half = W_PIECES // 2
        band = D_MODEL // per_half
        t = p // per_half if isinstance(p, int) else lax.shift_right_logical(
            p, per_half.bit_length() - 1)
        r0 = (p - t * per_half) * band
        r0 = r0 if isinstance(r0, int) else pl.multiple_of(r0, band)
        col0 = pl.multiple_of((t * MOE_NF + ftile) * MOE_TF, MOE_TF)
        return pltpu.make_async_copy(
            w_hbm.at[expert, pl.ds(r0, band), pl.ds(col0, MOE_TF)],
            wbuf.at[slot, t, pl.ds(r0, band)], wsem.at[slot])

    @pl.when(step == 0)
    def _():
        for p in range(W_PIECES):
            weight_piece(e, f, wslot, p).start()

    def make_in(r0, slot):
        return pltpu.make_async_copy(xs_ref.at[_rows_at(r0, MOE_CH)], xbuf.at[slot], isem.at[slot])

    def make_out(r0, slot):
        return pltpu.make_async_copy(obuf.at[slot], act_ref.at[f, _rows_at(r0, MOE_CH)],
                                     osem.at[slot])

    def make_tail_in(r0):
        return pltpu.make_async_copy(xs_ref.at[_rows_at(r0, SEG_ALIGN)], xtail, tsem.at[0])

    def make_tail_out(r0):
        return pltpu.make_async_copy(otail, act_ref.at[f, _rows_at(r0, SEG_ALIGN)], tsem.at[1])

    def cast_weights():
        for p in range(W_PIECES):
            weight_piece(e, f, wslot, p).wait()

        @pl.when(rows > 0)
        def _():
            wg_bf[...] = wbuf[wslot, 0].astype(BF16)
            wl_bf[...] = wbuf[wslot, 1].astype(BF16)

    def expert_mlp(words):
        lo, hi = _unpack_bf16_pairs(words)
        glu = (jnp.dot(lo, wg_bf[:HALF, :], preferred_element_type=F32)
               + jnp.dot(hi, wg_bf[HALF:, :], preferred_element_type=F32) + bg_ref[0])
        lin = (jnp.dot(lo, wl_bf[:HALF, :], preferred_element_type=F32)
               + jnp.dot(hi, wl_bf[HALF:, :], preferred_element_type=F32) + bl_ref[0])
        glu = jnp.minimum(glu, SWIGLU_LIMIT)
        lin = jnp.clip(lin, -SWIGLU_LIMIT, SWIGLU_LIMIT)
        return (glu * jax.nn.sigmoid(SWIGLU_ALPHA * glu) * (lin + 1.0)).astype(BF16)

    def compute_chunk(slot):
        obuf[slot] = expert_mlp(xbuf[slot])

    def compute_tail():
        otail[...] = expert_mlp(xtail[...])

    _stream_rows(step, n_steps, start, rows,
                 seg_ref[e_next], rows_ref[e_next], state,
                 make_in, make_out, make_tail_in, make_tail_out,
                 compute_chunk, compute_tail, cast_weights,
                 lambda p: weight_piece(e_next, f_next, 1 - wslot, p))

    @pl.when(e == N_EXPERTS - 1)
    def _():
        _zero_slack(slack_ref, otail, lambda rr: act_ref.at[f, rr], tsem.at[1])


def moe_up(seg_start, seg_rows, slack, xs, w1, b1):
    grid_spec = pltpu.PrefetchScalarGridSpec(
        num_scalar_prefetch=3,
        grid=(MOE_NF, N_EXPERTS),
        in_specs=[
            pl.BlockSpec(memory_space=pl.ANY),
            pl.BlockSpec(memory_space=pl.ANY),
            pl.BlockSpec((1, 1, MOE_TF), lambda f, e, s, r, z: (e, 0, f)),
            pl.BlockSpec((1, 1, MOE_TF), lambda f, e, s, r, z: (e, 0, MOE_NF + f)),
        ],
        out_specs=pl.BlockSpec(memory_space=pl.ANY),
        scratch_shapes=[
            pltpu.VMEM((2, MOE_CH, HALF), jnp.uint32),
            pltpu.VMEM((2, MOE_CH, MOE_TF), BF16),
            pltpu.VMEM((SEG_ALIGN, HALF), jnp.uint32),
            pltpu.VMEM((SEG_ALIGN, MOE_TF), BF16),
            pltpu.VMEM((2, 2, D_MODEL, MOE_TF), F32),
            pltpu.VMEM((D_MODEL, MOE_TF), BF16),
            pltpu.VMEM((D_MODEL, MOE_TF), BF16),
            pltpu.SMEM((N_STATE,), jnp.int32),
            pltpu.SemaphoreType.DMA((2,)),
            pltpu.SemaphoreType.DMA((2,)),
            pltpu.SemaphoreType.DMA((2,)),
            pltpu.SemaphoreType.DMA((2,)),
        ],
    )
    b13 = b1.reshape(N_EXPERTS, 1, 2 * D_FF)
    return pl.pallas_call(
        _moe_up_kernel,
        grid_spec=grid_spec,
        out_shape=jax.ShapeDtypeStruct((MOE_NF, MOE_R, MOE_TF), BF16),
        compiler_params=pltpu.CompilerParams(
            dimension_semantics=("arbitrary", "arbitrary"), vmem_limit_bytes=MOE_VMEM_LIMIT),
        name="moe_up",
    )(seg_start, seg_rows, slack, xs, w1, b13, b13)


def _moe_down_kernel(seg_ref, rows_ref, slack_ref, act_ref, w_hbm, b_ref, y_ref,
                     xbuf, obuf, xtail, otail, wbuf, w_bf, state, isem, osem, tsem, wsem):
    e = pl.program_id(0)
    start = seg_ref[e]
    rows = rows_ref[e]
    e_next = lax.rem(e + 1, N_EXPERTS)
    wslot = lax.rem(e, 2)

    def weight_piece(expert, slot, p):
        band = D_FF // W_PIECES
        r0 = p * band if isinstance(p, int) else pl.multiple_of(p * band, band)
        return pltpu.make_async_copy(w_hbm.at[expert, pl.ds(r0, band)],
                                     wbuf.at[slot, pl.ds(r0, band)], wsem.at[slot])

    @pl.when(e == 0)
    def _():
        for p in range(W_PIECES):
            weight_piece(e, wslot, p).start()

    def make_in(r0, slot):
        return _CopyGroup([
            pltpu.make_async_copy(act_ref.at[j, _rows_at(r0, MOE_CH)],
                                  xbuf.at[slot, :, pl.ds(j * MOE_TF, MOE_TF)], isem.at[slot])
            for j in range(MOE_NF)])

    def make_out(r0, slot):
        return pltpu.make_async_copy(obuf.at[slot], y_ref.at[_rows_at(r0, MOE_CH)], osem.at[slot])

    def make_tail_in(r0):
        return _CopyGroup([
            pltpu.make_async_copy(act_ref.at[j, _rows_at(r0, SEG_ALIGN)],
                                  xtail.at[:, pl.ds(j * MOE_TF, MOE_TF)], tsem.at[0])
            for j in range(MOE_NF)])

    def make_tail_out(r0):
        return pltpu.make_async_copy(otail, y_ref.at[_rows_at(r0, SEG_ALIGN)], tsem.at[1])

    def cast_weights():
        for p in range(W_PIECES):
            weight_piece(e, wslot, p).wait()

        @pl.when(rows > 0)
        def _():
            w_bf[...] = wbuf[wslot].astype(BF16)

    def expert_out(a):
        return _pack_bf16_pairs(jnp.dot(a, w_bf[...], preferred_element_type=F32) + b_ref[0])

    def compute_chunk(slot):
        obuf[slot] = expert_out(xbuf[slot])

    def compute_tail():
        otail[...] = expert_out(xtail[...])

    _stream_rows(e, N_EXPERTS, start, rows, seg_ref[e_next], rows_ref[e_next], state,
                 make_in, make_out, make_tail_in, make_tail_out,
                 compute_chunk, compute_tail, cast_weights,
                 lambda p: weight_piece(e_next, 1 - wslot, p))

    @pl.when(e == N_EXPERTS - 1)
    def _():
        _zero_slack(slack_ref, otail, lambda rr: y_ref.at[rr], tsem.at[1])


def moe_down(seg_start, seg_rows, slack, act, w2, b2):
    grid_spec = pltpu.PrefetchScalarGridSpec(
        num_scalar_prefetch=3,
        grid=(N_EXPERTS,),
        in_specs=[
            pl.BlockSpec(memory_space=pl.ANY),
            pl.BlockSpec(memory_space=pl.ANY),
            pl.BlockSpec((1, 1, D_MODEL), lambda e, s, r, z: (e, 0, 0)),
        ],
        out_specs=pl.BlockSpec(memory_space=pl.ANY),
        scratch_shapes=[
            pltpu.VMEM((2, MOE_CH, D_FF), BF16),
            pltpu.VMEM((2, MOE_CH, HALF), jnp.uint32),
            pltpu.VMEM((SEG_ALIGN, D_FF), BF16),
            pltpu.VMEM((SEG_ALIGN, HALF), jnp.uint32),
            pltpu.VMEM((2, D_FF, D_MODEL), F32),
            pltpu.VMEM((D_FF, D_MODEL), BF16),
            pltpu.SMEM((N_STATE,), jnp.int32),
            pltpu.SemaphoreType.DMA((2,)),
            pltpu.SemaphoreType.DMA((2,)),
            pltpu.SemaphoreType.DMA((2,)),
            pltpu.SemaphoreType.DMA((2,)),
        ],
    )
    return pl.pallas_call(
        _moe_down_kernel,
        grid_spec=grid_spec,
        out_shape=jax.ShapeDtypeStruct((MOE_R, HALF), jnp.uint32),
        compiler_params=pltpu.CompilerParams(
            dimension_semantics=("arbitrary",), vmem_limit_bytes=MOE_VMEM_LIMIT),
        name="moe_down",
    )(seg_start, seg_rows, slack, act, w2, b2.reshape(N_EXPERTS, 1, D_MODEL))


COMB_TM = 128
COMB_NT = N_TOK // COMB_TM


def _combine_kernel(dest_ref, y_ref, w_ref, h_ref, g_ref, o_ref, buf, sem):
    i = pl.program_id(0)

    def fetch(tile, slot):
        def tok(t, carry):
            a = (tile * COMB_TM + t) * TOP_K
            for kk in range(TOP_K):
                pltpu.make_async_copy(y_ref.at[pl.ds(dest_ref[a + kk], 1)],
                                      buf.at[slot, kk, pl.ds(t, 1)], sem.at[slot]).start(
                                          priority=kk % 2)
            return carry

        lax.fori_loop(0, COMB_TM, tok, 0, unroll=4)

    @pl.when(i == 0)
    def _():
        fetch(0, 0)

    slot = lax.rem(i, 2)

    @pl.when(i + 1 < COMB_NT)
    def _():
        fetch(i + 1, 1 - slot)

    for kk in range(TOP_K):
        pltpu.make_async_copy(y_ref.at[pl.ds(0, COMB_TM)], buf.at[slot, kk], sem.at[slot]).wait()
    w = w_ref[...]
    lo = h_ref[:, :HALF]
    hi = h_ref[:, HALF:]
    for kk in range(TOP_K):
        words = buf[slot, kk]
        wk = w[:, kk:kk + 1]
        lo = lo + wk * lax.bitcast_convert_type(lax.shift_left(words, jnp.uint32(16)), F32)
        hi = hi + wk * lax.bitcast_convert_type(words & jnp.uint32(0xFFFF0000), F32)
    ms = (jnp.sum(lo * lo, axis=-1, keepdims=True)
          + jnp.sum(hi * hi, axis=-1, keepdims=True)) * (1.0 / D_MODEL)
    inv = lax.rsqrt(ms + EPS)
    o_ref[:, :HALF] = lo * inv * g_ref[:, :HALF]
    o_ref[:, HALF:] = hi * inv * g_ref[:, HALF:]


def combine(dest, y, weight, h2, gain):
    grid_spec = pltpu.PrefetchScalarGridSpec(
        num_scalar_prefetch=1,
        grid=(COMB_NT,),
        in_specs=[
            pl.BlockSpec(memory_space=pl.ANY),
            pl.BlockSpec((COMB_TM, 128), lambda i, d: (i, 0)),
            pl.BlockSpec((COMB_TM, D_MODEL), lambda i, d: (i, 0)),
            pl.BlockSpec((1, D_MODEL), lambda i, d: (0, 0)),
        ],
        out_specs=pl.BlockSpec((COMB_TM, D_MODEL), lambda i, d: (i, 0)),
        scratch_shapes=[pltpu.VMEM((2, TOP_K, COMB_TM, HALF), jnp.uint32),
                        pltpu.SemaphoreType.DMA((2,))],
    )
    return pl.pallas_call(
        _combine_kernel,
        grid_spec=grid_spec,
        out_shape=jax.ShapeDtypeStruct((N_TOK, D_MODEL), F32),
        compiler_params=_cparams(1),
        name="moe_combine",
    )(dest, y, weight, h2, gain.reshape(1, D_MODEL))


def kernel(x, meta_tokens, rel_bias, norm_mix, w_in, conv_w, gate_bias_m, lambda_params, subln_da,
           w_branch_da, w_branch_m, w_gate, b_gate, w_out, norm_ffn, w_router, b_router,
           w1, b1, w2, b2, norm_final):
    layer = 0
    xn_pad, xn_real = norm_in(x, meta_tokens, norm_mix[layer])
    xn_pad2 = xn_pad.reshape(BATCH * LP, D_MODEL)
    w_in_t = w_in[layer].T
    proj = matmul(xn_pad2, w_in_t, n_cols=PROJ_COLS, tm=768, tn=1024, w_transposed=True,
                  name="proj_in")
    proj3 = proj.reshape(BATCH, LP, PROJ_COLS)
    w_g_t = jnp.pad(w_in_t[COL_M_G:], ((0, 128 - 4 * H_M), (0, 0)))
    mg = matmul(xn_pad2, w_g_t, n_cols=128, tm=768, tn=128, out_dtype=F32, w_transposed=True,
                name="proj_gates")
    gate = matmul(xn_real.reshape(N_TOK, D_MODEL), w_gate[layer], n_cols=2 * D_MODEL,
                  tm=1024, tn=1024, bias=b_gate[layer], act="sigmoid", name="mix_gate")

    tab, consts = _bias_tables(rel_bias)
    y_da = diff_attention(consts, proj3, tab, lambda_params[layer], subln_da[layer])

    qk_m = conv_qk(proj3, conv_w[layer])
    gp = gate_prep(mg.reshape(BATCH, LP, 128), gate_bias_m[layer])
    gp4 = gp.reshape(BATCH, LP, 4, H_M)
    grow = jnp.transpose(gp4, (0, 3, 2, 1))
    swap = lambda t: jnp.swapaxes(t, 1, 2)
    y_m = swap(mlstm(swap(qk_m[:, :, :H_M * DK_M]), qk_m,
                     swap(proj3[:, :, COL_M_V:COL_M_V + H_M * DV_M]),
                     swap(proj3[:, :, COL_M_O:COL_M_O + H_M * DV_M]), grow))

    mixed = branch_mix(y_da.reshape(N_TOK, H_DA * DV_DA), y_m.reshape(N_TOK, H_M * DV_M),
                       w_branch_da[layer], w_branch_m[layer], gate)
    h2 = matmul(mixed, w_out[layer], n_cols=D_MODEL, tm=1024, tn=1024,
                res=x.reshape(N_TOK, D_MODEL), out_dtype=F32, name="out_proj")

    hn_packed, top_e, weight, rank, counts = ffn_prep(
        h2, norm_ffn[layer], w_router[layer], b_router[layer])
    dest, seg_start, seg_rows, pad0, padn, slack = _plan(
        counts, top_e[:, :TOP_K], rank[:, :TOP_K])
    xs = dispatch(dest, pad0, padn, slack, hn_packed)
    act = moe_up(seg_start, seg_rows, slack, xs, w1[layer], b1[layer])
    y = moe_down(seg_start, seg_rows, slack, act, w2[layer], b2[layer])
    out = combine(dest, y, weight, h2, norm_final)
    return out.reshape(BATCH, SEQ, D_MODEL)
```

```python
import functools
import math

import jax
import jax.numpy as jnp
from jax import lax
from jax.experimental import pallas as pl
from jax.experimental.pallas import tpu as pltpu

F32 = jnp.float32
BF16 = jnp.bfloat16

D_MODEL = 2048
BATCH = 2
SEQ = 4096
N_META = 16
BLOCK = 128
PAD = (-N_META) % BLOCK
LP = PAD + N_META + SEQ
NBLK = LP // BLOCK
EPS = 1e-6
NEG_INF = -1e30

H_DA = 4
DK_DA = 128
DV_DA = 256
H_M = 4
DK_M = 128
DV_M = 256
CONV_W = 5
N_BUCKETS = 32
MAX_DISTANCE = 128
N_EXPERTS = 32
TOP_K = 4
D_FF = 2048
SWIGLU_ALPHA = 1.702
SWIGLU_LIMIT = 7.0
LAMBDA_INIT = 0.8 - 0.6 * math.exp(-0.3 * 0)

COL_DA_Q = 0
COL_DA_K = 1024
COL_DA_V = 2048
COL_M_Q = 3072
COL_M_K = 3584
COL_M_V = 4096
COL_M_O = 5120
COL_M_G = 6144
PROJ_COLS = 6144

N_TOK = BATCH * SEQ
N_ASSIGN = N_TOK * TOP_K

VMEM_LIMIT = 52 * 1024 * 1024
MOE_VMEM_LIMIT = 58 * 1024 * 1024


def _cparams(n_axes):
    return pltpu.CompilerParams(
        dimension_semantics=("arbitrary",) * n_axes, vmem_limit_bytes=VMEM_LIMIT)


def _rms(v, gain):
    ms = jnp.mean(v * v, axis=-1, keepdims=True)
    return v * lax.rsqrt(ms + EPS) * gain


def _norm_in_kernel(x_ref, meta_ref, g_ref, pad_ref, real_ref):
    j = pl.program_id(1)
    g = g_ref[...]

    @pl.when(j == 0)
    def _():
        pad_ref[0, :PAD, :] = jnp.zeros((PAD, D_MODEL), BF16)
        pad_ref[0, PAD:, :] = _rms(meta_ref[...], g).astype(BF16)

    @pl.when(j > 0)
    def _():
        y = _rms(x_ref[0], g).astype(BF16)
        pad_ref[0] = y
        real_ref[0] = y


def norm_in(x, meta, gain):
    return pl.pallas_call(
        _norm_in_kernel,
        grid=(BATCH, NBLK),
        in_specs=[
            pl.BlockSpec((1, BLOCK, D_MODEL), lambda b, j: (b, jnp.maximum(j - 1, 0), 0)),
            pl.BlockSpec((N_META, D_MODEL), lambda b, j: (0, 0)),
            pl.BlockSpec((1, D_MODEL), lambda b, j: (0, 0)),
        ],
        out_specs=[
            pl.BlockSpec((1, BLOCK, D_MODEL), lambda b, j: (b, j, 0)),
            pl.BlockSpec((1, BLOCK, D_MODEL), lambda b, j: (b, jnp.maximum(j - 1, 0), 0)),
        ],
        out_shape=[
            jax.ShapeDtypeStruct((BATCH, LP, D_MODEL), BF16),
            jax.ShapeDtypeStruct((BATCH, SEQ, D_MODEL), BF16),
        ],
        compiler_params=_cparams(2),
        name="norm_in",
    )(x, meta, gain.reshape(1, D_MODEL))


def _mm_kernel(*refs, has_bias, has_res, act, w_transposed):
    x_ref, w_ref = refs[0], refs[1]
    pos = 2
    b_ref = r_ref = None
    if has_bias:
        b_ref = refs[pos]
        pos += 1
    if has_res:
        r_ref = refs[pos]
        pos += 1
    o_ref, wbf_ref = refs[pos], refs[pos + 1]

    @pl.when(pl.program_id(1) == 0)
    def _():
        w = w_ref[...]
        wbf_ref[...] = (w.T if w_transposed else w).astype(BF16)

    acc = jnp.dot(x_ref[...], wbf_ref[...], preferred_element_type=F32)
    if has_bias:
        acc = acc + b_ref[...]
    if act == "sigmoid":
        acc = jax.nn.sigmoid(acc)
    if has_res:
        acc = acc + r_ref[...]
    o_ref[...] = acc.astype(o_ref.dtype)


def matmul(x, w, *, n_cols, col_block0=0, tm, tn, bias=None, res=None, act=None,
           out_dtype=BF16, w_transposed=False, name):
    m, k = x.shape
    in_specs = [
        pl.BlockSpec((tm, k), lambda j, i: (i, 0)),
        pl.BlockSpec((tn, k), lambda j, i: (j + col_block0, 0)) if w_transposed
        else pl.BlockSpec((k, tn), lambda j, i: (0, j + col_block0)),
    ]
    args = [x, w]
    if bias is not None:
        in_specs.append(pl.BlockSpec((1, tn), lambda j, i: (0, j)))
        args.append(bias.reshape(1, n_cols))
    if res is not None:
        in_specs.append(pl.BlockSpec((tm, tn), lambda j, i: (i, j)))
        args.append(res)
    return pl.pallas_call(
        functools.partial(_mm_kernel, has_bias=bias is not None, has_res=res is not None, act=act,
                          w_transposed=w_transposed),
        grid=(n_cols // tn, m // tm),
        in_specs=in_specs,
        out_specs=pl.BlockSpec((tm, tn), lambda j, i: (i, j)),
        out_shape=jax.ShapeDtypeStruct((m, n_cols), out_dtype),
        scratch_shapes=[pltpu.VMEM((k, tn), BF16)],
        compiler_params=_cparams(2),
        name=name,
    )(*args)


def _conv_kernel(p_ref, w_ref, o_ref):
    c = pl.program_id(1)
    x = p_ref[0].astype(F32)
    w = w_ref[...]
    half = CONV_W // 2
    acc = w[half:half + 1, :] * x
    for j in range(CONV_W):
        if j != half:
            acc = acc + w[j:j + 1, :] * pltpu.roll(x, (half - j) % LP, axis=0)
    y = acc * jax.nn.sigmoid(acc)
    rows = lax.broadcasted_iota(jnp.int32, (LP, 1), 0)
    y = jnp.where(rows >= PAD, y, 0.0)
    scale = jnp.where(c < 2, DK_M ** -0.5, 1.0).astype(F32)
    o_ref[0] = (y * scale).astype(BF16)


def conv_qk(proj3, conv_w):
    cw = 256
    return pl.pallas_call(
        _conv_kernel,
        grid=(BATCH, (2 * H_M * DK_M) // cw),
        in_specs=[
            pl.BlockSpec((1, LP, cw), lambda b, c: (b, 0, COL_M_Q // cw + c)),
            pl.BlockSpec((CONV_W, cw), lambda b, c: (0, c)),
        ],
        out_specs=pl.BlockSpec((1, LP, cw), lambda b, c: (b, 0, c)),
        out_shape=jax.ShapeDtypeStruct((BATCH, LP, 2 * H_M * DK_M), BF16),
        compiler_params=_cparams(2),
        name="conv_qk",
    )(proj3, conv_w)


def _split_dot(tri, v):
    hi = v.astype(BF16)
    r1 = v - hi.astype(F32)
    mid = r1.astype(BF16)
    lo = (r1 - mid.astype(F32)).astype(BF16)
    return (jnp.dot(tri, hi, preferred_element_type=F32)
            + jnp.dot(tri, mid, preferred_element_type=F32)
            + jnp.dot(tri, lo, preferred_element_type=F32))


def _gate_kernel(g_ref, bias_ref, o_ref):
    ti = lax.broadcasted_iota(jnp.int32, (BLOCK, BLOCK), 0)
    ui = lax.broadcasted_iota(jnp.int32, (BLOCK, BLOCK), 1)
    tril = jnp.where(ui <= ti, 1.0, 0.0).astype(BF16)
    triu = jnp.where(ui >= ti, 1.0, 0.0).astype(BF16)
    ch = lax.broadcasted_iota(jnp.int32, (BLOCK, 4 * H_M), 1)
    typ = lax.shift_right_logical(ch, 2)
    rloc = lax.broadcasted_iota(jnp.int32, (BLOCK, 4 * H_M), 0)

    def body(c, carry):
        r0 = pl.multiple_of(c * BLOCK, BLOCK)
        g = g_ref[0, pl.ds(r0, BLOCK), :][:, :4 * H_M] + bias_ref[...]
        valid = (rloc + r0) >= PAD
        lsig = -(jnp.maximum(-g, 0.0) + jnp.log1p(jnp.exp(-jnp.abs(g))))
        lf = jnp.where(valid, lsig, 0.0)
        cum = _split_dot(tril, lf)
        rcum = _split_dot(triu, lf)
        li = jnp.where(valid, g, -jnp.inf)
        out = jnp.where(typ == 1, cum, jnp.where(typ == 3, rcum, li))
        o_ref[0, pl.ds(r0, BLOCK), :] = out
        return carry

    lax.fori_loop(0, NBLK, body, 0)


def gate_prep(mg3, gate_bias):
    return pl.pallas_call(
        _gate_kernel,
        grid=(BATCH,),
        in_specs=[
            pl.BlockSpec((1, LP, 128), lambda b: (b, 0, 0)),
            pl.BlockSpec((1, 4 * H_M), lambda b: (0, 0)),
        ],
        out_specs=pl.BlockSpec((1, LP, 4 * H_M), lambda b: (b, 0, 0)),
        out_shape=jax.ShapeDtypeStruct((BATCH, LP, 4 * H_M), F32),
        compiler_params=_cparams(1),
        name="gate_prep",
    )(mg3, gate_bias.reshape(1, 4 * H_M))


MLSTM_HP = 2
MLSTM_MID = NBLK // 2


def _mlstm_kernel(qt_ref, k_ref, vt_ref, ot_ref, gr_ref, y_ref,
                  hs_ref, c_ref, n_ref, m_ref):
    c_ref[...] = jnp.zeros_like(c_ref)
    n_ref[...] = jnp.zeros_like(n_ref)
    m_ref[...] = jnp.zeros_like(m_ref)
    si = lax.broadcasted_iota(jnp.int32, (BLOCK, BLOCK), 0)
    ti = lax.broadcasted_iota(jnp.int32, (BLOCK, BLOCK), 1)
    mask_f = si <= ti
    mask_b = si >= ti

    def chain(c, hl, bwd, final):
        idx = 2 * hl + bwd
        r0 = c * BLOCK if isinstance(c, int) else pl.multiple_of(c * BLOCK, BLOCK)
        t_sl = pl.ds(r0, BLOCK)
        qt = qt_ref[0, hl * DK_M:(hl + 1) * DK_M, t_sl]
        k = k_ref[0, t_sl, hl * DK_M:(hl + 1) * DK_M]
        feat = slice(hl * DV_M, (hl + 1) * DV_M)
        vt = vt_ref[0, feat, t_sl]
        gr = gr_ref[0, hl, :, t_sl]
        li_r, b_r = gr[2 * bwd:2 * bwd + 1, :], gr[2 * bwd + 1:2 * bwd + 2, :]
        a_c = jnp.transpose(jnp.broadcast_to(li_r - b_r, (BLOCK, BLOCK)))
        b_end = b_r[:, 0:1] if bwd else b_r[:, BLOCK - 1:BLOCK]
        m_prev = m_ref[idx][:, 0:1]
        ct = c_ref[idx]
        nst = n_ref[idx]
        dmat = jnp.where(mask_b if bwd else mask_f, a_c + b_r, -jnp.inf)
        inter = b_r + m_prev
        m_t = jnp.maximum(inter, jnp.max(dmat, axis=0, keepdims=True))
        w_inter = jnp.exp(inter - m_t)
        st = jnp.dot(k, qt, preferred_element_type=F32) * jnp.exp(dmat - m_t)
        num = (w_inter * jnp.dot(ct.astype(BF16), qt, preferred_element_type=F32)
               + jnp.dot(vt, st.astype(BF16), preferred_element_type=F32))
        nq = jnp.dot(nst.astype(BF16), qt, preferred_element_type=F32)
        den = w_inter * nq + jnp.sum(st, axis=0, keepdims=True)
        h = num * (1.0 / jnp.maximum(jnp.abs(den), jnp.exp(-m_t)))
        if final:
            og = jax.nn.sigmoid(ot_ref[0, feat, t_sl].astype(F32))
            y_ref[0, feat, pl.ds(r0 - BLOCK, BLOCK)] = (og * (hs_ref[feat, t_sl] + h)).astype(BF16)
        else:
            hs_ref[feat, t_sl] = h
        ldec = b_end - b_r + li_r
        m_new = jnp.maximum(b_end + m_prev, jnp.max(ldec, axis=1, keepdims=True))
        w_c = jnp.exp(b_end + m_prev - m_new)
        w_s = jnp.exp(ldec - m_new)
        wvt = (vt.astype(F32) * w_s).astype(BF16)
        c_ref[idx] = w_c * ct + jnp.dot(wvt, k, preferred_element_type=F32)
        n_ref[idx] = w_c * nst + jnp.dot(w_s.astype(BF16), k, preferred_element_type=F32)
        m_ref[idx] = jnp.broadcast_to(m_new, (1, BLOCK))

    def first_half(i, carry):
        for hl in range(MLSTM_HP):
            chain(i, hl, 0, False)
            chain(NBLK - 1 - i, hl, 1, False)
        return carry

    def second_half(i, carry):
        for hl in range(MLSTM_HP):
            chain(i, hl, 0, True)
            chain(NBLK - 1 - i, hl, 1, True)
        return carry

    lax.fori_loop(0, MLSTM_MID, first_half, 0)
    for hl in range(MLSTM_HP):
        chain(MLSTM_MID, hl, 0, False)
        chain(MLSTM_MID, hl, 1, True)
    lax.fori_loop(MLSTM_MID + 1, NBLK - 1, second_half, 0)
    for hl in range(MLSTM_HP):
        chain(NBLK - 1, hl, 0, True)


def mlstm(q_t, qk_m, v_t, o_t, grow):
    hp = MLSTM_HP
    kw, vw = hp * DK_M, hp * DV_M
    return pl.pallas_call(
        _mlstm_kernel,
        grid=(BATCH, H_M // hp),
        in_specs=[
            pl.BlockSpec((1, kw, LP), lambda b, g: (b, g, 0)),
            pl.BlockSpec((1, LP, kw), lambda b, g: (b, 0, (H_M * DK_M) // kw + g)),
            pl.BlockSpec((1, vw, LP), lambda b, g: (b, g, 0)),
            pl.BlockSpec((1, vw, LP), lambda b, g: (b, g, 0)),
            pl.BlockSpec((1, hp, 4, LP), lambda b, g: (b, g, 0, 0)),
        ],
        out_specs=pl.BlockSpec((1, vw, SEQ), lambda b, g: (b, g, 0)),
        out_shape=jax.ShapeDtypeStruct((BATCH, H_M * DV_M, SEQ), BF16),
        scratch_shapes=[
            pltpu.VMEM((vw, LP), F32),
            pltpu.VMEM((2 * hp, DV_M, DK_M), F32),
            pltpu.VMEM((2 * hp, 1, DK_M), F32),
            pltpu.VMEM((2 * hp, 1, BLOCK), F32),
        ],
        compiler_params=_cparams(2),
        name="mlstm",
    )(q_t, qk_m, v_t, o_t, grow)


LOG2E = 1.4426950408889634
ATT_QB = 2
ATT_TQ = ATT_QB * BLOCK
ATT_BAND = (ATT_QB + 2) * BLOCK
ATT_GROUPS = (6, 6, 6, 6, 5)
assert BLOCK >= MAX_DISTANCE and ATT_BAND + sum(ATT_GROUPS) * BLOCK == LP


ATT_NQ = SEQ // ATT_TQ


def _attn_kernel(c_ref, *refs):
    q_refs = refs[:ATT_QB]
    (k1_ref, v1_ref, tab_ref, lam_ref, sg_ref, o_ref,
     s_a, s_b, mx_a, mx_b, k_ref, v_ref) = refs[ATT_QB:]
    h = pl.program_id(1)
    t = pl.program_id(2)

    @pl.when(t == 0)
    def _():
        for rep in range(2):
            k_ref[0, rep * LP:(rep + 1) * LP, :] = k1_ref[0]
            v_ref[0, rep * LP:(rep + 1) * LP, :] = v1_ref[0]

    scale = DK_DA ** -0.5 * LOG2E
    c_neg = c_ref[h, 0]
    c_pos = c_ref[h, 1]

    def groups_of(tile):
        koff = (ATT_QB * tile) * BLOCK
        out = [(koff, ATT_BAND, 0)]
        col = ATT_BAND
        for nblk in ATT_GROUPS:
            out.append((koff + col, nblk * BLOCK, col))
            col += nblk * BLOCK
        return out

    def lane_fold(acc, x, op):
        for j in range(x.shape[1] // BLOCK):
            piece = x[:, j * BLOCK:(j + 1) * BLOCK]
            acc = piece if acc is None else op(acc, piece)
        return acc

    def score(tile, s_ref, mx_ref):
        q = jnp.concatenate([r[0] for r in q_refs], axis=0)
        mx = [None, None]
        for gi, (koff, width, col0) in enumerate(groups_of(tile)):
            koff = pl.multiple_of(koff, BLOCK)
            if gi == 0:
                bias = tab_ref[0, 0]
            else:
                kpos = koff + lax.broadcasted_iota(jnp.int32, (1, width), 1)
                bias = jnp.where(kpos < LP, c_pos, jnp.where(kpos < LP + PAD, NEG_INF, c_neg))
            for m in range(2):
                kk = k_ref[0, pl.ds(koff, width), m * DK_DA:(m + 1) * DK_DA]
                s = lax.dot_general(q[:, m * DK_DA:(m + 1) * DK_DA], kk,
                                    (((1,), (1,)), ((), ())),
                                    preferred_element_type=F32) * scale + bias
                s_ref[m, :, col0:col0 + width] = s
                mx[m] = lane_fold(mx[m], s, jnp.maximum)
        for m in range(2):
            mx_ref[m] = mx[m]

    def finish(tile, s_ref, mx_ref):
        lp = lam_ref[...]
        lam = (jnp.exp(jnp.sum(lp[0:1] * lp[1:2], axis=1, keepdims=True))
               - jnp.exp(jnp.sum(lp[2:3] * lp[3:4], axis=1, keepdims=True)) + LAMBDA_INIT)
        row_max = [jnp.max(mx_ref[m], axis=1, keepdims=True) for m in range(2)]
        lsum = [None, None]
        acc = [None, None]
        for koff, width, col0 in groups_of(tile):
            koff = pl.multiple_of(koff, BLOCK)
            vv = v_ref[0, pl.ds(koff, width), :]
            for m in range(2):
                p = jnp.exp2(s_ref[m, :, col0:col0 + width] - row_max[m])
                lsum[m] = lane_fold(lsum[m], p, jnp.add)
                pv = jnp.dot(p.astype(BF16), vv, preferred_element_type=F32)
                acc[m] = pv if acc[m] is None else acc[m] + pv
        l1 = jnp.sum(lsum[0], axis=1, keepdims=True)
        l2 = jnp.sum(lsum[1], axis=1, keepdims=True)
        o = acc[0] / l1 - lam * (acc[1] / l2)
        o_ref[0] = (_rms(o, sg_ref[...]) * (1.0 - LAMBDA_INIT)).astype(BF16)

    even = lax.rem(t, 2) == 0
    inner = jnp.logical_and(t > 0, t < ATT_NQ)

    @pl.when(t == 0)
    def _():
        score(t, s_a, mx_a)

    @pl.when(jnp.logical_and(inner, even))
    def _():
        score(t, s_a, mx_a)
        finish(t - 1, s_b, mx_b)

    @pl.when(jnp.logical_and(inner, jnp.logical_not(even)))
    def _():
        score(t, s_b, mx_b)
        finish(t - 1, s_a, mx_a)

    @pl.when(t == ATT_NQ)
    def _():
        if (ATT_NQ - 1) % 2 == 0:
            finish(t - 1, s_a, mx_a)
        else:
            finish(t - 1, s_b, mx_b)


def diff_attention(consts, proj3, tab, lam_params, subln):
    nq = ATT_NQ
    kblk0 = COL_DA_K // (2 * DK_DA)
    vblk0 = COL_DA_V // DV_DA
    scored = lambda i: jnp.minimum(i, nq - 1)
    finished = lambda i: jnp.maximum(i - 1, 0)

    def tab_map(b, h, i):
        tile = scored(i)
        case = jnp.where(tile == 0, 0, jnp.where(tile == nq - 1, 2, 1))
        return (h, case, 0, 0)

    return pl.pallas_call(
        _attn_kernel,
        grid=(BATCH, H_DA, nq + 1),
        in_specs=[
            pl.BlockSpec(memory_space=pltpu.SMEM),
            *[pl.BlockSpec((1, BLOCK, 2 * DK_DA),
                           functools.partial(
                               lambda b, h, i, r: (b, ATT_QB * scored(i) + 1 + r, h), r=r))
              for r in range(ATT_QB)],
            pl.BlockSpec((1, LP, 2 * DK_DA), lambda b, h, i: (b, 0, kblk0 + h)),
            pl.BlockSpec((1, LP, DV_DA), lambda b, h, i: (b, 0, vblk0 + h)),
            pl.BlockSpec((1, 1, ATT_TQ, ATT_BAND), tab_map),
            pl.BlockSpec((4, DK_DA), lambda b, h, i: (0, 0)),
            pl.BlockSpec((1, DV_DA), lambda b, h, i: (0, 0)),
        ],
        out_specs=pl.BlockSpec((1, ATT_TQ, DV_DA), lambda b, h, i: (b, finished(i), h)),
        out_shape=jax.ShapeDtypeStruct((BATCH, SEQ, H_DA * DV_DA), BF16),
        scratch_shapes=[pltpu.VMEM((2, ATT_TQ, LP), F32),
                        pltpu.VMEM((2, ATT_TQ, LP), F32),
                        pltpu.VMEM((2, ATT_TQ, BLOCK), F32),
                        pltpu.VMEM((2, ATT_TQ, BLOCK), F32),
                        pltpu.VMEM((1, 2 * LP, 2 * DK_DA), BF16),
                        pltpu.VMEM((1, 2 * LP, DV_DA), BF16)],
        compiler_params=_cparams(3),
        name="diff_attn",
    )(consts, *([proj3] * (ATT_QB + 2)), tab, lam_params, subln.reshape(1, DV_DA))


def _bias_tables(rel_bias):
    rb = rel_bias.astype(F32)
    span = 1024
    assert span >= ATT_TQ + ATT_BAND
    rel = jnp.arange(span, dtype=jnp.int32) - span // 2
    nb = N_BUCKETS // 2
    max_exact = nb // 2
    n = jnp.abs(rel)
    nf = jnp.maximum(n, 1).astype(F32)
    large = max_exact + (jnp.log(nf / max_exact) / math.log(MAX_DISTANCE / max_exact)
                         * (nb - max_exact)).astype(jnp.int32)
    large = jnp.minimum(large, nb - 1)
    bucket = jnp.where(rel > 0, nb, 0) + jnp.where(n < max_exact, n, large)
    hit = bucket[None, :, None] == jnp.arange(N_BUCKETS, dtype=jnp.int32)
    by_rel = jnp.sum(jnp.where(hit, rb.T[:, None, :], 0.0), axis=-1)
    shifted = jnp.tile(by_rel, (1, ATT_TQ))[:, :ATT_TQ * (span - 1)].reshape(
        H_DA, ATT_TQ, span - 1)
    c0 = span // 2 - BLOCK
    gen = shifted[:, :, c0:c0 + ATT_BAND]
    c_neg = rb[nb - 1]
    c_pos = rb[N_BUCKETS - 1]
    jj = jnp.arange(ATT_BAND, dtype=jnp.int32)[None, None, :]
    first = jnp.where(jj < PAD, NEG_INF, gen)
    wrap0 = ATT_BAND - BLOCK
    wrapped = jnp.where(jj - wrap0 < PAD, NEG_INF, c_neg[:, None, None])
    last = jnp.where(jj >= wrap0, wrapped, gen)
    tab = jnp.stack([first, gen, last], axis=1)
    consts = jnp.stack([c_neg, c_pos], axis=1)
    return tab * LOG2E, consts * LOG2E


def _mix_kernel(ya_ref, ym_ref, wa_ref, wm_ref, ga_ref, gm_ref, o_ref, wa_bf, wm_bf):
    @pl.when(pl.program_id(1) == 0)
    def _():
        wa_bf[...] = wa_ref[...].astype(BF16)
        wm_bf[...] = wm_ref[...].astype(BF16)

    a = jnp.dot(ya_ref[...], wa_bf[...], preferred_element_type=F32)
    m = jnp.dot(ym_ref[...], wm_bf[...], preferred_element_type=F32)
    o_ref[...] = (ga_ref[...].astype(F32) * a + gm_ref[...].astype(F32) * m).astype(BF16)


def branch_mix(y_da, y_m, w_da, w_m, gate, *, tm=512, tn=1024):
    m, k = y_da.shape
    nj = D_MODEL // tn
    return pl.pallas_call(
        _mix_kernel,
        grid=(nj, m // tm),
        in_specs=[
            pl.BlockSpec((tm, k), lambda j, i: (i, 0)),
            pl.BlockSpec((tm, k), lambda j, i: (i, 0)),
            pl.BlockSpec((k, tn), lambda j, i: (0, j)),
            pl.BlockSpec((k, tn), lambda j, i: (0, j)),
            pl.BlockSpec((tm, tn), lambda j, i: (i, j)),
            pl.BlockSpec((tm, tn), lambda j, i: (i, nj + j)),
        ],
        out_specs=pl.BlockSpec((tm, tn), lambda j, i: (i, j)),
        out_shape=jax.ShapeDtypeStruct((m, D_MODEL), BF16),
        scratch_shapes=[pltpu.VMEM((k, tn), BF16), pltpu.VMEM((k, tn), BF16)],
        compiler_params=_cparams(2),
        name="branch_mix",
    )(y_da, y_m, w_da, w_m, gate, gate)


FFN_TM = 1024
HALF = D_MODEL // 2


def _pack_bf16_pairs(v):
    lo = lax.bitcast_convert_type(v[:, :HALF].astype(BF16).astype(F32), jnp.uint32)
    hi = lax.bitcast_convert_type(v[:, HALF:].astype(BF16).astype(F32), jnp.uint32)
    return (hi & jnp.uint32(0xFFFF0000)) | lax.shift_right_logical(lo, jnp.uint32(16))


def _unpack_bf16_pairs(w):
    lo = lax.bitcast_convert_type(lax.shift_left(w, jnp.uint32(16)), F32).astype(BF16)
    hi = lax.bitcast_convert_type(w & jnp.uint32(0xFFFF0000), F32).astype(BF16)
    return lo, hi


def _ffn_prep_kernel(h_ref, g_ref, wr_ref, br_ref, hn_ref, e_ref, w_ref, r_ref, cnt_ref, base_ref):
    @pl.when(pl.program_id(0) == 0)
    def _():
        base_ref[...] = jnp.zeros_like(base_ref)

    hn = _rms(h_ref[...], g_ref[...])
    hn_ref[...] = _pack_bf16_pairs(hn)
    logits = jnp.dot(hn.astype(BF16), wr_ref[...].astype(BF16),
                     preferred_element_type=F32) + br_ref[...]
    lane = lax.broadcasted_iota(jnp.int32, (FFN_TM, N_EXPERTS), 1)
    lane_o = lax.broadcasted_iota(jnp.int32, (FFN_TM, 128), 1)
    ti = lax.broadcasted_iota(jnp.int32, (FFN_TM, FFN_TM), 0)
    ui = lax.broadcasted_iota(jnp.int32, (FFN_TM, FFN_TM), 1)
    tril = jnp.where(ui <= ti, 1.0, 0.0).astype(BF16)
    e_out = jnp.zeros((FFN_TM, 128), jnp.int32)
    r_out = jnp.zeros((FFN_TM, 128), jnp.int32)
    l_out = jnp.full((FFN_TM, 128), -jnp.inf, F32)
    base = base_ref[...]
    l = logits
    for kk in range(TOP_K):
        mk = jnp.max(l, axis=1, keepdims=True)
        ik = jnp.min(jnp.where(l == mk, lane, N_EXPERTS), axis=1, keepdims=True)
        hit = lane == ik
        oh = jnp.where(hit, 1.0, 0.0)
        cum = jnp.dot(tril, oh.astype(BF16), preferred_element_type=F32)
        rank = jnp.sum(oh * (cum + base), axis=1, keepdims=True) - 1.0
        base = base + jnp.sum(oh, axis=0, keepdims=True)
        e_out = jnp.where(lane_o == kk, ik, e_out)
        r_out = jnp.where(lane_o == kk, rank.astype(jnp.int32), r_out)
        l_out = jnp.where(lane_o == kk, mk, l_out)
        l = jnp.where(hit, -jnp.inf, l)
    base_ref[...] = base
    cnt_ref[...] = base
    ex = jnp.exp(l_out - jnp.max(l_out, axis=1, keepdims=True))
    e_ref[...] = e_out
    r_ref[...] = r_out
    w_ref[...] = ex / jnp.sum(ex, axis=1, keepdims=True)


def ffn_prep(h2, gain, w_router, b_router):
    row = lambda i: (i, 0)
    fixed = lambda i: (0, 0)
    return pl.pallas_call(
        _ffn_prep_kernel,
        grid=(N_TOK // FFN_TM,),
        in_specs=[
            pl.BlockSpec((FFN_TM, D_MODEL), row),
            pl.BlockSpec((1, D_MODEL), fixed),
            pl.BlockSpec((D_MODEL, N_EXPERTS), fixed),
            pl.BlockSpec((1, N_EXPERTS), fixed),
        ],
        out_specs=[
            pl.BlockSpec((FFN_TM, HALF), row),
            pl.BlockSpec((FFN_TM, 128), row),
            pl.BlockSpec((FFN_TM, 128), row),
            pl.BlockSpec((FFN_TM, 128), row),
            pl.BlockSpec((1, N_EXPERTS), fixed),
        ],
        out_shape=[
            jax.ShapeDtypeStruct((N_TOK, HALF), jnp.uint32),
            jax.ShapeDtypeStruct((N_TOK, 128), jnp.int32),
            jax.ShapeDtypeStruct((N_TOK, 128), F32),
            jax.ShapeDtypeStruct((N_TOK, 128), jnp.int32),
            jax.ShapeDtypeStruct((1, N_EXPERTS), F32),
        ],
        scratch_shapes=[pltpu.VMEM((1, N_EXPERTS), F32)],
        compiler_params=_cparams(1),
        name="ffn_prep",
    )(h2, gain.reshape(1, D_MODEL), w_router, b_router.reshape(1, N_EXPERTS))


SEG_ALIGN = 128
MOE_R = N_ASSIGN + N_EXPERTS * SEG_ALIGN


def _plan(counts_f, top_e, rank):
    counts = counts_f[0].astype(jnp.int32)
    seg_rows = (counts + SEG_ALIGN - 1) // SEG_ALIGN * SEG_ALIGN
    seg_start = jnp.cumsum(seg_rows) - seg_rows
    eq = top_e[:, :, None] == jnp.arange(N_EXPERTS, dtype=jnp.int32)[None, None, :]
    dest = jnp.sum(jnp.where(eq, seg_start[None, None, :], 0), axis=-1) + rank
    used = jnp.sum(seg_rows)
    slack = jnp.stack([used, (MOE_R - used) // SEG_ALIGN])
    return (dest.reshape(N_ASSIGN).astype(jnp.int32), seg_start.astype(jnp.int32),
            seg_rows.astype(jnp.int32), (seg_start + counts).astype(jnp.int32),
            (seg_rows - counts).astype(jnp.int32), slack.astype(jnp.int32))


def _zero_slack(slack_ref, zero_block, dst_rows, sem):
    zero_block[...] = jnp.zeros_like(zero_block)

    def copy(j):
        r0 = pl.multiple_of(slack_ref[0] + j * SEG_ALIGN, SEG_ALIGN)
        return pltpu.make_async_copy(zero_block, dst_rows(pl.ds(r0, SEG_ALIGN)), sem)

    def start(j, carry):
        copy(j).start()
        return carry

    def wait(j, carry):
        copy(j).wait()
        return carry

    lax.fori_loop(0, slack_ref[1], start, 0)
    lax.fori_loop(0, slack_ref[1], wait, 0)


DISP_TOK = 256


def _dispatch_kernel(dest_ref, pad0_ref, padn_ref, slack_ref, hn_ref, xs_ref,
                     zrow_ref, zblk_ref, sem, zsem):
    i = pl.program_id(0)

    @pl.when(i == 0)
    def _():
        zrow_ref[...] = jnp.zeros_like(zrow_ref)
        _zero_slack(slack_ref, zblk_ref, lambda rows: xs_ref.at[rows], zsem.at[0])

        def expert(e, carry):
            p0 = pad0_ref[e]
            pn = padn_ref[e]

            def zstart(r, c2):
                pltpu.make_async_copy(zrow_ref, xs_ref.at[pl.ds(p0 + r, 1)], zsem.at[0]).start()
                return c2

            def zwait(r, c2):
                pltpu.make_async_copy(zrow_ref, xs_ref.at[pl.ds(p0, 1)], zsem.at[0]).wait()
                return c2

            lax.fori_loop(0, pn, zstart, 0)
            lax.fori_loop(0, pn, zwait, 0)
            return carry

        lax.fori_loop(0, N_EXPERTS, expert, 0)

    def tok(t, carry):
        a = (i * DISP_TOK + t) * TOP_K
        for kk in range(TOP_K):
            pltpu.make_async_copy(hn_ref.at[pl.ds(t, 1)], xs_ref.at[pl.ds(dest_ref[a + kk], 1)],
                                  sem.at[0]).start(priority=kk % 2)
        return carry

    lax.fori_loop(0, DISP_TOK, tok, 0, unroll=4)
    for kk in range(TOP_K):
        pltpu.make_async_copy(hn_ref, xs_ref.at[pl.ds(0, DISP_TOK)], sem.at[0]).wait()


def dispatch(dest, pad0, padn, slack, hn_packed):
    grid_spec = pltpu.PrefetchScalarGridSpec(
        num_scalar_prefetch=4,
        grid=(N_TOK // DISP_TOK,),
        in_specs=[pl.BlockSpec((DISP_TOK, HALF), lambda i, d, p0, pn, z: (i, 0))],
        out_specs=pl.BlockSpec(memory_space=pl.ANY),
        scratch_shapes=[pltpu.VMEM((1, HALF), jnp.uint32),
                        pltpu.VMEM((SEG_ALIGN, HALF), jnp.uint32),
                        pltpu.SemaphoreType.DMA((1,)), pltpu.SemaphoreType.DMA((1,))],
    )
    return pl.pallas_call(
        _dispatch_kernel,
        grid_spec=grid_spec,
        out_shape=jax.ShapeDtypeStruct((MOE_R, HALF), jnp.uint32),
        compiler_params=_cparams(1),
        name="moe_dispatch",
    )(dest, pad0, padn, slack, hn_packed)


MOE_CH = 2 * SEG_ALIGN
MOE_TF = 1024
MOE_NF = D_FF // MOE_TF


class _CopyGroup:
    def __init__(self, copies):
        self.copies = copies

    def start(self, priority=0):
        for cp in self.copies:
            cp.start(priority=priority)

    def wait(self):
        for cp in self.copies:
            cp.wait()


W_PIECES = 8
W_PER_CHUNK = 3


N_STATE = 4


def _stream_rows(step, n_steps, start, rows, next_start, next_rows, state,
                 make_in, make_out, make_tail_in, make_tail_out,
                 compute_chunk, compute_tail, before_first_wait, next_weight_piece):
    has_next = step + 1 < n_steps

    def request_weights(first, count):
        def one(p, carry):
            @pl.when(jnp.logical_and(has_next, p < W_PIECES))
            def _():
                next_weight_piece(p).start()
            return carry

        lax.fori_loop(first, first + count, one, 0)

    @pl.when(step == 0)
    def _():
        for j in range(N_STATE):
            state[j] = 0

    n_ch = lax.shift_right_logical(rows, MOE_CH.bit_length() - 1)
    tail = rows - n_ch * MOE_CH
    tail_row = start + n_ch * MOE_CH
    g0 = state[0]
    feeds_next = jnp.logical_and(step + 1 < n_steps, next_rows >= MOE_CH)

    def chunk_row(c):
        return start + c * MOE_CH

    @pl.when(tail > 0)
    def _():
        make_tail_in(tail_row).start(priority=1)

    @pl.when(jnp.logical_and(n_ch > 0, state[3] == 0))
    def _():
        make_in(start, lax.rem(g0, 2)).start(priority=1)

    before_first_wait()

    @pl.when(tail > 0)
    def _():
        make_tail_in(tail_row).wait()
        compute_tail()
        make_tail_out(tail_row).start(priority=1)

    def body(c, carry):
        slot = lax.rem(g0 + c, 2)
        make_in(chunk_row(c), slot).wait()

        @pl.when(c + 1 < n_ch)
        def _():
            make_in(chunk_row(c + 1), 1 - slot).start(priority=1)

        @pl.when(jnp.logical_and(c + 1 == n_ch, feeds_next))
        def _():
            make_in(next_start, 1 - slot).start(priority=1)

        request_weights(c * W_PER_CHUNK, W_PER_CHUNK)

        @pl.when(state[1 + slot] == 1)
        def _():
            make_out(chunk_row(c), slot).wait()

        compute_chunk(slot)
        make_out(chunk_row(c), slot).start(priority=1)
        state[1 + slot] = 1
        return carry

    lax.fori_loop(0, n_ch, body, 0)
    request_weights(n_ch * W_PER_CHUNK, W_PIECES)
    state[0] = g0 + n_ch
    state[3] = jnp.where(jnp.logical_and(n_ch > 0, feeds_next), 1, 0)

    @pl.when(tail > 0)
    def _():
        make_tail_out(tail_row).wait()

    @pl.when(step == n_steps - 1)
    def _():
        for slot in range(2):
            @pl.when(state[1 + slot] == 1)
            def _():
                make_out(0, slot).wait()
                state[1 + slot] = 0


def _rows_at(row0, n):
    return pl.ds(row0 if isinstance(row0, int) else pl.multiple_of(row0, SEG_ALIGN), n)


def _moe_up_kernel(seg_ref, rows_ref, slack_ref, xs_ref, w_hbm, bg_ref, bl_ref, act_ref,
                   xbuf, obuf, xtail, otail, wbuf, wg_bf, wl_bf, state, isem, osem, tsem, wsem):
    f = pl.program_id(0)
    e = pl.program_id(1)
    start = seg_ref[e]
    rows = rows_ref[e]
    e_next = lax.rem(e + 1, N_EXPERTS)
    f_next = jnp.where(e == N_EXPERTS - 1, f + 1, f)
    step = f * N_EXPERTS + e
    n_steps = MOE_NF * N_EXPERTS
    wslot = lax.rem(step, 2)

    def weight_piece(expert, ftile, slot, p):
        per_half = W_PIECES // 2
        band = D_MODEL // per_half
        t = p // per_half if isinstance(p, int) else lax.shift_right_logical(
            p, per_half.bit_length() - 1)
        r0 = (p - t * per_half) * band
        r0 = r0 if isinstance(r0, int) else pl.multiple_of(r0, band)
        col0 = pl.multiple_of((t * MOE_NF + ftile) * MOE_TF, MOE_TF)
        return pltpu.make_async_copy(
            w_hbm.at[expert, pl.ds(r0, band), pl.ds(col0, MOE_TF)],
            wbuf.at[slot, t, pl.ds(r0, band)], wsem.at[slot, p])

    @pl.when(step == 0)
    def _():
        for p in range(W_PIECES):
            weight_piece(e, f, wslot, p).start()

    def make_in(r0, slot):
        return pltpu.make_async_copy(xs_ref.at[_rows_at(r0, MOE_CH)], xbuf.at[slot], isem.at[slot])

    def make_out(r0, slot):
        return pltpu.make_async_copy(obuf.at[slot], act_ref.at[f, _rows_at(r0, MOE_CH)],
                                     osem.at[slot])

    def make_tail_in(r0):
        return pltpu.make_async_copy(xs_ref.at[_rows_at(r0, SEG_ALIGN)], xtail, tsem.at[0])

    def make_tail_out(r0):
        return pltpu.make_async_copy(otail, act_ref.at[f, _rows_at(r0, SEG_ALIGN)], tsem.at[1])

    def cast_weights():
        per_half = W_PIECES // 2
        band = D_MODEL // per_half
        for p in range(W_PIECES):
            weight_piece(e, f, wslot, p).wait()
            t, rows_p = p // per_half, pl.ds((p % per_half) * band, band)
            (wg_bf, wl_bf)[t][rows_p, :] = wbuf[wslot, t, rows_p, :].astype(BF16)

    def expert_mlp(words):
        lo, hi = _unpack_bf16_pairs(words)
        glu = (jnp.dot(lo, wg_bf[:HALF, :], preferred_element_type=F32)
               + jnp.dot(hi, wg_bf[HALF:, :], preferred_element_type=F32) + bg_ref[0])
        lin = (jnp.dot(lo, wl_bf[:HALF, :], preferred_element_type=F32)
               + jnp.dot(hi, wl_bf[HALF:, :], preferred_element_type=F32) + bl_ref[0])
        glu = jnp.minimum(glu, SWIGLU_LIMIT)
        lin = jnp.clip(lin, -SWIGLU_LIMIT, SWIGLU_LIMIT)
        return (glu * jax.nn.sigmoid(SWIGLU_ALPHA * glu) * (lin + 1.0)).astype(BF16)

    def compute_chunk(slot):
        obuf[slot] = expert_mlp(xbuf[slot])

    def compute_tail():
        otail[...] = expert_mlp(xtail[...])

    _stream_rows(step, n_steps, start, rows,
                 seg_ref[e_next], rows_ref[e_next], state,
                 make_in, make_out, make_tail_in, make_tail_out,
                 compute_chunk, compute_tail, cast_weights,
                 lambda p: weight_piece(e_next, f_next, 1 - wslot, p))

    @pl.when(e == N_EXPERTS - 1)
    def _():
        _zero_slack(slack_ref, otail, lambda rr: act_ref.at[f, rr], tsem.at[1])


def moe_up(seg_start, seg_rows, slack, xs, w1, b1):
    grid_spec = pltpu.PrefetchScalarGridSpec(
        num_scalar_prefetch=3,
        grid=(MOE_NF, N_EXPERTS),
        in_specs=[
            pl.BlockSpec(memory_space=pl.ANY),
            pl.BlockSpec(memory_space=pl.ANY),
            pl.BlockSpec((1, 1, MOE_TF), lambda f, e, s, r, z: (e, 0, f)),
            pl.BlockSpec((1, 1, MOE_TF), lambda f, e, s, r, z: (e, 0, MOE_NF + f)),
        ],
        out_specs=pl.BlockSpec(memory_space=pl.ANY),
        scratch_shapes=[
            pltpu.VMEM((2, MOE_CH, HALF), jnp.uint32),
            pltpu.VMEM((2, MOE_CH, MOE_TF), BF16),
            pltpu.VMEM((SEG_ALIGN, HALF), jnp.uint32),
            pltpu.VMEM((SEG_ALIGN, MOE_TF), BF16),
            pltpu.VMEM((2, 2, D_MODEL, MOE_TF), F32),
            pltpu.VMEM((D_MODEL, MOE_TF), BF16),
            pltpu.VMEM((D_MODEL, MOE_TF), BF16),
            pltpu.SMEM((N_STATE,), jnp.int32),
            pltpu.SemaphoreType.DMA((2,)),
            pltpu.SemaphoreType.DMA((2,)),
            pltpu.SemaphoreType.DMA((2,)),
            pltpu.SemaphoreType.DMA((2, W_PIECES)),
        ],
    )
    b13 = b1.reshape(N_EXPERTS, 1, 2 * D_FF)
    return pl.pallas_call(
        _moe_up_kernel,
        grid_spec=grid_spec,
        out_shape=jax.ShapeDtypeStruct((MOE_NF, MOE_R, MOE_TF), BF16),
        compiler_params=pltpu.CompilerParams(
            dimension_semantics=("arbitrary", "arbitrary"), vmem_limit_bytes=MOE_VMEM_LIMIT),
        name="moe_up",
    )(seg_start, seg_rows, slack, xs, w1, b13, b13)


def _moe_down_kernel(seg_ref, rows_ref, slack_ref, act_ref, w_hbm, b_ref, y_ref,
                     xbuf, obuf, xtail, otail, wbuf, w_bf, state, isem, osem, tsem, wsem):
    e = pl.program_id(0)
    start = seg_ref[e]
    rows = rows_ref[e]
    e_next = lax.rem(e + 1, N_EXPERTS)
    wslot = lax.rem(e, 2)

    def weight_piece(expert, slot, p):
        band = D_FF // W_PIECES
        r0 = p * band if isinstance(p, int) else pl.multiple_of(p * band, band)
        return pltpu.make_async_copy(w_hbm.at[expert, pl.ds(r0, band)],
                                     wbuf.at[slot, pl.ds(r0, band)], wsem.at[slot, p])

    @pl.when(e == 0)
    def _():
        for p in range(W_PIECES):
            weight_piece(e, wslot, p).start()

    def make_in(r0, slot):
        return _CopyGroup([
            pltpu.make_async_copy(act_ref.at[j, _rows_at(r0, MOE_CH)],
                                  xbuf.at[slot, :, pl.ds(j * MOE_TF, MOE_TF)], isem.at[slot])
            for j in range(MOE_NF)])

    def make_out(r0, slot):
        return pltpu.make_async_copy(obuf.at[slot], y_ref.at[_rows_at(r0, MOE_CH)], osem.at[slot])

    def make_tail_in(r0):
        return _CopyGroup([
            pltpu.make_async_copy(act_ref.at[j, _rows_at(r0, SEG_ALIGN)],
                                  xtail.at[:, pl.ds(j * MOE_TF, MOE_TF)], tsem.at[0])
            for j in range(MOE_NF)])

    def make_tail_out(r0):
        return pltpu.make_async_copy(otail, y_ref.at[_rows_at(r0, SEG_ALIGN)], tsem.at[1])

    def cast_weights():
        band = D_FF // W_PIECES
        for p in range(W_PIECES):
            weight_piece(e, wslot, p).wait()
            rows_p = pl.ds(p * band, band)
            w_bf[rows_p, :] = wbuf[wslot, rows_p, :].astype(BF16)

    def expert_out(a):
        return _pack_bf16_pairs(jnp.dot(a, w_bf[...], preferred_element_type=F32) + b_ref[0])

    def compute_chunk(slot):
        obuf[slot] = expert_out(xbuf[slot])

    def compute_tail():
        otail[...] = expert_out(xtail[...])

    _stream_rows(e, N_EXPERTS, start, rows, seg_ref[e_next], rows_ref[e_next], state,
                 make_in, make_out, make_tail_in, make_tail_out,
                 compute_chunk, compute_tail, cast_weights,
                 lambda p: weight_piece(e_next, 1 - wslot, p))

    @pl.when(e == N_EXPERTS - 1)
    def _():
        _zero_slack(slack_ref, otail, lambda rr: y_ref.at[rr], tsem.at[1])


def moe_down(seg_start, seg_rows, slack, act, w2, b2):
    grid_spec = pltpu.PrefetchScalarGridSpec(
        num_scalar_prefetch=3,
        grid=(N_EXPERTS,),
        in_specs=[
            pl.BlockSpec(memory_space=pl.ANY),
            pl.BlockSpec(memory_space=pl.ANY),
            pl.BlockSpec((1, 1, D_MODEL), lambda e, s, r, z: (e, 0, 0)),
        ],
        out_specs=pl.BlockSpec(memory_space=pl.ANY),
        scratch_shapes=[
            pltpu.VMEM((2, MOE_CH, D_FF), BF16),
            pltpu.VMEM((2, MOE_CH, HALF), jnp.uint32),
            pltpu.VMEM((SEG_ALIGN, D_FF), BF16),
            pltpu.VMEM((SEG_ALIGN, HALF), jnp.uint32),
            pltpu.VMEM((2, D_FF, D_MODEL), F32),
            pltpu.VMEM((D_FF, D_MODEL), BF16),
            pltpu.SMEM((N_STATE,), jnp.int32),
            pltpu.SemaphoreType.DMA((2,)),
            pltpu.SemaphoreType.DMA((2,)),
            pltpu.SemaphoreType.DMA((2,)),
            pltpu.SemaphoreType.DMA((2, W_PIECES)),
        ],
    )
    return pl.pallas_call(
        _moe_down_kernel,
        grid_spec=grid_spec,
        out_shape=jax.ShapeDtypeStruct((MOE_R, HALF), jnp.uint32),
        compiler_params=pltpu.CompilerParams(
            dimension_semantics=("arbitrary",), vmem_limit_bytes=MOE_VMEM_LIMIT),
        name="moe_down",
    )(seg_start, seg_rows, slack, act, w2, b2.reshape(N_EXPERTS, 1, D_MODEL))


COMB_TM = 256
COMB_NT = N_TOK // COMB_TM


def _combine_kernel(dest_ref, y_ref, w_ref, h_ref, g_ref, o_ref, buf, sem):
    i = pl.program_id(0)

    def fetch(tile, slot):
        def tok(t, carry):
            a = (tile * COMB_TM + t) * TOP_K
            for kk in range(TOP_K):
                pltpu.make_async_copy(y_ref.at[pl.ds(dest_ref[a + kk], 1)],
                                      buf.at[slot, kk, pl.ds(t, 1)], sem.at[slot]).start(
                                          priority=kk % 2)
            return carry

        lax.fori_loop(0, COMB_TM, tok, 0, unroll=4)

    @pl.when(i == 0)
    def _():
        fetch(0, 0)

    slot = lax.rem(i, 2)

    @pl.when(i + 1 < COMB_NT)
    def _():
        fetch(i + 1, 1 - slot)

    for kk in range(TOP_K):
        pltpu.make_async_copy(y_ref.at[pl.ds(0, COMB_TM)], buf.at[slot, kk], sem.at[slot]).wait()
    w = w_ref[...]
    lo = h_ref[:, :HALF]
    hi = h_ref[:, HALF:]
    for kk in range(TOP_K):
        words = buf[slot, kk]
        wk = w[:, kk:kk + 1]
        lo = lo + wk * lax.bitcast_convert_type(lax.shift_left(words, jnp.uint32(16)), F32)
        hi = hi + wk * lax.bitcast_convert_type(words & jnp.uint32(0xFFFF0000), F32)
    ms = (jnp.sum(lo * lo, axis=-1, keepdims=True)
          + jnp.sum(hi * hi, axis=-1, keepdims=True)) * (1.0 / D_MODEL)
    inv = lax.rsqrt(ms + EPS)
    o_ref[:, :HALF] = lo * inv * g_ref[:, :HALF]
    o_ref[:, HALF:] = hi * inv * g_ref[:, HALF:]


def combine(dest, y, weight, h2, gain):
    grid_spec = pltpu.PrefetchScalarGridSpec(
        num_scalar_prefetch=1,
        grid=(COMB_NT,),
        in_specs=[
            pl.BlockSpec(memory_space=pl.ANY),
            pl.BlockSpec((COMB_TM, 128), lambda i, d: (i, 0)),
            pl.BlockSpec((COMB_TM, D_MODEL), lambda i, d: (i, 0)),
            pl.BlockSpec((1, D_MODEL), lambda i, d: (0, 0)),
        ],
        out_specs=pl.BlockSpec((COMB_TM, D_MODEL), lambda i, d: (i, 0)),
        scratch_shapes=[pltpu.VMEM((2, TOP_K, COMB_TM, HALF), jnp.uint32),
                        pltpu.SemaphoreType.DMA((2,))],
    )
    return pl.pallas_call(
        _combine_kernel,
        grid_spec=grid_spec,
        out_shape=jax.ShapeDtypeStruct((N_TOK, D_MODEL), F32),
        compiler_params=_cparams(1),
        name="moe_combine",
    )(dest, y, weight, h2, gain.reshape(1, D_MODEL))


def kernel(x, meta_tokens, rel_bias, norm_mix, w_in, conv_w, gate_bias_m, lambda_params, subln_da,
           w_branch_da, w_branch_m, w_gate, b_gate, w_out, norm_ffn, w_router, b_router,
           w1, b1, w2, b2, norm_final):
    layer = 0
    xn_pad, xn_real = norm_in(x, meta_tokens, norm_mix[layer])
    xn_pad2 = xn_pad.reshape(BATCH * LP, D_MODEL)
    w_in_t = w_in[layer].T
    proj = matmul(xn_pad2, w_in_t, n_cols=PROJ_COLS, tm=768, tn=1024, w_transposed=True,
                  name="proj_in")
    proj3 = proj.reshape(BATCH, LP, PROJ_COLS)
    w_g_t = jnp.pad(w_in_t[COL_M_G:], ((0, 128 - 4 * H_M), (0, 0)))
    mg = matmul(xn_pad2, w_g_t, n_cols=128, tm=768, tn=128, out_dtype=F32, w_transposed=True,
                name="proj_gates")
    gate = matmul(xn_real.reshape(N_TOK, D_MODEL), w_gate[layer], n_cols=2 * D_MODEL,
                  tm=1024, tn=1024, bias=b_gate[layer], act="sigmoid", name="mix_gate")

    tab, consts = _bias_tables(rel_bias)
    y_da = diff_attention(consts, proj3, tab, lambda_params[layer], subln_da[layer])

    qk_m = conv_qk(proj3, conv_w[layer])
    gp = gate_prep(mg.reshape(BATCH, LP, 128), gate_bias_m[layer])
    gp4 = gp.reshape(BATCH, LP, 4, H_M)
    grow = jnp.transpose(gp4, (0, 3, 2, 1))
    swap = lambda t: jnp.swapaxes(t, 1, 2)
    y_m = swap(mlstm(swap(qk_m[:, :, :H_M * DK_M]), qk_m,
                     swap(proj3[:, :, COL_M_V:COL_M_V + H_M * DV_M]),
                     swap(proj3[:, :, COL_M_O:COL_M_O + H_M * DV_M]), grow))

    mixed = branch_mix(y_da.reshape(N_TOK, H_DA * DV_DA), y_m.reshape(N_TOK, H_M * DV_M),
                       w_branch_da[layer], w_branch_m[layer], gate)
    h2 = matmul(mixed, w_out[layer], n_cols=D_MODEL, tm=1024, tn=1024,
                res=x.reshape(N_TOK, D_MODEL), out_dtype=F32, name="out_proj")

    hn_packed, top_e, weight, rank, counts = ffn_prep(
        h2, norm_ffn[layer], w_router[layer], b_router[layer])
    dest, seg_start, seg_rows, pad0, padn, slack = _plan(
        counts, top_e[:, :TOP_K], rank[:, :TOP_K])
    xs = dispatch(dest, pad0, padn, slack, hn_packed)
    act = moe_up(seg_start, seg_rows, slack, xs, w1[layer], b1[layer])
    y = moe_down(seg_start, seg_rows, slack, act, w2[layer], b2[layer])
    out = combine(dest, y, weight, h2, norm_final)
    return out.reshape(BATCH, SEQ, D_MODEL)
```

```python
import functools
import math

import jax
import jax.numpy as jnp
from jax import lax
from jax.experimental import pallas as pl
from jax.experimental.pallas import tpu as pltpu

F32 = jnp.float32
BF16 = jnp.bfloat16

D_MODEL = 2048
BATCH = 2
SEQ = 4096
N_META = 16
BLOCK = 128
PAD = (-N_META) % BLOCK
LP = PAD + N_META + SEQ
NBLK = LP // BLOCK
EPS = 1e-6
NEG_INF = -1e30

H_DA = 4
DK_DA = 128
DV_DA = 256
H_M = 4
DK_M = 128
DV_M = 256
CONV_W = 5
N_BUCKETS = 32
MAX_DISTANCE = 128
N_EXPERTS = 32
TOP_K = 4
D_FF = 2048
SWIGLU_ALPHA = 1.702
SWIGLU_LIMIT = 7.0
LAMBDA_INIT = 0.8 - 0.6 * math.exp(-0.3 * 0)

COL_DA_Q = 0
COL_DA_K = 1024
COL_DA_V = 2048
COL_M_Q = 3072
COL_M_K = 3584
COL_M_V = 4096
COL_M_O = 5120
COL_M_G = 6144
PROJ_COLS = 6144

N_TOK = BATCH * SEQ
N_ASSIGN = N_TOK * TOP_K

VMEM_LIMIT = 52 * 1024 * 1024
MOE_VMEM_LIMIT = 58 * 1024 * 1024


def _cparams(n_axes):
    return pltpu.CompilerParams(
        dimension_semantics=("arbitrary",) * n_axes, vmem_limit_bytes=VMEM_LIMIT)


def _rms(v, gain):
    ms = jnp.mean(v * v, axis=-1, keepdims=True)
    return v * lax.rsqrt(ms + EPS) * gain


def _norm_in_kernel(x_ref, meta_ref, g_ref, pad_ref, real_ref):
    j = pl.program_id(1)
    g = g_ref[...]

    @pl.when(j == 0)
    def _():
        pad_ref[0, :PAD, :] = jnp.zeros((PAD, D_MODEL), BF16)
        pad_ref[0, PAD:, :] = _rms(meta_ref[...], g).astype(BF16)

    @pl.when(j > 0)
    def _():
        y = _rms(x_ref[0], g).astype(BF16)
        pad_ref[0] = y
        real_ref[0] = y


def norm_in(x, meta, gain):
    return pl.pallas_call(
        _norm_in_kernel,
        grid=(BATCH, NBLK),
        in_specs=[
            pl.BlockSpec((1, BLOCK, D_MODEL), lambda b, j: (b, jnp.maximum(j - 1, 0), 0)),
            pl.BlockSpec((N_META, D_MODEL), lambda b, j: (0, 0)),
            pl.BlockSpec((1, D_MODEL), lambda b, j: (0, 0)),
        ],
        out_specs=[
            pl.BlockSpec((1, BLOCK, D_MODEL), lambda b, j: (b, j, 0)),
            pl.BlockSpec((1, BLOCK, D_MODEL), lambda b, j: (b, jnp.maximum(j - 1, 0), 0)),
        ],
        out_shape=[
            jax.ShapeDtypeStruct((BATCH, LP, D_MODEL), BF16),
            jax.ShapeDtypeStruct((BATCH, SEQ, D_MODEL), BF16),
        ],
        compiler_params=_cparams(2),
        name="norm_in",
    )(x, meta, gain.reshape(1, D_MODEL))


def _mm_kernel(*refs, has_bias, has_res, act, w_transposed):
    x_ref, w_ref = refs[0], refs[1]
    pos = 2
    b_ref = r_ref = None
    if has_bias:
        b_ref = refs[pos]
        pos += 1
    if has_res:
        r_ref = refs[pos]
        pos += 1
    o_ref, wbf_ref = refs[pos], refs[pos + 1]

    @pl.when(pl.program_id(1) == 0)
    def _():
        w = w_ref[...]
        wbf_ref[...] = (w.T if w_transposed else w).astype(BF16)

    acc = jnp.dot(x_ref[...], wbf_ref[...], preferred_element_type=F32)
    if has_bias:
        acc = acc + b_ref[...]
    if act == "sigmoid":
        acc = jax.nn.sigmoid(acc)
    if has_res:
        acc = acc + r_ref[...]
    o_ref[...] = acc.astype(o_ref.dtype)


def matmul(x, w, *, n_cols, col_block0=0, tm, tn, bias=None, res=None, act=None,
           out_dtype=BF16, w_transposed=False, name):
    m, k = x.shape
    in_specs = [
        pl.BlockSpec((tm, k), lambda j, i: (i, 0)),
        pl.BlockSpec((tn, k), lambda j, i: (j + col_block0, 0)) if w_transposed
        else pl.BlockSpec((k, tn), lambda j, i: (0, j + col_block0)),
    ]
    args = [x, w]
    if bias is not None:
        in_specs.append(pl.BlockSpec((1, tn), lambda j, i: (0, j)))
        args.append(bias.reshape(1, n_cols))
    if res is not None:
        in_specs.append(pl.BlockSpec((tm, tn), lambda j, i: (i, j)))
        args.append(res)
    return pl.pallas_call(
        functools.partial(_mm_kernel, has_bias=bias is not None, has_res=res is not None, act=act,
                          w_transposed=w_transposed),
        grid=(n_cols // tn, m // tm),
        in_specs=in_specs,
        out_specs=pl.BlockSpec((tm, tn), lambda j, i: (i, j)),
        out_shape=jax.ShapeDtypeStruct((m, n_cols), out_dtype),
        scratch_shapes=[pltpu.VMEM((k, tn), BF16)],
        compiler_params=_cparams(2),
        name=name,
    )(*args)


def _conv_kernel(p_ref, w_ref, o_ref):
    c = pl.program_id(1)
    x = p_ref[0].astype(F32)
    w = w_ref[...]
    half = CONV_W // 2
    acc = w[half:half + 1, :] * x
    for j in range(CONV_W):
        if j != half:
            acc = acc + w[j:j + 1, :] * pltpu.roll(x, (half - j) % LP, axis=0)
    y = acc * jax.nn.sigmoid(acc)
    rows = lax.broadcasted_iota(jnp.int32, (LP, 1), 0)
    y = jnp.where(rows >= PAD, y, 0.0)
    scale = jnp.where(c < 2, DK_M ** -0.5, 1.0).astype(F32)
    o_ref[0] = (y * scale).astype(BF16)


def conv_qk(proj3, conv_w):
    cw = 256
    return pl.pallas_call(
        _conv_kernel,
        grid=(BATCH, (2 * H_M * DK_M) // cw),
        in_specs=[
            pl.BlockSpec((1, LP, cw), lambda b, c: (b, 0, COL_M_Q // cw + c)),
            pl.BlockSpec((CONV_W, cw), lambda b, c: (0, c)),
        ],
        out_specs=pl.BlockSpec((1, LP, cw), lambda b, c: (b, 0, c)),
        out_shape=jax.ShapeDtypeStruct((BATCH, LP, 2 * H_M * DK_M), BF16),
        compiler_params=_cparams(2),
        name="conv_qk",
    )(proj3, conv_w)


def _split_dot(tri, v):
    hi = v.astype(BF16)
    r1 = v - hi.astype(F32)
    mid = r1.astype(BF16)
    lo = (r1 - mid.astype(F32)).astype(BF16)
    return (jnp.dot(tri, hi, preferred_element_type=F32)
            + jnp.dot(tri, mid, preferred_element_type=F32)
            + jnp.dot(tri, lo, preferred_element_type=F32))


def _gate_kernel(g_ref, bias_ref, o_ref):
    ti = lax.broadcasted_iota(jnp.int32, (BLOCK, BLOCK), 0)
    ui = lax.broadcasted_iota(jnp.int32, (BLOCK, BLOCK), 1)
    tril = jnp.where(ui <= ti, 1.0, 0.0).astype(BF16)
    triu = jnp.where(ui >= ti, 1.0, 0.0).astype(BF16)
    ch = lax.broadcasted_iota(jnp.int32, (BLOCK, 4 * H_M), 1)
    typ = lax.shift_right_logical(ch, 2)
    rloc = lax.broadcasted_iota(jnp.int32, (BLOCK, 4 * H_M), 0)

    def body(c, carry):
        r0 = pl.multiple_of(c * BLOCK, BLOCK)
        g = g_ref[0, pl.ds(r0, BLOCK), :][:, :4 * H_M] + bias_ref[...]
        valid = (rloc + r0) >= PAD
        lsig = -(jnp.maximum(-g, 0.0) + jnp.log1p(jnp.exp(-jnp.abs(g))))
        lf = jnp.where(valid, lsig, 0.0)
        cum = _split_dot(tril, lf)
        rcum = _split_dot(triu, lf)
        li = jnp.where(valid, g, -jnp.inf)
        out = jnp.where(typ == 1, cum, jnp.where(typ == 3, rcum, li))
        o_ref[0, pl.ds(r0, BLOCK), :] = out
        return carry

    lax.fori_loop(0, NBLK, body, 0)


def gate_prep(mg3, gate_bias):
    return pl.pallas_call(
        _gate_kernel,
        grid=(BATCH,),
        in_specs=[
            pl.BlockSpec((1, LP, 128), lambda b: (b, 0, 0)),
            pl.BlockSpec((1, 4 * H_M), lambda b: (0, 0)),
        ],
        out_specs=pl.BlockSpec((1, LP, 4 * H_M), lambda b: (b, 0, 0)),
        out_shape=jax.ShapeDtypeStruct((BATCH, LP, 4 * H_M), F32),
        compiler_params=_cparams(1),
        name="gate_prep",
    )(mg3, gate_bias.reshape(1, 4 * H_M))


MLSTM_HP = 2
MLSTM_MID = NBLK // 2


def _mlstm_kernel(qt_ref, k_ref, vt_ref, ot_ref, gr_ref, y_ref,
                  hs_ref, c_ref, n_ref, m_ref):
    c_ref[...] = jnp.zeros_like(c_ref)
    n_ref[...] = jnp.zeros_like(n_ref)
    m_ref[...] = jnp.zeros_like(m_ref)
    si = lax.broadcasted_iota(jnp.int32, (BLOCK, BLOCK), 0)
    ti = lax.broadcasted_iota(jnp.int32, (BLOCK, BLOCK), 1)
    mask_f = si <= ti
    mask_b = si >= ti

    def chain(c, hl, bwd, final):
        idx = 2 * hl + bwd
        r0 = c * BLOCK if isinstance(c, int) else pl.multiple_of(c * BLOCK, BLOCK)
        t_sl = pl.ds(r0, BLOCK)
        qt = qt_ref[0, hl * DK_M:(hl + 1) * DK_M, t_sl]
        k = k_ref[0, t_sl, hl * DK_M:(hl + 1) * DK_M]
        feat = slice(hl * DV_M, (hl + 1) * DV_M)
        vt = vt_ref[0, feat, t_sl]
        gr = gr_ref[0, hl, :, t_sl]
        li_r, b_r = gr[2 * bwd:2 * bwd + 1, :], gr[2 * bwd + 1:2 * bwd + 2, :]
        a_c = jnp.transpose(jnp.broadcast_to(li_r - b_r, (BLOCK, BLOCK)))
        b_end = b_r[:, 0:1] if bwd else b_r[:, BLOCK - 1:BLOCK]
        m_prev = m_ref[idx][:, 0:1]
        ct = c_ref[idx]
        nst = n_ref[idx]
        dmat = jnp.where(mask_b if bwd else mask_f, a_c + b_r, -jnp.inf)
        inter = b_r + m_prev
        m_t = jnp.maximum(inter, jnp.max(dmat, axis=0, keepdims=True))
        w_inter = jnp.exp(inter - m_t)
        st = jnp.dot(k, qt, preferred_element_type=F32) * jnp.exp(dmat - m_t)
        num = (w_inter * jnp.dot(ct.astype(BF16), qt, preferred_element_type=F32)
               + jnp.dot(vt, st.astype(BF16), preferred_element_type=F32))
        nq = jnp.dot(nst.astype(BF16), qt, preferred_element_type=F32)
        den = w_inter * nq + jnp.sum(st, axis=0, keepdims=True)
        h = num * (1.0 / jnp.maximum(jnp.abs(den), jnp.exp(-m_t)))
        if final:
            og = jax.nn.sigmoid(ot_ref[0, feat, t_sl].astype(F32))
            y_ref[0, feat, pl.ds(r0 - BLOCK, BLOCK)] = (og * (hs_ref[feat, t_sl] + h)).astype(BF16)
        else:
            hs_ref[feat, t_sl] = h
        ldec = b_end - b_r + li_r
        m_new = jnp.maximum(b_end + m_prev, jnp.max(ldec, axis=1, keepdims=True))
        w_c = jnp.exp(b_end + m_prev - m_new)
        w_s = jnp.exp(ldec - m_new)
        wvt = (vt.astype(F32) * w_s).astype(BF16)
        c_ref[idx] = w_c * ct + jnp.dot(wvt, k, preferred_element_type=F32)
        n_ref[idx] = w_c * nst + jnp.dot(w_s.astype(BF16), k, preferred_element_type=F32)
        m_ref[idx] = jnp.broadcast_to(m_new, (1, BLOCK))

    def first_half(i, carry):
        for hl in range(MLSTM_HP):
            chain(i, hl, 0, False)
            chain(NBLK - 1 - i, hl, 1, False)
        return carry

    def second_half(i, carry):
        for hl in range(MLSTM_HP):
            chain(i, hl, 0, True)
            chain(NBLK - 1 - i, hl, 1, True)
        return carry

    lax.fori_loop(0, MLSTM_MID, first_half, 0, unroll=4)
    for hl in range(MLSTM_HP):
        chain(MLSTM_MID, hl, 0, False)
        chain(MLSTM_MID, hl, 1, True)
    lax.fori_loop(MLSTM_MID + 1, NBLK - 1, second_half, 0, unroll=5)
    for hl in range(MLSTM_HP):
        chain(NBLK - 1, hl, 0, True)


def mlstm(q_t, qk_m, v_t, o_t, grow):
    hp = MLSTM_HP
    kw, vw = hp * DK_M, hp * DV_M
    return pl.pallas_call(
        _mlstm_kernel,
        grid=(BATCH, H_M // hp),
        in_specs=[
            pl.BlockSpec((1, kw, LP), lambda b, g: (b, g, 0)),
            pl.BlockSpec((1, LP, kw), lambda b, g: (b, 0, (H_M * DK_M) // kw + g)),
            pl.BlockSpec((1, vw, LP), lambda b, g: (b, g, 0)),
            pl.BlockSpec((1, vw, LP), lambda b, g: (b, g, 0)),
            pl.BlockSpec((1, hp, 4, LP), lambda b, g: (b, g, 0, 0)),
        ],
        out_specs=pl.BlockSpec((1, vw, SEQ), lambda b, g: (b, g, 0)),
        out_shape=jax.ShapeDtypeStruct((BATCH, H_M * DV_M, SEQ), BF16),
        scratch_shapes=[
            pltpu.VMEM((vw, LP), F32),
            pltpu.VMEM((2 * hp, DV_M, DK_M), F32),
            pltpu.VMEM((2 * hp, 1, DK_M), F32),
            pltpu.VMEM((2 * hp, 1, BLOCK), F32),
        ],
        compiler_params=_cparams(2),
        name="mlstm",
    )(q_t, qk_m, v_t, o_t, grow)


LOG2E = 1.4426950408889634
ATT_QB = 2
ATT_TQ = ATT_QB * BLOCK
ATT_BAND = (ATT_QB + 2) * BLOCK
ATT_GROUPS = (6, 6, 6, 6, 5)
assert BLOCK >= MAX_DISTANCE and ATT_BAND + sum(ATT_GROUPS) * BLOCK == LP


ATT_NQ = SEQ // ATT_TQ


def _attn_kernel(c_ref, *refs):
    q_refs = refs[:ATT_QB]
    (k1_ref, v1_ref, tab_ref, lam_ref, sg_ref, o_ref,
     s_a, s_b, mx_a, mx_b, k_ref, v_ref) = refs[ATT_QB:]
    h = pl.program_id(1)
    t = pl.program_id(2)

    @pl.when(t == 0)
    def _():
        for rep in range(2):
            k_ref[0, rep * LP:(rep + 1) * LP, :] = k1_ref[0]
            v_ref[0, rep * LP:(rep + 1) * LP, :] = v1_ref[0]

    scale = DK_DA ** -0.5 * LOG2E
    c_neg = c_ref[h, 0]
    c_pos = c_ref[h, 1]

    def groups_of(tile):
        koff = (ATT_QB * tile) * BLOCK
        out = [(koff, ATT_BAND, 0)]
        col = ATT_BAND
        for nblk in ATT_GROUPS:
            out.append((koff + col, nblk * BLOCK, col))
            col += nblk * BLOCK
        return out

    def lane_fold(acc, x, op):
        for j in range(x.shape[1] // BLOCK):
            piece = x[:, j * BLOCK:(j + 1) * BLOCK]
            acc = piece if acc is None else op(acc, piece)
        return acc

    def score(tile, s_ref, mx_ref):
        q = jnp.concatenate([r[0] for r in q_refs], axis=0)
        mx = [None, None]
        for gi, (koff, width, col0) in enumerate(groups_of(tile)):
            koff = pl.multiple_of(koff, BLOCK)
            if gi == 0:
                bias = tab_ref[0, 0]
            else:
                kpos = koff + lax.broadcasted_iota(jnp.int32, (1, width), 1)
                bias = jnp.where(kpos < LP, c_pos, jnp.where(kpos < LP + PAD, NEG_INF, c_neg))
            for m in range(2):
                kk = k_ref[0, pl.ds(koff, width), m * DK_DA:(m + 1) * DK_DA]
                s = lax.dot_general(q[:, m * DK_DA:(m + 1) * DK_DA], kk,
                                    (((1,), (1,)), ((), ())),
                                    preferred_element_type=F32) * scale + bias
                s_ref[m, :, col0:col0 + width] = s
                mx[m] = lane_fold(mx[m], s, jnp.maximum)
        for m in range(2):
            mx_ref[m] = mx[m]

    def finish(tile, s_ref, mx_ref):
        lp = lam_ref[...]
        lam = (jnp.exp(jnp.sum(lp[0:1] * lp[1:2], axis=1, keepdims=True))
               - jnp.exp(jnp.sum(lp[2:3] * lp[3:4], axis=1, keepdims=True)) + LAMBDA_INIT)
        row_max = [jnp.max(mx_ref[m], axis=1, keepdims=True) for m in range(2)]
        lsum = [None, None]
        acc = [None, None]
        for koff, width, col0 in groups_of(tile):
            koff = pl.multiple_of(koff, BLOCK)
            vv = v_ref[0, pl.ds(koff, width), :]
            for m in range(2):
                p = jnp.exp2(s_ref[m, :, col0:col0 + width] - row_max[m])
                lsum[m] = lane_fold(lsum[m], p, jnp.add)
                pv = jnp.dot(p.astype(BF16), vv, preferred_element_type=F32)
                acc[m] = pv if acc[m] is None else acc[m] + pv
        l1 = jnp.sum(lsum[0], axis=1, keepdims=True)
        l2 = jnp.sum(lsum[1], axis=1, keepdims=True)
        o = acc[0] / l1 - lam * (acc[1] / l2)
        o_ref[0] = (_rms(o, sg_ref[...]) * (1.0 - LAMBDA_INIT)).astype(BF16)

    even = lax.rem(t, 2) == 0
    inner = jnp.logical_and(t > 0, t < ATT_NQ)

    @pl.when(t == 0)
    def _():
        score(t, s_a, mx_a)

    @pl.when(jnp.logical_and(inner, even))
    def _():
        score(t, s_a, mx_a)
        finish(t - 1, s_b, mx_b)

    @pl.when(jnp.logical_and(inner, jnp.logical_not(even)))
    def _():
        score(t, s_b, mx_b)
        finish(t - 1, s_a, mx_a)

    @pl.when(t == ATT_NQ)
    def _():
        if (ATT_NQ - 1) % 2 == 0:
            finish(t - 1, s_a, mx_a)
        else:
            finish(t - 1, s_b, mx_b)


def diff_attention(consts, proj3, tab, lam_params, subln):
    nq = ATT_NQ
    kblk0 = COL_DA_K // (2 * DK_DA)
    vblk0 = COL_DA_V // DV_DA
    scored = lambda i: jnp.minimum(i, nq - 1)
    finished = lambda i: jnp.maximum(i - 1, 0)

    def tab_map(b, h, i):
        tile = scored(i)
        case = jnp.where(tile == 0, 0, jnp.where(tile == nq - 1, 2, 1))
        return (h, case, 0, 0)

    return pl.pallas_call(
        _attn_kernel,
        grid=(BATCH, H_DA, nq + 1),
        in_specs=[
            pl.BlockSpec(memory_space=pltpu.SMEM),
            *[pl.BlockSpec((1, BLOCK, 2 * DK_DA),
                           functools.partial(
                               lambda b, h, i, r: (b, ATT_QB * scored(i) + 1 + r, h), r=r))
              for r in range(ATT_QB)],
            pl.BlockSpec((1, LP, 2 * DK_DA), lambda b, h, i: (b, 0, kblk0 + h)),
            pl.BlockSpec((1, LP, DV_DA), lambda b, h, i: (b, 0, vblk0 + h)),
            pl.BlockSpec((1, 1, ATT_TQ, ATT_BAND), tab_map),
            pl.BlockSpec((4, DK_DA), lambda b, h, i: (0, 0)),
            pl.BlockSpec((1, DV_DA), lambda b, h, i: (0, 0)),
        ],
        out_specs=pl.BlockSpec((1, ATT_TQ, DV_DA), lambda b, h, i: (b, finished(i), h)),
        out_shape=jax.ShapeDtypeStruct((BATCH, SEQ, H_DA * DV_DA), BF16),
        scratch_shapes=[pltpu.VMEM((2, ATT_TQ, LP), F32),
                        pltpu.VMEM((2, ATT_TQ, LP), F32),
                        pltpu.VMEM((2, ATT_TQ, BLOCK), F32),
                        pltpu.VMEM((2, ATT_TQ, BLOCK), F32),
                        pltpu.VMEM((1, 2 * LP, 2 * DK_DA), BF16),
                        pltpu.VMEM((1, 2 * LP, DV_DA), BF16)],
        compiler_params=_cparams(3),
        name="diff_attn",
    )(consts, *([proj3] * (ATT_QB + 2)), tab, lam_params, subln.reshape(1, DV_DA))


def _bias_tables(rel_bias):
    rb = rel_bias.astype(F32)
    span = 1024
    assert span >= ATT_TQ + ATT_BAND
    rel = jnp.arange(span, dtype=jnp.int32) - span // 2
    nb = N_BUCKETS // 2
    max_exact = nb // 2
    n = jnp.abs(rel)
    nf = jnp.maximum(n, 1).astype(F32)
    large = max_exact + (jnp.log(nf / max_exact) / math.log(MAX_DISTANCE / max_exact)
                         * (nb - max_exact)).astype(jnp.int32)
    large = jnp.minimum(large, nb - 1)
    bucket = jnp.where(rel > 0, nb, 0) + jnp.where(n < max_exact, n, large)
    hit = bucket[None, :, None] == jnp.arange(N_BUCKETS, dtype=jnp.int32)
    by_rel = jnp.sum(jnp.where(hit, rb.T[:, None, :], 0.0), axis=-1)
    shifted = jnp.tile(by_rel, (1, ATT_TQ))[:, :ATT_TQ * (span - 1)].reshape(
        H_DA, ATT_TQ, span - 1)
    c0 = span // 2 - BLOCK
    gen = shifted[:, :, c0:c0 + ATT_BAND]
    c_neg = rb[nb - 1]
    c_pos = rb[N_BUCKETS - 1]
    jj = jnp.arange(ATT_BAND, dtype=jnp.int32)[None, None, :]
    first = jnp.where(jj < PAD, NEG_INF, gen)
    wrap0 = ATT_BAND - BLOCK
    wrapped = jnp.where(jj - wrap0 < PAD, NEG_INF, c_neg[:, None, None])
    last = jnp.where(jj >= wrap0, wrapped, gen)
    tab = jnp.stack([first, gen, last], axis=1)
    consts = jnp.stack([c_neg, c_pos], axis=1)
    return tab * LOG2E, consts * LOG2E


def _mix_kernel(ya_ref, ym_ref, wa_ref, wm_ref, ga_ref, gm_ref, o_ref, wa_bf, wm_bf):
    @pl.when(pl.program_id(1) == 0)
    def _():
        wa_bf[...] = wa_ref[...].astype(BF16)
        wm_bf[...] = wm_ref[...].astype(BF16)

    a = jnp.dot(ya_ref[...], wa_bf[...], preferred_element_type=F32)
    m = jnp.dot(ym_ref[...], wm_bf[...], preferred_element_type=F32)
    o_ref[...] = (ga_ref[...].astype(F32) * a + gm_ref[...].astype(F32) * m).astype(BF16)


def branch_mix(y_da, y_m, w_da, w_m, gate, *, tm=512, tn=1024):
    m, k = y_da.shape
    nj = D_MODEL // tn
    return pl.pallas_call(
        _mix_kernel,
        grid=(nj, m // tm),
        in_specs=[
            pl.BlockSpec((tm, k), lambda j, i: (i, 0)),
            pl.BlockSpec((tm, k), lambda j, i: (i, 0)),
            pl.BlockSpec((k, tn), lambda j, i: (0, j)),
            pl.BlockSpec((k, tn), lambda j, i: (0, j)),
            pl.BlockSpec((tm, tn), lambda j, i: (i, j)),
            pl.BlockSpec((tm, tn), lambda j, i: (i, nj + j)),
        ],
        out_specs=pl.BlockSpec((tm, tn), lambda j, i: (i, j)),
        out_shape=jax.ShapeDtypeStruct((m, D_MODEL), BF16),
        scratch_shapes=[pltpu.VMEM((k, tn), BF16), pltpu.VMEM((k, tn), BF16)],
        compiler_params=_cparams(2),
        name="branch_mix",
    )(y_da, y_m, w_da, w_m, gate, gate)


FFN_TM = 1024
HALF = D_MODEL // 2


def _pack_bf16_pairs(v):
    lo = lax.bitcast_convert_type(v[:, :HALF].astype(BF16).astype(F32), jnp.uint32)
    hi = lax.bitcast_convert_type(v[:, HALF:].astype(BF16).astype(F32), jnp.uint32)
    return (hi & jnp.uint32(0xFFFF0000)) | lax.shift_right_logical(lo, jnp.uint32(16))


def _unpack_bf16_pairs(w):
    lo = lax.bitcast_convert_type(lax.shift_left(w, jnp.uint32(16)), F32).astype(BF16)
    hi = lax.bitcast_convert_type(w & jnp.uint32(0xFFFF0000), F32).astype(BF16)
    return lo, hi


def _ffn_prep_kernel(h_ref, g_ref, wr_ref, br_ref, hn_ref, e_ref, w_ref, r_ref, cnt_ref, base_ref):
    @pl.when(pl.program_id(0) == 0)
    def _():
        base_ref[...] = jnp.zeros_like(base_ref)

    hn = _rms(h_ref[...], g_ref[...])
    hn_ref[...] = _pack_bf16_pairs(hn)
    logits = jnp.dot(hn.astype(BF16), wr_ref[...].astype(BF16),
                     preferred_element_type=F32) + br_ref[...]
    lane = lax.broadcasted_iota(jnp.int32, (FFN_TM, N_EXPERTS), 1)
    lane_o = lax.broadcasted_iota(jnp.int32, (FFN_TM, 128), 1)
    ti = lax.broadcasted_iota(jnp.int32, (FFN_TM, FFN_TM), 0)
    ui = lax.broadcasted_iota(jnp.int32, (FFN_TM, FFN_TM), 1)
    tril = jnp.where(ui <= ti, 1.0, 0.0).astype(BF16)
    e_out = jnp.zeros((FFN_TM, 128), jnp.int32)
    r_out = jnp.zeros((FFN_TM, 128), jnp.int32)
    l_out = jnp.full((FFN_TM, 128), -jnp.inf, F32)
    base = base_ref[...]
    l = logits
    for kk in range(TOP_K):
        mk = jnp.max(l, axis=1, keepdims=True)
        ik = jnp.min(jnp.where(l == mk, lane, N_EXPERTS), axis=1, keepdims=True)
        hit = lane == ik
        oh = jnp.where(hit, 1.0, 0.0)
        cum = jnp.dot(tril, oh.astype(BF16), preferred_element_type=F32)
        rank = jnp.sum(oh * (cum + base), axis=1, keepdims=True) - 1.0
        base = base + jnp.sum(oh, axis=0, keepdims=True)
        e_out = jnp.where(lane_o == kk, ik, e_out)
        r_out = jnp.where(lane_o == kk, rank.astype(jnp.int32), r_out)
        l_out = jnp.where(lane_o == kk, mk, l_out)
        l = jnp.where(hit, -jnp.inf, l)
    base_ref[...] = base
    cnt_ref[...] = base
    ex = jnp.exp(l_out - jnp.max(l_out, axis=1, keepdims=True))
    e_ref[...] = e_out
    r_ref[...] = r_out
    w_ref[...] = ex / jnp.sum(ex, axis=1, keepdims=True)


def ffn_prep(h2, gain, w_router, b_router):
    row = lambda i: (i, 0)
    fixed = lambda i: (0, 0)
    return pl.pallas_call(
        _ffn_prep_kernel,
        grid=(N_TOK // FFN_TM,),
        in_specs=[
            pl.BlockSpec((FFN_TM, D_MODEL), row),
            pl.BlockSpec((1, D_MODEL), fixed),
            pl.BlockSpec((D_MODEL, N_EXPERTS), fixed),
            pl.BlockSpec((1, N_EXPERTS), fixed),
        ],
        out_specs=[
            pl.BlockSpec((FFN_TM, HALF), row),
            pl.BlockSpec((FFN_TM, 128), row),
            pl.BlockSpec((FFN_TM, 128), row),
            pl.BlockSpec((FFN_TM, 128), row),
            pl.BlockSpec((1, N_EXPERTS), fixed),
        ],
        out_shape=[
            jax.ShapeDtypeStruct((N_TOK, HALF), jnp.uint32),
            jax.ShapeDtypeStruct((N_TOK, 128), jnp.int32),
            jax.ShapeDtypeStruct((N_TOK, 128), F32),
            jax.ShapeDtypeStruct((N_TOK, 128), jnp.int32),
            jax.ShapeDtypeStruct((1, N_EXPERTS), F32),
        ],
        scratch_shapes=[pltpu.VMEM((1, N_EXPERTS), F32)],
        compiler_params=_cparams(1),
        name="ffn_prep",
    )(h2, gain.reshape(1, D_MODEL), w_router, b_router.reshape(1, N_EXPERTS))


SEG_ALIGN = 128
MOE_R = N_ASSIGN + N_EXPERTS * SEG_ALIGN


def _plan(counts_f, top_e, rank):
    counts = counts_f[0].astype(jnp.int32)
    seg_rows = (counts + SEG_ALIGN - 1) // SEG_ALIGN * SEG_ALIGN
    seg_start = jnp.cumsum(seg_rows) - seg_rows
    eq = top_e[:, :, None] == jnp.arange(N_EXPERTS, dtype=jnp.int32)[None, None, :]
    dest = jnp.sum(jnp.where(eq, seg_start[None, None, :], 0), axis=-1) + rank
    used = jnp.sum(seg_rows)
    slack = jnp.stack([used, (MOE_R - used) // SEG_ALIGN])
    return (dest.reshape(N_ASSIGN).astype(jnp.int32), seg_start.astype(jnp.int32),
            seg_rows.astype(jnp.int32), (seg_start + counts).astype(jnp.int32),
            (seg_rows - counts).astype(jnp.int32), slack.astype(jnp.int32))


def _zero_slack(slack_ref, zero_block, dst_rows, sem):
    zero_block[...] = jnp.zeros_like(zero_block)

    def copy(j):
        r0 = pl.multiple_of(slack_ref[0] + j * SEG_ALIGN, SEG_ALIGN)
        return pltpu.make_async_copy(zero_block, dst_rows(pl.ds(r0, SEG_ALIGN)), sem)

    def start(j, carry):
        copy(j).start()
        return carry

    def wait(j, carry):
        copy(j).wait()
        return carry

    lax.fori_loop(0, slack_ref[1], start, 0)
    lax.fori_loop(0, slack_ref[1], wait, 0)


DISP_TOK = 256


def _dispatch_kernel(dest_ref, pad0_ref, padn_ref, slack_ref, hn_ref, xs_ref,
                     zrow_ref, zblk_ref, sem, zsem):
    i = pl.program_id(0)

    @pl.when(i == 0)
    def _():
        zrow_ref[...] = jnp.zeros_like(zrow_ref)
        _zero_slack(slack_ref, zblk_ref, lambda rows: xs_ref.at[rows], zsem.at[0])

        def expert(e, carry):
            p0 = pad0_ref[e]
            pn = padn_ref[e]

            def zstart(r, c2):
                pltpu.make_async_copy(zrow_ref, xs_ref.at[pl.ds(p0 + r, 1)], zsem.at[0]).start()
                return c2

            def zwait(r, c2):
                pltpu.make_async_copy(zrow_ref, xs_ref.at[pl.ds(p0, 1)], zsem.at[0]).wait()
                return c2

            lax.fori_loop(0, pn, zstart, 0)
            lax.fori_loop(0, pn, zwait, 0)
            return carry

        lax.fori_loop(0, N_EXPERTS, expert, 0)

    def tok(t, carry):
        a = (i * DISP_TOK + t) * TOP_K
        for kk in range(TOP_K):
            pltpu.make_async_copy(hn_ref.at[pl.ds(t, 1)], xs_ref.at[pl.ds(dest_ref[a + kk], 1)],
                                  sem.at[0]).start(priority=kk % 2)
        return carry

    lax.fori_loop(0, DISP_TOK, tok, 0, unroll=4)
    for kk in range(TOP_K):
        pltpu.make_async_copy(hn_ref, xs_ref.at[pl.ds(0, DISP_TOK)], sem.at[0]).wait()


def dispatch(dest, pad0, padn, slack, hn_packed):
    grid_spec = pltpu.PrefetchScalarGridSpec(
        num_scalar_prefetch=4,
        grid=(N_TOK // DISP_TOK,),
        in_specs=[pl.BlockSpec((DISP_TOK, HALF), lambda i, d, p0, pn, z: (i, 0))],
        out_specs=pl.BlockSpec(memory_space=pl.ANY),
        scratch_shapes=[pltpu.VMEM((1, HALF), jnp.uint32),
                        pltpu.VMEM((SEG_ALIGN, HALF), jnp.uint32),
                        pltpu.SemaphoreType.DMA((1,)), pltpu.SemaphoreType.DMA((1,))],
    )
    return pl.pallas_call(
        _dispatch_kernel,
        grid_spec=grid_spec,
        out_shape=jax.ShapeDtypeStruct((MOE_R, HALF), jnp.uint32),
        compiler_params=_cparams(1),
        name="moe_dispatch",
    )(dest, pad0, padn, slack, hn_packed)


MOE_CH = 2 * SEG_ALIGN
MOE_TF = 1024
MOE_NF = D_FF // MOE_TF


class _CopyGroup:
    def __init__(self, copies):
        self.copies = copies

    def start(self, priority=0):
        for cp in self.copies:
            cp.start(priority=priority)

    def wait(self):
        for cp in self.copies:
            cp.wait()


W_PIECES = 8
W_PER_CHUNK = 3


N_STATE = 4


def _stream_rows(step, n_steps, start, rows, next_start, next_rows, state,
                 make_in, make_out, make_tail_in, make_tail_out,
                 compute_chunk, compute_tail, before_first_wait, next_weight_piece):
    has_next = step + 1 < n_steps

    def request_weights(first, count):
        def one(p, carry):
            @pl.when(jnp.logical_and(has_next, p < W_PIECES))
            def _():
                next_weight_piece(p).start()
            return carry

        lax.fori_loop(first, first + count, one, 0)

    @pl.when(step == 0)
    def _():
        for j in range(N_STATE):
            state[j] = 0

    n_ch = lax.shift_right_logical(rows, MOE_CH.bit_length() - 1)
    tail = rows - n_ch * MOE_CH
    tail_row = start + n_ch * MOE_CH
    g0 = state[0]
    feeds_next = jnp.logical_and(step + 1 < n_steps, next_rows >= MOE_CH)

    def chunk_row(c):
        return start + c * MOE_CH

    @pl.when(tail > 0)
    def _():
        make_tail_in(tail_row).start(priority=1)

    @pl.when(jnp.logical_and(n_ch > 0, state[3] == 0))
    def _():
        make_in(start, lax.rem(g0, 2)).start(priority=1)

    before_first_wait()

    @pl.when(tail > 0)
    def _():
        make_tail_in(tail_row).wait()
        compute_tail()
        make_tail_out(tail_row).start(priority=1)

    def body(c, carry):
        slot = lax.rem(g0 + c, 2)
        make_in(chunk_row(c), slot).wait()

        @pl.when(c + 1 < n_ch)
        def _():
            make_in(chunk_row(c + 1), 1 - slot).start(priority=1)

        @pl.when(jnp.logical_and(c + 1 == n_ch, feeds_next))
        def _():
            make_in(next_start, 1 - slot).start(priority=1)

        request_weights(c * W_PER_CHUNK, W_PER_CHUNK)

        @pl.when(state[1 + slot] == 1)
        def _():
            make_out(chunk_row(c), slot).wait()

        compute_chunk(slot)
        make_out(chunk_row(c), slot).start(priority=1)
        state[1 + slot] = 1
        return carry

    lax.fori_loop(0, n_ch, body, 0)
    request_weights(n_ch * W_PER_CHUNK, W_PIECES)
    state[0] = g0 + n_ch
    state[3] = jnp.where(jnp.logical_and(n_ch > 0, feeds_next), 1, 0)

    @pl.when(tail > 0)
    def _():
        make_tail_out(tail_row).wait()

    @pl.when(step == n_steps - 1)
    def _():
        for slot in range(2):
            @pl.when(state[1 + slot] == 1)
            def _():
                make_out(0, slot).wait()
                state[1 + slot] = 0


def _rows_at(row0, n):
    return pl.ds(row0 if isinstance(row0, int) else pl.multiple_of(row0, SEG_ALIGN), n)


def _moe_up_kernel(seg_ref, rows_ref, slack_ref, xs_ref, w_hbm, bg_ref, bl_ref, act_ref,
                   xbuf, obuf, xtail, otail, wbuf, wg_bf, wl_bf, state, isem, osem, tsem, wsem):
    f = pl.program_id(0)
    e = pl.program_id(1)
    start = seg_ref[e]
    rows = rows_ref[e]
    e_next = lax.rem(e + 1, N_EXPERTS)
    f_next = jnp.where(e == N_EXPERTS - 1, f + 1, f)
    step = f * N_EXPERTS + e
    n_steps = MOE_NF * N_EXPERTS
    wslot = lax.rem(step, 2)

    def weight_piece(expert, ftile, slot, p):
        per_half = W_PIECES // 2
        band = D_MODEL // per_half
        t = p // per_half if isinstance(p, int) else lax.shift_right_logical(
            p, per_half.bit_length() - 1)
        r0 = (p - t * per_half) * band
        r0 = r0 if isinstance(r0, int) else pl.multiple_of(r0, band)
        col0 = pl.multiple_of((t * MOE_NF + ftile) * MOE_TF, MOE_TF)
        return pltpu.make_async_copy(
            w_hbm.at[expert, pl.ds(r0, band), pl.ds(col0, MOE_TF)],
            wbuf.at[slot, t, pl.ds(r0, band)], wsem.at[slot, p])

    @pl.when(step == 0)
    def _():
        for p in range(W_PIECES):
            weight_piece(e, f, wslot, p).start()

    def make_in(r0, slot):
        return pltpu.make_async_copy(xs_ref.at[_rows_at(r0, MOE_CH)], xbuf.at[slot], isem.at[slot])

    def make_out(r0, slot):
        return pltpu.make_async_copy(obuf.at[slot], act_ref.at[f, _rows_at(r0, MOE_CH)],
                                     osem.at[slot])

    def make_tail_in(r0):
        return pltpu.make_async_copy(xs_ref.at[_rows_at(r0, SEG_ALIGN)], xtail, tsem.at[0])

    def make_tail_out(r0):
        return pltpu.make_async_copy(otail, act_ref.at[f, _rows_at(r0, SEG_ALIGN)], tsem.at[1])

    def cast_weights():
        per_half = W_PIECES // 2
        band = D_MODEL // per_half
        for p in range(W_PIECES):
            weight_piece(e, f, wslot, p).wait()
            t, rows_p = p // per_half, pl.ds((p % per_half) * band, band)
            (wg_bf, wl_bf)[t][rows_p, :] = wbuf[wslot, t, rows_p, :].astype(BF16)

    def expert_mlp(words):
        lo, hi = _unpack_bf16_pairs(words)
        glu = (jnp.dot(lo, wg_bf[:HALF, :], preferred_element_type=F32)
               + jnp.dot(hi, wg_bf[HALF:, :], preferred_element_type=F32) + bg_ref[0])
        lin = (jnp.dot(lo, wl_bf[:HALF, :], preferred_element_type=F32)
               + jnp.dot(hi, wl_bf[HALF:, :], preferred_element_type=F32) + bl_ref[0])
        glu = jnp.minimum(glu, SWIGLU_LIMIT)
        lin = jnp.clip(lin, -SWIGLU_LIMIT, SWIGLU_LIMIT)
        return (glu * jax.nn.sigmoid(SWIGLU_ALPHA * glu) * (lin + 1.0)).astype(BF16)

    def compute_chunk(slot):
        obuf[slot] = expert_mlp(xbuf[slot])

    def compute_tail():
        otail[...] = expert_mlp(xtail[...])

    _stream_rows(step, n_steps, start, rows,
                 seg_ref[e_next], rows_ref[e_next], state,
                 make_in, make_out, make_tail_in, make_tail_out,
                 compute_chunk, compute_tail, cast_weights,
                 lambda p: weight_piece(e_next, f_next, 1 - wslot, p))

    @pl.when(e == N_EXPERTS - 1)
    def _():
        _zero_slack(slack_ref, otail, lambda rr: act_ref.at[f, rr], tsem.at[1])


def moe_up(seg_start, seg_rows, slack, xs, w1, b1):
    grid_spec = pltpu.PrefetchScalarGridSpec(
        num_scalar_prefetch=3,
        grid=(MOE_NF, N_EXPERTS),
        in_specs=[
            pl.BlockSpec(memory_space=pl.ANY),
            pl.BlockSpec(memory_space=pl.ANY),
            pl.BlockSpec((1, 1, MOE_TF), lambda f, e, s, r, z: (e, 0, f)),
            pl.BlockSpec((1, 1, MOE_TF), lambda f, e, s, r, z: (e, 0, MOE_NF + f)),
        ],
        out_specs=pl.BlockSpec(memory_space=pl.ANY),
        scratch_shapes=[
            pltpu.VMEM((2, MOE_CH, HALF), jnp.uint32),
            pltpu.VMEM((2, MOE_CH, MOE_TF), BF16),
            pltpu.VMEM((SEG_ALIGN, HALF), jnp.uint32),
            pltpu.VMEM((SEG_ALIGN, MOE_TF), BF16),
            pltpu.VMEM((2, 2, D_MODEL, MOE_TF), F32),
            pltpu.VMEM((D_MODEL, MOE_TF), BF16),
            pltpu.VMEM((D_MODEL, MOE_TF), BF16),
            pltpu.SMEM((N_STATE,), jnp.int32),
            pltpu.SemaphoreType.DMA((2,)),
            pltpu.SemaphoreType.DMA((2,)),
            pltpu.SemaphoreType.DMA((2,)),
            pltpu.SemaphoreType.DMA((2, W_PIECES)),
        ],
    )
    b13 = b1.reshape(N_EXPERTS, 1, 2 * D_FF)
    return pl.pallas_call(
        _moe_up_kernel,
        grid_spec=grid_spec,
        out_shape=jax.ShapeDtypeStruct((MOE_NF, MOE_R, MOE_TF), BF16),
        compiler_params=pltpu.CompilerParams(
            dimension_semantics=("arbitrary", "arbitrary"), vmem_limit_bytes=MOE_VMEM_LIMIT),
        name="moe_up",
    )(seg_start, seg_rows, slack, xs, w1, b13, b13)


def _moe_down_kernel(seg_ref, rows_ref, slack_ref, act_ref, w_hbm, b_ref, y_ref,
                     xbuf, obuf, xtail, otail, wbuf, w_bf, state, isem, osem, tsem, wsem):
    e = pl.program_id(0)
    start = seg_ref[e]
    rows = rows_ref[e]
    e_next = lax.rem(e + 1, N_EXPERTS)
    wslot = lax.rem(e, 2)

    def weight_piece(expert, slot, p):
        band = D_FF // W_PIECES
        r0 = p * band if isinstance(p, int) else pl.multiple_of(p * band, band)
        return pltpu.make_async_copy(w_hbm.at[expert, pl.ds(r0, band)],
                                     wbuf.at[slot, pl.ds(r0, band)], wsem.at[slot, p])

    @pl.when(e == 0)
    def _():
        for p in range(W_PIECES):
            weight_piece(e, wslot, p).start()

    def make_in(r0, slot):
        return _CopyGroup([
            pltpu.make_async_copy(act_ref.at[j, _rows_at(r0, MOE_CH)],
                                  xbuf.at[slot, :, pl.ds(j * MOE_TF, MOE_TF)], isem.at[slot])
            for j in range(MOE_NF)])

    def make_out(r0, slot):
        return pltpu.make_async_copy(obuf.at[slot], y_ref.at[_rows_at(r0, MOE_CH)], osem.at[slot])

    def make_tail_in(r0):
        return _CopyGroup([
            pltpu.make_async_copy(act_ref.at[j, _rows_at(r0, SEG_ALIGN)],
                                  xtail.at[:, pl.ds(j * MOE_TF, MOE_TF)], tsem.at[0])
            for j in range(MOE_NF)])

    def make_tail_out(r0):
        return pltpu.make_async_copy(otail, y_ref.at[_rows_at(r0, SEG_ALIGN)], tsem.at[1])

    def cast_weights():
        band = D_FF // W_PIECES
        for p in range(W_PIECES):
            weight_piece(e, wslot, p).wait()
            rows_p = pl.ds(p * band, band)
            w_bf[rows_p, :] = wbuf[wslot, rows_p, :].astype(BF16)

    def expert_out(a):
        return _pack_bf16_pairs(jnp.dot(a, w_bf[...], preferred_element_type=F32) + b_ref[0])

    def compute_chunk(slot):
        obuf[slot] = expert_out(xbuf[slot])

    def compute_tail():
        otail[...] = expert_out(xtail[...])

    _stream_rows(e, N_EXPERTS, start, rows, seg_ref[e_next], rows_ref[e_next], state,
                 make_in, make_out, make_tail_in, make_tail_out,
                 compute_chunk, compute_tail, cast_weights,
                 lambda p: weight_piece(e_next, 1 - wslot, p))

    @pl.when(e == N_EXPERTS - 1)
    def _():
        _zero_slack(slack_ref, otail, lambda rr: y_ref.at[rr], tsem.at[1])


def moe_down(seg_start, seg_rows, slack, act, w2, b2):
    grid_spec = pltpu.PrefetchScalarGridSpec(
        num_scalar_prefetch=3,
        grid=(N_EXPERTS,),
        in_specs=[
            pl.BlockSpec(memory_space=pl.ANY),
            pl.BlockSpec(memory_space=pl.ANY),
            pl.BlockSpec((1, 1, D_MODEL), lambda e, s, r, z: (e, 0, 0)),
        ],
        out_specs=pl.BlockSpec(memory_space=pl.ANY),
        scratch_shapes=[
            pltpu.VMEM((2, MOE_CH, D_FF), BF16),
            pltpu.VMEM((2, MOE_CH, HALF), jnp.uint32),
            pltpu.VMEM((SEG_ALIGN, D_FF), BF16),
            pltpu.VMEM((SEG_ALIGN, HALF), jnp.uint32),
            pltpu.VMEM((2, D_FF, D_MODEL), F32),
            pltpu.VMEM((D_FF, D_MODEL), BF16),
            pltpu.SMEM((N_STATE,), jnp.int32),
            pltpu.SemaphoreType.DMA((2,)),
            pltpu.SemaphoreType.DMA((2,)),
            pltpu.SemaphoreType.DMA((2,)),
            pltpu.SemaphoreType.DMA((2, W_PIECES)),
        ],
    )
    return pl.pallas_call(
        _moe_down_kernel,
        grid_spec=grid_spec,
        out_shape=jax.ShapeDtypeStruct((MOE_R, HALF), jnp.uint32),
        compiler_params=pltpu.CompilerParams(
            dimension_semantics=("arbitrary",), vmem_limit_bytes=MOE_VMEM_LIMIT),
        name="moe_down",
    )(seg_start, seg_rows, slack, act, w2, b2.reshape(N_EXPERTS, 1, D_MODEL))


COMB_TM = 256
COMB_NT = N_TOK // COMB_TM


def _combine_kernel(dest_ref, y_ref, w_ref, h_ref, g_ref, o_ref, buf, sem):
    i = pl.program_id(0)

    def fetch(tile, slot):
        def tok(t, carry):
            a = (tile * COMB_TM + t) * TOP_K
            for kk in range(TOP_K):
                pltpu.make_async_copy(y_ref.at[pl.ds(dest_ref[a + kk], 1)],
                                      buf.at[slot, kk, pl.ds(t, 1)], sem.at[slot]).start(
                                          priority=kk % 2)
            return carry

        lax.fori_loop(0, COMB_TM, tok, 0, unroll=4)

    @pl.when(i == 0)
    def _():
        fetch(0, 0)

    slot = lax.rem(i, 2)

    @pl.when(i + 1 < COMB_NT)
    def _():
        fetch(i + 1, 1 - slot)

    for kk in range(TOP_K):
        pltpu.make_async_copy(y_ref.at[pl.ds(0, COMB_TM)], buf.at[slot, kk], sem.at[slot]).wait()
    w = w_ref[...]
    lo = h_ref[:, :HALF]
    hi = h_ref[:, HALF:]
    for kk in range(TOP_K):
        words = buf[slot, kk]
        wk = w[:, kk:kk + 1]
        lo = lo + wk * lax.bitcast_convert_type(lax.shift_left(words, jnp.uint32(16)), F32)
        hi = hi + wk * lax.bitcast_convert_type(words & jnp.uint32(0xFFFF0000), F32)
    ms = (jnp.sum(lo * lo, axis=-1, keepdims=True)
          + jnp.sum(hi * hi, axis=-1, keepdims=True)) * (1.0 / D_MODEL)
    inv = lax.rsqrt(ms + EPS)
    o_ref[:, :HALF] = lo * inv * g_ref[:, :HALF]
    o_ref[:, HALF:] = hi * inv * g_ref[:, HALF:]


def combine(dest, y, weight, h2, gain):
    grid_spec = pltpu.PrefetchScalarGridSpec(
        num_scalar_prefetch=1,
        grid=(COMB_NT,),
        in_specs=[
            pl.BlockSpec(memory_space=pl.ANY),
            pl.BlockSpec((COMB_TM, 128), lambda i, d: (i, 0)),
            pl.BlockSpec((COMB_TM, D_MODEL), lambda i, d: (i, 0)),
            pl.BlockSpec((1, D_MODEL), lambda i, d: (0, 0)),
        ],
        out_specs=pl.BlockSpec((COMB_TM, D_MODEL), lambda i, d: (i, 0)),
        scratch_shapes=[pltpu.VMEM((2, TOP_K, COMB_TM, HALF), jnp.uint32),
                        pltpu.SemaphoreType.DMA((2,))],
    )
    return pl.pallas_call(
        _combine_kernel,
        grid_spec=grid_spec,
        out_shape=jax.ShapeDtypeStruct((N_TOK, D_MODEL), F32),
        compiler_params=_cparams(1),
        name="moe_combine",
    )(dest, y, weight, h2, gain.reshape(1, D_MODEL))


def kernel(x, meta_tokens, rel_bias, norm_mix, w_in, conv_w, gate_bias_m, lambda_params, subln_da,
           w_branch_da, w_branch_m, w_gate, b_gate, w_out, norm_ffn, w_router, b_router,
           w1, b1, w2, b2, norm_final):
    layer = 0
    xn_pad, xn_real = norm_in(x, meta_tokens, norm_mix[layer])
    xn_pad2 = xn_pad.reshape(BATCH * LP, D_MODEL)
    w_in_t = w_in[layer].T
    proj = matmul(xn_pad2, w_in_t, n_cols=PROJ_COLS, tm=768, tn=1024, w_transposed=True,
                  name="proj_in")
    proj3 = proj.reshape(BATCH, LP, PROJ_COLS)
    w_g_t = jnp.pad(w_in_t[COL_M_G:], ((0, 128 - 4 * H_M), (0, 0)))
    mg = matmul(xn_pad2, w_g_t, n_cols=128, tm=768, tn=128, out_dtype=F32, w_transposed=True,
                name="proj_gates")
    gate = matmul(xn_real.reshape(N_TOK, D_MODEL), w_gate[layer], n_cols=2 * D_MODEL,
                  tm=1024, tn=1024, bias=b_gate[layer], act="sigmoid", name="mix_gate")

    tab, consts = _bias_tables(rel_bias)
    y_da = diff_attention(consts, proj3, tab, lambda_params[layer], subln_da[layer])

    qk_m = conv_qk(proj3, conv_w[layer])
    gp = gate_prep(mg.reshape(BATCH, LP, 128), gate_bias_m[layer])
    gp4 = gp.reshape(BATCH, LP, 4, H_M)
    grow = jnp.transpose(gp4, (0, 3, 2, 1))
    swap = lambda t: jnp.swapaxes(t, 1, 2)
    y_m = swap(mlstm(swap(qk_m[:, :, :H_M * DK_M]), qk_m,
                     swap(proj3[:, :, COL_M_V:COL_M_V + H_M * DV_M]),
                     swap(proj3[:, :, COL_M_O:COL_M_O + H_M * DV_M]), grow))

    mixed = branch_mix(y_da.reshape(N_TOK, H_DA * DV_DA), y_m.reshape(N_TOK, H_M * DV_M),
                       w_branch_da[layer], w_branch_m[layer], gate)
    h2 = matmul(mixed, w_out[layer], n_cols=D_MODEL, tm=1024, tn=1024,
                res=x.reshape(N_TOK, D_MODEL), out_dtype=F32, name="out_proj")

    hn_packed, top_e, weight, rank, counts = ffn_prep(
        h2, norm_ffn[layer], w_router[layer], b_router[layer])
    dest, seg_start, seg_rows, pad0, padn, slack = _plan(
        counts, top_e[:, :TOP_K], rank[:, :TOP_K])
    xs = dispatch(dest, pad0, padn, slack, hn_packed)
    act = moe_up(seg_start, seg_rows, slack, xs, w1[layer], b1[layer])
    y = moe_down(seg_start, seg_rows, slack, act, w2[layer], b2[layer])
    out = combine(dest, y, weight, h2, norm_final)
    return out.reshape(BATCH, SEQ, D_MODEL)
```

```python
import functools
import math

import jax
import jax.numpy as jnp
from jax import lax
from jax.experimental import pallas as pl
from jax.experimental.pallas import tpu as pltpu

F32 = jnp.float32
BF16 = jnp.bfloat16

D_MODEL = 2048
BATCH = 2
SEQ = 4096
N_META = 16
BLOCK = 128
PAD = (-N_META) % BLOCK
LP = PAD + N_META + SEQ
NBLK = LP // BLOCK
EPS = 1e-6
NEG_INF = -1e30

H_DA = 4
DK_DA = 128
DV_DA = 256
H_M = 4
DK_M = 128
DV_M = 256
CONV_W = 5
N_BUCKETS = 32
MAX_DISTANCE = 128
N_EXPERTS = 32
TOP_K = 4
D_FF = 2048
SWIGLU_ALPHA = 1.702
SWIGLU_LIMIT = 7.0
LAMBDA_INIT = 0.8 - 0.6 * math.exp(-0.3 * 0)

COL_DA_Q = 0
COL_DA_K = 1024
COL_DA_V = 2048
COL_M_Q = 3072
COL_M_K = 3584
COL_M_V = 4096
COL_M_O = 5120
COL_M_G = 6144
PROJ_COLS = 6144

N_TOK = BATCH * SEQ
N_ASSIGN = N_TOK * TOP_K

VMEM_LIMIT = 52 * 1024 * 1024
MOE_VMEM_LIMIT = 58 * 1024 * 1024


def _cparams(n_axes):
    return pltpu.CompilerParams(
        dimension_semantics=("arbitrary",) * n_axes, vmem_limit_bytes=VMEM_LIMIT)


def _rms(v, gain):
    ms = jnp.mean(v * v, axis=-1, keepdims=True)
    return v * lax.rsqrt(ms + EPS) * gain


def _norm_in_kernel(x_ref, meta_ref, g_ref, pad_ref, real_ref):
    j = pl.program_id(1)
    g = g_ref[...]

    @pl.when(j == 0)
    def _():
        pad_ref[0, :PAD, :] = jnp.zeros((PAD, D_MODEL), BF16)
        pad_ref[0, PAD:, :] = _rms(meta_ref[...], g).astype(BF16)

    @pl.when(j > 0)
    def _():
        y = _rms(x_ref[0], g).astype(BF16)
        pad_ref[0] = y
        real_ref[0] = y


def norm_in(x, meta, gain):
    return pl.pallas_call(
        _norm_in_kernel,
        grid=(BATCH, NBLK),
        in_specs=[
            pl.BlockSpec((1, BLOCK, D_MODEL), lambda b, j: (b, jnp.maximum(j - 1, 0), 0)),
            pl.BlockSpec((N_META, D_MODEL), lambda b, j: (0, 0)),
            pl.BlockSpec((1, D_MODEL), lambda b, j: (0, 0)),
        ],
        out_specs=[
            pl.BlockSpec((1, BLOCK, D_MODEL), lambda b, j: (b, j, 0)),
            pl.BlockSpec((1, BLOCK, D_MODEL), lambda b, j: (b, jnp.maximum(j - 1, 0), 0)),
        ],
        out_shape=[
            jax.ShapeDtypeStruct((BATCH, LP, D_MODEL), BF16),
            jax.ShapeDtypeStruct((BATCH, SEQ, D_MODEL), BF16),
        ],
        compiler_params=_cparams(2),
        name="norm_in",
    )(x, meta, gain.reshape(1, D_MODEL))


def _mm_kernel(*refs, has_bias, has_res, act, w_transposed):
    x_ref, w_ref = refs[0], refs[1]
    pos = 2
    b_ref = r_ref = None
    if has_bias:
        b_ref = refs[pos]
        pos += 1
    if has_res:
        r_ref = refs[pos]
        pos += 1
    o_ref, wbf_ref = refs[pos], refs[pos + 1]

    @pl.when(pl.program_id(1) == 0)
    def _():
        w = w_ref[...]
        wbf_ref[...] = (w.T if w_transposed else w).astype(BF16)

    acc = jnp.dot(x_ref[...], wbf_ref[...], preferred_element_type=F32)
    if has_bias:
        acc = acc + b_ref[...]
    if act == "sigmoid":
        acc = jax.nn.sigmoid(acc)
    if has_res:
        acc = acc + r_ref[...]
    o_ref[...] = acc.astype(o_ref.dtype)


def matmul(x, w, *, n_cols, col_block0=0, tm, tn, bias=None, res=None, act=None,
           out_dtype=BF16, w_transposed=False, name):
    m, k = x.shape
    in_specs = [
        pl.BlockSpec((tm, k), lambda j, i: (i, 0)),
        pl.BlockSpec((tn, k), lambda j, i: (j + col_block0, 0)) if w_transposed
        else pl.BlockSpec((k, tn), lambda j, i: (0, j + col_block0)),
    ]
    args = [x, w]
    if bias is not None:
        in_specs.append(pl.BlockSpec((1, tn), lambda j, i: (0, j)))
        args.append(bias.reshape(1, n_cols))
    if res is not None:
        in_specs.append(pl.BlockSpec((tm, tn), lambda j, i: (i, j)))
        args.append(res)
    return pl.pallas_call(
        functools.partial(_mm_kernel, has_bias=bias is not None, has_res=res is not None, act=act,
                          w_transposed=w_transposed),
        grid=(n_cols // tn, m // tm),
        in_specs=in_specs,
        out_specs=pl.BlockSpec((tm, tn), lambda j, i: (i, j)),
        out_shape=jax.ShapeDtypeStruct((m, n_cols), out_dtype),
        scratch_shapes=[pltpu.VMEM((k, tn), BF16)],
        compiler_params=_cparams(2),
        name=name,
    )(*args)


def _conv_kernel(p_ref, w_ref, o_ref):
    c = pl.program_id(1)
    x = p_ref[0].astype(F32)
    w = w_ref[...]
    half = CONV_W // 2
    acc = w[half:half + 1, :] * x
    for j in range(CONV_W):
        if j != half:
            acc = acc + w[j:j + 1, :] * pltpu.roll(x, (half - j) % LP, axis=0)
    y = acc * jax.nn.sigmoid(acc)
    rows = lax.broadcasted_iota(jnp.int32, (LP, 1), 0)
    y = jnp.where(rows >= PAD, y, 0.0)
    scale = jnp.where(c < 2, DK_M ** -0.5, 1.0).astype(F32)
    o_ref[0] = (y * scale).astype(BF16)


def conv_qk(proj3, conv_w):
    cw = 256
    return pl.pallas_call(
        _conv_kernel,
        grid=(BATCH, (2 * H_M * DK_M) // cw),
        in_specs=[
            pl.BlockSpec((1, LP, cw), lambda b, c: (b, 0, COL_M_Q // cw + c)),
            pl.BlockSpec((CONV_W, cw), lambda b, c: (0, c)),
        ],
        out_specs=pl.BlockSpec((1, LP, cw), lambda b, c: (b, 0, c)),
        out_shape=jax.ShapeDtypeStruct((BATCH, LP, 2 * H_M * DK_M), BF16),
        compiler_params=_cparams(2),
        name="conv_qk",
    )(proj3, conv_w)


def _split_dot(tri, v):
    hi = v.astype(BF16)
    r1 = v - hi.astype(F32)
    mid = r1.astype(BF16)
    lo = (r1 - mid.astype(F32)).astype(BF16)
    return (jnp.dot(tri, hi, preferred_element_type=F32)
            + jnp.dot(tri, mid, preferred_element_type=F32)
            + jnp.dot(tri, lo, preferred_element_type=F32))


def _gate_kernel(g_ref, bias_ref, o_ref):
    ti = lax.broadcasted_iota(jnp.int32, (BLOCK, BLOCK), 0)
    ui = lax.broadcasted_iota(jnp.int32, (BLOCK, BLOCK), 1)
    tril = jnp.where(ui <= ti, 1.0, 0.0).astype(BF16)
    triu = jnp.where(ui >= ti, 1.0, 0.0).astype(BF16)
    ch = lax.broadcasted_iota(jnp.int32, (BLOCK, 4 * H_M), 1)
    typ = lax.shift_right_logical(ch, 2)
    rloc = lax.broadcasted_iota(jnp.int32, (BLOCK, 4 * H_M), 0)

    def body(c, carry):
        r0 = pl.multiple_of(c * BLOCK, BLOCK)
        g = g_ref[0, pl.ds(r0, BLOCK), :][:, :4 * H_M] + bias_ref[...]
        valid = (rloc + r0) >= PAD
        lsig = -(jnp.maximum(-g, 0.0) + jnp.log1p(jnp.exp(-jnp.abs(g))))
        lf = jnp.where(valid, lsig, 0.0)
        cum = _split_dot(tril, lf)
        rcum = _split_dot(triu, lf)
        li = jnp.where(valid, g, -jnp.inf)
        out = jnp.where(typ == 1, cum, jnp.where(typ == 3, rcum, li))
        o_ref[0, pl.ds(r0, BLOCK), :] = out
        return carry

    lax.fori_loop(0, NBLK, body, 0)


def gate_prep(mg3, gate_bias):
    return pl.pallas_call(
        _gate_kernel,
        grid=(BATCH,),
        in_specs=[
            pl.BlockSpec((1, LP, 128), lambda b: (b, 0, 0)),
            pl.BlockSpec((1, 4 * H_M), lambda b: (0, 0)),
        ],
        out_specs=pl.BlockSpec((1, LP, 4 * H_M), lambda b: (b, 0, 0)),
        out_shape=jax.ShapeDtypeStruct((BATCH, LP, 4 * H_M), F32),
        compiler_params=_cparams(1),
        name="gate_prep",
    )(mg3, gate_bias.reshape(1, 4 * H_M))


MLSTM_HP = 2
MLSTM_MID = NBLK // 2


def _mlstm_kernel(qt_ref, k_ref, vt_ref, ot_ref, gr_ref, y_ref,
                  hs_ref, c_ref, n_ref, m_ref):
    c_ref[...] = jnp.zeros_like(c_ref)
    n_ref[...] = jnp.zeros_like(n_ref)
    m_ref[...] = jnp.zeros_like(m_ref)
    si = lax.broadcasted_iota(jnp.int32, (BLOCK, BLOCK), 0)
    ti = lax.broadcasted_iota(jnp.int32, (BLOCK, BLOCK), 1)
    mask_f = si <= ti
    mask_b = si >= ti

    def chain(c, hl, bwd, final):
        idx = 2 * hl + bwd
        r0 = c * BLOCK if isinstance(c, int) else pl.multiple_of(c * BLOCK, BLOCK)
        t_sl = pl.ds(r0, BLOCK)
        qt = qt_ref[0, hl * DK_M:(hl + 1) * DK_M, t_sl]
        k = k_ref[0, t_sl, hl * DK_M:(hl + 1) * DK_M]
        feat = slice(hl * DV_M, (hl + 1) * DV_M)
        vt = vt_ref[0, feat, t_sl]
        gr = gr_ref[0, hl, :, t_sl]
        li_r, b_r = gr[2 * bwd:2 * bwd + 1, :], gr[2 * bwd + 1:2 * bwd + 2, :]
        a_c = jnp.transpose(jnp.broadcast_to(li_r - b_r, (BLOCK, BLOCK)))
        b_end = b_r[:, 0:1] if bwd else b_r[:, BLOCK - 1:BLOCK]
        m_prev = m_ref[idx][:, 0:1]
        ct = c_ref[idx]
        nst = n_ref[idx]
        dmat = jnp.where(mask_b if bwd else mask_f, a_c + b_r, -jnp.inf)
        inter = b_r + m_prev
        m_t = jnp.maximum(inter, jnp.max(dmat, axis=0, keepdims=True))
        w_inter = jnp.exp(inter - m_t)
        st = jnp.dot(k, qt, preferred_element_type=F32) * jnp.exp(dmat - m_t)
        num = (w_inter * jnp.dot(ct.astype(BF16), qt, preferred_element_type=F32)
               + jnp.dot(vt, st.astype(BF16), preferred_element_type=F32))
        nq = jnp.dot(nst.astype(BF16), qt, preferred_element_type=F32)
        den = w_inter * nq + jnp.sum(st, axis=0, keepdims=True)
        h = num * (1.0 / jnp.maximum(jnp.abs(den), jnp.exp(-m_t)))
        if final:
            og = jax.nn.sigmoid(ot_ref[0, feat, t_sl].astype(F32))
            y_ref[0, feat, pl.ds(r0 - BLOCK, BLOCK)] = (og * (hs_ref[feat, t_sl] + h)).astype(BF16)
        else:
            hs_ref[feat, t_sl] = h
        ldec = b_end - b_r + li_r
        m_new = jnp.maximum(b_end + m_prev, jnp.max(ldec, axis=1, keepdims=True))
        w_c = jnp.exp(b_end + m_prev - m_new)
        w_s = jnp.exp(ldec - m_new)
        wvt = (vt.astype(F32) * w_s).astype(BF16)
        c_ref[idx] = w_c * ct + jnp.dot(wvt, k, preferred_element_type=F32)
        n_ref[idx] = w_c * nst + jnp.dot(w_s.astype(BF16), k, preferred_element_type=F32)
        m_ref[idx] = jnp.broadcast_to(m_new, (1, BLOCK))

    def first_half(i, carry):
        for hl in range(MLSTM_HP):
            chain(i, hl, 0, False)
            chain(NBLK - 1 - i, hl, 1, False)
        return carry

    def second_half(i, carry):
        for hl in range(MLSTM_HP):
            chain(i, hl, 0, True)
            chain(NBLK - 1 - i, hl, 1, True)
        return carry

    lax.fori_loop(0, MLSTM_MID, first_half, 0, unroll=8)
    for hl in range(MLSTM_HP):
        chain(MLSTM_MID, hl, 0, False)
        chain(MLSTM_MID, hl, 1, True)
    lax.fori_loop(MLSTM_MID + 1, NBLK - 1, second_half, 0, unroll=5)
    for hl in range(MLSTM_HP):
        chain(NBLK - 1, hl, 0, True)


def mlstm(q_t, qk_m, v_t, o_t, grow):
    hp = MLSTM_HP
    kw, vw = hp * DK_M, hp * DV_M
    return pl.pallas_call(
        _mlstm_kernel,
        grid=(BATCH, H_M // hp),
        in_specs=[
            pl.BlockSpec((1, kw, LP), lambda b, g: (b, g, 0)),
            pl.BlockSpec((1, LP, kw), lambda b, g: (b, 0, (H_M * DK_M) // kw + g)),
            pl.BlockSpec((1, vw, LP), lambda b, g: (b, g, 0)),
            pl.BlockSpec((1, vw, LP), lambda b, g: (b, g, 0)),
            pl.BlockSpec((1, hp, 4, LP), lambda b, g: (b, g, 0, 0)),
        ],
        out_specs=pl.BlockSpec((1, vw, SEQ), lambda b, g: (b, g, 0)),
        out_shape=jax.ShapeDtypeStruct((BATCH, H_M * DV_M, SEQ), BF16),
        scratch_shapes=[
            pltpu.VMEM((vw, LP), F32),
            pltpu.VMEM((2 * hp, DV_M, DK_M), F32),
            pltpu.VMEM((2 * hp, 1, DK_M), F32),
            pltpu.VMEM((2 * hp, 1, BLOCK), F32),
        ],
        compiler_params=_cparams(2),
        name="mlstm",
    )(q_t, qk_m, v_t, o_t, grow)


LOG2E = 1.4426950408889634
ATT_QB = 2
ATT_TQ = ATT_QB * BLOCK
ATT_BAND = (ATT_QB + 2) * BLOCK
ATT_GROUPS = (6, 6, 6, 6, 5)
assert BLOCK >= MAX_DISTANCE and ATT_BAND + sum(ATT_GROUPS) * BLOCK == LP


ATT_NQ = SEQ // ATT_TQ


def _attn_kernel(c_ref, *refs):
    q_refs = refs[:ATT_QB]
    (k1_ref, v1_ref, tab_ref, lam_ref, sg_ref, o_ref,
     s_a, s_b, mx_a, mx_b, k_ref, v_ref) = refs[ATT_QB:]
    h = pl.program_id(1)
    t = pl.program_id(2)

    @pl.when(t == 0)
    def _():
        for rep in range(2):
            k_ref[0, rep * LP:(rep + 1) * LP, :] = k1_ref[0]
            v_ref[0, rep * LP:(rep + 1) * LP, :] = v1_ref[0]

    scale = DK_DA ** -0.5 * LOG2E
    c_neg = c_ref[h, 0]
    c_pos = c_ref[h, 1]

    def groups_of(tile):
        koff = (ATT_QB * tile) * BLOCK
        out = [(koff, ATT_BAND, 0)]
        col = ATT_BAND
        for nblk in ATT_GROUPS:
            out.append((koff + col, nblk * BLOCK, col))
            col += nblk * BLOCK
        return out

    def lane_fold(acc, x, op):
        for j in range(x.shape[1] // BLOCK):
            piece = x[:, j * BLOCK:(j + 1) * BLOCK]
            acc = piece if acc is None else op(acc, piece)
        return acc

    def score(tile, s_ref, mx_ref):
        q = jnp.concatenate([r[0] for r in q_refs], axis=0)
        mx = [None, None]
        for gi, (koff, width, col0) in enumerate(groups_of(tile)):
            koff = pl.multiple_of(koff, BLOCK)
            if gi == 0:
                bias = tab_ref[0, 0]
            else:
                kpos = koff + lax.broadcasted_iota(jnp.int32, (1, width), 1)
                bias = jnp.where(kpos < LP, c_pos, jnp.where(kpos < LP + PAD, NEG_INF, c_neg))
            for m in range(2):
                kk = k_ref[0, pl.ds(koff, width), m * DK_DA:(m + 1) * DK_DA]
                s = lax.dot_general(q[:, m * DK_DA:(m + 1) * DK_DA], kk,
                                    (((1,), (1,)), ((), ())),
                                    preferred_element_type=F32) * scale + bias
                s_ref[m, :, col0:col0 + width] = s
                mx[m] = lane_fold(mx[m], s, jnp.maximum)
        for m in range(2):
            mx_ref[m] = mx[m]

    def finish(tile, s_ref, mx_ref):
        lp = lam_ref[...]
        lam = (jnp.exp(jnp.sum(lp[0:1] * lp[1:2], axis=1, keepdims=True))
               - jnp.exp(jnp.sum(lp[2:3] * lp[3:4], axis=1, keepdims=True)) + LAMBDA_INIT)
        row_max = [jnp.max(mx_ref[m], axis=1, keepdims=True) for m in range(2)]
        lsum = [None, None]
        acc = [None, None]
        for koff, width, col0 in groups_of(tile):
            koff = pl.multiple_of(koff, BLOCK)
            vv = v_ref[0, pl.ds(koff, width), :]
            for m in range(2):
                p = jnp.exp2(s_ref[m, :, col0:col0 + width] - row_max[m])
                lsum[m] = lane_fold(lsum[m], p, jnp.add)
                pv = jnp.dot(p.astype(BF16), vv, preferred_element_type=F32)
                acc[m] = pv if acc[m] is None else acc[m] + pv
        l1 = jnp.sum(lsum[0], axis=1, keepdims=True)
        l2 = jnp.sum(lsum[1], axis=1, keepdims=True)
        o = acc[0] / l1 - lam * (acc[1] / l2)
        o_ref[0] = (_rms(o, sg_ref[...]) * (1.0 - LAMBDA_INIT)).astype(BF16)

    even = lax.rem(t, 2) == 0
    inner = jnp.logical_and(t > 0, t < ATT_NQ)

    @pl.when(t == 0)
    def _():
        score(t, s_a, mx_a)

    @pl.when(jnp.logical_and(inner, even))
    def _():
        score(t, s_a, mx_a)
        finish(t - 1, s_b, mx_b)

    @pl.when(jnp.logical_and(inner, jnp.logical_not(even)))
    def _():
        score(t, s_b, mx_b)
        finish(t - 1, s_a, mx_a)

    @pl.when(t == ATT_NQ)
    def _():
        if (ATT_NQ - 1) % 2 == 0:
            finish(t - 1, s_a, mx_a)
        else:
            finish(t - 1, s_b, mx_b)


def diff_attention(consts, proj3, tab, lam_params, subln):
    nq = ATT_NQ
    kblk0 = COL_DA_K // (2 * DK_DA)
    vblk0 = COL_DA_V // DV_DA
    scored = lambda i: jnp.minimum(i, nq - 1)
    finished = lambda i: jnp.maximum(i - 1, 0)

    def tab_map(b, h, i):
        tile = scored(i)
        case = jnp.where(tile == 0, 0, jnp.where(tile == nq - 1, 2, 1))
        return (h, case, 0, 0)

    return pl.pallas_call(
        _attn_kernel,
        grid=(BATCH, H_DA, nq + 1),
        in_specs=[
            pl.BlockSpec(memory_space=pltpu.SMEM),
            *[pl.BlockSpec((1, BLOCK, 2 * DK_DA),
                           functools.partial(
                               lambda b, h, i, r: (b, ATT_QB * scored(i) + 1 + r, h), r=r))
              for r in range(ATT_QB)],
            pl.BlockSpec((1, LP, 2 * DK_DA), lambda b, h, i: (b, 0, kblk0 + h)),
            pl.BlockSpec((1, LP, DV_DA), lambda b, h, i: (b, 0, vblk0 + h)),
            pl.BlockSpec((1, 1, ATT_TQ, ATT_BAND), tab_map),
            pl.BlockSpec((4, DK_DA), lambda b, h, i: (0, 0)),
            pl.BlockSpec((1, DV_DA), lambda b, h, i: (0, 0)),
        ],
        out_specs=pl.BlockSpec((1, ATT_TQ, DV_DA), lambda b, h, i: (b, finished(i), h)),
        out_shape=jax.ShapeDtypeStruct((BATCH, SEQ, H_DA * DV_DA), BF16),
        scratch_shapes=[pltpu.VMEM((2, ATT_TQ, LP), F32),
                        pltpu.VMEM((2, ATT_TQ, LP), F32),
                        pltpu.VMEM((2, ATT_TQ, BLOCK), F32),
                        pltpu.VMEM((2, ATT_TQ, BLOCK), F32),
                        pltpu.VMEM((1, 2 * LP, 2 * DK_DA), BF16),
                        pltpu.VMEM((1, 2 * LP, DV_DA), BF16)],
        compiler_params=_cparams(3),
        name="diff_attn",
    )(consts, *([proj3] * (ATT_QB + 2)), tab, lam_params, subln.reshape(1, DV_DA))


def _bias_tables(rel_bias):
    rb = rel_bias.astype(F32)
    span = 1024
    assert span >= ATT_TQ + ATT_BAND
    rel = jnp.arange(span, dtype=jnp.int32) - span // 2
    nb = N_BUCKETS // 2
    max_exact = nb // 2
    n = jnp.abs(rel)
    nf = jnp.maximum(n, 1).astype(F32)
    large = max_exact + (jnp.log(nf / max_exact) / math.log(MAX_DISTANCE / max_exact)
                         * (nb - max_exact)).astype(jnp.int32)
    large = jnp.minimum(large, nb - 1)
    bucket = jnp.where(rel > 0, nb, 0) + jnp.where(n < max_exact, n, large)
    hit = bucket[None, :, None] == jnp.arange(N_BUCKETS, dtype=jnp.int32)
    by_rel = jnp.sum(jnp.where(hit, rb.T[:, None, :], 0.0), axis=-1)
    shifted = jnp.tile(by_rel, (1, ATT_TQ))[:, :ATT_TQ * (span - 1)].reshape(
        H_DA, ATT_TQ, span - 1)
    c0 = span // 2 - BLOCK
    gen = shifted[:, :, c0:c0 + ATT_BAND]
    c_neg = rb[nb - 1]
    c_pos = rb[N_BUCKETS - 1]
    jj = jnp.arange(ATT_BAND, dtype=jnp.int32)[None, None, :]
    first = jnp.where(jj < PAD, NEG_INF, gen)
    wrap0 = ATT_BAND - BLOCK
    wrapped = jnp.where(jj - wrap0 < PAD, NEG_INF, c_neg[:, None, None])
    last = jnp.where(jj >= wrap0, wrapped, gen)
    tab = jnp.stack([first, gen, last], axis=1)
    consts = jnp.stack([c_neg, c_pos], axis=1)
    return tab * LOG2E, consts * LOG2E


def _mix_kernel(ya_ref, ym_ref, wa_ref, wm_ref, ga_ref, gm_ref, o_ref, wa_bf, wm_bf):
    @pl.when(pl.program_id(1) == 0)
    def _():
        wa_bf[...] = wa_ref[...].astype(BF16)
        wm_bf[...] = wm_ref[...].astype(BF16)

    a = jnp.dot(ya_ref[...], wa_bf[...], preferred_element_type=F32)
    m = jnp.dot(ym_ref[...], wm_bf[...], preferred_element_type=F32)
    o_ref[...] = (ga_ref[...].astype(F32) * a + gm_ref[...].astype(F32) * m).astype(BF16)


def branch_mix(y_da, y_m, w_da, w_m, gate, *, tm=512, tn=1024):
    m, k = y_da.shape
    nj = D_MODEL // tn
    return pl.pallas_call(
        _mix_kernel,
        grid=(nj, m // tm),
        in_specs=[
            pl.BlockSpec((tm, k), lambda j, i: (i, 0)),
            pl.BlockSpec((tm, k), lambda j, i: (i, 0)),
            pl.BlockSpec((k, tn), lambda j, i: (0, j)),
            pl.BlockSpec((k, tn), lambda j, i: (0, j)),
            pl.BlockSpec((tm, tn), lambda j, i: (i, j)),
            pl.BlockSpec((tm, tn), lambda j, i: (i, nj + j)),
        ],
        out_specs=pl.BlockSpec((tm, tn), lambda j, i: (i, j)),
        out_shape=jax.ShapeDtypeStruct((m, D_MODEL), BF16),
        scratch_shapes=[pltpu.VMEM((k, tn), BF16), pltpu.VMEM((k, tn), BF16)],
        compiler_params=_cparams(2),
        name="branch_mix",
    )(y_da, y_m, w_da, w_m, gate, gate)


FFN_TM = 1024
HALF = D_MODEL // 2


def _pack_bf16_pairs(v):
    lo = lax.bitcast_convert_type(v[:, :HALF].astype(BF16).astype(F32), jnp.uint32)
    hi = lax.bitcast_convert_type(v[:, HALF:].astype(BF16).astype(F32), jnp.uint32)
    return (hi & jnp.uint32(0xFFFF0000)) | lax.shift_right_logical(lo, jnp.uint32(16))


def _unpack_bf16_pairs(w):
    lo = lax.bitcast_convert_type(lax.shift_left(w, jnp.uint32(16)), F32).astype(BF16)
    hi = lax.bitcast_convert_type(w & jnp.uint32(0xFFFF0000), F32).astype(BF16)
    return lo, hi


def _ffn_prep_kernel(h_ref, g_ref, wr_ref, br_ref, hn_ref, e_ref, w_ref, r_ref, cnt_ref, base_ref):
    @pl.when(pl.program_id(0) == 0)
    def _():
        base_ref[...] = jnp.zeros_like(base_ref)

    hn = _rms(h_ref[...], g_ref[...])
    hn_ref[...] = _pack_bf16_pairs(hn)
    logits = jnp.dot(hn.astype(BF16), wr_ref[...].astype(BF16),
                     preferred_element_type=F32) + br_ref[...]
    lane = lax.broadcasted_iota(jnp.int32, (FFN_TM, N_EXPERTS), 1)
    lane_o = lax.broadcasted_iota(jnp.int32, (FFN_TM, 128), 1)
    ti = lax.broadcasted_iota(jnp.int32, (FFN_TM, FFN_TM), 0)
    ui = lax.broadcasted_iota(jnp.int32, (FFN_TM, FFN_TM), 1)
    tril = jnp.where(ui <= ti, 1.0, 0.0).astype(BF16)
    e_out = jnp.zeros((FFN_TM, 128), jnp.int32)
    r_out = jnp.zeros((FFN_TM, 128), jnp.int32)
    l_out = jnp.full((FFN_TM, 128), -jnp.inf, F32)
    base = base_ref[...]
    l = logits
    for kk in range(TOP_K):
        mk = jnp.max(l, axis=1, keepdims=True)
        ik = jnp.min(jnp.where(l == mk, lane, N_EXPERTS), axis=1, keepdims=True)
        hit = lane == ik
        oh = jnp.where(hit, 1.0, 0.0)
        cum = jnp.dot(tril, oh.astype(BF16), preferred_element_type=F32)
        rank = jnp.sum(oh * (cum + base), axis=1, keepdims=True) - 1.0
        base = base + jnp.sum(oh, axis=0, keepdims=True)
        e_out = jnp.where(lane_o == kk, ik, e_out)
        r_out = jnp.where(lane_o == kk, rank.astype(jnp.int32), r_out)
        l_out = jnp.where(lane_o == kk, mk, l_out)
        l = jnp.where(hit, -jnp.inf, l)
    base_ref[...] = base
    cnt_ref[...] = base
    ex = jnp.exp(l_out - jnp.max(l_out, axis=1, keepdims=True))
    e_ref[...] = e_out
    r_ref[...] = r_out
    w_ref[...] = ex / jnp.sum(ex, axis=1, keepdims=True)


def ffn_prep(h2, gain, w_router, b_router):
    row = lambda i: (i, 0)
    fixed = lambda i: (0, 0)
    return pl.pallas_call(
        _ffn_prep_kernel,
        grid=(N_TOK // FFN_TM,),
        in_specs=[
            pl.BlockSpec((FFN_TM, D_MODEL), row),
            pl.BlockSpec((1, D_MODEL), fixed),
            pl.BlockSpec((D_MODEL, N_EXPERTS), fixed),
            pl.BlockSpec((1, N_EXPERTS), fixed),
        ],
        out_specs=[
            pl.BlockSpec((FFN_TM, HALF), row),
            pl.BlockSpec((FFN_TM, 128), row),
            pl.BlockSpec((FFN_TM, 128), row),
            pl.BlockSpec((FFN_TM, 128), row),
            pl.BlockSpec((1, N_EXPERTS), fixed),
        ],
        out_shape=[
            jax.ShapeDtypeStruct((N_TOK, HALF), jnp.uint32),
            jax.ShapeDtypeStruct((N_TOK, 128), jnp.int32),
            jax.ShapeDtypeStruct((N_TOK, 128), F32),
            jax.ShapeDtypeStruct((N_TOK, 128), jnp.int32),
            jax.ShapeDtypeStruct((1, N_EXPERTS), F32),
        ],
        scratch_shapes=[pltpu.VMEM((1, N_EXPERTS), F32)],
        compiler_params=_cparams(1),
        name="ffn_prep",
    )(h2, gain.reshape(1, D_MODEL), w_router, b_router.reshape(1, N_EXPERTS))


SEG_ALIGN = 128
MOE_R = N_ASSIGN + N_EXPERTS * SEG_ALIGN


def _plan(counts_f, top_e, rank):
    counts = counts_f[0].astype(jnp.int32)
    seg_rows = (counts + SEG_ALIGN - 1) // SEG_ALIGN * SEG_ALIGN
    seg_start = jnp.cumsum(seg_rows) - seg_rows
    eq = top_e[:, :, None] == jnp.arange(N_EXPERTS, dtype=jnp.int32)[None, None, :]
    dest = jnp.sum(jnp.where(eq, seg_start[None, None, :], 0), axis=-1) + rank
    used = jnp.sum(seg_rows)
    slack = jnp.stack([used, (MOE_R - used) // SEG_ALIGN])
    return (dest.reshape(N_ASSIGN).astype(jnp.int32), seg_start.astype(jnp.int32),
            seg_rows.astype(jnp.int32), (seg_start + counts).astype(jnp.int32),
            (seg_rows - counts).astype(jnp.int32), slack.astype(jnp.int32))


def _zero_slack(slack_ref, zero_block, dst_rows, sem):
    zero_block[...] = jnp.zeros_like(zero_block)

    def copy(j):
        r0 = pl.multiple_of(slack_ref[0] + j * SEG_ALIGN, SEG_ALIGN)
        return pltpu.make_async_copy(zero_block, dst_rows(pl.ds(r0, SEG_ALIGN)), sem)

    def start(j, carry):
        copy(j).start()
        return carry

    def wait(j, carry):
        copy(j).wait()
        return carry

    lax.fori_loop(0, slack_ref[1], start, 0)
    lax.fori_loop(0, slack_ref[1], wait, 0)


DISP_TOK = 512


def _dispatch_kernel(dest_ref, pad0_ref, padn_ref, slack_ref, hn_ref, xs_ref,
                     zrow_ref, zblk_ref, sem, zsem):
    i = pl.program_id(0)

    @pl.when(i == 0)
    def _():
        zrow_ref[...] = jnp.zeros_like(zrow_ref)
        _zero_slack(slack_ref, zblk_ref, lambda rows: xs_ref.at[rows], zsem.at[0])

        def expert(e, carry):
            p0 = pad0_ref[e]
            pn = padn_ref[e]

            def zstart(r, c2):
                pltpu.make_async_copy(zrow_ref, xs_ref.at[pl.ds(p0 + r, 1)], zsem.at[0]).start()
                return c2

            def zwait(r, c2):
                pltpu.make_async_copy(zrow_ref, xs_ref.at[pl.ds(p0, 1)], zsem.at[0]).wait()
                return c2

            lax.fori_loop(0, pn, zstart, 0)
            lax.fori_loop(0, pn, zwait, 0)
            return carry

        lax.fori_loop(0, N_EXPERTS, expert, 0)

    def tok(t, carry):
        a = (i * DISP_TOK + t) * TOP_K
        for kk in range(TOP_K):
            pltpu.make_async_copy(hn_ref.at[pl.ds(t, 1)], xs_ref.at[pl.ds(dest_ref[a + kk], 1)],
                                  sem.at[0]).start(priority=kk % 2)
        return carry

    lax.fori_loop(0, DISP_TOK, tok, 0, unroll=4)
    for kk in range(TOP_K):
        pltpu.make_async_copy(hn_ref, xs_ref.at[pl.ds(0, DISP_TOK)], sem.at[0]).wait()


def dispatch(dest, pad0, padn, slack, hn_packed):
    grid_spec = pltpu.PrefetchScalarGridSpec(
        num_scalar_prefetch=4,
        grid=(N_TOK // DISP_TOK,),
        in_specs=[pl.BlockSpec((DISP_TOK, HALF), lambda i, d, p0, pn, z: (i, 0))],
        out_specs=pl.BlockSpec(memory_space=pl.ANY),
        scratch_shapes=[pltpu.VMEM((1, HALF), jnp.uint32),
                        pltpu.VMEM((SEG_ALIGN, HALF), jnp.uint32),
                        pltpu.SemaphoreType.DMA((1,)), pltpu.SemaphoreType.DMA((1,))],
    )
    return pl.pallas_call(
        _dispatch_kernel,
        grid_spec=grid_spec,
        out_shape=jax.ShapeDtypeStruct((MOE_R, HALF), jnp.uint32),
        compiler_params=_cparams(1),
        name="moe_dispatch",
    )(dest, pad0, padn, slack, hn_packed)


MOE_CH = 2 * SEG_ALIGN
MOE_TF = 1024
MOE_NF = D_FF // MOE_TF


class _CopyGroup:
    def __init__(self, copies):
        self.copies = copies

    def start(self, priority=0):
        for cp in self.copies:
            cp.start(priority=priority)

    def wait(self):
        for cp in self.copies:
            cp.wait()


W_PIECES = 8
W_PER_CHUNK = 3


N_STATE = 4


def _stream_rows(step, n_steps, start, rows, next_start, next_rows, state,
                 make_in, make_out, make_tail_in, make_tail_out,
                 compute_chunk, compute_tail, before_first_wait, next_weight_piece):
    has_next = step + 1 < n_steps

    def request_weights(first, count):
        def one(p, carry):
            @pl.when(jnp.logical_and(has_next, p < W_PIECES))
            def _():
                next_weight_piece(p).start()
            return carry

        lax.fori_loop(first, first + count, one, 0)

    @pl.when(step == 0)
    def _():
        for j in range(N_STATE):
            state[j] = 0

    n_ch = lax.shift_right_logical(rows, MOE_CH.bit_length() - 1)
    tail = rows - n_ch * MOE_CH
    tail_row = start + n_ch * MOE_CH
    g0 = state[0]
    feeds_next = jnp.logical_and(step + 1 < n_steps, next_rows >= MOE_CH)

    def chunk_row(c):
        return start + c * MOE_CH

    @pl.when(tail > 0)
    def _():
        make_tail_in(tail_row).start(priority=1)

    @pl.when(jnp.logical_and(n_ch > 0, state[3] == 0))
    def _():
        make_in(start, lax.rem(g0, 2)).start(priority=1)

    before_first_wait()

    @pl.when(tail > 0)
    def _():
        make_tail_in(tail_row).wait()
        compute_tail()
        make_tail_out(tail_row).start(priority=1)

    def body(c, carry):
        slot = lax.rem(g0 + c, 2)
        make_in(chunk_row(c), slot).wait()

        @pl.when(c + 1 < n_ch)
        def _():
            make_in(chunk_row(c + 1), 1 - slot).start(priority=1)

        @pl.when(jnp.logical_and(c + 1 == n_ch, feeds_next))
        def _():
            make_in(next_start, 1 - slot).start(priority=1)

        request_weights(c * W_PER_CHUNK, W_PER_CHUNK)

        @pl.when(state[1 + slot] == 1)
        def _():
            make_out(chunk_row(c), slot).wait()

        compute_chunk(slot)
        make_out(chunk_row(c), slot).start(priority=1)
        state[1 + slot] = 1
        return carry

    lax.fori_loop(0, n_ch, body, 0)
    request_weights(n_ch * W_PER_CHUNK, W_PIECES)
    state[0] = g0 + n_ch
    state[3] = jnp.where(jnp.logical_and(n_ch > 0, feeds_next), 1, 0)

    @pl.when(tail > 0)
    def _():
        make_tail_out(tail_row).wait()

    @pl.when(step == n_steps - 1)
    def _():
        for slot in range(2):
            @pl.when(state[1 + slot] == 1)
            def _():
                make_out(0, slot).wait()
                state[1 + slot] = 0


def _rows_at(row0, n):
    return pl.ds(row0 if isinstance(row0, int) else pl.multiple_of(row0, SEG_ALIGN), n)


def _moe_up_kernel(seg_ref, rows_ref, slack_ref, xs_ref, w_hbm, bg_ref, bl_ref, act_ref,
                   xbuf, obuf, xtail, otail, wbuf, wg_bf, wl_bf, state, isem, osem, tsem, wsem):
    f = pl.program_id(0)
    e = pl.program_id(1)
    start = seg_ref[e]
    rows = rows_ref[e]
    e_next = lax.rem(e + 1, N_EXPERTS)
    f_next = jnp.where(e == N_EXPERTS - 1, f + 1, f)
    step = f * N_EXPERTS + e
    n_steps = MOE_NF * N_EXPERTS
    wslot = lax.rem(step, 2)

    def weight_piece(expert, ftile, slot, p):
        per_half = W_PIECES // 2
        band = D_MODEL // per_half
        t = p // per_half if isinstance(p, int) else lax.shift_right_logical(
            p, per_half.bit_length() - 1)
        r0 = (p - t * per_half) * band
        r0 = r0 if isinstance(r0, int) else pl.multiple_of(r0, band)
        col0 = pl.multiple_of((t * MOE_NF + ftile) * MOE_TF, MOE_TF)
        return pltpu.make_async_copy(
            w_hbm.at[expert, pl.ds(r0, band), pl.ds(col0, MOE_TF)],
            wbuf.at[slot, t, pl.ds(r0, band)], wsem.at[slot, p])

    @pl.when(step == 0)
    def _():
        for p in range(W_PIECES):
            weight_piece(e, f, wslot, p).start()

    def make_in(r0, slot):
        return pltpu.make_async_copy(xs_ref.at[_rows_at(r0, MOE_CH)], xbuf.at[slot], isem.at[slot])

    def make_out(r0, slot):
        return pltpu.make_async_copy(obuf.at[slot], act_ref.at[f, _rows_at(r0, MOE_CH)],
                                     osem.at[slot])

    def make_tail_in(r0):
        return pltpu.make_async_copy(xs_ref.at[_rows_at(r0, SEG_ALIGN)], xtail, tsem.at[0])

    def make_tail_out(r0):
        return pltpu.make_async_copy(otail, act_ref.at[f, _rows_at(r0, SEG_ALIGN)], tsem.at[1])

    def cast_weights():
        per_half = W_PIECES // 2
        band = D_MODEL // per_half
        for p in range(W_PIECES):
            weight_piece(e, f, wslot, p).wait()
            t, rows_p = p // per_half, pl.ds((p % per_half) * band, band)
            (wg_bf, wl_bf)[t][rows_p, :] = wbuf[wslot, t, rows_p, :].astype(BF16)

    def expert_mlp(words):
        lo, hi = _unpack_bf16_pairs(words)
        glu = (jnp.dot(lo, wg_bf[:HALF, :], preferred_element_type=F32)
               + jnp.dot(hi, wg_bf[HALF:, :], preferred_element_type=F32) + bg_ref[0])
        lin = (jnp.dot(lo, wl_bf[:HALF, :], preferred_element_type=F32)
               + jnp.dot(hi, wl_bf[HALF:, :], preferred_element_type=F32) + bl_ref[0])
        glu = jnp.minimum(glu, SWIGLU_LIMIT)
        lin = jnp.clip(lin, -SWIGLU_LIMIT, SWIGLU_LIMIT)
        return (glu * jax.nn.sigmoid(SWIGLU_ALPHA * glu) * (lin + 1.0)).astype(BF16)

    def compute_chunk(slot):
        obuf[slot] = expert_mlp(xbuf[slot])

    def compute_tail():
        otail[...] = expert_mlp(xtail[...])

    _stream_rows(step, n_steps, start, rows,
                 seg_ref[e_next], rows_ref[e_next], state,
                 make_in, make_out, make_tail_in, make_tail_out,
                 compute_chunk, compute_tail, cast_weights,
                 lambda p: weight_piece(e_next, f_next, 1 - wslot, p))

    @pl.when(e == N_EXPERTS - 1)
    def _():
        _zero_slack(slack_ref, otail, lambda rr: act_ref.at[f, rr], tsem.at[1])


def moe_up(seg_start, seg_rows, slack, xs, w1, b1):
    grid_spec = pltpu.PrefetchScalarGridSpec(
        num_scalar_prefetch=3,
        grid=(MOE_NF, N_EXPERTS),
        in_specs=[
            pl.BlockSpec(memory_space=pl.ANY),
            pl.BlockSpec(memory_space=pl.ANY),
            pl.BlockSpec((1, 1, MOE_TF), lambda f, e, s, r, z: (e, 0, f)),
            pl.BlockSpec((1, 1, MOE_TF), lambda f, e, s, r, z: (e, 0, MOE_NF + f)),
        ],
        out_specs=pl.BlockSpec(memory_space=pl.ANY),
        scratch_shapes=[
            pltpu.VMEM((2, MOE_CH, HALF), jnp.uint32),
            pltpu.VMEM((2, MOE_CH, MOE_TF), BF16),
            pltpu.VMEM((SEG_ALIGN, HALF), jnp.uint32),
            pltpu.VMEM((SEG_ALIGN, MOE_TF), BF16),
            pltpu.VMEM((2, 2, D_MODEL, MOE_TF), F32),
            pltpu.VMEM((D_MODEL, MOE_TF), BF16),
            pltpu.VMEM((D_MODEL, MOE_TF), BF16),
            pltpu.SMEM((N_STATE,), jnp.int32),
            pltpu.SemaphoreType.DMA((2,)),
            pltpu.SemaphoreType.DMA((2,)),
            pltpu.SemaphoreType.DMA((2,)),
            pltpu.SemaphoreType.DMA((2, W_PIECES)),
        ],
    )
    b13 = b1.reshape(N_EXPERTS, 1, 2 * D_FF)
    return pl.pallas_call(
        _moe_up_kernel,
        grid_spec=grid_spec,
        out_shape=jax.ShapeDtypeStruct((MOE_NF, MOE_R, MOE_TF), BF16),
        compiler_params=pltpu.CompilerParams(
            dimension_semantics=("arbitrary", "arbitrary"), vmem_limit_bytes=MOE_VMEM_LIMIT),
        name="moe_up",
    )(seg_start, seg_rows, slack, xs, w1, b13, b13)


def _moe_down_kernel(seg_ref, rows_ref, slack_ref, act_ref, w_hbm, b_ref, y_ref,
                     xbuf, obuf, xtail, otail, wbuf, w_bf, state, isem, osem, tsem, wsem):
    e = pl.program_id(0)
    start = seg_ref[e]
    rows = rows_ref[e]
    e_next = lax.rem(e + 1, N_EXPERTS)
    wslot = lax.rem(e, 2)

    def weight_piece(expert, slot, p):
        band = D_FF // W_PIECES
        r0 = p * band if isinstance(p, int) else pl.multiple_of(p * band, band)
        return pltpu.make_async_copy(w_hbm.at[expert, pl.ds(r0, band)],
                                     wbuf.at[slot, pl.ds(r0, band)], wsem.at[slot, p])

    @pl.when(e == 0)
    def _():
        for p in range(W_PIECES):
            weight_piece(e, wslot, p).start()

    def make_in(r0, slot):
        return _CopyGroup([
            pltpu.make_async_copy(act_ref.at[j, _rows_at(r0, MOE_CH)],
                                  xbuf.at[slot, :, pl.ds(j * MOE_TF, MOE_TF)], isem.at[slot])
            for j in range(MOE_NF)])

    def make_out(r0, slot):
        return pltpu.make_async_copy(obuf.at[slot], y_ref.at[_rows_at(r0, MOE_CH)], osem.at[slot])

    def make_tail_in(r0):
        return _CopyGroup([
            pltpu.make_async_copy(act_ref.at[j, _rows_at(r0, SEG_ALIGN)],
                                  xtail.at[:, pl.ds(j * MOE_TF, MOE_TF)], tsem.at[0])
            for j in range(MOE_NF)])

    def make_tail_out(r0):
        return pltpu.make_async_copy(otail, y_ref.at[_rows_at(r0, SEG_ALIGN)], tsem.at[1])

    def cast_weights():
        band = D_FF // W_PIECES
        for p in range(W_PIECES):
            weight_piece(e, wslot, p).wait()
            rows_p = pl.ds(p * band, band)
            w_bf[rows_p, :] = wbuf[wslot, rows_p, :].astype(BF16)

    def expert_out(a):
        return _pack_bf16_pairs(jnp.dot(a, w_bf[...], preferred_element_type=F32) + b_ref[0])

    def compute_chunk(slot):
        obuf[slot] = expert_out(xbuf[slot])

    def compute_tail():
        otail[...] = expert_out(xtail[...])

    _stream_rows(e, N_EXPERTS, start, rows, seg_ref[e_next], rows_ref[e_next], state,
                 make_in, make_out, make_tail_in, make_tail_out,
                 compute_chunk, compute_tail, cast_weights,
                 lambda p: weight_piece(e_next, 1 - wslot, p))

    @pl.when(e == N_EXPERTS - 1)
    def _():
        _zero_slack(slack_ref, otail, lambda rr: y_ref.at[rr], tsem.at[1])


def moe_down(seg_start, seg_rows, slack, act, w2, b2):
    grid_spec = pltpu.PrefetchScalarGridSpec(
        num_scalar_prefetch=3,
        grid=(N_EXPERTS,),
        in_specs=[
            pl.BlockSpec(memory_space=pl.ANY),
            pl.BlockSpec(memory_space=pl.ANY),
            pl.BlockSpec((1, 1, D_MODEL), lambda e, s, r, z: (e, 0, 0)),
        ],
        out_specs=pl.BlockSpec(memory_space=pl.ANY),
        scratch_shapes=[
            pltpu.VMEM((2, MOE_CH, D_FF), BF16),
            pltpu.VMEM((2, MOE_CH, HALF), jnp.uint32),
            pltpu.VMEM((SEG_ALIGN, D_FF), BF16),
            pltpu.VMEM((SEG_ALIGN, HALF), jnp.uint32),
            pltpu.VMEM((2, D_FF, D_MODEL), F32),
            pltpu.VMEM((D_FF, D_MODEL), BF16),
            pltpu.SMEM((N_STATE,), jnp.int32),
            pltpu.SemaphoreType.DMA((2,)),
            pltpu.SemaphoreType.DMA((2,)),
            pltpu.SemaphoreType.DMA((2,)),
            pltpu.SemaphoreType.DMA((2, W_PIECES)),
        ],
    )
    return pl.pallas_call(
        _moe_down_kernel,
        grid_spec=grid_spec,
        out_shape=jax.ShapeDtypeStruct((MOE_R, HALF), jnp.uint32),
        compiler_params=pltpu.CompilerParams(
            dimension_semantics=("arbitrary",), vmem_limit_bytes=MOE_VMEM_LIMIT),
        name="moe_down",
    )(seg_start, seg_rows, slack, act, w2, b2.reshape(N_EXPERTS, 1, D_MODEL))


COMB_TM = 256
COMB_NT = N_TOK // COMB_TM


def _combine_kernel(dest_ref, y_ref, w_ref, h_ref, g_ref, o_ref, buf, sem):
    i = pl.program_id(0)

    def fetch(tile, slot):
        def tok(t, carry):
            a = (tile * COMB_TM + t) * TOP_K
            for kk in range(TOP_K):
                pltpu.make_async_copy(y_ref.at[pl.ds(dest_ref[a + kk], 1)],
                                      buf.at[slot, kk, pl.ds(t, 1)], sem.at[slot]).start(
                                          priority=kk % 2)
            return carry

        lax.fori_loop(0, COMB_TM, tok, 0, unroll=4)

    @pl.when(i == 0)
    def _():
        fetch(0, 0)

    slot = lax.rem(i, 2)

    @pl.when(i + 1 < COMB_NT)
    def _():
        fetch(i + 1, 1 - slot)

    for kk in range(TOP_K):
        pltpu.make_async_copy(y_ref.at[pl.ds(0, COMB_TM)], buf.at[slot, kk], sem.at[slot]).wait()
    w = w_ref[...]
    lo = h_ref[:, :HALF]
    hi = h_ref[:, HALF:]
    for kk in range(TOP_K):
        words = buf[slot, kk]
        wk = w[:, kk:kk + 1]
        lo = lo + wk * lax.bitcast_convert_type(lax.shift_left(words, jnp.uint32(16)), F32)
        hi = hi + wk * lax.bitcast_convert_type(words & jnp.uint32(0xFFFF0000), F32)
    ms = (jnp.sum(lo * lo, axis=-1, keepdims=True)
          + jnp.sum(hi * hi, axis=-1, keepdims=True)) * (1.0 / D_MODEL)
    inv = lax.rsqrt(ms + EPS)
    o_ref[:, :HALF] = lo * inv * g_ref[:, :HALF]
    o_ref[:, HALF:] = hi * inv * g_ref[:, HALF:]


def combine(dest, y, weight, h2, gain):
    grid_spec = pltpu.PrefetchScalarGridSpec(
        num_scalar_prefetch=1,
        grid=(COMB_NT,),
        in_specs=[
            pl.BlockSpec(memory_space=pl.ANY),
            pl.BlockSpec((COMB_TM, 128), lambda i, d: (i, 0)),
            pl.BlockSpec((COMB_TM, D_MODEL), lambda i, d: (i, 0)),
            pl.BlockSpec((1, D_MODEL), lambda i, d: (0, 0)),
        ],
        out_specs=pl.BlockSpec((COMB_TM, D_MODEL), lambda i, d: (i, 0)),
        scratch_shapes=[pltpu.VMEM((2, TOP_K, COMB_TM, HALF), jnp.uint32),
                        pltpu.SemaphoreType.DMA((2,))],
    )
    return pl.pallas_call(
        _combine_kernel,
        grid_spec=grid_spec,
        out_shape=jax.ShapeDtypeStruct((N_TOK, D_MODEL), F32),
        compiler_params=_cparams(1),
        name="moe_combine",
    )(dest, y, weight, h2, gain.reshape(1, D_MODEL))


def kernel(x, meta_tokens, rel_bias, norm_mix, w_in, conv_w, gate_bias_m, lambda_params, subln_da,
           w_branch_da, w_branch_m, w_gate, b_gate, w_out, norm_ffn, w_router, b_router,
           w1, b1, w2, b2, norm_final):
    layer = 0
    xn_pad, xn_real = norm_in(x, meta_tokens, norm_mix[layer])
    xn_pad2 = xn_pad.reshape(BATCH * LP, D_MODEL)
    w_in_t = w_in[layer].T
    proj = matmul(xn_pad2, w_in_t, n_cols=PROJ_COLS, tm=768, tn=1024, w_transposed=True,
                  name="proj_in")
    proj3 = proj.reshape(BATCH, LP, PROJ_COLS)
    w_g_t = jnp.pad(w_in_t[COL_M_G:], ((0, 128 - 4 * H_M), (0, 0)))
    mg = matmul(xn_pad2, w_g_t, n_cols=128, tm=768, tn=128, out_dtype=F32, w_transposed=True,
                name="proj_gates")
    gate = matmul(xn_real.reshape(N_TOK, D_MODEL), w_gate[layer], n_cols=2 * D_MODEL,
                  tm=1024, tn=1024, bias=b_gate[layer], act="sigmoid", name="mix_gate")

    tab, consts = _bias_tables(rel_bias)
    y_da = diff_attention(consts, proj3, tab, lambda_params[layer], subln_da[layer])

    qk_m = conv_qk(proj3, conv_w[layer])
    gp = gate_prep(mg.reshape(BATCH, LP, 128), gate_bias_m[layer])
    gp4 = gp.reshape(BATCH, LP, 4, H_M)
    grow = jnp.transpose(gp4, (0, 3, 2, 1))
    swap = lambda t: jnp.swapaxes(t, 1, 2)
    y_m = swap(mlstm(swap(qk_m[:, :, :H_M * DK_M]), qk_m,
                     swap(proj3[:, :, COL_M_V:COL_M_V + H_M * DV_M]),
                     swap(proj3[:, :, COL_M_O:COL_M_O + H_M * DV_M]), grow))

    mixed = branch_mix(y_da.reshape(N_TOK, H_DA * DV_DA), y_m.reshape(N_TOK, H_M * DV_M),
                       w_branch_da[layer], w_branch_m[layer], gate)
    h2 = matmul(mixed, w_out[layer], n_cols=D_MODEL, tm=1024, tn=1024,
                res=x.reshape(N_TOK, D_MODEL), out_dtype=F32, name="out_proj")

    hn_packed, top_e, weight, rank, counts = ffn_prep(
        h2, norm_ffn[layer], w_router[layer], b_router[layer])
    dest, seg_start, seg_rows, pad0, padn, slack = _plan(
        counts, top_e[:, :TOP_K], rank[:, :TOP_K])
    xs = dispatch(dest, pad0, padn, slack, hn_packed)
    act = moe_up(seg_start, seg_rows, slack, xs, w1[layer], b1[layer])
    y = moe_down(seg_start, seg_rows, slack, act, w2[layer], b2[layer])
    out = combine(dest, y, weight, h2, norm_final)
    return out.reshape(BATCH, SEQ, D_MODEL)
```

```python
import functools
import math

import jax
import jax.numpy as jnp
from jax import lax
from jax.experimental import pallas as pl
from jax.experimental.pallas import tpu as pltpu

F32 = jnp.float32
BF16 = jnp.bfloat16

D_MODEL = 2048
BATCH = 2
SEQ = 4096
N_META = 16
BLOCK = 128
PAD = (-N_META) % BLOCK
LP = PAD + N_META + SEQ
NBLK = LP // BLOCK
EPS = 1e-6
NEG_INF = -1e30

H_DA = 4
DK_DA = 128
DV_DA = 256
H_M = 4
DK_M = 128
DV_M = 256
CONV_W = 5
N_BUCKETS = 32
MAX_DISTANCE = 128
N_EXPERTS = 32
TOP_K = 4
D_FF = 2048
SWIGLU_ALPHA = 1.702
SWIGLU_LIMIT = 7.0
LAMBDA_INIT = 0.8 - 0.6 * math.exp(-0.3 * 0)

COL_DA_Q = 0
COL_DA_K = 1024
COL_DA_V = 2048
COL_M_Q = 3072
COL_M_K = 3584
COL_M_V = 4096
COL_M_O = 5120
COL_M_G = 6144
PROJ_COLS = 6144

N_TOK = BATCH * SEQ
N_ASSIGN = N_TOK * TOP_K

LANES = 128
VMEM_LIMIT = 52 * 1024 * 1024
MOE_VMEM_LIMIT = 58 * 1024 * 1024


def _cparams(n_axes):
    return pltpu.CompilerParams(
        dimension_semantics=("arbitrary",) * n_axes, vmem_limit_bytes=VMEM_LIMIT)


def _rms(v, gain):
    ms = jnp.mean(v * v, axis=-1, keepdims=True)
    return v * lax.rsqrt(ms + EPS) * gain


def _norm_in_kernel(x_ref, meta_ref, g_ref, pad_ref, real_ref):
    j = pl.program_id(1)
    g = g_ref[...]

    @pl.when(j == 0)
    def _():
        pad_ref[0, :PAD, :] = jnp.zeros((PAD, D_MODEL), BF16)
        pad_ref[0, PAD:, :] = _rms(meta_ref[...], g).astype(BF16)

    @pl.when(j > 0)
    def _():
        y = _rms(x_ref[0], g).astype(BF16)
        pad_ref[0] = y
        real_ref[0] = y


def norm_in(x, meta, gain):
    return pl.pallas_call(
        _norm_in_kernel,
        grid=(BATCH, NBLK),
        in_specs=[
            pl.BlockSpec((1, BLOCK, D_MODEL), lambda b, j: (b, jnp.maximum(j - 1, 0), 0)),
            pl.BlockSpec((N_META, D_MODEL), lambda b, j: (0, 0)),
            pl.BlockSpec((1, D_MODEL), lambda b, j: (0, 0)),
        ],
        out_specs=[
            pl.BlockSpec((1, BLOCK, D_MODEL), lambda b, j: (b, j, 0)),
            pl.BlockSpec((1, BLOCK, D_MODEL), lambda b, j: (b, jnp.maximum(j - 1, 0), 0)),
        ],
        out_shape=[
            jax.ShapeDtypeStruct((BATCH, LP, D_MODEL), BF16),
            jax.ShapeDtypeStruct((BATCH, SEQ, D_MODEL), BF16),
        ],
        compiler_params=_cparams(2),
        name="norm_in",
    )(x, meta, gain.reshape(1, D_MODEL))


def _mm_kernel(*refs, has_bias, has_res, act, w_transposed):
    x_ref, w_ref = refs[0], refs[1]
    pos = 2
    b_ref = r_ref = None
    if has_bias:
        b_ref = refs[pos]
        pos += 1
    if has_res:
        r_ref = refs[pos]
        pos += 1
    o_ref, wbf_ref = refs[pos], refs[pos + 1]

    @pl.when(pl.program_id(1) == 0)
    def _():
        w = w_ref[...]
        wbf_ref[...] = (w.T if w_transposed else w).astype(BF16)

    acc = jnp.dot(x_ref[...], wbf_ref[...], preferred_element_type=F32)
    if has_bias:
        acc = acc + b_ref[...]
    if act == "sigmoid":
        acc = jax.nn.sigmoid(acc)
    if has_res:
        acc = acc + r_ref[...]
    o_ref[...] = acc.astype(o_ref.dtype)


def matmul(x, w, *, n_cols, col_block0=0, tm, tn, bias=None, res=None, act=None,
           out_dtype=BF16, w_transposed=False, name):
    m, k = x.shape
    in_specs = [
        pl.BlockSpec((tm, k), lambda j, i: (i, 0)),
        pl.BlockSpec((tn, k), lambda j, i: (j + col_block0, 0)) if w_transposed
        else pl.BlockSpec((k, tn), lambda j, i: (0, j + col_block0)),
    ]
    args = [x, w]
    if bias is not None:
        in_specs.append(pl.BlockSpec((1, tn), lambda j, i: (0, j)))
        args.append(bias.reshape(1, n_cols))
    if res is not None:
        in_specs.append(pl.BlockSpec((tm, tn), lambda j, i: (i, j)))
        args.append(res)
    return pl.pallas_call(
        functools.partial(_mm_kernel, has_bias=bias is not None, has_res=res is not None, act=act,
                          w_transposed=w_transposed),
        grid=(n_cols // tn, m // tm),
        in_specs=in_specs,
        out_specs=pl.BlockSpec((tm, tn), lambda j, i: (i, j)),
        out_shape=jax.ShapeDtypeStruct((m, n_cols), out_dtype),
        scratch_shapes=[pltpu.VMEM((k, tn), BF16)],
        compiler_params=_cparams(2),
        name=name,
    )(*args)


def _conv_kernel(p_ref, w_ref, o_ref):
    c = pl.program_id(1)
    x = p_ref[0].astype(F32)
    w = w_ref[...]
    half = CONV_W // 2
    acc = w[half:half + 1, :] * x
    for j in range(CONV_W):
        if j != half:
            acc = acc + w[j:j + 1, :] * pltpu.roll(x, (half - j) % LP, axis=0)
    y = acc * jax.nn.sigmoid(acc)
    rows = lax.broadcasted_iota(jnp.int32, (LP, 1), 0)
    y = jnp.where(rows >= PAD, y, 0.0)
    scale = jnp.where(c < 2, DK_M ** -0.5, 1.0).astype(F32)
    o_ref[0] = (y * scale).astype(BF16)


def conv_qk(proj3, conv_w):
    cw = 256
    return pl.pallas_call(
        _conv_kernel,
        grid=(BATCH, (2 * H_M * DK_M) // cw),
        in_specs=[
            pl.BlockSpec((1, LP, cw), lambda b, c: (b, 0, COL_M_Q // cw + c)),
            pl.BlockSpec((CONV_W, cw), lambda b, c: (0, c)),
        ],
        out_specs=pl.BlockSpec((1, LP, cw), lambda b, c: (b, 0, c)),
        out_shape=jax.ShapeDtypeStruct((BATCH, LP, 2 * H_M * DK_M), BF16),
        compiler_params=_cparams(2),
        name="conv_qk",
    )(proj3, conv_w)


def _split_dot(tri, v):
    hi = v.astype(BF16)
    r1 = v - hi.astype(F32)
    mid = r1.astype(BF16)
    lo = (r1 - mid.astype(F32)).astype(BF16)
    return (jnp.dot(tri, hi, preferred_element_type=F32)
            + jnp.dot(tri, mid, preferred_element_type=F32)
            + jnp.dot(tri, lo, preferred_element_type=F32))


def _gate_kernel(g_ref, bias_ref, o_ref):
    ti = lax.broadcasted_iota(jnp.int32, (BLOCK, BLOCK), 0)
    ui = lax.broadcasted_iota(jnp.int32, (BLOCK, BLOCK), 1)
    tril = jnp.where(ui <= ti, 1.0, 0.0).astype(BF16)
    triu = jnp.where(ui >= ti, 1.0, 0.0).astype(BF16)
    ch = lax.broadcasted_iota(jnp.int32, (BLOCK, 4 * H_M), 1)
    typ = lax.shift_right_logical(ch, 2)
    rloc = lax.broadcasted_iota(jnp.int32, (BLOCK, 4 * H_M), 0)

    def body(c, carry):
        r0 = pl.multiple_of(c * BLOCK, BLOCK)
        g = g_ref[0, pl.ds(r0, BLOCK), :][:, :4 * H_M] + bias_ref[...]
        valid = (rloc + r0) >= PAD
        lsig = -(jnp.maximum(-g, 0.0) + jnp.log1p(jnp.exp(-jnp.abs(g))))
        lf = jnp.where(valid, lsig, 0.0)
        cum = _split_dot(tril, lf)
        rcum = _split_dot(triu, lf)
        li = jnp.where(valid, g, -jnp.inf)
        out = jnp.where(typ == 1, cum, jnp.where(typ == 3, rcum, li))
        o_ref[0, pl.ds(r0, BLOCK), :] = out
        return carry

    lax.fori_loop(0, NBLK, body, 0)


def gate_prep(mg3, gate_bias):
    return pl.pallas_call(
        _gate_kernel,
        grid=(BATCH,),
        in_specs=[
            pl.BlockSpec((1, LP, LANES), lambda b: (b, 0, 0)),
            pl.BlockSpec((1, 4 * H_M), lambda b: (0, 0)),
        ],
        out_specs=pl.BlockSpec((1, LP, 4 * H_M), lambda b: (b, 0, 0)),
        out_shape=jax.ShapeDtypeStruct((BATCH, LP, 4 * H_M), F32),
        compiler_params=_cparams(1),
        name="gate_prep",
    )(mg3, gate_bias.reshape(1, 4 * H_M))


MLSTM_HP = 2
MLSTM_MID = NBLK // 2


def _mlstm_kernel(qt_ref, k_ref, vt_ref, ot_ref, gr_ref, y_ref,
                  hs_ref, c_ref, n_ref, m_ref):
    c_ref[...] = jnp.zeros_like(c_ref)
    n_ref[...] = jnp.zeros_like(n_ref)
    m_ref[...] = jnp.zeros_like(m_ref)
    si = lax.broadcasted_iota(jnp.int32, (BLOCK, BLOCK), 0)
    ti = lax.broadcasted_iota(jnp.int32, (BLOCK, BLOCK), 1)
    mask_f = si <= ti
    mask_b = si >= ti

    def chain(c, hl, bwd, final):
        idx = 2 * hl + bwd
        r0 = c * BLOCK if isinstance(c, int) else pl.multiple_of(c * BLOCK, BLOCK)
        t_sl = pl.ds(r0, BLOCK)
        qt = qt_ref[0, hl * DK_M:(hl + 1) * DK_M, t_sl]
        k = k_ref[0, t_sl, hl * DK_M:(hl + 1) * DK_M]
        feat = slice(hl * DV_M, (hl + 1) * DV_M)
        vt = vt_ref[0, feat, t_sl]
        gr = gr_ref[0, hl, :, t_sl]
        li_r, b_r = gr[2 * bwd:2 * bwd + 1, :], gr[2 * bwd + 1:2 * bwd + 2, :]
        a_c = jnp.transpose(jnp.broadcast_to(li_r - b_r, (BLOCK, BLOCK)))
        b_end = b_r[:, 0:1] if bwd else b_r[:, BLOCK - 1:BLOCK]
        m_prev = m_ref[idx][:, 0:1]
        ct = c_ref[idx]
        nst = n_ref[idx]
        dmat = jnp.where(mask_b if bwd else mask_f, a_c + b_r, -jnp.inf)
        inter = b_r + m_prev
        m_t = jnp.maximum(inter, jnp.max(dmat, axis=0, keepdims=True))
        w_inter = jnp.exp(inter - m_t)
        st = jnp.dot(k, qt, preferred_element_type=F32) * jnp.exp(dmat - m_t)
        num = (w_inter * jnp.dot(ct.astype(BF16), qt, preferred_element_type=F32)
               + jnp.dot(vt, st.astype(BF16), preferred_element_type=F32))
        nq = jnp.dot(nst.astype(BF16), qt, preferred_element_type=F32)
        den = w_inter * nq + jnp.sum(st, axis=0, keepdims=True)
        h = num * (1.0 / jnp.maximum(jnp.abs(den), jnp.exp(-m_t)))
        if final:
            og = jax.nn.sigmoid(ot_ref[0, feat, t_sl].astype(F32))
            y_ref[0, feat, pl.ds(r0 - BLOCK, BLOCK)] = (og * (hs_ref[feat, t_sl] + h)).astype(BF16)
        else:
            hs_ref[feat, t_sl] = h
        ldec = b_end - b_r + li_r
        m_new = jnp.maximum(b_end + m_prev, jnp.max(ldec, axis=1, keepdims=True))
        w_c = jnp.exp(b_end + m_prev - m_new)
        w_s = jnp.exp(ldec - m_new)
        wvt = (vt.astype(F32) * w_s).astype(BF16)
        c_ref[idx] = w_c * ct + jnp.dot(wvt, k, preferred_element_type=F32)
        n_ref[idx] = w_c * nst + jnp.dot(w_s.astype(BF16), k, preferred_element_type=F32)
        m_ref[idx] = jnp.broadcast_to(m_new, (1, BLOCK))

    def first_half(i, carry):
        for hl in range(MLSTM_HP):
            chain(i, hl, 0, False)
            chain(NBLK - 1 - i, hl, 1, False)
        return carry

    def second_half(i, carry):
        for hl in range(MLSTM_HP):
            chain(i, hl, 0, True)
            chain(NBLK - 1 - i, hl, 1, True)
        return carry

    lax.fori_loop(0, MLSTM_MID, first_half, 0, unroll=8)
    for hl in range(MLSTM_HP):
        chain(MLSTM_MID, hl, 0, False)
        chain(MLSTM_MID, hl, 1, True)
    lax.fori_loop(MLSTM_MID + 1, NBLK - 1, second_half, 0, unroll=5)
    for hl in range(MLSTM_HP):
        chain(NBLK - 1, hl, 0, True)


def mlstm(q_t, qk_m, v_t, o_t, grow):
    hp = MLSTM_HP
    kw, vw = hp * DK_M, hp * DV_M
    return pl.pallas_call(
        _mlstm_kernel,
        grid=(BATCH, H_M // hp),
        in_specs=[
            pl.BlockSpec((1, kw, LP), lambda b, g: (b, g, 0)),
            pl.BlockSpec((1, LP, kw), lambda b, g: (b, 0, (H_M * DK_M) // kw + g)),
            pl.BlockSpec((1, vw, LP), lambda b, g: (b, g, 0)),
            pl.BlockSpec((1, vw, LP), lambda b, g: (b, g, 0)),
            pl.BlockSpec((1, hp, 4, LP), lambda b, g: (b, g, 0, 0)),
        ],
        out_specs=pl.BlockSpec((1, vw, SEQ), lambda b, g: (b, g, 0)),
        out_shape=jax.ShapeDtypeStruct((BATCH, H_M * DV_M, SEQ), BF16),
        scratch_shapes=[
            pltpu.VMEM((vw, LP), F32),
            pltpu.VMEM((2 * hp, DV_M, DK_M), F32),
            pltpu.VMEM((2 * hp, 1, DK_M), F32),
            pltpu.VMEM((2 * hp, 1, BLOCK), F32),
        ],
        compiler_params=_cparams(2),
        name="mlstm",
    )(q_t, qk_m, v_t, o_t, grow)


LOG2E = 1.4426950408889634
ATT_QB = 2
ATT_TQ = ATT_QB * BLOCK
ATT_BAND = (ATT_QB + 2) * BLOCK
ATT_GROUPS = (6, 6, 6, 6, 5)
assert BLOCK >= MAX_DISTANCE and ATT_BAND + sum(ATT_GROUPS) * BLOCK == LP


ATT_NQ = SEQ // ATT_TQ


def _attn_kernel(c_ref, *refs):
    q_refs = refs[:ATT_QB]
    (k1_ref, v1_ref, tab_ref, lam_ref, sg_ref, o_ref,
     s_a, s_b, mx_a, mx_b, k_ref, v_ref) = refs[ATT_QB:]
    h = pl.program_id(1)
    t = pl.program_id(2)

    @pl.when(t == 0)
    def _():
        for rep in range(2):
            k_ref[0, rep * LP:(rep + 1) * LP, :] = k1_ref[0]
            v_ref[0, rep * LP:(rep + 1) * LP, :] = v1_ref[0]

    scale = DK_DA ** -0.5 * LOG2E
    c_neg = c_ref[h, 0]
    c_pos = c_ref[h, 1]

    def groups_of(tile):
        koff = (ATT_QB * tile) * BLOCK
        out = [(koff, ATT_BAND, 0)]
        col = ATT_BAND
        for nblk in ATT_GROUPS:
            out.append((koff + col, nblk * BLOCK, col))
            col += nblk * BLOCK
        return out

    def lane_fold(acc, x, op):
        for j in range(x.shape[1] // BLOCK):
            piece = x[:, j * BLOCK:(j + 1) * BLOCK]
            acc = piece if acc is None else op(acc, piece)
        return acc

    def score(tile, s_ref, mx_ref):
        q = jnp.concatenate([r[0] for r in q_refs], axis=0)
        mx = [None, None]
        for gi, (koff, width, col0) in enumerate(groups_of(tile)):
            koff = pl.multiple_of(koff, BLOCK)
            if gi == 0:
                bias = tab_ref[0, 0]
            else:
                kpos = koff + lax.broadcasted_iota(jnp.int32, (1, width), 1)
                bias = jnp.where(kpos < LP, c_pos, jnp.where(kpos < LP + PAD, NEG_INF, c_neg))
            for m in range(2):
                kk = k_ref[0, pl.ds(koff, width), m * DK_DA:(m + 1) * DK_DA]
                s = lax.dot_general(q[:, m * DK_DA:(m + 1) * DK_DA], kk,
                                    (((1,), (1,)), ((), ())),
                                    preferred_element_type=F32) * scale + bias
                s_ref[m, :, col0:col0 + width] = s
                mx[m] = lane_fold(mx[m], s, jnp.maximum)
        for m in range(2):
            mx_ref[m] = mx[m]

    def finish(tile, s_ref, mx_ref):
        lp = lam_ref[...]
        lam = (jnp.exp(jnp.sum(lp[0:1] * lp[1:2], axis=1, keepdims=True))
               - jnp.exp(jnp.sum(lp[2:3] * lp[3:4], axis=1, keepdims=True)) + LAMBDA_INIT)
        row_max = [jnp.max(mx_ref[m], axis=1, keepdims=True) for m in range(2)]
        lsum = [None, None]
        acc = [None, None]
        for koff, width, col0 in groups_of(tile):
            koff = pl.multiple_of(koff, BLOCK)
            vv = v_ref[0, pl.ds(koff, width), :]
            for m in range(2):
                p = jnp.exp2(s_ref[m, :, col0:col0 + width] - row_max[m])
                lsum[m] = lane_fold(lsum[m], p, jnp.add)
                pv = jnp.dot(p.astype(BF16), vv, preferred_element_type=F32)
                acc[m] = pv if acc[m] is None else acc[m] + pv
        l1 = jnp.sum(lsum[0], axis=1, keepdims=True)
        l2 = jnp.sum(lsum[1], axis=1, keepdims=True)
        o = acc[0] / l1 - lam * (acc[1] / l2)
        o_ref[0] = (_rms(o, sg_ref[...]) * (1.0 - LAMBDA_INIT)).astype(BF16)

    even = lax.rem(t, 2) == 0
    inner = jnp.logical_and(t > 0, t < ATT_NQ)

    @pl.when(t == 0)
    def _():
        score(t, s_a, mx_a)

    @pl.when(jnp.logical_and(inner, even))
    def _():
        score(t, s_a, mx_a)
        finish(t - 1, s_b, mx_b)

    @pl.when(jnp.logical_and(inner, jnp.logical_not(even)))
    def _():
        score(t, s_b, mx_b)
        finish(t - 1, s_a, mx_a)

    @pl.when(t == ATT_NQ)
    def _():
        if (ATT_NQ - 1) % 2 == 0:
            finish(t - 1, s_a, mx_a)
        else:
            finish(t - 1, s_b, mx_b)


def diff_attention(consts, proj3, tab, lam_params, subln):
    nq = ATT_NQ
    kblk0 = COL_DA_K // (2 * DK_DA)
    vblk0 = COL_DA_V // DV_DA
    scored = lambda i: jnp.minimum(i, nq - 1)
    finished = lambda i: jnp.maximum(i - 1, 0)

    def tab_map(b, h, i):
        tile = scored(i)
        case = jnp.where(tile == 0, 0, jnp.where(tile == nq - 1, 2, 1))
        return (h, case, 0, 0)

    return pl.pallas_call(
        _attn_kernel,
        grid=(BATCH, H_DA, nq + 1),
        in_specs=[
            pl.BlockSpec(memory_space=pltpu.SMEM),
            *[pl.BlockSpec((1, BLOCK, 2 * DK_DA),
                           functools.partial(
                               lambda b, h, i, r: (b, ATT_QB * scored(i) + 1 + r, h), r=r))
              for r in range(ATT_QB)],
            pl.BlockSpec((1, LP, 2 * DK_DA), lambda b, h, i: (b, 0, kblk0 + h)),
            pl.BlockSpec((1, LP, DV_DA), lambda b, h, i: (b, 0, vblk0 + h)),
            pl.BlockSpec((1, 1, ATT_TQ, ATT_BAND), tab_map),
            pl.BlockSpec((4, DK_DA), lambda b, h, i: (0, 0)),
            pl.BlockSpec((1, DV_DA), lambda b, h, i: (0, 0)),
        ],
        out_specs=pl.BlockSpec((1, ATT_TQ, DV_DA), lambda b, h, i: (b, finished(i), h)),
        out_shape=jax.ShapeDtypeStruct((BATCH, SEQ, H_DA * DV_DA), BF16),
        scratch_shapes=[pltpu.VMEM((2, ATT_TQ, LP), F32),
                        pltpu.VMEM((2, ATT_TQ, LP), F32),
                        pltpu.VMEM((2, ATT_TQ, BLOCK), F32),
                        pltpu.VMEM((2, ATT_TQ, BLOCK), F32),
                        pltpu.VMEM((1, 2 * LP, 2 * DK_DA), BF16),
                        pltpu.VMEM((1, 2 * LP, DV_DA), BF16)],
        compiler_params=_cparams(3),
        name="diff_attn",
    )(consts, *([proj3] * (ATT_QB + 2)), tab, lam_params, subln.reshape(1, DV_DA))


def _bias_tables(rel_bias):
    rb = rel_bias.astype(F32)
    span = 1024
    assert span >= ATT_TQ + ATT_BAND
    rel = jnp.arange(span, dtype=jnp.int32) - span // 2
    nb = N_BUCKETS // 2
    max_exact = nb // 2
    n = jnp.abs(rel)
    nf = jnp.maximum(n, 1).astype(F32)
    large = max_exact + (jnp.log(nf / max_exact) / math.log(MAX_DISTANCE / max_exact)
                         * (nb - max_exact)).astype(jnp.int32)
    large = jnp.minimum(large, nb - 1)
    bucket = jnp.where(rel > 0, nb, 0) + jnp.where(n < max_exact, n, large)
    hit = bucket[None, :, None] == jnp.arange(N_BUCKETS, dtype=jnp.int32)
    by_rel = jnp.sum(jnp.where(hit, rb.T[:, None, :], 0.0), axis=-1)
    shifted = jnp.tile(by_rel, (1, ATT_TQ))[:, :ATT_TQ * (span - 1)].reshape(
        H_DA, ATT_TQ, span - 1)
    c0 = span // 2 - BLOCK
    gen = shifted[:, :, c0:c0 + ATT_BAND]
    c_neg = rb[nb - 1]
    c_pos = rb[N_BUCKETS - 1]
    jj = jnp.arange(ATT_BAND, dtype=jnp.int32)[None, None, :]
    first = jnp.where(jj < PAD, NEG_INF, gen)
    wrap0 = ATT_BAND - BLOCK
    wrapped = jnp.where(jj - wrap0 < PAD, NEG_INF, c_neg[:, None, None])
    last = jnp.where(jj >= wrap0, wrapped, gen)
    tab = jnp.stack([first, gen, last], axis=1)
    consts = jnp.stack([c_neg, c_pos], axis=1)
    return tab * LOG2E, consts * LOG2E


def _mix_kernel(ya_ref, ym_ref, wa_ref, wm_ref, ga_ref, gm_ref, o_ref, wa_bf, wm_bf):
    @pl.when(pl.program_id(1) == 0)
    def _():
        wa_bf[...] = wa_ref[...].astype(BF16)
        wm_bf[...] = wm_ref[...].astype(BF16)

    a = jnp.dot(ya_ref[...], wa_bf[...], preferred_element_type=F32)
    m = jnp.dot(ym_ref[...], wm_bf[...], preferred_element_type=F32)
    o_ref[...] = (ga_ref[...].astype(F32) * a + gm_ref[...].astype(F32) * m).astype(BF16)


def branch_mix(y_da, y_m, w_da, w_m, gate, *, tm=512, tn=1024):
    m, k = y_da.shape
    nj = D_MODEL // tn
    return pl.pallas_call(
        _mix_kernel,
        grid=(nj, m // tm),
        in_specs=[
            pl.BlockSpec((tm, k), lambda j, i: (i, 0)),
            pl.BlockSpec((tm, k), lambda j, i: (i, 0)),
            pl.BlockSpec((k, tn), lambda j, i: (0, j)),
            pl.BlockSpec((k, tn), lambda j, i: (0, j)),
            pl.BlockSpec((tm, tn), lambda j, i: (i, j)),
            pl.BlockSpec((tm, tn), lambda j, i: (i, nj + j)),
        ],
        out_specs=pl.BlockSpec((tm, tn), lambda j, i: (i, j)),
        out_shape=jax.ShapeDtypeStruct((m, D_MODEL), BF16),
        scratch_shapes=[pltpu.VMEM((k, tn), BF16), pltpu.VMEM((k, tn), BF16)],
        compiler_params=_cparams(2),
        name="branch_mix",
    )(y_da, y_m, w_da, w_m, gate, gate)


FFN_TM = 1024
HALF = D_MODEL // 2


def _pack_bf16_pairs(v):
    lo = lax.bitcast_convert_type(v[:, :HALF].astype(BF16).astype(F32), jnp.uint32)
    hi = lax.bitcast_convert_type(v[:, HALF:].astype(BF16).astype(F32), jnp.uint32)
    return (hi & jnp.uint32(0xFFFF0000)) | lax.shift_right_logical(lo, jnp.uint32(16))


def _unpack_bf16_pairs(w):
    lo = lax.bitcast_convert_type(lax.shift_left(w, jnp.uint32(16)), F32).astype(BF16)
    hi = lax.bitcast_convert_type(w & jnp.uint32(0xFFFF0000), F32).astype(BF16)
    return lo, hi


def _ffn_prep_kernel(h_ref, g_ref, wr_ref, br_ref, hn_ref, e_ref, w_ref, r_ref, cnt_ref, base_ref):
    @pl.when(pl.program_id(0) == 0)
    def _():
        base_ref[...] = jnp.zeros_like(base_ref)

    hn = _rms(h_ref[...], g_ref[...])
    hn_ref[...] = _pack_bf16_pairs(hn)
    logits = jnp.dot(hn.astype(BF16), wr_ref[...].astype(BF16),
                     preferred_element_type=F32) + br_ref[...]
    lane = lax.broadcasted_iota(jnp.int32, (FFN_TM, N_EXPERTS), 1)
    lane_o = lax.broadcasted_iota(jnp.int32, (FFN_TM, LANES), 1)
    ti = lax.broadcasted_iota(jnp.int32, (FFN_TM, FFN_TM), 0)
    ui = lax.broadcasted_iota(jnp.int32, (FFN_TM, FFN_TM), 1)
    tril = jnp.where(ui <= ti, 1.0, 0.0).astype(BF16)
    e_out = jnp.zeros((FFN_TM, LANES), jnp.int32)
    r_out = jnp.zeros((FFN_TM, LANES), jnp.int32)
    l_out = jnp.full((FFN_TM, LANES), -jnp.inf, F32)
    base = base_ref[...]
    l = logits
    for kk in range(TOP_K):
        mk = jnp.max(l, axis=1, keepdims=True)
        ik = jnp.min(jnp.where(l == mk, lane, N_EXPERTS), axis=1, keepdims=True)
        hit = lane == ik
        oh = jnp.where(hit, 1.0, 0.0)
        cum = jnp.dot(tril, oh.astype(BF16), preferred_element_type=F32)
        rank = jnp.sum(oh * (cum + base), axis=1, keepdims=True) - 1.0
        base = base + jnp.sum(oh, axis=0, keepdims=True)
        e_out = jnp.where(lane_o == kk, ik, e_out)
        r_out = jnp.where(lane_o == kk, rank.astype(jnp.int32), r_out)
        l_out = jnp.where(lane_o == kk, mk, l_out)
        l = jnp.where(hit, -jnp.inf, l)
    base_ref[...] = base
    cnt_ref[...] = base
    ex = jnp.exp(l_out - jnp.max(l_out, axis=1, keepdims=True))
    e_ref[...] = e_out
    r_ref[...] = r_out
    w_ref[...] = ex / jnp.sum(ex, axis=1, keepdims=True)


def ffn_prep(h2, gain, w_router, b_router):
    row = lambda i: (i, 0)
    fixed = lambda i: (0, 0)
    return pl.pallas_call(
        _ffn_prep_kernel,
        grid=(N_TOK // FFN_TM,),
        in_specs=[
            pl.BlockSpec((FFN_TM, D_MODEL), row),
            pl.BlockSpec((1, D_MODEL), fixed),
            pl.BlockSpec((D_MODEL, N_EXPERTS), fixed),
            pl.BlockSpec((1, N_EXPERTS), fixed),
        ],
        out_specs=[
            pl.BlockSpec((FFN_TM, HALF), row),
            pl.BlockSpec((FFN_TM, LANES), row),
            pl.BlockSpec((FFN_TM, LANES), row),
            pl.BlockSpec((FFN_TM, LANES), row),
            pl.BlockSpec((1, N_EXPERTS), fixed),
        ],
        out_shape=[
            jax.ShapeDtypeStruct((N_TOK, HALF), jnp.uint32),
            jax.ShapeDtypeStruct((N_TOK, LANES), jnp.int32),
            jax.ShapeDtypeStruct((N_TOK, LANES), F32),
            jax.ShapeDtypeStruct((N_TOK, LANES), jnp.int32),
            jax.ShapeDtypeStruct((1, N_EXPERTS), F32),
        ],
        scratch_shapes=[pltpu.VMEM((1, N_EXPERTS), F32)],
        compiler_params=_cparams(1),
        name="ffn_prep",
    )(h2, gain.reshape(1, D_MODEL), w_router, b_router.reshape(1, N_EXPERTS))


SEG_ALIGN = 128
MOE_R = N_ASSIGN + N_EXPERTS * SEG_ALIGN


def _plan(counts_f, top_e, rank):
    counts = counts_f[0].astype(jnp.int32)
    seg_rows = (counts + SEG_ALIGN - 1) // SEG_ALIGN * SEG_ALIGN
    seg_start = jnp.cumsum(seg_rows) - seg_rows
    eq = top_e[:, :, None] == jnp.arange(N_EXPERTS, dtype=jnp.int32)[None, None, :]
    dest = jnp.sum(jnp.where(eq, seg_start[None, None, :], 0), axis=-1) + rank
    used = jnp.sum(seg_rows)
    slack = jnp.stack([used, (MOE_R - used) // SEG_ALIGN])
    return (dest.reshape(N_ASSIGN).astype(jnp.int32), seg_start.astype(jnp.int32),
            seg_rows.astype(jnp.int32), (seg_start + counts).astype(jnp.int32),
            (seg_rows - counts).astype(jnp.int32), slack.astype(jnp.int32))


def _zero_slack(slack_ref, zero_block, dst_rows, sem):
    zero_block[...] = jnp.zeros_like(zero_block)

    def copy(j):
        r0 = pl.multiple_of(slack_ref[0] + j * SEG_ALIGN, SEG_ALIGN)
        return pltpu.make_async_copy(zero_block, dst_rows(pl.ds(r0, SEG_ALIGN)), sem)

    def start(j, carry):
        copy(j).start()
        return carry

    def wait(j, carry):
        copy(j).wait()
        return carry

    lax.fori_loop(0, slack_ref[1], start, 0)
    lax.fori_loop(0, slack_ref[1], wait, 0)


DISP_TOK = 512


def _dispatch_kernel(dest_ref, pad0_ref, padn_ref, slack_ref, hn_ref, xs_ref,
                     zrow_ref, zblk_ref, sem, zsem):
    i = pl.program_id(0)

    @pl.when(i == 0)
    def _():
        zrow_ref[...] = jnp.zeros_like(zrow_ref)
        _zero_slack(slack_ref, zblk_ref, lambda rows: xs_ref.at[rows], zsem.at[0])

        def expert(e, carry):
            p0 = pad0_ref[e]
            pn = padn_ref[e]

            def zstart(r, c2):
                pltpu.make_async_copy(zrow_ref, xs_ref.at[pl.ds(p0 + r, 1)], zsem.at[0]).start()
                return c2

            def zwait(r, c2):
                pltpu.make_async_copy(zrow_ref, xs_ref.at[pl.ds(p0, 1)], zsem.at[0]).wait()
                return c2

            lax.fori_loop(0, pn, zstart, 0)
            lax.fori_loop(0, pn, zwait, 0)
            return carry

        lax.fori_loop(0, N_EXPERTS, expert, 0)

    def tok(t, carry):
        a = (i * DISP_TOK + t) * TOP_K
        for kk in range(TOP_K):
            pltpu.make_async_copy(hn_ref.at[pl.ds(t, 1)], xs_ref.at[pl.ds(dest_ref[a + kk], 1)],
                                  sem.at[0]).start(priority=kk % 2)
        return carry

    lax.fori_loop(0, DISP_TOK, tok, 0, unroll=4)
    for kk in range(TOP_K):
        pltpu.make_async_copy(hn_ref, xs_ref.at[pl.ds(0, DISP_TOK)], sem.at[0]).wait()


def dispatch(dest, pad0, padn, slack, hn_packed):
    grid_spec = pltpu.PrefetchScalarGridSpec(
        num_scalar_prefetch=4,
        grid=(N_TOK // DISP_TOK,),
        in_specs=[pl.BlockSpec((DISP_TOK, HALF), lambda i, d, p0, pn, z: (i, 0))],
        out_specs=pl.BlockSpec(memory_space=pl.ANY),
        scratch_shapes=[pltpu.VMEM((1, HALF), jnp.uint32),
                        pltpu.VMEM((SEG_ALIGN, HALF), jnp.uint32),
                        pltpu.SemaphoreType.DMA((1,)), pltpu.SemaphoreType.DMA((1,))],
    )
    return pl.pallas_call(
        _dispatch_kernel,
        grid_spec=grid_spec,
        out_shape=jax.ShapeDtypeStruct((MOE_R, HALF), jnp.uint32),
        compiler_params=_cparams(1),
        name="moe_dispatch",
    )(dest, pad0, padn, slack, hn_packed)


MOE_CH = 2 * SEG_ALIGN
MOE_TF = 1024
MOE_NF = D_FF // MOE_TF


class _CopyGroup:
    def __init__(self, copies):
        self.copies = copies

    def start(self, priority=0):
        for cp in self.copies:
            cp.start(priority=priority)

    def wait(self):
        for cp in self.copies:
            cp.wait()


W_PIECES = 8
W_PER_CHUNK = 3


N_STATE = 4


def _stream_rows(step, n_steps, start, rows, next_start, next_rows, state,
                 make_in, make_out, make_tail_in, make_tail_out,
                 compute_chunk, compute_tail, before_first_wait, next_weight_piece):
    has_next = step + 1 < n_steps

    def request_weights(first, count):
        def one(p, carry):
            @pl.when(jnp.logical_and(has_next, p < W_PIECES))
            def _():
                next_weight_piece(p).start()
            return carry

        lax.fori_loop(first, first + count, one, 0)

    @pl.when(step == 0)
    def _():
        for j in range(N_STATE):
            state[j] = 0

    n_ch = lax.shift_right_logical(rows, MOE_CH.bit_length() - 1)
    tail = rows - n_ch * MOE_CH
    tail_row = start + n_ch * MOE_CH
    g0 = state[0]
    feeds_next = jnp.logical_and(step + 1 < n_steps, next_rows >= MOE_CH)

    def chunk_row(c):
        return start + c * MOE_CH

    @pl.when(tail > 0)
    def _():
        make_tail_in(tail_row).start(priority=1)

    @pl.when(jnp.logical_and(n_ch > 0, state[3] == 0))
    def _():
        make_in(start, lax.rem(g0, 2)).start(priority=1)

    before_first_wait()

    @pl.when(tail > 0)
    def _():
        make_tail_in(tail_row).wait()
        compute_tail()
        make_tail_out(tail_row).start(priority=1)

    def body(c, carry):
        slot = lax.rem(g0 + c, 2)
        make_in(chunk_row(c), slot).wait()

        @pl.when(c + 1 < n_ch)
        def _():
            make_in(chunk_row(c + 1), 1 - slot).start(priority=1)

        @pl.when(jnp.logical_and(c + 1 == n_ch, feeds_next))
        def _():
            make_in(next_start, 1 - slot).start(priority=1)

        request_weights(c * W_PER_CHUNK, W_PER_CHUNK)

        @pl.when(state[1 + slot] == 1)
        def _():
            make_out(chunk_row(c), slot).wait()

        compute_chunk(slot)
        make_out(chunk_row(c), slot).start(priority=1)
        state[1 + slot] = 1
        return carry

    lax.fori_loop(0, n_ch, body, 0)
    request_weights(n_ch * W_PER_CHUNK, W_PIECES)
    state[0] = g0 + n_ch
    state[3] = jnp.where(jnp.logical_and(n_ch > 0, feeds_next), 1, 0)

    @pl.when(tail > 0)
    def _():
        make_tail_out(tail_row).wait()

    @pl.when(step == n_steps - 1)
    def _():
        for slot in range(2):
            @pl.when(state[1 + slot] == 1)
            def _():
                make_out(0, slot).wait()
                state[1 + slot] = 0


def _rows_at(row0, n):
    return pl.ds(row0 if isinstance(row0, int) else pl.multiple_of(row0, SEG_ALIGN), n)


def _moe_up_kernel(seg_ref, rows_ref, slack_ref, xs_ref, w_hbm, bg_ref, bl_ref, act_ref,
                   xbuf, obuf, xtail, otail, wbuf, wg_bf, wl_bf, state, isem, osem, tsem, wsem):
    f = pl.program_id(0)
    e = pl.program_id(1)
    start = seg_ref[e]
    rows = rows_ref[e]
    e_next = lax.rem(e + 1, N_EXPERTS)
    f_next = jnp.where(e == N_EXPERTS - 1, f + 1, f)
    step = f * N_EXPERTS + e
    n_steps = MOE_NF * N_EXPERTS
    wslot = lax.rem(step, 2)

    def weight_piece(expert, ftile, slot, p):
        per_half = W_PIECES // 2
        band = D_MODEL // per_half
        t = p // per_half if isinstance(p, int) else lax.shift_right_logical(
            p, per_half.bit_length() - 1)
        r0 = (p - t * per_half) * band
        r0 = r0 if isinstance(r0, int) else pl.multiple_of(r0, band)
        col0 = pl.multiple_of((t * MOE_NF + ftile) * MOE_TF, MOE_TF)
        return pltpu.make_async_copy(
            w_hbm.at[expert, pl.ds(r0, band), pl.ds(col0, MOE_TF)],
            wbuf.at[slot, t, pl.ds(r0, band)], wsem.at[slot, p])

    @pl.when(step == 0)
    def _():
        for p in range(W_PIECES):
            weight_piece(e, f, wslot, p).start()

    def make_in(r0, slot):
        return pltpu.make_async_copy(xs_ref.at[_rows_at(r0, MOE_CH)], xbuf.at[slot], isem.at[slot])

    def make_out(r0, slot):
        return pltpu.make_async_copy(obuf.at[slot], act_ref.at[f, _rows_at(r0, MOE_CH)],
                                     osem.at[slot])

    def make_tail_in(r0):
        return pltpu.make_async_copy(xs_ref.at[_rows_at(r0, SEG_ALIGN)], xtail, tsem.at[0])

    def make_tail_out(r0):
        return pltpu.make_async_copy(otail, act_ref.at[f, _rows_at(r0, SEG_ALIGN)], tsem.at[1])

    def cast_weights():
        per_half = W_PIECES // 2
        band = D_MODEL // per_half
        for p in range(W_PIECES):
            weight_piece(e, f, wslot, p).wait()
            t, rows_p = p // per_half, pl.ds((p % per_half) * band, band)
            (wg_bf, wl_bf)[t][rows_p, :] = wbuf[wslot, t, rows_p, :].astype(BF16)

    def expert_mlp(words):
        lo, hi = _unpack_bf16_pairs(words)
        glu = (jnp.dot(lo, wg_bf[:HALF, :], preferred_element_type=F32)
               + jnp.dot(hi, wg_bf[HALF:, :], preferred_element_type=F32) + bg_ref[0])
        lin = (jnp.dot(lo, wl_bf[:HALF, :], preferred_element_type=F32)
               + jnp.dot(hi, wl_bf[HALF:, :], preferred_element_type=F32) + bl_ref[0])
        glu = jnp.minimum(glu, SWIGLU_LIMIT)
        lin = jnp.clip(lin, -SWIGLU_LIMIT, SWIGLU_LIMIT)
        return (glu * jax.nn.sigmoid(SWIGLU_ALPHA * glu) * (lin + 1.0)).astype(BF16)

    def compute_chunk(slot):
        obuf[slot] = expert_mlp(xbuf[slot])

    def compute_tail():
        otail[...] = expert_mlp(xtail[...])

    _stream_rows(step, n_steps, start, rows,
                 seg_ref[e_next], rows_ref[e_next], state,
                 make_in, make_out, make_tail_in, make_tail_out,
                 compute_chunk, compute_tail, cast_weights,
                 lambda p: weight_piece(e_next, f_next, 1 - wslot, p))

    @pl.when(e == N_EXPERTS - 1)
    def _():
        _zero_slack(slack_ref, otail, lambda rr: act_ref.at[f, rr], tsem.at[1])


def moe_up(seg_start, seg_rows, slack, xs, w1, b1):
    grid_spec = pltpu.PrefetchScalarGridSpec(
        num_scalar_prefetch=3,
        grid=(MOE_NF, N_EXPERTS),
        in_specs=[
            pl.BlockSpec(memory_space=pl.ANY),
            pl.BlockSpec(memory_space=pl.ANY),
            pl.BlockSpec((1, 1, MOE_TF), lambda f, e, s, r, z: (e, 0, f)),
            pl.BlockSpec((1, 1, MOE_TF), lambda f, e, s, r, z: (e, 0, MOE_NF + f)),
        ],
        out_specs=pl.BlockSpec(memory_space=pl.ANY),
        scratch_shapes=[
            pltpu.VMEM((2, MOE_CH, HALF), jnp.uint32),
            pltpu.VMEM((2, MOE_CH, MOE_TF), BF16),
            pltpu.VMEM((SEG_ALIGN, HALF), jnp.uint32),
            pltpu.VMEM((SEG_ALIGN, MOE_TF), BF16),
            pltpu.VMEM((2, 2, D_MODEL, MOE_TF), F32),
            pltpu.VMEM((D_MODEL, MOE_TF), BF16),
            pltpu.VMEM((D_MODEL, MOE_TF), BF16),
            pltpu.SMEM((N_STATE,), jnp.int32),
            pltpu.SemaphoreType.DMA((2,)),
            pltpu.SemaphoreType.DMA((2,)),
            pltpu.SemaphoreType.DMA((2,)),
            pltpu.SemaphoreType.DMA((2, W_PIECES)),
        ],
    )
    b13 = b1.reshape(N_EXPERTS, 1, 2 * D_FF)
    return pl.pallas_call(
        _moe_up_kernel,
        grid_spec=grid_spec,
        out_shape=jax.ShapeDtypeStruct((MOE_NF, MOE_R, MOE_TF), BF16),
        compiler_params=pltpu.CompilerParams(
            dimension_semantics=("arbitrary", "arbitrary"), vmem_limit_bytes=MOE_VMEM_LIMIT),
        name="moe_up",
    )(seg_start, seg_rows, slack, xs, w1, b13, b13)


def _moe_down_kernel(seg_ref, rows_ref, slack_ref, act_ref, w_hbm, b_ref, y_ref,
                     xbuf, obuf, xtail, otail, wbuf, w_bf, state, isem, osem, tsem, wsem):
    e = pl.program_id(0)
    start = seg_ref[e]
    rows = rows_ref[e]
    e_next = lax.rem(e + 1, N_EXPERTS)
    wslot = lax.rem(e, 2)

    def weight_piece(expert, slot, p):
        band = D_FF // W_PIECES
        r0 = p * band if isinstance(p, int) else pl.multiple_of(p * band, band)
        return pltpu.make_async_copy(w_hbm.at[expert, pl.ds(r0, band)],
                                     wbuf.at[slot, pl.ds(r0, band)], wsem.at[slot, p])

    @pl.when(e == 0)
    def _():
        for p in range(W_PIECES):
            weight_piece(e, wslot, p).start()

    def make_in(r0, slot):
        return _CopyGroup([
            pltpu.make_async_copy(act_ref.at[j, _rows_at(r0, MOE_CH)],
                                  xbuf.at[slot, :, pl.ds(j * MOE_TF, MOE_TF)], isem.at[slot])
            for j in range(MOE_NF)])

    def make_out(r0, slot):
        return pltpu.make_async_copy(obuf.at[slot], y_ref.at[_rows_at(r0, MOE_CH)], osem.at[slot])

    def make_tail_in(r0):
        return _CopyGroup([
            pltpu.make_async_copy(act_ref.at[j, _rows_at(r0, SEG_ALIGN)],
                                  xtail.at[:, pl.ds(j * MOE_TF, MOE_TF)], tsem.at[0])
            for j in range(MOE_NF)])

    def make_tail_out(r0):
        return pltpu.make_async_copy(otail, y_ref.at[_rows_at(r0, SEG_ALIGN)], tsem.at[1])

    def cast_weights():
        band = D_FF // W_PIECES
        for p in range(W_PIECES):
            weight_piece(e, wslot, p).wait()
            rows_p = pl.ds(p * band, band)
            w_bf[rows_p, :] = wbuf[wslot, rows_p, :].astype(BF16)

    def expert_out(a):
        return _pack_bf16_pairs(jnp.dot(a, w_bf[...], preferred_element_type=F32) + b_ref[0])

    def compute_chunk(slot):
        obuf[slot] = expert_out(xbuf[slot])

    def compute_tail():
        otail[...] = expert_out(xtail[...])

    _stream_rows(e, N_EXPERTS, start, rows, seg_ref[e_next], rows_ref[e_next], state,
                 make_in, make_out, make_tail_in, make_tail_out,
                 compute_chunk, compute_tail, cast_weights,
                 lambda p: weight_piece(e_next, 1 - wslot, p))

    @pl.when(e == N_EXPERTS - 1)
    def _():
        _zero_slack(slack_ref, otail, lambda rr: y_ref.at[rr], tsem.at[1])


def moe_down(seg_start, seg_rows, slack, act, w2, b2):
    grid_spec = pltpu.PrefetchScalarGridSpec(
        num_scalar_prefetch=3,
        grid=(N_EXPERTS,),
        in_specs=[
            pl.BlockSpec(memory_space=pl.ANY),
            pl.BlockSpec(memory_space=pl.ANY),
            pl.BlockSpec((1, 1, D_MODEL), lambda e, s, r, z: (e, 0, 0)),
        ],
        out_specs=pl.BlockSpec(memory_space=pl.ANY),
        scratch_shapes=[
            pltpu.VMEM((2, MOE_CH, D_FF), BF16),
            pltpu.VMEM((2, MOE_CH, HALF), jnp.uint32),
            pltpu.VMEM((SEG_ALIGN, D_FF), BF16),
            pltpu.VMEM((SEG_ALIGN, HALF), jnp.uint32),
            pltpu.VMEM((2, D_FF, D_MODEL), F32),
            pltpu.VMEM((D_FF, D_MODEL), BF16),
            pltpu.SMEM((N_STATE,), jnp.int32),
            pltpu.SemaphoreType.DMA((2,)),
            pltpu.SemaphoreType.DMA((2,)),
            pltpu.SemaphoreType.DMA((2,)),
            pltpu.SemaphoreType.DMA((2, W_PIECES)),
        ],
    )
    return pl.pallas_call(
        _moe_down_kernel,
        grid_spec=grid_spec,
        out_shape=jax.ShapeDtypeStruct((MOE_R, HALF), jnp.uint32),
        compiler_params=pltpu.CompilerParams(
            dimension_semantics=("arbitrary",), vmem_limit_bytes=MOE_VMEM_LIMIT),
        name="moe_down",
    )(seg_start, seg_rows, slack, act, w2, b2.reshape(N_EXPERTS, 1, D_MODEL))


COMB_TM = 256
COMB_NT = N_TOK // COMB_TM


def _combine_kernel(dest_ref, y_ref, w_ref, h_ref, g_ref, o_ref, buf, sem):
    i = pl.program_id(0)

    def fetch(tile, slot):
        def tok(t, carry):
            a = (tile * COMB_TM + t) * TOP_K
            for kk in range(TOP_K):
                pltpu.make_async_copy(y_ref.at[pl.ds(dest_ref[a + kk], 1)],
                                      buf.at[slot, kk, pl.ds(t, 1)], sem.at[slot]).start(
                                          priority=kk % 2)
            return carry

        lax.fori_loop(0, COMB_TM, tok, 0, unroll=4)

    @pl.when(i == 0)
    def _():
        fetch(0, 0)

    slot = lax.rem(i, 2)

    @pl.when(i + 1 < COMB_NT)
    def _():
        fetch(i + 1, 1 - slot)

    for kk in range(TOP_K):
        pltpu.make_async_copy(y_ref.at[pl.ds(0, COMB_TM)], buf.at[slot, kk], sem.at[slot]).wait()
    w = w_ref[...]
    lo = h_ref[:, :HALF]
    hi = h_ref[:, HALF:]
    for kk in range(TOP_K):
        words = buf[slot, kk]
        wk = w[:, kk:kk + 1]
        lo = lo + wk * lax.bitcast_convert_type(lax.shift_left(words, jnp.uint32(16)), F32)
        hi = hi + wk * lax.bitcast_convert_type(words & jnp.uint32(0xFFFF0000), F32)
    ms = (jnp.sum(lo * lo, axis=-1, keepdims=True)
          + jnp.sum(hi * hi, axis=-1, keepdims=True)) * (1.0 / D_MODEL)
    inv = lax.rsqrt(ms + EPS)
    o_ref[:, :HALF] = lo * inv * g_ref[:, :HALF]
    o_ref[:, HALF:] = hi * inv * g_ref[:, HALF:]


def combine(dest, y, weight, h2, gain):
    grid_spec = pltpu.PrefetchScalarGridSpec(
        num_scalar_prefetch=1,
        grid=(COMB_NT,),
        in_specs=[
            pl.BlockSpec(memory_space=pl.ANY),
            pl.BlockSpec((COMB_TM, LANES), lambda i, d: (i, 0)),
            pl.BlockSpec((COMB_TM, D_MODEL), lambda i, d: (i, 0)),
            pl.BlockSpec((1, D_MODEL), lambda i, d: (0, 0)),
        ],
        out_specs=pl.BlockSpec((COMB_TM, D_MODEL), lambda i, d: (i, 0)),
        scratch_shapes=[pltpu.VMEM((2, TOP_K, COMB_TM, HALF), jnp.uint32),
                        pltpu.SemaphoreType.DMA((2,))],
    )
    return pl.pallas_call(
        _combine_kernel,
        grid_spec=grid_spec,
        out_shape=jax.ShapeDtypeStruct((N_TOK, D_MODEL), F32),
        compiler_params=_cparams(1),
        name="moe_combine",
    )(dest, y, weight, h2, gain.reshape(1, D_MODEL))


def kernel(x, meta_tokens, rel_bias, norm_mix, w_in, conv_w, gate_bias_m, lambda_params, subln_da,
           w_branch_da, w_branch_m, w_gate, b_gate, w_out, norm_ffn, w_router, b_router,
           w1, b1, w2, b2, norm_final):
    layer = 0
    xn_pad, xn_real = norm_in(x, meta_tokens, norm_mix[layer])
    xn_pad2 = xn_pad.reshape(BATCH * LP, D_MODEL)
    w_in_t = w_in[layer].T
    proj = matmul(xn_pad2, w_in_t, n_cols=PROJ_COLS, tm=768, tn=1024, w_transposed=True,
                  name="proj_in")
    proj3 = proj.reshape(BATCH, LP, PROJ_COLS)
    w_g_t = jnp.pad(w_in_t[COL_M_G:], ((0, LANES - 4 * H_M), (0, 0)))
    mg = matmul(xn_pad2, w_g_t, n_cols=LANES, tm=768, tn=LANES, out_dtype=F32,
                w_transposed=True, name="proj_gates")
    gate = matmul(xn_real.reshape(N_TOK, D_MODEL), w_gate[layer], n_cols=2 * D_MODEL,
                  tm=1024, tn=1024, bias=b_gate[layer], act="sigmoid", name="mix_gate")

    tab, consts = _bias_tables(rel_bias)
    y_da = diff_attention(consts, proj3, tab, lambda_params[layer], subln_da[layer])

    qk_m = conv_qk(proj3, conv_w[layer])
    gp = gate_prep(mg.reshape(BATCH, LP, LANES), gate_bias_m[layer])
    gp4 = gp.reshape(BATCH, LP, 4, H_M)
    grow = jnp.transpose(gp4, (0, 3, 2, 1))
    swap = lambda t: jnp.swapaxes(t, 1, 2)
    y_m = swap(mlstm(swap(qk_m[:, :, :H_M * DK_M]), qk_m,
                     swap(proj3[:, :, COL_M_V:COL_M_V + H_M * DV_M]),
                     swap(proj3[:, :, COL_M_O:COL_M_O + H_M * DV_M]), grow))

    mixed = branch_mix(y_da.reshape(N_TOK, H_DA * DV_DA), y_m.reshape(N_TOK, H_M * DV_M),
                       w_branch_da[layer], w_branch_m[layer], gate)
    h2 = matmul(mixed, w_out[layer], n_cols=D_MODEL, tm=1024, tn=1024,
                res=x.reshape(N_TOK, D_MODEL), out_dtype=F32, name="out_proj")

    hn_packed, top_e, weight, rank, counts = ffn_prep(
        h2, norm_ffn[layer], w_router[layer], b_router[layer])
    dest, seg_start, seg_rows, pad0, padn, slack = _plan(
        counts, top_e[:, :TOP_K], rank[:, :TOP_K])
    xs = dispatch(dest, pad0, padn, slack, hn_packed)
    act = moe_up(seg_start, seg_rows, slack, xs, w1[layer], b1[layer])
    y = moe_down(seg_start, seg_rows, slack, act, w2[layer], b2[layer])
    out = combine(dest, y, weight, h2, norm_final)
    return out.reshape(BATCH, SEQ, D_MODEL)
```

```python
import functools
import math

import jax
import jax.numpy as jnp
from jax import lax
from jax.experimental import pallas as pl
from jax.experimental.pallas import tpu as pltpu

F32 = jnp.float32
BF16 = jnp.bfloat16

D_MODEL = 2048
BATCH = 2
SEQ = 4096
N_META = 16
BLOCK = 128
PAD = (-N_META) % BLOCK
LP = PAD + N_META + SEQ
NBLK = LP // BLOCK
EPS = 1e-6
NEG_INF = -1e30

H_DA = 4
DK_DA = 128
DV_DA = 256
H_M = 4
DK_M = 128
DV_M = 256
CONV_W = 5
N_BUCKETS = 32
MAX_DISTANCE = 128
N_EXPERTS = 32
TOP_K = 4
D_FF = 2048
SWIGLU_ALPHA = 1.702
SWIGLU_LIMIT = 7.0
LAMBDA_INIT = 0.8 - 0.6 * math.exp(-0.3 * 0)

COL_DA_Q = 0
COL_DA_K = 1024
COL_DA_V = 2048
COL_M_Q = 3072
COL_M_K = 3584
COL_M_V = 4096
COL_M_O = 5120
COL_M_G = 6144
PROJ_COLS = 6144

N_TOK = BATCH * SEQ
N_ASSIGN = N_TOK * TOP_K

LANES = 128
VMEM_LIMIT = 52 * 1024 * 1024
MOE_VMEM_LIMIT = 58 * 1024 * 1024


def _cparams(n_axes):
    return pltpu.CompilerParams(
        dimension_semantics=("arbitrary",) * n_axes, vmem_limit_bytes=VMEM_LIMIT)


def _rms(v, gain):
    ms = jnp.mean(v * v, axis=-1, keepdims=True)
    return v * lax.rsqrt(ms + EPS) * gain


def _norm_in_kernel(x_ref, meta_ref, g_ref, pad_ref, real_ref):
    j = pl.program_id(1)
    g = g_ref[...]

    @pl.when(j == 0)
    def _():
        pad_ref[0, :PAD, :] = jnp.zeros((PAD, D_MODEL), BF16)
        pad_ref[0, PAD:, :] = _rms(meta_ref[...], g).astype(BF16)

    @pl.when(j > 0)
    def _():
        y = _rms(x_ref[0], g).astype(BF16)
        pad_ref[0] = y
        real_ref[0] = y


def norm_in(x, meta, gain):
    return pl.pallas_call(
        _norm_in_kernel,
        grid=(BATCH, NBLK),
        in_specs=[
            pl.BlockSpec((1, BLOCK, D_MODEL), lambda b, j: (b, jnp.maximum(j - 1, 0), 0)),
            pl.BlockSpec((N_META, D_MODEL), lambda b, j: (0, 0)),
            pl.BlockSpec((1, D_MODEL), lambda b, j: (0, 0)),
        ],
        out_specs=[
            pl.BlockSpec((1, BLOCK, D_MODEL), lambda b, j: (b, j, 0)),
            pl.BlockSpec((1, BLOCK, D_MODEL), lambda b, j: (b, jnp.maximum(j - 1, 0), 0)),
        ],
        out_shape=[
            jax.ShapeDtypeStruct((BATCH, LP, D_MODEL), BF16),
            jax.ShapeDtypeStruct((BATCH, SEQ, D_MODEL), BF16),
        ],
        compiler_params=_cparams(2),
        name="norm_in",
    )(x, meta, gain.reshape(1, D_MODEL))


def _mm_kernel(*refs, has_bias, has_res, act, w_transposed):
    x_ref, w_ref = refs[0], refs[1]
    pos = 2
    b_ref = r_ref = None
    if has_bias:
        b_ref = refs[pos]
        pos += 1
    if has_res:
        r_ref = refs[pos]
        pos += 1
    o_ref, wbf_ref = refs[pos], refs[pos + 1]

    @pl.when(pl.program_id(1) == 0)
    def _():
        w = w_ref[...]
        wbf_ref[...] = (w.T if w_transposed else w).astype(BF16)

    acc = jnp.dot(x_ref[...], wbf_ref[...], preferred_element_type=F32)
    if has_bias:
        acc = acc + b_ref[...]
    if act == "sigmoid":
        acc = jax.nn.sigmoid(acc)
    if has_res:
        acc = acc + r_ref[...]
    o_ref[...] = acc.astype(o_ref.dtype)


def matmul(x, w, *, n_cols, col_block0=0, tm, tn, bias=None, res=None, act=None,
           out_dtype=BF16, w_transposed=False, name):
    m, k = x.shape
    in_specs = [
        pl.BlockSpec((tm, k), lambda j, i: (i, 0)),
        pl.BlockSpec((tn, k), lambda j, i: (j + col_block0, 0)) if w_transposed
        else pl.BlockSpec((k, tn), lambda j, i: (0, j + col_block0)),
    ]
    args = [x, w]
    if bias is not None:
        in_specs.append(pl.BlockSpec((1, tn), lambda j, i: (0, j)))
        args.append(bias.reshape(1, n_cols))
    if res is not None:
        in_specs.append(pl.BlockSpec((tm, tn), lambda j, i: (i, j)))
        args.append(res)
    return pl.pallas_call(
        functools.partial(_mm_kernel, has_bias=bias is not None, has_res=res is not None, act=act,
                          w_transposed=w_transposed),
        grid=(n_cols // tn, m // tm),
        in_specs=in_specs,
        out_specs=pl.BlockSpec((tm, tn), lambda j, i: (i, j)),
        out_shape=jax.ShapeDtypeStruct((m, n_cols), out_dtype),
        scratch_shapes=[pltpu.VMEM((k, tn), BF16)],
        compiler_params=_cparams(2),
        name=name,
    )(*args)


def _conv_kernel(p_ref, w_ref, o_ref):
    c = pl.program_id(1)
    x = p_ref[0].astype(F32)
    w = w_ref[...]
    half = CONV_W // 2
    acc = w[half:half + 1, :] * x
    for j in range(CONV_W):
        if j != half:
            acc = acc + w[j:j + 1, :] * pltpu.roll(x, (half - j) % LP, axis=0)
    y = acc * jax.nn.sigmoid(acc)
    rows = lax.broadcasted_iota(jnp.int32, (LP, 1), 0)
    y = jnp.where(rows >= PAD, y, 0.0)
    scale = jnp.where(c < 2, DK_M ** -0.5, 1.0).astype(F32)
    o_ref[0] = (y * scale).astype(BF16)


def conv_qk(proj3, conv_w):
    cw = 256
    return pl.pallas_call(
        _conv_kernel,
        grid=(BATCH, (2 * H_M * DK_M) // cw),
        in_specs=[
            pl.BlockSpec((1, LP, cw), lambda b, c: (b, 0, COL_M_Q // cw + c)),
            pl.BlockSpec((CONV_W, cw), lambda b, c: (0, c)),
        ],
        out_specs=pl.BlockSpec((1, LP, cw), lambda b, c: (b, 0, c)),
        out_shape=jax.ShapeDtypeStruct((BATCH, LP, 2 * H_M * DK_M), BF16),
        compiler_params=_cparams(2),
        name="conv_qk",
    )(proj3, conv_w)


def _split_dot(tri, v):
    hi = v.astype(BF16)
    r1 = v - hi.astype(F32)
    mid = r1.astype(BF16)
    lo = (r1 - mid.astype(F32)).astype(BF16)
    return (jnp.dot(tri, hi, preferred_element_type=F32)
            + jnp.dot(tri, mid, preferred_element_type=F32)
            + jnp.dot(tri, lo, preferred_element_type=F32))


def _gate_kernel(g_ref, bias_ref, o_ref):
    ti = lax.broadcasted_iota(jnp.int32, (BLOCK, BLOCK), 0)
    ui = lax.broadcasted_iota(jnp.int32, (BLOCK, BLOCK), 1)
    tril = jnp.where(ui <= ti, 1.0, 0.0).astype(BF16)
    triu = jnp.where(ui >= ti, 1.0, 0.0).astype(BF16)
    ch = lax.broadcasted_iota(jnp.int32, (BLOCK, 4 * H_M), 1)
    typ = lax.shift_right_logical(ch, 2)
    rloc = lax.broadcasted_iota(jnp.int32, (BLOCK, 4 * H_M), 0)

    def body(c, carry):
        r0 = pl.multiple_of(c * BLOCK, BLOCK)
        g = g_ref[0, pl.ds(r0, BLOCK), :][:, :4 * H_M] + bias_ref[...]
        valid = (rloc + r0) >= PAD
        lsig = -(jnp.maximum(-g, 0.0) + jnp.log1p(jnp.exp(-jnp.abs(g))))
        lf = jnp.where(valid, lsig, 0.0)
        cum = _split_dot(tril, lf)
        rcum = _split_dot(triu, lf)
        li = jnp.where(valid, g, -jnp.inf)
        out = jnp.where(typ == 1, cum, jnp.where(typ == 3, rcum, li))
        o_ref[0, pl.ds(r0, BLOCK), :] = out
        return carry

    lax.fori_loop(0, NBLK, body, 0)


def gate_prep(mg3, gate_bias):
    return pl.pallas_call(
        _gate_kernel,
        grid=(BATCH,),
        in_specs=[
            pl.BlockSpec((1, LP, LANES), lambda b: (b, 0, 0)),
            pl.BlockSpec((1, 4 * H_M), lambda b: (0, 0)),
        ],
        out_specs=pl.BlockSpec((1, LP, 4 * H_M), lambda b: (b, 0, 0)),
        out_shape=jax.ShapeDtypeStruct((BATCH, LP, 4 * H_M), F32),
        compiler_params=_cparams(1),
        name="gate_prep",
    )(mg3, gate_bias.reshape(1, 4 * H_M))


MLSTM_HP = 2
MLSTM_MID = NBLK // 2


def _mlstm_kernel(qt_ref, k_ref, vt_ref, ot_ref, gr_ref, y_ref,
                  hs_ref, c_ref, n_ref, m_ref):
    c_ref[...] = jnp.zeros_like(c_ref)
    n_ref[...] = jnp.zeros_like(n_ref)
    m_ref[...] = jnp.zeros_like(m_ref)
    si = lax.broadcasted_iota(jnp.int32, (BLOCK, BLOCK), 0)
    ti = lax.broadcasted_iota(jnp.int32, (BLOCK, BLOCK), 1)
    mask_f = si <= ti
    mask_b = si >= ti

    def chain(c, hl, bwd, final):
        idx = 2 * hl + bwd
        r0 = c * BLOCK if isinstance(c, int) else pl.multiple_of(c * BLOCK, BLOCK)
        t_sl = pl.ds(r0, BLOCK)
        qt = qt_ref[0, hl * DK_M:(hl + 1) * DK_M, t_sl]
        k = k_ref[0, t_sl, hl * DK_M:(hl + 1) * DK_M]
        feat = slice(hl * DV_M, (hl + 1) * DV_M)
        vt = vt_ref[0, feat, t_sl]
        gr = gr_ref[0, hl, :, t_sl]
        li_r, b_r = gr[2 * bwd:2 * bwd + 1, :], gr[2 * bwd + 1:2 * bwd + 2, :]
        a_c = jnp.transpose(jnp.broadcast_to(li_r - b_r, (BLOCK, BLOCK)))
        b_end = b_r[:, 0:1] if bwd else b_r[:, BLOCK - 1:BLOCK]
        m_prev = m_ref[idx][:, 0:1]
        ct = c_ref[idx]
        nst = n_ref[idx]
        dmat = jnp.where(mask_b if bwd else mask_f, a_c + b_r, -jnp.inf)
        inter = b_r + m_prev
        m_t = jnp.maximum(inter, jnp.max(dmat, axis=0, keepdims=True))
        w_inter = jnp.exp(inter - m_t)
        st = jnp.dot(k, qt, preferred_element_type=F32) * jnp.exp(dmat - m_t)
        num = (w_inter * jnp.dot(ct.astype(BF16), qt, preferred_element_type=F32)
               + jnp.dot(vt, st.astype(BF16), preferred_element_type=F32))
        nq = jnp.dot(nst.astype(BF16), qt, preferred_element_type=F32)
        den = w_inter * nq + jnp.sum(st, axis=0, keepdims=True)
        h = num * (1.0 / jnp.maximum(jnp.abs(den), jnp.exp(-m_t)))
        if final:
            og = jax.nn.sigmoid(ot_ref[0, feat, t_sl].astype(F32))
            y_ref[0, feat, pl.ds(r0 - BLOCK, BLOCK)] = (og * (hs_ref[feat, t_sl] + h)).astype(BF16)
        else:
            hs_ref[feat, t_sl] = h
        ldec = b_end - b_r + li_r
        m_new = jnp.maximum(b_end + m_prev, jnp.max(ldec, axis=1, keepdims=True))
        w_c = jnp.exp(b_end + m_prev - m_new)
        w_s = jnp.exp(ldec - m_new)
        wvt = (vt.astype(F32) * w_s).astype(BF16)
        c_ref[idx] = w_c * ct + jnp.dot(wvt, k, preferred_element_type=F32)
        n_ref[idx] = w_c * nst + jnp.dot(w_s.astype(BF16), k, preferred_element_type=F32)
        m_ref[idx] = jnp.broadcast_to(m_new, (1, BLOCK))

    def first_half(i, carry):
        for hl in range(MLSTM_HP):
            chain(i, hl, 0, False)
            chain(NBLK - 1 - i, hl, 1, False)
        return carry

    def second_half(i, carry):
        for hl in range(MLSTM_HP):
            chain(i, hl, 0, True)
            chain(NBLK - 1 - i, hl, 1, True)
        return carry

    lax.fori_loop(0, MLSTM_MID, first_half, 0, unroll=8)
    for hl in range(MLSTM_HP):
        chain(MLSTM_MID, hl, 0, False)
        chain(MLSTM_MID, hl, 1, True)
    lax.fori_loop(MLSTM_MID + 1, NBLK - 1, second_half, 0, unroll=5)
    for hl in range(MLSTM_HP):
        chain(NBLK - 1, hl, 0, True)


def mlstm(q_t, qk_m, v_t, o_t, grow):
    hp = MLSTM_HP
    kw, vw = hp * DK_M, hp * DV_M
    return pl.pallas_call(
        _mlstm_kernel,
        grid=(BATCH, H_M // hp),
        in_specs=[
            pl.BlockSpec((1, kw, LP), lambda b, g: (b, g, 0)),
            pl.BlockSpec((1, LP, kw), lambda b, g: (b, 0, (H_M * DK_M) // kw + g)),
            pl.BlockSpec((1, vw, LP), lambda b, g: (b, g, 0)),
            pl.BlockSpec((1, vw, LP), lambda b, g: (b, g, 0)),
            pl.BlockSpec((1, hp, 4, LP), lambda b, g: (b, g, 0, 0)),
        ],
        out_specs=pl.BlockSpec((1, vw, SEQ), lambda b, g: (b, g, 0)),
        out_shape=jax.ShapeDtypeStruct((BATCH, H_M * DV_M, SEQ), BF16),
        scratch_shapes=[
            pltpu.VMEM((vw, LP), F32),
            pltpu.VMEM((2 * hp, DV_M, DK_M), F32),
            pltpu.VMEM((2 * hp, 1, DK_M), F32),
            pltpu.VMEM((2 * hp, 1, BLOCK), F32),
        ],
        compiler_params=_cparams(2),
        name="mlstm",
    )(q_t, qk_m, v_t, o_t, grow)


LOG2E = 1.4426950408889634
ATT_QB = 2
ATT_TQ = ATT_QB * BLOCK
ATT_BAND = (ATT_QB + 2) * BLOCK
ATT_GROUPS = (6, 6, 6, 6, 5)
assert BLOCK >= MAX_DISTANCE and ATT_BAND + sum(ATT_GROUPS) * BLOCK == LP


ATT_NQ = SEQ // ATT_TQ


def _attn_kernel(c_ref, *refs):
    q_refs = refs[:ATT_QB]
    (k1_ref, v1_ref, tab_ref, lam_ref, sg_ref, o_ref,
     s_a, s_b, mx_a, mx_b, k_ref, v_ref) = refs[ATT_QB:]
    h = pl.program_id(1)
    t = pl.program_id(2)

    @pl.when(t == 0)
    def _():
        for rep in range(2):
            k_ref[0, rep * LP:(rep + 1) * LP, :] = k1_ref[0]
            v_ref[0, rep * LP:(rep + 1) * LP, :] = v1_ref[0]

    scale = DK_DA ** -0.5 * LOG2E
    c_neg = c_ref[h, 0]
    c_pos = c_ref[h, 1]

    def groups_of(tile):
        koff = (ATT_QB * tile) * BLOCK
        out = [(koff, ATT_BAND, 0)]
        col = ATT_BAND
        for nblk in ATT_GROUPS:
            out.append((koff + col, nblk * BLOCK, col))
            col += nblk * BLOCK
        return out

    def lane_fold(acc, x, op):
        for j in range(x.shape[1] // BLOCK):
            piece = x[:, j * BLOCK:(j + 1) * BLOCK]
            acc = piece if acc is None else op(acc, piece)
        return acc

    def score(tile, s_ref, mx_ref):
        q = jnp.concatenate([r[0] for r in q_refs], axis=0)
        mx = [None, None]
        for gi, (koff, width, col0) in enumerate(groups_of(tile)):
            koff = pl.multiple_of(koff, BLOCK)
            if gi == 0:
                bias = tab_ref[0, 0]
            else:
                kpos = koff + lax.broadcasted_iota(jnp.int32, (1, width), 1)
                bias = jnp.where(kpos < LP, c_pos, jnp.where(kpos < LP + PAD, NEG_INF, c_neg))
            for m in range(2):
                kk = k_ref[0, pl.ds(koff, width), m * DK_DA:(m + 1) * DK_DA]
                s = lax.dot_general(q[:, m * DK_DA:(m + 1) * DK_DA], kk,
                                    (((1,), (1,)), ((), ())),
                                    preferred_element_type=F32) * scale + bias
                s_ref[m, :, col0:col0 + width] = s
                mx[m] = lane_fold(mx[m], s, jnp.maximum)
        for m in range(2):
            mx_ref[m] = mx[m]

    def finish(tile, s_ref, mx_ref):
        lp = lam_ref[...]
        lam = (jnp.exp(jnp.sum(lp[0:1] * lp[1:2], axis=1, keepdims=True))
               - jnp.exp(jnp.sum(lp[2:3] * lp[3:4], axis=1, keepdims=True)) + LAMBDA_INIT)
        row_max = [jnp.max(mx_ref[m], axis=1, keepdims=True) for m in range(2)]
        lsum = [None, None]
        acc = [None, None]
        for koff, width, col0 in groups_of(tile):
            koff = pl.multiple_of(koff, BLOCK)
            vv = v_ref[0, pl.ds(koff, width), :]
            for m in range(2):
                p = jnp.exp2(s_ref[m, :, col0:col0 + width] - row_max[m])
                lsum[m] = lane_fold(lsum[m], p, jnp.add)
                pv = jnp.dot(p.astype(BF16), vv, preferred_element_type=F32)
                acc[m] = pv if acc[m] is None else acc[m] + pv
        l1 = jnp.sum(lsum[0], axis=1, keepdims=True)
        l2 = jnp.sum(lsum[1], axis=1, keepdims=True)
        o = acc[0] / l1 - lam * (acc[1] / l2)
        o_ref[0] = (_rms(o, sg_ref[...]) * (1.0 - LAMBDA_INIT)).astype(BF16)

    even = lax.rem(t, 2) == 0
    inner = jnp.logical_and(t > 0, t < ATT_NQ)

    @pl.when(t == 0)
    def _():
        score(t, s_a, mx_a)

    @pl.when(jnp.logical_and(inner, even))
    def _():
        score(t, s_a, mx_a)
        finish(t - 1, s_b, mx_b)

    @pl.when(jnp.logical_and(inner, jnp.logical_not(even)))
    def _():
        score(t, s_b, mx_b)
        finish(t - 1, s_a, mx_a)

    @pl.when(t == ATT_NQ)
    def _():
        if (ATT_NQ - 1) % 2 == 0:
            finish(t - 1, s_a, mx_a)
        else:
            finish(t - 1, s_b, mx_b)


def diff_attention(consts, proj3, tab, lam_params, subln):
    nq = ATT_NQ
    kblk0 = COL_DA_K // (2 * DK_DA)
    vblk0 = COL_DA_V // DV_DA
    scored = lambda i: jnp.minimum(i, nq - 1)
    finished = lambda i: jnp.maximum(i - 1, 0)

    def tab_map(b, h, i):
        tile = scored(i)
        case = jnp.where(tile == 0, 0, jnp.where(tile == nq - 1, 2, 1))
        return (h, case, 0, 0)

    return pl.pallas_call(
        _attn_kernel,
        grid=(BATCH, H_DA, nq + 1),
        in_specs=[
            pl.BlockSpec(memory_space=pltpu.SMEM),
            *[pl.BlockSpec((1, BLOCK, 2 * DK_DA),
                           functools.partial(
                               lambda b, h, i, r: (b, ATT_QB * scored(i) + 1 + r, h), r=r))
              for r in range(ATT_QB)],
            pl.BlockSpec((1, LP, 2 * DK_DA), lambda b, h, i: (b, 0, kblk0 + h)),
            pl.BlockSpec((1, LP, DV_DA), lambda b, h, i: (b, 0, vblk0 + h)),
            pl.BlockSpec((1, 1, ATT_TQ, ATT_BAND), tab_map),
            pl.BlockSpec((4, DK_DA), lambda b, h, i: (0, 0)),
            pl.BlockSpec((1, DV_DA), lambda b, h, i: (0, 0)),
        ],
        out_specs=pl.BlockSpec((1, ATT_TQ, DV_DA), lambda b, h, i: (b, finished(i), h)),
        out_shape=jax.ShapeDtypeStruct((BATCH, SEQ, H_DA * DV_DA), BF16),
        scratch_shapes=[pltpu.VMEM((2, ATT_TQ, LP), F32),
                        pltpu.VMEM((2, ATT_TQ, LP), F32),
                        pltpu.VMEM((2, ATT_TQ, BLOCK), F32),
                        pltpu.VMEM((2, ATT_TQ, BLOCK), F32),
                        pltpu.VMEM((1, 2 * LP, 2 * DK_DA), BF16),
                        pltpu.VMEM((1, 2 * LP, DV_DA), BF16)],
        compiler_params=_cparams(3),
        name="diff_attn",
    )(consts, *([proj3] * (ATT_QB + 2)), tab, lam_params, subln.reshape(1, DV_DA))


def _bias_tables(rel_bias):
    rb = rel_bias.astype(F32)
    span = 1024
    assert span >= ATT_TQ + ATT_BAND
    rel = jnp.arange(span, dtype=jnp.int32) - span // 2
    nb = N_BUCKETS // 2
    max_exact = nb // 2
    n = jnp.abs(rel)
    nf = jnp.maximum(n, 1).astype(F32)
    large = max_exact + (jnp.log(nf / max_exact) / math.log(MAX_DISTANCE / max_exact)
                         * (nb - max_exact)).astype(jnp.int32)
    large = jnp.minimum(large, nb - 1)
    bucket = jnp.where(rel > 0, nb, 0) + jnp.where(n < max_exact, n, large)
    hit = bucket[None, :, None] == jnp.arange(N_BUCKETS, dtype=jnp.int32)
    by_rel = jnp.sum(jnp.where(hit, rb.T[:, None, :], 0.0), axis=-1)
    shifted = jnp.tile(by_rel, (1, ATT_TQ))[:, :ATT_TQ * (span - 1)].reshape(
        H_DA, ATT_TQ, span - 1)
    c0 = span // 2 - BLOCK
    gen = shifted[:, :, c0:c0 + ATT_BAND]
    c_neg = rb[nb - 1]
    c_pos = rb[N_BUCKETS - 1]
    jj = jnp.arange(ATT_BAND, dtype=jnp.int32)[None, None, :]
    first = jnp.where(jj < PAD, NEG_INF, gen)
    wrap0 = ATT_BAND - BLOCK
    wrapped = jnp.where(jj - wrap0 < PAD, NEG_INF, c_neg[:, None, None])
    last = jnp.where(jj >= wrap0, wrapped, gen)
    tab = jnp.stack([first, gen, last], axis=1)
    consts = jnp.stack([c_neg, c_pos], axis=1)
    return tab * LOG2E, consts * LOG2E


def _mix_kernel(ya_ref, ym_ref, wa_ref, wm_ref, ga_ref, gm_ref, o_ref, wa_bf, wm_bf):
    @pl.when(pl.program_id(1) == 0)
    def _():
        wa_bf[...] = wa_ref[...].astype(BF16)
        wm_bf[...] = wm_ref[...].astype(BF16)

    a = jnp.dot(ya_ref[...], wa_bf[...], preferred_element_type=F32)
    m = jnp.dot(ym_ref[...], wm_bf[...], preferred_element_type=F32)
    o_ref[...] = (ga_ref[...].astype(F32) * a + gm_ref[...].astype(F32) * m).astype(BF16)


def branch_mix(y_da, y_m, w_da, w_m, gate, *, tm=512, tn=1024):
    m, k = y_da.shape
    nj = D_MODEL // tn
    return pl.pallas_call(
        _mix_kernel,
        grid=(nj, m // tm),
        in_specs=[
            pl.BlockSpec((tm, k), lambda j, i: (i, 0)),
            pl.BlockSpec((tm, k), lambda j, i: (i, 0)),
            pl.BlockSpec((k, tn), lambda j, i: (0, j)),
            pl.BlockSpec((k, tn), lambda j, i: (0, j)),
            pl.BlockSpec((tm, tn), lambda j, i: (i, j)),
            pl.BlockSpec((tm, tn), lambda j, i: (i, nj + j)),
        ],
        out_specs=pl.BlockSpec((tm, tn), lambda j, i: (i, j)),
        out_shape=jax.ShapeDtypeStruct((m, D_MODEL), BF16),
        scratch_shapes=[pltpu.VMEM((k, tn), BF16), pltpu.VMEM((k, tn), BF16)],
        compiler_params=_cparams(2),
        name="branch_mix",
    )(y_da, y_m, w_da, w_m, gate, gate)


FFN_TM = 1024
HALF = D_MODEL // 2


def _pack_bf16_pairs(v):
    lo = lax.bitcast_convert_type(v[:, :HALF].astype(BF16).astype(F32), jnp.uint32)
    hi = lax.bitcast_convert_type(v[:, HALF:].astype(BF16).astype(F32), jnp.uint32)
    return (hi & jnp.uint32(0xFFFF0000)) | lax.shift_right_logical(lo, jnp.uint32(16))


def _unpack_bf16_pairs(w):
    lo = lax.bitcast_convert_type(lax.shift_left(w, jnp.uint32(16)), F32).astype(BF16)
    hi = lax.bitcast_convert_type(w & jnp.uint32(0xFFFF0000), F32).astype(BF16)
    return lo, hi


def _ffn_prep_kernel(h_ref, g_ref, wr_ref, br_ref, hn_ref, e_ref, w_ref, r_ref, cnt_ref, base_ref):
    @pl.when(pl.program_id(0) == 0)
    def _():
        base_ref[...] = jnp.zeros_like(base_ref)

    hn = _rms(h_ref[...], g_ref[...])
    hn_ref[...] = _pack_bf16_pairs(hn)
    logits = jnp.dot(hn.astype(BF16), wr_ref[...].astype(BF16),
                     preferred_element_type=F32) + br_ref[...]
    lane = lax.broadcasted_iota(jnp.int32, (FFN_TM, N_EXPERTS), 1)
    lane_o = lax.broadcasted_iota(jnp.int32, (FFN_TM, LANES), 1)
    ti = lax.broadcasted_iota(jnp.int32, (FFN_TM, FFN_TM), 0)
    ui = lax.broadcasted_iota(jnp.int32, (FFN_TM, FFN_TM), 1)
    tril = jnp.where(ui <= ti, 1.0, 0.0).astype(BF16)
    e_out = jnp.zeros((FFN_TM, LANES), jnp.int32)
    r_out = jnp.zeros((FFN_TM, LANES), jnp.int32)
    l_out = jnp.full((FFN_TM, LANES), -jnp.inf, F32)
    base = base_ref[...]
    l = logits
    for kk in range(TOP_K):
        mk = jnp.max(l, axis=1, keepdims=True)
        ik = jnp.min(jnp.where(l == mk, lane, N_EXPERTS), axis=1, keepdims=True)
        hit = lane == ik
        oh = jnp.where(hit, 1.0, 0.0)
        cum = jnp.dot(tril, oh.astype(BF16), preferred_element_type=F32)
        rank = jnp.sum(oh * (cum + base), axis=1, keepdims=True) - 1.0
        base = base + jnp.sum(oh, axis=0, keepdims=True)
        e_out = jnp.where(lane_o == kk, ik, e_out)
        r_out = jnp.where(lane_o == kk, rank.astype(jnp.int32), r_out)
        l_out = jnp.where(lane_o == kk, mk, l_out)
        l = jnp.where(hit, -jnp.inf, l)
    base_ref[...] = base
    cnt_ref[...] = base
    ex = jnp.exp(l_out - jnp.max(l_out, axis=1, keepdims=True))
    e_ref[...] = e_out
    r_ref[...] = r_out
    w_ref[...] = ex / jnp.sum(ex, axis=1, keepdims=True)


def ffn_prep(h2, gain, w_router, b_router):
    row = lambda i: (i, 0)
    fixed = lambda i: (0, 0)
    return pl.pallas_call(
        _ffn_prep_kernel,
        grid=(N_TOK // FFN_TM,),
        in_specs=[
            pl.BlockSpec((FFN_TM, D_MODEL), row),
            pl.BlockSpec((1, D_MODEL), fixed),
            pl.BlockSpec((D_MODEL, N_EXPERTS), fixed),
            pl.BlockSpec((1, N_EXPERTS), fixed),
        ],
        out_specs=[
            pl.BlockSpec((FFN_TM, HALF), row),
            pl.BlockSpec((FFN_TM, LANES), row),
            pl.BlockSpec((FFN_TM, LANES), row),
            pl.BlockSpec((FFN_TM, LANES), row),
            pl.BlockSpec((1, N_EXPERTS), fixed),
        ],
        out_shape=[
            jax.ShapeDtypeStruct((N_TOK, HALF), jnp.uint32),
            jax.ShapeDtypeStruct((N_TOK, LANES), jnp.int32),
            jax.ShapeDtypeStruct((N_TOK, LANES), F32),
            jax.ShapeDtypeStruct((N_TOK, LANES), jnp.int32),
            jax.ShapeDtypeStruct((1, N_EXPERTS), F32),
        ],
        scratch_shapes=[pltpu.VMEM((1, N_EXPERTS), F32)],
        compiler_params=_cparams(1),
        name="ffn_prep",
    )(h2, gain.reshape(1, D_MODEL), w_router, b_router.reshape(1, N_EXPERTS))


SEG_ALIGN = 128
MOE_R = N_ASSIGN + N_EXPERTS * SEG_ALIGN


def _plan(counts_f, top_e, rank):
    counts = counts_f[0].astype(jnp.int32)
    seg_rows = (counts + SEG_ALIGN - 1) // SEG_ALIGN * SEG_ALIGN
    seg_start = jnp.cumsum(seg_rows) - seg_rows
    eq = top_e[:, :, None] == jnp.arange(N_EXPERTS, dtype=jnp.int32)[None, None, :]
    dest = jnp.sum(jnp.where(eq, seg_start[None, None, :], 0), axis=-1) + rank
    used = jnp.sum(seg_rows)
    slack = jnp.stack([used, (MOE_R - used) // SEG_ALIGN])
    return (dest.reshape(N_ASSIGN).astype(jnp.int32), seg_start.astype(jnp.int32),
            seg_rows.astype(jnp.int32), (seg_start + counts).astype(jnp.int32),
            (seg_rows - counts).astype(jnp.int32), slack.astype(jnp.int32))


def _zero_slack(slack_ref, zero_block, dst_rows, sem):
    zero_block[...] = jnp.zeros_like(zero_block)

    def copy(j):
        r0 = pl.multiple_of(slack_ref[0] + j * SEG_ALIGN, SEG_ALIGN)
        return pltpu.make_async_copy(zero_block, dst_rows(pl.ds(r0, SEG_ALIGN)), sem)

    def start(j, carry):
        copy(j).start()
        return carry

    def wait(j, carry):
        copy(j).wait()
        return carry

    lax.fori_loop(0, slack_ref[1], start, 0)
    lax.fori_loop(0, slack_ref[1], wait, 0)


DISP_TOK = 1024


def _dispatch_kernel(dest_ref, pad0_ref, padn_ref, slack_ref, hn_ref, xs_ref,
                     zrow_ref, zblk_ref, sem, zsem):
    i = pl.program_id(0)

    @pl.when(i == 0)
    def _():
        zrow_ref[...] = jnp.zeros_like(zrow_ref)
        _zero_slack(slack_ref, zblk_ref, lambda rows: xs_ref.at[rows], zsem.at[0])

        def expert(e, carry):
            p0 = pad0_ref[e]
            pn = padn_ref[e]

            def zstart(r, c2):
                pltpu.make_async_copy(zrow_ref, xs_ref.at[pl.ds(p0 + r, 1)], zsem.at[0]).start()
                return c2

            def zwait(r, c2):
                pltpu.make_async_copy(zrow_ref, xs_ref.at[pl.ds(p0, 1)], zsem.at[0]).wait()
                return c2

            lax.fori_loop(0, pn, zstart, 0)
            lax.fori_loop(0, pn, zwait, 0)
            return carry

        lax.fori_loop(0, N_EXPERTS, expert, 0)

    def group(g, carry):
        for r in range(8):
            a = (i * DISP_TOK + g * 8 + r) * TOP_K
            for kk in range(TOP_K):
                pltpu.make_async_copy(hn_ref.at[g, pl.ds(r, 1)],
                                      xs_ref.at[pl.ds(dest_ref[a + kk], 1)],
                                      sem.at[0]).start(priority=kk % 2)
        return carry

    lax.fori_loop(0, DISP_TOK // 8, group, 0)
    for kk in range(TOP_K):
        pltpu.make_async_copy(xs_ref.at[pl.ds(0, DISP_TOK)], xs_ref.at[pl.ds(0, DISP_TOK)],
                              sem.at[0]).wait()


def dispatch(dest, pad0, padn, slack, hn_packed):
    grid_spec = pltpu.PrefetchScalarGridSpec(
        num_scalar_prefetch=4,
        grid=(N_TOK // DISP_TOK,),
        in_specs=[pl.BlockSpec((DISP_TOK // 8, 8, HALF), lambda i, d, p0, pn, z: (i, 0, 0))],
        out_specs=pl.BlockSpec(memory_space=pl.ANY),
        scratch_shapes=[pltpu.VMEM((1, HALF), jnp.uint32),
                        pltpu.VMEM((SEG_ALIGN, HALF), jnp.uint32),
                        pltpu.SemaphoreType.DMA((1,)), pltpu.SemaphoreType.DMA((1,))],
    )
    return pl.pallas_call(
        _dispatch_kernel,
        grid_spec=grid_spec,
        out_shape=jax.ShapeDtypeStruct((MOE_R, HALF), jnp.uint32),
        compiler_params=_cparams(1),
        name="moe_dispatch",
    )(dest, pad0, padn, slack, hn_packed.reshape(N_TOK // 8, 8, HALF))


MOE_CH = 2 * SEG_ALIGN
MOE_TF = 1024
MOE_NF = D_FF // MOE_TF


class _CopyGroup:
    def __init__(self, copies):
        self.copies = copies

    def start(self, priority=0):
        for cp in self.copies:
            cp.start(priority=priority)

    def wait(self):
        for cp in self.copies:
            cp.wait()


W_PIECES = 8
W_PER_CHUNK = 3


N_STATE = 4


def _stream_rows(step, n_steps, start, rows, next_start, next_rows, state,
                 make_in, make_out, make_tail_in, make_tail_out,
                 compute_chunk, compute_tail, before_first_wait, next_weight_piece):
    has_next = step + 1 < n_steps

    def request_weights(first, count):
        def one(p, carry):
            @pl.when(jnp.logical_and(has_next, p < W_PIECES))
            def _():
                next_weight_piece(p).start()
            return carry

        lax.fori_loop(first, first + count, one, 0)

    @pl.when(step == 0)
    def _():
        for j in range(N_STATE):
            state[j] = 0

    n_ch = lax.shift_right_logical(rows, MOE_CH.bit_length() - 1)
    tail = rows - n_ch * MOE_CH
    tail_row = start + n_ch * MOE_CH
    g0 = state[0]
    feeds_next = jnp.logical_and(step + 1 < n_steps, next_rows >= MOE_CH)

    def chunk_row(c):
        return start + c * MOE_CH

    @pl.when(tail > 0)
    def _():
        make_tail_in(tail_row).start(priority=1)

    @pl.when(jnp.logical_and(n_ch > 0, state[3] == 0))
    def _():
        make_in(start, lax.rem(g0, 2)).start(priority=1)

    before_first_wait()

    @pl.when(tail > 0)
    def _():
        make_tail_in(tail_row).wait()
        compute_tail()
        make_tail_out(tail_row).start(priority=1)

    def body(c, carry):
        slot = lax.rem(g0 + c, 2)
        make_in(chunk_row(c), slot).wait()

        @pl.when(c + 1 < n_ch)
        def _():
            make_in(chunk_row(c + 1), 1 - slot).start(priority=1)

        @pl.when(jnp.logical_and(c + 1 == n_ch, feeds_next))
        def _():
            make_in(next_start, 1 - slot).start(priority=1)

        request_weights(c * W_PER_CHUNK, W_PER_CHUNK)

        @pl.when(state[1 + slot] == 1)
        def _():
            make_out(chunk_row(c), slot).wait()

        compute_chunk(slot)
        make_out(chunk_row(c), slot).start(priority=1)
        state[1 + slot] = 1
        return carry

    lax.fori_loop(0, n_ch, body, 0)
    request_weights(n_ch * W_PER_CHUNK, W_PIECES)
    state[0] = g0 + n_ch
    state[3] = jnp.where(jnp.logical_and(n_ch > 0, feeds_next), 1, 0)

    @pl.when(tail > 0)
    def _():
        make_tail_out(tail_row).wait()

    @pl.when(step == n_steps - 1)
    def _():
        for slot in range(2):
            @pl.when(state[1 + slot] == 1)
            def _():
                make_out(0, slot).wait()
                state[1 + slot] = 0


def _rows_at(row0, n):
    return pl.ds(row0 if isinstance(row0, int) else pl.multiple_of(row0, SEG_ALIGN), n)


def _moe_up_kernel(seg_ref, rows_ref, slack_ref, xs_ref, w_hbm, bg_ref, bl_ref, act_ref,
                   xbuf, obuf, xtail, otail, wbuf, wg_bf, wl_bf, state, isem, osem, tsem, wsem):
    f = pl.program_id(0)
    e = pl.program_id(1)
    start = seg_ref[e]
    rows = rows_ref[e]
    e_next = lax.rem(e + 1, N_EXPERTS)
    f_next = jnp.where(e == N_EXPERTS - 1, f + 1, f)
    step = f * N_EXPERTS + e
    n_steps = MOE_NF * N_EXPERTS
    wslot = lax.rem(step, 2)

    def weight_piece(expert, ftile, slot, p):
        per_half = W_PIECES // 2
        band = D_MODEL // per_half
        t = p // per_half if isinstance(p, int) else lax.shift_right_logical(
            p, per_half.bit_length() - 1)
        r0 = (p - t * per_half) * band
        r0 = r0 if isinstance(r0, int) else pl.multiple_of(r0, band)
        col0 = pl.multiple_of((t * MOE_NF + ftile) * MOE_TF, MOE_TF)
        return pltpu.make_async_copy(
            w_hbm.at[expert, pl.ds(r0, band), pl.ds(col0, MOE_TF)],
            wbuf.at[slot, t, pl.ds(r0, band)], wsem.at[slot, p])

    @pl.when(step == 0)
    def _():
        for p in range(W_PIECES):
            weight_piece(e, f, wslot, p).start()

    def make_in(r0, slot):
        return pltpu.make_async_copy(xs_ref.at[_rows_at(r0, MOE_CH)], xbuf.at[slot], isem.at[slot])

    def make_out(r0, slot):
        return pltpu.make_async_copy(obuf.at[slot], act_ref.at[f, _rows_at(r0, MOE_CH)],
                                     osem.at[slot])

    def make_tail_in(r0):
        return pltpu.make_async_copy(xs_ref.at[_rows_at(r0, SEG_ALIGN)], xtail, tsem.at[0])

    def make_tail_out(r0):
        return pltpu.make_async_copy(otail, act_ref.at[f, _rows_at(r0, SEG_ALIGN)], tsem.at[1])

    def cast_weights():
        per_half = W_PIECES // 2
        band = D_MODEL // per_half
        for p in range(W_PIECES):
            weight_piece(e, f, wslot, p).wait()
            t, rows_p = p // per_half, pl.ds((p % per_half) * band, band)
            (wg_bf, wl_bf)[t][rows_p, :] = wbuf[wslot, t, rows_p, :].astype(BF16)

    def expert_mlp(words):
        lo, hi = _unpack_bf16_pairs(words)
        glu = (jnp.dot(lo, wg_bf[:HALF, :], preferred_element_type=F32)
               + jnp.dot(hi, wg_bf[HALF:, :], preferred_element_type=F32) + bg_ref[0])
        lin = (jnp.dot(lo, wl_bf[:HALF, :], preferred_element_type=F32)
               + jnp.dot(hi, wl_bf[HALF:, :], preferred_element_type=F32) + bl_ref[0])
        glu = jnp.minimum(glu, SWIGLU_LIMIT)
        lin = jnp.clip(lin, -SWIGLU_LIMIT, SWIGLU_LIMIT)
        return (glu * jax.nn.sigmoid(SWIGLU_ALPHA * glu) * (lin + 1.0)).astype(BF16)

    def compute_chunk(slot):
        obuf[slot] = expert_mlp(xbuf[slot])

    def compute_tail():
        otail[...] = expert_mlp(xtail[...])

    _stream_rows(step, n_steps, start, rows,
                 seg_ref[e_next], rows_ref[e_next], state,
                 make_in, make_out, make_tail_in, make_tail_out,
                 compute_chunk, compute_tail, cast_weights,
                 lambda p: weight_piece(e_next, f_next, 1 - wslot, p))

    @pl.when(e == N_EXPERTS - 1)
    def _():
        _zero_slack(slack_ref, otail, lambda rr: act_ref.at[f, rr], tsem.at[1])


def moe_up(seg_start, seg_rows, slack, xs, w1, b1):
    grid_spec = pltpu.PrefetchScalarGridSpec(
        num_scalar_prefetch=3,
        grid=(MOE_NF, N_EXPERTS),
        in_specs=[
            pl.BlockSpec(memory_space=pl.ANY),
            pl.BlockSpec(memory_space=pl.ANY),
            pl.BlockSpec((1, 1, MOE_TF), lambda f, e, s, r, z: (e, 0, f)),
            pl.BlockSpec((1, 1, MOE_TF), lambda f, e, s, r, z: (e, 0, MOE_NF + f)),
        ],
        out_specs=pl.BlockSpec(memory_space=pl.ANY),
        scratch_shapes=[
            pltpu.VMEM((2, MOE_CH, HALF), jnp.uint32),
            pltpu.VMEM((2, MOE_CH, MOE_TF), BF16),
            pltpu.VMEM((SEG_ALIGN, HALF), jnp.uint32),
            pltpu.VMEM((SEG_ALIGN, MOE_TF), BF16),
            pltpu.VMEM((2, 2, D_MODEL, MOE_TF), F32),
            pltpu.VMEM((D_MODEL, MOE_TF), BF16),
            pltpu.VMEM((D_MODEL, MOE_TF), BF16),
            pltpu.SMEM((N_STATE,), jnp.int32),
            pltpu.SemaphoreType.DMA((2,)),
            pltpu.SemaphoreType.DMA((2,)),
            pltpu.SemaphoreType.DMA((2,)),
            pltpu.SemaphoreType.DMA((2, W_PIECES)),
        ],
    )
    b13 = b1.reshape(N_EXPERTS, 1, 2 * D_FF)
    return pl.pallas_call(
        _moe_up_kernel,
        grid_spec=grid_spec,
        out_shape=jax.ShapeDtypeStruct((MOE_NF, MOE_R, MOE_TF), BF16),
        compiler_params=pltpu.CompilerParams(
            dimension_semantics=("arbitrary", "arbitrary"), vmem_limit_bytes=MOE_VMEM_LIMIT),
        name="moe_up",
    )(seg_start, seg_rows, slack, xs, w1, b13, b13)


def _moe_down_kernel(seg_ref, rows_ref, slack_ref, act_ref, w_hbm, b_ref, y_ref,
                     xbuf, obuf, xtail, otail, wbuf, w_bf, state, isem, osem, tsem, wsem):
    e = pl.program_id(0)
    start = seg_ref[e]
    rows = rows_ref[e]
    e_next = lax.rem(e + 1, N_EXPERTS)
    wslot = lax.rem(e, 2)

    def weight_piece(expert, slot, p):
        band = D_FF // W_PIECES
        r0 = p * band if isinstance(p, int) else pl.multiple_of(p * band, band)
        return pltpu.make_async_copy(w_hbm.at[expert, pl.ds(r0, band)],
                                     wbuf.at[slot, pl.ds(r0, band)], wsem.at[slot, p])

    @pl.when(e == 0)
    def _():
        for p in range(W_PIECES):
            weight_piece(e, wslot, p).start()

    def make_in(r0, slot):
        return _CopyGroup([
            pltpu.make_async_copy(act_ref.at[j, _rows_at(r0, MOE_CH)],
                                  xbuf.at[slot, :, pl.ds(j * MOE_TF, MOE_TF)], isem.at[slot])
            for j in range(MOE_NF)])

    def make_out(r0, slot):
        return pltpu.make_async_copy(obuf.at[slot], y_ref.at[_rows_at(r0, MOE_CH)], osem.at[slot])

    def make_tail_in(r0):
        return _CopyGroup([
            pltpu.make_async_copy(act_ref.at[j, _rows_at(r0, SEG_ALIGN)],
                                  xtail.at[:, pl.ds(j * MOE_TF, MOE_TF)], tsem.at[0])
            for j in range(MOE_NF)])

    def make_tail_out(r0):
        return pltpu.make_async_copy(otail, y_ref.at[_rows_at(r0, SEG_ALIGN)], tsem.at[1])

    def cast_weights():
        band = D_FF // W_PIECES
        for p in range(W_PIECES):
            weight_piece(e, wslot, p).wait()
            rows_p = pl.ds(p * band, band)
            w_bf[rows_p, :] = wbuf[wslot, rows_p, :].astype(BF16)

    def expert_out(a):
        return _pack_bf16_pairs(jnp.dot(a, w_bf[...], preferred_element_type=F32) + b_ref[0])

    def compute_chunk(slot):
        obuf[slot] = expert_out(xbuf[slot])

    def compute_tail():
        otail[...] = expert_out(xtail[...])

    _stream_rows(e, N_EXPERTS, start, rows, seg_ref[e_next], rows_ref[e_next], state,
                 make_in, make_out, make_tail_in, make_tail_out,
                 compute_chunk, compute_tail, cast_weights,
                 lambda p: weight_piece(e_next, 1 - wslot, p))

    @pl.when(e == N_EXPERTS - 1)
    def _():
        _zero_slack(slack_ref, otail, lambda rr: y_ref.at[rr], tsem.at[1])


def moe_down(seg_start, seg_rows, slack, act, w2, b2):
    grid_spec = pltpu.PrefetchScalarGridSpec(
        num_scalar_prefetch=3,
        grid=(N_EXPERTS,),
        in_specs=[
            pl.BlockSpec(memory_space=pl.ANY),
            pl.BlockSpec(memory_space=pl.ANY),
            pl.BlockSpec((1, 1, D_MODEL), lambda e, s, r, z: (e, 0, 0)),
        ],
        out_specs=pl.BlockSpec(memory_space=pl.ANY),
        scratch_shapes=[
            pltpu.VMEM((2, MOE_CH, D_FF), BF16),
            pltpu.VMEM((2, MOE_CH, HALF), jnp.uint32),
            pltpu.VMEM((SEG_ALIGN, D_FF), BF16),
            pltpu.VMEM((SEG_ALIGN, HALF), jnp.uint32),
            pltpu.VMEM((2, D_FF, D_MODEL), F32),
            pltpu.VMEM((D_FF, D_MODEL), BF16),
            pltpu.SMEM((N_STATE,), jnp.int32),
            pltpu.SemaphoreType.DMA((2,)),
            pltpu.SemaphoreType.DMA((2,)),
            pltpu.SemaphoreType.DMA((2,)),
            pltpu.SemaphoreType.DMA((2, W_PIECES)),
        ],
    )
    return pl.pallas_call(
        _moe_down_kernel,
        grid_spec=grid_spec,
        out_shape=jax.ShapeDtypeStruct((MOE_R, HALF), jnp.uint32),
        compiler_params=pltpu.CompilerParams(
            dimension_semantics=("arbitrary",), vmem_limit_bytes=MOE_VMEM_LIMIT),
        name="moe_down",
    )(seg_start, seg_rows, slack, act, w2, b2.reshape(N_EXPERTS, 1, D_MODEL))


COMB_TM = 512
COMB_NT = N_TOK // COMB_TM


def _combine_kernel(dest_ref, y_ref, w_ref, h_ref, g_ref, o_ref, buf, sem):
    i = pl.program_id(0)

    def fetch(tile, slot):
        def group(g, carry):
            for r in range(8):
                a = (tile * COMB_TM + g * 8 + r) * TOP_K
                for kk in range(TOP_K):
                    pltpu.make_async_copy(y_ref.at[pl.ds(dest_ref[a + kk], 1)],
                                          buf.at[slot, kk, g, pl.ds(r, 1)], sem.at[slot]).start(
                                              priority=kk % 2)
            return carry

        lax.fori_loop(0, COMB_TM // 8, group, 0)

    @pl.when(i == 0)
    def _():
        fetch(0, 0)

    slot = lax.rem(i, 2)

    @pl.when(i + 1 < COMB_NT)
    def _():
        fetch(i + 1, 1 - slot)

    for kk in range(TOP_K):
        pltpu.make_async_copy(y_ref.at[pl.ds(0, COMB_TM)], y_ref.at[pl.ds(0, COMB_TM)],
                              sem.at[slot]).wait()
    w = w_ref[...]
    lo = h_ref[:, :HALF]
    hi = h_ref[:, HALF:]
    for kk in range(TOP_K):
        words = buf[slot, kk].reshape(COMB_TM, HALF)
        wk = w[:, kk:kk + 1]
        lo = lo + wk * lax.bitcast_convert_type(lax.shift_left(words, jnp.uint32(16)), F32)
        hi = hi + wk * lax.bitcast_convert_type(words & jnp.uint32(0xFFFF0000), F32)
    ms = (jnp.sum(lo * lo, axis=-1, keepdims=True)
          + jnp.sum(hi * hi, axis=-1, keepdims=True)) * (1.0 / D_MODEL)
    inv = lax.rsqrt(ms + EPS)
    o_ref[:, :HALF] = lo * inv * g_ref[:, :HALF]
    o_ref[:, HALF:] = hi * inv * g_ref[:, HALF:]


def combine(dest, y, weight, h2, gain):
    grid_spec = pltpu.PrefetchScalarGridSpec(
        num_scalar_prefetch=1,
        grid=(COMB_NT,),
        in_specs=[
            pl.BlockSpec(memory_space=pl.ANY),
            pl.BlockSpec((COMB_TM, LANES), lambda i, d: (i, 0)),
            pl.BlockSpec((COMB_TM, D_MODEL), lambda i, d: (i, 0)),
            pl.BlockSpec((1, D_MODEL), lambda i, d: (0, 0)),
        ],
        out_specs=pl.BlockSpec((COMB_TM, D_MODEL), lambda i, d: (i, 0)),
        scratch_shapes=[pltpu.VMEM((2, TOP_K, COMB_TM // 8, 8, HALF), jnp.uint32),
                        pltpu.SemaphoreType.DMA((2,))],
    )
    return pl.pallas_call(
        _combine_kernel,
        grid_spec=grid_spec,
        out_shape=jax.ShapeDtypeStruct((N_TOK, D_MODEL), F32),
        compiler_params=_cparams(1),
        name="moe_combine",
    )(dest, y, weight, h2, gain.reshape(1, D_MODEL))


def kernel(x, meta_tokens, rel_bias, norm_mix, w_in, conv_w, gate_bias_m, lambda_params, subln_da,
           w_branch_da, w_branch_m, w_gate, b_gate, w_out, norm_ffn, w_router, b_router,
           w1, b1, w2, b2, norm_final):
    layer = 0
    xn_pad, xn_real = norm_in(x, meta_tokens, norm_mix[layer])
    xn_pad2 = xn_pad.reshape(BATCH * LP, D_MODEL)
    w_in_t = w_in[layer].T
    proj = matmul(xn_pad2, w_in_t, n_cols=PROJ_COLS, tm=768, tn=1024, w_transposed=True,
                  name="proj_in")
    proj3 = proj.reshape(BATCH, LP, PROJ_COLS)
    w_g_t = jnp.pad(w_in_t[COL_M_G:], ((0, LANES - 4 * H_M), (0, 0)))
    mg = matmul(xn_pad2, w_g_t, n_cols=LANES, tm=768, tn=LANES, out_dtype=F32,
                w_transposed=True, name="proj_gates")
    gate = matmul(xn_real.reshape(N_TOK, D_MODEL), w_gate[layer], n_cols=2 * D_MODEL,
                  tm=1024, tn=1024, bias=b_gate[layer], act="sigmoid", name="mix_gate")

    tab, consts = _bias_tables(rel_bias)
    y_da = diff_attention(consts, proj3, tab, lambda_params[layer], subln_da[layer])

    qk_m = conv_qk(proj3, conv_w[layer])
    gp = gate_prep(mg.reshape(BATCH, LP, LANES), gate_bias_m[layer])
    gp4 = gp.reshape(BATCH, LP, 4, H_M)
    grow = jnp.transpose(gp4, (0, 3, 2, 1))
    swap = lambda t: jnp.swapaxes(t, 1, 2)
    y_m = swap(mlstm(swap(qk_m[:, :, :H_M * DK_M]), qk_m,
                     swap(proj3[:, :, COL_M_V:COL_M_V + H_M * DV_M]),
                     swap(proj3[:, :, COL_M_O:COL_M_O + H_M * DV_M]), grow))

    mixed = branch_mix(y_da.reshape(N_TOK, H_DA * DV_DA), y_m.reshape(N_TOK, H_M * DV_M),
                       w_branch_da[layer], w_branch_m[layer], gate)
    h2 = matmul(mixed, w_out[layer], n_cols=D_MODEL, tm=1024, tn=1024,
                res=x.reshape(N_TOK, D_MODEL), out_dtype=F32, name="out_proj")

    hn_packed, top_e, weight, rank, counts = ffn_prep(
        h2, norm_ffn[layer], w_router[layer], b_router[layer])
    dest, seg_start, seg_rows, pad0, padn, slack = _plan(
        counts, top_e[:, :TOP_K], rank[:, :TOP_K])
    xs = dispatch(dest, pad0, padn, slack, hn_packed)
    act = moe_up(seg_start, seg_rows, slack, xs, w1[layer], b1[layer])
    y = moe_down(seg_start, seg_rows, slack, act, w2[layer], b2[layer])
    out = combine(dest, y, weight, h2, norm_final)
    return out.reshape(BATCH, SEQ, D_MODEL)
```

```python
import functools
import math

import jax
import jax.numpy as jnp
from jax import lax
from jax.experimental import pallas as pl
from jax.experimental.pallas import tpu as pltpu

F32 = jnp.float32
BF16 = jnp.bfloat16

D_MODEL = 2048
BATCH = 2
SEQ = 4096
N_META = 16
BLOCK = 128
PAD = (-N_META) % BLOCK
LP = PAD + N_META + SEQ
NBLK = LP // BLOCK
EPS = 1e-6
NEG_INF = -1e30

H_DA = 4
DK_DA = 128
DV_DA = 256
H_M = 4
DK_M = 128
DV_M = 256
CONV_W = 5
N_BUCKETS = 32
MAX_DISTANCE = 128
N_EXPERTS = 32
TOP_K = 4
D_FF = 2048
SWIGLU_ALPHA = 1.702
SWIGLU_LIMIT = 7.0
LAMBDA_INIT = 0.8 - 0.6 * math.exp(-0.3 * 0)

COL_DA_Q = 0
COL_DA_K = 1024
COL_DA_V = 2048
COL_M_Q = 3072
COL_M_K = 3584
COL_M_V = 4096
COL_M_O = 5120
COL_M_G = 6144
PROJ_COLS = 6144

N_TOK = BATCH * SEQ
N_ASSIGN = N_TOK * TOP_K

LANES = 128
VMEM_LIMIT = 52 * 1024 * 1024
MOE_VMEM_LIMIT = 58 * 1024 * 1024


def _cparams(n_axes):
    return pltpu.CompilerParams(
        dimension_semantics=("arbitrary",) * n_axes, vmem_limit_bytes=VMEM_LIMIT)


def _rms(v, gain):
    ms = jnp.mean(v * v, axis=-1, keepdims=True)
    return v * lax.rsqrt(ms + EPS) * gain


def _sigmoid(v):
    return 0.5 * jnp.tanh(0.5 * v) + 0.5


def _norm_in_kernel(x_ref, meta_ref, g_ref, pad_ref, real_ref):
    j = pl.program_id(1)
    g = g_ref[...]

    @pl.when(j == 0)
    def _():
        pad_ref[0, :PAD, :] = jnp.zeros((PAD, D_MODEL), BF16)
        pad_ref[0, PAD:, :] = _rms(meta_ref[...], g).astype(BF16)

    @pl.when(j > 0)
    def _():
        y = _rms(x_ref[0], g).astype(BF16)
        pad_ref[0] = y
        real_ref[0] = y


def norm_in(x, meta, gain):
    return pl.pallas_call(
        _norm_in_kernel,
        grid=(BATCH, NBLK),
        in_specs=[
            pl.BlockSpec((1, BLOCK, D_MODEL), lambda b, j: (b, jnp.maximum(j - 1, 0), 0)),
            pl.BlockSpec((N_META, D_MODEL), lambda b, j: (0, 0)),
            pl.BlockSpec((1, D_MODEL), lambda b, j: (0, 0)),
        ],
        out_specs=[
            pl.BlockSpec((1, BLOCK, D_MODEL), lambda b, j: (b, j, 0)),
            pl.BlockSpec((1, BLOCK, D_MODEL), lambda b, j: (b, jnp.maximum(j - 1, 0), 0)),
        ],
        out_shape=[
            jax.ShapeDtypeStruct((BATCH, LP, D_MODEL), BF16),
            jax.ShapeDtypeStruct((BATCH, SEQ, D_MODEL), BF16),
        ],
        compiler_params=_cparams(2),
        name="norm_in",
    )(x, meta, gain.reshape(1, D_MODEL))


def _mm_kernel(*refs, has_bias, has_res, act, w_transposed):
    x_ref, w_ref = refs[0], refs[1]
    pos = 2
    b_ref = r_ref = None
    if has_bias:
        b_ref = refs[pos]
        pos += 1
    if has_res:
        r_ref = refs[pos]
        pos += 1
    o_ref, wbf_ref = refs[pos], refs[pos + 1]

    @pl.when(pl.program_id(1) == 0)
    def _():
        w = w_ref[...]
        wbf_ref[...] = (w.T if w_transposed else w).astype(BF16)

    acc = jnp.dot(x_ref[...], wbf_ref[...], preferred_element_type=F32)
    if has_bias:
        acc = acc + b_ref[...]
    if act == "sigmoid":
        acc = _sigmoid(acc)
    if has_res:
        acc = acc + r_ref[...]
    o_ref[...] = acc.astype(o_ref.dtype)


def matmul(x, w, *, n_cols, col_block0=0, tm, tn, bias=None, res=None, act=None,
           out_dtype=BF16, w_transposed=False, name):
    m, k = x.shape
    in_specs = [
        pl.BlockSpec((tm, k), lambda j, i: (i, 0)),
        pl.BlockSpec((tn, k), lambda j, i: (j + col_block0, 0)) if w_transposed
        else pl.BlockSpec((k, tn), lambda j, i: (0, j + col_block0)),
    ]
    args = [x, w]
    if bias is not None:
        in_specs.append(pl.BlockSpec((1, tn), lambda j, i: (0, j)))
        args.append(bias.reshape(1, n_cols))
    if res is not None:
        in_specs.append(pl.BlockSpec((tm, tn), lambda j, i: (i, j)))
        args.append(res)
    return pl.pallas_call(
        functools.partial(_mm_kernel, has_bias=bias is not None, has_res=res is not None, act=act,
                          w_transposed=w_transposed),
        grid=(n_cols // tn, m // tm),
        in_specs=in_specs,
        out_specs=pl.BlockSpec((tm, tn), lambda j, i: (i, j)),
        out_shape=jax.ShapeDtypeStruct((m, n_cols), out_dtype),
        scratch_shapes=[pltpu.VMEM((k, tn), BF16)],
        compiler_params=_cparams(2),
        name=name,
    )(*args)


def _conv_kernel(p_ref, w_ref, o_ref):
    c = pl.program_id(1)
    x = p_ref[0].astype(F32)
    w = w_ref[...]
    half = CONV_W // 2
    acc = w[half:half + 1, :] * x
    for j in range(CONV_W):
        if j != half:
            acc = acc + w[j:j + 1, :] * pltpu.roll(x, (half - j) % LP, axis=0)
    y = acc * jax.nn.sigmoid(acc)
    rows = lax.broadcasted_iota(jnp.int32, (LP, 1), 0)
    y = jnp.where(rows >= PAD, y, 0.0)
    scale = jnp.where(c < 2, DK_M ** -0.5, 1.0).astype(F32)
    o_ref[0] = (y * scale).astype(BF16)


def conv_qk(proj3, conv_w):
    cw = 256
    return pl.pallas_call(
        _conv_kernel,
        grid=(BATCH, (2 * H_M * DK_M) // cw),
        in_specs=[
            pl.BlockSpec((1, LP, cw), lambda b, c: (b, 0, COL_M_Q // cw + c)),
            pl.BlockSpec((CONV_W, cw), lambda b, c: (0, c)),
        ],
        out_specs=pl.BlockSpec((1, LP, cw), lambda b, c: (b, 0, c)),
        out_shape=jax.ShapeDtypeStruct((BATCH, LP, 2 * H_M * DK_M), BF16),
        compiler_params=_cparams(2),
        name="conv_qk",
    )(proj3, conv_w)


def _split_dot(tri, v):
    hi = v.astype(BF16)
    r1 = v - hi.astype(F32)
    mid = r1.astype(BF16)
    lo = (r1 - mid.astype(F32)).astype(BF16)
    return (jnp.dot(tri, hi, preferred_element_type=F32)
            + jnp.dot(tri, mid, preferred_element_type=F32)
            + jnp.dot(tri, lo, preferred_element_type=F32))


def _gate_kernel(g_ref, bias_ref, o_ref):
    ti = lax.broadcasted_iota(jnp.int32, (BLOCK, BLOCK), 0)
    ui = lax.broadcasted_iota(jnp.int32, (BLOCK, BLOCK), 1)
    tril = jnp.where(ui <= ti, 1.0, 0.0).astype(BF16)
    triu = jnp.where(ui >= ti, 1.0, 0.0).astype(BF16)
    ch = lax.broadcasted_iota(jnp.int32, (BLOCK, 4 * H_M), 1)
    typ = lax.shift_right_logical(ch, 2)
    rloc = lax.broadcasted_iota(jnp.int32, (BLOCK, 4 * H_M), 0)

    def body(c, carry):
        r0 = pl.multiple_of(c * BLOCK, BLOCK)
        g = g_ref[0, pl.ds(r0, BLOCK), :][:, :4 * H_M] + bias_ref[...]
        valid = (rloc + r0) >= PAD
        lsig = -(jnp.maximum(-g, 0.0) + jnp.log1p(jnp.exp(-jnp.abs(g))))
        lf = jnp.where(valid, lsig, 0.0)
        cum = _split_dot(tril, lf)
        rcum = _split_dot(triu, lf)
        li = jnp.where(valid, g, -jnp.inf)
        out = jnp.where(typ == 1, cum, jnp.where(typ == 3, rcum, li))
        o_ref[0, pl.ds(r0, BLOCK), :] = out
        return carry

    lax.fori_loop(0, NBLK, body, 0)


def gate_prep(mg3, gate_bias):
    return pl.pallas_call(
        _gate_kernel,
        grid=(BATCH,),
        in_specs=[
            pl.BlockSpec((1, LP, LANES), lambda b: (b, 0, 0)),
            pl.BlockSpec((1, 4 * H_M), lambda b: (0, 0)),
        ],
        out_specs=pl.BlockSpec((1, LP, 4 * H_M), lambda b: (b, 0, 0)),
        out_shape=jax.ShapeDtypeStruct((BATCH, LP, 4 * H_M), F32),
        compiler_params=_cparams(1),
        name="gate_prep",
    )(mg3, gate_bias.reshape(1, 4 * H_M))


MLSTM_HP = 2
MLSTM_MID = NBLK // 2


def _mlstm_kernel(qt_ref, k_ref, vt_ref, ot_ref, gr_ref, y_ref,
                  hs_ref, c_ref, n_ref, m_ref):
    c_ref[...] = jnp.zeros_like(c_ref)
    n_ref[...] = jnp.zeros_like(n_ref)
    m_ref[...] = jnp.zeros_like(m_ref)
    si = lax.broadcasted_iota(jnp.int32, (BLOCK, BLOCK), 0)
    ti = lax.broadcasted_iota(jnp.int32, (BLOCK, BLOCK), 1)
    mask_f = si <= ti
    mask_b = si >= ti

    def chain(c, hl, bwd, final):
        idx = 2 * hl + bwd
        r0 = c * BLOCK if isinstance(c, int) else pl.multiple_of(c * BLOCK, BLOCK)
        t_sl = pl.ds(r0, BLOCK)
        qt = qt_ref[0, hl * DK_M:(hl + 1) * DK_M, t_sl]
        k = k_ref[0, t_sl, hl * DK_M:(hl + 1) * DK_M]
        feat = slice(hl * DV_M, (hl + 1) * DV_M)
        vt = vt_ref[0, feat, t_sl]
        gr = gr_ref[0, hl, :, t_sl]
        li_r, b_r = gr[2 * bwd:2 * bwd + 1, :], gr[2 * bwd + 1:2 * bwd + 2, :]
        a_c = jnp.transpose(jnp.broadcast_to(li_r - b_r, (BLOCK, BLOCK)))
        b_end = b_r[:, 0:1] if bwd else b_r[:, BLOCK - 1:BLOCK]
        m_prev = m_ref[idx][:, 0:1]
        ct = c_ref[idx]
        nst = n_ref[idx]
        dmat = jnp.where(mask_b if bwd else mask_f, a_c + b_r, -jnp.inf)
        inter = b_r + m_prev
        m_t = jnp.maximum(inter, jnp.max(dmat, axis=0, keepdims=True))
        w_inter = jnp.exp(inter - m_t)
        st = jnp.dot(k, qt, preferred_element_type=F32) * jnp.exp(dmat - m_t)
        num = (w_inter * jnp.dot(ct.astype(BF16), qt, preferred_element_type=F32)
               + jnp.dot(vt, st.astype(BF16), preferred_element_type=F32))
        nq = jnp.dot(nst.astype(BF16), qt, preferred_element_type=F32)
        den = w_inter * nq + jnp.sum(st, axis=0, keepdims=True)
        h = num * (1.0 / jnp.maximum(jnp.abs(den), jnp.exp(-m_t)))
        if final:
            og = jax.nn.sigmoid(ot_ref[0, feat, t_sl].astype(F32))
            y_ref[0, feat, pl.ds(r0 - BLOCK, BLOCK)] = (og * (hs_ref[feat, t_sl] + h)).astype(BF16)
        else:
            hs_ref[feat, t_sl] = h
        ldec = b_end - b_r + li_r
        m_new = jnp.maximum(b_end + m_prev, jnp.max(ldec, axis=1, keepdims=True))
        w_c = jnp.exp(b_end + m_prev - m_new)
        w_s = jnp.exp(ldec - m_new)
        wvt = (vt.astype(F32) * w_s).astype(BF16)
        c_ref[idx] = w_c * ct + jnp.dot(wvt, k, preferred_element_type=F32)
        n_ref[idx] = w_c * nst + jnp.dot(w_s.astype(BF16), k, preferred_element_type=F32)
        m_ref[idx] = jnp.broadcast_to(m_new, (1, BLOCK))

    def first_half(i, carry):
        for hl in range(MLSTM_HP):
            chain(i, hl, 0, False)
            chain(NBLK - 1 - i, hl, 1, False)
        return carry

    def second_half(i, carry):
        for hl in range(MLSTM_HP):
            chain(i, hl, 0, True)
            chain(NBLK - 1 - i, hl, 1, True)
        return carry

    lax.fori_loop(0, MLSTM_MID, first_half, 0, unroll=8)
    for hl in range(MLSTM_HP):
        chain(MLSTM_MID, hl, 0, False)
        chain(MLSTM_MID, hl, 1, True)
    lax.fori_loop(MLSTM_MID + 1, NBLK - 1, second_half, 0, unroll=5)
    for hl in range(MLSTM_HP):
        chain(NBLK - 1, hl, 0, True)


def mlstm(q_t, qk_m, v_t, o_t, grow):
    hp = MLSTM_HP
    kw, vw = hp * DK_M, hp * DV_M
    return pl.pallas_call(
        _mlstm_kernel,
        grid=(BATCH, H_M // hp),
        in_specs=[
            pl.BlockSpec((1, kw, LP), lambda b, g: (b, g, 0)),
            pl.BlockSpec((1, LP, kw), lambda b, g: (b, 0, (H_M * DK_M) // kw + g)),
            pl.BlockSpec((1, vw, LP), lambda b, g: (b, g, 0)),
            pl.BlockSpec((1, vw, LP), lambda b, g: (b, g, 0)),
            pl.BlockSpec((1, hp, 4, LP), lambda b, g: (b, g, 0, 0)),
        ],
        out_specs=pl.BlockSpec((1, vw, SEQ), lambda b, g: (b, g, 0)),
        out_shape=jax.ShapeDtypeStruct((BATCH, H_M * DV_M, SEQ), BF16),
        scratch_shapes=[
            pltpu.VMEM((vw, LP), F32),
            pltpu.VMEM((2 * hp, DV_M, DK_M), F32),
            pltpu.VMEM((2 * hp, 1, DK_M), F32),
            pltpu.VMEM((2 * hp, 1, BLOCK), F32),
        ],
        compiler_params=_cparams(2),
        name="mlstm",
    )(q_t, qk_m, v_t, o_t, grow)


LOG2E = 1.4426950408889634
ATT_QB = 2
ATT_TQ = ATT_QB * BLOCK
ATT_BAND = (ATT_QB + 2) * BLOCK
ATT_GROUPS = (6, 6, 6, 6, 5)
assert BLOCK >= MAX_DISTANCE and ATT_BAND + sum(ATT_GROUPS) * BLOCK == LP


ATT_NQ = SEQ // ATT_TQ


def _attn_kernel(c_ref, *refs):
    q_refs = refs[:ATT_QB]
    (k1_ref, v1_ref, tab_ref, lam_ref, sg_ref, o_ref,
     s_a, s_b, mx_a, mx_b, k_ref, v_ref) = refs[ATT_QB:]
    h = pl.program_id(1)
    t = pl.program_id(2)

    @pl.when(t == 0)
    def _():
        for rep in range(2):
            k_ref[0, rep * LP:(rep + 1) * LP, :] = k1_ref[0]
            v_ref[0, rep * LP:(rep + 1) * LP, :] = v1_ref[0]

    scale = DK_DA ** -0.5 * LOG2E
    c_neg = c_ref[h, 0]
    c_pos = c_ref[h, 1]

    def groups_of(tile):
        koff = (ATT_QB * tile) * BLOCK
        out = [(koff, ATT_BAND, 0)]
        col = ATT_BAND
        for nblk in ATT_GROUPS:
            out.append((koff + col, nblk * BLOCK, col))
            col += nblk * BLOCK
        return out

    def lane_fold(acc, x, op):
        for j in range(x.shape[1] // BLOCK):
            piece = x[:, j * BLOCK:(j + 1) * BLOCK]
            acc = piece if acc is None else op(acc, piece)
        return acc

    def score(tile, s_ref, mx_ref):
        q = jnp.concatenate([r[0] for r in q_refs], axis=0)
        mx = [None, None]
        for gi, (koff, width, col0) in enumerate(groups_of(tile)):
            koff = pl.multiple_of(koff, BLOCK)
            if gi == 0:
                bias = tab_ref[0, 0]
            else:
                kpos = koff + lax.broadcasted_iota(jnp.int32, (1, width), 1)
                bias = jnp.where(kpos < LP, c_pos, jnp.where(kpos < LP + PAD, NEG_INF, c_neg))
            for m in range(2):
                kk = k_ref[0, pl.ds(koff, width), m * DK_DA:(m + 1) * DK_DA]
                s = lax.dot_general(q[:, m * DK_DA:(m + 1) * DK_DA], kk,
                                    (((1,), (1,)), ((), ())),
                                    preferred_element_type=F32) * scale + bias
                s_ref[m, :, col0:col0 + width] = s
                mx[m] = lane_fold(mx[m], s, jnp.maximum)
        for m in range(2):
            mx_ref[m] = mx[m]

    def finish(tile, s_ref, mx_ref):
        lp = lam_ref[...]
        lam = (jnp.exp(jnp.sum(lp[0:1] * lp[1:2], axis=1, keepdims=True))
               - jnp.exp(jnp.sum(lp[2:3] * lp[3:4], axis=1, keepdims=True)) + LAMBDA_INIT)
        row_max = [jnp.max(mx_ref[m], axis=1, keepdims=True) for m in range(2)]
        lsum = [None, None]
        acc = [None, None]
        for koff, width, col0 in groups_of(tile):
            koff = pl.multiple_of(koff, BLOCK)
            vv = v_ref[0, pl.ds(koff, width), :]
            for m in range(2):
                p = jnp.exp2(s_ref[m, :, col0:col0 + width] - row_max[m])
                lsum[m] = lane_fold(lsum[m], p, jnp.add)
                pv = jnp.dot(p.astype(BF16), vv, preferred_element_type=F32)
                acc[m] = pv if acc[m] is None else acc[m] + pv
        l1 = jnp.sum(lsum[0], axis=1, keepdims=True)
        l2 = jnp.sum(lsum[1], axis=1, keepdims=True)
        o = acc[0] / l1 - lam * (acc[1] / l2)
        o_ref[0] = (_rms(o, sg_ref[...]) * (1.0 - LAMBDA_INIT)).astype(BF16)

    even = lax.rem(t, 2) == 0
    inner = jnp.logical_and(t > 0, t < ATT_NQ)

    @pl.when(t == 0)
    def _():
        score(t, s_a, mx_a)

    @pl.when(jnp.logical_and(inner, even))
    def _():
        score(t, s_a, mx_a)
        finish(t - 1, s_b, mx_b)

    @pl.when(jnp.logical_and(inner, jnp.logical_not(even)))
    def _():
        score(t, s_b, mx_b)
        finish(t - 1, s_a, mx_a)

    @pl.when(t == ATT_NQ)
    def _():
        if (ATT_NQ - 1) % 2 == 0:
            finish(t - 1, s_a, mx_a)
        else:
            finish(t - 1, s_b, mx_b)


def diff_attention(consts, proj3, tab, lam_params, subln):
    nq = ATT_NQ
    kblk0 = COL_DA_K // (2 * DK_DA)
    vblk0 = COL_DA_V // DV_DA
    scored = lambda i: jnp.minimum(i, nq - 1)
    finished = lambda i: jnp.maximum(i - 1, 0)

    def tab_map(b, h, i):
        tile = scored(i)
        case = jnp.where(tile == 0, 0, jnp.where(tile == nq - 1, 2, 1))
        return (h, case, 0, 0)

    return pl.pallas_call(
        _attn_kernel,
        grid=(BATCH, H_DA, nq + 1),
        in_specs=[
            pl.BlockSpec(memory_space=pltpu.SMEM),
            *[pl.BlockSpec((1, BLOCK, 2 * DK_DA),
                           functools.partial(
                               lambda b, h, i, r: (b, ATT_QB * scored(i) + 1 + r, h), r=r))
              for r in range(ATT_QB)],
            pl.BlockSpec((1, LP, 2 * DK_DA), lambda b, h, i: (b, 0, kblk0 + h)),
            pl.BlockSpec((1, LP, DV_DA), lambda b, h, i: (b, 0, vblk0 + h)),
            pl.BlockSpec((1, 1, ATT_TQ, ATT_BAND), tab_map),
            pl.BlockSpec((4, DK_DA), lambda b, h, i: (0, 0)),
            pl.BlockSpec((1, DV_DA), lambda b, h, i: (0, 0)),
        ],
        out_specs=pl.BlockSpec((1, ATT_TQ, DV_DA), lambda b, h, i: (b, finished(i), h)),
        out_shape=jax.ShapeDtypeStruct((BATCH, SEQ, H_DA * DV_DA), BF16),
        scratch_shapes=[pltpu.VMEM((2, ATT_TQ, LP), F32),
                        pltpu.VMEM((2, ATT_TQ, LP), F32),
                        pltpu.VMEM((2, ATT_TQ, BLOCK), F32),
                        pltpu.VMEM((2, ATT_TQ, BLOCK), F32),
                        pltpu.VMEM((1, 2 * LP, 2 * DK_DA), BF16),
                        pltpu.VMEM((1, 2 * LP, DV_DA), BF16)],
        compiler_params=_cparams(3),
        name="diff_attn",
    )(consts, *([proj3] * (ATT_QB + 2)), tab, lam_params, subln.reshape(1, DV_DA))


def _bias_tables(rel_bias):
    rb = rel_bias.astype(F32)
    span = 1024
    assert span >= ATT_TQ + ATT_BAND
    rel = jnp.arange(span, dtype=jnp.int32) - span // 2
    nb = N_BUCKETS // 2
    max_exact = nb // 2
    n = jnp.abs(rel)
    nf = jnp.maximum(n, 1).astype(F32)
    large = max_exact + (jnp.log(nf / max_exact) / math.log(MAX_DISTANCE / max_exact)
                         * (nb - max_exact)).astype(jnp.int32)
    large = jnp.minimum(large, nb - 1)
    bucket = jnp.where(rel > 0, nb, 0) + jnp.where(n < max_exact, n, large)
    hit = bucket[None, :, None] == jnp.arange(N_BUCKETS, dtype=jnp.int32)
    by_rel = jnp.sum(jnp.where(hit, rb.T[:, None, :], 0.0), axis=-1)
    shifted = jnp.tile(by_rel, (1, ATT_TQ))[:, :ATT_TQ * (span - 1)].reshape(
        H_DA, ATT_TQ, span - 1)
    c0 = span // 2 - BLOCK
    gen = shifted[:, :, c0:c0 + ATT_BAND]
    c_neg = rb[nb - 1]
    c_pos = rb[N_BUCKETS - 1]
    jj = jnp.arange(ATT_BAND, dtype=jnp.int32)[None, None, :]
    first = jnp.where(jj < PAD, NEG_INF, gen)
    wrap0 = ATT_BAND - BLOCK
    wrapped = jnp.where(jj - wrap0 < PAD, NEG_INF, c_neg[:, None, None])
    last = jnp.where(jj >= wrap0, wrapped, gen)
    tab = jnp.stack([first, gen, last], axis=1)
    consts = jnp.stack([c_neg, c_pos], axis=1)
    return tab * LOG2E, consts * LOG2E


def _mix_kernel(ya_ref, ym_ref, wa_ref, wm_ref, ga_ref, gm_ref, o_ref, wa_bf, wm_bf):
    @pl.when(pl.program_id(1) == 0)
    def _():
        wa_bf[...] = wa_ref[...].astype(BF16)
        wm_bf[...] = wm_ref[...].astype(BF16)

    a = jnp.dot(ya_ref[...], wa_bf[...], preferred_element_type=F32)
    m = jnp.dot(ym_ref[...], wm_bf[...], preferred_element_type=F32)
    o_ref[...] = (ga_ref[...].astype(F32) * a + gm_ref[...].astype(F32) * m).astype(BF16)


def branch_mix(y_da, y_m, w_da, w_m, gate, *, tm=512, tn=1024):
    m, k = y_da.shape
    nj = D_MODEL // tn
    return pl.pallas_call(
        _mix_kernel,
        grid=(nj, m // tm),
        in_specs=[
            pl.BlockSpec((tm, k), lambda j, i: (i, 0)),
            pl.BlockSpec((tm, k), lambda j, i: (i, 0)),
            pl.BlockSpec((k, tn), lambda j, i: (0, j)),
            pl.BlockSpec((k, tn), lambda j, i: (0, j)),
            pl.BlockSpec((tm, tn), lambda j, i: (i, j)),
            pl.BlockSpec((tm, tn), lambda j, i: (i, nj + j)),
        ],
        out_specs=pl.BlockSpec((tm, tn), lambda j, i: (i, j)),
        out_shape=jax.ShapeDtypeStruct((m, D_MODEL), BF16),
        scratch_shapes=[pltpu.VMEM((k, tn), BF16), pltpu.VMEM((k, tn), BF16)],
        compiler_params=_cparams(2),
        name="branch_mix",
    )(y_da, y_m, w_da, w_m, gate, gate)


FFN_TM = 256
HALF = D_MODEL // 2


def _pack_bf16_pairs(v):
    lo = lax.bitcast_convert_type(v[:, :HALF].astype(BF16).astype(F32), jnp.uint32)
    hi = lax.bitcast_convert_type(v[:, HALF:].astype(BF16).astype(F32), jnp.uint32)
    return (hi & jnp.uint32(0xFFFF0000)) | lax.shift_right_logical(lo, jnp.uint32(16))


def _unpack_bf16_pairs(w):
    lo = lax.bitcast_convert_type(lax.shift_left(w, jnp.uint32(16)), F32).astype(BF16)
    hi = lax.bitcast_convert_type(w & jnp.uint32(0xFFFF0000), F32).astype(BF16)
    return lo, hi


def _ffn_prep_kernel(x_ref, wo_ref, res_ref, g_ref, wr_ref, br_ref,
                     h_ref, hn_ref, e_ref, w_ref, r_ref, cnt_ref, wbf_ref, base_ref):
    @pl.when(pl.program_id(0) == 0)
    def _():
        base_ref[...] = jnp.zeros_like(base_ref)
        wbf_ref[...] = wo_ref[...].astype(BF16)

    h = jnp.dot(x_ref[...], wbf_ref[...], preferred_element_type=F32) + res_ref[...]
    h_ref[...] = h
    hn = _rms(h, g_ref[...])
    hn_ref[...] = _pack_bf16_pairs(hn)
    logits = jnp.dot(hn.astype(BF16), wr_ref[...].astype(BF16),
                     preferred_element_type=F32) + br_ref[...]
    lane = lax.broadcasted_iota(jnp.int32, (FFN_TM, N_EXPERTS), 1)
    lane_o = lax.broadcasted_iota(jnp.int32, (FFN_TM, LANES), 1)
    ti = lax.broadcasted_iota(jnp.int32, (FFN_TM, FFN_TM), 0)
    ui = lax.broadcasted_iota(jnp.int32, (FFN_TM, FFN_TM), 1)
    tril = jnp.where(ui <= ti, 1.0, 0.0).astype(BF16)
    e_out = jnp.zeros((FFN_TM, LANES), jnp.int32)
    r_out = jnp.zeros((FFN_TM, LANES), jnp.int32)
    l_out = jnp.full((FFN_TM, LANES), -jnp.inf, F32)
    base = base_ref[...]
    l = logits
    for kk in range(TOP_K):
        mk = jnp.max(l, axis=1, keepdims=True)
        ik = jnp.min(jnp.where(l == mk, lane, N_EXPERTS), axis=1, keepdims=True)
        hit = lane == ik
        oh = jnp.where(hit, 1.0, 0.0)
        cum = jnp.dot(tril, oh.astype(BF16), preferred_element_type=F32)
        rank = jnp.sum(oh * (cum + base), axis=1, keepdims=True) - 1.0
        base = base + jnp.sum(oh, axis=0, keepdims=True)
        e_out = jnp.where(lane_o == kk, ik, e_out)
        r_out = jnp.where(lane_o == kk, rank.astype(jnp.int32), r_out)
        l_out = jnp.where(lane_o == kk, mk, l_out)
        l = jnp.where(hit, -jnp.inf, l)
    base_ref[...] = base
    cnt_ref[...] = base
    ex = jnp.exp(l_out - jnp.max(l_out, axis=1, keepdims=True))
    e_ref[...] = e_out
    r_ref[...] = r_out
    w_ref[...] = ex / jnp.sum(ex, axis=1, keepdims=True)


def ffn_prep(mixed, w_out, res, gain, w_router, b_router):
    row = lambda i: (i, 0)
    fixed = lambda i: (0, 0)
    return pl.pallas_call(
        _ffn_prep_kernel,
        grid=(N_TOK // FFN_TM,),
        in_specs=[
            pl.BlockSpec((FFN_TM, D_MODEL), row),
            pl.BlockSpec((D_MODEL, D_MODEL), fixed, pipeline_mode=pl.Buffered(1)),
            pl.BlockSpec((FFN_TM, D_MODEL), row),
            pl.BlockSpec((1, D_MODEL), fixed),
            pl.BlockSpec((D_MODEL, N_EXPERTS), fixed),
            pl.BlockSpec((1, N_EXPERTS), fixed),
        ],
        out_specs=[
            pl.BlockSpec((FFN_TM, D_MODEL), row),
            pl.BlockSpec((FFN_TM, HALF), row),
            pl.BlockSpec((FFN_TM, LANES), row),
            pl.BlockSpec((FFN_TM, LANES), row),
            pl.BlockSpec((FFN_TM, LANES), row),
            pl.BlockSpec((1, N_EXPERTS), fixed),
        ],
        out_shape=[
            jax.ShapeDtypeStruct((N_TOK, D_MODEL), F32),
            jax.ShapeDtypeStruct((N_TOK, HALF), jnp.uint32),
            jax.ShapeDtypeStruct((N_TOK, LANES), jnp.int32),
            jax.ShapeDtypeStruct((N_TOK, LANES), F32),
            jax.ShapeDtypeStruct((N_TOK, LANES), jnp.int32),
            jax.ShapeDtypeStruct((1, N_EXPERTS), F32),
        ],
        scratch_shapes=[pltpu.VMEM((D_MODEL, D_MODEL), BF16), pltpu.VMEM((1, N_EXPERTS), F32)],
        compiler_params=_cparams(1),
        name="ffn_prep",
    )(mixed, w_out, res, gain.reshape(1, D_MODEL), w_router, b_router.reshape(1, N_EXPERTS))


SEG_ALIGN = 128
MOE_R = N_ASSIGN + N_EXPERTS * SEG_ALIGN


def _plan(counts_f, top_e, rank):
    counts = counts_f[0].astype(jnp.int32)
    seg_rows = (counts + SEG_ALIGN - 1) // SEG_ALIGN * SEG_ALIGN
    seg_start = jnp.cumsum(seg_rows) - seg_rows
    eq = top_e[:, :, None] == jnp.arange(N_EXPERTS, dtype=jnp.int32)[None, None, :]
    dest = jnp.sum(jnp.where(eq, seg_start[None, None, :], 0), axis=-1) + rank
    used = jnp.sum(seg_rows)
    slack = jnp.stack([used, (MOE_R - used) // SEG_ALIGN])
    return (dest.reshape(N_ASSIGN).astype(jnp.int32), seg_start.astype(jnp.int32),
            seg_rows.astype(jnp.int32), (seg_start + counts).astype(jnp.int32),
            (seg_rows - counts).astype(jnp.int32), slack.astype(jnp.int32))


def _zero_slack(slack_ref, zero_block, dst_rows, sem):
    zero_block[...] = jnp.zeros_like(zero_block)

    def copy(j):
        r0 = pl.multiple_of(slack_ref[0] + j * SEG_ALIGN, SEG_ALIGN)
        return pltpu.make_async_copy(zero_block, dst_rows(pl.ds(r0, SEG_ALIGN)), sem)

    def start(j, carry):
        copy(j).start()
        return carry

    def wait(j, carry):
        copy(j).wait()
        return carry

    lax.fori_loop(0, slack_ref[1], start, 0)
    lax.fori_loop(0, slack_ref[1], wait, 0)


DISP_TOK = 512


def _dispatch_kernel(dest_ref, pad0_ref, padn_ref, slack_ref, hn_ref, xs_ref,
                     zrow_ref, zblk_ref, sem, zsem):
    i = pl.program_id(0)

    @pl.when(i == 0)
    def _():
        zrow_ref[...] = jnp.zeros_like(zrow_ref)
        _zero_slack(slack_ref, zblk_ref, lambda rows: xs_ref.at[rows], zsem.at[0])

        def expert(e, carry):
            p0 = pad0_ref[e]
            pn = padn_ref[e]

            def zstart(r, c2):
                pltpu.make_async_copy(zrow_ref, xs_ref.at[pl.ds(p0 + r, 1)], zsem.at[0]).start()
                return c2

            def zwait(r, c2):
                pltpu.make_async_copy(zrow_ref, xs_ref.at[pl.ds(p0, 1)], zsem.at[0]).wait()
                return c2

            lax.fori_loop(0, pn, zstart, 0)
            lax.fori_loop(0, pn, zwait, 0)
            return carry

        lax.fori_loop(0, N_EXPERTS, expert, 0)

    def group(g, carry):
        for r in range(8):
            a = (i * DISP_TOK + g * 8 + r) * TOP_K
            for kk in range(TOP_K):
                pltpu.make_async_copy(hn_ref.at[g, pl.ds(r, 1)],
                                      xs_ref.at[pl.ds(dest_ref[a + kk], 1)],
                                      sem.at[0]).start(priority=kk % 2)
        return carry

    lax.fori_loop(0, DISP_TOK // 8, group, 0)
    for kk in range(TOP_K):
        pltpu.make_async_copy(xs_ref.at[pl.ds(0, DISP_TOK)], xs_ref.at[pl.ds(0, DISP_TOK)],
                              sem.at[0]).wait()


def dispatch(dest, pad0, padn, slack, hn_packed):
    grid_spec = pltpu.PrefetchScalarGridSpec(
        num_scalar_prefetch=4,
        grid=(N_TOK // DISP_TOK,),
        in_specs=[pl.BlockSpec((DISP_TOK // 8, 8, HALF), lambda i, d, p0, pn, z: (i, 0, 0))],
        out_specs=pl.BlockSpec(memory_space=pl.ANY),
        scratch_shapes=[pltpu.VMEM((1, HALF), jnp.uint32),
                        pltpu.VMEM((SEG_ALIGN, HALF), jnp.uint32),
                        pltpu.SemaphoreType.DMA((1,)), pltpu.SemaphoreType.DMA((1,))],
    )
    return pl.pallas_call(
        _dispatch_kernel,
        grid_spec=grid_spec,
        out_shape=jax.ShapeDtypeStruct((MOE_R, HALF), jnp.uint32),
        compiler_params=_cparams(1),
        name="moe_dispatch",
    )(dest, pad0, padn, slack, hn_packed.reshape(N_TOK // 8, 8, HALF))


MOE_CH = 2 * SEG_ALIGN
MOE_TF = 1024
MOE_NF = D_FF // MOE_TF


class _CopyGroup:
    def __init__(self, copies):
        self.copies = copies

    def start(self, priority=0):
        for cp in self.copies:
            cp.start(priority=priority)

    def wait(self):
        for cp in self.copies:
            cp.wait()


W_PIECES = 8
W_PER_CHUNK = 3


N_STATE = 4


def _stream_rows(step, n_steps, start, rows, next_start, next_rows, state,
                 make_in, make_out, make_tail_in, make_tail_out,
                 compute_chunk, compute_tail, before_first_wait, next_weight_piece):
    has_next = step + 1 < n_steps

    def request_weights(first, count):
        def one(p, carry):
            @pl.when(jnp.logical_and(has_next, p < W_PIECES))
            def _():
                next_weight_piece(p).start()
            return carry

        lax.fori_loop(first, first + count, one, 0)

    @pl.when(step == 0)
    def _():
        for j in range(N_STATE):
            state[j] = 0

    n_ch = lax.shift_right_logical(rows, MOE_CH.bit_length() - 1)
    tail = rows - n_ch * MOE_CH
    tail_row = start + n_ch * MOE_CH
    g0 = state[0]
    feeds_next = jnp.logical_and(step + 1 < n_steps, next_rows >= MOE_CH)

    def chunk_row(c):
        return start + c * MOE_CH

    @pl.when(tail > 0)
    def _():
        make_tail_in(tail_row).start(priority=1)

    @pl.when(jnp.logical_and(n_ch > 0, state[3] == 0))
    def _():
        make_in(start, lax.rem(g0, 2)).start(priority=1)

    before_first_wait()

    @pl.when(tail > 0)
    def _():
        make_tail_in(tail_row).wait()
        compute_tail()
        make_tail_out(tail_row).start(priority=1)

    def body(c, carry):
        slot = lax.rem(g0 + c, 2)
        make_in(chunk_row(c), slot).wait()

        @pl.when(c + 1 < n_ch)
        def _():
            make_in(chunk_row(c + 1), 1 - slot).start(priority=1)

        @pl.when(jnp.logical_and(c + 1 == n_ch, feeds_next))
        def _():
            make_in(next_start, 1 - slot).start(priority=1)

        request_weights(c * W_PER_CHUNK, W_PER_CHUNK)

        @pl.when(state[1 + slot] == 1)
        def _():
            make_out(chunk_row(c), slot).wait()

        compute_chunk(slot)
        make_out(chunk_row(c), slot).start(priority=1)
        state[1 + slot] = 1
        return carry

    lax.fori_loop(0, n_ch, body, 0)
    request_weights(n_ch * W_PER_CHUNK, W_PIECES)
    state[0] = g0 + n_ch
    state[3] = jnp.where(jnp.logical_and(n_ch > 0, feeds_next), 1, 0)

    @pl.when(tail > 0)
    def _():
        make_tail_out(tail_row).wait()

    @pl.when(step == n_steps - 1)
    def _():
        for slot in range(2):
            @pl.when(state[1 + slot] == 1)
            def _():
                make_out(0, slot).wait()
                state[1 + slot] = 0


def _rows_at(row0, n):
    return pl.ds(row0 if isinstance(row0, int) else pl.multiple_of(row0, SEG_ALIGN), n)


def _moe_up_kernel(seg_ref, rows_ref, slack_ref, xs_ref, w_hbm, bg_ref, bl_ref, act_ref,
                   xbuf, obuf, xtail, otail, wbuf, wg_bf, wl_bf, state, isem, osem, tsem, wsem):
    f = pl.program_id(0)
    e = pl.program_id(1)
    start = seg_ref[e]
    rows = rows_ref[e]
    e_next = lax.rem(e + 1, N_EXPERTS)
    f_next = jnp.where(e == N_EXPERTS - 1, f + 1, f)
    step = f * N_EXPERTS + e
    n_steps = MOE_NF * N_EXPERTS
    wslot = lax.rem(step, 2)

    def weight_piece(expert, ftile, slot, p):
        per_half = W_PIECES // 2
        band = D_MODEL // per_half
        t = p // per_half if isinstance(p, int) else lax.shift_right_logical(
            p, per_half.bit_length() - 1)
        r0 = (p - t * per_half) * band
        r0 = r0 if isinstance(r0, int) else pl.multiple_of(r0, band)
        col0 = pl.multiple_of((t * MOE_NF + ftile) * MOE_TF, MOE_TF)
        return pltpu.make_async_copy(
            w_hbm.at[expert, pl.ds(r0, band), pl.ds(col0, MOE_TF)],
            wbuf.at[slot, t, pl.ds(r0, band)], wsem.at[slot, p])

    @pl.when(step == 0)
    def _():
        for p in range(W_PIECES):
            weight_piece(e, f, wslot, p).start()

    def make_in(r0, slot):
        return pltpu.make_async_copy(xs_ref.at[_rows_at(r0, MOE_CH)], xbuf.at[slot], isem.at[slot])

    def make_out(r0, slot):
        return pltpu.make_async_copy(obuf.at[slot], act_ref.at[f, _rows_at(r0, MOE_CH)],
                                     osem.at[slot])

    def make_tail_in(r0):
        return pltpu.make_async_copy(xs_ref.at[_rows_at(r0, SEG_ALIGN)], xtail, tsem.at[0])

    def make_tail_out(r0):
        return pltpu.make_async_copy(otail, act_ref.at[f, _rows_at(r0, SEG_ALIGN)], tsem.at[1])

    def cast_weights():
        per_half = W_PIECES // 2
        band = D_MODEL // per_half
        for p in range(W_PIECES):
            weight_piece(e, f, wslot, p).wait()
            t, rows_p = p // per_half, pl.ds((p % per_half) * band, band)
            (wg_bf, wl_bf)[t][rows_p, :] = wbuf[wslot, t, rows_p, :].astype(BF16)

    def expert_mlp(words):
        lo, hi = _unpack_bf16_pairs(words)
        glu = (jnp.dot(lo, wg_bf[:HALF, :], preferred_element_type=F32)
               + jnp.dot(hi, wg_bf[HALF:, :], preferred_element_type=F32) + bg_ref[0])
        lin = (jnp.dot(lo, wl_bf[:HALF, :], preferred_element_type=F32)
               + jnp.dot(hi, wl_bf[HALF:, :], preferred_element_type=F32) + bl_ref[0])
        glu = jnp.minimum(glu, SWIGLU_LIMIT)
        lin = jnp.clip(lin, -SWIGLU_LIMIT, SWIGLU_LIMIT)
        return (glu * _sigmoid(SWIGLU_ALPHA * glu) * (lin + 1.0)).astype(BF16)

    def compute_chunk(slot):
        obuf[slot] = expert_mlp(xbuf[slot])

    def compute_tail():
        otail[...] = expert_mlp(xtail[...])

    _stream_rows(step, n_steps, start, rows,
                 seg_ref[e_next], rows_ref[e_next], state,
                 make_in, make_out, make_tail_in, make_tail_out,
                 compute_chunk, compute_tail, cast_weights,
                 lambda p: weight_piece(e_next, f_next, 1 - wslot, p))

    @pl.when(e == N_EXPERTS - 1)
    def _():
        _zero_slack(slack_ref, otail, lambda rr: act_ref.at[f, rr], tsem.at[1])


def moe_up(seg_start, seg_rows, slack, xs, w1, b1):
    grid_spec = pltpu.PrefetchScalarGridSpec(
        num_scalar_prefetch=3,
        grid=(MOE_NF, N_EXPERTS),
        in_specs=[
            pl.BlockSpec(memory_space=pl.ANY),
            pl.BlockSpec(memory_space=pl.ANY),
            pl.BlockSpec((1, 1, MOE_TF), lambda f, e, s, r, z: (e, 0, f)),
            pl.BlockSpec((1, 1, MOE_TF), lambda f, e, s, r, z: (e, 0, MOE_NF + f)),
        ],
        out_specs=pl.BlockSpec(memory_space=pl.ANY),
        scratch_shapes=[
            pltpu.VMEM((2, MOE_CH, HALF), jnp.uint32),
            pltpu.VMEM((2, MOE_CH, MOE_TF), BF16),
            pltpu.VMEM((SEG_ALIGN, HALF), jnp.uint32),
            pltpu.VMEM((SEG_ALIGN, MOE_TF), BF16),
            pltpu.VMEM((2, 2, D_MODEL, MOE_TF), F32),
            pltpu.VMEM((D_MODEL, MOE_TF), BF16),
            pltpu.VMEM((D_MODEL, MOE_TF), BF16),
            pltpu.SMEM((N_STATE,), jnp.int32),
            pltpu.SemaphoreType.DMA((2,)),
            pltpu.SemaphoreType.DMA((2,)),
            pltpu.SemaphoreType.DMA((2,)),
            pltpu.SemaphoreType.DMA((2, W_PIECES)),
        ],
    )
    b13 = b1.reshape(N_EXPERTS, 1, 2 * D_FF)
    return pl.pallas_call(
        _moe_up_kernel,
        grid_spec=grid_spec,
        out_shape=jax.ShapeDtypeStruct((MOE_NF, MOE_R, MOE_TF), BF16),
        compiler_params=pltpu.CompilerParams(
            dimension_semantics=("arbitrary", "arbitrary"), vmem_limit_bytes=MOE_VMEM_LIMIT),
        name="moe_up",
    )(seg_start, seg_rows, slack, xs, w1, b13, b13)


def _moe_down_kernel(seg_ref, rows_ref, slack_ref, act_ref, w_hbm, b_ref, y_ref,
                     xbuf, obuf, xtail, otail, wbuf, w_bf, state, isem, osem, tsem, wsem):
    e = pl.program_id(0)
    start = seg_ref[e]
    rows = rows_ref[e]
    e_next = lax.rem(e + 1, N_EXPERTS)
    wslot = lax.rem(e, 2)

    def weight_piece(expert, slot, p):
        band = D_FF // W_PIECES
        r0 = p * band if isinstance(p, int) else pl.multiple_of(p * band, band)
        return pltpu.make_async_copy(w_hbm.at[expert, pl.ds(r0, band)],
                                     wbuf.at[slot, pl.ds(r0, band)], wsem.at[slot, p])

    @pl.when(e == 0)
    def _():
        for p in range(W_PIECES):
            weight_piece(e, wslot, p).start()

    def make_in(r0, slot):
        return _CopyGroup([
            pltpu.make_async_copy(act_ref.at[j, _rows_at(r0, MOE_CH)],
                                  xbuf.at[slot, :, pl.ds(j * MOE_TF, MOE_TF)], isem.at[slot])
            for j in range(MOE_NF)])

    def make_out(r0, slot):
        return pltpu.make_async_copy(obuf.at[slot], y_ref.at[_rows_at(r0, MOE_CH)], osem.at[slot])

    def make_tail_in(r0):
        return _CopyGroup([
            pltpu.make_async_copy(act_ref.at[j, _rows_at(r0, SEG_ALIGN)],
                                  xtail.at[:, pl.ds(j * MOE_TF, MOE_TF)], tsem.at[0])
            for j in range(MOE_NF)])

    def make_tail_out(r0):
        return pltpu.make_async_copy(otail, y_ref.at[_rows_at(r0, SEG_ALIGN)], tsem.at[1])

    def cast_weights():
        band = D_FF // W_PIECES
        for p in range(W_PIECES):
            weight_piece(e, wslot, p).wait()
            rows_p = pl.ds(p * band, band)
            w_bf[rows_p, :] = wbuf[wslot, rows_p, :].astype(BF16)

    def expert_out(a):
        return _pack_bf16_pairs(jnp.dot(a, w_bf[...], preferred_element_type=F32) + b_ref[0])

    def compute_chunk(slot):
        obuf[slot] = expert_out(xbuf[slot])

    def compute_tail():
        otail[...] = expert_out(xtail[...])

    _stream_rows(e, N_EXPERTS, start, rows, seg_ref[e_next], rows_ref[e_next], state,
                 make_in, make_out, make_tail_in, make_tail_out,
                 compute_chunk, compute_tail, cast_weights,
                 lambda p: weight_piece(e_next, 1 - wslot, p))

    @pl.when(e == N_EXPERTS - 1)
    def _():
        _zero_slack(slack_ref, otail, lambda rr: y_ref.at[rr], tsem.at[1])


def moe_down(seg_start, seg_rows, slack, act, w2, b2):
    grid_spec = pltpu.PrefetchScalarGridSpec(
        num_scalar_prefetch=3,
        grid=(N_EXPERTS,),
        in_specs=[
            pl.BlockSpec(memory_space=pl.ANY),
            pl.BlockSpec(memory_space=pl.ANY),
            pl.BlockSpec((1, 1, D_MODEL), lambda e, s, r, z: (e, 0, 0)),
        ],
        out_specs=pl.BlockSpec(memory_space=pl.ANY),
        scratch_shapes=[
            pltpu.VMEM((2, MOE_CH, D_FF), BF16),
            pltpu.VMEM((2, MOE_CH, HALF), jnp.uint32),
            pltpu.VMEM((SEG_ALIGN, D_FF), BF16),
            pltpu.VMEM((SEG_ALIGN, HALF), jnp.uint32),
            pltpu.VMEM((2, D_FF, D_MODEL), F32),
            pltpu.VMEM((D_FF, D_MODEL), BF16),
            pltpu.SMEM((N_STATE,), jnp.int32),
            pltpu.SemaphoreType.DMA((2,)),
            pltpu.SemaphoreType.DMA((2,)),
            pltpu.SemaphoreType.DMA((2,)),
            pltpu.SemaphoreType.DMA((2, W_PIECES)),
        ],
    )
    return pl.pallas_call(
        _moe_down_kernel,
        grid_spec=grid_spec,
        out_shape=jax.ShapeDtypeStruct((MOE_R, HALF), jnp.uint32),
        compiler_params=pltpu.CompilerParams(
            dimension_semantics=("arbitrary",), vmem_limit_bytes=MOE_VMEM_LIMIT),
        name="moe_down",
    )(seg_start, seg_rows, slack, act, w2, b2.reshape(N_EXPERTS, 1, D_MODEL))


COMB_TM = 256
COMB_NT = N_TOK // COMB_TM


def _combine_kernel(dest_ref, y_ref, w_ref, h_ref, g_ref, o_ref, buf, sem):
    i = pl.program_id(0)

    def fetch(tile, slot):
        def group(g, carry):
            for r in range(8):
                a = (tile * COMB_TM + g * 8 + r) * TOP_K
                for kk in range(TOP_K):
                    pltpu.make_async_copy(y_ref.at[pl.ds(dest_ref[a + kk], 1)],
                                          buf.at[slot, kk, g, pl.ds(r, 1)], sem.at[slot]).start(
                                              priority=kk % 2)
            return carry

        lax.fori_loop(0, COMB_TM // 8, group, 0)

    @pl.when(i == 0)
    def _():
        fetch(0, 0)

    slot = lax.rem(i, 2)

    @pl.when(i + 1 < COMB_NT)
    def _():
        fetch(i + 1, 1 - slot)

    for kk in range(TOP_K):
        pltpu.make_async_copy(y_ref.at[pl.ds(0, COMB_TM)], y_ref.at[pl.ds(0, COMB_TM)],
                              sem.at[slot]).wait()
    w = w_ref[...]
    lo = h_ref[:, :HALF]
    hi = h_ref[:, HALF:]
    for kk in range(TOP_K):
        words = buf[slot, kk].reshape(COMB_TM, HALF)
        wk = w[:, kk:kk + 1]
        lo = lo + wk * lax.bitcast_convert_type(lax.shift_left(words, jnp.uint32(16)), F32)
        hi = hi + wk * lax.bitcast_convert_type(words & jnp.uint32(0xFFFF0000), F32)
    ms = (jnp.sum(lo * lo, axis=-1, keepdims=True)
          + jnp.sum(hi * hi, axis=-1, keepdims=True)) * (1.0 / D_MODEL)
    inv = lax.rsqrt(ms + EPS)
    o_ref[:, :HALF] = lo * inv * g_ref[:, :HALF]
    o_ref[:, HALF:] = hi * inv * g_ref[:, HALF:]


def combine(dest, y, weight, h2, gain):
    grid_spec = pltpu.PrefetchScalarGridSpec(
        num_scalar_prefetch=1,
        grid=(COMB_NT,),
        in_specs=[
            pl.BlockSpec(memory_space=pl.ANY),
            pl.BlockSpec((COMB_TM, LANES), lambda i, d: (i, 0)),
            pl.BlockSpec((COMB_TM, D_MODEL), lambda i, d: (i, 0)),
            pl.BlockSpec((1, D_MODEL), lambda i, d: (0, 0)),
        ],
        out_specs=pl.BlockSpec((COMB_TM, D_MODEL), lambda i, d: (i, 0)),
        scratch_shapes=[pltpu.VMEM((2, TOP_K, COMB_TM // 8, 8, HALF), jnp.uint32),
                        pltpu.SemaphoreType.DMA((2,))],
    )
    return pl.pallas_call(
        _combine_kernel,
        grid_spec=grid_spec,
        out_shape=jax.ShapeDtypeStruct((N_TOK, D_MODEL), F32),
        compiler_params=_cparams(1),
        name="moe_combine",
    )(dest, y, weight, h2, gain.reshape(1, D_MODEL))


def kernel(x, meta_tokens, rel_bias, norm_mix, w_in, conv_w, gate_bias_m, lambda_params, subln_da,
           w_branch_da, w_branch_m, w_gate, b_gate, w_out, norm_ffn, w_router, b_router,
           w1, b1, w2, b2, norm_final):
    layer = 0
    xn_pad, xn_real = norm_in(x, meta_tokens, norm_mix[layer])
    xn_pad2 = xn_pad.reshape(BATCH * LP, D_MODEL)
    w_in_t = w_in[layer].T
    proj = matmul(xn_pad2, w_in_t, n_cols=PROJ_COLS, tm=768, tn=1024, w_transposed=True,
                  name="proj_in")
    proj3 = proj.reshape(BATCH, LP, PROJ_COLS)
    w_g_t = jnp.pad(w_in_t[COL_M_G:], ((0, LANES - 4 * H_M), (0, 0)))
    mg = matmul(xn_pad2, w_g_t, n_cols=LANES, tm=768, tn=LANES, out_dtype=F32,
                w_transposed=True, name="proj_gates")
    gate = matmul(xn_real.reshape(N_TOK, D_MODEL), w_gate[layer], n_cols=2 * D_MODEL,
                  tm=1024, tn=1024, bias=b_gate[layer], act="sigmoid", name="mix_gate")

    tab, consts = _bias_tables(rel_bias)
    y_da = diff_attention(consts, proj3, tab, lambda_params[layer], subln_da[layer])

    qk_m = conv_qk(proj3, conv_w[layer])
    gp = gate_prep(mg.reshape(BATCH, LP, LANES), gate_bias_m[layer])
    gp4 = gp.reshape(BATCH, LP, 4, H_M)
    grow = jnp.transpose(gp4, (0, 3, 2, 1))
    swap = lambda t: jnp.swapaxes(t, 1, 2)
    y_m = swap(mlstm(swap(qk_m[:, :, :H_M * DK_M]), qk_m,
                     swap(proj3[:, :, COL_M_V:COL_M_V + H_M * DV_M]),
                     swap(proj3[:, :, COL_M_O:COL_M_O + H_M * DV_M]), grow))

    mixed = branch_mix(y_da.reshape(N_TOK, H_DA * DV_DA), y_m.reshape(N_TOK, H_M * DV_M),
                       w_branch_da[layer], w_branch_m[layer], gate)
    h2, hn_packed, top_e, weight, rank, counts = ffn_prep(
        mixed, w_out[layer], x.reshape(N_TOK, D_MODEL),
        norm_ffn[layer], w_router[layer], b_router[layer])
    dest, seg_start, seg_rows, pad0, padn, slack = _plan(
        counts, top_e[:, :TOP_K], rank[:, :TOP_K])
    xs = dispatch(dest, pad0, padn, slack, hn_packed)
    act = moe_up(seg_start, seg_rows, slack, xs, w1[layer], b1[layer])
    y = moe_down(seg_start, seg_rows, slack, act, w2[layer], b2[layer])
    out = combine(dest, y, weight, h2, norm_final)
    return out.reshape(BATCH, SEQ, D_MODEL)
```
